```python
import jax, jax.numpy as jnp
from jax import lax
import numpy as np

D_MODEL = 1024
BATCH = 8
SEQ = 4096
DEPTH = 2

MIX_WIDTH = 512
N_BRANCH = 4
MLA_HEADS = 8
QK_NOPE = 64
QK_ROPE = 32
V_HEAD = 64
Q_LORA = 384
KV_LORA = 256
ROPE_THETA = 10000.0
Q_BLOCK = 128
POOL_WINDOWS = (2, 4, 8, 16)
POOL_GROUP = MIX_WIDTH // 4
SSD_HEADS = 8
SSD_HEADDIM = 64
SSD_GROUPS = 2
SSD_STATE = 64
SSD_CHUNK = 128
CONV_WIDTH = 4
SSD_XBC = SSD_HEADS * SSD_HEADDIM + 2 * SSD_GROUPS * SSD_STATE
LRU_BLOCKS = 8
LRU_BLOCK = MIX_WIDTH // LRU_BLOCKS
LRU_C = 8.0
D_FF = 4 * D_MODEL
PLE_DIM = 256
EPS = 1e-6

SPLIT_SIZES = (Q_LORA, KV_LORA, QK_ROPE,
               MIX_WIDTH,
               MIX_WIDTH, SSD_XBC, SSD_HEADS,
               MIX_WIDTH, MIX_WIDTH,
               N_BRANCH * D_MODEL)
IN_COLS = sum(SPLIT_SIZES)

kernel_name = "hybrid_gated_mla_pool_ssd_rglru_block"


def _split_points():
    pts, acc = [], 0
    for s in SPLIT_SIZES[:-1]:
        acc += s
        pts.append(acc)
    return pts


def rmsnorm(x, g):
    x32 = x.astype(jnp.float32)
    y = x32 * lax.rsqrt(jnp.mean(x32 * x32, axis=-1, keepdims=True) + EPS)
    return (y * g.astype(jnp.float32)).astype(x.dtype)


def causal_dwconv(x, w, b):
    c = x.shape[-1]
    y = lax.conv_general_dilated(x, w[:, None, :].astype(x.dtype), window_strides=(1,),
                                 padding=[(CONV_WIDTH - 1, 0)],
                                 dimension_numbers=('NWC', 'WIO', 'NWC'),
                                 feature_group_count=c)
    return y + b.astype(x.dtype)


def rope_tables(positions):
    inv = 1.0 / (ROPE_THETA ** (jnp.arange(0, QK_ROPE, 2, dtype=jnp.float32) / QK_ROPE))
    ang = positions.astype(jnp.float32)[..., None] * inv
    return jnp.cos(ang), jnp.sin(ang)


def apply_rope(x, cos, sin):
    x32 = x.astype(jnp.float32)
    x1, x2 = jnp.split(x32, 2, axis=-1)
    out = jnp.concatenate([x1 * cos - x2 * sin, x2 * cos + x1 * sin], axis=-1)
    return out.astype(x.dtype)


def mla_mixer(c_q, c_kv, k_r, cos, sin, q_norm, w_uq, kv_norm, w_ukv):
    b, s, _ = c_q.shape
    q = (rmsnorm(c_q, q_norm) @ w_uq).reshape(b, s, MLA_HEADS, QK_NOPE + QK_ROPE)
    q_nope = q[..., :QK_NOPE]
    q_rope = apply_rope(q[..., QK_NOPE:], cos[:, :, None], sin[:, :, None])
    kv = (rmsnorm(c_kv, kv_norm) @ w_ukv).reshape(b, s, MLA_HEADS, QK_NOPE + V_HEAD)
    k_nope, v = kv[..., :QK_NOPE], kv[..., QK_NOPE:]
    k_rope = apply_rope(k_r, cos, sin)
    scale = (QK_NOPE + QK_ROPE) ** -0.5
    outs = []
    for blk in range(s // Q_BLOCK):
        q0, kend = blk * Q_BLOCK, (blk + 1) * Q_BLOCK
        sc = (jnp.einsum('bqhd,bkhd->bhqk', q_nope[:, q0:kend], k_nope[:, :kend])
              + jnp.einsum('bqhd,bkd->bhqk', q_rope[:, q0:kend], k_rope[:, :kend]))
        sc = sc.astype(jnp.float32) * scale
        qi = q0 + jnp.arange(Q_BLOCK)[:, None]
        ki = jnp.arange(kend)[None, :]
        sc = jnp.where(ki <= qi, sc, -jnp.inf)
        pr = jax.nn.softmax(sc, axis=-1).astype(v.dtype)
        outs.append(jnp.einsum('bhqk,bkhd->bqhd', pr, v[:, :kend]))
    o = jnp.concatenate(outs, axis=1)
    return o.reshape(b, s, MLA_HEADS * V_HEAD)


def pool_mixer(u, w_pool, pool_scale):
    b, s, _ = u.shape
    u32 = u.astype(jnp.float32)
    maxw = max(POOL_WINDOWS)
    cs = jnp.pad(jnp.cumsum(u32, axis=1), ((0, 0), (maxw, 0), (0, 0)))
    t = jnp.arange(s)
    groups = []
    for g, w in enumerate(POOL_WINDOWS):
        sl = slice(g * POOL_GROUP, (g + 1) * POOL_GROUP)
        win_sum = cs[:, maxw:, sl] - cs[:, maxw - w:maxw - w + s, sl]
        count = jnp.minimum(t + 1, w).astype(jnp.float32)[None, :, None]
        groups.append(win_sum / count - u32[..., sl])
    d = jnp.stack(groups, axis=2).astype(u.dtype)
    y = jnp.einsum('bsgc,gcd->bsgd', d, w_pool).reshape(b, s, MIX_WIDTH)
    return y * pool_scale


def segsum(a):
    t = a.shape[-1]
    cs = jnp.cumsum(a, axis=-1)
    diff = cs[..., :, None] - cs[..., None, :]
    mask = jnp.tril(jnp.ones((t, t), dtype=bool))
    return jnp.where(mask, diff, -jnp.inf)


def ssd_mixer(z, xbc, dt, conv_w, conv_b, dt_bias, a_log, d_skip, norm_g):
    b, s, _ = z.shape
    nc, lc, g, r, n, hp = s // SSD_CHUNK, SSD_CHUNK, SSD_GROUPS, SSD_HEADS // SSD_GROUPS, SSD_STATE, SSD_HEADDIM
    xbc = jax.nn.silu(causal_dwconv(xbc, conv_w, conv_b)).astype(jnp.float32)
    xs = xbc[..., :MIX_WIDTH]
    bm = xbc[..., MIX_WIDTH:MIX_WIDTH + g * n].reshape(b, nc, lc, g, n)
    cm = xbc[..., MIX_WIDTH + g * n:].reshape(b, nc, lc, g, n)
    dt = jax.nn.softplus(dt.astype(jnp.float32) + dt_bias.astype(jnp.float32))
    a_head = -jnp.exp(a_log.astype(jnp.float32))
    x = xs.reshape(b, nc, lc, g, r, hp)
    xdt = x * dt.reshape(b, nc, lc, g, r)[..., None]
    a = (dt * a_head).reshape(b, nc, lc, g, r).transpose(0, 3, 4, 1, 2)
    a_cs = jnp.cumsum(a, axis=-1)
    lmat = jnp.exp(segsum(a))
    cb = jnp.einsum('bclgn,bcsgn->bgcls', cm, bm)
    y_diag = jnp.einsum('bgrcls,bcsgrp->bclgrp', cb[:, :, None] * lmat, xdt)
    decay_states = jnp.exp(a_cs[..., -1:] - a_cs)
    states = jnp.einsum('bclgn,bgrcl,bclgrp->bcgrpn', bm, decay_states, xdt)
    states = jnp.concatenate([jnp.zeros_like(states[:, :1]), states], axis=1)
    chunk_a = jnp.pad(a_cs[..., -1], ((0, 0), (0, 0), (0, 0), (1, 0)))
    decay_chunk = jnp.exp(segsum(chunk_a))
    states = jnp.einsum('bgrzc,bcgrpn->bzgrpn', decay_chunk, states)[:, :-1]
    y_off = jnp.einsum('bclgn,bcgrpn,bgrcl->bclgrp', cm, states, jnp.exp(a_cs))
    y = (y_diag + y_off).reshape(b, s, SSD_HEADS, hp) \
        + xs.reshape(b, s, SSD_HEADS, hp) * d_skip.astype(jnp.float32)[:, None]
    y = y.reshape(b, s, MIX_WIDTH) * jax.nn.silu(z.astype(jnp.float32))
    return rmsnorm(y, norm_g).astype(z.dtype)


def rglru_mixer(gate_in, x_in, conv_w, conv_b, w_a, b_a, w_i, b_i, lam):
    b, s, _ = x_in.shape
    gate = jax.nn.gelu(gate_in)
    xc = causal_dwconv(x_in, conv_w, conv_b)
    xb = xc.reshape(b, s, LRU_BLOCKS, LRU_BLOCK)
    r_t = jax.nn.sigmoid((jnp.einsum('bshi,hij->bshj', xb, w_a).reshape(b, s, MIX_WIDTH) + b_a).astype(jnp.float32))
    i_t = jax.nn.sigmoid((jnp.einsum('bshi,hij->bshj', xb, w_i).reshape(b, s, MIX_WIDTH) + b_i).astype(jnp.float32))
    log_a = -LRU_C * r_t * jax.nn.softplus(-lam.astype(jnp.float32))
    a_t = jnp.exp(log_a)
    mult = jnp.sqrt(-jnp.expm1(2.0 * log_a))
    u = xc.astype(jnp.float32) * i_t * mult

    def combine(lhs, rhs):
        a1, b1 = lhs
        a2, b2 = rhs
        return a1 * a2, a2 * b1 + b2

    _, h = lax.associative_scan(combine, (a_t, u), axis=1)
    return h.astype(x_in.dtype) * gate


def _fwd_setup_inputs(seed: int = 0) -> dict:
    key = jax.random.key(seed)
    ks = iter(jax.random.split(key, 48))
    f32 = jnp.float32

    def nrm(shape, fan_in):
        return jax.random.normal(next(ks), shape, f32) * (fan_in ** -0.5)

    def gain(shape):
        return 1.0 + 0.05 * jax.random.normal(next(ks), shape, f32)

    def small(shape):
        return 0.01 * jax.random.normal(next(ks), shape, f32)

    L = DEPTH
    x = jax.random.normal(next(ks), (BATCH, SEQ, D_MODEL), f32)
    p = jax.random.normal(next(ks), (DEPTH, BATCH, SEQ, PLE_DIM), f32)
    offs = jax.random.randint(next(ks), (BATCH, 1), 0, 1024, dtype=jnp.int32)
    positions = (offs + jnp.arange(SEQ, dtype=jnp.int32)[None, :]).astype(jnp.int32)
    dt0 = jnp.exp(jax.random.uniform(next(ks), (L, SSD_HEADS), f32, np.log(1e-3), np.log(1e-1)))
    dt_bias = dt0 + jnp.log(-jnp.expm1(-dt0))
    a_log = jnp.log(jax.random.uniform(next(ks), (L, SSD_HEADS), f32, 1.0, 16.0))
    a_pow = jax.random.uniform(next(ks), (L, MIX_WIDTH), f32, 0.9, 0.999) ** (1.0 / LRU_C)
    lam = jnp.log(a_pow) - jnp.log1p(-a_pow)
    return {
        "x": x,
        "p": p,
        "positions": positions,
        "g_mix": gain((L, D_MODEL)),
        "w_in": nrm((L, D_MODEL, IN_COLS), D_MODEL),
        "q_norm": gain((L, Q_LORA)),
        "w_uq": nrm((L, Q_LORA, MLA_HEADS * (QK_NOPE + QK_ROPE)), Q_LORA),
        "kv_norm": gain((L, KV_LORA)),
        "w_ukv": nrm((L, KV_LORA, MLA_HEADS * (QK_NOPE + V_HEAD)), KV_LORA),
        "w_pool": nrm((L, 4, POOL_GROUP, POOL_GROUP), POOL_GROUP),
        "pool_scale": 1.0 + 0.1 * jax.random.normal(next(ks), (L, MIX_WIDTH), f32),
        "ssd_conv_w": nrm((L, CONV_WIDTH, SSD_XBC), CONV_WIDTH),
        "ssd_conv_b": small((L, SSD_XBC)),
        "ssd_dt_bias": dt_bias,
        "ssd_a_log": a_log,
        "ssd_d": gain((L, SSD_HEADS)),
        "ssd_norm": gain((L, MIX_WIDTH)),
        "lru_conv_w": nrm((L, CONV_WIDTH, MIX_WIDTH), CONV_WIDTH),
        "lru_conv_b": small((L, MIX_WIDTH)),
        "lru_w_a": nrm((L, LRU_BLOCKS, LRU_BLOCK, LRU_BLOCK), LRU_BLOCK),
        "lru_b_a": small((L, MIX_WIDTH)),
        "lru_w_i": nrm((L, LRU_BLOCKS, LRU_BLOCK, LRU_BLOCK), LRU_BLOCK),
        "lru_b_i": small((L, MIX_WIDTH)),
        "lru_lambda": lam,
        "w_branch": nrm((L, N_BRANCH, MIX_WIDTH, D_MODEL), MIX_WIDTH),
        "w_out": nrm((L, D_MODEL, D_MODEL), D_MODEL),
        "g_mlp": gain((L, D_MODEL)),
        "w_ff1": nrm((L, D_MODEL, D_FF), D_MODEL),
        "w_ff2": nrm((L, D_FF, D_MODEL), D_FF),
        "g_ple": gain((L, D_MODEL)),
        "w_ple_gate": nrm((L, D_MODEL, D_MODEL), D_MODEL),
        "w_ple": nrm((L, PLE_DIM, D_MODEL), PLE_DIM),
        "g_final": gain((D_MODEL,)),
    }


def _fwd_reference(x, p, positions, g_mix, w_in, q_norm, w_uq, kv_norm, w_ukv, w_pool, pool_scale,
              ssd_conv_w, ssd_conv_b, ssd_dt_bias, ssd_a_log, ssd_d, ssd_norm,
              lru_conv_w, lru_conv_b, lru_w_a, lru_b_a, lru_w_i, lru_b_i, lru_lambda,
              w_branch, w_out, g_mlp, w_ff1, w_ff2, g_ple, w_ple_gate, w_ple, g_final):
    b, s, _ = x.shape
    cos, sin = rope_tables(positions)
    pts = _split_points()
    for l in range(DEPTH):
        h = rmsnorm(x, g_mix[l])
        u = h @ w_in[l]
        c_q, c_kv, k_r, u_pool, z, xbc, dt, lru_g, lru_x, gates = jnp.split(u, pts, axis=-1)
        y_a = mla_mixer(c_q, c_kv, k_r, cos, sin, q_norm[l], w_uq[l], kv_norm[l], w_ukv[l])
        y_b = pool_mixer(u_pool, w_pool[l], pool_scale[l])
        y_c = ssd_mixer(z, xbc, dt, ssd_conv_w[l], ssd_conv_b[l], ssd_dt_bias[l], ssd_a_log[l],
                        ssd_d[l], ssd_norm[l])
        y_d = rglru_mixer(lru_g, lru_x, lru_conv_w[l], lru_conv_b[l], lru_w_a[l], lru_b_a[l],
                          lru_w_i[l], lru_b_i[l], lru_lambda[l])
        gates = jax.nn.sigmoid(gates.reshape(b, s, N_BRANCH, D_MODEL))
        merged = (gates[:, :, 0] * (y_a @ w_branch[l, 0])
                  + gates[:, :, 1] * (y_b @ w_branch[l, 1])
                  + gates[:, :, 2] * (y_c @ w_branch[l, 2])
                  + gates[:, :, 3] * (y_d @ w_branch[l, 3]))
        x = x + merged @ w_out[l]
        h2 = rmsnorm(x, g_mlp[l])
        x = x + jnp.square(jax.nn.relu(h2 @ w_ff1[l])) @ w_ff2[l]
        ple_gate = jax.nn.sigmoid(rmsnorm(x, g_ple[l]) @ w_ple_gate[l])
        x = x + (p[l] @ w_ple[l]) * ple_gate
    return rmsnorm(x, g_final)


import jax as _jax
import jax.numpy as _jnp

TWIN_FORMAT = 'train_step'
FWD_PARAMS = ['x', 'p', 'positions', 'g_mix', 'w_in', 'q_norm', 'w_uq', 'kv_norm', 'w_ukv', 'w_pool', 'pool_scale', 'ssd_conv_w', 'ssd_conv_b', 'ssd_dt_bias', 'ssd_a_log', 'ssd_d', 'ssd_norm', 'lru_conv_w', 'lru_conv_b', 'lru_w_a', 'lru_b_a', 'lru_w_i', 'lru_b_i', 'lru_lambda', 'w_branch', 'w_out', 'g_mlp', 'w_ff1', 'w_ff2', 'g_ple', 'w_ple_gate', 'w_ple', 'g_final']
TWIN_WEIGHTS = ['g_mix', 'w_in', 'q_norm', 'w_uq', 'kv_norm', 'w_ukv', 'w_pool', 'pool_scale', 'ssd_conv_w', 'ssd_conv_b', 'ssd_dt_bias', 'ssd_a_log', 'ssd_d', 'ssd_norm', 'lru_conv_w', 'lru_conv_b', 'lru_w_a', 'lru_b_a', 'lru_w_i', 'lru_b_i', 'lru_lambda', 'w_branch', 'w_out', 'g_mlp', 'w_ff1', 'w_ff2', 'g_ple', 'w_ple_gate', 'w_ple', 'g_final']
TWIN_DIFF_INPUT = 'x'
TWIN_INPUTS = ['x', 'p', 'positions', 'g_mix', 'w_in', 'q_norm', 'w_uq', 'kv_norm', 'w_ukv', 'w_pool', 'pool_scale', 'ssd_conv_w', 'ssd_conv_b', 'ssd_dt_bias', 'ssd_a_log', 'ssd_d', 'ssd_norm', 'lru_conv_w', 'lru_conv_b', 'lru_w_a', 'lru_b_a', 'lru_w_i', 'lru_b_i', 'lru_lambda', 'w_branch', 'w_out', 'g_mlp', 'w_ff1', 'w_ff2', 'g_ple', 'w_ple_gate', 'w_ple', 'g_final', 'loss_target', 'm_g_mix', 'm_w_in', 'm_q_norm', 'm_w_uq', 'm_kv_norm', 'm_w_ukv', 'm_w_pool', 'm_pool_scale', 'm_ssd_conv_w', 'm_ssd_conv_b', 'm_ssd_dt_bias', 'm_ssd_a_log', 'm_ssd_d', 'm_ssd_norm', 'm_lru_conv_w', 'm_lru_conv_b', 'm_lru_w_a', 'm_lru_b_a', 'm_lru_w_i', 'm_lru_b_i', 'm_lru_lambda', 'm_w_branch', 'm_w_out', 'm_g_mlp', 'm_w_ff1', 'm_w_ff2', 'm_g_ple', 'm_w_ple_gate', 'm_w_ple', 'm_g_final', 'v_g_mix', 'v_w_in', 'v_q_norm', 'v_w_uq', 'v_kv_norm', 'v_w_ukv', 'v_w_pool', 'v_pool_scale', 'v_ssd_conv_w', 'v_ssd_conv_b', 'v_ssd_dt_bias', 'v_ssd_a_log', 'v_ssd_d', 'v_ssd_norm', 'v_lru_conv_w', 'v_lru_conv_b', 'v_lru_w_a', 'v_lru_b_a', 'v_lru_w_i', 'v_lru_b_i', 'v_lru_lambda', 'v_w_branch', 'v_w_out', 'v_g_mlp', 'v_w_ff1', 'v_w_ff2', 'v_g_ple', 'v_w_ple_gate', 'v_w_ple', 'v_g_final']
TWIN_OUTPUTS = ['loss', 'grad_x', 'grad_g_mix', 'grad_w_in', 'grad_q_norm', 'grad_w_uq', 'grad_kv_norm', 'grad_w_ukv', 'grad_w_pool', 'grad_pool_scale', 'grad_ssd_conv_w', 'grad_ssd_conv_b', 'grad_ssd_dt_bias', 'grad_ssd_a_log', 'grad_ssd_d', 'grad_ssd_norm', 'grad_lru_conv_w', 'grad_lru_conv_b', 'grad_lru_w_a', 'grad_lru_b_a', 'grad_lru_w_i', 'grad_lru_b_i', 'grad_lru_lambda', 'grad_w_branch', 'grad_w_out', 'grad_g_mlp', 'grad_w_ff1', 'grad_w_ff2', 'grad_g_ple', 'grad_w_ple_gate', 'grad_w_ple', 'grad_g_final', 'delta_g_mix', 'delta_w_in', 'delta_q_norm', 'delta_w_uq', 'delta_kv_norm', 'delta_w_ukv', 'delta_w_pool', 'delta_pool_scale', 'delta_ssd_conv_w', 'delta_ssd_conv_b', 'delta_ssd_dt_bias', 'delta_ssd_a_log', 'delta_ssd_d', 'delta_ssd_norm', 'delta_lru_conv_w', 'delta_lru_conv_b', 'delta_lru_w_a', 'delta_lru_b_a', 'delta_lru_w_i', 'delta_lru_b_i', 'delta_lru_lambda', 'delta_w_branch', 'delta_w_out', 'delta_g_mlp', 'delta_w_ff1', 'delta_w_ff2', 'delta_g_ple', 'delta_w_ple_gate', 'delta_w_ple', 'delta_g_final', 'new_m_g_mix', 'new_m_w_in', 'new_m_q_norm', 'new_m_w_uq', 'new_m_kv_norm', 'new_m_w_ukv', 'new_m_w_pool', 'new_m_pool_scale', 'new_m_ssd_conv_w', 'new_m_ssd_conv_b', 'new_m_ssd_dt_bias', 'new_m_ssd_a_log', 'new_m_ssd_d', 'new_m_ssd_norm', 'new_m_lru_conv_w', 'new_m_lru_conv_b', 'new_m_lru_w_a', 'new_m_lru_b_a', 'new_m_lru_w_i', 'new_m_lru_b_i', 'new_m_lru_lambda', 'new_m_w_branch', 'new_m_w_out', 'new_m_g_mlp', 'new_m_w_ff1', 'new_m_w_ff2', 'new_m_g_ple', 'new_m_w_ple_gate', 'new_m_w_ple', 'new_m_g_final', 'new_v_g_mix', 'new_v_w_in', 'new_v_q_norm', 'new_v_w_uq', 'new_v_kv_norm', 'new_v_w_ukv', 'new_v_w_pool', 'new_v_pool_scale', 'new_v_ssd_conv_w', 'new_v_ssd_conv_b', 'new_v_ssd_dt_bias', 'new_v_ssd_a_log', 'new_v_ssd_d', 'new_v_ssd_norm', 'new_v_lru_conv_w', 'new_v_lru_conv_b', 'new_v_lru_w_a', 'new_v_lru_b_a', 'new_v_lru_w_i', 'new_v_lru_b_i', 'new_v_lru_lambda', 'new_v_w_branch', 'new_v_w_out', 'new_v_g_mlp', 'new_v_w_ff1', 'new_v_w_ff2', 'new_v_g_ple', 'new_v_w_ple_gate', 'new_v_w_ple', 'new_v_g_final']
TWIN_LEAF_KINDS = {'loss': 'loss', 'grad_x': 'grad_x', 'grad_g_mix': 'grad_w', 'grad_w_in': 'grad_w', 'grad_q_norm': 'grad_w', 'grad_w_uq': 'grad_w', 'grad_kv_norm': 'grad_w', 'grad_w_ukv': 'grad_w', 'grad_w_pool': 'grad_w', 'grad_pool_scale': 'grad_w', 'grad_ssd_conv_w': 'grad_w', 'grad_ssd_conv_b': 'grad_w', 'grad_ssd_dt_bias': 'grad_w', 'grad_ssd_a_log': 'grad_w', 'grad_ssd_d': 'grad_w', 'grad_ssd_norm': 'grad_w', 'grad_lru_conv_w': 'grad_w', 'grad_lru_conv_b': 'grad_w', 'grad_lru_w_a': 'grad_w', 'grad_lru_b_a': 'grad_w', 'grad_lru_w_i': 'grad_w', 'grad_lru_b_i': 'grad_w', 'grad_lru_lambda': 'grad_w', 'grad_w_branch': 'grad_w', 'grad_w_out': 'grad_w', 'grad_g_mlp': 'grad_w', 'grad_w_ff1': 'grad_w', 'grad_w_ff2': 'grad_w', 'grad_g_ple': 'grad_w', 'grad_w_ple_gate': 'grad_w', 'grad_w_ple': 'grad_w', 'grad_g_final': 'grad_w', 'delta_g_mix': 'delta_w', 'delta_w_in': 'delta_w', 'delta_q_norm': 'delta_w', 'delta_w_uq': 'delta_w', 'delta_kv_norm': 'delta_w', 'delta_w_ukv': 'delta_w', 'delta_w_pool': 'delta_w', 'delta_pool_scale': 'delta_w', 'delta_ssd_conv_w': 'delta_w', 'delta_ssd_conv_b': 'delta_w', 'delta_ssd_dt_bias': 'delta_w', 'delta_ssd_a_log': 'delta_w', 'delta_ssd_d': 'delta_w', 'delta_ssd_norm': 'delta_w', 'delta_lru_conv_w': 'delta_w', 'delta_lru_conv_b': 'delta_w', 'delta_lru_w_a': 'delta_w', 'delta_lru_b_a': 'delta_w', 'delta_lru_w_i': 'delta_w', 'delta_lru_b_i': 'delta_w', 'delta_lru_lambda': 'delta_w', 'delta_w_branch': 'delta_w', 'delta_w_out': 'delta_w', 'delta_g_mlp': 'delta_w', 'delta_w_ff1': 'delta_w', 'delta_w_ff2': 'delta_w', 'delta_g_ple': 'delta_w', 'delta_w_ple_gate': 'delta_w', 'delta_w_ple': 'delta_w', 'delta_g_final': 'delta_w', 'new_m_g_mix': 'new_m', 'new_m_w_in': 'new_m', 'new_m_q_norm': 'new_m', 'new_m_w_uq': 'new_m', 'new_m_kv_norm': 'new_m', 'new_m_w_ukv': 'new_m', 'new_m_w_pool': 'new_m', 'new_m_pool_scale': 'new_m', 'new_m_ssd_conv_w': 'new_m', 'new_m_ssd_conv_b': 'new_m', 'new_m_ssd_dt_bias': 'new_m', 'new_m_ssd_a_log': 'new_m', 'new_m_ssd_d': 'new_m', 'new_m_ssd_norm': 'new_m', 'new_m_lru_conv_w': 'new_m', 'new_m_lru_conv_b': 'new_m', 'new_m_lru_w_a': 'new_m', 'new_m_lru_b_a': 'new_m', 'new_m_lru_w_i': 'new_m', 'new_m_lru_b_i': 'new_m', 'new_m_lru_lambda': 'new_m', 'new_m_w_branch': 'new_m', 'new_m_w_out': 'new_m', 'new_m_g_mlp': 'new_m', 'new_m_w_ff1': 'new_m', 'new_m_w_ff2': 'new_m', 'new_m_g_ple': 'new_m', 'new_m_w_ple_gate': 'new_m', 'new_m_w_ple': 'new_m', 'new_m_g_final': 'new_m', 'new_v_g_mix': 'new_v', 'new_v_w_in': 'new_v', 'new_v_q_norm': 'new_v', 'new_v_w_uq': 'new_v', 'new_v_kv_norm': 'new_v', 'new_v_w_ukv': 'new_v', 'new_v_w_pool': 'new_v', 'new_v_pool_scale': 'new_v', 'new_v_ssd_conv_w': 'new_v', 'new_v_ssd_conv_b': 'new_v', 'new_v_ssd_dt_bias': 'new_v', 'new_v_ssd_a_log': 'new_v', 'new_v_ssd_d': 'new_v', 'new_v_ssd_norm': 'new_v', 'new_v_lru_conv_w': 'new_v', 'new_v_lru_conv_b': 'new_v', 'new_v_lru_w_a': 'new_v', 'new_v_lru_b_a': 'new_v', 'new_v_lru_w_i': 'new_v', 'new_v_lru_b_i': 'new_v', 'new_v_lru_lambda': 'new_v', 'new_v_w_branch': 'new_v', 'new_v_w_out': 'new_v', 'new_v_g_mlp': 'new_v', 'new_v_w_ff1': 'new_v', 'new_v_w_ff2': 'new_v', 'new_v_g_ple': 'new_v', 'new_v_w_ple_gate': 'new_v', 'new_v_w_ple': 'new_v', 'new_v_g_final': 'new_v'}


def _forward(args):
    return _fwd_reference(*[args[k] for k in FWD_PARAMS])


def _output_shape():
    out = _jax.eval_shape(lambda: _forward(_fwd_setup_inputs(0)))
    return out.shape, out.dtype

N_MICROBATCH = 1
ADAM_LR = 0.001
ADAM_B1 = 0.9
ADAM_B2 = 0.999
ADAM_EPS = 1e-08
ADAM_WD = 0.01
ADAM_STEP = 10
PER_EXAMPLE_BATCH_AXIS = {'x': 0, 'p': 1, 'positions': 0, 'loss_target': 0}
SHARED_INPUTS = []
_WEIGHT_DTYPES = {'g_mix': _jnp.float32, 'w_in': _jnp.float32, 'q_norm': _jnp.float32, 'w_uq': _jnp.float32, 'kv_norm': _jnp.float32, 'w_ukv': _jnp.float32, 'w_pool': _jnp.float32, 'pool_scale': _jnp.float32, 'ssd_conv_w': _jnp.float32, 'ssd_conv_b': _jnp.float32, 'ssd_dt_bias': _jnp.float32, 'ssd_a_log': _jnp.float32, 'ssd_d': _jnp.float32, 'ssd_norm': _jnp.float32, 'lru_conv_w': _jnp.float32, 'lru_conv_b': _jnp.float32, 'lru_w_a': _jnp.float32, 'lru_b_a': _jnp.float32, 'lru_w_i': _jnp.float32, 'lru_b_i': _jnp.float32, 'lru_lambda': _jnp.float32, 'w_branch': _jnp.float32, 'w_out': _jnp.float32, 'g_mlp': _jnp.float32, 'w_ff1': _jnp.float32, 'w_ff2': _jnp.float32, 'g_ple': _jnp.float32, 'w_ple_gate': _jnp.float32, 'w_ple': _jnp.float32, 'g_final': _jnp.float32}
MOMENT_SCALE = {'g_mix': 1.543551e-01, 'w_in': 6.017087e-02, 'q_norm': 2.144066e-02, 'w_uq': 1.481839e-02, 'kv_norm': 5.071526e-02, 'w_ukv': 2.629432e-02, 'w_pool': 8.780861e-02, 'pool_scale': 8.678261e-02, 'ssd_conv_w': 9.435441e-02, 'ssd_conv_b': 1.477405e-01, 'ssd_dt_bias': 1.693697e-01, 'ssd_a_log': 4.299212e-01, 'ssd_d': 6.306496e-01, 'ssd_norm': 1.083345e-01, 'lru_conv_w': 1.359060e-01, 'lru_conv_b': 7.550284e-01, 'lru_w_a': 2.768488e-02, 'lru_b_a': 3.272189e-02, 'lru_w_i': 5.422859e-02, 'lru_b_i': 5.927585e-02, 'lru_lambda': 7.524072e-02, 'w_branch': 6.335687e-02, 'w_out': 1.259063e-01, 'g_mlp': 1.430241e-01, 'w_ff1': 6.891385e-02, 'w_ff2': 1.965091e-01, 'g_ple': 2.153216e-02, 'w_ple_gate': 2.138495e-02, 'w_ple': 4.939558e-02, 'g_final': 3.235808e+01}


def _to_microbatches(a, axis):
    t = _jnp.moveaxis(a, axis, 0)
    t = t.reshape((N_MICROBATCH, t.shape[0] // N_MICROBATCH) + t.shape[1:])
    return _jnp.moveaxis(t, 1, axis + 1)


def setup_inputs(seed: int = 0) -> dict:
    inp = _fwd_setup_inputs(seed)
    key = _jax.random.fold_in(_jax.random.key(seed), 7919)
    shape, _ = _output_shape()
    out = dict(inp)
    out["loss_target"] = _jax.random.normal(_jax.random.fold_in(key, 0), shape, _jnp.float32)
    for i, name in enumerate(TWIN_WEIGHTS):
        w = inp[name].astype(_jnp.float32)
        if MOMENT_SCALE is None:
            s = _jnp.sqrt(_jnp.mean(_jnp.square(w)) + 1e-30)
        else:
            s = MOMENT_SCALE[name]
        km, kv = _jax.random.split(_jax.random.fold_in(key, i + 1))
        out[name] = w
        out["m_" + name] = s * _jax.random.normal(km, w.shape, _jnp.float32)
        out["v_" + name] = (s * s) * _jax.random.uniform(kv, w.shape, _jnp.float32, 0.5, 1.5)
    if N_MICROBATCH > 1:
        for name, axis in PER_EXAMPLE_BATCH_AXIS.items():
            out[name] = _to_microbatches(out[name], axis)
    return {'x': out['x'], 'p': out['p'], 'positions': out['positions'], 'g_mix': out['g_mix'], 'w_in': out['w_in'], 'q_norm': out['q_norm'], 'w_uq': out['w_uq'], 'kv_norm': out['kv_norm'], 'w_ukv': out['w_ukv'], 'w_pool': out['w_pool'], 'pool_scale': out['pool_scale'], 'ssd_conv_w': out['ssd_conv_w'], 'ssd_conv_b': out['ssd_conv_b'], 'ssd_dt_bias': out['ssd_dt_bias'], 'ssd_a_log': out['ssd_a_log'], 'ssd_d': out['ssd_d'], 'ssd_norm': out['ssd_norm'], 'lru_conv_w': out['lru_conv_w'], 'lru_conv_b': out['lru_conv_b'], 'lru_w_a': out['lru_w_a'], 'lru_b_a': out['lru_b_a'], 'lru_w_i': out['lru_w_i'], 'lru_b_i': out['lru_b_i'], 'lru_lambda': out['lru_lambda'], 'w_branch': out['w_branch'], 'w_out': out['w_out'], 'g_mlp': out['g_mlp'], 'w_ff1': out['w_ff1'], 'w_ff2': out['w_ff2'], 'g_ple': out['g_ple'], 'w_ple_gate': out['w_ple_gate'], 'w_ple': out['w_ple'], 'g_final': out['g_final'], 'loss_target': out['loss_target'], 'm_g_mix': out['m_g_mix'], 'm_w_in': out['m_w_in'], 'm_q_norm': out['m_q_norm'], 'm_w_uq': out['m_w_uq'], 'm_kv_norm': out['m_kv_norm'], 'm_w_ukv': out['m_w_ukv'], 'm_w_pool': out['m_w_pool'], 'm_pool_scale': out['m_pool_scale'], 'm_ssd_conv_w': out['m_ssd_conv_w'], 'm_ssd_conv_b': out['m_ssd_conv_b'], 'm_ssd_dt_bias': out['m_ssd_dt_bias'], 'm_ssd_a_log': out['m_ssd_a_log'], 'm_ssd_d': out['m_ssd_d'], 'm_ssd_norm': out['m_ssd_norm'], 'm_lru_conv_w': out['m_lru_conv_w'], 'm_lru_conv_b': out['m_lru_conv_b'], 'm_lru_w_a': out['m_lru_w_a'], 'm_lru_b_a': out['m_lru_b_a'], 'm_lru_w_i': out['m_lru_w_i'], 'm_lru_b_i': out['m_lru_b_i'], 'm_lru_lambda': out['m_lru_lambda'], 'm_w_branch': out['m_w_branch'], 'm_w_out': out['m_w_out'], 'm_g_mlp': out['m_g_mlp'], 'm_w_ff1': out['m_w_ff1'], 'm_w_ff2': out['m_w_ff2'], 'm_g_ple': out['m_g_ple'], 'm_w_ple_gate': out['m_w_ple_gate'], 'm_w_ple': out['m_w_ple'], 'm_g_final': out['m_g_final'], 'v_g_mix': out['v_g_mix'], 'v_w_in': out['v_w_in'], 'v_q_norm': out['v_q_norm'], 'v_w_uq': out['v_w_uq'], 'v_kv_norm': out['v_kv_norm'], 'v_w_ukv': out['v_w_ukv'], 'v_w_pool': out['v_w_pool'], 'v_pool_scale': out['v_pool_scale'], 'v_ssd_conv_w': out['v_ssd_conv_w'], 'v_ssd_conv_b': out['v_ssd_conv_b'], 'v_ssd_dt_bias': out['v_ssd_dt_bias'], 'v_ssd_a_log': out['v_ssd_a_log'], 'v_ssd_d': out['v_ssd_d'], 'v_ssd_norm': out['v_ssd_norm'], 'v_lru_conv_w': out['v_lru_conv_w'], 'v_lru_conv_b': out['v_lru_conv_b'], 'v_lru_w_a': out['v_lru_w_a'], 'v_lru_b_a': out['v_lru_b_a'], 'v_lru_w_i': out['v_lru_w_i'], 'v_lru_b_i': out['v_lru_b_i'], 'v_lru_lambda': out['v_lru_lambda'], 'v_w_branch': out['v_w_branch'], 'v_w_out': out['v_w_out'], 'v_g_mlp': out['v_g_mlp'], 'v_w_ff1': out['v_w_ff1'], 'v_w_ff2': out['v_w_ff2'], 'v_g_ple': out['v_g_ple'], 'v_w_ple_gate': out['v_w_ple_gate'], 'v_w_ple': out['v_w_ple'], 'v_g_final': out['v_g_final']}


def _loss(weights, diff, rest, loss_target):
    with _jax.named_scope("forward"):
        args = {**rest, TWIN_DIFF_INPUT: diff, **{k: w.astype(_WEIGHT_DTYPES[k]) for k, w in weights.items()}}
        y = _forward(args)
    with _jax.named_scope("loss_head"):
        err = _jnp.square(y.astype(_jnp.float32) - loss_target)
        return 0.5 * _jnp.sum(_jnp.mean(err, axis=-1)) if err.ndim else 0.5 * err


def _adamw(w, g, m, v):
    m = ADAM_B1 * m + (1.0 - ADAM_B1) * g
    v = ADAM_B2 * v + (1.0 - ADAM_B2) * _jnp.square(g)
    m_hat = m / (1.0 - ADAM_B1 ** ADAM_STEP)
    v_hat = v / (1.0 - ADAM_B2 ** ADAM_STEP)
    delta = -ADAM_LR * (m_hat / (_jnp.sqrt(v_hat) + ADAM_EPS) + ADAM_WD * w)
    return delta, m, v


def reference(x, p, positions, g_mix, w_in, q_norm, w_uq, kv_norm, w_ukv, w_pool, pool_scale, ssd_conv_w, ssd_conv_b, ssd_dt_bias, ssd_a_log, ssd_d, ssd_norm, lru_conv_w, lru_conv_b, lru_w_a, lru_b_a, lru_w_i, lru_b_i, lru_lambda, w_branch, w_out, g_mlp, w_ff1, w_ff2, g_ple, w_ple_gate, w_ple, g_final, loss_target, m_g_mix, m_w_in, m_q_norm, m_w_uq, m_kv_norm, m_w_ukv, m_w_pool, m_pool_scale, m_ssd_conv_w, m_ssd_conv_b, m_ssd_dt_bias, m_ssd_a_log, m_ssd_d, m_ssd_norm, m_lru_conv_w, m_lru_conv_b, m_lru_w_a, m_lru_b_a, m_lru_w_i, m_lru_b_i, m_lru_lambda, m_w_branch, m_w_out, m_g_mlp, m_w_ff1, m_w_ff2, m_g_ple, m_w_ple_gate, m_w_ple, m_g_final, v_g_mix, v_w_in, v_q_norm, v_w_uq, v_kv_norm, v_w_ukv, v_w_pool, v_pool_scale, v_ssd_conv_w, v_ssd_conv_b, v_ssd_dt_bias, v_ssd_a_log, v_ssd_d, v_ssd_norm, v_lru_conv_w, v_lru_conv_b, v_lru_w_a, v_lru_b_a, v_lru_w_i, v_lru_b_i, v_lru_lambda, v_w_branch, v_w_out, v_g_mlp, v_w_ff1, v_w_ff2, v_g_ple, v_w_ple_gate, v_w_ple, v_g_final):
    given = dict(x=x, p=p, positions=positions, g_mix=g_mix, w_in=w_in, q_norm=q_norm, w_uq=w_uq, kv_norm=kv_norm, w_ukv=w_ukv, w_pool=w_pool, pool_scale=pool_scale, ssd_conv_w=ssd_conv_w, ssd_conv_b=ssd_conv_b, ssd_dt_bias=ssd_dt_bias, ssd_a_log=ssd_a_log, ssd_d=ssd_d, ssd_norm=ssd_norm, lru_conv_w=lru_conv_w, lru_conv_b=lru_conv_b, lru_w_a=lru_w_a, lru_b_a=lru_b_a, lru_w_i=lru_w_i, lru_b_i=lru_b_i, lru_lambda=lru_lambda, w_branch=w_branch, w_out=w_out, g_mlp=g_mlp, w_ff1=w_ff1, w_ff2=w_ff2, g_ple=g_ple, w_ple_gate=w_ple_gate, w_ple=w_ple, g_final=g_final, loss_target=loss_target, m_g_mix=m_g_mix, m_w_in=m_w_in, m_q_norm=m_q_norm, m_w_uq=m_w_uq, m_kv_norm=m_kv_norm, m_w_ukv=m_w_ukv, m_w_pool=m_w_pool, m_pool_scale=m_pool_scale, m_ssd_conv_w=m_ssd_conv_w, m_ssd_conv_b=m_ssd_conv_b, m_ssd_dt_bias=m_ssd_dt_bias, m_ssd_a_log=m_ssd_a_log, m_ssd_d=m_ssd_d, m_ssd_norm=m_ssd_norm, m_lru_conv_w=m_lru_conv_w, m_lru_conv_b=m_lru_conv_b, m_lru_w_a=m_lru_w_a, m_lru_b_a=m_lru_b_a, m_lru_w_i=m_lru_w_i, m_lru_b_i=m_lru_b_i, m_lru_lambda=m_lru_lambda, m_w_branch=m_w_branch, m_w_out=m_w_out, m_g_mlp=m_g_mlp, m_w_ff1=m_w_ff1, m_w_ff2=m_w_ff2, m_g_ple=m_g_ple, m_w_ple_gate=m_w_ple_gate, m_w_ple=m_w_ple, m_g_final=m_g_final, v_g_mix=v_g_mix, v_w_in=v_w_in, v_q_norm=v_q_norm, v_w_uq=v_w_uq, v_kv_norm=v_kv_norm, v_w_ukv=v_w_ukv, v_w_pool=v_w_pool, v_pool_scale=v_pool_scale, v_ssd_conv_w=v_ssd_conv_w, v_ssd_conv_b=v_ssd_conv_b, v_ssd_dt_bias=v_ssd_dt_bias, v_ssd_a_log=v_ssd_a_log, v_ssd_d=v_ssd_d, v_ssd_norm=v_ssd_norm, v_lru_conv_w=v_lru_conv_w, v_lru_conv_b=v_lru_conv_b, v_lru_w_a=v_lru_w_a, v_lru_b_a=v_lru_b_a, v_lru_w_i=v_lru_w_i, v_lru_b_i=v_lru_b_i, v_lru_lambda=v_lru_lambda, v_w_branch=v_w_branch, v_w_out=v_w_out, v_g_mlp=v_g_mlp, v_w_ff1=v_w_ff1, v_w_ff2=v_w_ff2, v_g_ple=v_g_ple, v_w_ple_gate=v_w_ple_gate, v_w_ple=v_w_ple, v_g_final=v_g_final)
    weights = {n: given[n] for n in TWIN_WEIGHTS}
    shared = {n: given[n] for n in SHARED_INPUTS}
    per_example = {n: given[n] for n in ['x', 'p', 'positions']}
    grad_fn = _jax.value_and_grad(_loss, argnums=(0, 1))

    def one_microbatch(ex, loss_target):
        ex = dict(ex)
        diff = ex.pop(TWIN_DIFF_INPUT)
        return grad_fn(weights, diff, {**shared, **ex}, loss_target)

    if N_MICROBATCH == 1:
        loss, (grad_w, grad_x) = one_microbatch(per_example, given["loss_target"])
    else:
        def body(carry, xs):
            loss_sum, grad_sum = carry
            l_k, (gw_k, gx_k) = one_microbatch(xs[0], xs[1])
            with _jax.named_scope("update"):
                return (loss_sum + l_k, _jax.tree.map(_jnp.add, grad_sum, gw_k)), gx_k

        init = (_jnp.zeros((), _jnp.float32), _jax.tree.map(_jnp.zeros_like, weights))
        (loss, grad_w), grad_x = _jax.lax.scan(body, init, (per_example, given["loss_target"]))
    with _jax.named_scope("update"):
        delta_w, new_m, new_v = {}, {}, {}
        for n in TWIN_WEIGHTS:
            delta_w[n], new_m[n], new_v[n] = _adamw(weights[n], grad_w[n], given["m_" + n], given["v_" + n])
    return (loss, grad_x, *[grad_w[n] for n in TWIN_WEIGHTS], *[delta_w[n] for n in TWIN_WEIGHTS],
            *[new_m[n] for n in TWIN_WEIGHTS], *[new_v[n] for n in TWIN_WEIGHTS])
```

```python
import functools
import math

import jax
import jax.numpy as jnp
import numpy as np
from jax import lax
from jax.experimental import pallas as pl
from jax.experimental.pallas import tpu as pltpu

F32 = jnp.float32
BF16 = jnp.bfloat16
MXU_DTYPE = BF16
LANES = 128
VMEM_LIMIT = 56 * 1024 * 1024

D_MODEL = 1024
N_HEADS = 8
HEAD = 64
QK_ROPE = 32
Q_LORA = 384
KV_LORA = 256
MIX = 512
SSD_CHUNK = 128
CONV_W = 4
POOL_WINDOWS = (2, 4, 8, 16)
LRU_C = 8.0
EPS = 1e-6
ROPE_THETA = 10000.0
ATT_SCALE = (HEAD + QK_ROPE) ** -0.5
SPLIT_SIZES = (Q_LORA, KV_LORA, QK_ROPE, MIX, MIX, 768, N_HEADS, MIX, MIX, 4 * D_MODEL)

ADAM_LR, ADAM_B1, ADAM_B2, ADAM_EPS, ADAM_WD, ADAM_STEP = 0.001, 0.9, 0.999, 1e-08, 0.01, 10

BIG = (("w_in", 2), ("w_uq", 2), ("w_ukv", 2), ("ssd_conv_w", 2), ("lru_conv_w", 2), ("w_branch", 3),
       ("w_out", 1), ("w_ff1", 2), ("w_ff2", 1), ("w_ple_gate", 1), ("w_ple", 2))
SMALL = ("g_mix", "q_norm", "kv_norm", "w_pool", "pool_scale", "ssd_conv_b", "ssd_dt_bias", "ssd_a_log",
         "ssd_d", "ssd_norm", "lru_conv_b", "lru_w_a", "lru_b_a", "lru_w_i", "lru_b_i", "lru_lambda",
         "g_mlp", "g_ple", "g_final")
WEIGHTS = ("g_mix", "w_in", "q_norm", "w_uq", "kv_norm", "w_ukv", "w_pool", "pool_scale", "ssd_conv_w",
           "ssd_conv_b", "ssd_dt_bias", "ssd_a_log", "ssd_d", "ssd_norm", "lru_conv_w", "lru_conv_b", "lru_w_a",
           "lru_b_a", "lru_w_i", "lru_b_i", "lru_lambda", "w_branch", "w_out", "g_mlp", "w_ff1", "w_ff2", "g_ple",
           "w_ple_gate", "w_ple", "g_final")
CONV_SHARDED = ("ssd_conv_w", "lru_conv_w")
PACK_W = 1024
PACK_ROWS = 64


def _cparams(sem, vmem=VMEM_LIMIT):
    return pltpu.CompilerParams(dimension_semantics=sem, vmem_limit_bytes=vmem)


def _pick(n, cands):
    for c in cands:
        if n % c == 0:
            return c
    return n


def _sigmoid(x):
    return 1.0 / (1.0 + jnp.exp(-x))


def _silu(x):
    return x * _sigmoid(x)


def _silu_grad(x):
    s = _sigmoid(x)
    return s * (1.0 + x * (1.0 - s))


def _softplus(x):
    e = jnp.exp(-jnp.abs(x))
    log1p_e = jnp.where(e < 1e-3, e * (1.0 - e * (0.5 - e * (1.0 / 3.0))), jnp.log(1.0 + e))
    return jnp.maximum(x, 0.0) + log1p_e


_GELU_C = math.sqrt(2.0 / math.pi)


def _gelu(x):
    t = jnp.tanh(_GELU_C * (x + 0.044715 * x * x * x))
    return 0.5 * x * (1.0 + t)


def _gelu_grad(x):
    t = jnp.tanh(_GELU_C * (x + 0.044715 * x * x * x))
    return 0.5 * (1.0 + t) + 0.5 * x * (1.0 - t * t) * _GELU_C * (1.0 + 3.0 * 0.044715 * x * x)


def _neg_expm1(x):
    series = -x * (1.0 + 0.5 * x * (1.0 + (1.0 / 3.0) * x * (1.0 + 0.25 * x)))
    return jnp.where(x > -0.05, series, 1.0 - jnp.exp(x))


def _shift_down(x, k, row):
    return jnp.where(row >= k, pltpu.roll(x, k, 0), 0.0)


def _shift_up(x, k, row):
    n = x.shape[0]
    return jnp.where(row < n - k, pltpu.roll(x, n - k, 0), 0.0)


def _cumsum_rows(x, row):
    d = 1
    while d < x.shape[0]:
        x = x + _shift_down(x, d, row)
        d *= 2
    return x


def _rev_cumsum_rows(x, row):
    d = 1
    while d < x.shape[0]:
        x = x + _shift_up(x, d, row)
        d *= 2
    return x


def _cumsum_lanes(x, col):
    d = 1
    while d < x.shape[1]:
        x = x + jnp.where(col >= d, pltpu.roll(x, d, 1), 0.0)
        d *= 2
    return x


def _dot(a, b, ta=False, tb=False):
    dn = (((0 if ta else 1,), (1 if tb else 0,)), ((), ()))
    return lax.dot_general(a.astype(MXU_DTYPE), b.astype(MXU_DTYPE), dn, preferred_element_type=F32)


def _mm(a, b, *, ta=False, tb=False, epilogue=None, tiles=(), rowvecs=(), out_dtypes=(F32,), name):
    m, k = (a.shape[1], a.shape[0]) if ta else a.shape
    n = b.shape[0] if tb else b.shape[1]
    assert (b.shape[1] if tb else b.shape[0]) == k, (a.shape, b.shape, ta, tb)
    tm = _pick(m, (512, 384, 256, 128))
    tn = _pick(n, (512, 384, 256, 128))
    tk = _pick(k, (512, 384, 256, 128))
    nk = k // tk
    nt, nr, no = len(tiles), len(rowvecs), len(out_dtypes)

    def body(*refs):
        a_ref, b_ref = refs[0], refs[1]
        tile_refs = refs[2:2 + nt]
        row_refs = refs[2 + nt:2 + nt + nr]
        out_refs = refs[2 + nt + nr:2 + nt + nr + no]
        acc_ref = refs[-1]
        kk = pl.program_id(2)

        @pl.when(kk == 0)
        def _():
            acc_ref[...] = jnp.zeros_like(acc_ref)

        acc_ref[...] += _dot(a_ref[...], b_ref[...], ta, tb)

        @pl.when(kk == nk - 1)
        def _():
            acc = acc_ref[...]
            if epilogue is None:
                outs = (acc,)
            else:
                outs = epilogue(acc, *[t[...] for t in tile_refs], *[r[...] for r in row_refs])
            for o_ref, o in zip(out_refs, outs):
                o_ref[...] = o.astype(o_ref.dtype)

    a_spec = pl.BlockSpec((tk, tm), lambda i, j, kk: (kk, i)) if ta else pl.BlockSpec((tm, tk), lambda i, j, kk: (i, kk))
    b_spec = pl.BlockSpec((tn, tk), lambda i, j, kk: (j, kk)) if tb else pl.BlockSpec((tk, tn), lambda i, j, kk: (kk, j))
    mn_spec = pl.BlockSpec((tm, tn), lambda i, j, kk: (i, j))
    row_spec = pl.BlockSpec((1, tn), lambda i, j, kk: (0, j))
    outs = pl.pallas_call(
        body, name=name,
        grid=(m // tm, n // tn, nk),
        in_specs=[a_spec, b_spec] + [mn_spec] * nt + [row_spec] * nr,
        out_specs=[mn_spec] * no,
        out_shape=[jax.ShapeDtypeStruct((m, n), dt) for dt in out_dtypes],
        scratch_shapes=[pltpu.VMEM((tm, tn), F32)],
        compiler_params=_cparams(("parallel", "parallel", "arbitrary")),
    )(a, b, *tiles, *rowvecs)
    return outs[0] if no == 1 else tuple(outs)


def _rowwise(fn, rows, fulls, outs, *, name, tm=None):
    r = rows[0].shape[0]
    if tm is None:
        widest = max([x.shape[1] for x in rows] + [o[0] for o in outs])
        tm = _pick(r, (max(8, min(512, (512 * 1024) // widest)), 256, 128, 64, 32, 16, 8))
    nrow, nfull, nout = len(rows), len(fulls), len(outs)

    def body(*refs):
        row_refs = refs[:nrow]
        full_refs = refs[nrow:nrow + nfull]
        out_refs = refs[nrow + nfull:]
        res = fn(*[x[...] for x in row_refs], *[x[...] for x in full_refs])
        if not isinstance(res, (tuple, list)):
            res = (res,)
        step = pl.program_id(0)
        for o_ref, o, spec in zip(out_refs, res, outs):
            if spec[2] == "row":
                o_ref[...] = o.astype(o_ref.dtype)
            else:
                @pl.when(step == 0)
                def _(o_ref=o_ref):
                    o_ref[...] = jnp.zeros_like(o_ref)
                o_ref[...] += o

    in_specs = [pl.BlockSpec((tm, x.shape[1]), lambda i: (i, 0)) for x in rows]
    in_specs += [pl.BlockSpec(x.shape, lambda i, nd=x.ndim: (0,) * nd) for x in fulls]
    out_specs, out_shape = [], []
    for c, dt, kind in outs:
        if kind == "row":
            out_specs.append(pl.BlockSpec((tm, c), lambda i: (i, 0)))
            out_shape.append(jax.ShapeDtypeStruct((r, c), dt))
        else:
            out_specs.append(pl.BlockSpec((1, c), lambda i: (0, 0)))
            out_shape.append(jax.ShapeDtypeStruct((1, c), F32))
    res = pl.pallas_call(
        body, name=name, grid=(r // tm,), in_specs=in_specs, out_specs=out_specs, out_shape=out_shape,
        compiler_params=_cparams(("arbitrary",)),
    )(*rows, *fulls)
    return res[0] if nout == 1 else tuple(res)


def _colsum(x):
    return jnp.sum(x, axis=0, keepdims=True)


def _rms_parts(x, n_real):
    r = lax.rsqrt(jnp.sum(x * x, axis=-1, keepdims=True) * (1.0 / n_real) + EPS)
    return x * r, r


def _rms_fwd(x, g, *, n_real=None, out_dtype=BF16, name):
    n_real = n_real or x.shape[1]

    def fn(xv, gv):
        xh, _ = _rms_parts(xv, n_real)
        return xh * gv

    return _rowwise(fn, [x], [g], [(x.shape[1], out_dtype, "row")], name=name)


def _rms_bwd_math(xv, gv, dh, n_real):
    xh, r = _rms_parts(xv, n_real)
    dxh = dh * gv
    dx = r * (dxh - xh * (jnp.sum(dxh * xh, axis=-1, keepdims=True) * (1.0 / n_real)))
    return dx, _colsum(dh * xh)


def _rms_bwd(x, g, dh, res=None, *, name):
    n = x.shape[1]
    if res is None:
        def fn(xv, dhv, gv):
            return _rms_bwd_math(xv, gv, dhv.astype(F32), n)
        rows = [x, dh]
    else:
        def fn(xv, dhv, rv, gv):
            dx, dg = _rms_bwd_math(xv, gv, dhv.astype(F32), n)
            return dx + rv, dg
        rows = [x, dh, res]
    return _rowwise(fn, rows, [g], [(n, F32, "row"), (n, F32, "acc")], name=name)


def _seq_call(body, ins, outs, n_blocks, *, name):
    in_specs, args = [], []
    for x, kind in ins:
        in_specs.append(pl.BlockSpec((x.shape[0], LANES), lambda j: (0, j)))
        args.append(x)
    out_specs, out_shape = [], []
    for shape, dt in outs:
        out_specs.append(pl.BlockSpec((shape[0], LANES), lambda j: (0, j)))
        out_shape.append(jax.ShapeDtypeStruct(shape, dt))
    res = pl.pallas_call(body, name=name, grid=(n_blocks,), in_specs=in_specs, out_specs=out_specs,
                         out_shape=out_shape, compiler_params=_cparams(("parallel",)))(*args)
    return res[0] if len(outs) == 1 else tuple(res)


def _conv_pre(x, w, b, row):
    acc = x * w[CONV_W - 1:CONV_W, :] + b
    for k in range(CONV_W - 1):
        acc = acc + _shift_down(x, CONV_W - 1 - k, row) * w[k:k + 1, :]
    return acc


def _conv_fwd(x, w, b, *, silu, name):
    s, c = x.shape

    def body(x_ref, w_ref, b_ref, y_ref):
        xv = x_ref[...]
        row = lax.broadcasted_iota(jnp.int32, xv.shape, 0)
        pre = _conv_pre(xv, w_ref[...], b_ref[...], row)
        y_ref[...] = _silu(pre) if silu else pre

    return _seq_call(body, [(x, "seq"), (w, "par"), (b, "par")], [((s, c), F32)], c // LANES, name=name)


def _conv_bwd(x, w, b, dy, *, silu, name):
    s, c = x.shape

    def body(x_ref, w_ref, b_ref, dy_ref, dx_ref, dw_ref, db_ref):
        xv, wv, dv = x_ref[...], w_ref[...], dy_ref[...]
        row = lax.broadcasted_iota(jnp.int32, xv.shape, 0)
        if silu:
            dv = dv * _silu_grad(_conv_pre(xv, wv, b_ref[...], row))
        dx = dv * wv[CONV_W - 1:CONV_W, :]
        dws = [None] * CONV_W
        dws[CONV_W - 1] = _colsum(dv * xv)
        for k in range(CONV_W - 1):
            sh = CONV_W - 1 - k
            dx = dx + _shift_up(dv, sh, row) * wv[k:k + 1, :]
            dws[k] = _colsum(dv * _shift_down(xv, sh, row))
        dx_ref[...] = dx
        for k in range(CONV_W):
            dw_ref[k:k + 1, :] = dws[k]
        db_ref[...] = _colsum(dv)

    return _seq_call(body, [(x, "seq"), (w, "par"), (b, "par"), (dy, "seq")],
                     [((s, c), F32), ((CONV_W, c), F32), ((1, c), F32)], c // LANES, name=name)


def _pool_select(levels):
    g = pl.program_id(0)
    return jnp.where(g == 0, levels[0], jnp.where(g == 1, levels[1], jnp.where(g == 2, levels[2], levels[3])))


def _pool_count(row):
    g = pl.program_id(0)
    w = jnp.where(g == 0, POOL_WINDOWS[0], jnp.where(g == 1, POOL_WINDOWS[1],
                                                     jnp.where(g == 2, POOL_WINDOWS[2], POOL_WINDOWS[3])))
    return jnp.minimum(row + 1, w).astype(F32)


def _pool_fwd(u, *, name):
    def body(u_ref, d_ref):
        uv = u_ref[...]
        row = lax.broadcasted_iota(jnp.int32, uv.shape, 0)
        levels, cur, sh = [], uv, 1
        for _ in POOL_WINDOWS:
            cur = cur + _shift_down(cur, sh, row)
            levels.append(cur)
            sh *= 2
        d_ref[...] = _pool_select(levels) / _pool_count(row) - uv

    return _seq_call(body, [(u, "seq")], [(u.shape, F32)], u.shape[1] // LANES, name=name)


def _pool_bwd(dd, *, name):
    def body(dd_ref, du_ref):
        dv = dd_ref[...]
        row = lax.broadcasted_iota(jnp.int32, dv.shape, 0)
        levels, cur, sh = [], dv / _pool_count(row), 1
        for _ in POOL_WINDOWS:
            cur = cur + _shift_up(cur, sh, row)
            levels.append(cur)
            sh *= 2
        du_ref[...] = _pool_select(levels) - dv

    return _seq_call(body, [(dd, "seq")], [(dd.shape, F32)], dd.shape[1] // LANES, name=name)


def _lru_gates(pre_a, pre_i, xc, lam, b_a, b_i):
    r = _sigmoid(pre_a + b_a)
    i = _sigmoid(pre_i + b_i)
    sp = _softplus(-lam)
    log_a = -LRU_C * r * sp
    a = jnp.exp(log_a)
    mult = jnp.sqrt(_neg_expm1(2.0 * log_a))
    return r, i, sp, a, mult


def _lru_fwd(pre, xc, gate_in, lam, b_a, b_i, *, name):
    s, c = xc.shape
    nb = c // LANES

    def body(pa_ref, pi_ref, xc_ref, g_ref, lam_ref, ba_ref, bi_ref, y_ref, h_ref):
        xv = xc_ref[...]
        row = lax.broadcasted_iota(jnp.int32, xv.shape, 0)
        _, i, _, a, mult = _lru_gates(pa_ref[...], pi_ref[...], xv, lam_ref[...], ba_ref[...], bi_ref[...])
        h = xv * i * mult
        d = 1
        while d < s:
            h = h + a * _shift_down(h, d, row)
            a = a * jnp.where(row >= d, pltpu.roll(a, d, 0), 1.0)
            d *= 2
        h_ref[...] = h
        y_ref[...] = h * _gelu(g_ref[...])

    blk = lambda off: pl.BlockSpec((s, LANES), lambda j: (0, j + off))
    par = pl.BlockSpec((1, LANES), lambda j: (0, j))
    return pl.pallas_call(
        body, name=name, grid=(nb,),
        in_specs=[blk(0), blk(nb), blk(0), blk(0), par, par, par],
        out_specs=[blk(0), blk(0)],
        out_shape=[jax.ShapeDtypeStruct((s, c), F32)] * 2,
        compiler_params=_cparams(("parallel",)),
    )(pre, pre, xc, gate_in, lam, b_a, b_i)


def _lru_bwd(pre, xc, gate_in, lam, b_a, b_i, h, dy, *, name):
    s, c = xc.shape
    nb = c // LANES

    def body(pa_ref, pi_ref, xc_ref, g_ref, lam_ref, ba_ref, bi_ref, h_ref, dy_ref,
             dpa_ref, dpi_ref, dxc_ref, dg_ref, dlam_ref, dba_ref, dbi_ref):
        xv, gv, hv, dv = xc_ref[...], g_ref[...], h_ref[...], dy_ref[...]
        row = lax.broadcasted_iota(jnp.int32, xv.shape, 0)
        r, i, sp, a, mult = _lru_gates(pa_ref[...], pi_ref[...], xv, lam_ref[...], ba_ref[...], bi_ref[...])
        dg_ref[...] = dv * hv * _gelu_grad(gv)
        dh = dv * _gelu(gv)
        an = jnp.where(row < s - 1, pltpu.roll(a, s - 1, 0), 0.0)
        d = 1
        while d < s:
            dh = dh + an * _shift_up(dh, d, row)
            an = an * jnp.where(row < s - d, pltpu.roll(an, s - d, 0), 1.0)
            d *= 2
        da = dh * _shift_down(hv, 1, row)
        dxc_ref[...] = dh * i * mult
        di = dh * xv * mult
        dmult = dh * xv * i
        dlog_a = (da - dmult * a / mult) * a
        dr = dlog_a * (-LRU_C) * sp
        dlam_ref[...] = _colsum(dlog_a * LRU_C * r * _sigmoid(-lam_ref[...]))
        dpa = dr * r * (1.0 - r)
        dpi = di * i * (1.0 - i)
        dpa_ref[...] = dpa
        dpi_ref[...] = dpi
        dba_ref[...] = _colsum(dpa)
        dbi_ref[...] = _colsum(dpi)

    blk = lambda off: pl.BlockSpec((s, LANES), lambda j: (0, j + off))
    par = pl.BlockSpec((1, LANES), lambda j: (0, j))
    sc = jax.ShapeDtypeStruct((s, c), F32)
    pc = jax.ShapeDtypeStruct((1, c), F32)
    dpa, dpi, dxc, dg, dlam, dba, dbi = pl.pallas_call(
        body, name=name, grid=(nb,),
        in_specs=[blk(0), blk(nb), blk(0), blk(0), par, par, par, blk(0), blk(0)],
        out_specs=[blk(0), blk(0), blk(0), blk(0), par, par, par],
        out_shape=[sc, sc, sc, sc, pc, pc, pc],
        compiler_params=_cparams(("parallel",)),
    )(pre, pre, xc, gate_in, lam, b_a, b_i, h, dy)
    return dpa, dpi, dxc, dg, dlam, dba, dbi


def _ssd_chunk_terms(dtcol, dtrow, bias, a_log):
    shp = (SSD_CHUNK, SSD_CHUNK)
    row = lax.broadcasted_iota(jnp.int32, shp, 0)
    col = lax.broadcasted_iota(jnp.int32, shp, 1)
    a_head = -jnp.exp(a_log)
    dt_c = jnp.broadcast_to(_softplus(dtcol + bias), shp)
    dt_r = jnp.broadcast_to(_softplus(dtrow + bias), shp)
    cs_c = _cumsum_rows(dt_c * a_head, row)
    cs_r = _cumsum_lanes(dt_r * a_head, col)
    cs_last = jnp.sum(jnp.where(row == SSD_CHUNK - 1, cs_c, 0.0), axis=0, keepdims=True)
    return row, col, a_head, dt_c, cs_c, cs_r, cs_last


def _ssd_fwd(xbc, dtcol, dtrow, bias, a_log, dskip, *, name):
    s = xbc.shape[0]
    nc = s // SSD_CHUNK

    def body(x_ref, b_ref, c_ref, dtc_ref, dtr_ref, bias_ref, alog_ref, d_ref, y_ref, st_ref, state):
        ci = pl.program_id(1)

        @pl.when(ci == 0)
        def _():
            state[...] = jnp.zeros_like(state)

        xv, bm, cm = x_ref[...], b_ref[...], c_ref[...]
        row, col, _, dt_c, cs_c, cs_r, cs_last = _ssd_chunk_terms(
            dtc_ref[0], dtr_ref[0], bias_ref[0], alog_ref[0])
        lmat = jnp.exp(jnp.where(col <= row, cs_c - cs_r, -jnp.inf))
        g = _dot(cm, bm, tb=True) * lmat
        xdt = xv * dt_c
        st = state[...]
        st_ref[0, 0] = st
        y_ref[...] = _dot(g, xdt) + _dot(cm, st) * jnp.exp(cs_c) + xv * d_ref[0]
        w = xdt * jnp.exp(cs_last - cs_c)
        state[...] = jnp.exp(cs_last) * st + _dot(bm.T, w)

    hc = lambda h, ci: (ci, h)
    scal = pl.BlockSpec((1, 1, 1), lambda h, ci: (h, 0, 0))
    return pl.pallas_call(
        body, name=name, grid=(N_HEADS, nc),
        in_specs=[pl.BlockSpec((SSD_CHUNK, LANES), hc),
                  pl.BlockSpec((SSD_CHUNK, LANES), lambda h, ci: (ci, N_HEADS + h // 4)),
                  pl.BlockSpec((SSD_CHUNK, LANES), lambda h, ci: (ci, N_HEADS + 2 + h // 4)),
                  pl.BlockSpec((1, SSD_CHUNK, 1), lambda h, ci: (h, ci, 0)),
                  pl.BlockSpec((1, 1, SSD_CHUNK), lambda h, ci: (h, 0, ci)),
                  scal, scal, scal],
        out_specs=[pl.BlockSpec((SSD_CHUNK, LANES), hc),
                   pl.BlockSpec((1, 1, LANES, LANES), lambda h, ci: (h, ci, 0, 0))],
        out_shape=[jax.ShapeDtypeStruct((s, N_HEADS * LANES), F32),
                   jax.ShapeDtypeStruct((N_HEADS, nc, LANES, LANES), F32)],
        scratch_shapes=[pltpu.VMEM((LANES, LANES), F32)],
        compiler_params=_cparams(("parallel", "arbitrary")),
    )(xbc, xbc, xbc, dtcol, dtrow, bias, a_log, dskip)


def _ssd_bwd(xbc, dtcol, dtrow, bias, a_log, dskip, states, dy, *, name):
    s = xbc.shape[0]
    nc = s // SSD_CHUNK

    def body(x_ref, b_ref, c_ref, dtc_ref, dtr_ref, bias_ref, alog_ref, d_ref, st_ref, dy_ref,
             dx_ref, db_ref, dc_ref, ddt_ref, dbias_ref, dalog_ref, dd_ref, dstate):
        ci = pl.program_id(1)

        @pl.when(ci == 0)
        def _():
            dstate[...] = jnp.zeros_like(dstate)
            dbias_ref[...] = jnp.zeros_like(dbias_ref)
            dalog_ref[...] = jnp.zeros_like(dalog_ref)
            dd_ref[...] = jnp.zeros_like(dd_ref)

        xv, bm, cm, dyv, st = x_ref[...], b_ref[...], c_ref[...], dy_ref[...], st_ref[0, 0]
        dtraw_c = dtc_ref[0]
        bias = bias_ref[0]
        row, col, a_head, dt_c, cs_c, cs_r, cs_last = _ssd_chunk_terms(dtraw_c, dtr_ref[0], bias, alog_ref[0])
        lmat = jnp.exp(jnp.where(col <= row, cs_c - cs_r, -jnp.inf))
        lmat_t = jnp.exp(jnp.where(row <= col, cs_r - cs_c, -jnp.inf))
        g = _dot(cm, bm, tb=True) * lmat
        g_t = _dot(bm, cm, tb=True) * lmat_t
        xdt = xv * dt_c
        e_c = jnp.exp(cs_c)
        f_c = jnp.exp(cs_last - cs_c)
        e_last = jnp.exp(cs_last)
        w = xdt * f_c
        dst = dstate[...]

        dg = _dot(dyv, xdt, tb=True)
        dg_t = _dot(xdt, dyv, tb=True)
        dxdt = _dot(g_t, dyv)
        rowsum = lambda v: jnp.sum(v, axis=1, keepdims=True)
        dcs = rowsum(dg * g) - rowsum(dg_t * g_t)
        dcm = _dot(dg * lmat, bm)
        dbm = _dot(dg_t * lmat_t, cm)
        z = _dot(cm, st)
        dz = dyv * e_c
        dcs = dcs + rowsum(dz * z)
        dcm = dcm + _dot(dz, st, tb=True)
        dst_in = _dot(cm.T, dz) + e_last * dst
        dcs_last = jnp.sum(jnp.sum(dst * st, axis=1, keepdims=True), axis=0, keepdims=True) * jnp.max(e_last, axis=1, keepdims=True)
        dbm = dbm + _dot(w, dst, tb=True)
        dw = _dot(bm, dst)
        dxdt = dxdt + dw * f_c
        q = rowsum(dw * w)
        dcs = dcs - q
        dcs_last = dcs_last + jnp.sum(q, axis=0, keepdims=True)
        dx_ref[...] = dxdt * dt_c + dyv * d_ref[0]
        ddt = rowsum(dxdt * xv)
        dcs_full = jnp.broadcast_to(dcs, (SSD_CHUNK, SSD_CHUNK)) + jnp.where(row == SSD_CHUNK - 1, dcs_last, 0.0)
        da = jnp.max(_rev_cumsum_rows(dcs_full, row), axis=1, keepdims=True)
        dt_col = jnp.max(dt_c, axis=1, keepdims=True)
        ddt = ddt + da * a_head
        draw = ddt * _sigmoid(dtraw_c + bias)
        ddt_ref[0] = draw
        db_ref[0] = dbm
        dc_ref[0] = dcm
        dstate[...] = dst_in
        tot = lambda v: jnp.broadcast_to(jnp.sum(v, axis=0, keepdims=True), (1, LANES))
        dbias_ref[0] += tot(draw)
        dalog_ref[0] += tot(da * dt_col) * a_head
        dd_ref[0] += tot(rowsum(dyv * xv))

    rev = lambda ci: nc - 1 - ci
    hc = lambda h, ci: (rev(ci), h)
    scal = pl.BlockSpec((1, 1, 1), lambda h, ci: (h, 0, 0))
    pacc = pl.BlockSpec((1, 1, LANES), lambda h, ci: (h, 0, 0))
    per_head = pl.BlockSpec((1, SSD_CHUNK, LANES), lambda h, ci: (h, rev(ci), 0))
    return pl.pallas_call(
        body, name=name, grid=(N_HEADS, nc),
        in_specs=[pl.BlockSpec((SSD_CHUNK, LANES), hc),
                  pl.BlockSpec((SSD_CHUNK, LANES), lambda h, ci: (rev(ci), N_HEADS + h // 4)),
                  pl.BlockSpec((SSD_CHUNK, LANES), lambda h, ci: (rev(ci), N_HEADS + 2 + h // 4)),
                  pl.BlockSpec((1, SSD_CHUNK, 1), lambda h, ci: (h, rev(ci), 0)),
                  pl.BlockSpec((1, 1, SSD_CHUNK), lambda h, ci: (h, 0, rev(ci))),
                  scal, scal, scal,
                  pl.BlockSpec((1, 1, LANES, LANES), lambda h, ci: (h, rev(ci), 0, 0)),
                  pl.BlockSpec((SSD_CHUNK, LANES), hc)],
        out_specs=[pl.BlockSpec((SSD_CHUNK, LANES), hc), per_head, per_head,
                   pl.BlockSpec((1, SSD_CHUNK, 1), lambda h, ci: (h, rev(ci), 0)),
                   pacc, pacc, pacc],
        out_shape=[jax.ShapeDtypeStruct((s, N_HEADS * LANES), F32),
                   jax.ShapeDtypeStruct((N_HEADS, s, LANES), F32),
                   jax.ShapeDtypeStruct((N_HEADS, s, LANES), F32),
                   jax.ShapeDtypeStruct((N_HEADS, s, 1), F32),
                   jax.ShapeDtypeStruct((N_HEADS, 1, LANES), F32),
                   jax.ShapeDtypeStruct((N_HEADS, 1, LANES), F32),
                   jax.ShapeDtypeStruct((N_HEADS, 1, LANES), F32)],
        scratch_shapes=[pltpu.VMEM((LANES, LANES), F32)],
        compiler_params=_cparams(("parallel", "arbitrary")),
    )(xbc, xbc, xbc, dtcol, dtrow, bias, a_log, dskip, states, dy)


def _att_tile(s):
    return _pick(s, (256, 128))


def _att_mask(i, j, t, transposed=False):
    qpos = i * t + lax.broadcasted_iota(jnp.int32, (t, t), 1 if transposed else 0)
    kpos = j * t + lax.broadcasted_iota(jnp.int32, (t, t), 0 if transposed else 1)
    return kpos <= qpos


def _flash_fwd(q, k, v, *, name):
    s = q.shape[0]
    t = _att_tile(s)
    nq = s // t

    def body(q_ref, k_ref, v_ref, o_ref, lse_ref, m_ref, l_ref, acc_ref):
        i, j = pl.program_id(1), pl.program_id(2)

        @pl.when(j == 0)
        def _():
            m_ref[...] = jnp.full_like(m_ref, -jnp.inf)
            l_ref[...] = jnp.zeros_like(l_ref)
            acc_ref[...] = jnp.zeros_like(acc_ref)

        @pl.when(j <= i)
        def _():
            sc = _dot(q_ref[...], k_ref[...], tb=True) * ATT_SCALE
            sc = jnp.where(_att_mask(i, j, t), sc, -jnp.inf)
            m_old = m_ref[...]
            m_new = jnp.maximum(m_old, jnp.max(sc, axis=1, keepdims=True))
            alpha = jnp.exp(m_old - m_new)
            p = jnp.exp(sc - m_new)
            l_ref[...] = alpha * l_ref[...] + jnp.sum(p, axis=1, keepdims=True)
            acc_ref[...] = alpha * acc_ref[...] + _dot(p, v_ref[...])
            m_ref[...] = m_new

        @pl.when(j == nq - 1)
        def _():
            o_ref[...] = (acc_ref[...] / l_ref[...]).astype(o_ref.dtype)
            lse_ref[0] = m_ref[...] + jnp.log(l_ref[...])

    kv_spec = pl.BlockSpec((t, LANES), lambda h, i, j: (jnp.minimum(j, i), h))
    return pl.pallas_call(
        body, name=name, grid=(N_HEADS, nq, nq),
        in_specs=[pl.BlockSpec((t, LANES), lambda h, i, j: (i, h)), kv_spec, kv_spec],
        out_specs=[pl.BlockSpec((t, LANES), lambda h, i, j: (i, h)),
                   pl.BlockSpec((1, t, 1), lambda h, i, j: (h, i, 0))],
        out_shape=[jax.ShapeDtypeStruct(q.shape, BF16), jax.ShapeDtypeStruct((N_HEADS, s, 1), F32)],
        scratch_shapes=[pltpu.VMEM((t, 1), F32), pltpu.VMEM((t, 1), F32), pltpu.VMEM((t, LANES), F32)],
        compiler_params=_cparams(("parallel", "parallel", "arbitrary")),
    )(q, k, v)


def _flash_bwd_dq(q, k, v, o, do, lse, *, name):
    s = q.shape[0]
    t = _att_tile(s)
    nq = s // t

    def body(q_ref, k_ref, v_ref, o_ref, do_ref, lse_ref, dq_ref, dl_ref, acc_ref):
        i, j = pl.program_id(1), pl.program_id(2)

        @pl.when(j == 0)
        def _():
            acc_ref[...] = jnp.zeros_like(acc_ref)
            dl_ref[0] = jnp.sum(do_ref[...].astype(F32) * o_ref[...].astype(F32), axis=1, keepdims=True)

        @pl.when(j <= i)
        def _():
            sc = _dot(q_ref[...], k_ref[...], tb=True) * ATT_SCALE
            p = jnp.where(_att_mask(i, j, t), jnp.exp(sc - lse_ref[0]), 0.0)
            dp = _dot(do_ref[...], v_ref[...], tb=True)
            ds = p * (dp - dl_ref[0]) * ATT_SCALE
            acc_ref[...] += _dot(ds, k_ref[...])

        @pl.when(j == nq - 1)
        def _():
            dq_ref[...] = acc_ref[...]

    q_spec = pl.BlockSpec((t, LANES), lambda h, i, j: (i, h))
    kv_spec = pl.BlockSpec((t, LANES), lambda h, i, j: (jnp.minimum(j, i), h))
    col_spec = pl.BlockSpec((1, t, 1), lambda h, i, j: (h, i, 0))
    return pl.pallas_call(
        body, name=name, grid=(N_HEADS, nq, nq),
        in_specs=[q_spec, kv_spec, kv_spec, q_spec, q_spec, col_spec],
        out_specs=[q_spec, col_spec],
        out_shape=[jax.ShapeDtypeStruct(q.shape, F32), jax.ShapeDtypeStruct((N_HEADS, s, 1), F32)],
        scratch_shapes=[pltpu.VMEM((t, LANES), F32)],
        compiler_params=_cparams(("parallel", "parallel", "arbitrary")),
    )(q, k, v, o, do, lse)


def _flash_bwd_dkv(q, k, v, do, lse_row, delta_row, *, name):
    s = q.shape[0]
    t = _att_tile(s)
    nq = s // t

    def body(q_ref, k_ref, v_ref, do_ref, lse_ref, dl_ref, dk_ref, dv_ref, dk_acc, dv_acc):
        j, i = pl.program_id(1), pl.program_id(2)

        @pl.when(i == 0)
        def _():
            dk_acc[...] = jnp.zeros_like(dk_acc)
            dv_acc[...] = jnp.zeros_like(dv_acc)

        @pl.when(i >= j)
        def _():
            sc_t = _dot(k_ref[...], q_ref[...], tb=True) * ATT_SCALE
            p_t = jnp.where(_att_mask(i, j, t, transposed=True), jnp.exp(sc_t - lse_ref[0]), 0.0)
            dv_acc[...] += _dot(p_t, do_ref[...])
            dp_t = _dot(v_ref[...], do_ref[...], tb=True)
            ds_t = p_t * (dp_t - dl_ref[0]) * ATT_SCALE
            dk_acc[...] += _dot(ds_t, q_ref[...])

        @pl.when(i == nq - 1)
        def _():
            dk_ref[...] = dk_acc[...]
            dv_ref[...] = dv_acc[...]

    q_spec = pl.BlockSpec((t, LANES), lambda h, j, i: (jnp.maximum(i, j), h))
    kv_spec = pl.BlockSpec((t, LANES), lambda h, j, i: (j, h))
    row_spec = pl.BlockSpec((1, 1, t), lambda h, j, i: (h, 0, jnp.maximum(i, j)))
    return pl.pallas_call(
        body, name=name, grid=(N_HEADS, nq, nq),
        in_specs=[q_spec, kv_spec, kv_spec, q_spec, row_spec, row_spec],
        out_specs=[kv_spec, kv_spec],
        out_shape=[jax.ShapeDtypeStruct(q.shape, F32)] * 2,
        scratch_shapes=[pltpu.VMEM((t, LANES), F32)] * 2,
        compiler_params=_cparams(("parallel", "parallel", "arbitrary")),
    )(q, k, v, do, lse_row, delta_row)


def _rope(v, cos_t, sin_p, sin_m):
    return v * cos_t + pltpu.roll(v, QK_ROPE // 2, 1) * sin_p + pltpu.roll(v, LANES - QK_ROPE // 2, 1) * sin_m


def _rope_t(d, cos_t, sin_p, sin_m):
    return d * cos_t + pltpu.roll(d * sin_p, LANES - QK_ROPE // 2, 1) + pltpu.roll(d * sin_m, QK_ROPE // 2, 1)


def _att_prep(q_pad, kv2, kr, cos_t, sin_p, sin_m, *, name):
    w = N_HEADS * LANES

    def fn(qv, kvv, krv, c, sp, sm):
        kr_rot = _rope(krv, c, sp, sm)
        qs, ks = [], []
        for h in range(N_HEADS):
            blk = slice(h * LANES, (h + 1) * LANES)
            qs.append(_rope(qv[:, blk], c, sp, sm))
            ks.append(kvv[:, blk] + kr_rot)
        return jnp.concatenate(qs, axis=1), jnp.concatenate(ks, axis=1), kvv[:, w:]

    return _rowwise(fn, [q_pad, kv2, kr, cos_t, sin_p, sin_m], [],
                    [(w, BF16, "row"), (w, BF16, "row"), (w, BF16, "row")], name=name)


def _att_prep_bwd(dq, dk, cos_t, sin_p, sin_m, *, name):
    w = N_HEADS * LANES

    def fn(dqv, dkv, c, sp, sm):
        outs, dkr = [], None
        for h in range(N_HEADS):
            blk = slice(h * LANES, (h + 1) * LANES)
            outs.append(_rope_t(dqv[:, blk], c, sp, sm))
            dkr = dkv[:, blk] if dkr is None else dkr + dkv[:, blk]
        return jnp.concatenate(outs, axis=1), _rope_t(dkr, c, sp, sm)

    return _rowwise(fn, [dq, dk, cos_t, sin_p, sin_m], [], [(w, BF16, "row"), (LANES, F32, "row")], name=name)


_ANY = pl.BlockSpec(memory_space=pl.ANY)
_MESH = pl.DeviceIdType.MESH


def _mesh_pos():
    return lax.axis_index("x"), lax.axis_index("y"), lax.axis_index("c")


def _remote(src, dst, send_sem, recv_sem, dev):
    return pltpu.make_async_remote_copy(src_ref=src, dst_ref=dst, send_sem=send_sem, recv_sem=recv_sem,
                                        device_id=dev, device_id_type=_MESH)


def _gather_chips(shard):
    r, w = shard.shape
    half = r // 2

    def body(x_ref, out_ref, send_sems, recv_sems, local_sem):
        x, y, c = _mesh_pos()
        k = 2 * x + y
        sibling = (x, y, 1 - c)
        chips = [(1 - x, y), (x, 1 - y), (1 - x, 1 - y)]

        def blk(chip, hf):
            return out_ref.at[2 * chip[0] + chip[1], pl.ds(hf * half, half), :]

        mine = pltpu.make_async_copy(x_ref, out_ref.at[k], local_sem)
        mine.start()
        first = [_remote(x_ref.at[pl.ds(c * half, half), :], blk((x, y), c), send_sems.at[j], recv_sems.at[j],
                         (*chip, c)) for j, chip in enumerate(chips)]
        for cp in first:
            cp.start()
        passed = [_remote(blk(chip, c), blk(chip, c), send_sems.at[3 + j], recv_sems.at[3 + j], sibling)
                  for j, chip in enumerate(chips)]
        for j, chip in enumerate(chips):
            _remote(blk(chip, c), blk(chip, c), send_sems.at[j], recv_sems.at[j], sibling).wait_recv()
            passed[j].start()
        for j, chip in enumerate(chips):
            _remote(blk(chip, 1 - c), blk(chip, 1 - c), send_sems.at[3 + j], recv_sems.at[3 + j], sibling).wait_recv()
        for cp in first + passed:
            cp.wait_send()
        mine.wait()

    return pl.pallas_call(
        body, name="gather_chips", in_specs=[_ANY], out_specs=_ANY,
        out_shape=jax.ShapeDtypeStruct((4, r, w), shard.dtype),
        scratch_shapes=[pltpu.SemaphoreType.DMA((6,)), pltpu.SemaphoreType.DMA((6,)), pltpu.SemaphoreType.DMA],
    )(shard)


def _swap_halves(g):
    _, r, w = g.shape
    half = r // 2

    def body(g_ref, out_ref, send_sem, recv_sem):
        x, y, c = _mesh_pos()
        cp = _remote(g_ref.at[:, pl.ds((1 - c) * half, half), :], out_ref, send_sem, recv_sem, (x, y, 1 - c))
        cp.start()
        cp.wait()

    return pl.pallas_call(
        body, name="swap_halves", in_specs=[_ANY], out_specs=_ANY,
        out_shape=jax.ShapeDtypeStruct((4, half, w), g.dtype),
        scratch_shapes=[pltpu.SemaphoreType.DMA, pltpu.SemaphoreType.DMA],
    )(g)


def _scatter_chips(part):
    _, hrows, w = part.shape

    def body(p_ref, out_ref, send_sems, recv_sems, local_sem):
        x, y, c = _mesh_pos()
        k = 2 * x + y
        chips = [(1 - x, y), (x, 1 - y), (1 - x, 1 - y)]
        mine = pltpu.make_async_copy(p_ref.at[k], out_ref.at[k], local_sem)
        mine.start()
        cps = [_remote(p_ref.at[2 * chip[0] + chip[1]], out_ref.at[k], send_sems.at[j], recv_sems.at[j], (*chip, c))
               for j, chip in enumerate(chips)]
        for cp in cps:
            cp.start()
        for j, chip in enumerate(chips):
            slot = out_ref.at[2 * chip[0] + chip[1]]
            _remote(slot, slot, send_sems.at[j], recv_sems.at[j], (*chip, c)).wait_recv()
        for cp in cps:
            cp.wait_send()
        mine.wait()

    return pl.pallas_call(
        body, name="scatter_chips", in_specs=[_ANY], out_specs=_ANY,
        out_shape=jax.ShapeDtypeStruct(part.shape, part.dtype),
        scratch_shapes=[pltpu.SemaphoreType.DMA((3,)), pltpu.SemaphoreType.DMA((3,)), pltpu.SemaphoreType.DMA],
    )(part)


def _join_halves(mine_half):
    hrows, w = mine_half.shape

    def body(h_ref, out_ref, send_sem, recv_sem, local_sem):
        x, y, c = _mesh_pos()
        mine = pltpu.make_async_copy(h_ref, out_ref.at[pl.ds(c * hrows, hrows), :], local_sem)
        mine.start()
        cp = _remote(h_ref, out_ref.at[pl.ds(c * hrows, hrows), :], send_sem, recv_sem, (x, y, 1 - c))
        cp.start()
        other = out_ref.at[pl.ds((1 - c) * hrows, hrows), :]
        _remote(other, other, send_sem, recv_sem, (x, y, 1 - c)).wait_recv()
        cp.wait_send()
        mine.wait()

    return pl.pallas_call(
        body, name="join_halves", in_specs=[_ANY], out_specs=_ANY,
        out_shape=jax.ShapeDtypeStruct((2 * hrows, w), mine_half.dtype),
        scratch_shapes=[pltpu.SemaphoreType.DMA, pltpu.SemaphoreType.DMA, pltpu.SemaphoreType.DMA],
    )(mine_half)


def _gather_all(vec, *, name):
    r, w = vec.shape

    def body(v_ref, out_ref, send_sems, recv_sems, local_sem):
        x, y, c = _mesh_pos()

        def slot(px, py, pc):
            return out_ref.at[4 * px + 2 * py + pc]

        mine = pltpu.make_async_copy(v_ref, slot(x, y, c), local_sem)
        mine.start()
        peers = []
        for rel in range(1, 8):
            fx, fy, fc = (rel >> 2) & 1, (rel >> 1) & 1, rel & 1
            peers.append((x ^ fx, y ^ fy, c ^ fc))
        cps = [_remote(v_ref, slot(x, y, c), send_sems.at[j], recv_sems.at[j], peer) for j, peer in enumerate(peers)]
        for cp in cps:
            cp.start()
        for j, peer in enumerate(peers):
            _remote(slot(*peer), slot(*peer), send_sems.at[j], recv_sems.at[j], peer).wait_recv()
        for cp in cps:
            cp.wait_send()
        mine.wait()

    return pl.pallas_call(
        body, name=name, in_specs=[_ANY], out_specs=_ANY,
        out_shape=jax.ShapeDtypeStruct((8, r, w), vec.dtype),
        scratch_shapes=[pltpu.SemaphoreType.DMA((7,)), pltpu.SemaphoreType.DMA((7,)), pltpu.SemaphoreType.DMA],
    )(vec)


def _add_half(g, got, c):
    nb, r, w = g.shape
    half = r // 2
    tm = _pick(half, (256, 128, 64, 32, 16, 8))
    per = half // tm

    def body(c_ref, g_ref, o_ref, out_ref):
        out_ref[...] = g_ref[...] + o_ref[...]

    return pl.pallas_call(
        body, name="add_half",
        grid_spec=pltpu.PrefetchScalarGridSpec(
            num_scalar_prefetch=1, grid=(nb, per),
            in_specs=[pl.BlockSpec((1, tm, w), lambda b, i, c_ref: (b, c_ref[0] * per + i, 0)),
                      pl.BlockSpec((1, tm, w), lambda b, i, c_ref: (b, i, 0))],
            out_specs=pl.BlockSpec((1, tm, w), lambda b, i, c_ref: (b, i, 0))),
        out_shape=jax.ShapeDtypeStruct((nb, half, w), g.dtype),
        compiler_params=_cparams(("parallel", "parallel")),
    )(jnp.reshape(c, (1,)).astype(jnp.int32), g, got)


def _sum_slots(stack, *, name):
    n, r, w = stack.shape
    tm = _pick(r, (256, 128, 64, 32, 16, 8))

    def body(s_ref, out_ref):
        acc = s_ref[0]
        for i in range(1, n):
            acc = acc + s_ref[i]
        out_ref[...] = acc

    return pl.pallas_call(
        body, name=name, grid=(r // tm,),
        in_specs=[pl.BlockSpec((n, tm, w), lambda i: (0, i, 0))],
        out_specs=pl.BlockSpec((tm, w), lambda i: (i, 0)),
        out_shape=jax.ShapeDtypeStruct((r, w), stack.dtype),
        compiler_params=_cparams(("parallel",)),
    )(stack)


def _adamw(w, g, m, v, *, name):
    shape = w.shape
    cols = shape[-1]
    flat = lambda t: t.reshape(-1, cols)

    def fn(wv, gv, mv, vv):
        m_new = ADAM_B1 * mv + (1.0 - ADAM_B1) * gv
        v_new = ADAM_B2 * vv + (1.0 - ADAM_B2) * (gv * gv)
        m_hat = m_new / (1.0 - ADAM_B1 ** ADAM_STEP)
        v_hat = v_new / (1.0 - ADAM_B2 ** ADAM_STEP)
        delta = -ADAM_LR * (m_hat / (jnp.sqrt(v_hat) + ADAM_EPS) + ADAM_WD * wv)
        return delta, m_new, v_new

    rows = flat(w).shape[0]
    tm = _pick(rows, (256, 128, 64, 32, 16, 8))
    outs = _rowwise(fn, [flat(w), flat(g), flat(m), flat(v)], [], [(cols, F32, "row")] * 3, name=name, tm=tm)
    return tuple(o.reshape(shape) for o in outs)


def _pad_blocks(w, axis, n_blocks, real, to=LANES, offset=0):
    axis = axis % w.ndim
    shp = w.shape
    w = w.reshape(shp[:axis] + (n_blocks, real) + shp[axis + 1:])
    pads = [(0, 0)] * w.ndim
    pads[axis + 1] = (offset, to - real - offset)
    w = jnp.pad(w, pads)
    return w.reshape(shp[:axis] + (n_blocks * to,) + shp[axis + 1:])


def _unpad_blocks(w, axis, n_blocks, real, to=LANES, offset=0):
    axis = axis % w.ndim
    shp = w.shape
    w = w.reshape(shp[:axis] + (n_blocks, to) + shp[axis + 1:])
    w = lax.slice_in_dim(w, offset, offset + real, axis=axis + 1)
    return w.reshape(shp[:axis] + (n_blocks * real,) + shp[axis + 1:])


def _block_diag(w):
    n, a, b = w.shape
    eye = jnp.eye(n, dtype=w.dtype)
    return (eye[:, None, :, None] * w[:, :, None, :]).reshape(n * a, n * b)


def _block_diag_t(d, n):
    a, b = d.shape[0] // n, d.shape[1] // n
    d = d.reshape(n, a, n, b)
    return jnp.stack([d[i, :, i, :] for i in range(n)])


_SPLITS = np.cumsum((0,) + SPLIT_SIZES)


def _w_in_groups(w_in):
    sl = lambda i: w_in[:, _SPLITS[i]:_SPLITS[i + 1]]
    xbc = sl(5)
    xbc_pad = jnp.concatenate([_pad_blocks(xbc[:, :MIX], 1, N_HEADS, HEAD),
                               _pad_blocks(xbc[:, MIX:MIX + 2 * HEAD], 1, 2, HEAD),
                               _pad_blocks(xbc[:, MIX + 2 * HEAD:], 1, 2, HEAD)], axis=1)
    return dict(
        cq=sl(0), ckv=sl(1), kr=_pad_blocks(sl(2), 1, 1, QK_ROPE, offset=HEAD), pool=sl(3),
        z=_pad_blocks(sl(4), 1, N_HEADS, HEAD), xbc=xbc_pad, dt=_pad_blocks(sl(6), 1, 1, N_HEADS),
        lru_g=sl(7), lru_x=sl(8), gates=sl(9))


def _w_in_ungroup(d):
    xbc = d["xbc"]
    w = N_HEADS * LANES
    xbc_real = jnp.concatenate([_unpad_blocks(xbc[:, :w], 1, N_HEADS, HEAD),
                                _unpad_blocks(xbc[:, w:w + 2 * LANES], 1, 2, HEAD),
                                _unpad_blocks(xbc[:, w + 2 * LANES:], 1, 2, HEAD)], axis=1)
    return jnp.concatenate([d["cq"], d["ckv"], _unpad_blocks(d["kr"], 1, 1, QK_ROPE, offset=HEAD), d["pool"],
                            _unpad_blocks(d["z"], 1, N_HEADS, HEAD), xbc_real, _unpad_blocks(d["dt"], 1, 1, N_HEADS),
                            d["lru_g"], d["lru_x"], d["gates"]], axis=1)


def _pad_xbc_vec(v):
    return jnp.concatenate([_pad_blocks(v[..., :MIX], -1, N_HEADS, HEAD),
                            _pad_blocks(v[..., MIX:MIX + 2 * HEAD], -1, 2, HEAD),
                            _pad_blocks(v[..., MIX + 2 * HEAD:], -1, 2, HEAD)], axis=-1)


def _unpad_xbc_vec(v):
    w = N_HEADS * LANES
    return jnp.concatenate([_unpad_blocks(v[..., :w], -1, N_HEADS, HEAD),
                            _unpad_blocks(v[..., w:w + 2 * LANES], -1, 2, HEAD),
                            _unpad_blocks(v[..., w + 2 * LANES:], -1, 2, HEAD)], axis=-1)


def _layer_weights(p):
    q = dict(p)
    q["in"] = _w_in_groups(p["w_in"])
    q["uq"] = _pad_blocks(p["w_uq"], 1, N_HEADS, HEAD + QK_ROPE)
    ukv = p["w_ukv"].reshape(KV_LORA, N_HEADS, 2 * HEAD)
    q["ukv"] = jnp.concatenate([_pad_blocks(ukv[:, :, :HEAD].reshape(KV_LORA, -1), 1, N_HEADS, HEAD),
                                _pad_blocks(ukv[:, :, HEAD:].reshape(KV_LORA, -1), 1, N_HEADS, HEAD)], axis=1)
    q["pool_bd"] = _block_diag(p["w_pool"])
    q["lru_bd"] = jnp.concatenate([_block_diag(p["lru_w_a"]), _block_diag(p["lru_w_i"])], axis=1)
    q["br"] = [_pad_blocks(p["w_branch"][0], 0, N_HEADS, HEAD), p["w_branch"][1],
               _pad_blocks(p["w_branch"][2], 0, N_HEADS, HEAD), p["w_branch"][3]]
    q["ssd_conv_w_pad"] = _pad_xbc_vec(p["ssd_conv_w"])
    q["ssd_conv_b_pad"] = _pad_xbc_vec(p["ssd_conv_b"])[None, :]
    q["ssd_norm_pad"] = _pad_blocks(p["ssd_norm"], 0, N_HEADS, HEAD)[None, :]
    return q


def _row(v):
    return v.reshape(1, -1)


def _scal3(v):
    return v.reshape(N_HEADS, 1, 1)


def _layer_fwd(x, p_emb, w, rope, tag):
    n = lambda s: f"{s}_{tag}"
    sv = {"x": x}
    h = _rms_fwd(x, _row(w["g_mix"]), name=n("rms_mix"))
    sv["h"] = h
    u = {k: _mm(h, wk, name=n("in_" + k)) for k, wk in w["in"].items()}
    sv["u"] = u

    cqn = _rms_fwd(u["cq"], _row(w["q_norm"]), name=n("rms_q"))
    ckvn = _rms_fwd(u["ckv"], _row(w["kv_norm"]), name=n("rms_kv"))
    q_pad = _mm(cqn, w["uq"], name=n("uq"))
    kv2 = _mm(ckvn, w["ukv"], name=n("ukv"))
    qc, kc, vc = _att_prep(q_pad, kv2, u["kr"], *rope, name=n("att_prep"))
    y_a, lse = _flash_fwd(qc, kc, vc, name=n("flash_fwd"))
    sv.update(cqn=cqn, ckvn=ckvn, qc=qc, kc=kc, vc=vc, y_a=y_a, lse=lse)

    pool_d = _pool_fwd(u["pool"], name=n("pool_fwd"))
    yb_pre, y_b = _mm(pool_d, w["pool_bd"], epilogue=lambda acc, sc: (acc, acc * sc),
                      rowvecs=[_row(w["pool_scale"])], out_dtypes=(F32, BF16), name=n("pool_mm"))
    sv.update(pool_d=pool_d, yb_pre=yb_pre, y_b=y_b)

    xbc_c = _conv_fwd(u["xbc"], w["ssd_conv_w_pad"], w["ssd_conv_b_pad"], silu=True, name=n("ssd_conv"))
    dt8 = u["dt"][:, :N_HEADS]
    dtcol = dt8.T[:, :, None]
    dtrow = dt8.T[:, None, :]
    ssd_par = (_scal3(w["ssd_dt_bias"]), _scal3(w["ssd_a_log"]), _scal3(w["ssd_d"]))
    y_ssd, states = _ssd_fwd(xbc_c, dtcol, dtrow, *ssd_par, name=n("ssd_fwd"))

    def ssd_post(yv, zv, gv):
        xh, _ = _rms_parts(yv * _silu(zv), MIX)
        return xh * gv

    y_c = _rowwise(ssd_post, [y_ssd, u["z"]], [w["ssd_norm_pad"]], [(N_HEADS * LANES, BF16, "row")], name=n("ssd_post"))
    sv.update(xbc_c=xbc_c, dtcol=dtcol, dtrow=dtrow, y_ssd=y_ssd, states=states, y_c=y_c)

    xc = _conv_fwd(u["lru_x"], w["lru_conv_w"], _row(w["lru_conv_b"]), silu=False, name=n("lru_conv"))
    pre = _mm(xc, w["lru_bd"], name=n("lru_mm"))
    lru_par = (_row(w["lru_lambda"]), _row(w["lru_b_a"]), _row(w["lru_b_i"]))
    y_d, h_lru = _lru_fwd(pre, xc, u["lru_g"], *lru_par, name=n("lru_fwd"))
    sv.update(xc=xc, pre=pre, h_lru=h_lru, y_d=y_d)

    ys = [y_a, y_b, y_c, y_d]
    merged, ybs = None, []
    for b in range(4):
        gate = u["gates"][:, b * D_MODEL:(b + 1) * D_MODEL]
        if merged is None:
            merged, yb = _mm(ys[b], w["br"][b], epilogue=lambda acc, gt: (_sigmoid(gt) * acc, acc), tiles=[gate],
                             out_dtypes=(F32, F32), name=n(f"branch{b}"))
        else:
            merged, yb = _mm(ys[b], w["br"][b], epilogue=lambda acc, gt, mg: (mg + _sigmoid(gt) * acc, acc),
                             tiles=[gate, merged], out_dtypes=(F32, F32), name=n(f"branch{b}"))
        ybs.append(yb)
    x1 = _mm(merged, w["w_out"], epilogue=lambda acc, xr: (acc + xr,), tiles=[x], name=n("out_proj"))
    sv.update(ybs=ybs, merged=merged, x1=x1)

    h2 = _rms_fwd(x1, _row(w["g_mlp"]), name=n("rms_mlp"))
    a_ff, f_ff = _mm(h2, w["w_ff1"], epilogue=lambda acc: (acc, jnp.square(jnp.maximum(acc, 0.0))),
                     out_dtypes=(F32, BF16), name=n("ff1"))
    x2 = _mm(f_ff, w["w_ff2"], epilogue=lambda acc, xr: (acc + xr,), tiles=[x1], name=n("ff2"))
    sv.update(h2=h2, a_ff=a_ff, f_ff=f_ff, x2=x2)

    h3 = _rms_fwd(x2, _row(w["g_ple"]), name=n("rms_ple"))
    e_ple = _mm(p_emb, w["w_ple"], name=n("ple_emb"))
    x3, gt_ple = _mm(h3, w["w_ple_gate"], epilogue=lambda acc, ev, xr: (xr + ev * _sigmoid(acc), _sigmoid(acc)),
                     tiles=[e_ple, x2], out_dtypes=(F32, F32), name=n("ple_gate"))
    sv.update(h3=h3, e_ple=e_ple, gt_ple=gt_ple, p_emb=p_emb)
    return x3, sv


def _layer_bwd(dx3, sv, w, rope, tag):
    n = lambda s: f"{s}_{tag}"
    gr = {}
    u = sv["u"]

    de, dpre = _rowwise(lambda d, gt, ev: (d * gt, d * ev * gt * (1.0 - gt)), [dx3, sv["gt_ple"], sv["e_ple"]], [],
                        [(D_MODEL, BF16, "row"), (D_MODEL, BF16, "row")], name=n("ple_bwd"))
    gr["w_ple"] = _mm(sv["p_emb"], de, ta=True, name=n("d_w_ple"))
    gr["w_ple_gate"] = _mm(sv["h3"], dpre, ta=True, name=n("d_w_ple_gate"))
    dh3 = _mm(dpre, w["w_ple_gate"], tb=True, out_dtypes=(BF16,), name=n("d_h3"))
    dx2, dg = _rms_bwd(sv["x2"], _row(w["g_ple"]), dh3, dx3, name=n("rms_ple_bwd"))
    gr["g_ple"] = dg[0]

    gr["w_ff2"] = _mm(sv["f_ff"], dx2, ta=True, name=n("d_w_ff2"))
    da = _mm(dx2, w["w_ff2"], tb=True, epilogue=lambda acc, av: (acc * 2.0 * jnp.maximum(av, 0.0),),
             tiles=[sv["a_ff"]], out_dtypes=(BF16,), name=n("d_a_ff"))
    gr["w_ff1"] = _mm(sv["h2"], da, ta=True, name=n("d_w_ff1"))
    dh2 = _mm(da, w["w_ff1"], tb=True, out_dtypes=(BF16,), name=n("d_h2"))
    dx1, dg = _rms_bwd(sv["x1"], _row(w["g_mlp"]), dh2, dx2, name=n("rms_mlp_bwd"))
    gr["g_mlp"] = dg[0]

    gr["w_out"] = _mm(sv["merged"], dx1, ta=True, name=n("d_w_out"))
    dmerged = _mm(dx1, w["w_out"], tb=True, name=n("d_merged"))

    def merge_bwd(dm, gts, y0, y1, y2, y3):
        dys, dgs = [], []
        for b, yb in enumerate((y0, y1, y2, y3)):
            sg = _sigmoid(gts[:, b * D_MODEL:(b + 1) * D_MODEL])
            dys.append(dm * sg)
            dgs.append(dm * yb * sg * (1.0 - sg))
        return (*dys, jnp.concatenate(dgs, axis=1))

    *dybs, dgates = _rowwise(merge_bwd, [dmerged, u["gates"]] + sv["ybs"], [],
                             [(D_MODEL, BF16, "row")] * 4 + [(4 * D_MODEL, BF16, "row")], name=n("merge_bwd"))
    ys = [sv["y_a"], sv["y_b"], sv["y_c"], sv["y_d"]]
    dwb = [_mm(ys[b], dybs[b], ta=True, name=n(f"d_w_branch{b}")) for b in range(4)]
    gr["w_branch"] = jnp.stack([_unpad_blocks(dwb[0], 0, N_HEADS, HEAD), dwb[1],
                                _unpad_blocks(dwb[2], 0, N_HEADS, HEAD), dwb[3]])
    dy_a = _mm(dybs[0], w["br"][0], tb=True, out_dtypes=(BF16,), name=n("d_y_a"))
    dy_b = _mm(dybs[1], w["br"][1], tb=True, name=n("d_y_b"))
    dy_c = _mm(dybs[2], w["br"][2], tb=True, name=n("d_y_c"))
    dy_d = _mm(dybs[3], w["br"][3], tb=True, name=n("d_y_d"))
    du = {"gates": dgates}

    lru_par = (_row(w["lru_lambda"]), _row(w["lru_b_a"]), _row(w["lru_b_i"]))
    dpa, dpi, dxc_direct, du["lru_g"], dlam, dba, dbi = _lru_bwd(
        sv["pre"], sv["xc"], u["lru_g"], *lru_par, sv["h_lru"], dy_d, name=n("lru_bwd"))
    dpre_lru = jnp.concatenate([dpa, dpi], axis=1)
    d_bd = _mm(sv["xc"], dpre_lru, ta=True, name=n("d_lru_w"))
    gr["lru_w_a"] = _block_diag_t(d_bd[:, :MIX], N_HEADS)
    gr["lru_w_i"] = _block_diag_t(d_bd[:, MIX:], N_HEADS)
    gr["lru_lambda"], gr["lru_b_a"], gr["lru_b_i"] = dlam[0], dba[0], dbi[0]
    dxc = _mm(dpre_lru, w["lru_bd"], tb=True, epilogue=lambda acc, t: (acc + t,), tiles=[dxc_direct], name=n("d_xc"))
    du["lru_x"], gr["lru_conv_w"], dcb = _conv_bwd(u["lru_x"], w["lru_conv_w"], _row(w["lru_conv_b"]), dxc,
                                                  silu=False, name=n("lru_conv_bwd"))
    gr["lru_conv_b"] = dcb[0]

    def ssd_post_bwd(dyc, yv, zv, gv):
        sz = _silu(zv)
        dyz, dgain = _rms_bwd_math(yv * sz, gv, dyc, MIX)
        return dyz * sz, dyz * yv * _silu_grad(zv), dgain

    dy_ssd, du["z"], dgain = _rowwise(ssd_post_bwd, [dy_c, sv["y_ssd"], u["z"]], [w["ssd_norm_pad"]],
                                      [(N_HEADS * LANES, F32, "row"), (N_HEADS * LANES, BF16, "row"),
                                       (N_HEADS * LANES, F32, "acc")], name=n("ssd_post_bwd"))
    gr["ssd_norm"] = _unpad_blocks(dgain[0], 0, N_HEADS, HEAD)
    ssd_par = (_scal3(w["ssd_dt_bias"]), _scal3(w["ssd_a_log"]), _scal3(w["ssd_d"]))
    dxs, dbh, dch, ddt, dbias, dalog, dd = _ssd_bwd(sv["xbc_c"], sv["dtcol"], sv["dtrow"], *ssd_par, sv["states"],
                                                    dy_ssd, name=n("ssd_bwd"))
    s = dxs.shape[0]
    group_sum = lambda t: t.reshape(2, 4, s, LANES).sum(axis=1).transpose(1, 0, 2).reshape(s, 2 * LANES)
    dxbc_c = jnp.concatenate([dxs, group_sum(dbh), group_sum(dch)], axis=1)
    gr["ssd_dt_bias"], gr["ssd_a_log"], gr["ssd_d"] = dbias[:, 0, 0], dalog[:, 0, 0], dd[:, 0, 0]
    du["xbc"], dcw, dcb = _conv_bwd(u["xbc"], w["ssd_conv_w_pad"], w["ssd_conv_b_pad"], dxbc_c, silu=True,
                                    name=n("ssd_conv_bwd"))
    gr["ssd_conv_w"], gr["ssd_conv_b"] = _unpad_xbc_vec(dcw), _unpad_xbc_vec(dcb[0])
    du["dt"] = jnp.pad(ddt[:, :, 0].T, ((0, 0), (0, LANES - N_HEADS)))

    dyb_pre, dscale = _rowwise(lambda d, yp, sc: (d * sc, _colsum(d * yp)), [dy_b, sv["yb_pre"]],
                               [_row(w["pool_scale"])], [(MIX, BF16, "row"), (MIX, F32, "acc")], name=n("pool_scale_bwd"))
    gr["pool_scale"] = dscale[0]
    gr["w_pool"] = _block_diag_t(_mm(sv["pool_d"], dyb_pre, ta=True, name=n("d_w_pool")), 4)
    dd_pool = _mm(dyb_pre, w["pool_bd"], tb=True, name=n("d_pool_d"))
    du["pool"] = _pool_bwd(dd_pool, name=n("pool_bwd"))

    dqc, delta = _flash_bwd_dq(sv["qc"], sv["kc"], sv["vc"], sv["y_a"], dy_a, sv["lse"], name=n("flash_dq"))
    to_row = lambda t: t.reshape(N_HEADS, 1, s)
    dkc, dvc = _flash_bwd_dkv(sv["qc"], sv["kc"], sv["vc"], dy_a, to_row(sv["lse"]), to_row(delta), name=n("flash_dkv"))
    dq_pad, du["kr"] = _att_prep_bwd(dqc, dkc, *rope, name=n("att_prep_bwd"))
    d_uq = _mm(sv["cqn"], dq_pad, ta=True, name=n("d_w_uq"))
    gr["w_uq"] = _unpad_blocks(d_uq, 1, N_HEADS, HEAD + QK_ROPE)
    dcqn = _mm(dq_pad, w["uq"], tb=True, out_dtypes=(BF16,), name=n("d_cqn"))
    du["cq"], dg = _rms_bwd(u["cq"], _row(w["q_norm"]), dcqn, name=n("rms_q_bwd"))
    gr["q_norm"] = dg[0]
    dkv2 = jnp.concatenate([dkc, dvc], axis=1).astype(BF16)
    d_ukv = _mm(sv["ckvn"], dkv2, ta=True, name=n("d_w_ukv"))
    wk = N_HEADS * LANES
    dk_real = _unpad_blocks(d_ukv[:, :wk], 1, N_HEADS, HEAD).reshape(KV_LORA, N_HEADS, HEAD)
    dv_real = _unpad_blocks(d_ukv[:, wk:], 1, N_HEADS, HEAD).reshape(KV_LORA, N_HEADS, HEAD)
    gr["w_ukv"] = jnp.concatenate([dk_real, dv_real], axis=2).reshape(KV_LORA, N_HEADS * 2 * HEAD)
    dckvn = _mm(dkv2, w["ukv"], tb=True, out_dtypes=(BF16,), name=n("d_ckvn"))
    du["ckv"], dg = _rms_bwd(u["ckv"], _row(w["kv_norm"]), dckvn, name=n("rms_kv_bwd"))
    gr["kv_norm"] = dg[0]

    dw_in, dh = {}, None
    for k, wk_ in w["in"].items():
        dw_in[k] = _mm(sv["h"], du[k], ta=True, name=n("d_w_in_" + k))
        if dh is None:
            dh = _mm(du[k], wk_, tb=True, name=n("d_h_" + k))
        else:
            dh = _mm(du[k], wk_, tb=True, epilogue=lambda acc, t: (acc + t,), tiles=[dh], name=n("d_h_" + k))
    gr["w_in"] = _w_in_ungroup(dw_in)
    dx, dg = _rms_bwd(sv["x"], _row(w["g_mix"]), dh, dx1, name=n("rms_mix_bwd"))
    gr["g_mix"] = dg[0]
    return dx, gr


def _pack_rows(n_elems):
    per = PACK_W * PACK_ROWS
    return -(-n_elems // per) * PACK_ROWS


def _pack_flat(parts, dtype):
    flat = jnp.concatenate([p.reshape(-1).astype(dtype) for p in parts])
    rows = _pack_rows(flat.shape[0])
    return jnp.pad(flat, (0, rows * PACK_W - flat.shape[0])).reshape(rows, PACK_W)


def _unpack_flat(buf, shapes):
    lead = buf.shape[:-2]
    flat = buf.reshape(lead + (-1,))
    out, off = [], 0
    for shp in shapes:
        size = int(np.prod(shp))
        out.append(flat[..., off:off + size].reshape(lead + tuple(shp)))
        off += size
    return out


def _merge_shards(t, axis):
    return jnp.concatenate([t[i] for i in range(4)], axis=axis)


def _split_shards(t, axis):
    return jnp.stack(jnp.split(t, 4, axis=axis))


def _rope_tables(positions):
    inv = 1.0 / (ROPE_THETA ** (jnp.arange(0, QK_ROPE, 2, dtype=F32) / QK_ROPE))
    ang = positions.astype(F32)[:, None] * inv
    cos, sin = jnp.cos(ang), jnp.sin(ang)
    s = ang.shape[0]
    half = QK_ROPE // 2
    z = lambda n_: jnp.zeros((s, n_), F32)
    cos_t = jnp.concatenate([jnp.ones((s, HEAD), F32), cos, cos, jnp.ones((s, LANES - HEAD - QK_ROPE), F32)], axis=1)
    sin_p = jnp.concatenate([z(HEAD + half), sin, z(LANES - HEAD - QK_ROPE)], axis=1)
    sin_m = jnp.concatenate([z(HEAD), -sin, z(half + LANES - HEAD - QK_ROPE)], axis=1)
    return cos_t, sin_p, sin_m


def _loss_head(x, g, target, *, name):
    d = x.shape[1]

    def fn(xv, tv, gv):
        xh, r = _rms_parts(xv, d)
        y = xh * gv
        err = y - tv
        dy = err * (1.0 / d)
        dxh = dy * gv
        dx = r * (dxh - xh * (jnp.sum(dxh * xh, axis=-1, keepdims=True) * (1.0 / d)))
        return dx, _colsum(dy * xh), _colsum(err * err) * (0.5 / d)

    return _rowwise(fn, [x, target], [g], [(d, F32, "row"), (d, F32, "acc"), (d, F32, "acc")], name=name)


def _step(args):
    x = args["x"][0]
    s = x.shape[0]
    c_idx = lax.axis_index("c")

    big_shapes = [args[nm].shape for nm, _ in BIG]
    mats = [(nm, ax) for nm, ax in BIG if nm not in CONV_SHARDED]
    gathered = _gather_chips(_pack_flat([args[nm] for nm, _ in mats], BF16))
    full = {}
    for (nm, ax), t in zip(mats, _unpack_flat(gathered, [args[nm].shape for nm, _ in mats])):
        full[nm] = _merge_shards(t, ax)
    convs = [(nm, ax) for nm, ax in BIG if nm in CONV_SHARDED]
    conv_all = _gather_all(_pack_flat([args[nm] for nm, _ in convs], F32), name="gather_conv_taps")[0::2]
    for (nm, ax), t in zip(convs, _unpack_flat(conv_all, [args[nm].shape for nm, _ in convs])):
        full[nm] = _merge_shards(t, ax)
    rope = _rope_tables(args["positions"][0])

    layers = []
    for l in range(2):
        p = {nm: full[nm][l] for nm, _ in BIG}
        p.update({nm: args[nm][l] for nm in SMALL if nm != "g_final"})
        layers.append(_layer_weights(p))

    saved = []
    for l in range(2):
        x, sv = _layer_fwd(x, args["p"][l, 0], layers[l], rope, f"l{l}")
        saved.append(sv)

    dx, dg_final, loss_part = _loss_head(x, _row(args["g_final"]), args["loss_target"][0], name="loss_head")
    loss = lax.psum(jnp.sum(loss_part), ("x", "y", "c"))

    grads = [None, None]
    for l in (1, 0):
        dx, grads[l] = _layer_bwd(dx, saved[l], layers[l], rope, f"l{l}")

    g_local = {nm: jnp.stack([grads[0][nm], grads[1][nm]]) for nm in WEIGHTS if nm != "g_final"}
    g_local["g_final"] = dg_final[0]

    packed = jnp.stack([_pack_flat([_split_shards(g_local[nm], ax)[j] for nm, ax in BIG], F32) for j in range(4)])
    chip_sum = _add_half(packed, _swap_halves(packed), c_idx)
    reduced_half = _sum_slots(_scatter_chips(chip_sum), name="sum_chips")
    reduced = _join_halves(reduced_half)
    g_red = dict(zip([nm for nm, _ in BIG], _unpack_flat(reduced, big_shapes)))

    small_shapes = [args[nm].shape for nm in SMALL]
    small_sum = _sum_slots(_gather_all(_pack_flat([g_local[nm] for nm in SMALL], F32), name="gather_small_grads"),
                           name="sum_devices")
    g_red.update(zip(SMALL, _unpack_flat(small_sum, small_shapes)))

    pack_small = lambda pre: _pack_flat([args[pre + nm] for nm in SMALL], F32)
    upd_small = _adamw(pack_small(""), small_sum, pack_small("m_"), pack_small("v_"), name="adamw_small")
    upd = {nm: trip for nm, trip in zip(SMALL, zip(*[_unpack_flat(t, small_shapes) for t in upd_small]))}
    for nm, _ in BIG:
        upd[nm] = _adamw(args[nm], g_red[nm], args["m_" + nm], args["v_" + nm], name="adamw_" + nm)

    outs = [loss, dx[None]]
    outs += [g_red[nm] for nm in WEIGHTS]
    for i in range(3):
        outs += [upd[nm][i] for nm in WEIGHTS]
    return tuple(outs)


_ARG_NAMES = ("x", "p", "positions") + WEIGHTS + ("loss_target",) + tuple("m_" + nm for nm in WEIGHTS) \
    + tuple("v_" + nm for nm in WEIGHTS)


def kernel(*arrays):
    assert len(arrays) == len(_ARG_NAMES), len(arrays)
    return _step(dict(zip(_ARG_NAMES, arrays)))
```

```python
import functools
import math

import jax
import jax.numpy as jnp
import numpy as np
from jax import lax
from jax.experimental import pallas as pl
from jax.experimental.pallas import tpu as pltpu

F32 = jnp.float32
BF16 = jnp.bfloat16
MXU_DTYPE = BF16
LANES = 128
VMEM_LIMIT = 56 * 1024 * 1024
MM_VMEM_BUDGET = 36 * 1024 * 1024

D_MODEL = 1024
N_HEADS = 8
HEAD = 64
QK_ROPE = 32
Q_LORA = 384
KV_LORA = 256
MIX = 512
SSD_CHUNK = 128
CONV_W = 4
POOL_WINDOWS = (2, 4, 8, 16)
LRU_C = 8.0
EPS = 1e-6
ROPE_THETA = 10000.0
ATT_SCALE = (HEAD + QK_ROPE) ** -0.5
SPLIT_SIZES = (Q_LORA, KV_LORA, QK_ROPE, MIX, MIX, 768, N_HEADS, MIX, MIX, 4 * D_MODEL)

ADAM_LR, ADAM_B1, ADAM_B2, ADAM_EPS, ADAM_WD, ADAM_STEP = 0.001, 0.9, 0.999, 1e-08, 0.01, 10

BIG = (("w_in", 2), ("w_uq", 2), ("w_ukv", 2), ("ssd_conv_w", 2), ("lru_conv_w", 2), ("w_branch", 3),
       ("w_out", 1), ("w_ff1", 2), ("w_ff2", 1), ("w_ple_gate", 1), ("w_ple", 2))
SMALL = ("g_mix", "q_norm", "kv_norm", "w_pool", "pool_scale", "ssd_conv_b", "ssd_dt_bias", "ssd_a_log",
         "ssd_d", "ssd_norm", "lru_conv_b", "lru_w_a", "lru_b_a", "lru_w_i", "lru_b_i", "lru_lambda",
         "g_mlp", "g_ple", "g_final")
WEIGHTS = ("g_mix", "w_in", "q_norm", "w_uq", "kv_norm", "w_ukv", "w_pool", "pool_scale", "ssd_conv_w",
           "ssd_conv_b", "ssd_dt_bias", "ssd_a_log", "ssd_d", "ssd_norm", "lru_conv_w", "lru_conv_b", "lru_w_a",
           "lru_b_a", "lru_w_i", "lru_b_i", "lru_lambda", "w_branch", "w_out", "g_mlp", "w_ff1", "w_ff2", "g_ple",
           "w_ple_gate", "w_ple", "g_final")
CONV_SHARDED = ("ssd_conv_w", "lru_conv_w")
PACK_W = 1024
PACK_ROWS = 64


def _cparams(sem, vmem=VMEM_LIMIT):
    return pltpu.CompilerParams(dimension_semantics=sem, vmem_limit_bytes=vmem)


def _pick(n, cands):
    for c in cands:
        if n % c == 0:
            return c
    return n


def _sigmoid(x):
    return 1.0 / (1.0 + jnp.exp(-x))


def _silu(x):
    return x * _sigmoid(x)


def _silu_grad(x):
    s = _sigmoid(x)
    return s * (1.0 + x * (1.0 - s))


def _softplus(x):
    e = jnp.exp(-jnp.abs(x))
    log1p_e = jnp.where(e < 1e-3, e * (1.0 - e * (0.5 - e * (1.0 / 3.0))), jnp.log(1.0 + e))
    return jnp.maximum(x, 0.0) + log1p_e


_GELU_C = math.sqrt(2.0 / math.pi)


def _gelu(x):
    t = jnp.tanh(_GELU_C * (x + 0.044715 * x * x * x))
    return 0.5 * x * (1.0 + t)


def _gelu_grad(x):
    t = jnp.tanh(_GELU_C * (x + 0.044715 * x * x * x))
    return 0.5 * (1.0 + t) + 0.5 * x * (1.0 - t * t) * _GELU_C * (1.0 + 3.0 * 0.044715 * x * x)


def _neg_expm1(x):
    series = -x * (1.0 + 0.5 * x * (1.0 + (1.0 / 3.0) * x * (1.0 + 0.25 * x)))
    return jnp.where(x > -0.05, series, 1.0 - jnp.exp(x))


def _shift_down(x, k, row):
    return jnp.where(row >= k, pltpu.roll(x, k, 0), 0.0)


def _shift_up(x, k, row):
    n = x.shape[0]
    return jnp.where(row < n - k, pltpu.roll(x, n - k, 0), 0.0)


def _cumsum_rows(x, row):
    d = 1
    while d < x.shape[0]:
        x = x + _shift_down(x, d, row)
        d *= 2
    return x


def _rev_cumsum_rows(x, row):
    d = 1
    while d < x.shape[0]:
        x = x + _shift_up(x, d, row)
        d *= 2
    return x


def _cumsum_lanes(x, col):
    d = 1
    while d < x.shape[1]:
        x = x + jnp.where(col >= d, pltpu.roll(x, d, 1), 0.0)
        d *= 2
    return x


def _dot(a, b, ta=False, tb=False):
    dn = (((0 if ta else 1,), (1 if tb else 0,)), ((), ()))
    return lax.dot_general(a.astype(MXU_DTYPE), b.astype(MXU_DTYPE), dn, preferred_element_type=F32)


def _mm_tiles(m, n, k, a_bytes, b_bytes, mn_bytes):
    best = None
    for tm in (1024, 512, 384, 256, 128):
        for tn in (1024, 512, 384, 256, 128):
            for tk in (2048, 1024, 512, 384, 256, 128):
                if m % tm or n % tn or k % tk:
                    continue
                vmem = 2 * (tm * tk * a_bytes + tk * tn * b_bytes) + 2 * tm * tn * mn_bytes + 4 * tm * tn
                vmem += 2 * (tm * tk + tk * tn)
                if vmem > MM_VMEM_BUDGET:
                    continue
                steps = (m // tm) * (n // tn) * (k // tk)
                key = (steps, vmem)
                if best is None or key < best[0]:
                    best = (key, (tm, tn, tk))
    assert best is not None, (m, n, k)
    return best[1]


def _mm(a, b, *, ta=False, tb=False, epilogue=None, tiles=(), rowvecs=(), out_dtypes=(F32,), name):
    m, k = (a.shape[1], a.shape[0]) if ta else a.shape
    n = b.shape[0] if tb else b.shape[1]
    assert (b.shape[1] if tb else b.shape[0]) == k, (a.shape, b.shape, ta, tb)
    mn_bytes = sum(t.dtype.itemsize for t in tiles) + sum(jnp.dtype(dt).itemsize for dt in out_dtypes)
    tm, tn, tk = _mm_tiles(m, n, k, a.dtype.itemsize, b.dtype.itemsize, mn_bytes)
    nk = k // tk
    nt, nr, no = len(tiles), len(rowvecs), len(out_dtypes)

    def body(*refs):
        a_ref, b_ref = refs[0], refs[1]
        tile_refs = refs[2:2 + nt]
        row_refs = refs[2 + nt:2 + nt + nr]
        out_refs = refs[2 + nt + nr:2 + nt + nr + no]
        acc_ref = refs[-1]
        kk = pl.program_id(2)

        @pl.when(kk == 0)
        def _():
            acc_ref[...] = jnp.zeros_like(acc_ref)

        acc_ref[...] += _dot(a_ref[...], b_ref[...], ta, tb)

        @pl.when(kk == nk - 1)
        def _():
            acc = acc_ref[...]
            if epilogue is None:
                outs = (acc,)
            else:
                outs = epilogue(acc, *[t[...] for t in tile_refs], *[r[...] for r in row_refs])
            for o_ref, o in zip(out_refs, outs):
                o_ref[...] = o.astype(o_ref.dtype)

    a_spec = pl.BlockSpec((tk, tm), lambda i, j, kk: (kk, i)) if ta else pl.BlockSpec((tm, tk), lambda i, j, kk: (i, kk))
    b_spec = pl.BlockSpec((tn, tk), lambda i, j, kk: (j, kk)) if tb else pl.BlockSpec((tk, tn), lambda i, j, kk: (kk, j))
    mn_spec = pl.BlockSpec((tm, tn), lambda i, j, kk: (i, j))
    row_spec = pl.BlockSpec((1, tn), lambda i, j, kk: (0, j))
    outs = pl.pallas_call(
        body, name=name,
        grid=(m // tm, n // tn, nk),
        in_specs=[a_spec, b_spec] + [mn_spec] * nt + [row_spec] * nr,
        out_specs=[mn_spec] * no,
        out_shape=[jax.ShapeDtypeStruct((m, n), dt) for dt in out_dtypes],
        scratch_shapes=[pltpu.VMEM((tm, tn), F32)],
        compiler_params=_cparams(("parallel", "parallel", "arbitrary")),
    )(a, b, *tiles, *rowvecs)
    return outs[0] if no == 1 else tuple(outs)


def _rowwise(fn, rows, fulls, outs, *, name, tm=None):
    r = rows[0].shape[0]
    if tm is None:
        widest = max([x.shape[1] for x in rows] + [o[0] for o in outs])
        tm = _pick(r, (max(8, min(512, (512 * 1024) // widest)), 256, 128, 64, 32, 16, 8))
    nrow, nfull, nout = len(rows), len(fulls), len(outs)

    def body(*refs):
        row_refs = refs[:nrow]
        full_refs = refs[nrow:nrow + nfull]
        out_refs = refs[nrow + nfull:]
        res = fn(*[x[...] for x in row_refs], *[x[...] for x in full_refs])
        if not isinstance(res, (tuple, list)):
            res = (res,)
        step = pl.program_id(0)
        for o_ref, o, spec in zip(out_refs, res, outs):
            if spec[2] == "row":
                o_ref[...] = o.astype(o_ref.dtype)
            else:
                @pl.when(step == 0)
                def _(o_ref=o_ref):
                    o_ref[...] = jnp.zeros_like(o_ref)
                o_ref[...] += o

    in_specs = [pl.BlockSpec((tm, x.shape[1]), lambda i: (i, 0)) for x in rows]
    in_specs += [pl.BlockSpec(x.shape, lambda i, nd=x.ndim: (0,) * nd) for x in fulls]
    out_specs, out_shape = [], []
    for c, dt, kind in outs:
        if kind == "row":
            out_specs.append(pl.BlockSpec((tm, c), lambda i: (i, 0)))
            out_shape.append(jax.ShapeDtypeStruct((r, c), dt))
        else:
            out_specs.append(pl.BlockSpec((1, c), lambda i: (0, 0)))
            out_shape.append(jax.ShapeDtypeStruct((1, c), F32))
    res = pl.pallas_call(
        body, name=name, grid=(r // tm,), in_specs=in_specs, out_specs=out_specs, out_shape=out_shape,
        compiler_params=_cparams(("arbitrary",)),
    )(*rows, *fulls)
    return res[0] if nout == 1 else tuple(res)


def _colsum(x):
    return jnp.sum(x, axis=0, keepdims=True)


def _rms_parts(x, n_real):
    r = lax.rsqrt(jnp.sum(x * x, axis=-1, keepdims=True) * (1.0 / n_real) + EPS)
    return x * r, r


def _rms_fwd(x, g, *, n_real=None, out_dtype=BF16, name):
    n_real = n_real or x.shape[1]

    def fn(xv, gv):
        xh, _ = _rms_parts(xv, n_real)
        return xh * gv

    return _rowwise(fn, [x], [g], [(x.shape[1], out_dtype, "row")], name=name)


def _rms_bwd_math(xv, gv, dh, n_real):
    xh, r = _rms_parts(xv, n_real)
    dxh = dh * gv
    dx = r * (dxh - xh * (jnp.sum(dxh * xh, axis=-1, keepdims=True) * (1.0 / n_real)))
    return dx, _colsum(dh * xh)


def _rms_bwd(x, g, dh, res=None, *, name):
    n = x.shape[1]
    if res is None:
        def fn(xv, dhv, gv):
            return _rms_bwd_math(xv, gv, dhv.astype(F32), n)
        rows = [x, dh]
    else:
        def fn(xv, dhv, rv, gv):
            dx, dg = _rms_bwd_math(xv, gv, dhv.astype(F32), n)
            return dx + rv, dg
        rows = [x, dh, res]
    return _rowwise(fn, rows, [g], [(n, F32, "row"), (n, F32, "acc")], name=name)


def _seq_call(body, ins, outs, n_blocks, *, name):
    in_specs, args = [], []
    for x, kind in ins:
        in_specs.append(pl.BlockSpec((x.shape[0], LANES), lambda j: (0, j)))
        args.append(x)
    out_specs, out_shape = [], []
    for shape, dt in outs:
        out_specs.append(pl.BlockSpec((shape[0], LANES), lambda j: (0, j)))
        out_shape.append(jax.ShapeDtypeStruct(shape, dt))
    res = pl.pallas_call(body, name=name, grid=(n_blocks,), in_specs=in_specs, out_specs=out_specs,
                         out_shape=out_shape, compiler_params=_cparams(("parallel",)))(*args)
    return res[0] if len(outs) == 1 else tuple(res)


def _conv_pre(x, w, b, row):
    acc = x * w[CONV_W - 1:CONV_W, :] + b
    for k in range(CONV_W - 1):
        acc = acc + _shift_down(x, CONV_W - 1 - k, row) * w[k:k + 1, :]
    return acc


def _conv_fwd(x, w, b, *, silu, name):
    s, c = x.shape

    def body(x_ref, w_ref, b_ref, y_ref):
        xv = x_ref[...]
        row = lax.broadcasted_iota(jnp.int32, xv.shape, 0)
        pre = _conv_pre(xv, w_ref[...], b_ref[...], row)
        y_ref[...] = _silu(pre) if silu else pre

    return _seq_call(body, [(x, "seq"), (w, "par"), (b, "par")], [((s, c), F32)], c // LANES, name=name)


def _conv_bwd(x, w, b, dy, *, silu, name):
    s, c = x.shape

    def body(x_ref, w_ref, b_ref, dy_ref, dx_ref, dw_ref, db_ref):
        xv, wv, dv = x_ref[...], w_ref[...], dy_ref[...]
        row = lax.broadcasted_iota(jnp.int32, xv.shape, 0)
        if silu:
            dv = dv * _silu_grad(_conv_pre(xv, wv, b_ref[...], row))
        dx = dv * wv[CONV_W - 1:CONV_W, :]
        dws = [None] * CONV_W
        dws[CONV_W - 1] = _colsum(dv * xv)
        for k in range(CONV_W - 1):
            sh = CONV_W - 1 - k
            dx = dx + _shift_up(dv, sh, row) * wv[k:k + 1, :]
            dws[k] = _colsum(dv * _shift_down(xv, sh, row))
        dx_ref[...] = dx
        for k in range(CONV_W):
            dw_ref[k:k + 1, :] = dws[k]
        db_ref[...] = _colsum(dv)

    return _seq_call(body, [(x, "seq"), (w, "par"), (b, "par"), (dy, "seq")],
                     [((s, c), F32), ((CONV_W, c), F32), ((1, c), F32)], c // LANES, name=name)


def _pool_select(levels):
    g = pl.program_id(0)
    return jnp.where(g == 0, levels[0], jnp.where(g == 1, levels[1], jnp.where(g == 2, levels[2], levels[3])))


def _pool_count(row):
    g = pl.program_id(0)
    w = jnp.where(g == 0, POOL_WINDOWS[0], jnp.where(g == 1, POOL_WINDOWS[1],
                                                     jnp.where(g == 2, POOL_WINDOWS[2], POOL_WINDOWS[3])))
    return jnp.minimum(row + 1, w).astype(F32)


def _pool_fwd(u, *, name):
    def body(u_ref, d_ref):
        uv = u_ref[...]
        row = lax.broadcasted_iota(jnp.int32, uv.shape, 0)
        levels, cur, sh = [], uv, 1
        for _ in POOL_WINDOWS:
            cur = cur + _shift_down(cur, sh, row)
            levels.append(cur)
            sh *= 2
        d_ref[...] = _pool_select(levels) / _pool_count(row) - uv

    return _seq_call(body, [(u, "seq")], [(u.shape, F32)], u.shape[1] // LANES, name=name)


def _pool_bwd(dd, *, name):
    def body(dd_ref, du_ref):
        dv = dd_ref[...]
        row = lax.broadcasted_iota(jnp.int32, dv.shape, 0)
        levels, cur, sh = [], dv / _pool_count(row), 1
        for _ in POOL_WINDOWS:
            cur = cur + _shift_up(cur, sh, row)
            levels.append(cur)
            sh *= 2
        du_ref[...] = _pool_select(levels) - dv

    return _seq_call(body, [(dd, "seq")], [(dd.shape, F32)], dd.shape[1] // LANES, name=name)


def _lru_gates(pre_a, pre_i, xc, lam, b_a, b_i):
    r = _sigmoid(pre_a + b_a)
    i = _sigmoid(pre_i + b_i)
    sp = _softplus(-lam)
    log_a = -LRU_C * r * sp
    a = jnp.exp(log_a)
    mult = jnp.sqrt(_neg_expm1(2.0 * log_a))
    return r, i, sp, a, mult


def _lru_fwd(pre, xc, gate_in, lam, b_a, b_i, *, name):
    s, c = xc.shape
    nb = c // LANES

    def body(pa_ref, pi_ref, xc_ref, g_ref, lam_ref, ba_ref, bi_ref, y_ref, h_ref):
        xv = xc_ref[...]
        row = lax.broadcasted_iota(jnp.int32, xv.shape, 0)
        _, i, _, a, mult = _lru_gates(pa_ref[...], pi_ref[...], xv, lam_ref[...], ba_ref[...], bi_ref[...])
        h = xv * i * mult
        d = 1
        while d < s:
            h = h + a * _shift_down(h, d, row)
            a = a * jnp.where(row >= d, pltpu.roll(a, d, 0), 1.0)
            d *= 2
        h_ref[...] = h
        y_ref[...] = h * _gelu(g_ref[...])

    blk = lambda off: pl.BlockSpec((s, LANES), lambda j: (0, j + off))
    par = pl.BlockSpec((1, LANES), lambda j: (0, j))
    return pl.pallas_call(
        body, name=name, grid=(nb,),
        in_specs=[blk(0), blk(nb), blk(0), blk(0), par, par, par],
        out_specs=[blk(0), blk(0)],
        out_shape=[jax.ShapeDtypeStruct((s, c), F32)] * 2,
        compiler_params=_cparams(("parallel",)),
    )(pre, pre, xc, gate_in, lam, b_a, b_i)


def _lru_bwd(pre, xc, gate_in, lam, b_a, b_i, h, dy, *, name):
    s, c = xc.shape
    nb = c // LANES

    def body(pa_ref, pi_ref, xc_ref, g_ref, lam_ref, ba_ref, bi_ref, h_ref, dy_ref,
             dpa_ref, dpi_ref, dxc_ref, dg_ref, dlam_ref, dba_ref, dbi_ref):
        xv, gv, hv, dv = xc_ref[...], g_ref[...], h_ref[...], dy_ref[...]
        row = lax.broadcasted_iota(jnp.int32, xv.shape, 0)
        r, i, sp, a, mult = _lru_gates(pa_ref[...], pi_ref[...], xv, lam_ref[...], ba_ref[...], bi_ref[...])
        dg_ref[...] = dv * hv * _gelu_grad(gv)
        dh = dv * _gelu(gv)
        an = jnp.where(row < s - 1, pltpu.roll(a, s - 1, 0), 0.0)
        d = 1
        while d < s:
            dh = dh + an * _shift_up(dh, d, row)
            an = an * jnp.where(row < s - d, pltpu.roll(an, s - d, 0), 1.0)
            d *= 2
        da = dh * _shift_down(hv, 1, row)
        dxc_ref[...] = dh * i * mult
        di = dh * xv * mult
        dmult = dh * xv * i
        dlog_a = (da - dmult * a / mult) * a
        dr = dlog_a * (-LRU_C) * sp
        dlam_ref[...] = _colsum(dlog_a * LRU_C * r * _sigmoid(-lam_ref[...]))
        dpa = dr * r * (1.0 - r)
        dpi = di * i * (1.0 - i)
        dpa_ref[...] = dpa
        dpi_ref[...] = dpi
        dba_ref[...] = _colsum(dpa)
        dbi_ref[...] = _colsum(dpi)

    blk = lambda off: pl.BlockSpec((s, LANES), lambda j: (0, j + off))
    par = pl.BlockSpec((1, LANES), lambda j: (0, j))
    sc = jax.ShapeDtypeStruct((s, c), F32)
    pc = jax.ShapeDtypeStruct((1, c), F32)
    dpa, dpi, dxc, dg, dlam, dba, dbi = pl.pallas_call(
        body, name=name, grid=(nb,),
        in_specs=[blk(0), blk(nb), blk(0), blk(0), par, par, par, blk(0), blk(0)],
        out_specs=[blk(0), blk(0), blk(0), blk(0), par, par, par],
        out_shape=[sc, sc, sc, sc, pc, pc, pc],
        compiler_params=_cparams(("parallel",)),
    )(pre, pre, xc, gate_in, lam, b_a, b_i, h, dy)
    return dpa, dpi, dxc, dg, dlam, dba, dbi


def _ssd_chunk_terms(dtcol, dtrow, bias, a_log):
    shp = (SSD_CHUNK, SSD_CHUNK)
    row = lax.broadcasted_iota(jnp.int32, shp, 0)
    col = lax.broadcasted_iota(jnp.int32, shp, 1)
    a_head = -jnp.exp(a_log)
    dt_c = jnp.broadcast_to(_softplus(dtcol + bias), shp)
    dt_r = jnp.broadcast_to(_softplus(dtrow + bias), shp)
    cs_c = _cumsum_rows(dt_c * a_head, row)
    cs_r = _cumsum_lanes(dt_r * a_head, col)
    cs_last = jnp.sum(jnp.where(row == SSD_CHUNK - 1, cs_c, 0.0), axis=0, keepdims=True)
    return row, col, a_head, dt_c, cs_c, cs_r, cs_last


def _ssd_fwd(xbc, dtcol, dtrow, bias, a_log, dskip, *, name):
    s = xbc.shape[0]
    nc = s // SSD_CHUNK

    def body(x_ref, b_ref, c_ref, dtc_ref, dtr_ref, bias_ref, alog_ref, d_ref, y_ref, st_ref, state):
        ci = pl.program_id(1)

        @pl.when(ci == 0)
        def _():
            state[...] = jnp.zeros_like(state)

        xv, bm, cm = x_ref[...], b_ref[...], c_ref[...]
        row, col, _, dt_c, cs_c, cs_r, cs_last = _ssd_chunk_terms(
            dtc_ref[0], dtr_ref[0], bias_ref[0], alog_ref[0])
        lmat = jnp.exp(jnp.where(col <= row, cs_c - cs_r, -jnp.inf))
        g = _dot(cm, bm, tb=True) * lmat
        xdt = xv * dt_c
        st = state[...]
        st_ref[0, 0] = st
        y_ref[...] = _dot(g, xdt) + _dot(cm, st) * jnp.exp(cs_c) + xv * d_ref[0]
        w = xdt * jnp.exp(cs_last - cs_c)
        state[...] = jnp.exp(cs_last) * st + _dot(bm.T, w)

    hc = lambda h, ci: (ci, h)
    scal = pl.BlockSpec((1, 1, 1), lambda h, ci: (h, 0, 0))
    return pl.pallas_call(
        body, name=name, grid=(N_HEADS, nc),
        in_specs=[pl.BlockSpec((SSD_CHUNK, LANES), hc),
                  pl.BlockSpec((SSD_CHUNK, LANES), lambda h, ci: (ci, N_HEADS + h // 4)),
                  pl.BlockSpec((SSD_CHUNK, LANES), lambda h, ci: (ci, N_HEADS + 2 + h // 4)),
                  pl.BlockSpec((1, SSD_CHUNK, 1), lambda h, ci: (h, ci, 0)),
                  pl.BlockSpec((1, 1, SSD_CHUNK), lambda h, ci: (h, 0, ci)),
                  scal, scal, scal],
        out_specs=[pl.BlockSpec((SSD_CHUNK, LANES), hc),
                   pl.BlockSpec((1, 1, LANES, LANES), lambda h, ci: (h, ci, 0, 0))],
        out_shape=[jax.ShapeDtypeStruct((s, N_HEADS * LANES), F32),
                   jax.ShapeDtypeStruct((N_HEADS, nc, LANES, LANES), F32)],
        scratch_shapes=[pltpu.VMEM((LANES, LANES), F32)],
        compiler_params=_cparams(("parallel", "arbitrary")),
    )(xbc, xbc, xbc, dtcol, dtrow, bias, a_log, dskip)


def _ssd_bwd(xbc, dtcol, dtrow, bias, a_log, dskip, states, dy, *, name):
    s = xbc.shape[0]
    nc = s // SSD_CHUNK

    def body(x_ref, b_ref, c_ref, dtc_ref, dtr_ref, bias_ref, alog_ref, d_ref, st_ref, dy_ref,
             dx_ref, db_ref, dc_ref, ddt_ref, dbias_ref, dalog_ref, dd_ref, dstate):
        ci = pl.program_id(1)

        @pl.when(ci == 0)
        def _():
            dstate[...] = jnp.zeros_like(dstate)
            dbias_ref[...] = jnp.zeros_like(dbias_ref)
            dalog_ref[...] = jnp.zeros_like(dalog_ref)
            dd_ref[...] = jnp.zeros_like(dd_ref)

        xv, bm, cm, dyv, st = x_ref[...], b_ref[...], c_ref[...], dy_ref[...], st_ref[0, 0]
        dtraw_c = dtc_ref[0]
        bias = bias_ref[0]
        row, col, a_head, dt_c, cs_c, cs_r, cs_last = _ssd_chunk_terms(dtraw_c, dtr_ref[0], bias, alog_ref[0])
        lmat = jnp.exp(jnp.where(col <= row, cs_c - cs_r, -jnp.inf))
        lmat_t = jnp.exp(jnp.where(row <= col, cs_r - cs_c, -jnp.inf))
        g = _dot(cm, bm, tb=True) * lmat
        g_t = _dot(bm, cm, tb=True) * lmat_t
        xdt = xv * dt_c
        e_c = jnp.exp(cs_c)
        f_c = jnp.exp(cs_last - cs_c)
        e_last = jnp.exp(cs_last)
        w = xdt * f_c
        dst = dstate[...]

        dg = _dot(dyv, xdt, tb=True)
        dg_t = _dot(xdt, dyv, tb=True)
        dxdt = _dot(g_t, dyv)
        rowsum = lambda v: jnp.sum(v, axis=1, keepdims=True)
        dcs = rowsum(dg * g) - rowsum(dg_t * g_t)
        dcm = _dot(dg * lmat, bm)
        dbm = _dot(dg_t * lmat_t, cm)
        z = _dot(cm, st)
        dz = dyv * e_c
        dcs = dcs + rowsum(dz * z)
        dcm = dcm + _dot(dz, st, tb=True)
        dst_in = _dot(cm.T, dz) + e_last * dst
        dcs_last = jnp.sum(jnp.sum(dst * st, axis=1, keepdims=True), axis=0, keepdims=True) * jnp.max(e_last, axis=1, keepdims=True)
        dbm = dbm + _dot(w, dst, tb=True)
        dw = _dot(bm, dst)
        dxdt = dxdt + dw * f_c
        q = rowsum(dw * w)
        dcs = dcs - q
        dcs_last = dcs_last + jnp.sum(q, axis=0, keepdims=True)
        dx_ref[...] = dxdt * dt_c + dyv * d_ref[0]
        ddt = rowsum(dxdt * xv)
        dcs_full = jnp.broadcast_to(dcs, (SSD_CHUNK, SSD_CHUNK)) + jnp.where(row == SSD_CHUNK - 1, dcs_last, 0.0)
        da = jnp.max(_rev_cumsum_rows(dcs_full, row), axis=1, keepdims=True)
        dt_col = jnp.max(dt_c, axis=1, keepdims=True)
        ddt = ddt + da * a_head
        draw = ddt * _sigmoid(dtraw_c + bias)
        ddt_ref[0] = draw
        db_ref[0] = dbm
        dc_ref[0] = dcm
        dstate[...] = dst_in
        tot = lambda v: jnp.broadcast_to(jnp.sum(v, axis=0, keepdims=True), (1, LANES))
        dbias_ref[0] += tot(draw)
        dalog_ref[0] += tot(da * dt_col) * a_head
        dd_ref[0] += tot(rowsum(dyv * xv))

    rev = lambda ci: nc - 1 - ci
    hc = lambda h, ci: (rev(ci), h)
    scal = pl.BlockSpec((1, 1, 1), lambda h, ci: (h, 0, 0))
    pacc = pl.BlockSpec((1, 1, LANES), lambda h, ci: (h, 0, 0))
    per_head = pl.BlockSpec((1, SSD_CHUNK, LANES), lambda h, ci: (h, rev(ci), 0))
    return pl.pallas_call(
        body, name=name, grid=(N_HEADS, nc),
        in_specs=[pl.BlockSpec((SSD_CHUNK, LANES), hc),
                  pl.BlockSpec((SSD_CHUNK, LANES), lambda h, ci: (rev(ci), N_HEADS + h // 4)),
                  pl.BlockSpec((SSD_CHUNK, LANES), lambda h, ci: (rev(ci), N_HEADS + 2 + h // 4)),
                  pl.BlockSpec((1, SSD_CHUNK, 1), lambda h, ci: (h, rev(ci), 0)),
                  pl.BlockSpec((1, 1, SSD_CHUNK), lambda h, ci: (h, 0, rev(ci))),
                  scal, scal, scal,
                  pl.BlockSpec((1, 1, LANES, LANES), lambda h, ci: (h, rev(ci), 0, 0)),
                  pl.BlockSpec((SSD_CHUNK, LANES), hc)],
        out_specs=[pl.BlockSpec((SSD_CHUNK, LANES), hc), per_head, per_head,
                   pl.BlockSpec((1, SSD_CHUNK, 1), lambda h, ci: (h, rev(ci), 0)),
                   pacc, pacc, pacc],
        out_shape=[jax.ShapeDtypeStruct((s, N_HEADS * LANES), F32),
                   jax.ShapeDtypeStruct((N_HEADS, s, LANES), F32),
                   jax.ShapeDtypeStruct((N_HEADS, s, LANES), F32),
                   jax.ShapeDtypeStruct((N_HEADS, s, 1), F32),
                   jax.ShapeDtypeStruct((N_HEADS, 1, LANES), F32),
                   jax.ShapeDtypeStruct((N_HEADS, 1, LANES), F32),
                   jax.ShapeDtypeStruct((N_HEADS, 1, LANES), F32)],
        scratch_shapes=[pltpu.VMEM((LANES, LANES), F32)],
        compiler_params=_cparams(("parallel", "arbitrary")),
    )(xbc, xbc, xbc, dtcol, dtrow, bias, a_log, dskip, states, dy)


def _att_tile(s):
    return _pick(s, (512, 256, 128))


def _tri(t, transposed=False):
    r = lax.broadcasted_iota(jnp.int32, (t, t), 0)
    c = lax.broadcasted_iota(jnp.int32, (t, t), 1)
    return (r <= c) if transposed else (c <= r)


def _rows_at(ref, blk, t):
    return ref[pl.ds(pl.multiple_of(blk * t, t), t), :]


def _flash_fwd(q, k, v, *, name):
    s = q.shape[0]
    t = _att_tile(s)
    nq = s // t

    def body(q_ref, k_ref, v_ref, o_ref, lse_ref):
        i = pl.program_id(1)
        qv = q_ref[...]

        def step(j, carry, diagonal):
            m_old, l_old, acc = carry
            sc = _dot(qv, _rows_at(k_ref, j, t), tb=True)
            if diagonal:
                sc = jnp.where(_tri(t), sc, -jnp.inf)
            m_new = jnp.maximum(m_old, jnp.max(sc, axis=1, keepdims=True))
            alpha = jnp.exp(m_old - m_new)
            p = jnp.exp(sc - m_new)
            return (m_new, alpha * l_old + jnp.sum(p, axis=1, keepdims=True),
                    alpha * acc + _dot(p, _rows_at(v_ref, j, t)))

        init = (jnp.full((t, 1), -jnp.inf, F32), jnp.zeros((t, 1), F32), jnp.zeros((t, LANES), F32))
        carry = lax.fori_loop(0, i, lambda j, c: step(j, c, False), init)
        m_fin, l_fin, acc = step(i, carry, True)
        o_ref[...] = (acc / l_fin).astype(o_ref.dtype)
        lse_ref[0] = m_fin + jnp.log(l_fin)

    q_spec = pl.BlockSpec((t, LANES), lambda h, i: (i, h))
    kv_spec = pl.BlockSpec((s, LANES), lambda h, i: (0, h))
    return pl.pallas_call(
        body, name=name, grid=(N_HEADS, nq),
        in_specs=[q_spec, kv_spec, kv_spec],
        out_specs=[q_spec, pl.BlockSpec((1, t, 1), lambda h, i: (h, i, 0))],
        out_shape=[jax.ShapeDtypeStruct(q.shape, BF16), jax.ShapeDtypeStruct((N_HEADS, s, 1), F32)],
        compiler_params=_cparams(("parallel", "arbitrary")),
    )(q, k, v)


def _flash_bwd_dq(q, k, v, o, do, lse, *, name):
    s = q.shape[0]
    t = _att_tile(s)
    nq = s // t

    def body(q_ref, k_ref, v_ref, o_ref, do_ref, lse_ref, dq_ref, dl_ref):
        i = pl.program_id(1)
        qv, dov, lse = q_ref[...], do_ref[...], lse_ref[0]
        delta = jnp.sum(dov.astype(F32) * o_ref[...].astype(F32), axis=1, keepdims=True)
        dl_ref[0] = delta

        def step(j, acc, diagonal):
            kj = _rows_at(k_ref, j, t)
            p = jnp.exp(_dot(qv, kj, tb=True) - lse)
            if diagonal:
                p = jnp.where(_tri(t), p, 0.0)
            ds = p * (_dot(dov, _rows_at(v_ref, j, t), tb=True) - delta)
            return acc + _dot(ds, kj)

        acc = lax.fori_loop(0, i, lambda j, c: step(j, c, False), jnp.zeros((t, LANES), F32))
        dq_ref[...] = step(i, acc, True) * ATT_SCALE

    q_spec = pl.BlockSpec((t, LANES), lambda h, i: (i, h))
    kv_spec = pl.BlockSpec((s, LANES), lambda h, i: (0, h))
    col_spec = pl.BlockSpec((1, t, 1), lambda h, i: (h, i, 0))
    return pl.pallas_call(
        body, name=name, grid=(N_HEADS, nq),
        in_specs=[q_spec, kv_spec, kv_spec, q_spec, q_spec, col_spec],
        out_specs=[q_spec, col_spec],
        out_shape=[jax.ShapeDtypeStruct(q.shape, F32), jax.ShapeDtypeStruct((N_HEADS, s, 1), F32)],
        compiler_params=_cparams(("parallel", "arbitrary")),
    )(q, k, v, o, do, lse)


def _flash_bwd_dkv(q, k, v, do, lse_row, delta_row, *, name):
    s = q.shape[0]
    t = _att_tile(s)
    nq = s // t

    def body(q_ref, k_ref, v_ref, do_ref, lse_ref, dl_ref, dk_ref, dv_ref):
        j = pl.program_id(1)
        kv, vv = k_ref[...], v_ref[...]

        def step(i, carry, diagonal):
            dk, dv = carry
            qi, doi = _rows_at(q_ref, i, t), _rows_at(do_ref, i, t)
            cols = pl.ds(pl.multiple_of(i * t, t), t)
            p_t = jnp.exp(_dot(kv, qi, tb=True) - lse_ref[0, :, cols])
            if diagonal:
                p_t = jnp.where(_tri(t, transposed=True), p_t, 0.0)
            ds_t = p_t * (_dot(vv, doi, tb=True) - dl_ref[0, :, cols])
            return dk + _dot(ds_t, qi), dv + _dot(p_t, doi)

        zero = jnp.zeros((t, LANES), F32)
        carry = step(j, (zero, zero), True)
        dk, dv = lax.fori_loop(j + 1, nq, lambda i, c: step(i, c, False), carry)
        dk_ref[...] = dk
        dv_ref[...] = dv

    q_spec = pl.BlockSpec((s, LANES), lambda h, j: (0, h))
    kv_spec = pl.BlockSpec((t, LANES), lambda h, j: (j, h))
    row_spec = pl.BlockSpec((1, 1, s), lambda h, j: (h, 0, 0))
    return pl.pallas_call(
        body, name=name, grid=(N_HEADS, nq),
        in_specs=[q_spec, kv_spec, kv_spec, q_spec, row_spec, row_spec],
        out_specs=[kv_spec, kv_spec],
        out_shape=[jax.ShapeDtypeStruct(q.shape, F32)] * 2,
        compiler_params=_cparams(("parallel", "arbitrary")),
    )(q, k, v, do, lse_row, delta_row)


def _rope(v, cos_t, sin_p, sin_m):
    return v * cos_t + pltpu.roll(v, QK_ROPE // 2, 1) * sin_p + pltpu.roll(v, LANES - QK_ROPE // 2, 1) * sin_m


def _rope_t(d, cos_t, sin_p, sin_m):
    return d * cos_t + pltpu.roll(d * sin_p, LANES - QK_ROPE // 2, 1) + pltpu.roll(d * sin_m, QK_ROPE // 2, 1)


def _att_prep(q_pad, kv2, kr, cos_t, sin_p, sin_m, *, name):
    w = N_HEADS * LANES

    def fn(qv, kvv, krv, c, sp, sm):
        kr_rot = _rope(krv, c, sp, sm)
        qs, ks = [], []
        for h in range(N_HEADS):
            blk = slice(h * LANES, (h + 1) * LANES)
            qs.append(_rope(qv[:, blk], c, sp, sm) * ATT_SCALE)
            ks.append(kvv[:, blk] + kr_rot)
        return jnp.concatenate(qs, axis=1), jnp.concatenate(ks, axis=1), kvv[:, w:]

    return _rowwise(fn, [q_pad, kv2, kr, cos_t, sin_p, sin_m], [],
                    [(w, BF16, "row"), (w, BF16, "row"), (w, BF16, "row")], name=name)


def _att_prep_bwd(dq, dk, cos_t, sin_p, sin_m, *, name):
    w = N_HEADS * LANES

    def fn(dqv, dkv, c, sp, sm):
        outs, dkr = [], None
        for h in range(N_HEADS):
            blk = slice(h * LANES, (h + 1) * LANES)
            outs.append(_rope_t(dqv[:, blk], c, sp, sm))
            dkr = dkv[:, blk] if dkr is None else dkr + dkv[:, blk]
        return jnp.concatenate(outs, axis=1), _rope_t(dkr, c, sp, sm)

    return _rowwise(fn, [dq, dk, cos_t, sin_p, sin_m], [], [(w, BF16, "row"), (LANES, F32, "row")], name=name)


_ANY = pl.BlockSpec(memory_space=pl.ANY)
_MESH = pl.DeviceIdType.MESH


def _mesh_pos():
    return lax.axis_index("x"), lax.axis_index("y"), lax.axis_index("c")


def _remote(src, dst, send_sem, recv_sem, dev):
    return pltpu.make_async_remote_copy(src_ref=src, dst_ref=dst, send_sem=send_sem, recv_sem=recv_sem,
                                        device_id=dev, device_id_type=_MESH)


def _gather_chips(shard):
    r, w = shard.shape
    half = r // 2

    def body(x_ref, out_ref, send_sems, recv_sems, local_sem):
        x, y, c = _mesh_pos()
        k = 2 * x + y
        sibling = (x, y, 1 - c)
        chips = [(1 - x, y), (x, 1 - y), (1 - x, 1 - y)]

        def blk(chip, hf):
            return out_ref.at[2 * chip[0] + chip[1], pl.ds(hf * half, half), :]

        mine = pltpu.make_async_copy(x_ref, out_ref.at[k], local_sem)
        mine.start()
        first = [_remote(x_ref.at[pl.ds(c * half, half), :], blk((x, y), c), send_sems.at[j], recv_sems.at[j],
                         (*chip, c)) for j, chip in enumerate(chips)]
        for cp in first:
            cp.start()
        passed = [_remote(blk(chip, c), blk(chip, c), send_sems.at[3 + j], recv_sems.at[3 + j], sibling)
                  for j, chip in enumerate(chips)]
        for j, chip in enumerate(chips):
            _remote(blk(chip, c), blk(chip, c), send_sems.at[j], recv_sems.at[j], sibling).wait_recv()
            passed[j].start()
        for j, chip in enumerate(chips):
            _remote(blk(chip, 1 - c), blk(chip, 1 - c), send_sems.at[3 + j], recv_sems.at[3 + j], sibling).wait_recv()
        for cp in first + passed:
            cp.wait_send()
        mine.wait()

    return pl.pallas_call(
        body, name="gather_chips", in_specs=[_ANY], out_specs=_ANY,
        out_shape=jax.ShapeDtypeStruct((4, r, w), shard.dtype),
        scratch_shapes=[pltpu.SemaphoreType.DMA((6,)), pltpu.SemaphoreType.DMA((6,)), pltpu.SemaphoreType.DMA],
    )(shard)


def _swap_halves(g):
    _, r, w = g.shape
    half = r // 2

    def body(g_ref, out_ref, send_sem, recv_sem):
        x, y, c = _mesh_pos()
        cp = _remote(g_ref.at[:, pl.ds((1 - c) * half, half), :], out_ref, send_sem, recv_sem, (x, y, 1 - c))
        cp.start()
        cp.wait()

    return pl.pallas_call(
        body, name="swap_halves", in_specs=[_ANY], out_specs=_ANY,
        out_shape=jax.ShapeDtypeStruct((4, half, w), g.dtype),
        scratch_shapes=[pltpu.SemaphoreType.DMA, pltpu.SemaphoreType.DMA],
    )(g)


def _scatter_chips(part):
    _, hrows, w = part.shape

    def body(p_ref, out_ref, send_sems, recv_sems, local_sem):
        x, y, c = _mesh_pos()
        k = 2 * x + y
        chips = [(1 - x, y), (x, 1 - y), (1 - x, 1 - y)]
        mine = pltpu.make_async_copy(p_ref.at[k], out_ref.at[k], local_sem)
        mine.start()
        cps = [_remote(p_ref.at[2 * chip[0] + chip[1]], out_ref.at[k], send_sems.at[j], recv_sems.at[j], (*chip, c))
               for j, chip in enumerate(chips)]
        for cp in cps:
            cp.start()
        for j, chip in enumerate(chips):
            slot = out_ref.at[2 * chip[0] + chip[1]]
            _remote(slot, slot, send_sems.at[j], recv_sems.at[j], (*chip, c)).wait_recv()
        for cp in cps:
            cp.wait_send()
        mine.wait()

    return pl.pallas_call(
        body, name="scatter_chips", in_specs=[_ANY], out_specs=_ANY,
        out_shape=jax.ShapeDtypeStruct(part.shape, part.dtype),
        scratch_shapes=[pltpu.SemaphoreType.DMA((3,)), pltpu.SemaphoreType.DMA((3,)), pltpu.SemaphoreType.DMA],
    )(part)


def _join_halves(mine_half):
    hrows, w = mine_half.shape

    def body(h_ref, out_ref, send_sem, recv_sem, local_sem):
        x, y, c = _mesh_pos()
        mine = pltpu.make_async_copy(h_ref, out_ref.at[pl.ds(c * hrows, hrows), :], local_sem)
        mine.start()
        cp = _remote(h_ref, out_ref.at[pl.ds(c * hrows, hrows), :], send_sem, recv_sem, (x, y, 1 - c))
        cp.start()
        other = out_ref.at[pl.ds((1 - c) * hrows, hrows), :]
        _remote(other, other, send_sem, recv_sem, (x, y, 1 - c)).wait_recv()
        cp.wait_send()
        mine.wait()

    return pl.pallas_call(
        body, name="join_halves", in_specs=[_ANY], out_specs=_ANY,
        out_shape=jax.ShapeDtypeStruct((2 * hrows, w), mine_half.dtype),
        scratch_shapes=[pltpu.SemaphoreType.DMA, pltpu.SemaphoreType.DMA, pltpu.SemaphoreType.DMA],
    )(mine_half)


def _gather_all(vec, *, name):
    r, w = vec.shape

    def body(v_ref, out_ref, send_sems, recv_sems, local_sem):
        x, y, c = _mesh_pos()

        def slot(px, py, pc):
            return out_ref.at[4 * px + 2 * py + pc]

        mine = pltpu.make_async_copy(v_ref, slot(x, y, c), local_sem)
        mine.start()
        peers = []
        for rel in range(1, 8):
            fx, fy, fc = (rel >> 2) & 1, (rel >> 1) & 1, rel & 1
            peers.append((x ^ fx, y ^ fy, c ^ fc))
        cps = [_remote(v_ref, slot(x, y, c), send_sems.at[j], recv_sems.at[j], peer) for j, peer in enumerate(peers)]
        for cp in cps:
            cp.start()
        for j, peer in enumerate(peers):
            _remote(slot(*peer), slot(*peer), send_sems.at[j], recv_sems.at[j], peer).wait_recv()
        for cp in cps:
            cp.wait_send()
        mine.wait()

    return pl.pallas_call(
        body, name=name, in_specs=[_ANY], out_specs=_ANY,
        out_shape=jax.ShapeDtypeStruct((8, r, w), vec.dtype),
        scratch_shapes=[pltpu.SemaphoreType.DMA((7,)), pltpu.SemaphoreType.DMA((7,)), pltpu.SemaphoreType.DMA],
    )(vec)


def _add_half(g, got, c):
    nb, r, w = g.shape
    half = r // 2
    tm = _pick(half, (256, 128, 64, 32, 16, 8))
    per = half // tm

    def body(c_ref, g_ref, o_ref, out_ref):
        out_ref[...] = g_ref[...] + o_ref[...]

    return pl.pallas_call(
        body, name="add_half",
        grid_spec=pltpu.PrefetchScalarGridSpec(
            num_scalar_prefetch=1, grid=(nb, per),
            in_specs=[pl.BlockSpec((1, tm, w), lambda b, i, c_ref: (b, c_ref[0] * per + i, 0)),
                      pl.BlockSpec((1, tm, w), lambda b, i, c_ref: (b, i, 0))],
            out_specs=pl.BlockSpec((1, tm, w), lambda b, i, c_ref: (b, i, 0))),
        out_shape=jax.ShapeDtypeStruct((nb, half, w), g.dtype),
        compiler_params=_cparams(("parallel", "parallel")),
    )(jnp.reshape(c, (1,)).astype(jnp.int32), g, got)


def _sum_slots(stack, *, name):
    n, r, w = stack.shape
    tm = _pick(r, (256, 128, 64, 32, 16, 8))

    def body(s_ref, out_ref):
        acc = s_ref[0]
        for i in range(1, n):
            acc = acc + s_ref[i]
        out_ref[...] = acc

    return pl.pallas_call(
        body, name=name, grid=(r // tm,),
        in_specs=[pl.BlockSpec((n, tm, w), lambda i: (0, i, 0))],
        out_specs=pl.BlockSpec((tm, w), lambda i: (i, 0)),
        out_shape=jax.ShapeDtypeStruct((r, w), stack.dtype),
        compiler_params=_cparams(("parallel",)),
    )(stack)


def _adamw(w, g, m, v, *, name):
    shape = w.shape
    cols = shape[-1]
    flat = lambda t: t.reshape(-1, cols)

    def fn(wv, gv, mv, vv):
        m_new = ADAM_B1 * mv + (1.0 - ADAM_B1) * gv
        v_new = ADAM_B2 * vv + (1.0 - ADAM_B2) * (gv * gv)
        m_hat = m_new / (1.0 - ADAM_B1 ** ADAM_STEP)
        v_hat = v_new / (1.0 - ADAM_B2 ** ADAM_STEP)
        delta = -ADAM_LR * (m_hat / (jnp.sqrt(v_hat) + ADAM_EPS) + ADAM_WD * wv)
        return delta, m_new, v_new

    rows = flat(w).shape[0]
    tm = _pick(rows, (256, 128, 64, 32, 16, 8))
    outs = _rowwise(fn, [flat(w), flat(g), flat(m), flat(v)], [], [(cols, F32, "row")] * 3, name=name, tm=tm)
    return tuple(o.reshape(shape) for o in outs)


def _pad_blocks(w, axis, n_blocks, real, to=LANES, offset=0):
    axis = axis % w.ndim
    shp = w.shape
    w = w.reshape(shp[:axis] + (n_blocks, real) + shp[axis + 1:])
    pads = [(0, 0)] * w.ndim
    pads[axis + 1] = (offset, to - real - offset)
    w = jnp.pad(w, pads)
    return w.reshape(shp[:axis] + (n_blocks * to,) + shp[axis + 1:])


def _unpad_blocks(w, axis, n_blocks, real, to=LANES, offset=0):
    axis = axis % w.ndim
    shp = w.shape
    w = w.reshape(shp[:axis] + (n_blocks, to) + shp[axis + 1:])
    w = lax.slice_in_dim(w, offset, offset + real, axis=axis + 1)
    return w.reshape(shp[:axis] + (n_blocks * real,) + shp[axis + 1:])


def _block_diag(w):
    n, a, b = w.shape
    eye = jnp.eye(n, dtype=w.dtype)
    return (eye[:, None, :, None] * w[:, :, None, :]).reshape(n * a, n * b)


def _block_diag_t(d, n):
    a, b = d.shape[0] // n, d.shape[1] // n
    d = d.reshape(n, a, n, b)
    return jnp.stack([d[i, :, i, :] for i in range(n)])


_SPLITS = np.cumsum((0,) + SPLIT_SIZES)


def _w_in_groups(w_in):
    sl = lambda i: w_in[:, _SPLITS[i]:_SPLITS[i + 1]]
    xbc = sl(5)
    xbc_pad = jnp.concatenate([_pad_blocks(xbc[:, :MIX], 1, N_HEADS, HEAD),
                               _pad_blocks(xbc[:, MIX:MIX + 2 * HEAD], 1, 2, HEAD),
                               _pad_blocks(xbc[:, MIX + 2 * HEAD:], 1, 2, HEAD)], axis=1)
    return dict(
        cq=sl(0), ckv=sl(1), kr=_pad_blocks(sl(2), 1, 1, QK_ROPE, offset=HEAD), pool=sl(3),
        z=_pad_blocks(sl(4), 1, N_HEADS, HEAD), xbc=xbc_pad, dt=_pad_blocks(sl(6), 1, 1, N_HEADS),
        lru_g=sl(7), lru_x=sl(8), gates=sl(9))


def _w_in_ungroup(d):
    xbc = d["xbc"]
    w = N_HEADS * LANES
    xbc_real = jnp.concatenate([_unpad_blocks(xbc[:, :w], 1, N_HEADS, HEAD),
                                _unpad_blocks(xbc[:, w:w + 2 * LANES], 1, 2, HEAD),
                                _unpad_blocks(xbc[:, w + 2 * LANES:], 1, 2, HEAD)], axis=1)
    return jnp.concatenate([d["cq"], d["ckv"], _unpad_blocks(d["kr"], 1, 1, QK_ROPE, offset=HEAD), d["pool"],
                            _unpad_blocks(d["z"], 1, N_HEADS, HEAD), xbc_real, _unpad_blocks(d["dt"], 1, 1, N_HEADS),
                            d["lru_g"], d["lru_x"], d["gates"]], axis=1)


def _pad_xbc_vec(v):
    return jnp.concatenate([_pad_blocks(v[..., :MIX], -1, N_HEADS, HEAD),
                            _pad_blocks(v[..., MIX:MIX + 2 * HEAD], -1, 2, HEAD),
                            _pad_blocks(v[..., MIX + 2 * HEAD:], -1, 2, HEAD)], axis=-1)


def _unpad_xbc_vec(v):
    w = N_HEADS * LANES
    return jnp.concatenate([_unpad_blocks(v[..., :w], -1, N_HEADS, HEAD),
                            _unpad_blocks(v[..., w:w + 2 * LANES], -1, 2, HEAD),
                            _unpad_blocks(v[..., w + 2 * LANES:], -1, 2, HEAD)], axis=-1)


def _layer_weights(p):
    q = dict(p)
    q["in"] = _w_in_groups(p["w_in"])
    q["uq"] = _pad_blocks(p["w_uq"], 1, N_HEADS, HEAD + QK_ROPE)
    ukv = p["w_ukv"].reshape(KV_LORA, N_HEADS, 2 * HEAD)
    q["ukv"] = jnp.concatenate([_pad_blocks(ukv[:, :, :HEAD].reshape(KV_LORA, -1), 1, N_HEADS, HEAD),
                                _pad_blocks(ukv[:, :, HEAD:].reshape(KV_LORA, -1), 1, N_HEADS, HEAD)], axis=1)
    q["pool_bd"] = _block_diag(p["w_pool"])
    q["lru_bd"] = jnp.concatenate([_block_diag(p["lru_w_a"]), _block_diag(p["lru_w_i"])], axis=1)
    q["br"] = [_pad_blocks(p["w_branch"][0], 0, N_HEADS, HEAD), p["w_branch"][1],
               _pad_blocks(p["w_branch"][2], 0, N_HEADS, HEAD), p["w_branch"][3]]
    q["ssd_conv_w_pad"] = _pad_xbc_vec(p["ssd_conv_w"])
    q["ssd_conv_b_pad"] = _pad_xbc_vec(p["ssd_conv_b"])[None, :]
    q["ssd_norm_pad"] = _pad_blocks(p["ssd_norm"], 0, N_HEADS, HEAD)[None, :]
    return q


def _row(v):
    return v.reshape(1, -1)


def _scal3(v):
    return v.reshape(N_HEADS, 1, 1)


def _layer_fwd(x, p_emb, w, rope, tag):
    n = lambda s: f"{s}_{tag}"
    sv = {"x": x}
    h = _rms_fwd(x, _row(w["g_mix"]), name=n("rms_mix"))
    sv["h"] = h
    u = {k: _mm(h, wk, name=n("in_" + k)) for k, wk in w["in"].items()}
    sv["u"] = u

    cqn = _rms_fwd(u["cq"], _row(w["q_norm"]), name=n("rms_q"))
    ckvn = _rms_fwd(u["ckv"], _row(w["kv_norm"]), name=n("rms_kv"))
    q_pad = _mm(cqn, w["uq"], name=n("uq"))
    kv2 = _mm(ckvn, w["ukv"], name=n("ukv"))
    qc, kc, vc = _att_prep(q_pad, kv2, u["kr"], *rope, name=n("att_prep"))
    y_a, lse = _flash_fwd(qc, kc, vc, name=n("flash_fwd"))
    sv.update(cqn=cqn, ckvn=ckvn, qc=qc, kc=kc, vc=vc, y_a=y_a, lse=lse)

    pool_d = _pool_fwd(u["pool"], name=n("pool_fwd"))
    yb_pre, y_b = _mm(pool_d, w["pool_bd"], epilogue=lambda acc, sc: (acc, acc * sc),
                      rowvecs=[_row(w["pool_scale"])], out_dtypes=(F32, BF16), name=n("pool_mm"))
    sv.update(pool_d=pool_d, yb_pre=yb_pre, y_b=y_b)

    xbc_c = _conv_fwd(u["xbc"], w["ssd_conv_w_pad"], w["ssd_conv_b_pad"], silu=True, name=n("ssd_conv"))
    dt8 = u["dt"][:, :N_HEADS]
    dtcol = dt8.T[:, :, None]
    dtrow = dt8.T[:, None, :]
    ssd_par = (_scal3(w["ssd_dt_bias"]), _scal3(w["ssd_a_log"]), _scal3(w["ssd_d"]))
    y_ssd, states = _ssd_fwd(xbc_c, dtcol, dtrow, *ssd_par, name=n("ssd_fwd"))

    def ssd_post(yv, zv, gv):
        xh, _ = _rms_parts(yv * _silu(zv), MIX)
        return xh * gv

    y_c = _rowwise(ssd_post, [y_ssd, u["z"]], [w["ssd_norm_pad"]], [(N_HEADS * LANES, BF16, "row")], name=n("ssd_post"))
    sv.update(xbc_c=xbc_c, dtcol=dtcol, dtrow=dtrow, y_ssd=y_ssd, states=states, y_c=y_c)

    xc = _conv_fwd(u["lru_x"], w["lru_conv_w"], _row(w["lru_conv_b"]), silu=False, name=n("lru_conv"))
    pre = _mm(xc, w["lru_bd"], name=n("lru_mm"))
    lru_par = (_row(w["lru_lambda"]), _row(w["lru_b_a"]), _row(w["lru_b_i"]))
    y_d, h_lru = _lru_fwd(pre, xc, u["lru_g"], *lru_par, name=n("lru_fwd"))
    sv.update(xc=xc, pre=pre, h_lru=h_lru, y_d=y_d)

    ys = [y_a, y_b, y_c, y_d]
    merged, ybs = None, []
    for b in range(4):
        gate = u["gates"][:, b * D_MODEL:(b + 1) * D_MODEL]
        if merged is None:
            merged, yb = _mm(ys[b], w["br"][b], epilogue=lambda acc, gt: (_sigmoid(gt) * acc, acc), tiles=[gate],
                             out_dtypes=(F32, F32), name=n(f"branch{b}"))
        else:
            merged, yb = _mm(ys[b], w["br"][b], epilogue=lambda acc, gt, mg: (mg + _sigmoid(gt) * acc, acc),
                             tiles=[gate, merged], out_dtypes=(F32, F32), name=n(f"branch{b}"))
        ybs.append(yb)
    x1 = _mm(merged, w["w_out"], epilogue=lambda acc, xr: (acc + xr,), tiles=[x], name=n("out_proj"))
    sv.update(ybs=ybs, merged=merged, x1=x1)

    h2 = _rms_fwd(x1, _row(w["g_mlp"]), name=n("rms_mlp"))
    a_ff, f_ff = _mm(h2, w["w_ff1"], epilogue=lambda acc: (acc, jnp.square(jnp.maximum(acc, 0.0))),
                     out_dtypes=(F32, BF16), name=n("ff1"))
    x2 = _mm(f_ff, w["w_ff2"], epilogue=lambda acc, xr: (acc + xr,), tiles=[x1], name=n("ff2"))
    sv.update(h2=h2, a_ff=a_ff, f_ff=f_ff, x2=x2)

    h3 = _rms_fwd(x2, _row(w["g_ple"]), name=n("rms_ple"))
    e_ple = _mm(p_emb, w["w_ple"], name=n("ple_emb"))
    x3, gt_ple = _mm(h3, w["w_ple_gate"], epilogue=lambda acc, ev, xr: (xr + ev * _sigmoid(acc), _sigmoid(acc)),
                     tiles=[e_ple, x2], out_dtypes=(F32, F32), name=n("ple_gate"))
    sv.update(h3=h3, e_ple=e_ple, gt_ple=gt_ple, p_emb=p_emb)
    return x3, sv


def _layer_bwd(dx3, sv, w, rope, tag):
    n = lambda s: f"{s}_{tag}"
    gr = {}
    u = sv["u"]

    de, dpre = _rowwise(lambda d, gt, ev: (d * gt, d * ev * gt * (1.0 - gt)), [dx3, sv["gt_ple"], sv["e_ple"]], [],
                        [(D_MODEL, BF16, "row"), (D_MODEL, BF16, "row")], name=n("ple_bwd"))
    gr["w_ple"] = _mm(sv["p_emb"], de, ta=True, name=n("d_w_ple"))
    gr["w_ple_gate"] = _mm(sv["h3"], dpre, ta=True, name=n("d_w_ple_gate"))
    dh3 = _mm(dpre, w["w_ple_gate"], tb=True, out_dtypes=(BF16,), name=n("d_h3"))
    dx2, dg = _rms_bwd(sv["x2"], _row(w["g_ple"]), dh3, dx3, name=n("rms_ple_bwd"))
    gr["g_ple"] = dg[0]

    gr["w_ff2"] = _mm(sv["f_ff"], dx2, ta=True, name=n("d_w_ff2"))
    da = _mm(dx2, w["w_ff2"], tb=True, epilogue=lambda acc, av: (acc * 2.0 * jnp.maximum(av, 0.0),),
             tiles=[sv["a_ff"]], out_dtypes=(BF16,), name=n("d_a_ff"))
    gr["w_ff1"] = _mm(sv["h2"], da, ta=True, name=n("d_w_ff1"))
    dh2 = _mm(da, w["w_ff1"], tb=True, out_dtypes=(BF16,), name=n("d_h2"))
    dx1, dg = _rms_bwd(sv["x1"], _row(w["g_mlp"]), dh2, dx2, name=n("rms_mlp_bwd"))
    gr["g_mlp"] = dg[0]

    gr["w_out"] = _mm(sv["merged"], dx1, ta=True, name=n("d_w_out"))
    dmerged = _mm(dx1, w["w_out"], tb=True, name=n("d_merged"))

    def merge_bwd(dm, gts, y0, y1, y2, y3):
        dys, dgs = [], []
        for b, yb in enumerate((y0, y1, y2, y3)):
            sg = _sigmoid(gts[:, b * D_MODEL:(b + 1) * D_MODEL])
            dys.append(dm * sg)
            dgs.append(dm * yb * sg * (1.0 - sg))
        return (*dys, jnp.concatenate(dgs, axis=1))

    *dybs, dgates = _rowwise(merge_bwd, [dmerged, u["gates"]] + sv["ybs"], [],
                             [(D_MODEL, BF16, "row")] * 4 + [(4 * D_MODEL, BF16, "row")], name=n("merge_bwd"))
    ys = [sv["y_a"], sv["y_b"], sv["y_c"], sv["y_d"]]
    dwb = [_mm(ys[b], dybs[b], ta=True, name=n(f"d_w_branch{b}")) for b in range(4)]
    gr["w_branch"] = jnp.stack([_unpad_blocks(dwb[0], 0, N_HEADS, HEAD), dwb[1],
                                _unpad_blocks(dwb[2], 0, N_HEADS, HEAD), dwb[3]])
    dy_a = _mm(dybs[0], w["br"][0], tb=True, out_dtypes=(BF16,), name=n("d_y_a"))
    dy_b = _mm(dybs[1], w["br"][1], tb=True, name=n("d_y_b"))
    dy_c = _mm(dybs[2], w["br"][2], tb=True, name=n("d_y_c"))
    dy_d = _mm(dybs[3], w["br"][3], tb=True, name=n("d_y_d"))
    du = {"gates": dgates}

    lru_par = (_row(w["lru_lambda"]), _row(w["lru_b_a"]), _row(w["lru_b_i"]))
    dpa, dpi, dxc_direct, du["lru_g"], dlam, dba, dbi = _lru_bwd(
        sv["pre"], sv["xc"], u["lru_g"], *lru_par, sv["h_lru"], dy_d, name=n("lru_bwd"))
    dpre_lru = jnp.concatenate([dpa, dpi], axis=1)
    d_bd = _mm(sv["xc"], dpre_lru, ta=True, name=n("d_lru_w"))
    gr["lru_w_a"] = _block_diag_t(d_bd[:, :MIX], N_HEADS)
    gr["lru_w_i"] = _block_diag_t(d_bd[:, MIX:], N_HEADS)
    gr["lru_lambda"], gr["lru_b_a"], gr["lru_b_i"] = dlam[0], dba[0], dbi[0]
    dxc = _mm(dpre_lru, w["lru_bd"], tb=True, epilogue=lambda acc, t: (acc + t,), tiles=[dxc_direct], name=n("d_xc"))
    du["lru_x"], gr["lru_conv_w"], dcb = _conv_bwd(u["lru_x"], w["lru_conv_w"], _row(w["lru_conv_b"]), dxc,
                                                  silu=False, name=n("lru_conv_bwd"))
    gr["lru_conv_b"] = dcb[0]

    def ssd_post_bwd(dyc, yv, zv, gv):
        sz = _silu(zv)
        dyz, dgain = _rms_bwd_math(yv * sz, gv, dyc, MIX)
        return dyz * sz, dyz * yv * _silu_grad(zv), dgain

    dy_ssd, du["z"], dgain = _rowwise(ssd_post_bwd, [dy_c, sv["y_ssd"], u["z"]], [w["ssd_norm_pad"]],
                                      [(N_HEADS * LANES, F32, "row"), (N_HEADS * LANES, BF16, "row"),
                                       (N_HEADS * LANES, F32, "acc")], name=n("ssd_post_bwd"))
    gr["ssd_norm"] = _unpad_blocks(dgain[0], 0, N_HEADS, HEAD)
    ssd_par = (_scal3(w["ssd_dt_bias"]), _scal3(w["ssd_a_log"]), _scal3(w["ssd_d"]))
    dxs, dbh, dch, ddt, dbias, dalog, dd = _ssd_bwd(sv["xbc_c"], sv["dtcol"], sv["dtrow"], *ssd_par, sv["states"],
                                                    dy_ssd, name=n("ssd_bwd"))
    s = dxs.shape[0]
    group_sum = lambda t: t.reshape(2, 4, s, LANES).sum(axis=1).transpose(1, 0, 2).reshape(s, 2 * LANES)
    dxbc_c = jnp.concatenate([dxs, group_sum(dbh), group_sum(dch)], axis=1)
    gr["ssd_dt_bias"], gr["ssd_a_log"], gr["ssd_d"] = dbias[:, 0, 0], dalog[:, 0, 0], dd[:, 0, 0]
    du["xbc"], dcw, dcb = _conv_bwd(u["xbc"], w["ssd_conv_w_pad"], w["ssd_conv_b_pad"], dxbc_c, silu=True,
                                    name=n("ssd_conv_bwd"))
    gr["ssd_conv_w"], gr["ssd_conv_b"] = _unpad_xbc_vec(dcw), _unpad_xbc_vec(dcb[0])
    du["dt"] = jnp.pad(ddt[:, :, 0].T, ((0, 0), (0, LANES - N_HEADS)))

    dyb_pre, dscale = _rowwise(lambda d, yp, sc: (d * sc, _colsum(d * yp)), [dy_b, sv["yb_pre"]],
                               [_row(w["pool_scale"])], [(MIX, BF16, "row"), (MIX, F32, "acc")], name=n("pool_scale_bwd"))
    gr["pool_scale"] = dscale[0]
    gr["w_pool"] = _block_diag_t(_mm(sv["pool_d"], dyb_pre, ta=True, name=n("d_w_pool")), 4)
    dd_pool = _mm(dyb_pre, w["pool_bd"], tb=True, name=n("d_pool_d"))
    du["pool"] = _pool_bwd(dd_pool, name=n("pool_bwd"))

    dqc, delta = _flash_bwd_dq(sv["qc"], sv["kc"], sv["vc"], sv["y_a"], dy_a, sv["lse"], name=n("flash_dq"))
    to_row = lambda t: t.reshape(N_HEADS, 1, s)
    dkc, dvc = _flash_bwd_dkv(sv["qc"], sv["kc"], sv["vc"], dy_a, to_row(sv["lse"]), to_row(delta), name=n("flash_dkv"))
    dq_pad, du["kr"] = _att_prep_bwd(dqc, dkc, *rope, name=n("att_prep_bwd"))
    d_uq = _mm(sv["cqn"], dq_pad, ta=True, name=n("d_w_uq"))
    gr["w_uq"] = _unpad_blocks(d_uq, 1, N_HEADS, HEAD + QK_ROPE)
    dcqn = _mm(dq_pad, w["uq"], tb=True, out_dtypes=(BF16,), name=n("d_cqn"))
    du["cq"], dg = _rms_bwd(u["cq"], _row(w["q_norm"]), dcqn, name=n("rms_q_bwd"))
    gr["q_norm"] = dg[0]
    dkv2 = jnp.concatenate([dkc, dvc], axis=1).astype(BF16)
    d_ukv = _mm(sv["ckvn"], dkv2, ta=True, name=n("d_w_ukv"))
    wk = N_HEADS * LANES
    dk_real = _unpad_blocks(d_ukv[:, :wk], 1, N_HEADS, HEAD).reshape(KV_LORA, N_HEADS, HEAD)
    dv_real = _unpad_blocks(d_ukv[:, wk:], 1, N_HEADS, HEAD).reshape(KV_LORA, N_HEADS, HEAD)
    gr["w_ukv"] = jnp.concatenate([dk_real, dv_real], axis=2).reshape(KV_LORA, N_HEADS * 2 * HEAD)
    dckvn = _mm(dkv2, w["ukv"], tb=True, out_dtypes=(BF16,), name=n("d_ckvn"))
    du["ckv"], dg = _rms_bwd(u["ckv"], _row(w["kv_norm"]), dckvn, name=n("rms_kv_bwd"))
    gr["kv_norm"] = dg[0]

    dw_in, dh = {}, None
    for k, wk_ in w["in"].items():
        dw_in[k] = _mm(sv["h"], du[k], ta=True, name=n("d_w_in_" + k))
        if dh is None:
            dh = _mm(du[k], wk_, tb=True, name=n("d_h_" + k))
        else:
            dh = _mm(du[k], wk_, tb=True, epilogue=lambda acc, t: (acc + t,), tiles=[dh], name=n("d_h_" + k))
    gr["w_in"] = _w_in_ungroup(dw_in)
    dx, dg = _rms_bwd(sv["x"], _row(w["g_mix"]), dh, dx1, name=n("rms_mix_bwd"))
    gr["g_mix"] = dg[0]
    return dx, gr


def _pack_rows(n_elems):
    per = PACK_W * PACK_ROWS
    return -(-n_elems // per) * PACK_ROWS


def _pack_flat(parts, dtype):
    flat = jnp.concatenate([p.reshape(-1).astype(dtype) for p in parts])
    rows = _pack_rows(flat.shape[0])
    return jnp.pad(flat, (0, rows * PACK_W - flat.shape[0])).reshape(rows, PACK_W)


def _unpack_flat(buf, shapes):
    lead = buf.shape[:-2]
    flat = buf.reshape(lead + (-1,))
    out, off = [], 0
    for shp in shapes:
        size = int(np.prod(shp))
        out.append(flat[..., off:off + size].reshape(lead + tuple(shp)))
        off += size
    return out


def _merge_shards(t, axis):
    return jnp.concatenate([t[i] for i in range(4)], axis=axis)


def _split_shards(t, axis):
    return jnp.stack(jnp.split(t, 4, axis=axis))


def _rope_tables(positions):
    inv = 1.0 / (ROPE_THETA ** (jnp.arange(0, QK_ROPE, 2, dtype=F32) / QK_ROPE))
    ang = positions.astype(F32)[:, None] * inv
    cos, sin = jnp.cos(ang), jnp.sin(ang)
    s = ang.shape[0]
    half = QK_ROPE // 2
    z = lambda n_: jnp.zeros((s, n_), F32)
    cos_t = jnp.concatenate([jnp.ones((s, HEAD), F32), cos, cos, jnp.ones((s, LANES - HEAD - QK_ROPE), F32)], axis=1)
    sin_p = jnp.concatenate([z(HEAD + half), sin, z(LANES - HEAD - QK_ROPE)], axis=1)
    sin_m = jnp.concatenate([z(HEAD), -sin, z(half + LANES - HEAD - QK_ROPE)], axis=1)
    return cos_t, sin_p, sin_m


def _loss_head(x, g, target, *, name):
    d = x.shape[1]

    def fn(xv, tv, gv):
        xh, r = _rms_parts(xv, d)
        y = xh * gv
        err = y - tv
        dy = err * (1.0 / d)
        dxh = dy * gv
        dx = r * (dxh - xh * (jnp.sum(dxh * xh, axis=-1, keepdims=True) * (1.0 / d)))
        return dx, _colsum(dy * xh), _colsum(err * err) * (0.5 / d)

    return _rowwise(fn, [x, target], [g], [(d, F32, "row"), (d, F32, "acc"), (d, F32, "acc")], name=name)


def _step(args):
    x = args["x"][0]
    s = x.shape[0]
    c_idx = lax.axis_index("c")

    big_shapes = [args[nm].shape for nm, _ in BIG]
    mats = [(nm, ax) for nm, ax in BIG if nm not in CONV_SHARDED]
    gathered = _gather_chips(_pack_flat([args[nm] for nm, _ in mats], BF16))
    full = {}
    for (nm, ax), t in zip(mats, _unpack_flat(gathered, [args[nm].shape for nm, _ in mats])):
        full[nm] = _merge_shards(t, ax)
    convs = [(nm, ax) for nm, ax in BIG if nm in CONV_SHARDED]
    conv_all = _gather_all(_pack_flat([args[nm] for nm, _ in convs], F32), name="gather_conv_taps")[0::2]
    for (nm, ax), t in zip(convs, _unpack_flat(conv_all, [args[nm].shape for nm, _ in convs])):
        full[nm] = _merge_shards(t, ax)
    rope = _rope_tables(args["positions"][0])

    layers = []
    for l in range(2):
        p = {nm: full[nm][l] for nm, _ in BIG}
        p.update({nm: args[nm][l] for nm in SMALL if nm != "g_final"})
        layers.append(_layer_weights(p))

    saved = []
    for l in range(2):
        x, sv = _layer_fwd(x, args["p"][l, 0], layers[l], rope, f"l{l}")
        saved.append(sv)

    dx, dg_final, loss_part = _loss_head(x, _row(args["g_final"]), args["loss_target"][0], name="loss_head")
    loss = lax.psum(jnp.sum(loss_part), ("x", "y", "c"))

    grads = [None, None]
    for l in (1, 0):
        dx, grads[l] = _layer_bwd(dx, saved[l], layers[l], rope, f"l{l}")

    g_local = {nm: jnp.stack([grads[0][nm], grads[1][nm]]) for nm in WEIGHTS if nm != "g_final"}
    g_local["g_final"] = dg_final[0]

    packed = jnp.stack([_pack_flat([_split_shards(g_local[nm], ax)[j] for nm, ax in BIG], F32) for j in range(4)])
    chip_sum = _add_half(packed, _swap_halves(packed), c_idx)
    reduced_half = _sum_slots(_scatter_chips(chip_sum), name="sum_chips")
    reduced = _join_halves(reduced_half)
    g_red = dict(zip([nm for nm, _ in BIG], _unpack_flat(reduced, big_shapes)))

    small_shapes = [args[nm].shape for nm in SMALL]
    small_sum = _sum_slots(_gather_all(_pack_flat([g_local[nm] for nm in SMALL], F32), name="gather_small_grads"),
                           name="sum_devices")
    g_red.update(zip(SMALL, _unpack_flat(small_sum, small_shapes)))

    pack_small = lambda pre: _pack_flat([args[pre + nm] for nm in SMALL], F32)
    upd_small = _adamw(pack_small(""), small_sum, pack_small("m_"), pack_small("v_"), name="adamw_small")
    upd = {nm: trip for nm, trip in zip(SMALL, zip(*[_unpack_flat(t, small_shapes) for t in upd_small]))}
    for nm, _ in BIG:
        upd[nm] = _adamw(args[nm], g_red[nm], args["m_" + nm], args["v_" + nm], name="adamw_" + nm)

    outs = [loss, dx[None]]
    outs += [g_red[nm] for nm in WEIGHTS]
    for i in range(3):
        outs += [upd[nm][i] for nm in WEIGHTS]
    return tuple(outs)


_ARG_NAMES = ("x", "p", "positions") + WEIGHTS + ("loss_target",) + tuple("m_" + nm for nm in WEIGHTS) \
    + tuple("v_" + nm for nm in WEIGHTS)


def kernel(*arrays):
    assert len(arrays) == len(_ARG_NAMES), len(arrays)
    return _step(dict(zip(_ARG_NAMES, arrays)))
```

```python
import functools
import math

import jax
import jax.numpy as jnp
import numpy as np
from jax import lax
from jax.experimental import pallas as pl
from jax.experimental.pallas import tpu as pltpu

F32 = jnp.float32
BF16 = jnp.bfloat16
MXU_DTYPE = BF16
LANES = 128
VMEM_LIMIT = 56 * 1024 * 1024
MM_VMEM_BUDGET = 36 * 1024 * 1024
ELEMENTWISE_BLOCK_BYTES = 2 * 1024 * 1024

D_MODEL = 1024
N_HEADS = 8
HEAD = 64
QK_ROPE = 32
Q_LORA = 384
KV_LORA = 256
MIX = 512
SSD_CHUNK = 128
CONV_W = 4
POOL_WINDOWS = (2, 4, 8, 16)
LRU_C = 8.0
EPS = 1e-6
ROPE_THETA = 10000.0
ATT_SCALE = (HEAD + QK_ROPE) ** -0.5
SPLIT_SIZES = (Q_LORA, KV_LORA, QK_ROPE, MIX, MIX, 768, N_HEADS, MIX, MIX, 4 * D_MODEL)
IN_PAD_COLS = 9088
IN_ALL_COLS = 9216

ADAM_LR, ADAM_B1, ADAM_B2, ADAM_EPS, ADAM_WD, ADAM_STEP = 0.001, 0.9, 0.999, 1e-08, 0.01, 10

BIG = (("w_in", 2), ("w_uq", 2), ("w_ukv", 2), ("ssd_conv_w", 2), ("lru_conv_w", 2), ("w_branch", 3),
       ("w_out", 1), ("w_ff1", 2), ("w_ff2", 1), ("w_ple_gate", 1), ("w_ple", 2))
SMALL = ("g_mix", "q_norm", "kv_norm", "w_pool", "pool_scale", "ssd_conv_b", "ssd_dt_bias", "ssd_a_log",
         "ssd_d", "ssd_norm", "lru_conv_b", "lru_w_a", "lru_b_a", "lru_w_i", "lru_b_i", "lru_lambda",
         "g_mlp", "g_ple", "g_final")
WEIGHTS = ("g_mix", "w_in", "q_norm", "w_uq", "kv_norm", "w_ukv", "w_pool", "pool_scale", "ssd_conv_w",
           "ssd_conv_b", "ssd_dt_bias", "ssd_a_log", "ssd_d", "ssd_norm", "lru_conv_w", "lru_conv_b", "lru_w_a",
           "lru_b_a", "lru_w_i", "lru_b_i", "lru_lambda", "w_branch", "w_out", "g_mlp", "w_ff1", "w_ff2", "g_ple",
           "w_ple_gate", "w_ple", "g_final")
CONV_SHARDED = ("ssd_conv_w", "lru_conv_w")
PACK_W = 1024
PACK_ROWS = 64


def _cparams(sem, vmem=VMEM_LIMIT):
    return pltpu.CompilerParams(dimension_semantics=sem, vmem_limit_bytes=vmem)


def _pick(n, cands):
    for c in cands:
        if n % c == 0:
            return c
    return n


def _sigmoid(x):
    return 1.0 / (1.0 + jnp.exp(-x))


def _silu(x):
    return x * _sigmoid(x)


def _silu_grad(x):
    s = _sigmoid(x)
    return s * (1.0 + x * (1.0 - s))


def _softplus(x):
    e = jnp.exp(-jnp.abs(x))
    log1p_e = jnp.where(e < 1e-3, e * (1.0 - e * (0.5 - e * (1.0 / 3.0))), jnp.log(1.0 + e))
    return jnp.maximum(x, 0.0) + log1p_e


_GELU_C = math.sqrt(2.0 / math.pi)


def _gelu(x):
    t = jnp.tanh(_GELU_C * (x + 0.044715 * x * x * x))
    return 0.5 * x * (1.0 + t)


def _gelu_grad(x):
    t = jnp.tanh(_GELU_C * (x + 0.044715 * x * x * x))
    return 0.5 * (1.0 + t) + 0.5 * x * (1.0 - t * t) * _GELU_C * (1.0 + 3.0 * 0.044715 * x * x)


def _neg_expm1(x):
    series = -x * (1.0 + 0.5 * x * (1.0 + (1.0 / 3.0) * x * (1.0 + 0.25 * x)))
    return jnp.where(x > -0.05, series, 1.0 - jnp.exp(x))


def _shift_down(x, k, row):
    return jnp.where(row >= k, pltpu.roll(x, k, 0), 0.0)


def _shift_up(x, k, row):
    n = x.shape[0]
    return jnp.where(row < n - k, pltpu.roll(x, n - k, 0), 0.0)


def _cumsum_rows(x, row):
    d = 1
    while d < x.shape[0]:
        x = x + _shift_down(x, d, row)
        d *= 2
    return x


def _rev_cumsum_rows(x, row):
    d = 1
    while d < x.shape[0]:
        x = x + _shift_up(x, d, row)
        d *= 2
    return x


def _cumsum_lanes(x, col):
    d = 1
    while d < x.shape[1]:
        x = x + jnp.where(col >= d, pltpu.roll(x, d, 1), 0.0)
        d *= 2
    return x


def _dot(a, b, ta=False, tb=False):
    dn = (((0 if ta else 1,), (1 if tb else 0,)), ((), ()))
    return lax.dot_general(a.astype(MXU_DTYPE), b.astype(MXU_DTYPE), dn, preferred_element_type=F32)


def _mm_tiles(m, n, k, a_bytes, b_bytes, mn_bytes):
    best = None
    for tm in (1024, 512, 384, 256, 128):
        for tn in (1024, 512, 384, 256, 128):
            for tk in (2048, 1024, 512, 384, 256, 128):
                if m % tm or n % tn or k % tk:
                    continue
                vmem = 2 * (tm * tk * a_bytes + tk * tn * b_bytes) + 2 * tm * tn * mn_bytes + 4 * tm * tn
                vmem += 2 * (tm * tk + tk * tn)
                if vmem > MM_VMEM_BUDGET:
                    continue
                steps = (m // tm) * (n // tn) * (k // tk)
                key = (steps, vmem)
                if best is None or key < best[0]:
                    best = (key, (tm, tn, tk))
    assert best is not None, (m, n, k)
    return best[1]


def _mm(a, b, *, ta=False, tb=False, epilogue=None, tiles=(), tile_offsets=None, rowvecs=(), out_dtypes=(F32,), name):
    m, k = (a.shape[1], a.shape[0]) if ta else a.shape
    n = b.shape[0] if tb else b.shape[1]
    assert (b.shape[1] if tb else b.shape[0]) == k, (a.shape, b.shape, ta, tb)
    mn_bytes = sum(t.dtype.itemsize for t in tiles) + sum(jnp.dtype(dt).itemsize for dt in out_dtypes)
    tm, tn, tk = _mm_tiles(m, n, k, a.dtype.itemsize, b.dtype.itemsize, mn_bytes)
    nk = k // tk
    nt, nr, no = len(tiles), len(rowvecs), len(out_dtypes)

    def body(*refs):
        a_ref, b_ref = refs[0], refs[1]
        tile_refs = refs[2:2 + nt]
        row_refs = refs[2 + nt:2 + nt + nr]
        out_refs = refs[2 + nt + nr:2 + nt + nr + no]
        acc_ref = refs[-1]
        kk = pl.program_id(2)

        @pl.when(kk == 0)
        def _():
            acc_ref[...] = jnp.zeros_like(acc_ref)

        acc_ref[...] += _dot(a_ref[...], b_ref[...], ta, tb)

        @pl.when(kk == nk - 1)
        def _():
            acc = acc_ref[...]
            if epilogue is None:
                outs = (acc,)
            else:
                outs = epilogue(acc, *[t[...] for t in tile_refs], *[r[...] for r in row_refs])
            for o_ref, o in zip(out_refs, outs):
                o_ref[...] = o.astype(o_ref.dtype)

    a_spec = pl.BlockSpec((tk, tm), lambda i, j, kk: (kk, i)) if ta else pl.BlockSpec((tm, tk), lambda i, j, kk: (i, kk))
    b_spec = pl.BlockSpec((tn, tk), lambda i, j, kk: (j, kk)) if tb else pl.BlockSpec((tk, tn), lambda i, j, kk: (kk, j))
    mn_spec = pl.BlockSpec((tm, tn), lambda i, j, kk: (i, j))
    row_spec = pl.BlockSpec((1, tn), lambda i, j, kk: (0, j))
    tile_offsets = tile_offsets or (0,) * nt
    assert all(off % tn == 0 for off in tile_offsets), (tile_offsets, tn)
    tile_specs = [pl.BlockSpec((tm, tn), lambda i, j, kk, ob=off // tn: (i, j + ob)) for off in tile_offsets]
    outs = pl.pallas_call(
        body, name=name,
        grid=(m // tm, n // tn, nk),
        in_specs=[a_spec, b_spec] + tile_specs + [row_spec] * nr,
        out_specs=[mn_spec] * no,
        out_shape=[jax.ShapeDtypeStruct((m, n), dt) for dt in out_dtypes],
        scratch_shapes=[pltpu.VMEM((tm, tn), F32)],
        compiler_params=_cparams(("parallel", "parallel", "arbitrary")),
    )(a, b, *tiles, *rowvecs)
    return outs[0] if no == 1 else tuple(outs)


def _rowwise(fn, rows, fulls, outs, *, name, tm=None):
    r = rows[0].shape[0]
    if tm is None:
        widest = max([x.shape[1] for x in rows] + [o[0] for o in outs])
        tm = _pick(r, (max(8, min(512, (512 * 1024) // widest)), 256, 128, 64, 32, 16, 8))
    nrow, nfull, nout = len(rows), len(fulls), len(outs)

    def body(*refs):
        row_refs = refs[:nrow]
        full_refs = refs[nrow:nrow + nfull]
        out_refs = refs[nrow + nfull:]
        res = fn(*[x[...] for x in row_refs], *[x[...] for x in full_refs])
        if not isinstance(res, (tuple, list)):
            res = (res,)
        step = pl.program_id(0)
        for o_ref, o, spec in zip(out_refs, res, outs):
            if spec[2] == "row":
                o_ref[...] = o.astype(o_ref.dtype)
            else:
                @pl.when(step == 0)
                def _(o_ref=o_ref):
                    o_ref[...] = jnp.zeros_like(o_ref)
                o_ref[...] += o

    in_specs = [pl.BlockSpec((tm, x.shape[1]), lambda i: (i, 0)) for x in rows]
    in_specs += [pl.BlockSpec(x.shape, lambda i, nd=x.ndim: (0,) * nd) for x in fulls]
    out_specs, out_shape = [], []
    for c, dt, kind in outs:
        if kind == "row":
            out_specs.append(pl.BlockSpec((tm, c), lambda i: (i, 0)))
            out_shape.append(jax.ShapeDtypeStruct((r, c), dt))
        else:
            out_specs.append(pl.BlockSpec((1, c), lambda i: (0, 0)))
            out_shape.append(jax.ShapeDtypeStruct((1, c), F32))
    res = pl.pallas_call(
        body, name=name, grid=(r // tm,), in_specs=in_specs, out_specs=out_specs, out_shape=out_shape,
        compiler_params=_cparams(("arbitrary",)),
    )(*rows, *fulls)
    return res[0] if nout == 1 else tuple(res)


def _colsum(x):
    return jnp.sum(x, axis=0, keepdims=True)


def _rms_parts(x, n_real):
    r = lax.rsqrt(jnp.sum(x * x, axis=-1, keepdims=True) * (1.0 / n_real) + EPS)
    return x * r, r


def _rms_fwd(x, g, *, n_real=None, out_dtype=BF16, name):
    n_real = n_real or x.shape[1]

    def fn(xv, gv):
        xh, _ = _rms_parts(xv, n_real)
        return xh * gv

    return _rowwise(fn, [x], [g], [(x.shape[1], out_dtype, "row")], name=name)


def _rms_bwd_math(xv, gv, dh, n_real):
    xh, r = _rms_parts(xv, n_real)
    dxh = dh * gv
    dx = r * (dxh - xh * (jnp.sum(dxh * xh, axis=-1, keepdims=True) * (1.0 / n_real)))
    return dx, _colsum(dh * xh)


def _rms_bwd(x, g, dh, res=None, *, name):
    n = x.shape[1]
    if res is None:
        def fn(xv, dhv, gv):
            return _rms_bwd_math(xv, gv, dhv.astype(F32), n)
        rows = [x, dh]
    else:
        def fn(xv, dhv, rv, gv):
            dx, dg = _rms_bwd_math(xv, gv, dhv.astype(F32), n)
            return dx + rv, dg
        rows = [x, dh, res]
    return _rowwise(fn, rows, [g], [(n, F32, "row"), (n, F32, "acc")], name=name)


def _seq_call(body, ins, outs, n_blocks, *, name):
    in_specs, args = [], []
    for x, kind in ins:
        in_specs.append(pl.BlockSpec((x.shape[0], LANES), lambda j: (0, j)))
        args.append(x)
    out_specs, out_shape = [], []
    for shape, dt in outs:
        out_specs.append(pl.BlockSpec((shape[0], LANES), lambda j: (0, j)))
        out_shape.append(jax.ShapeDtypeStruct(shape, dt))
    res = pl.pallas_call(body, name=name, grid=(n_blocks,), in_specs=in_specs, out_specs=out_specs,
                         out_shape=out_shape, compiler_params=_cparams(("parallel",)))(*args)
    return res[0] if len(outs) == 1 else tuple(res)


def _conv_pre(x, w, b, row):
    acc = x * w[CONV_W - 1:CONV_W, :] + b
    for k in range(CONV_W - 1):
        acc = acc + _shift_down(x, CONV_W - 1 - k, row) * w[k:k + 1, :]
    return acc


def _conv_fwd(x, w, b, *, silu, name):
    s, c = x.shape

    def body(x_ref, w_ref, b_ref, y_ref):
        xv = x_ref[...]
        row = lax.broadcasted_iota(jnp.int32, xv.shape, 0)
        pre = _conv_pre(xv, w_ref[...], b_ref[...], row)
        y_ref[...] = _silu(pre) if silu else pre

    return _seq_call(body, [(x, "seq"), (w, "par"), (b, "par")], [((s, c), F32)], c // LANES, name=name)


def _conv_bwd(x, w, b, dy, *, silu, name):
    s, c = x.shape

    def body(x_ref, w_ref, b_ref, dy_ref, dx_ref, dw_ref, db_ref):
        xv, wv, dv = x_ref[...], w_ref[...], dy_ref[...]
        row = lax.broadcasted_iota(jnp.int32, xv.shape, 0)
        if silu:
            dv = dv * _silu_grad(_conv_pre(xv, wv, b_ref[...], row))
        dx = dv * wv[CONV_W - 1:CONV_W, :]
        dws = [None] * CONV_W
        dws[CONV_W - 1] = _colsum(dv * xv)
        for k in range(CONV_W - 1):
            sh = CONV_W - 1 - k
            dx = dx + _shift_up(dv, sh, row) * wv[k:k + 1, :]
            dws[k] = _colsum(dv * _shift_down(xv, sh, row))
        dx_ref[...] = dx
        for k in range(CONV_W):
            dw_ref[k:k + 1, :] = dws[k]
        db_ref[...] = _colsum(dv)

    return _seq_call(body, [(x, "seq"), (w, "par"), (b, "par"), (dy, "seq")],
                     [((s, c), F32), ((CONV_W, c), F32), ((1, c), F32)], c // LANES, name=name)


def _pool_select(levels):
    g = pl.program_id(0)
    return jnp.where(g == 0, levels[0], jnp.where(g == 1, levels[1], jnp.where(g == 2, levels[2], levels[3])))


def _pool_count(row):
    g = pl.program_id(0)
    w = jnp.where(g == 0, POOL_WINDOWS[0], jnp.where(g == 1, POOL_WINDOWS[1],
                                                     jnp.where(g == 2, POOL_WINDOWS[2], POOL_WINDOWS[3])))
    return jnp.minimum(row + 1, w).astype(F32)


def _pool_fwd(u, *, name):
    def body(u_ref, d_ref):
        uv = u_ref[...]
        row = lax.broadcasted_iota(jnp.int32, uv.shape, 0)
        levels, cur, sh = [], uv, 1
        for _ in POOL_WINDOWS:
            cur = cur + _shift_down(cur, sh, row)
            levels.append(cur)
            sh *= 2
        d_ref[...] = _pool_select(levels) / _pool_count(row) - uv

    return _seq_call(body, [(u, "seq")], [(u.shape, F32)], u.shape[1] // LANES, name=name)


def _pool_bwd(dd, *, name):
    def body(dd_ref, du_ref):
        dv = dd_ref[...]
        row = lax.broadcasted_iota(jnp.int32, dv.shape, 0)
        levels, cur, sh = [], dv / _pool_count(row), 1
        for _ in POOL_WINDOWS:
            cur = cur + _shift_up(cur, sh, row)
            levels.append(cur)
            sh *= 2
        du_ref[...] = _pool_select(levels) - dv

    return _seq_call(body, [(dd, "seq")], [(dd.shape, F32)], dd.shape[1] // LANES, name=name)


def _lru_gates(pre_a, pre_i, xc, lam, b_a, b_i):
    r = _sigmoid(pre_a + b_a)
    i = _sigmoid(pre_i + b_i)
    sp = _softplus(-lam)
    log_a = -LRU_C * r * sp
    a = jnp.exp(log_a)
    mult = jnp.sqrt(_neg_expm1(2.0 * log_a))
    return r, i, sp, a, mult


def _lru_fwd(pre, xc, gate_in, lam, b_a, b_i, *, name):
    s, c = xc.shape
    nb = c // LANES

    def body(pa_ref, pi_ref, xc_ref, g_ref, lam_ref, ba_ref, bi_ref, y_ref, h_ref):
        xv = xc_ref[...]
        row = lax.broadcasted_iota(jnp.int32, xv.shape, 0)
        _, i, _, a, mult = _lru_gates(pa_ref[...], pi_ref[...], xv, lam_ref[...], ba_ref[...], bi_ref[...])
        h = xv * i * mult
        d = 1
        while d < s:
            h = h + a * _shift_down(h, d, row)
            a = a * jnp.where(row >= d, pltpu.roll(a, d, 0), 1.0)
            d *= 2
        h_ref[...] = h
        y_ref[...] = h * _gelu(g_ref[...])

    blk = lambda off: pl.BlockSpec((s, LANES), lambda j: (0, j + off))
    par = pl.BlockSpec((1, LANES), lambda j: (0, j))
    return pl.pallas_call(
        body, name=name, grid=(nb,),
        in_specs=[blk(0), blk(nb), blk(0), blk(0), par, par, par],
        out_specs=[blk(0), blk(0)],
        out_shape=[jax.ShapeDtypeStruct((s, c), F32)] * 2,
        compiler_params=_cparams(("parallel",)),
    )(pre, pre, xc, gate_in, lam, b_a, b_i)


def _lru_bwd(pre, xc, gate_in, lam, b_a, b_i, h, dy, *, name):
    s, c = xc.shape
    nb = c // LANES

    def body(pa_ref, pi_ref, xc_ref, g_ref, lam_ref, ba_ref, bi_ref, h_ref, dy_ref,
             dpa_ref, dpi_ref, dxc_ref, dg_ref, dlam_ref, dba_ref, dbi_ref):
        xv, gv, hv, dv = xc_ref[...], g_ref[...], h_ref[...], dy_ref[...]
        row = lax.broadcasted_iota(jnp.int32, xv.shape, 0)
        r, i, sp, a, mult = _lru_gates(pa_ref[...], pi_ref[...], xv, lam_ref[...], ba_ref[...], bi_ref[...])
        dg_ref[...] = dv * hv * _gelu_grad(gv)
        dh = dv * _gelu(gv)
        an = jnp.where(row < s - 1, pltpu.roll(a, s - 1, 0), 0.0)
        d = 1
        while d < s:
            dh = dh + an * _shift_up(dh, d, row)
            an = an * jnp.where(row < s - d, pltpu.roll(an, s - d, 0), 1.0)
            d *= 2
        da = dh * _shift_down(hv, 1, row)
        dxc_ref[...] = dh * i * mult
        di = dh * xv * mult
        dmult = dh * xv * i
        dlog_a = (da - dmult * a / mult) * a
        dr = dlog_a * (-LRU_C) * sp
        dlam_ref[...] = _colsum(dlog_a * LRU_C * r * _sigmoid(-lam_ref[...]))
        dpa = dr * r * (1.0 - r)
        dpi = di * i * (1.0 - i)
        dpa_ref[...] = dpa
        dpi_ref[...] = dpi
        dba_ref[...] = _colsum(dpa)
        dbi_ref[...] = _colsum(dpi)

    blk = lambda off: pl.BlockSpec((s, LANES), lambda j: (0, j + off))
    par = pl.BlockSpec((1, LANES), lambda j: (0, j))
    sc = jax.ShapeDtypeStruct((s, c), F32)
    pc = jax.ShapeDtypeStruct((1, c), F32)
    dpa, dpi, dxc, dg, dlam, dba, dbi = pl.pallas_call(
        body, name=name, grid=(nb,),
        in_specs=[blk(0), blk(nb), blk(0), blk(0), par, par, par, blk(0), blk(0)],
        out_specs=[blk(0), blk(0), blk(0), blk(0), par, par, par],
        out_shape=[sc, sc, sc, sc, pc, pc, pc],
        compiler_params=_cparams(("parallel",)),
    )(pre, pre, xc, gate_in, lam, b_a, b_i, h, dy)
    return dpa, dpi, dxc, dg, dlam, dba, dbi


def _ssd_chunk_terms(dtcol, dtrow, bias, a_log):
    shp = (SSD_CHUNK, SSD_CHUNK)
    row = lax.broadcasted_iota(jnp.int32, shp, 0)
    col = lax.broadcasted_iota(jnp.int32, shp, 1)
    a_head = -jnp.exp(a_log)
    dt_c = jnp.broadcast_to(_softplus(dtcol + bias), shp)
    dt_r = jnp.broadcast_to(_softplus(dtrow + bias), shp)
    cs_c = _cumsum_rows(dt_c * a_head, row)
    cs_r = _cumsum_lanes(dt_r * a_head, col)
    cs_last = jnp.sum(jnp.where(row == SSD_CHUNK - 1, cs_c, 0.0), axis=0, keepdims=True)
    return row, col, a_head, dt_c, cs_c, cs_r, cs_last


def _ssd_fwd(xbc, dtcol, dtrow, bias, a_log, dskip, *, name):
    s = xbc.shape[0]
    nc = s // SSD_CHUNK

    def body(x_ref, b_ref, c_ref, dtc_ref, dtr_ref, bias_ref, alog_ref, d_ref, y_ref, st_ref, state):
        ci = pl.program_id(1)

        @pl.when(ci == 0)
        def _():
            state[...] = jnp.zeros_like(state)

        xv, bm, cm = x_ref[...], b_ref[...], c_ref[...]
        row, col, _, dt_c, cs_c, cs_r, cs_last = _ssd_chunk_terms(
            dtc_ref[0], dtr_ref[0], bias_ref[0], alog_ref[0])
        lmat = jnp.exp(jnp.where(col <= row, cs_c - cs_r, -jnp.inf))
        g = _dot(cm, bm, tb=True) * lmat
        xdt = xv * dt_c
        st = state[...]
        st_ref[0, 0] = st
        y_ref[...] = _dot(g, xdt) + _dot(cm, st) * jnp.exp(cs_c) + xv * d_ref[0]
        w = xdt * jnp.exp(cs_last - cs_c)
        state[...] = jnp.exp(cs_last) * st + _dot(bm.T, w)

    hc = lambda h, ci: (ci, h)
    scal = pl.BlockSpec((1, 1, 1), lambda h, ci: (h, 0, 0))
    return pl.pallas_call(
        body, name=name, grid=(N_HEADS, nc),
        in_specs=[pl.BlockSpec((SSD_CHUNK, LANES), hc),
                  pl.BlockSpec((SSD_CHUNK, LANES), lambda h, ci: (ci, N_HEADS + h // 4)),
                  pl.BlockSpec((SSD_CHUNK, LANES), lambda h, ci: (ci, N_HEADS + 2 + h // 4)),
                  pl.BlockSpec((1, SSD_CHUNK, 1), lambda h, ci: (h, ci, 0)),
                  pl.BlockSpec((1, 1, SSD_CHUNK), lambda h, ci: (h, 0, ci)),
                  scal, scal, scal],
        out_specs=[pl.BlockSpec((SSD_CHUNK, LANES), hc),
                   pl.BlockSpec((1, 1, LANES, LANES), lambda h, ci: (h, ci, 0, 0))],
        out_shape=[jax.ShapeDtypeStruct((s, N_HEADS * LANES), F32),
                   jax.ShapeDtypeStruct((N_HEADS, nc, LANES, LANES), F32)],
        scratch_shapes=[pltpu.VMEM((LANES, LANES), F32)],
        compiler_params=_cparams(("parallel", "arbitrary")),
    )(xbc, xbc, xbc, dtcol, dtrow, bias, a_log, dskip)


def _ssd_bwd(xbc, dtcol, dtrow, bias, a_log, dskip, states, dy, *, name):
    s = xbc.shape[0]
    nc = s // SSD_CHUNK

    def body(x_ref, b_ref, c_ref, dtc_ref, dtr_ref, bias_ref, alog_ref, d_ref, st_ref, dy_ref,
             dx_ref, db_ref, dc_ref, ddt_ref, dbias_ref, dalog_ref, dd_ref, dstate):
        ci = pl.program_id(1)

        @pl.when(ci == 0)
        def _():
            dstate[...] = jnp.zeros_like(dstate)
            dbias_ref[...] = jnp.zeros_like(dbias_ref)
            dalog_ref[...] = jnp.zeros_like(dalog_ref)
            dd_ref[...] = jnp.zeros_like(dd_ref)

        xv, bm, cm, dyv, st = x_ref[...], b_ref[...], c_ref[...], dy_ref[...], st_ref[0, 0]
        dtraw_c = dtc_ref[0]
        bias = bias_ref[0]
        row, col, a_head, dt_c, cs_c, cs_r, cs_last = _ssd_chunk_terms(dtraw_c, dtr_ref[0], bias, alog_ref[0])
        lmat = jnp.exp(jnp.where(col <= row, cs_c - cs_r, -jnp.inf))
        lmat_t = jnp.exp(jnp.where(row <= col, cs_r - cs_c, -jnp.inf))
        g = _dot(cm, bm, tb=True) * lmat
        g_t = _dot(bm, cm, tb=True) * lmat_t
        xdt = xv * dt_c
        e_c = jnp.exp(cs_c)
        f_c = jnp.exp(cs_last - cs_c)
        e_last = jnp.exp(cs_last)
        w = xdt * f_c
        dst = dstate[...]

        dg = _dot(dyv, xdt, tb=True)
        dg_t = _dot(xdt, dyv, tb=True)
        dxdt = _dot(g_t, dyv)
        rowsum = lambda v: jnp.sum(v, axis=1, keepdims=True)
        dcs = rowsum(dg * g) - rowsum(dg_t * g_t)
        dcm = _dot(dg * lmat, bm)
        dbm = _dot(dg_t * lmat_t, cm)
        z = _dot(cm, st)
        dz = dyv * e_c
        dcs = dcs + rowsum(dz * z)
        dcm = dcm + _dot(dz, st, tb=True)
        dst_in = _dot(cm.T, dz) + e_last * dst
        dcs_last = jnp.sum(jnp.sum(dst * st, axis=1, keepdims=True), axis=0, keepdims=True) * jnp.max(e_last, axis=1, keepdims=True)
        dbm = dbm + _dot(w, dst, tb=True)
        dw = _dot(bm, dst)
        dxdt = dxdt + dw * f_c
        q = rowsum(dw * w)
        dcs = dcs - q
        dcs_last = dcs_last + jnp.sum(q, axis=0, keepdims=True)
        dx_ref[...] = dxdt * dt_c + dyv * d_ref[0]
        ddt = rowsum(dxdt * xv)
        dcs_full = jnp.broadcast_to(dcs, (SSD_CHUNK, SSD_CHUNK)) + jnp.where(row == SSD_CHUNK - 1, dcs_last, 0.0)
        da = jnp.max(_rev_cumsum_rows(dcs_full, row), axis=1, keepdims=True)
        dt_col = jnp.max(dt_c, axis=1, keepdims=True)
        ddt = ddt + da * a_head
        draw = ddt * _sigmoid(dtraw_c + bias)
        ddt_ref[0] = draw
        db_ref[0] = dbm
        dc_ref[0] = dcm
        dstate[...] = dst_in
        tot = lambda v: jnp.broadcast_to(jnp.sum(v, axis=0, keepdims=True), (1, LANES))
        dbias_ref[0] += tot(draw)
        dalog_ref[0] += tot(da * dt_col) * a_head
        dd_ref[0] += tot(rowsum(dyv * xv))

    rev = lambda ci: nc - 1 - ci
    hc = lambda h, ci: (rev(ci), h)
    scal = pl.BlockSpec((1, 1, 1), lambda h, ci: (h, 0, 0))
    pacc = pl.BlockSpec((1, 1, LANES), lambda h, ci: (h, 0, 0))
    per_head = pl.BlockSpec((1, SSD_CHUNK, LANES), lambda h, ci: (h, rev(ci), 0))
    return pl.pallas_call(
        body, name=name, grid=(N_HEADS, nc),
        in_specs=[pl.BlockSpec((SSD_CHUNK, LANES), hc),
                  pl.BlockSpec((SSD_CHUNK, LANES), lambda h, ci: (rev(ci), N_HEADS + h // 4)),
                  pl.BlockSpec((SSD_CHUNK, LANES), lambda h, ci: (rev(ci), N_HEADS + 2 + h // 4)),
                  pl.BlockSpec((1, SSD_CHUNK, 1), lambda h, ci: (h, rev(ci), 0)),
                  pl.BlockSpec((1, 1, SSD_CHUNK), lambda h, ci: (h, 0, rev(ci))),
                  scal, scal, scal,
                  pl.BlockSpec((1, 1, LANES, LANES), lambda h, ci: (h, rev(ci), 0, 0)),
                  pl.BlockSpec((SSD_CHUNK, LANES), hc)],
        out_specs=[pl.BlockSpec((SSD_CHUNK, LANES), hc), per_head, per_head,
                   pl.BlockSpec((1, SSD_CHUNK, 1), lambda h, ci: (h, rev(ci), 0)),
                   pacc, pacc, pacc],
        out_shape=[jax.ShapeDtypeStruct((s, N_HEADS * LANES), F32),
                   jax.ShapeDtypeStruct((N_HEADS, s, LANES), F32),
                   jax.ShapeDtypeStruct((N_HEADS, s, LANES), F32),
                   jax.ShapeDtypeStruct((N_HEADS, s, 1), F32),
                   jax.ShapeDtypeStruct((N_HEADS, 1, LANES), F32),
                   jax.ShapeDtypeStruct((N_HEADS, 1, LANES), F32),
                   jax.ShapeDtypeStruct((N_HEADS, 1, LANES), F32)],
        scratch_shapes=[pltpu.VMEM((LANES, LANES), F32)],
        compiler_params=_cparams(("parallel", "arbitrary")),
    )(xbc, xbc, xbc, dtcol, dtrow, bias, a_log, dskip, states, dy)


def _att_tile(s):
    return _pick(s, (512, 256, 128))


def _tri(t, transposed=False):
    r = lax.broadcasted_iota(jnp.int32, (t, t), 0)
    c = lax.broadcasted_iota(jnp.int32, (t, t), 1)
    return (r <= c) if transposed else (c <= r)


def _rows_at(ref, blk, t):
    return ref[pl.ds(pl.multiple_of(blk * t, t), t), :]


def _flash_fwd(q, k, v, *, name):
    s = q.shape[0]
    t = _att_tile(s)
    nq = s // t

    def body(q_ref, k_ref, v_ref, o_ref, lse_ref):
        i = pl.program_id(1)
        qv = q_ref[...]

        def step(j, carry, diagonal):
            m_old, l_old, acc = carry
            sc = _dot(qv, _rows_at(k_ref, j, t), tb=True)
            if diagonal:
                sc = jnp.where(_tri(t), sc, -jnp.inf)
            m_new = jnp.maximum(m_old, jnp.max(sc, axis=1, keepdims=True))
            alpha = jnp.exp(m_old - m_new)
            p = jnp.exp(sc - m_new)
            return (m_new, alpha * l_old + jnp.sum(p, axis=1, keepdims=True),
                    alpha * acc + _dot(p, _rows_at(v_ref, j, t)))

        init = (jnp.full((t, 1), -jnp.inf, F32), jnp.zeros((t, 1), F32), jnp.zeros((t, LANES), F32))
        carry = lax.fori_loop(0, i, lambda j, c: step(j, c, False), init)
        m_fin, l_fin, acc = step(i, carry, True)
        o_ref[...] = (acc / l_fin).astype(o_ref.dtype)
        lse_ref[0] = m_fin + jnp.log(l_fin)

    q_spec = pl.BlockSpec((t, LANES), lambda h, i: (i, h))
    kv_spec = pl.BlockSpec((s, LANES), lambda h, i: (0, h))
    return pl.pallas_call(
        body, name=name, grid=(N_HEADS, nq),
        in_specs=[q_spec, kv_spec, kv_spec],
        out_specs=[q_spec, pl.BlockSpec((1, t, 1), lambda h, i: (h, i, 0))],
        out_shape=[jax.ShapeDtypeStruct(q.shape, BF16), jax.ShapeDtypeStruct((N_HEADS, s, 1), F32)],
        compiler_params=_cparams(("parallel", "arbitrary")),
    )(q, k, v)


def _flash_bwd_dq(q, k, v, o, do, lse, *, name):
    s = q.shape[0]
    t = _att_tile(s)
    nq = s // t

    def body(q_ref, k_ref, v_ref, o_ref, do_ref, lse_ref, dq_ref, dl_ref):
        i = pl.program_id(1)
        qv, dov, lse = q_ref[...], do_ref[...], lse_ref[0]
        delta = jnp.sum(dov.astype(F32) * o_ref[...].astype(F32), axis=1, keepdims=True)
        dl_ref[0] = delta

        def step(j, acc, diagonal):
            kj = _rows_at(k_ref, j, t)
            p = jnp.exp(_dot(qv, kj, tb=True) - lse)
            if diagonal:
                p = jnp.where(_tri(t), p, 0.0)
            ds = p * (_dot(dov, _rows_at(v_ref, j, t), tb=True) - delta)
            return acc + _dot(ds, kj)

        acc = lax.fori_loop(0, i, lambda j, c: step(j, c, False), jnp.zeros((t, LANES), F32))
        dq_ref[...] = step(i, acc, True) * ATT_SCALE

    q_spec = pl.BlockSpec((t, LANES), lambda h, i: (i, h))
    kv_spec = pl.BlockSpec((s, LANES), lambda h, i: (0, h))
    col_spec = pl.BlockSpec((1, t, 1), lambda h, i: (h, i, 0))
    return pl.pallas_call(
        body, name=name, grid=(N_HEADS, nq),
        in_specs=[q_spec, kv_spec, kv_spec, q_spec, q_spec, col_spec],
        out_specs=[q_spec, col_spec],
        out_shape=[jax.ShapeDtypeStruct(q.shape, F32), jax.ShapeDtypeStruct((N_HEADS, s, 1), F32)],
        compiler_params=_cparams(("parallel", "arbitrary")),
    )(q, k, v, o, do, lse)


def _flash_bwd_dkv(q, k, v, do, lse_row, delta_row, *, name):
    s = q.shape[0]
    t = _att_tile(s)
    nq = s // t

    def body(q_ref, k_ref, v_ref, do_ref, lse_ref, dl_ref, dk_ref, dv_ref):
        j = pl.program_id(1)
        kv, vv = k_ref[...], v_ref[...]

        def step(i, carry, diagonal):
            dk, dv = carry
            qi, doi = _rows_at(q_ref, i, t), _rows_at(do_ref, i, t)
            cols = pl.ds(pl.multiple_of(i * t, t), t)
            p_t = jnp.exp(_dot(kv, qi, tb=True) - lse_ref[0, :, cols])
            if diagonal:
                p_t = jnp.where(_tri(t, transposed=True), p_t, 0.0)
            ds_t = p_t * (_dot(vv, doi, tb=True) - dl_ref[0, :, cols])
            return dk + _dot(ds_t, qi), dv + _dot(p_t, doi)

        zero = jnp.zeros((t, LANES), F32)
        carry = step(j, (zero, zero), True)
        dk, dv = lax.fori_loop(j + 1, nq, lambda i, c: step(i, c, False), carry)
        dk_ref[...] = dk
        dv_ref[...] = dv

    q_spec = pl.BlockSpec((s, LANES), lambda h, j: (0, h))
    kv_spec = pl.BlockSpec((t, LANES), lambda h, j: (j, h))
    row_spec = pl.BlockSpec((1, 1, s), lambda h, j: (h, 0, 0))
    return pl.pallas_call(
        body, name=name, grid=(N_HEADS, nq),
        in_specs=[q_spec, kv_spec, kv_spec, q_spec, row_spec, row_spec],
        out_specs=[kv_spec, kv_spec],
        out_shape=[jax.ShapeDtypeStruct(q.shape, F32)] * 2,
        compiler_params=_cparams(("parallel", "arbitrary")),
    )(q, k, v, do, lse_row, delta_row)


def _rope(v, cos_t, sin_p, sin_m):
    return v * cos_t + pltpu.roll(v, QK_ROPE // 2, 1) * sin_p + pltpu.roll(v, LANES - QK_ROPE // 2, 1) * sin_m


def _rope_t(d, cos_t, sin_p, sin_m):
    return d * cos_t + pltpu.roll(d * sin_p, LANES - QK_ROPE // 2, 1) + pltpu.roll(d * sin_m, QK_ROPE // 2, 1)


def _att_prep(q_pad, kv2, kr, cos_t, sin_p, sin_m, *, name):
    w = N_HEADS * LANES

    def fn(qv, kvv, krv, c, sp, sm):
        kr_rot = _rope(krv, c, sp, sm)
        qs, ks = [], []
        for h in range(N_HEADS):
            blk = slice(h * LANES, (h + 1) * LANES)
            qs.append(_rope(qv[:, blk], c, sp, sm) * ATT_SCALE)
            ks.append(kvv[:, blk] + kr_rot)
        return jnp.concatenate(qs, axis=1), jnp.concatenate(ks, axis=1), kvv[:, w:]

    return _rowwise(fn, [q_pad, kv2, kr, cos_t, sin_p, sin_m], [],
                    [(w, BF16, "row"), (w, BF16, "row"), (w, BF16, "row")], name=name)


def _att_prep_bwd(dq, dk, cos_t, sin_p, sin_m, *, name):
    w = N_HEADS * LANES

    def fn(dqv, dkv, c, sp, sm):
        outs, dkr = [], None
        for h in range(N_HEADS):
            blk = slice(h * LANES, (h + 1) * LANES)
            outs.append(_rope_t(dqv[:, blk], c, sp, sm))
            dkr = dkv[:, blk] if dkr is None else dkr + dkv[:, blk]
        return jnp.concatenate(outs, axis=1), _rope_t(dkr, c, sp, sm)

    return _rowwise(fn, [dq, dk, cos_t, sin_p, sin_m], [], [(w, BF16, "row"), (LANES, F32, "row")], name=name)


_ANY = pl.BlockSpec(memory_space=pl.ANY)
_MESH = pl.DeviceIdType.MESH


def _mesh_pos():
    return lax.axis_index("x"), lax.axis_index("y"), lax.axis_index("c")


def _remote(src, dst, send_sem, recv_sem, dev):
    return pltpu.make_async_remote_copy(src_ref=src, dst_ref=dst, send_sem=send_sem, recv_sem=recv_sem,
                                        device_id=dev, device_id_type=_MESH)


def _other_chips(x, y):
    chips = [(1 - x, y), (x, 1 - y), (1 - x, 1 - y)]
    return chips, [2 * cx + cy for cx, cy in chips]


def _comm_call(body, ins, out_shapes, n_sems, *, name):
    return pl.pallas_call(
        body, name=name, in_specs=[_ANY] * len(ins), out_specs=[_ANY] * len(out_shapes), out_shape=out_shapes,
        scratch_shapes=[pltpu.SemaphoreType.DMA((k,)) for k in n_sems],
    )(*ins)


def _gather_layers(shards):
    n = len(shards)

    def body(*refs):
        xs, outs = refs[:n], refs[n:2 * n]
        send_sems, recv_sems, local_sems = refs[2 * n:]
        x, y, c = _mesh_pos()
        k = 2 * x + y
        sibling = (x, y, 1 - c)
        chips, ks = _other_chips(x, y)
        local = [pltpu.make_async_copy(xs[w], outs[w].at[k], local_sems.at[w]) for w in range(n)]
        for cp in local:
            cp.start()
        first = [_remote(xs[w].at[c], outs[w].at[k, c], send_sems.at[6 * w + j], recv_sems.at[6 * w + j], (*chips[j], c))
                 for w in range(n) for j in range(3)]
        for cp in first:
            cp.start()
        passed = []
        for j in range(3):
            for w in range(n):
                land = outs[w].at[ks[j], c]
                _remote(land, land, send_sems.at[6 * w + j], recv_sems.at[6 * w + j], sibling).wait_recv()
                passed.append(_remote(land, land, send_sems.at[6 * w + 3 + j], recv_sems.at[6 * w + 3 + j], sibling))
                passed[-1].start()
        for j in range(3):
            for w in range(n):
                land = outs[w].at[ks[j], 1 - c]
                _remote(land, land, send_sems.at[6 * w + 3 + j], recv_sems.at[6 * w + 3 + j], sibling).wait_recv()
        for cp in first + passed:
            cp.wait_send()
        for cp in local:
            cp.wait()

    shapes = [jax.ShapeDtypeStruct((4,) + t.shape, t.dtype) for t in shards]
    return _comm_call(body, shards, shapes, (6 * n, 6 * n, n), name="gather_layers")


def _send_other_layer(g0s, g1s):
    n = len(g0s)

    def body(*refs):
        g0, g1, outs = refs[:n], refs[n:2 * n], refs[2 * n:3 * n]
        send_sems, recv_sems = refs[3 * n:]
        x, y, c = _mesh_pos()
        sibling = (x, y, 1 - c)

        @pl.when(c == 0)
        def _():
            for w in range(n):
                _remote(g1[w], outs[w], send_sems.at[w], recv_sems.at[w], sibling).start()

        @pl.when(c == 1)
        def _():
            for w in range(n):
                _remote(g0[w], outs[w], send_sems.at[w], recv_sems.at[w], sibling).start()

        for w in range(n):
            done = _remote(outs[w], outs[w], send_sems.at[w], recv_sems.at[w], sibling)
            done.wait_send()
            done.wait_recv()

    shapes = [jax.ShapeDtypeStruct(t.shape, t.dtype) for t in g0s]
    return _comm_call(body, list(g0s) + list(g1s), shapes, (n, n), name="send_other_layer")


def _scatter_join(parts):
    n = len(parts)

    def body(*refs):
        ps, outs = refs[:n], refs[n:2 * n]
        send_sems, recv_sems, local_sems = refs[2 * n:]
        x, y, c = _mesh_pos()
        k = 2 * x + y
        sibling = (x, y, 1 - c)
        chips, ks = _other_chips(x, y)
        sem = lambda w, j: (send_sems.at[7 * w + j], recv_sems.at[7 * w + j])
        local = [pltpu.make_async_copy(ps[w].at[k], outs[w].at[c, k], local_sems.at[w]) for w in range(n)]
        sends = [_remote(ps[w].at[ks[j]], outs[w].at[c, k], *sem(w, j), (*chips[j], c)) for w in range(n) for j in range(3)]
        sends += [_remote(ps[w].at[k], outs[w].at[c, k], *sem(w, 6), sibling) for w in range(n)]
        for cp in local + sends:
            cp.start()
        for j in range(3):
            for w in range(n):
                land = outs[w].at[c, ks[j]]
                _remote(land, land, *sem(w, j), sibling).wait_recv()
                sends.append(_remote(land, land, *sem(w, 3 + j), sibling))
                sends[-1].start()
        for w in range(n):
            land = outs[w].at[1 - c, k]
            _remote(land, land, *sem(w, 6), sibling).wait_recv()
        for j in range(3):
            for w in range(n):
                land = outs[w].at[1 - c, ks[j]]
                _remote(land, land, *sem(w, 3 + j), sibling).wait_recv()
        for cp in sends:
            cp.wait_send()
        for cp in local:
            cp.wait()

    shapes = [jax.ShapeDtypeStruct((2,) + t.shape, t.dtype) for t in parts]
    return _comm_call(body, parts, shapes, (7 * n, 7 * n, n), name="scatter_join")


def _gather_all(vec, *, name):
    r, w = vec.shape

    def body(v_ref, out_ref, send_sems, recv_sems, local_sem):
        x, y, c = _mesh_pos()

        def slot(px, py, pc):
            return out_ref.at[4 * px + 2 * py + pc]

        mine = pltpu.make_async_copy(v_ref, slot(x, y, c), local_sem)
        mine.start()
        peers = []
        for rel in range(1, 8):
            fx, fy, fc = (rel >> 2) & 1, (rel >> 1) & 1, rel & 1
            peers.append((x ^ fx, y ^ fy, c ^ fc))
        cps = [_remote(v_ref, slot(x, y, c), send_sems.at[j], recv_sems.at[j], peer) for j, peer in enumerate(peers)]
        for cp in cps:
            cp.start()
        for j, peer in enumerate(peers):
            _remote(slot(*peer), slot(*peer), send_sems.at[j], recv_sems.at[j], peer).wait_recv()
        for cp in cps:
            cp.wait_send()
        mine.wait()

    return pl.pallas_call(
        body, name=name, in_specs=[_ANY], out_specs=_ANY,
        out_shape=jax.ShapeDtypeStruct((8, r, w), vec.dtype),
        scratch_shapes=[pltpu.SemaphoreType.DMA((7,)), pltpu.SemaphoreType.DMA((7,)), pltpu.SemaphoreType.DMA],
    )(vec)


def _row_tile(rows, row_bytes):
    for tm in (1024, 512, 256, 128, 64, 32, 16):
        if rows % tm == 0 and tm * row_bytes <= ELEMENTWISE_BLOCK_BYTES:
            return tm
    return 16 if rows % 16 == 0 else rows


def _chip_sum(g0, g1, got, c, *, name):
    r, w = g0.shape
    tm = _row_tile(r, w * 4)

    def body(c_ref, g0_ref, g1_ref, o_ref, out_ref):
        mine = jnp.where(c_ref[0] == 0, g0_ref[...], g1_ref[...])
        out_ref[...] = (mine + o_ref[...]).astype(out_ref.dtype)

    return pl.pallas_call(
        body, name=name,
        grid_spec=pltpu.PrefetchScalarGridSpec(
            num_scalar_prefetch=1, grid=(r // tm,),
            in_specs=[pl.BlockSpec((tm, w), lambda i, c_ref: (i * (1 - c_ref[0]), 0)),
                      pl.BlockSpec((tm, w), lambda i, c_ref: (i * c_ref[0], 0)),
                      pl.BlockSpec((tm, w), lambda i, c_ref: (i, 0))],
            out_specs=pl.BlockSpec((tm, w), lambda i, c_ref: (i, 0))),
        out_shape=jax.ShapeDtypeStruct((r, w), BF16),
        compiler_params=_cparams(("arbitrary",)),
    )(jnp.reshape(c, (1,)).astype(jnp.int32), g0, g1, got)


def _sum_slots(stack, *, name):
    n, r, w = stack.shape
    tm = _row_tile(r, n * w * stack.dtype.itemsize)

    def body(s_ref, out_ref):
        acc = s_ref[0].astype(F32)
        for i in range(1, n):
            acc = acc + s_ref[i].astype(F32)
        out_ref[...] = acc

    return pl.pallas_call(
        body, name=name, grid=(r // tm,),
        in_specs=[pl.BlockSpec((n, tm, w), lambda i: (0, i, 0))],
        out_specs=pl.BlockSpec((tm, w), lambda i: (i, 0)),
        out_shape=jax.ShapeDtypeStruct((r, w), F32),
        compiler_params=_cparams(("parallel",)),
    )(stack)


def _adam_math(wv, gv, mv, vv):
    m_new = ADAM_B1 * mv + (1.0 - ADAM_B1) * gv
    v_new = ADAM_B2 * vv + (1.0 - ADAM_B2) * (gv * gv)
    m_hat = m_new / (1.0 - ADAM_B1 ** ADAM_STEP)
    v_hat = v_new / (1.0 - ADAM_B2 ** ADAM_STEP)
    delta = -ADAM_LR * (m_hat / (jnp.sqrt(v_hat) + ADAM_EPS) + ADAM_WD * wv)
    return delta, m_new, v_new


def _adamw(w, g, m, v, *, name):
    shape = w.shape
    cols = shape[-1]
    flat = lambda t: t.reshape(-1, cols)
    rows = flat(w).shape[0]
    tm = _pick(rows, (256, 128, 64, 32, 16, 8))
    outs = _rowwise(_adam_math, [flat(w), flat(g), flat(m), flat(v)], [], [(cols, F32, "row")] * 3, name=name, tm=tm)
    return tuple(o.reshape(shape) for o in outs)


def _adamw_slots(w, slots, m, v, *, name):
    shape = w.shape
    cols = shape[-1]
    v3 = lambda t: t.reshape(2, -1, cols)
    rows = v3(w).shape[1]
    assert slots.shape == (2, 4, rows, cols), (slots.shape, shape)
    tm = _row_tile(rows, cols * 4)

    def body(w_ref, s_ref, m_ref, v_ref, g_ref, d_ref, mo_ref, vo_ref):
        g = s_ref[0, 0].astype(F32)
        for i in range(1, 4):
            g = g + s_ref[0, i].astype(F32)
        delta, m_new, v_new = _adam_math(w_ref[0], g, m_ref[0], v_ref[0])
        g_ref[0], d_ref[0], mo_ref[0], vo_ref[0] = g, delta, m_new, v_new

    blk = pl.BlockSpec((1, tm, cols), lambda l, i: (l, i, 0))
    outs = pl.pallas_call(
        body, name=name, grid=(2, rows // tm),
        in_specs=[blk, pl.BlockSpec((1, 4, tm, cols), lambda l, i: (l, 0, i, 0)), blk, blk],
        out_specs=[blk] * 4, out_shape=[jax.ShapeDtypeStruct((2, rows, cols), F32)] * 4,
        compiler_params=_cparams(("parallel", "parallel")),
    )(v3(w), slots, v3(m), v3(v))
    return tuple(o.reshape(shape) for o in outs)


def _pad_blocks(w, axis, n_blocks, real, to=LANES, offset=0):
    axis = axis % w.ndim
    shp = w.shape
    w = w.reshape(shp[:axis] + (n_blocks, real) + shp[axis + 1:])
    pads = [(0, 0)] * w.ndim
    pads[axis + 1] = (offset, to - real - offset)
    w = jnp.pad(w, pads)
    return w.reshape(shp[:axis] + (n_blocks * to,) + shp[axis + 1:])


def _unpad_blocks(w, axis, n_blocks, real, to=LANES, offset=0):
    axis = axis % w.ndim
    shp = w.shape
    w = w.reshape(shp[:axis] + (n_blocks, to) + shp[axis + 1:])
    w = lax.slice_in_dim(w, offset, offset + real, axis=axis + 1)
    return w.reshape(shp[:axis] + (n_blocks * real,) + shp[axis + 1:])


def _block_diag(w):
    n, a, b = w.shape
    eye = jnp.eye(n, dtype=w.dtype)
    return (eye[:, None, :, None] * w[:, :, None, :]).reshape(n * a, n * b)


def _block_diag_t(d, n):
    a, b = d.shape[0] // n, d.shape[1] // n
    d = d.reshape(n, a, n, b)
    return jnp.stack([d[i, :, i, :] for i in range(n)])


_SPLITS = np.cumsum((0,) + SPLIT_SIZES)


def _w_in_groups(w_in):
    sl = lambda i: w_in[:, _SPLITS[i]:_SPLITS[i + 1]]
    xbc = sl(5)
    xbc_pad = jnp.concatenate([_pad_blocks(xbc[:, :MIX], 1, N_HEADS, HEAD),
                               _pad_blocks(xbc[:, MIX:MIX + 2 * HEAD], 1, 2, HEAD),
                               _pad_blocks(xbc[:, MIX + 2 * HEAD:], 1, 2, HEAD)], axis=1)
    return dict(
        cq=sl(0), ckv=sl(1), kr=_pad_blocks(sl(2), 1, 1, QK_ROPE, offset=HEAD), pool=sl(3),
        z=_pad_blocks(sl(4), 1, N_HEADS, HEAD), xbc=xbc_pad, dt=_pad_blocks(sl(6), 1, 1, N_HEADS),
        lru_g=sl(7), lru_x=sl(8), gates=sl(9))


def _w_in_ungroup(d):
    xbc = d["xbc"]
    w = N_HEADS * LANES
    xbc_real = jnp.concatenate([_unpad_blocks(xbc[:, :w], 1, N_HEADS, HEAD),
                                _unpad_blocks(xbc[:, w:w + 2 * LANES], 1, 2, HEAD),
                                _unpad_blocks(xbc[:, w + 2 * LANES:], 1, 2, HEAD)], axis=1)
    return jnp.concatenate([d["cq"], d["ckv"], _unpad_blocks(d["kr"], 1, 1, QK_ROPE, offset=HEAD), d["pool"],
                            _unpad_blocks(d["z"], 1, N_HEADS, HEAD), xbc_real, _unpad_blocks(d["dt"], 1, 1, N_HEADS),
                            d["lru_g"], d["lru_x"], d["gates"]], axis=1)


def _pad_xbc_vec(v):
    return jnp.concatenate([_pad_blocks(v[..., :MIX], -1, N_HEADS, HEAD),
                            _pad_blocks(v[..., MIX:MIX + 2 * HEAD], -1, 2, HEAD),
                            _pad_blocks(v[..., MIX + 2 * HEAD:], -1, 2, HEAD)], axis=-1)


def _unpad_xbc_vec(v):
    w = N_HEADS * LANES
    return jnp.concatenate([_unpad_blocks(v[..., :w], -1, N_HEADS, HEAD),
                            _unpad_blocks(v[..., w:w + 2 * LANES], -1, 2, HEAD),
                            _unpad_blocks(v[..., w + 2 * LANES:], -1, 2, HEAD)], axis=-1)


def _layer_weights(p):
    q = dict(p)
    q["in"] = _w_in_groups(p["w_in"])
    assert sum(g.shape[1] for g in q["in"].values()) == IN_PAD_COLS
    q["in_all"] = jnp.concatenate(list(q["in"].values())
                                  + [jnp.zeros((D_MODEL, IN_ALL_COLS - IN_PAD_COLS), p["w_in"].dtype)], axis=1)
    q["uq"] = _pad_blocks(p["w_uq"], 1, N_HEADS, HEAD + QK_ROPE)
    ukv = p["w_ukv"].reshape(KV_LORA, N_HEADS, 2 * HEAD)
    q["ukv"] = jnp.concatenate([_pad_blocks(ukv[:, :, :HEAD].reshape(KV_LORA, -1), 1, N_HEADS, HEAD),
                                _pad_blocks(ukv[:, :, HEAD:].reshape(KV_LORA, -1), 1, N_HEADS, HEAD)], axis=1)
    q["pool_bd"] = _block_diag(p["w_pool"])
    q["lru_bd"] = jnp.concatenate([_block_diag(p["lru_w_a"]), _block_diag(p["lru_w_i"])], axis=1)
    q["br"] = [_pad_blocks(p["w_branch"][0], 0, N_HEADS, HEAD), p["w_branch"][1],
               _pad_blocks(p["w_branch"][2], 0, N_HEADS, HEAD), p["w_branch"][3]]
    q["ssd_conv_w_pad"] = _pad_xbc_vec(p["ssd_conv_w"])
    q["ssd_conv_b_pad"] = _pad_xbc_vec(p["ssd_conv_b"])[None, :]
    q["ssd_norm_pad"] = _pad_blocks(p["ssd_norm"], 0, N_HEADS, HEAD)[None, :]
    return q


def _row(v):
    return v.reshape(1, -1)


def _scal3(v):
    return v.reshape(N_HEADS, 1, 1)


def _layer_fwd(x, p_emb, w, rope, tag):
    n = lambda s: f"{s}_{tag}"
    sv = {"x": x}
    h = _rms_fwd(x, _row(w["g_mix"]), name=n("rms_mix"))
    sv["h"] = h
    u = {k: _mm(h, wk, name=n("in_" + k)) for k, wk in w["in"].items()}
    sv["u"] = u

    cqn = _rms_fwd(u["cq"], _row(w["q_norm"]), name=n("rms_q"))
    ckvn = _rms_fwd(u["ckv"], _row(w["kv_norm"]), name=n("rms_kv"))
    q_pad = _mm(cqn, w["uq"], name=n("uq"))
    kv2 = _mm(ckvn, w["ukv"], name=n("ukv"))
    qc, kc, vc = _att_prep(q_pad, kv2, u["kr"], *rope, name=n("att_prep"))
    y_a, lse = _flash_fwd(qc, kc, vc, name=n("flash_fwd"))
    sv.update(cqn=cqn, ckvn=ckvn, qc=qc, kc=kc, vc=vc, y_a=y_a, lse=lse)

    pool_d = _pool_fwd(u["pool"], name=n("pool_fwd"))
    yb_pre, y_b = _mm(pool_d, w["pool_bd"], epilogue=lambda acc, sc: (acc, acc * sc),
                      rowvecs=[_row(w["pool_scale"])], out_dtypes=(F32, BF16), name=n("pool_mm"))
    sv.update(pool_d=pool_d, yb_pre=yb_pre, y_b=y_b)

    xbc_c = _conv_fwd(u["xbc"], w["ssd_conv_w_pad"], w["ssd_conv_b_pad"], silu=True, name=n("ssd_conv"))
    dt8 = u["dt"][:, :N_HEADS]
    dtcol = dt8.T[:, :, None]
    dtrow = dt8.T[:, None, :]
    ssd_par = (_scal3(w["ssd_dt_bias"]), _scal3(w["ssd_a_log"]), _scal3(w["ssd_d"]))
    y_ssd, states = _ssd_fwd(xbc_c, dtcol, dtrow, *ssd_par, name=n("ssd_fwd"))

    def ssd_post(yv, zv, gv):
        xh, _ = _rms_parts(yv * _silu(zv), MIX)
        return xh * gv

    y_c = _rowwise(ssd_post, [y_ssd, u["z"]], [w["ssd_norm_pad"]], [(N_HEADS * LANES, BF16, "row")], name=n("ssd_post"))
    sv.update(xbc_c=xbc_c, dtcol=dtcol, dtrow=dtrow, y_ssd=y_ssd, states=states, y_c=y_c)

    xc = _conv_fwd(u["lru_x"], w["lru_conv_w"], _row(w["lru_conv_b"]), silu=False, name=n("lru_conv"))
    pre = _mm(xc, w["lru_bd"], name=n("lru_mm"))
    lru_par = (_row(w["lru_lambda"]), _row(w["lru_b_a"]), _row(w["lru_b_i"]))
    y_d, h_lru = _lru_fwd(pre, xc, u["lru_g"], *lru_par, name=n("lru_fwd"))
    sv.update(xc=xc, pre=pre, h_lru=h_lru, y_d=y_d)

    ys = [y_a, y_b, y_c, y_d]
    merged, ybs = None, []
    for b in range(4):
        if merged is None:
            merged, yb = _mm(ys[b], w["br"][b], epilogue=lambda acc, gt: (_sigmoid(gt) * acc, acc),
                             tiles=[u["gates"]], tile_offsets=(b * D_MODEL,), out_dtypes=(F32, F32), name=n(f"branch{b}"))
        else:
            merged, yb = _mm(ys[b], w["br"][b], epilogue=lambda acc, gt, mg: (mg + _sigmoid(gt) * acc, acc),
                             tiles=[u["gates"], merged], tile_offsets=(b * D_MODEL, 0), out_dtypes=(F32, F32),
                             name=n(f"branch{b}"))
        ybs.append(yb)
    x1 = _mm(merged, w["w_out"], epilogue=lambda acc, xr: (acc + xr,), tiles=[x], name=n("out_proj"))
    sv.update(ybs=ybs, merged=merged, x1=x1)

    h2 = _rms_fwd(x1, _row(w["g_mlp"]), name=n("rms_mlp"))
    a_ff, f_ff = _mm(h2, w["w_ff1"], epilogue=lambda acc: (acc, jnp.square(jnp.maximum(acc, 0.0))),
                     out_dtypes=(F32, BF16), name=n("ff1"))
    x2 = _mm(f_ff, w["w_ff2"], epilogue=lambda acc, xr: (acc + xr,), tiles=[x1], name=n("ff2"))
    sv.update(h2=h2, a_ff=a_ff, f_ff=f_ff, x2=x2)

    h3 = _rms_fwd(x2, _row(w["g_ple"]), name=n("rms_ple"))
    e_ple = _mm(p_emb, w["w_ple"], name=n("ple_emb"))
    x3, gt_ple = _mm(h3, w["w_ple_gate"], epilogue=lambda acc, ev, xr: (xr + ev * _sigmoid(acc), _sigmoid(acc)),
                     tiles=[e_ple, x2], out_dtypes=(F32, F32), name=n("ple_gate"))
    sv.update(h3=h3, e_ple=e_ple, gt_ple=gt_ple, p_emb=p_emb)
    return x3, sv


def _layer_bwd(dx3, sv, w, rope, tag):
    n = lambda s: f"{s}_{tag}"
    gr = {}
    u = sv["u"]

    de, dpre = _rowwise(lambda d, gt, ev: (d * gt, d * ev * gt * (1.0 - gt)), [dx3, sv["gt_ple"], sv["e_ple"]], [],
                        [(D_MODEL, BF16, "row"), (D_MODEL, BF16, "row")], name=n("ple_bwd"))
    gr["w_ple"] = _mm(sv["p_emb"], de, ta=True, name=n("d_w_ple"))
    gr["w_ple_gate"] = _mm(sv["h3"], dpre, ta=True, name=n("d_w_ple_gate"))
    dh3 = _mm(dpre, w["w_ple_gate"], tb=True, out_dtypes=(BF16,), name=n("d_h3"))
    dx2, dg = _rms_bwd(sv["x2"], _row(w["g_ple"]), dh3, dx3, name=n("rms_ple_bwd"))
    gr["g_ple"] = dg[0]

    gr["w_ff2"] = _mm(sv["f_ff"], dx2, ta=True, name=n("d_w_ff2"))
    da = _mm(dx2, w["w_ff2"], tb=True, epilogue=lambda acc, av: (acc * 2.0 * jnp.maximum(av, 0.0),),
             tiles=[sv["a_ff"]], out_dtypes=(BF16,), name=n("d_a_ff"))
    gr["w_ff1"] = _mm(sv["h2"], da, ta=True, name=n("d_w_ff1"))
    dh2 = _mm(da, w["w_ff1"], tb=True, out_dtypes=(BF16,), name=n("d_h2"))
    dx1, dg = _rms_bwd(sv["x1"], _row(w["g_mlp"]), dh2, dx2, name=n("rms_mlp_bwd"))
    gr["g_mlp"] = dg[0]

    gr["w_out"] = _mm(sv["merged"], dx1, ta=True, name=n("d_w_out"))
    dmerged = _mm(dx1, w["w_out"], tb=True, name=n("d_merged"))

    def merge_bwd(dm, gts, y0, y1, y2, y3):
        dys, dgs = [], []
        for b, yb in enumerate((y0, y1, y2, y3)):
            sg = _sigmoid(gts[:, b * D_MODEL:(b + 1) * D_MODEL])
            dys.append(dm * sg)
            dgs.append(dm * yb * sg * (1.0 - sg))
        return (*dys, jnp.concatenate(dgs, axis=1))

    *dybs, dgates = _rowwise(merge_bwd, [dmerged, u["gates"]] + sv["ybs"], [],
                             [(D_MODEL, BF16, "row")] * 4 + [(4 * D_MODEL, BF16, "row")], name=n("merge_bwd"))
    ys = [sv["y_a"], sv["y_b"], sv["y_c"], sv["y_d"]]
    dwb = [_mm(ys[b], dybs[b], ta=True, name=n(f"d_w_branch{b}")) for b in range(4)]
    gr["w_branch"] = jnp.stack([_unpad_blocks(dwb[0], 0, N_HEADS, HEAD), dwb[1],
                                _unpad_blocks(dwb[2], 0, N_HEADS, HEAD), dwb[3]])
    dy_a = _mm(dybs[0], w["br"][0], tb=True, out_dtypes=(BF16,), name=n("d_y_a"))
    dy_b = _mm(dybs[1], w["br"][1], tb=True, name=n("d_y_b"))
    dy_c = _mm(dybs[2], w["br"][2], tb=True, name=n("d_y_c"))
    dy_d = _mm(dybs[3], w["br"][3], tb=True, name=n("d_y_d"))
    du = {"gates": dgates}

    lru_par = (_row(w["lru_lambda"]), _row(w["lru_b_a"]), _row(w["lru_b_i"]))
    dpa, dpi, dxc_direct, du["lru_g"], dlam, dba, dbi = _lru_bwd(
        sv["pre"], sv["xc"], u["lru_g"], *lru_par, sv["h_lru"], dy_d, name=n("lru_bwd"))
    dpre_lru = jnp.concatenate([dpa, dpi], axis=1)
    d_bd = _mm(sv["xc"], dpre_lru, ta=True, name=n("d_lru_w"))
    gr["lru_w_a"] = _block_diag_t(d_bd[:, :MIX], N_HEADS)
    gr["lru_w_i"] = _block_diag_t(d_bd[:, MIX:], N_HEADS)
    gr["lru_lambda"], gr["lru_b_a"], gr["lru_b_i"] = dlam[0], dba[0], dbi[0]
    dxc = _mm(dpre_lru, w["lru_bd"], tb=True, epilogue=lambda acc, t: (acc + t,), tiles=[dxc_direct], name=n("d_xc"))
    du["lru_x"], gr["lru_conv_w"], dcb = _conv_bwd(u["lru_x"], w["lru_conv_w"], _row(w["lru_conv_b"]), dxc,
                                                  silu=False, name=n("lru_conv_bwd"))
    gr["lru_conv_b"] = dcb[0]

    def ssd_post_bwd(dyc, yv, zv, gv):
        sz = _silu(zv)
        dyz, dgain = _rms_bwd_math(yv * sz, gv, dyc, MIX)
        return dyz * sz, dyz * yv * _silu_grad(zv), dgain

    dy_ssd, du["z"], dgain = _rowwise(ssd_post_bwd, [dy_c, sv["y_ssd"], u["z"]], [w["ssd_norm_pad"]],
                                      [(N_HEADS * LANES, F32, "row"), (N_HEADS * LANES, BF16, "row"),
                                       (N_HEADS * LANES, F32, "acc")], name=n("ssd_post_bwd"))
    gr["ssd_norm"] = _unpad_blocks(dgain[0], 0, N_HEADS, HEAD)
    ssd_par = (_scal3(w["ssd_dt_bias"]), _scal3(w["ssd_a_log"]), _scal3(w["ssd_d"]))
    dxs, dbh, dch, ddt, dbias, dalog, dd = _ssd_bwd(sv["xbc_c"], sv["dtcol"], sv["dtrow"], *ssd_par, sv["states"],
                                                    dy_ssd, name=n("ssd_bwd"))
    s = dxs.shape[0]
    group_sum = lambda t: t.reshape(2, 4, s, LANES).sum(axis=1).transpose(1, 0, 2).reshape(s, 2 * LANES)
    dxbc_c = jnp.concatenate([dxs, group_sum(dbh), group_sum(dch)], axis=1)
    gr["ssd_dt_bias"], gr["ssd_a_log"], gr["ssd_d"] = dbias[:, 0, 0], dalog[:, 0, 0], dd[:, 0, 0]
    du["xbc"], dcw, dcb = _conv_bwd(u["xbc"], w["ssd_conv_w_pad"], w["ssd_conv_b_pad"], dxbc_c, silu=True,
                                    name=n("ssd_conv_bwd"))
    gr["ssd_conv_w"], gr["ssd_conv_b"] = _unpad_xbc_vec(dcw), _unpad_xbc_vec(dcb[0])
    du["dt"] = jnp.pad(ddt[:, :, 0].T, ((0, 0), (0, LANES - N_HEADS)))

    dyb_pre, dscale = _rowwise(lambda d, yp, sc: (d * sc, _colsum(d * yp)), [dy_b, sv["yb_pre"]],
                               [_row(w["pool_scale"])], [(MIX, BF16, "row"), (MIX, F32, "acc")], name=n("pool_scale_bwd"))
    gr["pool_scale"] = dscale[0]
    gr["w_pool"] = _block_diag_t(_mm(sv["pool_d"], dyb_pre, ta=True, name=n("d_w_pool")), 4)
    dd_pool = _mm(dyb_pre, w["pool_bd"], tb=True, name=n("d_pool_d"))
    du["pool"] = _pool_bwd(dd_pool, name=n("pool_bwd"))

    dqc, delta = _flash_bwd_dq(sv["qc"], sv["kc"], sv["vc"], sv["y_a"], dy_a, sv["lse"], name=n("flash_dq"))
    to_row = lambda t: t.reshape(N_HEADS, 1, s)
    dkc, dvc = _flash_bwd_dkv(sv["qc"], sv["kc"], sv["vc"], dy_a, to_row(sv["lse"]), to_row(delta), name=n("flash_dkv"))
    dq_pad, du["kr"] = _att_prep_bwd(dqc, dkc, *rope, name=n("att_prep_bwd"))
    d_uq = _mm(sv["cqn"], dq_pad, ta=True, name=n("d_w_uq"))
    gr["w_uq"] = _unpad_blocks(d_uq, 1, N_HEADS, HEAD + QK_ROPE)
    dcqn = _mm(dq_pad, w["uq"], tb=True, out_dtypes=(BF16,), name=n("d_cqn"))
    du["cq"], dg = _rms_bwd(u["cq"], _row(w["q_norm"]), dcqn, name=n("rms_q_bwd"))
    gr["q_norm"] = dg[0]
    dkv2 = jnp.concatenate([dkc, dvc], axis=1).astype(BF16)
    d_ukv = _mm(sv["ckvn"], dkv2, ta=True, name=n("d_w_ukv"))
    wk = N_HEADS * LANES
    dk_real = _unpad_blocks(d_ukv[:, :wk], 1, N_HEADS, HEAD).reshape(KV_LORA, N_HEADS, HEAD)
    dv_real = _unpad_blocks(d_ukv[:, wk:], 1, N_HEADS, HEAD).reshape(KV_LORA, N_HEADS, HEAD)
    gr["w_ukv"] = jnp.concatenate([dk_real, dv_real], axis=2).reshape(KV_LORA, N_HEADS * 2 * HEAD)
    dckvn = _mm(dkv2, w["ukv"], tb=True, out_dtypes=(BF16,), name=n("d_ckvn"))
    du["ckv"], dg = _rms_bwd(u["ckv"], _row(w["kv_norm"]), dckvn, name=n("rms_kv_bwd"))
    gr["kv_norm"] = dg[0]

    dw_in = {k: _mm(sv["h"], du[k], ta=True, name=n("d_w_in_" + k)) for k in w["in"]}
    gr["w_in"] = _w_in_ungroup(dw_in)
    s = dx3.shape[0]
    du_all = jnp.concatenate([du[k].astype(BF16) for k in w["in"]]
                             + [jnp.zeros((s, w["in_all"].shape[1] - IN_PAD_COLS), BF16)], axis=1)
    dh = _mm(du_all, w["in_all"], tb=True, name=n("d_h"))
    dx, dg = _rms_bwd(sv["x"], _row(w["g_mix"]), dh, dx1, name=n("rms_mix_bwd"))
    gr["g_mix"] = dg[0]
    return dx, gr


def _pack_rows(n_elems):
    per = PACK_W * PACK_ROWS
    return -(-n_elems // per) * PACK_ROWS


def _pack_flat(parts, dtype):
    flat = jnp.concatenate([p.reshape(-1).astype(dtype) for p in parts])
    rows = _pack_rows(flat.shape[0])
    return jnp.pad(flat, (0, rows * PACK_W - flat.shape[0])).reshape(rows, PACK_W)


def _unpack_flat(buf, shapes):
    lead = buf.shape[:-2]
    flat = buf.reshape(lead + (-1,))
    out, off = [], 0
    for shp in shapes:
        size = int(np.prod(shp))
        out.append(flat[..., off:off + size].reshape(lead + tuple(shp)))
        off += size
    return out


def _merge_shards(t, axis):
    return jnp.concatenate([t[i] for i in range(4)], axis=axis)


def _split_shards(t, axis):
    return jnp.stack(jnp.split(t, 4, axis=axis))


def _rope_tables(positions):
    inv = 1.0 / (ROPE_THETA ** (jnp.arange(0, QK_ROPE, 2, dtype=F32) / QK_ROPE))
    ang = positions.astype(F32)[:, None] * inv
    cos, sin = jnp.cos(ang), jnp.sin(ang)
    s = ang.shape[0]
    half = QK_ROPE // 2
    z = lambda n_: jnp.zeros((s, n_), F32)
    cos_t = jnp.concatenate([jnp.ones((s, HEAD), F32), cos, cos, jnp.ones((s, LANES - HEAD - QK_ROPE), F32)], axis=1)
    sin_p = jnp.concatenate([z(HEAD + half), sin, z(LANES - HEAD - QK_ROPE)], axis=1)
    sin_m = jnp.concatenate([z(HEAD), -sin, z(half + LANES - HEAD - QK_ROPE)], axis=1)
    return cos_t, sin_p, sin_m


def _loss_head(x, g, target, *, name):
    d = x.shape[1]

    def fn(xv, tv, gv):
        xh, r = _rms_parts(xv, d)
        y = xh * gv
        err = y - tv
        dy = err * (1.0 / d)
        dxh = dy * gv
        dx = r * (dxh - xh * (jnp.sum(dxh * xh, axis=-1, keepdims=True) * (1.0 / d)))
        return dx, _colsum(dy * xh), _colsum(err * err) * (0.5 / d)

    return _rowwise(fn, [x, target], [g], [(d, F32, "row"), (d, F32, "acc"), (d, F32, "acc")], name=name)


def _step(args):
    x = args["x"][0]
    c_idx = lax.axis_index("c")

    mats = [(nm, ax) for nm, ax in BIG if nm not in CONV_SHARDED]
    gathered = _gather_layers([args[nm].astype(BF16) for nm, _ in mats])
    full = {nm: [_merge_shards(t[:, l], ax - 1) for l in range(2)] for (nm, ax), t in zip(mats, gathered)}
    convs = [(nm, ax) for nm, ax in BIG if nm in CONV_SHARDED]
    conv_all = _gather_all(_pack_flat([args[nm] for nm, _ in convs], F32), name="gather_conv_taps")[0::2]
    for (nm, ax), t in zip(convs, _unpack_flat(conv_all, [args[nm].shape for nm, _ in convs])):
        full[nm] = _merge_shards(t, ax)
    rope = _rope_tables(args["positions"][0])

    layers = []
    for l in range(2):
        p = {nm: full[nm][l] for nm, _ in BIG}
        p.update({nm: args[nm][l] for nm in SMALL if nm != "g_final"})
        layers.append(_layer_weights(p))

    saved = []
    for l in range(2):
        x, sv = _layer_fwd(x, args["p"][l, 0], layers[l], rope, f"l{l}")
        saved.append(sv)

    dx, dg_final, loss_part = _loss_head(x, _row(args["g_final"]), args["loss_target"][0], name="loss_head")
    loss = lax.psum(jnp.sum(loss_part), ("x", "y", "c"))

    grads = [None, None]
    for l in (1, 0):
        dx, grads[l] = _layer_bwd(dx, saved[l], layers[l], rope, f"l{l}")

    g_local = {nm: jnp.stack([grads[0][nm], grads[1][nm]]) for nm in SMALL if nm != "g_final"}
    g_local["g_final"] = dg_final[0]

    view2d = lambda t: t.reshape(-1, t.shape[-1])
    names = [nm for nm, _ in BIG]
    got = _send_other_layer([view2d(grads[0][nm]) for nm in names], [view2d(grads[1][nm]) for nm in names])
    parts = []
    for (nm, ax), gt in zip(BIG, got):
        both = _chip_sum(view2d(grads[0][nm]), view2d(grads[1][nm]), gt, c_idx, name="chip_sum_" + nm)
        shards = _split_shards(both.reshape(grads[0][nm].shape), ax - 1)
        parts.append(shards.reshape(4, -1, shards.shape[-1]))
    upd, g_red = {}, {}
    for nm, slots in zip(names, _scatter_join(parts)):
        g_red[nm], *upd[nm] = _adamw_slots(args[nm], slots, args["m_" + nm], args["v_" + nm], name="adamw_" + nm)

    small_shapes = [args[nm].shape for nm in SMALL]
    small_sum = _sum_slots(_gather_all(_pack_flat([g_local[nm] for nm in SMALL], F32), name="gather_small_grads"),
                           name="sum_devices")
    g_red.update(zip(SMALL, _unpack_flat(small_sum, small_shapes)))

    pack_small = lambda pre: _pack_flat([args[pre + nm] for nm in SMALL], F32)
    upd_small = _adamw(pack_small(""), small_sum, pack_small("m_"), pack_small("v_"), name="adamw_small")
    upd.update({nm: trip for nm, trip in zip(SMALL, zip(*[_unpack_flat(t, small_shapes) for t in upd_small]))})

    outs = [loss, dx[None]]
    outs += [g_red[nm] for nm in WEIGHTS]
    for i in range(3):
        outs += [upd[nm][i] for nm in WEIGHTS]
    return tuple(outs)


_ARG_NAMES = ("x", "p", "positions") + WEIGHTS + ("loss_target",) + tuple("m_" + nm for nm in WEIGHTS) \
    + tuple("v_" + nm for nm in WEIGHTS)


def kernel(*arrays):
    assert len(arrays) == len(_ARG_NAMES), len(arrays)
    return _step(dict(zip(_ARG_NAMES, arrays)))
```

```python
import functools
import math

import jax
import jax.numpy as jnp
import numpy as np
from jax import lax
from jax.experimental import pallas as pl
from jax.experimental.pallas import tpu as pltpu

F32 = jnp.float32
BF16 = jnp.bfloat16
MXU_DTYPE = BF16
LANES = 128
VMEM_LIMIT = 56 * 1024 * 1024
MM_VMEM_BUDGET = 36 * 1024 * 1024
ELEMENTWISE_BLOCK_BYTES = 2 * 1024 * 1024

D_MODEL = 1024
N_HEADS = 8
HEAD = 64
QK_ROPE = 32
Q_LORA = 384
KV_LORA = 256
MIX = 512
SSD_CHUNK = 128
CONV_W = 4
POOL_WINDOWS = (2, 4, 8, 16)
LRU_C = 8.0
EPS = 1e-6
ROPE_THETA = 10000.0
ATT_SCALE = (HEAD + QK_ROPE) ** -0.5
SPLIT_SIZES = (Q_LORA, KV_LORA, QK_ROPE, MIX, MIX, 768, N_HEADS, MIX, MIX, 4 * D_MODEL)
IN_LAYOUT = (("gates", 0, 4096), ("z", 4096, 1024), ("pool", 5120, 512), ("lru_g", 5632, 512), ("lru_x", 6144, 512),
             ("cq", 6912, 384), ("ckv", 7424, 256), ("xbc", 7680, 1536), ("kr", 9216, 128), ("dt", 9344, 128))
IN_OFFSETS = {name: (off, width) for name, off, width in IN_LAYOUT}
IN_ALL_COLS = 9728

ADAM_LR, ADAM_B1, ADAM_B2, ADAM_EPS, ADAM_WD, ADAM_STEP = 0.001, 0.9, 0.999, 1e-08, 0.01, 10

BIG = (("w_in", 2), ("w_uq", 2), ("w_ukv", 2), ("ssd_conv_w", 2), ("lru_conv_w", 2), ("w_branch", 3),
       ("w_out", 1), ("w_ff1", 2), ("w_ff2", 1), ("w_ple_gate", 1), ("w_ple", 2))
SMALL = ("g_mix", "q_norm", "kv_norm", "w_pool", "pool_scale", "ssd_conv_b", "ssd_dt_bias", "ssd_a_log",
         "ssd_d", "ssd_norm", "lru_conv_b", "lru_w_a", "lru_b_a", "lru_w_i", "lru_b_i", "lru_lambda",
         "g_mlp", "g_ple", "g_final")
WEIGHTS = ("g_mix", "w_in", "q_norm", "w_uq", "kv_norm", "w_ukv", "w_pool", "pool_scale", "ssd_conv_w",
           "ssd_conv_b", "ssd_dt_bias", "ssd_a_log", "ssd_d", "ssd_norm", "lru_conv_w", "lru_conv_b", "lru_w_a",
           "lru_b_a", "lru_w_i", "lru_b_i", "lru_lambda", "w_branch", "w_out", "g_mlp", "w_ff1", "w_ff2", "g_ple",
           "w_ple_gate", "w_ple", "g_final")
CONV_SHARDED = ("ssd_conv_w", "lru_conv_w")
PACK_W = 1024
PACK_ROWS = 64


def _cparams(sem, vmem=VMEM_LIMIT):
    return pltpu.CompilerParams(dimension_semantics=sem, vmem_limit_bytes=vmem)


def _pick(n, cands):
    for c in cands:
        if n % c == 0:
            return c
    return n


class _Cols:
    def __init__(self, arr, off, width):
        self.arr, self.off, self.width = arr, off, width

    shape = property(lambda self: (self.arr.shape[0], self.width))
    dtype = property(lambda self: self.arr.dtype)


def _arr(x):
    return x.arr if isinstance(x, _Cols) else x


def _off(x, unit):
    off = x.off if isinstance(x, _Cols) else 0
    assert off % unit == 0, (off, unit)
    return off // unit


def _sigmoid(x):
    return 1.0 / (1.0 + jnp.exp(-x))


def _silu(x):
    return x * _sigmoid(x)


def _silu_grad(x):
    s = _sigmoid(x)
    return s * (1.0 + x * (1.0 - s))


def _softplus(x):
    e = jnp.exp(-jnp.abs(x))
    log1p_e = jnp.where(e < 1e-3, e * (1.0 - e * (0.5 - e * (1.0 / 3.0))), jnp.log(1.0 + e))
    return jnp.maximum(x, 0.0) + log1p_e


_GELU_C = math.sqrt(2.0 / math.pi)


def _gelu(x):
    t = jnp.tanh(_GELU_C * (x + 0.044715 * x * x * x))
    return 0.5 * x * (1.0 + t)


def _gelu_grad(x):
    t = jnp.tanh(_GELU_C * (x + 0.044715 * x * x * x))
    return 0.5 * (1.0 + t) + 0.5 * x * (1.0 - t * t) * _GELU_C * (1.0 + 3.0 * 0.044715 * x * x)


def _neg_expm1(x):
    series = -x * (1.0 + 0.5 * x * (1.0 + (1.0 / 3.0) * x * (1.0 + 0.25 * x)))
    return jnp.where(x > -0.05, series, 1.0 - jnp.exp(x))


def _shift_down(x, k, row):
    return jnp.where(row >= k, pltpu.roll(x, k, 0), 0.0)


def _shift_up(x, k, row):
    n = x.shape[0]
    return jnp.where(row < n - k, pltpu.roll(x, n - k, 0), 0.0)


def _cumsum_rows(x, row):
    d = 1
    while d < x.shape[0]:
        x = x + _shift_down(x, d, row)
        d *= 2
    return x


def _rev_cumsum_rows(x, row):
    d = 1
    while d < x.shape[0]:
        x = x + _shift_up(x, d, row)
        d *= 2
    return x


def _cumsum_lanes(x, col):
    d = 1
    while d < x.shape[1]:
        x = x + jnp.where(col >= d, pltpu.roll(x, d, 1), 0.0)
        d *= 2
    return x


def _dot(a, b, ta=False, tb=False):
    dn = (((0 if ta else 1,), (1 if tb else 0,)), ((), ()))
    return lax.dot_general(a.astype(MXU_DTYPE), b.astype(MXU_DTYPE), dn, preferred_element_type=F32)


def _mm_tiles(m, n, k, a_bytes, b_bytes, mn_bytes):
    best = None
    for tm in (1024, 512, 384, 256, 128):
        for tn in (1024, 512, 384, 256, 128):
            for tk in (2048, 1024, 512, 384, 256, 128):
                if m % tm or n % tn or k % tk:
                    continue
                vmem = 2 * (tm * tk * a_bytes + tk * tn * b_bytes) + 2 * tm * tn * mn_bytes + 4 * tm * tn
                vmem += 2 * (tm * tk + tk * tn)
                if vmem > MM_VMEM_BUDGET:
                    continue
                steps = (m // tm) * (n // tn) * (k // tk)
                key = (steps, vmem)
                if best is None or key < best[0]:
                    best = (key, (tm, tn, tk))
    assert best is not None, (m, n, k)
    return best[1]


def _mm(a, b, *, ta=False, tb=False, epilogue=None, tiles=(), rowvecs=(), out_dtypes=(F32,), name):
    m, k = (a.shape[1], a.shape[0]) if ta else a.shape
    n = b.shape[0] if tb else b.shape[1]
    assert (b.shape[1] if tb else b.shape[0]) == k, (a.shape, b.shape, ta, tb)
    mn_bytes = sum(t.dtype.itemsize for t in tiles) + sum(jnp.dtype(dt).itemsize for dt in out_dtypes)
    tm, tn, tk = _mm_tiles(m, n, k, a.dtype.itemsize, b.dtype.itemsize, mn_bytes)
    nk = k // tk
    nt, nr, no = len(tiles), len(rowvecs), len(out_dtypes)

    def body(*refs):
        a_ref, b_ref = refs[0], refs[1]
        tile_refs = refs[2:2 + nt]
        row_refs = refs[2 + nt:2 + nt + nr]
        out_refs = refs[2 + nt + nr:2 + nt + nr + no]
        acc_ref = refs[-1]
        kk = pl.program_id(2)

        @pl.when(kk == 0)
        def _():
            acc_ref[...] = jnp.zeros_like(acc_ref)

        acc_ref[...] += _dot(a_ref[...], b_ref[...], ta, tb)

        @pl.when(kk == nk - 1)
        def _():
            acc = acc_ref[...]
            if epilogue is None:
                outs = (acc,)
            else:
                outs = epilogue(acc, *[t[...] for t in tile_refs], *[r[...] for r in row_refs])
            for o_ref, o in zip(out_refs, outs):
                o_ref[...] = o.astype(o_ref.dtype)

    a_spec = pl.BlockSpec((tk, tm), lambda i, j, kk: (kk, i)) if ta else pl.BlockSpec((tm, tk), lambda i, j, kk: (i, kk))
    b_spec = pl.BlockSpec((tn, tk), lambda i, j, kk: (j, kk)) if tb else pl.BlockSpec((tk, tn), lambda i, j, kk: (kk, j))
    mn_spec = pl.BlockSpec((tm, tn), lambda i, j, kk: (i, j))
    row_spec = pl.BlockSpec((1, tn), lambda i, j, kk: (0, j))
    tile_specs = [pl.BlockSpec((tm, tn), lambda i, j, kk, ob=_off(t, tn): (i, j + ob)) for t in tiles]
    outs = pl.pallas_call(
        body, name=name,
        grid=(m // tm, n // tn, nk),
        in_specs=[a_spec, b_spec] + tile_specs + [row_spec] * nr,
        out_specs=[mn_spec] * no,
        out_shape=[jax.ShapeDtypeStruct((m, n), dt) for dt in out_dtypes],
        scratch_shapes=[pltpu.VMEM((tm, tn), F32)],
        compiler_params=_cparams(("parallel", "parallel", "arbitrary")),
    )(a, b, *[_arr(t) for t in tiles], *rowvecs)
    return outs[0] if no == 1 else tuple(outs)


def _rowwise(fn, rows, fulls, outs, *, name, tm=None):
    r = rows[0].shape[0]
    if tm is None:
        widest = max([x.shape[1] for x in rows] + [o[0] for o in outs])
        tm = _pick(r, (max(8, min(512, (512 * 1024) // widest)), 256, 128, 64, 32, 16, 8))
    nrow, nfull, nout = len(rows), len(fulls), len(outs)

    def body(*refs):
        row_refs = refs[:nrow]
        full_refs = refs[nrow:nrow + nfull]
        out_refs = refs[nrow + nfull:]
        res = fn(*[x[...] for x in row_refs], *[x[...] for x in full_refs])
        if not isinstance(res, (tuple, list)):
            res = (res,)
        step = pl.program_id(0)
        for o_ref, o, spec in zip(out_refs, res, outs):
            if spec[2] == "row":
                o_ref[...] = o.astype(o_ref.dtype)
            else:
                @pl.when(step == 0)
                def _(o_ref=o_ref):
                    o_ref[...] = jnp.zeros_like(o_ref)
                o_ref[...] += o

    in_specs = [pl.BlockSpec((tm, x.shape[1]), lambda i, ob=_off(x, x.shape[1]): (i, ob)) for x in rows]
    in_specs += [pl.BlockSpec(x.shape, lambda i, nd=x.ndim: (0,) * nd) for x in fulls]
    out_specs, out_shape = [], []
    for c, dt, kind in outs:
        if kind == "row":
            out_specs.append(pl.BlockSpec((tm, c), lambda i: (i, 0)))
            out_shape.append(jax.ShapeDtypeStruct((r, c), dt))
        else:
            out_specs.append(pl.BlockSpec((1, c), lambda i: (0, 0)))
            out_shape.append(jax.ShapeDtypeStruct((1, c), F32))
    res = pl.pallas_call(
        body, name=name, grid=(r // tm,), in_specs=in_specs, out_specs=out_specs, out_shape=out_shape,
        compiler_params=_cparams(("arbitrary",)),
    )(*[_arr(x) for x in rows], *fulls)
    return res[0] if nout == 1 else tuple(res)


def _colsum(x):
    return jnp.sum(x, axis=0, keepdims=True)


def _rms_parts(x, n_real):
    r = lax.rsqrt(jnp.sum(x * x, axis=-1, keepdims=True) * (1.0 / n_real) + EPS)
    return x * r, r


def _rms_fwd(x, g, *, n_real=None, out_dtype=BF16, name):
    n_real = n_real or x.shape[1]

    def fn(xv, gv):
        xh, _ = _rms_parts(xv, n_real)
        return xh * gv

    return _rowwise(fn, [x], [g], [(x.shape[1], out_dtype, "row")], name=name)


def _rms_bwd_math(xv, gv, dh, n_real):
    xh, r = _rms_parts(xv, n_real)
    dxh = dh * gv
    dx = r * (dxh - xh * (jnp.sum(dxh * xh, axis=-1, keepdims=True) * (1.0 / n_real)))
    return dx, _colsum(dh * xh)


def _rms_bwd(x, g, dh, res=None, *, name):
    n = x.shape[1]
    if res is None:
        def fn(xv, dhv, gv):
            return _rms_bwd_math(xv, gv, dhv.astype(F32), n)
        rows = [x, dh]
    else:
        def fn(xv, dhv, rv, gv):
            dx, dg = _rms_bwd_math(xv, gv, dhv.astype(F32), n)
            return dx + rv, dg
        rows = [x, dh, res]
    return _rowwise(fn, rows, [g], [(n, F32, "row"), (n, F32, "acc")], name=name)


def _seq_call(body, ins, outs, n_blocks, *, name):
    in_specs, args = [], []
    for x, kind in ins:
        in_specs.append(pl.BlockSpec((x.shape[0], LANES), lambda j, ob=_off(x, LANES): (0, j + ob)))
        args.append(_arr(x))
    out_specs, out_shape = [], []
    for shape, dt in outs:
        out_specs.append(pl.BlockSpec((shape[0], LANES), lambda j: (0, j)))
        out_shape.append(jax.ShapeDtypeStruct(shape, dt))
    res = pl.pallas_call(body, name=name, grid=(n_blocks,), in_specs=in_specs, out_specs=out_specs,
                         out_shape=out_shape, compiler_params=_cparams(("parallel",)))(*args)
    return res[0] if len(outs) == 1 else tuple(res)


def _conv_pre(x, w, b, row):
    acc = x * w[CONV_W - 1:CONV_W, :] + b
    for k in range(CONV_W - 1):
        acc = acc + _shift_down(x, CONV_W - 1 - k, row) * w[k:k + 1, :]
    return acc


def _conv_fwd(x, w, b, *, silu, name):
    s, c = x.shape

    def body(x_ref, w_ref, b_ref, y_ref):
        xv = x_ref[...]
        row = lax.broadcasted_iota(jnp.int32, xv.shape, 0)
        pre = _conv_pre(xv, w_ref[...], b_ref[...], row)
        y_ref[...] = _silu(pre) if silu else pre

    return _seq_call(body, [(x, "seq"), (w, "par"), (b, "par")], [((s, c), F32)], c // LANES, name=name)


def _conv_bwd(x, w, b, dy, *, silu, name):
    s, c = x.shape

    def body(x_ref, w_ref, b_ref, dy_ref, dx_ref, dw_ref, db_ref):
        xv, wv, dv = x_ref[...], w_ref[...], dy_ref[...]
        row = lax.broadcasted_iota(jnp.int32, xv.shape, 0)
        if silu:
            dv = dv * _silu_grad(_conv_pre(xv, wv, b_ref[...], row))
        dx = dv * wv[CONV_W - 1:CONV_W, :]
        dws = [None] * CONV_W
        dws[CONV_W - 1] = _colsum(dv * xv)
        for k in range(CONV_W - 1):
            sh = CONV_W - 1 - k
            dx = dx + _shift_up(dv, sh, row) * wv[k:k + 1, :]
            dws[k] = _colsum(dv * _shift_down(xv, sh, row))
        dx_ref[...] = dx
        for k in range(CONV_W):
            dw_ref[k:k + 1, :] = dws[k]
        db_ref[...] = _colsum(dv)

    return _seq_call(body, [(x, "seq"), (w, "par"), (b, "par"), (dy, "seq")],
                     [((s, c), F32), ((CONV_W, c), F32), ((1, c), F32)], c // LANES, name=name)


def _pool_select(levels):
    g = pl.program_id(0)
    return jnp.where(g == 0, levels[0], jnp.where(g == 1, levels[1], jnp.where(g == 2, levels[2], levels[3])))


def _pool_count(row):
    g = pl.program_id(0)
    w = jnp.where(g == 0, POOL_WINDOWS[0], jnp.where(g == 1, POOL_WINDOWS[1],
                                                     jnp.where(g == 2, POOL_WINDOWS[2], POOL_WINDOWS[3])))
    return jnp.minimum(row + 1, w).astype(F32)


def _pool_fwd(u, *, name):
    def body(u_ref, d_ref):
        uv = u_ref[...]
        row = lax.broadcasted_iota(jnp.int32, uv.shape, 0)
        levels, cur, sh = [], uv, 1
        for _ in POOL_WINDOWS:
            cur = cur + _shift_down(cur, sh, row)
            levels.append(cur)
            sh *= 2
        d_ref[...] = _pool_select(levels) / _pool_count(row) - uv

    return _seq_call(body, [(u, "seq")], [(u.shape, F32)], u.shape[1] // LANES, name=name)


def _pool_bwd(dd, *, name):
    def body(dd_ref, du_ref):
        dv = dd_ref[...]
        row = lax.broadcasted_iota(jnp.int32, dv.shape, 0)
        levels, cur, sh = [], dv / _pool_count(row), 1
        for _ in POOL_WINDOWS:
            cur = cur + _shift_up(cur, sh, row)
            levels.append(cur)
            sh *= 2
        du_ref[...] = _pool_select(levels) - dv

    return _seq_call(body, [(dd, "seq")], [(dd.shape, F32)], dd.shape[1] // LANES, name=name)


def _lru_gates(pre_a, pre_i, xc, lam, b_a, b_i):
    r = _sigmoid(pre_a + b_a)
    i = _sigmoid(pre_i + b_i)
    sp = _softplus(-lam)
    log_a = -LRU_C * r * sp
    a = jnp.exp(log_a)
    mult = jnp.sqrt(_neg_expm1(2.0 * log_a))
    return r, i, sp, a, mult


def _lru_fwd(pre, xc, gate_in, lam, b_a, b_i, *, name):
    s, c = xc.shape
    nb = c // LANES

    def body(pa_ref, pi_ref, xc_ref, g_ref, lam_ref, ba_ref, bi_ref, y_ref, h_ref):
        xv = xc_ref[...]
        row = lax.broadcasted_iota(jnp.int32, xv.shape, 0)
        _, i, _, a, mult = _lru_gates(pa_ref[...], pi_ref[...], xv, lam_ref[...], ba_ref[...], bi_ref[...])
        h = xv * i * mult
        d = 1
        while d < s:
            h = h + a * _shift_down(h, d, row)
            a = a * jnp.where(row >= d, pltpu.roll(a, d, 0), 1.0)
            d *= 2
        h_ref[...] = h
        y_ref[...] = h * _gelu(g_ref[...])

    blk = lambda off: pl.BlockSpec((s, LANES), lambda j: (0, j + off))
    par = pl.BlockSpec((1, LANES), lambda j: (0, j))
    return pl.pallas_call(
        body, name=name, grid=(nb,),
        in_specs=[blk(0), blk(nb), blk(0), blk(_off(gate_in, LANES)), par, par, par],
        out_specs=[blk(0), blk(0)],
        out_shape=[jax.ShapeDtypeStruct((s, c), F32)] * 2,
        compiler_params=_cparams(("parallel",)),
    )(pre, pre, xc, _arr(gate_in), lam, b_a, b_i)


def _lru_bwd(pre, xc, gate_in, lam, b_a, b_i, h, dy, *, name):
    s, c = xc.shape
    nb = c // LANES

    def body(pa_ref, pi_ref, xc_ref, g_ref, lam_ref, ba_ref, bi_ref, h_ref, dy_ref,
             dpa_ref, dpi_ref, dxc_ref, dg_ref, dlam_ref, dba_ref, dbi_ref):
        xv, gv, hv, dv = xc_ref[...], g_ref[...], h_ref[...], dy_ref[...]
        row = lax.broadcasted_iota(jnp.int32, xv.shape, 0)
        r, i, sp, a, mult = _lru_gates(pa_ref[...], pi_ref[...], xv, lam_ref[...], ba_ref[...], bi_ref[...])
        dg_ref[...] = dv * hv * _gelu_grad(gv)
        dh = dv * _gelu(gv)
        an = jnp.where(row < s - 1, pltpu.roll(a, s - 1, 0), 0.0)
        d = 1
        while d < s:
            dh = dh + an * _shift_up(dh, d, row)
            an = an * jnp.where(row < s - d, pltpu.roll(an, s - d, 0), 1.0)
            d *= 2
        da = dh * _shift_down(hv, 1, row)
        dxc_ref[...] = dh * i * mult
        di = dh * xv * mult
        dmult = dh * xv * i
        dlog_a = (da - dmult * a / mult) * a
        dr = dlog_a * (-LRU_C) * sp
        dlam_ref[...] = _colsum(dlog_a * LRU_C * r * _sigmoid(-lam_ref[...]))
        dpa = dr * r * (1.0 - r)
        dpi = di * i * (1.0 - i)
        dpa_ref[...] = dpa
        dpi_ref[...] = dpi
        dba_ref[...] = _colsum(dpa)
        dbi_ref[...] = _colsum(dpi)

    blk = lambda off: pl.BlockSpec((s, LANES), lambda j: (0, j + off))
    par = pl.BlockSpec((1, LANES), lambda j: (0, j))
    sc = jax.ShapeDtypeStruct((s, c), F32)
    pc = jax.ShapeDtypeStruct((1, c), F32)
    dpa, dpi, dxc, dg, dlam, dba, dbi = pl.pallas_call(
        body, name=name, grid=(nb,),
        in_specs=[blk(0), blk(nb), blk(0), blk(_off(gate_in, LANES)), par, par, par, blk(0), blk(0)],
        out_specs=[blk(0), blk(0), blk(0), blk(0), par, par, par],
        out_shape=[sc, sc, sc, sc, pc, pc, pc],
        compiler_params=_cparams(("parallel",)),
    )(pre, pre, xc, _arr(gate_in), lam, b_a, b_i, h, dy)
    return dpa, dpi, dxc, dg, dlam, dba, dbi


GROUP_HEADS = 4


def _ssd_specs(nc, order):
    gw = GROUP_HEADS * LANES
    return dict(
        x=pl.BlockSpec((SSD_CHUNK, gw), lambda g, ci: (order(ci), g)),
        b=pl.BlockSpec((SSD_CHUNK, LANES), lambda g, ci: (order(ci), N_HEADS + g)),
        c=pl.BlockSpec((SSD_CHUNK, LANES), lambda g, ci: (order(ci), N_HEADS + 2 + g)),
        dtcol=pl.BlockSpec((GROUP_HEADS, SSD_CHUNK, 1), lambda g, ci: (g, order(ci), 0)),
        dtrow=pl.BlockSpec((GROUP_HEADS, 1, SSD_CHUNK), lambda g, ci: (g, 0, order(ci))),
        scal=pl.BlockSpec((GROUP_HEADS, 1, 1), lambda g, ci: (g, 0, 0)),
        state=pl.BlockSpec((GROUP_HEADS, 1, LANES, LANES), lambda g, ci: (g, order(ci), 0, 0)),
        group=pl.BlockSpec((SSD_CHUNK, LANES), lambda g, ci: (order(ci), g)),
        pacc=pl.BlockSpec((GROUP_HEADS, 1, LANES), lambda g, ci: (g, 0, 0)),
    )


def _ssd_chunk_terms(dtcol, dtrow, bias, a_log):
    shp = (SSD_CHUNK, SSD_CHUNK)
    row = lax.broadcasted_iota(jnp.int32, shp, 0)
    col = lax.broadcasted_iota(jnp.int32, shp, 1)
    a_head = -jnp.exp(a_log)
    dt_c = jnp.broadcast_to(_softplus(dtcol + bias), shp)
    dt_r = jnp.broadcast_to(_softplus(dtrow + bias), shp)
    cs_c = _cumsum_rows(dt_c * a_head, row)
    cs_r = _cumsum_lanes(dt_r * a_head, col)
    cs_last = jnp.sum(jnp.where(row == SSD_CHUNK - 1, cs_c, 0.0), axis=0, keepdims=True)
    return row, col, a_head, dt_c, cs_c, cs_r, cs_last


def _ssd_fwd(xbc, dtcol, dtrow, bias, a_log, dskip, *, name):
    s = xbc.shape[0]
    nc = s // SSD_CHUNK

    def body(x_ref, b_ref, c_ref, dtc_ref, dtr_ref, bias_ref, alog_ref, d_ref, y_ref, st_ref, state):
        ci = pl.program_id(1)

        @pl.when(ci == 0)
        def _():
            state[...] = jnp.zeros_like(state)

        bm, cm = b_ref[...], c_ref[...]
        cb = _dot(cm, bm, tb=True)
        bm_t = bm.T
        for r in range(GROUP_HEADS):
            lanes = slice(r * LANES, (r + 1) * LANES)
            xv = x_ref[:, lanes]
            row, col, _, dt_c, cs_c, cs_r, cs_last = _ssd_chunk_terms(dtc_ref[r], dtr_ref[r], bias_ref[r], alog_ref[r])
            g = cb * jnp.exp(jnp.where(col <= row, cs_c - cs_r, -jnp.inf))
            xdt = xv * dt_c
            st = state[r]
            st_ref[r, 0] = st
            y_ref[:, lanes] = _dot(g, xdt) + _dot(cm, st) * jnp.exp(cs_c) + xv * d_ref[r]
            state[r] = jnp.exp(cs_last) * st + _dot(bm_t, xdt * jnp.exp(cs_last - cs_c))

    sp = _ssd_specs(nc, lambda ci: ci)
    return pl.pallas_call(
        body, name=name, grid=(N_HEADS // GROUP_HEADS, nc),
        in_specs=[sp["x"], sp["b"], sp["c"], sp["dtcol"], sp["dtrow"], sp["scal"], sp["scal"], sp["scal"]],
        out_specs=[sp["x"], sp["state"]],
        out_shape=[jax.ShapeDtypeStruct((s, N_HEADS * LANES), F32),
                   jax.ShapeDtypeStruct((N_HEADS, nc, LANES, LANES), F32)],
        scratch_shapes=[pltpu.VMEM((GROUP_HEADS, LANES, LANES), F32)],
        compiler_params=_cparams(("parallel", "arbitrary")),
    )(xbc, xbc, xbc, dtcol, dtrow, bias, a_log, dskip)


def _ssd_bwd(xbc, dtcol, dtrow, bias, a_log, dskip, states, dy, *, name):
    s = xbc.shape[0]
    nc = s // SSD_CHUNK

    def body(x_ref, b_ref, c_ref, dtc_ref, dtr_ref, bias_ref, alog_ref, d_ref, st_ref, dy_ref,
             dx_ref, db_ref, dc_ref, ddt_ref, dbias_ref, dalog_ref, dd_ref, dstate):
        ci = pl.program_id(1)

        @pl.when(ci == 0)
        def _():
            dstate[...] = jnp.zeros_like(dstate)
            dbias_ref[...] = jnp.zeros_like(dbias_ref)
            dalog_ref[...] = jnp.zeros_like(dalog_ref)
            dd_ref[...] = jnp.zeros_like(dd_ref)

        bm, cm = b_ref[...], c_ref[...]
        cb = _dot(cm, bm, tb=True)
        cb_t = _dot(bm, cm, tb=True)
        cm_t = cm.T
        rowsum = lambda v: jnp.sum(v, axis=1, keepdims=True)
        tot = lambda v: jnp.broadcast_to(jnp.sum(v, axis=0, keepdims=True), (1, LANES))
        dbm_sum, dcm_sum = None, None
        for r in range(GROUP_HEADS):
            lanes = slice(r * LANES, (r + 1) * LANES)
            xv, dyv, st = x_ref[:, lanes], dy_ref[:, lanes], st_ref[r, 0]
            dtraw_c, bias = dtc_ref[r], bias_ref[r]
            row, col, a_head, dt_c, cs_c, cs_r, cs_last = _ssd_chunk_terms(dtraw_c, dtr_ref[r], bias, alog_ref[r])
            lmat = jnp.exp(jnp.where(col <= row, cs_c - cs_r, -jnp.inf))
            lmat_t = jnp.exp(jnp.where(row <= col, cs_r - cs_c, -jnp.inf))
            g, g_t = cb * lmat, cb_t * lmat_t
            xdt = xv * dt_c
            e_c = jnp.exp(cs_c)
            f_c = jnp.exp(cs_last - cs_c)
            e_last = jnp.exp(cs_last)
            w = xdt * f_c
            dst = dstate[r]

            dg = _dot(dyv, xdt, tb=True)
            dg_t = _dot(xdt, dyv, tb=True)
            dxdt = _dot(g_t, dyv)
            dcs = rowsum(dg * g) - rowsum(dg_t * g_t)
            dcm = _dot(dg * lmat, bm)
            dbm = _dot(dg_t * lmat_t, cm)
            z = _dot(cm, st)
            dz = dyv * e_c
            dcs = dcs + rowsum(dz * z)
            dcm = dcm + _dot(dz, st, tb=True)
            dstate[r] = _dot(cm_t, dz) + e_last * dst
            dcs_last = jnp.sum(rowsum(dst * st), axis=0, keepdims=True) * jnp.max(e_last, axis=1, keepdims=True)
            dbm = dbm + _dot(w, dst, tb=True)
            dw = _dot(bm, dst)
            dxdt = dxdt + dw * f_c
            q = rowsum(dw * w)
            dcs = dcs - q
            dcs_last = dcs_last + jnp.sum(q, axis=0, keepdims=True)
            dx_ref[:, lanes] = dxdt * dt_c + dyv * d_ref[r]
            ddt = rowsum(dxdt * xv)
            dcs_full = jnp.broadcast_to(dcs, (SSD_CHUNK, SSD_CHUNK)) + jnp.where(row == SSD_CHUNK - 1, dcs_last, 0.0)
            da = jnp.max(_rev_cumsum_rows(dcs_full, row), axis=1, keepdims=True)
            dt_col = jnp.max(dt_c, axis=1, keepdims=True)
            draw = (ddt + da * a_head) * _sigmoid(dtraw_c + bias)
            ddt_ref[r] = draw
            dbias_ref[r] += tot(draw)
            dalog_ref[r] += tot(da * dt_col) * a_head
            dd_ref[r] += tot(rowsum(dyv * xv))
            dbm_sum = dbm if dbm_sum is None else dbm_sum + dbm
            dcm_sum = dcm if dcm_sum is None else dcm_sum + dcm
        db_ref[...] = dbm_sum
        dc_ref[...] = dcm_sum

    sp = _ssd_specs(nc, lambda ci: nc - 1 - ci)
    return pl.pallas_call(
        body, name=name, grid=(N_HEADS // GROUP_HEADS, nc),
        in_specs=[sp["x"], sp["b"], sp["c"], sp["dtcol"], sp["dtrow"], sp["scal"], sp["scal"], sp["scal"],
                  sp["state"], sp["x"]],
        out_specs=[sp["x"], sp["group"], sp["group"], sp["dtcol"], sp["pacc"], sp["pacc"], sp["pacc"]],
        out_shape=[jax.ShapeDtypeStruct((s, N_HEADS * LANES), F32),
                   jax.ShapeDtypeStruct((s, 2 * LANES), F32),
                   jax.ShapeDtypeStruct((s, 2 * LANES), F32),
                   jax.ShapeDtypeStruct((N_HEADS, s, 1), F32),
                   jax.ShapeDtypeStruct((N_HEADS, 1, LANES), F32),
                   jax.ShapeDtypeStruct((N_HEADS, 1, LANES), F32),
                   jax.ShapeDtypeStruct((N_HEADS, 1, LANES), F32)],
        scratch_shapes=[pltpu.VMEM((GROUP_HEADS, LANES, LANES), F32)],
        compiler_params=_cparams(("parallel", "arbitrary")),
    )(xbc, xbc, xbc, dtcol, dtrow, bias, a_log, dskip, states, dy)


def _att_tile(s):
    return _pick(s, (512, 256, 128))


def _tri(t, transposed=False):
    r = lax.broadcasted_iota(jnp.int32, (t, t), 0)
    c = lax.broadcasted_iota(jnp.int32, (t, t), 1)
    return (r <= c) if transposed else (c <= r)


def _rows_at(ref, blk, t):
    return ref[pl.ds(pl.multiple_of(blk * t, t), t), :]


def _flash_fwd(q, k, v, *, name):
    s = q.shape[0]
    t = _att_tile(s)
    nq = s // t

    def body(q_ref, k_ref, v_ref, o_ref, lse_ref):
        i = pl.program_id(1)
        qv = q_ref[...]

        def step(j, carry, diagonal):
            m_old, l_old, acc = carry
            sc = _dot(qv, _rows_at(k_ref, j, t), tb=True)
            if diagonal:
                sc = jnp.where(_tri(t), sc, -jnp.inf)
            m_new = jnp.maximum(m_old, jnp.max(sc, axis=1, keepdims=True))
            alpha = jnp.exp(m_old - m_new)
            p = jnp.exp(sc - m_new)
            return (m_new, alpha * l_old + jnp.sum(p, axis=1, keepdims=True),
                    alpha * acc + _dot(p, _rows_at(v_ref, j, t)))

        init = (jnp.full((t, 1), -jnp.inf, F32), jnp.zeros((t, 1), F32), jnp.zeros((t, LANES), F32))
        carry = lax.fori_loop(0, i, lambda j, c: step(j, c, False), init)
        m_fin, l_fin, acc = step(i, carry, True)
        o_ref[...] = (acc / l_fin).astype(o_ref.dtype)
        lse_ref[0] = m_fin + jnp.log(l_fin)

    q_spec = pl.BlockSpec((t, LANES), lambda h, i: (i, h))
    kv_spec = pl.BlockSpec((s, LANES), lambda h, i: (0, h))
    return pl.pallas_call(
        body, name=name, grid=(N_HEADS, nq),
        in_specs=[q_spec, kv_spec, kv_spec],
        out_specs=[q_spec, pl.BlockSpec((1, t, 1), lambda h, i: (h, i, 0))],
        out_shape=[jax.ShapeDtypeStruct(q.shape, BF16), jax.ShapeDtypeStruct((N_HEADS, s, 1), F32)],
        compiler_params=_cparams(("parallel", "arbitrary")),
    )(q, k, v)


def _flash_bwd_dq(q, k, v, o, do, lse, *, name):
    s = q.shape[0]
    t = _att_tile(s)
    nq = s // t

    def body(q_ref, k_ref, v_ref, o_ref, do_ref, lse_ref, dq_ref, dl_ref):
        i = pl.program_id(1)
        qv, dov, lse = q_ref[...], do_ref[...], lse_ref[0]
        delta = jnp.sum(dov.astype(F32) * o_ref[...].astype(F32), axis=1, keepdims=True)
        dl_ref[0] = delta

        def step(j, acc, diagonal):
            kj = _rows_at(k_ref, j, t)
            p = jnp.exp(_dot(qv, kj, tb=True) - lse)
            if diagonal:
                p = jnp.where(_tri(t), p, 0.0)
            ds = p * (_dot(dov, _rows_at(v_ref, j, t), tb=True) - delta)
            return acc + _dot(ds, kj)

        acc = lax.fori_loop(0, i, lambda j, c: step(j, c, False), jnp.zeros((t, LANES), F32))
        dq_ref[...] = step(i, acc, True) * ATT_SCALE

    q_spec = pl.BlockSpec((t, LANES), lambda h, i: (i, h))
    kv_spec = pl.BlockSpec((s, LANES), lambda h, i: (0, h))
    col_spec = pl.BlockSpec((1, t, 1), lambda h, i: (h, i, 0))
    return pl.pallas_call(
        body, name=name, grid=(N_HEADS, nq),
        in_specs=[q_spec, kv_spec, kv_spec, q_spec, q_spec, col_spec],
        out_specs=[q_spec, col_spec],
        out_shape=[jax.ShapeDtypeStruct(q.shape, F32), jax.ShapeDtypeStruct((N_HEADS, s, 1), F32)],
        compiler_params=_cparams(("parallel", "arbitrary")),
    )(q, k, v, o, do, lse)


def _flash_bwd_dkv(q, k, v, do, lse_row, delta_row, *, name):
    s = q.shape[0]
    t = _att_tile(s)
    nq = s // t

    def body(q_ref, k_ref, v_ref, do_ref, lse_ref, dl_ref, dk_ref, dv_ref):
        j = pl.program_id(1)
        kv, vv = k_ref[...], v_ref[...]

        def step(i, carry, diagonal):
            dk, dv = carry
            qi, doi = _rows_at(q_ref, i, t), _rows_at(do_ref, i, t)
            cols = pl.ds(pl.multiple_of(i * t, t), t)
            p_t = jnp.exp(_dot(kv, qi, tb=True) - lse_ref[0, :, cols])
            if diagonal:
                p_t = jnp.where(_tri(t, transposed=True), p_t, 0.0)
            ds_t = p_t * (_dot(vv, doi, tb=True) - dl_ref[0, :, cols])
            return dk + _dot(ds_t, qi), dv + _dot(p_t, doi)

        zero = jnp.zeros((t, LANES), F32)
        carry = step(j, (zero, zero), True)
        dk, dv = lax.fori_loop(j + 1, nq, lambda i, c: step(i, c, False), carry)
        dk_ref[...] = dk
        dv_ref[...] = dv

    q_spec = pl.BlockSpec((s, LANES), lambda h, j: (0, h))
    kv_spec = pl.BlockSpec((t, LANES), lambda h, j: (j, h))
    row_spec = pl.BlockSpec((1, 1, s), lambda h, j: (h, 0, 0))
    return pl.pallas_call(
        body, name=name, grid=(N_HEADS, nq),
        in_specs=[q_spec, kv_spec, kv_spec, q_spec, row_spec, row_spec],
        out_specs=[kv_spec, kv_spec],
        out_shape=[jax.ShapeDtypeStruct(q.shape, F32)] * 2,
        compiler_params=_cparams(("parallel", "arbitrary")),
    )(q, k, v, do, lse_row, delta_row)


def _rope(v, cos_t, sin_p, sin_m):
    return v * cos_t + pltpu.roll(v, QK_ROPE // 2, 1) * sin_p + pltpu.roll(v, LANES - QK_ROPE // 2, 1) * sin_m


def _rope_t(d, cos_t, sin_p, sin_m):
    return d * cos_t + pltpu.roll(d * sin_p, LANES - QK_ROPE // 2, 1) + pltpu.roll(d * sin_m, QK_ROPE // 2, 1)


def _att_prep(q_pad, kv2, kr, cos_t, sin_p, sin_m, *, name):
    w = N_HEADS * LANES

    def fn(qv, kvv, krv, c, sp, sm):
        kr_rot = _rope(krv, c, sp, sm)
        qs, ks = [], []
        for h in range(N_HEADS):
            blk = slice(h * LANES, (h + 1) * LANES)
            qs.append(_rope(qv[:, blk], c, sp, sm) * ATT_SCALE)
            ks.append(kvv[:, blk] + kr_rot)
        return jnp.concatenate(qs, axis=1), jnp.concatenate(ks, axis=1), kvv[:, w:]

    return _rowwise(fn, [q_pad, kv2, kr, cos_t, sin_p, sin_m], [],
                    [(w, BF16, "row"), (w, BF16, "row"), (w, BF16, "row")], name=name)


def _att_prep_bwd(dq, dk, cos_t, sin_p, sin_m, *, name):
    w = N_HEADS * LANES

    def fn(dqv, dkv, c, sp, sm):
        outs, dkr = [], None
        for h in range(N_HEADS):
            blk = slice(h * LANES, (h + 1) * LANES)
            outs.append(_rope_t(dqv[:, blk], c, sp, sm))
            dkr = dkv[:, blk] if dkr is None else dkr + dkv[:, blk]
        return jnp.concatenate(outs, axis=1), _rope_t(dkr, c, sp, sm)

    return _rowwise(fn, [dq, dk, cos_t, sin_p, sin_m], [], [(w, BF16, "row"), (LANES, F32, "row")], name=name)


_ANY = pl.BlockSpec(memory_space=pl.ANY)
_MESH = pl.DeviceIdType.MESH


def _mesh_pos():
    return lax.axis_index("x"), lax.axis_index("y"), lax.axis_index("c")


def _remote(src, dst, send_sem, recv_sem, dev):
    return pltpu.make_async_remote_copy(src_ref=src, dst_ref=dst, send_sem=send_sem, recv_sem=recv_sem,
                                        device_id=dev, device_id_type=_MESH)


def _other_chips(x, y):
    chips = [(1 - x, y), (x, 1 - y), (1 - x, 1 - y)]
    return chips, [2 * cx + cy for cx, cy in chips]


def _comm_call(body, ins, out_shapes, n_sems, *, name):
    return pl.pallas_call(
        body, name=name, in_specs=[_ANY] * len(ins), out_specs=[_ANY] * len(out_shapes), out_shape=out_shapes,
        scratch_shapes=[pltpu.SemaphoreType.DMA((k,)) for k in n_sems],
    )(*ins)


def _gather_layers(shards):
    n = len(shards)

    def body(*refs):
        xs, outs = refs[:n], refs[n:2 * n]
        send_sems, recv_sems, local_sems = refs[2 * n:]
        x, y, c = _mesh_pos()
        k = 2 * x + y
        sibling = (x, y, 1 - c)
        chips, ks = _other_chips(x, y)
        local = [pltpu.make_async_copy(xs[w], outs[w].at[k], local_sems.at[w]) for w in range(n)]
        for cp in local:
            cp.start()
        first = [_remote(xs[w].at[c], outs[w].at[k, c], send_sems.at[6 * w + j], recv_sems.at[6 * w + j], (*chips[j], c))
                 for w in range(n) for j in range(3)]
        for cp in first:
            cp.start()
        passed = []
        for j in range(3):
            for w in range(n):
                land = outs[w].at[ks[j], c]
                _remote(land, land, send_sems.at[6 * w + j], recv_sems.at[6 * w + j], sibling).wait_recv()
                passed.append(_remote(land, land, send_sems.at[6 * w + 3 + j], recv_sems.at[6 * w + 3 + j], sibling))
                passed[-1].start()
        for j in range(3):
            for w in range(n):
                land = outs[w].at[ks[j], 1 - c]
                _remote(land, land, send_sems.at[6 * w + 3 + j], recv_sems.at[6 * w + 3 + j], sibling).wait_recv()
        for cp in first + passed:
            cp.wait_send()
        for cp in local:
            cp.wait()

    shapes = [jax.ShapeDtypeStruct((4,) + t.shape, t.dtype) for t in shards]
    return _comm_call(body, shards, shapes, (6 * n, 6 * n, n), name="gather_layers")


def _send_other_layer(g0s, g1s):
    n = len(g0s)

    def body(*refs):
        g0, g1, outs = refs[:n], refs[n:2 * n], refs[2 * n:3 * n]
        send_sems, recv_sems = refs[3 * n:]
        x, y, c = _mesh_pos()
        sibling = (x, y, 1 - c)

        @pl.when(c == 0)
        def _():
            for w in range(n):
                _remote(g1[w], outs[w], send_sems.at[w], recv_sems.at[w], sibling).start()

        @pl.when(c == 1)
        def _():
            for w in range(n):
                _remote(g0[w], outs[w], send_sems.at[w], recv_sems.at[w], sibling).start()

        for w in range(n):
            done = _remote(outs[w], outs[w], send_sems.at[w], recv_sems.at[w], sibling)
            done.wait_send()
            done.wait_recv()

    shapes = [jax.ShapeDtypeStruct(t.shape, t.dtype) for t in g0s]
    return _comm_call(body, list(g0s) + list(g1s), shapes, (n, n), name="send_other_layer")


def _scatter_join(parts):
    n = len(parts)

    def body(*refs):
        ps, outs = refs[:n], refs[n:2 * n]
        send_sems, recv_sems, local_sems = refs[2 * n:]
        x, y, c = _mesh_pos()
        k = 2 * x + y
        sibling = (x, y, 1 - c)
        chips, ks = _other_chips(x, y)
        sem = lambda w, j: (send_sems.at[7 * w + j], recv_sems.at[7 * w + j])
        local = [pltpu.make_async_copy(ps[w].at[k], outs[w].at[c, k], local_sems.at[w]) for w in range(n)]
        sends = [_remote(ps[w].at[ks[j]], outs[w].at[c, k], *sem(w, j), (*chips[j], c)) for w in range(n) for j in range(3)]
        sends += [_remote(ps[w].at[k], outs[w].at[c, k], *sem(w, 6), sibling) for w in range(n)]
        for cp in local + sends:
            cp.start()
        for j in range(3):
            for w in range(n):
                land = outs[w].at[c, ks[j]]
                _remote(land, land, *sem(w, j), sibling).wait_recv()
                sends.append(_remote(land, land, *sem(w, 3 + j), sibling))
                sends[-1].start()
        for w in range(n):
            land = outs[w].at[1 - c, k]
            _remote(land, land, *sem(w, 6), sibling).wait_recv()
        for j in range(3):
            for w in range(n):
                land = outs[w].at[1 - c, ks[j]]
                _remote(land, land, *sem(w, 3 + j), sibling).wait_recv()
        for cp in sends:
            cp.wait_send()
        for cp in local:
            cp.wait()

    shapes = [jax.ShapeDtypeStruct((2,) + t.shape, t.dtype) for t in parts]
    return _comm_call(body, parts, shapes, (7 * n, 7 * n, n), name="scatter_join")


def _gather_all(vec, *, name):
    r, w = vec.shape

    def body(v_ref, out_ref, send_sems, recv_sems, local_sem):
        x, y, c = _mesh_pos()

        def slot(px, py, pc):
            return out_ref.at[4 * px + 2 * py + pc]

        mine = pltpu.make_async_copy(v_ref, slot(x, y, c), local_sem)
        mine.start()
        peers = []
        for rel in range(1, 8):
            fx, fy, fc = (rel >> 2) & 1, (rel >> 1) & 1, rel & 1
            peers.append((x ^ fx, y ^ fy, c ^ fc))
        cps = [_remote(v_ref, slot(x, y, c), send_sems.at[j], recv_sems.at[j], peer) for j, peer in enumerate(peers)]
        for cp in cps:
            cp.start()
        for j, peer in enumerate(peers):
            _remote(slot(*peer), slot(*peer), send_sems.at[j], recv_sems.at[j], peer).wait_recv()
        for cp in cps:
            cp.wait_send()
        mine.wait()

    return pl.pallas_call(
        body, name=name, in_specs=[_ANY], out_specs=_ANY,
        out_shape=jax.ShapeDtypeStruct((8, r, w), vec.dtype),
        scratch_shapes=[pltpu.SemaphoreType.DMA((7,)), pltpu.SemaphoreType.DMA((7,)), pltpu.SemaphoreType.DMA],
    )(vec)


def _row_tile(rows, row_bytes):
    for tm in (1024, 512, 256, 128, 64, 32, 16):
        if rows % tm == 0 and tm * row_bytes <= ELEMENTWISE_BLOCK_BYTES:
            return tm
    return 16 if rows % 16 == 0 else rows


def _chip_sum(g0, g1, got, c, *, name):
    r, w = g0.shape
    tm = _row_tile(r, w * 4)

    def body(c_ref, g0_ref, g1_ref, o_ref, out_ref):
        mine = jnp.where(c_ref[0] == 0, g0_ref[...], g1_ref[...])
        out_ref[...] = (mine + o_ref[...]).astype(out_ref.dtype)

    return pl.pallas_call(
        body, name=name,
        grid_spec=pltpu.PrefetchScalarGridSpec(
            num_scalar_prefetch=1, grid=(r // tm,),
            in_specs=[pl.BlockSpec((tm, w), lambda i, c_ref: (i * (1 - c_ref[0]), 0)),
                      pl.BlockSpec((tm, w), lambda i, c_ref: (i * c_ref[0], 0)),
                      pl.BlockSpec((tm, w), lambda i, c_ref: (i, 0))],
            out_specs=pl.BlockSpec((tm, w), lambda i, c_ref: (i, 0))),
        out_shape=jax.ShapeDtypeStruct((r, w), BF16),
        compiler_params=_cparams(("arbitrary",)),
    )(jnp.reshape(c, (1,)).astype(jnp.int32), g0, g1, got)


def _sum_slots(stack, *, name):
    n, r, w = stack.shape
    tm = _row_tile(r, n * w * stack.dtype.itemsize)

    def body(s_ref, out_ref):
        acc = s_ref[0].astype(F32)
        for i in range(1, n):
            acc = acc + s_ref[i].astype(F32)
        out_ref[...] = acc

    return pl.pallas_call(
        body, name=name, grid=(r // tm,),
        in_specs=[pl.BlockSpec((n, tm, w), lambda i: (0, i, 0))],
        out_specs=pl.BlockSpec((tm, w), lambda i: (i, 0)),
        out_shape=jax.ShapeDtypeStruct((r, w), F32),
        compiler_params=_cparams(("parallel",)),
    )(stack)


def _adam_math(wv, gv, mv, vv):
    m_new = ADAM_B1 * mv + (1.0 - ADAM_B1) * gv
    v_new = ADAM_B2 * vv + (1.0 - ADAM_B2) * (gv * gv)
    m_hat = m_new / (1.0 - ADAM_B1 ** ADAM_STEP)
    v_hat = v_new / (1.0 - ADAM_B2 ** ADAM_STEP)
    delta = -ADAM_LR * (m_hat / (jnp.sqrt(v_hat) + ADAM_EPS) + ADAM_WD * wv)
    return delta, m_new, v_new


def _adamw(w, g, m, v, *, name):
    shape = w.shape
    cols = shape[-1]
    flat = lambda t: t.reshape(-1, cols)
    rows = flat(w).shape[0]
    tm = _pick(rows, (256, 128, 64, 32, 16, 8))
    outs = _rowwise(_adam_math, [flat(w), flat(g), flat(m), flat(v)], [], [(cols, F32, "row")] * 3, name=name, tm=tm)
    return tuple(o.reshape(shape) for o in outs)


def _adamw_slots(w, slots, m, v, *, name):
    shape = w.shape
    cols = shape[-1]
    v3 = lambda t: t.reshape(2, -1, cols)
    rows = v3(w).shape[1]
    assert slots.shape == (2, 4, rows, cols), (slots.shape, shape)
    tm = _row_tile(rows, cols * 4)

    def body(w_ref, s_ref, m_ref, v_ref, g_ref, d_ref, mo_ref, vo_ref):
        g = s_ref[0, 0].astype(F32)
        for i in range(1, 4):
            g = g + s_ref[0, i].astype(F32)
        delta, m_new, v_new = _adam_math(w_ref[0], g, m_ref[0], v_ref[0])
        g_ref[0], d_ref[0], mo_ref[0], vo_ref[0] = g, delta, m_new, v_new

    blk = pl.BlockSpec((1, tm, cols), lambda l, i: (l, i, 0))
    outs = pl.pallas_call(
        body, name=name, grid=(2, rows // tm),
        in_specs=[blk, pl.BlockSpec((1, 4, tm, cols), lambda l, i: (l, 0, i, 0)), blk, blk],
        out_specs=[blk] * 4, out_shape=[jax.ShapeDtypeStruct((2, rows, cols), F32)] * 4,
        compiler_params=_cparams(("parallel", "parallel")),
    )(v3(w), slots, v3(m), v3(v))
    return tuple(o.reshape(shape) for o in outs)


def _pad_blocks(w, axis, n_blocks, real, to=LANES, offset=0):
    axis = axis % w.ndim
    shp = w.shape
    w = w.reshape(shp[:axis] + (n_blocks, real) + shp[axis + 1:])
    pads = [(0, 0)] * w.ndim
    pads[axis + 1] = (offset, to - real - offset)
    w = jnp.pad(w, pads)
    return w.reshape(shp[:axis] + (n_blocks * to,) + shp[axis + 1:])


def _unpad_blocks(w, axis, n_blocks, real, to=LANES, offset=0):
    axis = axis % w.ndim
    shp = w.shape
    w = w.reshape(shp[:axis] + (n_blocks, to) + shp[axis + 1:])
    w = lax.slice_in_dim(w, offset, offset + real, axis=axis + 1)
    return w.reshape(shp[:axis] + (n_blocks * real,) + shp[axis + 1:])


def _block_diag(w):
    n, a, b = w.shape
    eye = jnp.eye(n, dtype=w.dtype)
    return (eye[:, None, :, None] * w[:, :, None, :]).reshape(n * a, n * b)


def _block_diag_t(d, n):
    a, b = d.shape[0] // n, d.shape[1] // n
    d = d.reshape(n, a, n, b)
    return jnp.stack([d[i, :, i, :] for i in range(n)])


_SPLITS = np.cumsum((0,) + SPLIT_SIZES)


def _w_in_groups(w_in):
    sl = lambda i: w_in[:, _SPLITS[i]:_SPLITS[i + 1]]
    xbc = sl(5)
    xbc_pad = jnp.concatenate([_pad_blocks(xbc[:, :MIX], 1, N_HEADS, HEAD),
                               _pad_blocks(xbc[:, MIX:MIX + 2 * HEAD], 1, 2, HEAD),
                               _pad_blocks(xbc[:, MIX + 2 * HEAD:], 1, 2, HEAD)], axis=1)
    return dict(
        cq=sl(0), ckv=sl(1), kr=_pad_blocks(sl(2), 1, 1, QK_ROPE, offset=HEAD), pool=sl(3),
        z=_pad_blocks(sl(4), 1, N_HEADS, HEAD), xbc=xbc_pad, dt=_pad_blocks(sl(6), 1, 1, N_HEADS),
        lru_g=sl(7), lru_x=sl(8), gates=sl(9))


def _w_in_fused(groups):
    parts, at = [], 0
    for name, off, width in IN_LAYOUT:
        assert groups[name].shape[1] == width and off >= at
        if off > at:
            parts.append(jnp.zeros((groups[name].shape[0], off - at), groups[name].dtype))
        parts.append(groups[name])
        at = off + width
    parts.append(jnp.zeros((parts[0].shape[0], IN_ALL_COLS - at), parts[0].dtype))
    return jnp.concatenate(parts, axis=1)


def _in_cols(arr, name):
    off, width = IN_OFFSETS[name]
    return _Cols(arr, off, width)


def _w_in_ungroup(d):
    xbc = d["xbc"]
    w = N_HEADS * LANES
    xbc_real = jnp.concatenate([_unpad_blocks(xbc[:, :w], 1, N_HEADS, HEAD),
                                _unpad_blocks(xbc[:, w:w + 2 * LANES], 1, 2, HEAD),
                                _unpad_blocks(xbc[:, w + 2 * LANES:], 1, 2, HEAD)], axis=1)
    return jnp.concatenate([d["cq"], d["ckv"], _unpad_blocks(d["kr"], 1, 1, QK_ROPE, offset=HEAD), d["pool"],
                            _unpad_blocks(d["z"], 1, N_HEADS, HEAD), xbc_real, _unpad_blocks(d["dt"], 1, 1, N_HEADS),
                            d["lru_g"], d["lru_x"], d["gates"]], axis=1)


def _pad_xbc_vec(v):
    return jnp.concatenate([_pad_blocks(v[..., :MIX], -1, N_HEADS, HEAD),
                            _pad_blocks(v[..., MIX:MIX + 2 * HEAD], -1, 2, HEAD),
                            _pad_blocks(v[..., MIX + 2 * HEAD:], -1, 2, HEAD)], axis=-1)


def _unpad_xbc_vec(v):
    w = N_HEADS * LANES
    return jnp.concatenate([_unpad_blocks(v[..., :w], -1, N_HEADS, HEAD),
                            _unpad_blocks(v[..., w:w + 2 * LANES], -1, 2, HEAD),
                            _unpad_blocks(v[..., w + 2 * LANES:], -1, 2, HEAD)], axis=-1)


def _layer_weights(p):
    q = dict(p)
    q["in_all"] = _w_in_fused(_w_in_groups(p["w_in"]))
    q["uq"] = _pad_blocks(p["w_uq"], 1, N_HEADS, HEAD + QK_ROPE)
    ukv = p["w_ukv"].reshape(KV_LORA, N_HEADS, 2 * HEAD)
    q["ukv"] = jnp.concatenate([_pad_blocks(ukv[:, :, :HEAD].reshape(KV_LORA, -1), 1, N_HEADS, HEAD),
                                _pad_blocks(ukv[:, :, HEAD:].reshape(KV_LORA, -1), 1, N_HEADS, HEAD)], axis=1)
    q["pool_bd"] = _block_diag(p["w_pool"])
    q["lru_bd"] = jnp.concatenate([_block_diag(p["lru_w_a"]), _block_diag(p["lru_w_i"])], axis=1)
    q["br"] = [_pad_blocks(p["w_branch"][0], 0, N_HEADS, HEAD), p["w_branch"][1],
               _pad_blocks(p["w_branch"][2], 0, N_HEADS, HEAD), p["w_branch"][3]]
    q["ssd_conv_w_pad"] = _pad_xbc_vec(p["ssd_conv_w"])
    q["ssd_conv_b_pad"] = _pad_xbc_vec(p["ssd_conv_b"])[None, :]
    q["ssd_norm_pad"] = _pad_blocks(p["ssd_norm"], 0, N_HEADS, HEAD)[None, :]
    return q


def _row(v):
    return v.reshape(1, -1)


def _scal3(v):
    return v.reshape(N_HEADS, 1, 1)


def _layer_fwd(x, p_emb, w, rope, tag):
    n = lambda s: f"{s}_{tag}"
    sv = {"x": x}
    h = _rms_fwd(x, _row(w["g_mix"]), name=n("rms_mix"))
    sv["h"] = h
    u_all = _mm(h, w["in_all"], name=n("in_proj"))
    u = {k: _in_cols(u_all, k) for k in IN_OFFSETS}
    sv["u"] = u

    cqn = _rms_fwd(u["cq"], _row(w["q_norm"]), name=n("rms_q"))
    ckvn = _rms_fwd(u["ckv"], _row(w["kv_norm"]), name=n("rms_kv"))
    q_pad = _mm(cqn, w["uq"], name=n("uq"))
    kv2 = _mm(ckvn, w["ukv"], name=n("ukv"))
    qc, kc, vc = _att_prep(q_pad, kv2, u["kr"], *rope, name=n("att_prep"))
    y_a, lse = _flash_fwd(qc, kc, vc, name=n("flash_fwd"))
    sv.update(cqn=cqn, ckvn=ckvn, qc=qc, kc=kc, vc=vc, y_a=y_a, lse=lse)

    pool_d = _pool_fwd(u["pool"], name=n("pool_fwd"))
    yb_pre, y_b = _mm(pool_d, w["pool_bd"], epilogue=lambda acc, sc: (acc, acc * sc),
                      rowvecs=[_row(w["pool_scale"])], out_dtypes=(F32, BF16), name=n("pool_mm"))
    sv.update(pool_d=pool_d, yb_pre=yb_pre, y_b=y_b)

    xbc_c = _conv_fwd(u["xbc"], w["ssd_conv_w_pad"], w["ssd_conv_b_pad"], silu=True, name=n("ssd_conv"))
    dt8 = lax.slice_in_dim(u_all, IN_OFFSETS["dt"][0], IN_OFFSETS["dt"][0] + N_HEADS, axis=1)
    dtcol = dt8.T[:, :, None]
    dtrow = dt8.T[:, None, :]
    ssd_par = (_scal3(w["ssd_dt_bias"]), _scal3(w["ssd_a_log"]), _scal3(w["ssd_d"]))
    y_ssd, states = _ssd_fwd(xbc_c, dtcol, dtrow, *ssd_par, name=n("ssd_fwd"))

    def ssd_post(yv, zv, gv):
        xh, _ = _rms_parts(yv * _silu(zv), MIX)
        return xh * gv

    y_c = _rowwise(ssd_post, [y_ssd, u["z"]], [w["ssd_norm_pad"]], [(N_HEADS * LANES, BF16, "row")], name=n("ssd_post"))
    sv.update(xbc_c=xbc_c, dtcol=dtcol, dtrow=dtrow, y_ssd=y_ssd, states=states, y_c=y_c)

    xc = _conv_fwd(u["lru_x"], w["lru_conv_w"], _row(w["lru_conv_b"]), silu=False, name=n("lru_conv"))
    pre = _mm(xc, w["lru_bd"], name=n("lru_mm"))
    lru_par = (_row(w["lru_lambda"]), _row(w["lru_b_a"]), _row(w["lru_b_i"]))
    y_d, h_lru = _lru_fwd(pre, xc, u["lru_g"], *lru_par, name=n("lru_fwd"))
    sv.update(xc=xc, pre=pre, h_lru=h_lru, y_d=y_d)

    ys = [y_a, y_b, y_c, y_d]
    merged, ybs = None, []
    for b in range(4):
        if merged is None:
            merged, yb = _mm(ys[b], w["br"][b], epilogue=lambda acc, gt: (_sigmoid(gt) * acc, acc),
                             tiles=[_Cols(u_all, b * D_MODEL, D_MODEL)], out_dtypes=(F32, F32), name=n(f"branch{b}"))
        else:
            merged, yb = _mm(ys[b], w["br"][b], epilogue=lambda acc, gt, mg: (mg + _sigmoid(gt) * acc, acc),
                             tiles=[_Cols(u_all, b * D_MODEL, D_MODEL), merged], out_dtypes=(F32, F32),
                             name=n(f"branch{b}"))
        ybs.append(yb)
    x1 = _mm(merged, w["w_out"], epilogue=lambda acc, xr: (acc + xr,), tiles=[x], name=n("out_proj"))
    sv.update(ybs=ybs, merged=merged, x1=x1)

    h2 = _rms_fwd(x1, _row(w["g_mlp"]), name=n("rms_mlp"))
    a_ff, f_ff = _mm(h2, w["w_ff1"], epilogue=lambda acc: (acc, jnp.square(jnp.maximum(acc, 0.0))),
                     out_dtypes=(F32, BF16), name=n("ff1"))
    x2 = _mm(f_ff, w["w_ff2"], epilogue=lambda acc, xr: (acc + xr,), tiles=[x1], name=n("ff2"))
    sv.update(h2=h2, a_ff=a_ff, f_ff=f_ff, x2=x2)

    h3 = _rms_fwd(x2, _row(w["g_ple"]), name=n("rms_ple"))
    e_ple = _mm(p_emb, w["w_ple"], name=n("ple_emb"))
    x3, gt_ple = _mm(h3, w["w_ple_gate"], epilogue=lambda acc, ev, xr: (xr + ev * _sigmoid(acc), _sigmoid(acc)),
                     tiles=[e_ple, x2], out_dtypes=(F32, F32), name=n("ple_gate"))
    sv.update(h3=h3, e_ple=e_ple, gt_ple=gt_ple, p_emb=p_emb)
    return x3, sv


def _layer_bwd(dx3, sv, w, rope, tag):
    n = lambda s: f"{s}_{tag}"
    gr = {}
    u = sv["u"]

    de, dpre = _rowwise(lambda d, gt, ev: (d * gt, d * ev * gt * (1.0 - gt)), [dx3, sv["gt_ple"], sv["e_ple"]], [],
                        [(D_MODEL, BF16, "row"), (D_MODEL, BF16, "row")], name=n("ple_bwd"))
    gr["w_ple"] = _mm(sv["p_emb"], de, ta=True, name=n("d_w_ple"))
    gr["w_ple_gate"] = _mm(sv["h3"], dpre, ta=True, name=n("d_w_ple_gate"))
    dh3 = _mm(dpre, w["w_ple_gate"], tb=True, out_dtypes=(BF16,), name=n("d_h3"))
    dx2, dg = _rms_bwd(sv["x2"], _row(w["g_ple"]), dh3, dx3, name=n("rms_ple_bwd"))
    gr["g_ple"] = dg[0]

    gr["w_ff2"] = _mm(sv["f_ff"], dx2, ta=True, name=n("d_w_ff2"))
    da = _mm(dx2, w["w_ff2"], tb=True, epilogue=lambda acc, av: (acc * 2.0 * jnp.maximum(av, 0.0),),
             tiles=[sv["a_ff"]], out_dtypes=(BF16,), name=n("d_a_ff"))
    gr["w_ff1"] = _mm(sv["h2"], da, ta=True, name=n("d_w_ff1"))
    dh2 = _mm(da, w["w_ff1"], tb=True, out_dtypes=(BF16,), name=n("d_h2"))
    dx1, dg = _rms_bwd(sv["x1"], _row(w["g_mlp"]), dh2, dx2, name=n("rms_mlp_bwd"))
    gr["g_mlp"] = dg[0]

    gr["w_out"] = _mm(sv["merged"], dx1, ta=True, name=n("d_w_out"))
    dmerged = _mm(dx1, w["w_out"], tb=True, name=n("d_merged"))

    def merge_bwd(dm, gts, y0, y1, y2, y3):
        dys, dgs = [], []
        for b, yb in enumerate((y0, y1, y2, y3)):
            sg = _sigmoid(gts[:, b * D_MODEL:(b + 1) * D_MODEL])
            dys.append(dm * sg)
            dgs.append(dm * yb * sg * (1.0 - sg))
        return (*dys, jnp.concatenate(dgs, axis=1))

    *dybs, dgates = _rowwise(merge_bwd, [dmerged, u["gates"]] + sv["ybs"], [],
                             [(D_MODEL, BF16, "row")] * 4 + [(4 * D_MODEL, BF16, "row")], name=n("merge_bwd"))
    ys = [sv["y_a"], sv["y_b"], sv["y_c"], sv["y_d"]]
    dwb = [_mm(ys[b], dybs[b], ta=True, name=n(f"d_w_branch{b}")) for b in range(4)]
    gr["w_branch"] = jnp.stack([_unpad_blocks(dwb[0], 0, N_HEADS, HEAD), dwb[1],
                                _unpad_blocks(dwb[2], 0, N_HEADS, HEAD), dwb[3]])
    dy_a = _mm(dybs[0], w["br"][0], tb=True, out_dtypes=(BF16,), name=n("d_y_a"))
    dy_b = _mm(dybs[1], w["br"][1], tb=True, name=n("d_y_b"))
    dy_c = _mm(dybs[2], w["br"][2], tb=True, name=n("d_y_c"))
    dy_d = _mm(dybs[3], w["br"][3], tb=True, name=n("d_y_d"))
    du = {"gates": dgates}

    lru_par = (_row(w["lru_lambda"]), _row(w["lru_b_a"]), _row(w["lru_b_i"]))
    dpa, dpi, dxc_direct, du["lru_g"], dlam, dba, dbi = _lru_bwd(
        sv["pre"], sv["xc"], u["lru_g"], *lru_par, sv["h_lru"], dy_d, name=n("lru_bwd"))
    dpre_lru = jnp.concatenate([dpa, dpi], axis=1)
    d_bd = _mm(sv["xc"], dpre_lru, ta=True, name=n("d_lru_w"))
    gr["lru_w_a"] = _block_diag_t(d_bd[:, :MIX], N_HEADS)
    gr["lru_w_i"] = _block_diag_t(d_bd[:, MIX:], N_HEADS)
    gr["lru_lambda"], gr["lru_b_a"], gr["lru_b_i"] = dlam[0], dba[0], dbi[0]
    dxc = _mm(dpre_lru, w["lru_bd"], tb=True, epilogue=lambda acc, t: (acc + t,), tiles=[dxc_direct], name=n("d_xc"))
    du["lru_x"], gr["lru_conv_w"], dcb = _conv_bwd(u["lru_x"], w["lru_conv_w"], _row(w["lru_conv_b"]), dxc,
                                                  silu=False, name=n("lru_conv_bwd"))
    gr["lru_conv_b"] = dcb[0]

    def ssd_post_bwd(dyc, yv, zv, gv):
        sz = _silu(zv)
        dyz, dgain = _rms_bwd_math(yv * sz, gv, dyc, MIX)
        return dyz * sz, dyz * yv * _silu_grad(zv), dgain

    dy_ssd, du["z"], dgain = _rowwise(ssd_post_bwd, [dy_c, sv["y_ssd"], u["z"]], [w["ssd_norm_pad"]],
                                      [(N_HEADS * LANES, F32, "row"), (N_HEADS * LANES, BF16, "row"),
                                       (N_HEADS * LANES, F32, "acc")], name=n("ssd_post_bwd"))
    gr["ssd_norm"] = _unpad_blocks(dgain[0], 0, N_HEADS, HEAD)
    ssd_par = (_scal3(w["ssd_dt_bias"]), _scal3(w["ssd_a_log"]), _scal3(w["ssd_d"]))
    dxs, dbg, dcg, ddt, dbias, dalog, dd = _ssd_bwd(sv["xbc_c"], sv["dtcol"], sv["dtrow"], *ssd_par, sv["states"],
                                                    dy_ssd, name=n("ssd_bwd"))
    s = dxs.shape[0]
    dxbc_c = jnp.concatenate([dxs, dbg, dcg], axis=1)
    gr["ssd_dt_bias"], gr["ssd_a_log"], gr["ssd_d"] = dbias[:, 0, 0], dalog[:, 0, 0], dd[:, 0, 0]
    du["xbc"], dcw, dcb = _conv_bwd(u["xbc"], w["ssd_conv_w_pad"], w["ssd_conv_b_pad"], dxbc_c, silu=True,
                                    name=n("ssd_conv_bwd"))
    gr["ssd_conv_w"], gr["ssd_conv_b"] = _unpad_xbc_vec(dcw), _unpad_xbc_vec(dcb[0])
    du["dt"] = jnp.pad(ddt[:, :, 0].T, ((0, 0), (0, LANES - N_HEADS)))

    dyb_pre, dscale = _rowwise(lambda d, yp, sc: (d * sc, _colsum(d * yp)), [dy_b, sv["yb_pre"]],
                               [_row(w["pool_scale"])], [(MIX, BF16, "row"), (MIX, F32, "acc")], name=n("pool_scale_bwd"))
    gr["pool_scale"] = dscale[0]
    gr["w_pool"] = _block_diag_t(_mm(sv["pool_d"], dyb_pre, ta=True, name=n("d_w_pool")), 4)
    dd_pool = _mm(dyb_pre, w["pool_bd"], tb=True, name=n("d_pool_d"))
    du["pool"] = _pool_bwd(dd_pool, name=n("pool_bwd"))

    dqc, delta = _flash_bwd_dq(sv["qc"], sv["kc"], sv["vc"], sv["y_a"], dy_a, sv["lse"], name=n("flash_dq"))
    to_row = lambda t: t.reshape(N_HEADS, 1, s)
    dkc, dvc = _flash_bwd_dkv(sv["qc"], sv["kc"], sv["vc"], dy_a, to_row(sv["lse"]), to_row(delta), name=n("flash_dkv"))
    dq_pad, du["kr"] = _att_prep_bwd(dqc, dkc, *rope, name=n("att_prep_bwd"))
    d_uq = _mm(sv["cqn"], dq_pad, ta=True, name=n("d_w_uq"))
    gr["w_uq"] = _unpad_blocks(d_uq, 1, N_HEADS, HEAD + QK_ROPE)
    dcqn = _mm(dq_pad, w["uq"], tb=True, out_dtypes=(BF16,), name=n("d_cqn"))
    du["cq"], dg = _rms_bwd(u["cq"], _row(w["q_norm"]), dcqn, name=n("rms_q_bwd"))
    gr["q_norm"] = dg[0]
    dkv2 = jnp.concatenate([dkc, dvc], axis=1).astype(BF16)
    d_ukv = _mm(sv["ckvn"], dkv2, ta=True, name=n("d_w_ukv"))
    wk = N_HEADS * LANES
    dk_real = _unpad_blocks(d_ukv[:, :wk], 1, N_HEADS, HEAD).reshape(KV_LORA, N_HEADS, HEAD)
    dv_real = _unpad_blocks(d_ukv[:, wk:], 1, N_HEADS, HEAD).reshape(KV_LORA, N_HEADS, HEAD)
    gr["w_ukv"] = jnp.concatenate([dk_real, dv_real], axis=2).reshape(KV_LORA, N_HEADS * 2 * HEAD)
    dckvn = _mm(dkv2, w["ukv"], tb=True, out_dtypes=(BF16,), name=n("d_ckvn"))
    du["ckv"], dg = _rms_bwd(u["ckv"], _row(w["kv_norm"]), dckvn, name=n("rms_kv_bwd"))
    gr["kv_norm"] = dg[0]

    du_all = _w_in_fused({k: v.astype(BF16) for k, v in du.items()})
    dw_all = _mm(sv["h"], du_all, ta=True, name=n("d_w_in"))
    gr["w_in"] = _w_in_ungroup({k: dw_all[:, off:off + width] for k, off, width in IN_LAYOUT})
    dh = _mm(du_all, w["in_all"], tb=True, name=n("d_h"))
    dx, dg = _rms_bwd(sv["x"], _row(w["g_mix"]), dh, dx1, name=n("rms_mix_bwd"))
    gr["g_mix"] = dg[0]
    return dx, gr


def _pack_rows(n_elems):
    per = PACK_W * PACK_ROWS
    return -(-n_elems // per) * PACK_ROWS


def _pack_flat(parts, dtype):
    flat = jnp.concatenate([p.reshape(-1).astype(dtype) for p in parts])
    rows = _pack_rows(flat.shape[0])
    return jnp.pad(flat, (0, rows * PACK_W - flat.shape[0])).reshape(rows, PACK_W)


def _unpack_flat(buf, shapes):
    lead = buf.shape[:-2]
    flat = buf.reshape(lead + (-1,))
    out, off = [], 0
    for shp in shapes:
        size = int(np.prod(shp))
        out.append(flat[..., off:off + size].reshape(lead + tuple(shp)))
        off += size
    return out


def _merge_shards(t, axis):
    return jnp.concatenate([t[i] for i in range(4)], axis=axis)


def _split_shards(t, axis):
    return jnp.stack(jnp.split(t, 4, axis=axis))


def _rope_tables(positions):
    inv = 1.0 / (ROPE_THETA ** (jnp.arange(0, QK_ROPE, 2, dtype=F32) / QK_ROPE))
    ang = positions.astype(F32)[:, None] * inv
    cos, sin = jnp.cos(ang), jnp.sin(ang)
    s = ang.shape[0]
    half = QK_ROPE // 2
    z = lambda n_: jnp.zeros((s, n_), F32)
    cos_t = jnp.concatenate([jnp.ones((s, HEAD), F32), cos, cos, jnp.ones((s, LANES - HEAD - QK_ROPE), F32)], axis=1)
    sin_p = jnp.concatenate([z(HEAD + half), sin, z(LANES - HEAD - QK_ROPE)], axis=1)
    sin_m = jnp.concatenate([z(HEAD), -sin, z(half + LANES - HEAD - QK_ROPE)], axis=1)
    return cos_t, sin_p, sin_m


def _loss_head(x, g, target, *, name):
    d = x.shape[1]

    def fn(xv, tv, gv):
        xh, r = _rms_parts(xv, d)
        y = xh * gv
        err = y - tv
        dy = err * (1.0 / d)
        dxh = dy * gv
        dx = r * (dxh - xh * (jnp.sum(dxh * xh, axis=-1, keepdims=True) * (1.0 / d)))
        return dx, _colsum(dy * xh), _colsum(err * err) * (0.5 / d)

    return _rowwise(fn, [x, target], [g], [(d, F32, "row"), (d, F32, "acc"), (d, F32, "acc")], name=name)


def _step(args):
    x = args["x"][0]
    c_idx = lax.axis_index("c")

    mats = [(nm, ax) for nm, ax in BIG if nm not in CONV_SHARDED]
    gathered = _gather_layers([args[nm].astype(BF16) for nm, _ in mats])
    full = {nm: [_merge_shards(t[:, l], ax - 1) for l in range(2)] for (nm, ax), t in zip(mats, gathered)}
    convs = [(nm, ax) for nm, ax in BIG if nm in CONV_SHARDED]
    conv_all = _gather_all(_pack_flat([args[nm] for nm, _ in convs], F32), name="gather_conv_taps")[0::2]
    for (nm, ax), t in zip(convs, _unpack_flat(conv_all, [args[nm].shape for nm, _ in convs])):
        full[nm] = _merge_shards(t, ax)
    rope = _rope_tables(args["positions"][0])

    layers = []
    for l in range(2):
        p = {nm: full[nm][l] for nm, _ in BIG}
        p.update({nm: args[nm][l] for nm in SMALL if nm != "g_final"})
        layers.append(_layer_weights(p))

    saved = []
    for l in range(2):
        x, sv = _layer_fwd(x, args["p"][l, 0], layers[l], rope, f"l{l}")
        saved.append(sv)

    dx, dg_final, loss_part = _loss_head(x, _row(args["g_final"]), args["loss_target"][0], name="loss_head")
    loss = lax.psum(jnp.sum(loss_part), ("x", "y", "c"))

    grads = [None, None]
    for l in (1, 0):
        dx, grads[l] = _layer_bwd(dx, saved[l], layers[l], rope, f"l{l}")

    g_local = {nm: jnp.stack([grads[0][nm], grads[1][nm]]) for nm in SMALL if nm != "g_final"}
    g_local["g_final"] = dg_final[0]

    view2d = lambda t: t.reshape(-1, t.shape[-1])
    names = [nm for nm, _ in BIG]
    got = _send_other_layer([view2d(grads[0][nm]) for nm in names], [view2d(grads[1][nm]) for nm in names])
    parts = []
    for (nm, ax), gt in zip(BIG, got):
        both = _chip_sum(view2d(grads[0][nm]), view2d(grads[1][nm]), gt, c_idx, name="chip_sum_" + nm)
        shards = _split_shards(both.reshape(grads[0][nm].shape), ax - 1)
        parts.append(shards.reshape(4, -1, shards.shape[-1]))
    upd, g_red = {}, {}
    for nm, slots in zip(names, _scatter_join(parts)):
        g_red[nm], *upd[nm] = _adamw_slots(args[nm], slots, args["m_" + nm], args["v_" + nm], name="adamw_" + nm)

    small_shapes = [args[nm].shape for nm in SMALL]
    small_sum = _sum_slots(_gather_all(_pack_flat([g_local[nm] for nm in SMALL], F32), name="gather_small_grads"),
                           name="sum_devices")
    g_red.update(zip(SMALL, _unpack_flat(small_sum, small_shapes)))

    pack_small = lambda pre: _pack_flat([args[pre + nm] for nm in SMALL], F32)
    upd_small = _adamw(pack_small(""), small_sum, pack_small("m_"), pack_small("v_"), name="adamw_small")
    upd.update({nm: trip for nm, trip in zip(SMALL, zip(*[_unpack_flat(t, small_shapes) for t in upd_small]))})

    outs = [loss, dx[None]]
    outs += [g_red[nm] for nm in WEIGHTS]
    for i in range(3):
        outs += [upd[nm][i] for nm in WEIGHTS]
    return tuple(outs)


_ARG_NAMES = ("x", "p", "positions") + WEIGHTS + ("loss_target",) + tuple("m_" + nm for nm in WEIGHTS) \
    + tuple("v_" + nm for nm in WEIGHTS)


def kernel(*arrays):
    assert len(arrays) == len(_ARG_NAMES), len(arrays)
    return _step(dict(zip(_ARG_NAMES, arrays)))
```

```python
import functools
import math

import jax
import jax.numpy as jnp
import numpy as np
from jax import lax
from jax.experimental import pallas as pl
from jax.experimental.pallas import tpu as pltpu

F32 = jnp.float32
BF16 = jnp.bfloat16
MXU_DTYPE = BF16
LANES = 128
VMEM_LIMIT = 56 * 1024 * 1024
MM_VMEM_BUDGET = 36 * 1024 * 1024
ELEMENTWISE_BLOCK_BYTES = 2 * 1024 * 1024

D_MODEL = 1024
N_HEADS = 8
HEAD = 64
QK_ROPE = 32
Q_LORA = 384
KV_LORA = 256
MIX = 512
SSD_CHUNK = 128
CONV_W = 4
POOL_WINDOWS = (2, 4, 8, 16)
LRU_C = 8.0
EPS = 1e-6
ROPE_THETA = 10000.0
ATT_SCALE = (HEAD + QK_ROPE) ** -0.5
SPLIT_SIZES = (Q_LORA, KV_LORA, QK_ROPE, MIX, MIX, 768, N_HEADS, MIX, MIX, 4 * D_MODEL)
IN_LAYOUT = (("gates", 0, 4096), ("z", 4096, 1024), ("pool", 5120, 512), ("lru_g", 5632, 512), ("lru_x", 6144, 512),
             ("cq", 6912, 384), ("ckv", 7424, 256), ("xbc", 7680, 1536), ("kr", 9216, 128), ("dt", 9344, 128))
IN_OFFSETS = {name: (off, width) for name, off, width in IN_LAYOUT}
IN_ALL_COLS = 9728

ADAM_LR, ADAM_B1, ADAM_B2, ADAM_EPS, ADAM_WD, ADAM_STEP = 0.001, 0.9, 0.999, 1e-08, 0.01, 10

BIG = (("w_in", 2), ("w_uq", 2), ("w_ukv", 2), ("ssd_conv_w", 2), ("lru_conv_w", 2), ("w_branch", 3),
       ("w_out", 1), ("w_ff1", 2), ("w_ff2", 1), ("w_ple_gate", 1), ("w_ple", 2))
SMALL = ("g_mix", "q_norm", "kv_norm", "w_pool", "pool_scale", "ssd_conv_b", "ssd_dt_bias", "ssd_a_log",
         "ssd_d", "ssd_norm", "lru_conv_b", "lru_w_a", "lru_b_a", "lru_w_i", "lru_b_i", "lru_lambda",
         "g_mlp", "g_ple", "g_final")
WEIGHTS = ("g_mix", "w_in", "q_norm", "w_uq", "kv_norm", "w_ukv", "w_pool", "pool_scale", "ssd_conv_w",
           "ssd_conv_b", "ssd_dt_bias", "ssd_a_log", "ssd_d", "ssd_norm", "lru_conv_w", "lru_conv_b", "lru_w_a",
           "lru_b_a", "lru_w_i", "lru_b_i", "lru_lambda", "w_branch", "w_out", "g_mlp", "w_ff1", "w_ff2", "g_ple",
           "w_ple_gate", "w_ple", "g_final")
CONV_SHARDED = ("ssd_conv_w", "lru_conv_w")
PACK_W = 1024
PACK_ROWS = 64


def _cparams(sem, vmem=VMEM_LIMIT):
    return pltpu.CompilerParams(dimension_semantics=sem, vmem_limit_bytes=vmem)


def _pick(n, cands):
    for c in cands:
        if n % c == 0:
            return c
    return n


class _Cols:
    def __init__(self, arr, off, width):
        self.arr, self.off, self.width = arr, off, width

    shape = property(lambda self: (self.arr.shape[0], self.width))
    dtype = property(lambda self: self.arr.dtype)


def _arr(x):
    return x.arr if isinstance(x, _Cols) else x


def _off(x, unit):
    off = x.off if isinstance(x, _Cols) else 0
    assert off % unit == 0, (off, unit)
    return off // unit


def _sigmoid(x):
    return 1.0 / (1.0 + jnp.exp(-x))


def _silu(x):
    return x * _sigmoid(x)


def _silu_grad(x):
    s = _sigmoid(x)
    return s * (1.0 + x * (1.0 - s))


def _softplus(x):
    e = jnp.exp(-jnp.abs(x))
    log1p_e = jnp.where(e < 1e-3, e * (1.0 - e * (0.5 - e * (1.0 / 3.0))), jnp.log(1.0 + e))
    return jnp.maximum(x, 0.0) + log1p_e


_GELU_C = math.sqrt(2.0 / math.pi)


def _gelu(x):
    t = jnp.tanh(_GELU_C * (x + 0.044715 * x * x * x))
    return 0.5 * x * (1.0 + t)


def _gelu_grad(x):
    t = jnp.tanh(_GELU_C * (x + 0.044715 * x * x * x))
    return 0.5 * (1.0 + t) + 0.5 * x * (1.0 - t * t) * _GELU_C * (1.0 + 3.0 * 0.044715 * x * x)


def _neg_expm1(x):
    series = -x * (1.0 + 0.5 * x * (1.0 + (1.0 / 3.0) * x * (1.0 + 0.25 * x)))
    return jnp.where(x > -0.05, series, 1.0 - jnp.exp(x))


def _shift_down(x, k, row):
    return jnp.where(row >= k, pltpu.roll(x, k, 0), 0.0)


def _shift_up(x, k, row):
    n = x.shape[0]
    return jnp.where(row < n - k, pltpu.roll(x, n - k, 0), 0.0)


def _cumsum_rows(x, row):
    d = 1
    while d < x.shape[0]:
        x = x + _shift_down(x, d, row)
        d *= 2
    return x


def _rev_cumsum_rows(x, row):
    d = 1
    while d < x.shape[0]:
        x = x + _shift_up(x, d, row)
        d *= 2
    return x


def _cumsum_lanes(x, col):
    d = 1
    while d < x.shape[1]:
        x = x + jnp.where(col >= d, pltpu.roll(x, d, 1), 0.0)
        d *= 2
    return x


def _dot(a, b, ta=False, tb=False):
    dn = (((0 if ta else 1,), (1 if tb else 0,)), ((), ()))
    return lax.dot_general(a.astype(MXU_DTYPE), b.astype(MXU_DTYPE), dn, preferred_element_type=F32)


def _mm_tiles(m, n, k, a_bytes, b_bytes, mn_bytes):
    best = None
    for tm in (1024, 512, 384, 256, 128):
        for tn in (1024, 512, 384, 256, 128):
            for tk in (2048, 1024, 512, 384, 256, 128):
                if m % tm or n % tn or k % tk:
                    continue
                vmem = 2 * (tm * tk * a_bytes + tk * tn * b_bytes) + 2 * tm * tn * mn_bytes + 4 * tm * tn
                vmem += 2 * (tm * tk + tk * tn)
                if vmem > MM_VMEM_BUDGET:
                    continue
                steps = (m // tm) * (n // tn) * (k // tk)
                key = (steps, vmem)
                if best is None or key < best[0]:
                    best = (key, (tm, tn, tk))
    assert best is not None, (m, n, k)
    return best[1]


def _mm(a, b, *, ta=False, tb=False, epilogue=None, tiles=(), rowvecs=(), out_dtypes=(F32,), name):
    m, k = (a.shape[1], a.shape[0]) if ta else a.shape
    n = b.shape[0] if tb else b.shape[1]
    assert (b.shape[1] if tb else b.shape[0]) == k, (a.shape, b.shape, ta, tb)
    mn_bytes = sum(t.dtype.itemsize for t in tiles) + sum(jnp.dtype(dt).itemsize for dt in out_dtypes)
    tm, tn, tk = _mm_tiles(m, n, k, a.dtype.itemsize, b.dtype.itemsize, mn_bytes)
    nk = k // tk
    nt, nr, no = len(tiles), len(rowvecs), len(out_dtypes)

    def body(*refs):
        a_ref, b_ref = refs[0], refs[1]
        tile_refs = refs[2:2 + nt]
        row_refs = refs[2 + nt:2 + nt + nr]
        out_refs = refs[2 + nt + nr:2 + nt + nr + no]
        acc_ref = refs[-1]
        kk = pl.program_id(2)

        @pl.when(kk == 0)
        def _():
            acc_ref[...] = jnp.zeros_like(acc_ref)

        acc_ref[...] += _dot(a_ref[...], b_ref[...], ta, tb)

        @pl.when(kk == nk - 1)
        def _():
            acc = acc_ref[...]
            if epilogue is None:
                outs = (acc,)
            else:
                outs = epilogue(acc, *[t[...] for t in tile_refs], *[r[...] for r in row_refs])
            for o_ref, o in zip(out_refs, outs):
                o_ref[...] = o.astype(o_ref.dtype)

    a_spec = pl.BlockSpec((tk, tm), lambda i, j, kk: (kk, i)) if ta else pl.BlockSpec((tm, tk), lambda i, j, kk: (i, kk))
    b_spec = pl.BlockSpec((tn, tk), lambda i, j, kk: (j, kk)) if tb else pl.BlockSpec((tk, tn), lambda i, j, kk: (kk, j))
    mn_spec = pl.BlockSpec((tm, tn), lambda i, j, kk: (i, j))
    row_spec = pl.BlockSpec((1, tn), lambda i, j, kk: (0, j))
    tile_specs = [pl.BlockSpec((tm, tn), lambda i, j, kk, ob=_off(t, tn): (i, j + ob)) for t in tiles]
    outs = pl.pallas_call(
        body, name=name,
        grid=(m // tm, n // tn, nk),
        in_specs=[a_spec, b_spec] + tile_specs + [row_spec] * nr,
        out_specs=[mn_spec] * no,
        out_shape=[jax.ShapeDtypeStruct((m, n), dt) for dt in out_dtypes],
        scratch_shapes=[pltpu.VMEM((tm, tn), F32)],
        compiler_params=_cparams(("parallel", "parallel", "arbitrary")),
    )(a, b, *[_arr(t) for t in tiles], *rowvecs)
    return outs[0] if no == 1 else tuple(outs)


def _rowwise(fn, rows, fulls, outs, *, name, tm=None):
    r = rows[0].shape[0]
    if tm is None:
        widest = max([x.shape[1] for x in rows] + [o[0] for o in outs])
        tm = _pick(r, (max(8, min(512, (512 * 1024) // widest)), 256, 128, 64, 32, 16, 8))
    nrow, nfull, nout = len(rows), len(fulls), len(outs)

    def body(*refs):
        row_refs = refs[:nrow]
        full_refs = refs[nrow:nrow + nfull]
        out_refs = refs[nrow + nfull:]
        res = fn(*[x[...] for x in row_refs], *[x[...] for x in full_refs])
        if not isinstance(res, (tuple, list)):
            res = (res,)
        step = pl.program_id(0)
        for o_ref, o, spec in zip(out_refs, res, outs):
            if spec[2] == "row":
                o_ref[...] = o.astype(o_ref.dtype)
            else:
                @pl.when(step == 0)
                def _(o_ref=o_ref):
                    o_ref[...] = jnp.zeros_like(o_ref)
                o_ref[...] += o

    in_specs = [pl.BlockSpec((tm, x.shape[1]), lambda i, ob=_off(x, x.shape[1]): (i, ob)) for x in rows]
    in_specs += [pl.BlockSpec(x.shape, lambda i, nd=x.ndim: (0,) * nd) for x in fulls]
    out_specs, out_shape = [], []
    for c, dt, kind in outs:
        if kind == "row":
            out_specs.append(pl.BlockSpec((tm, c), lambda i: (i, 0)))
            out_shape.append(jax.ShapeDtypeStruct((r, c), dt))
        else:
            out_specs.append(pl.BlockSpec((1, c), lambda i: (0, 0)))
            out_shape.append(jax.ShapeDtypeStruct((1, c), F32))
    res = pl.pallas_call(
        body, name=name, grid=(r // tm,), in_specs=in_specs, out_specs=out_specs, out_shape=out_shape,
        compiler_params=_cparams(("arbitrary",)),
    )(*[_arr(x) for x in rows], *fulls)
    return res[0] if nout == 1 else tuple(res)


def _colsum(x):
    return jnp.sum(x, axis=0, keepdims=True)


def _rms_parts(x, n_real):
    r = lax.rsqrt(jnp.sum(x * x, axis=-1, keepdims=True) * (1.0 / n_real) + EPS)
    return x * r, r


def _rms_fwd(x, g, *, n_real=None, out_dtype=BF16, name):
    n_real = n_real or x.shape[1]

    def fn(xv, gv):
        xh, _ = _rms_parts(xv, n_real)
        return xh * gv

    return _rowwise(fn, [x], [g], [(x.shape[1], out_dtype, "row")], name=name)


def _rms_bwd_math(xv, gv, dh, n_real):
    xh, r = _rms_parts(xv, n_real)
    dxh = dh * gv
    dx = r * (dxh - xh * (jnp.sum(dxh * xh, axis=-1, keepdims=True) * (1.0 / n_real)))
    return dx, _colsum(dh * xh)


def _rms_bwd(x, g, dh, res=None, *, name):
    n = x.shape[1]
    if res is None:
        def fn(xv, dhv, gv):
            return _rms_bwd_math(xv, gv, dhv.astype(F32), n)
        rows = [x, dh]
    else:
        def fn(xv, dhv, rv, gv):
            dx, dg = _rms_bwd_math(xv, gv, dhv.astype(F32), n)
            return dx + rv, dg
        rows = [x, dh, res]
    return _rowwise(fn, rows, [g], [(n, F32, "row"), (n, F32, "acc")], name=name)


def _seq_call(body, ins, outs, n_blocks, *, name):
    in_specs, args = [], []
    for x, kind in ins:
        in_specs.append(pl.BlockSpec((x.shape[0], LANES), lambda j, ob=_off(x, LANES): (0, j + ob)))
        args.append(_arr(x))
    out_specs, out_shape = [], []
    for shape, dt in outs:
        out_specs.append(pl.BlockSpec((shape[0], LANES), lambda j: (0, j)))
        out_shape.append(jax.ShapeDtypeStruct(shape, dt))
    res = pl.pallas_call(body, name=name, grid=(n_blocks,), in_specs=in_specs, out_specs=out_specs,
                         out_shape=out_shape, compiler_params=_cparams(("parallel",)))(*args)
    return res[0] if len(outs) == 1 else tuple(res)


def _conv_pre(x, w, b, row):
    acc = x * w[CONV_W - 1:CONV_W, :] + b
    for k in range(CONV_W - 1):
        acc = acc + _shift_down(x, CONV_W - 1 - k, row) * w[k:k + 1, :]
    return acc


def _conv_fwd(x, w, b, *, silu, name):
    s, c = x.shape

    def body(x_ref, w_ref, b_ref, y_ref):
        xv = x_ref[...]
        row = lax.broadcasted_iota(jnp.int32, xv.shape, 0)
        pre = _conv_pre(xv, w_ref[...], b_ref[...], row)
        y_ref[...] = _silu(pre) if silu else pre

    return _seq_call(body, [(x, "seq"), (w, "par"), (b, "par")], [((s, c), F32)], c // LANES, name=name)


def _conv_bwd(x, w, b, dy, *, silu, name):
    s, c = x.shape

    def body(x_ref, w_ref, b_ref, dy_ref, dx_ref, dw_ref, db_ref):
        xv, wv, dv = x_ref[...], w_ref[...], dy_ref[...]
        row = lax.broadcasted_iota(jnp.int32, xv.shape, 0)
        if silu:
            dv = dv * _silu_grad(_conv_pre(xv, wv, b_ref[...], row))
        dx = dv * wv[CONV_W - 1:CONV_W, :]
        dws = [None] * CONV_W
        dws[CONV_W - 1] = _colsum(dv * xv)
        for k in range(CONV_W - 1):
            sh = CONV_W - 1 - k
            dx = dx + _shift_up(dv, sh, row) * wv[k:k + 1, :]
            dws[k] = _colsum(dv * _shift_down(xv, sh, row))
        dx_ref[...] = dx
        for k in range(CONV_W):
            dw_ref[k:k + 1, :] = dws[k]
        db_ref[...] = _colsum(dv)

    return _seq_call(body, [(x, "seq"), (w, "par"), (b, "par"), (dy, "seq")],
                     [((s, c), F32), ((CONV_W, c), F32), ((1, c), F32)], c // LANES, name=name)


def _pool_select(levels):
    g = pl.program_id(0)
    return jnp.where(g == 0, levels[0], jnp.where(g == 1, levels[1], jnp.where(g == 2, levels[2], levels[3])))


def _pool_count(row):
    g = pl.program_id(0)
    w = jnp.where(g == 0, POOL_WINDOWS[0], jnp.where(g == 1, POOL_WINDOWS[1],
                                                     jnp.where(g == 2, POOL_WINDOWS[2], POOL_WINDOWS[3])))
    return jnp.minimum(row + 1, w).astype(F32)


def _pool_fwd(u, *, name):
    def body(u_ref, d_ref):
        uv = u_ref[...]
        row = lax.broadcasted_iota(jnp.int32, uv.shape, 0)
        levels, cur, sh = [], uv, 1
        for _ in POOL_WINDOWS:
            cur = cur + _shift_down(cur, sh, row)
            levels.append(cur)
            sh *= 2
        d_ref[...] = _pool_select(levels) / _pool_count(row) - uv

    return _seq_call(body, [(u, "seq")], [(u.shape, F32)], u.shape[1] // LANES, name=name)


def _pool_bwd(dd, *, name):
    def body(dd_ref, du_ref):
        dv = dd_ref[...]
        row = lax.broadcasted_iota(jnp.int32, dv.shape, 0)
        levels, cur, sh = [], dv / _pool_count(row), 1
        for _ in POOL_WINDOWS:
            cur = cur + _shift_up(cur, sh, row)
            levels.append(cur)
            sh *= 2
        du_ref[...] = _pool_select(levels) - dv

    return _seq_call(body, [(dd, "seq")], [(dd.shape, F32)], dd.shape[1] // LANES, name=name)


def _lru_gates(pre_a, pre_i, xc, lam, b_a, b_i):
    r = _sigmoid(pre_a + b_a)
    i = _sigmoid(pre_i + b_i)
    sp = _softplus(-lam)
    log_a = -LRU_C * r * sp
    a = jnp.exp(log_a)
    mult = jnp.sqrt(_neg_expm1(2.0 * log_a))
    return r, i, sp, a, mult


def _lru_fwd(pre, xc, gate_in, lam, b_a, b_i, *, name):
    s, c = xc.shape
    nb = c // LANES

    def body(pa_ref, pi_ref, xc_ref, g_ref, lam_ref, ba_ref, bi_ref, y_ref, h_ref):
        xv = xc_ref[...]
        row = lax.broadcasted_iota(jnp.int32, xv.shape, 0)
        _, i, _, a, mult = _lru_gates(pa_ref[...], pi_ref[...], xv, lam_ref[...], ba_ref[...], bi_ref[...])
        h = xv * i * mult
        d = 1
        while d < s:
            h = h + a * _shift_down(h, d, row)
            a = a * jnp.where(row >= d, pltpu.roll(a, d, 0), 1.0)
            d *= 2
        h_ref[...] = h
        y_ref[...] = h * _gelu(g_ref[...])

    blk = lambda off: pl.BlockSpec((s, LANES), lambda j: (0, j + off))
    par = pl.BlockSpec((1, LANES), lambda j: (0, j))
    return pl.pallas_call(
        body, name=name, grid=(nb,),
        in_specs=[blk(0), blk(nb), blk(0), blk(_off(gate_in, LANES)), par, par, par],
        out_specs=[blk(0), blk(0)],
        out_shape=[jax.ShapeDtypeStruct((s, c), F32)] * 2,
        compiler_params=_cparams(("parallel",)),
    )(pre, pre, xc, _arr(gate_in), lam, b_a, b_i)


def _lru_bwd(pre, xc, gate_in, lam, b_a, b_i, h, dy, *, name):
    s, c = xc.shape
    nb = c // LANES

    def body(pa_ref, pi_ref, xc_ref, g_ref, lam_ref, ba_ref, bi_ref, h_ref, dy_ref,
             dpa_ref, dpi_ref, dxc_ref, dg_ref, dlam_ref, dba_ref, dbi_ref):
        xv, gv, hv, dv = xc_ref[...], g_ref[...], h_ref[...], dy_ref[...]
        row = lax.broadcasted_iota(jnp.int32, xv.shape, 0)
        r, i, sp, a, mult = _lru_gates(pa_ref[...], pi_ref[...], xv, lam_ref[...], ba_ref[...], bi_ref[...])
        dg_ref[...] = dv * hv * _gelu_grad(gv)
        dh = dv * _gelu(gv)
        an = jnp.where(row < s - 1, pltpu.roll(a, s - 1, 0), 0.0)
        d = 1
        while d < s:
            dh = dh + an * _shift_up(dh, d, row)
            an = an * jnp.where(row < s - d, pltpu.roll(an, s - d, 0), 1.0)
            d *= 2
        da = dh * _shift_down(hv, 1, row)
        dxc_ref[...] = dh * i * mult
        di = dh * xv * mult
        dmult = dh * xv * i
        dlog_a = (da - dmult * a / mult) * a
        dr = dlog_a * (-LRU_C) * sp
        dlam_ref[...] = _colsum(dlog_a * LRU_C * r * _sigmoid(-lam_ref[...]))
        dpa = dr * r * (1.0 - r)
        dpi = di * i * (1.0 - i)
        dpa_ref[...] = dpa
        dpi_ref[...] = dpi
        dba_ref[...] = _colsum(dpa)
        dbi_ref[...] = _colsum(dpi)

    blk = lambda off: pl.BlockSpec((s, LANES), lambda j: (0, j + off))
    par = pl.BlockSpec((1, LANES), lambda j: (0, j))
    sc = jax.ShapeDtypeStruct((s, c), F32)
    pc = jax.ShapeDtypeStruct((1, c), F32)
    dpa, dpi, dxc, dg, dlam, dba, dbi = pl.pallas_call(
        body, name=name, grid=(nb,),
        in_specs=[blk(0), blk(nb), blk(0), blk(_off(gate_in, LANES)), par, par, par, blk(0), blk(0)],
        out_specs=[blk(0), blk(0), blk(0), blk(0), par, par, par],
        out_shape=[sc, sc, sc, sc, pc, pc, pc],
        compiler_params=_cparams(("parallel",)),
    )(pre, pre, xc, _arr(gate_in), lam, b_a, b_i, h, dy)
    return dpa, dpi, dxc, dg, dlam, dba, dbi


GROUP_HEADS = 4


def _ssd_specs(nc, order):
    gw = GROUP_HEADS * LANES
    return dict(
        x=pl.BlockSpec((SSD_CHUNK, gw), lambda g, ci: (order(ci), g)),
        b=pl.BlockSpec((SSD_CHUNK, LANES), lambda g, ci: (order(ci), N_HEADS + g)),
        c=pl.BlockSpec((SSD_CHUNK, LANES), lambda g, ci: (order(ci), N_HEADS + 2 + g)),
        dtcol=pl.BlockSpec((GROUP_HEADS, SSD_CHUNK, 1), lambda g, ci: (g, order(ci), 0)),
        dtrow=pl.BlockSpec((GROUP_HEADS, 1, SSD_CHUNK), lambda g, ci: (g, 0, order(ci))),
        scal=pl.BlockSpec((GROUP_HEADS, 1, 1), lambda g, ci: (g, 0, 0)),
        state=pl.BlockSpec((GROUP_HEADS, 1, LANES, LANES), lambda g, ci: (g, order(ci), 0, 0)),
        group=pl.BlockSpec((SSD_CHUNK, LANES), lambda g, ci: (order(ci), g)),
        pacc=pl.BlockSpec((GROUP_HEADS, 1, LANES), lambda g, ci: (g, 0, 0)),
    )


def _ssd_chunk_terms(dtcol, dtrow, bias, a_log):
    shp = (SSD_CHUNK, SSD_CHUNK)
    row = lax.broadcasted_iota(jnp.int32, shp, 0)
    col = lax.broadcasted_iota(jnp.int32, shp, 1)
    a_head = -jnp.exp(a_log)
    dt_c = jnp.broadcast_to(_softplus(dtcol + bias), shp)
    dt_r = jnp.broadcast_to(_softplus(dtrow + bias), shp)
    cs_c = _cumsum_rows(dt_c * a_head, row)
    cs_r = _cumsum_lanes(dt_r * a_head, col)
    cs_last = jnp.sum(jnp.where(row == SSD_CHUNK - 1, cs_c, 0.0), axis=0, keepdims=True)
    return row, col, a_head, dt_c, cs_c, cs_r, cs_last


def _ssd_fwd(xbc, dtcol, dtrow, bias, a_log, dskip, *, name):
    s = xbc.shape[0]
    nc = s // SSD_CHUNK

    def body(x_ref, b_ref, c_ref, dtc_ref, dtr_ref, bias_ref, alog_ref, d_ref, y_ref, st_ref, state):
        ci = pl.program_id(1)

        @pl.when(ci == 0)
        def _():
            state[...] = jnp.zeros_like(state)

        bm, cm = b_ref[...], c_ref[...]
        cb = _dot(cm, bm, tb=True)
        bm_t = bm.T
        for r in range(GROUP_HEADS):
            lanes = slice(r * LANES, (r + 1) * LANES)
            xv = x_ref[:, lanes]
            row, col, _, dt_c, cs_c, cs_r, cs_last = _ssd_chunk_terms(dtc_ref[r], dtr_ref[r], bias_ref[r], alog_ref[r])
            g = cb * jnp.exp(jnp.where(col <= row, cs_c - cs_r, -jnp.inf))
            xdt = xv * dt_c
            st = state[r]
            st_ref[r, 0] = st
            y_ref[:, lanes] = _dot(g, xdt) + _dot(cm, st) * jnp.exp(cs_c) + xv * d_ref[r]
            state[r] = jnp.exp(cs_last) * st + _dot(bm_t, xdt * jnp.exp(cs_last - cs_c))

    sp = _ssd_specs(nc, lambda ci: ci)
    return pl.pallas_call(
        body, name=name, grid=(N_HEADS // GROUP_HEADS, nc),
        in_specs=[sp["x"], sp["b"], sp["c"], sp["dtcol"], sp["dtrow"], sp["scal"], sp["scal"], sp["scal"]],
        out_specs=[sp["x"], sp["state"]],
        out_shape=[jax.ShapeDtypeStruct((s, N_HEADS * LANES), F32),
                   jax.ShapeDtypeStruct((N_HEADS, nc, LANES, LANES), F32)],
        scratch_shapes=[pltpu.VMEM((GROUP_HEADS, LANES, LANES), F32)],
        compiler_params=_cparams(("parallel", "arbitrary")),
    )(xbc, xbc, xbc, dtcol, dtrow, bias, a_log, dskip)


def _ssd_bwd(xbc, dtcol, dtrow, bias, a_log, dskip, states, dy, *, name):
    s = xbc.shape[0]
    nc = s // SSD_CHUNK

    def body(x_ref, b_ref, c_ref, dtc_ref, dtr_ref, bias_ref, alog_ref, d_ref, st_ref, dy_ref,
             dx_ref, db_ref, dc_ref, ddt_ref, dbias_ref, dalog_ref, dd_ref, dstate):
        ci = pl.program_id(1)

        @pl.when(ci == 0)
        def _():
            dstate[...] = jnp.zeros_like(dstate)
            dbias_ref[...] = jnp.zeros_like(dbias_ref)
            dalog_ref[...] = jnp.zeros_like(dalog_ref)
            dd_ref[...] = jnp.zeros_like(dd_ref)

        bm, cm = b_ref[...], c_ref[...]
        cb = _dot(cm, bm, tb=True)
        cb_t = _dot(bm, cm, tb=True)
        cm_t = cm.T
        rowsum = lambda v: jnp.sum(v, axis=1, keepdims=True)
        tot = lambda v: jnp.broadcast_to(jnp.sum(v, axis=0, keepdims=True), (1, LANES))
        dbm_sum, dcm_sum = None, None
        for r in range(GROUP_HEADS):
            lanes = slice(r * LANES, (r + 1) * LANES)
            xv, dyv, st = x_ref[:, lanes], dy_ref[:, lanes], st_ref[r, 0]
            dtraw_c, bias = dtc_ref[r], bias_ref[r]
            row, col, a_head, dt_c, cs_c, cs_r, cs_last = _ssd_chunk_terms(dtraw_c, dtr_ref[r], bias, alog_ref[r])
            lmat = jnp.exp(jnp.where(col <= row, cs_c - cs_r, -jnp.inf))
            lmat_t = jnp.exp(jnp.where(row <= col, cs_r - cs_c, -jnp.inf))
            g, g_t = cb * lmat, cb_t * lmat_t
            xdt = xv * dt_c
            e_c = jnp.exp(cs_c)
            f_c = jnp.exp(cs_last - cs_c)
            e_last = jnp.exp(cs_last)
            w = xdt * f_c
            dst = dstate[r]

            dg = _dot(dyv, xdt, tb=True)
            dg_t = _dot(xdt, dyv, tb=True)
            dxdt = _dot(g_t, dyv)
            dcs = rowsum(dg * g) - rowsum(dg_t * g_t)
            dcm = _dot(dg * lmat, bm)
            dbm = _dot(dg_t * lmat_t, cm)
            z = _dot(cm, st)
            dz = dyv * e_c
            dcs = dcs + rowsum(dz * z)
            dcm = dcm + _dot(dz, st, tb=True)
            dstate[r] = _dot(cm_t, dz) + e_last * dst
            dcs_last = jnp.sum(rowsum(dst * st), axis=0, keepdims=True) * jnp.max(e_last, axis=1, keepdims=True)
            dbm = dbm + _dot(w, dst, tb=True)
            dw = _dot(bm, dst)
            dxdt = dxdt + dw * f_c
            q = rowsum(dw * w)
            dcs = dcs - q
            dcs_last = dcs_last + jnp.sum(q, axis=0, keepdims=True)
            dx_ref[:, lanes] = dxdt * dt_c + dyv * d_ref[r]
            ddt = rowsum(dxdt * xv)
            dcs_full = jnp.broadcast_to(dcs, (SSD_CHUNK, SSD_CHUNK)) + jnp.where(row == SSD_CHUNK - 1, dcs_last, 0.0)
            da = jnp.max(_rev_cumsum_rows(dcs_full, row), axis=1, keepdims=True)
            dt_col = jnp.max(dt_c, axis=1, keepdims=True)
            draw = (ddt + da * a_head) * _sigmoid(dtraw_c + bias)
            ddt_ref[r] = draw
            dbias_ref[r] += tot(draw)
            dalog_ref[r] += tot(da * dt_col) * a_head
            dd_ref[r] += tot(rowsum(dyv * xv))
            dbm_sum = dbm if dbm_sum is None else dbm_sum + dbm
            dcm_sum = dcm if dcm_sum is None else dcm_sum + dcm
        db_ref[...] = dbm_sum
        dc_ref[...] = dcm_sum

    sp = _ssd_specs(nc, lambda ci: nc - 1 - ci)
    return pl.pallas_call(
        body, name=name, grid=(N_HEADS // GROUP_HEADS, nc),
        in_specs=[sp["x"], sp["b"], sp["c"], sp["dtcol"], sp["dtrow"], sp["scal"], sp["scal"], sp["scal"],
                  sp["state"], sp["x"]],
        out_specs=[sp["x"], sp["group"], sp["group"], sp["dtcol"], sp["pacc"], sp["pacc"], sp["pacc"]],
        out_shape=[jax.ShapeDtypeStruct((s, N_HEADS * LANES), F32),
                   jax.ShapeDtypeStruct((s, 2 * LANES), F32),
                   jax.ShapeDtypeStruct((s, 2 * LANES), F32),
                   jax.ShapeDtypeStruct((N_HEADS, s, 1), F32),
                   jax.ShapeDtypeStruct((N_HEADS, 1, LANES), F32),
                   jax.ShapeDtypeStruct((N_HEADS, 1, LANES), F32),
                   jax.ShapeDtypeStruct((N_HEADS, 1, LANES), F32)],
        scratch_shapes=[pltpu.VMEM((GROUP_HEADS, LANES, LANES), F32)],
        compiler_params=_cparams(("parallel", "arbitrary")),
    )(xbc, xbc, xbc, dtcol, dtrow, bias, a_log, dskip, states, dy)


def _att_tile(s):
    return _pick(s, (512, 256, 128))


def _tri(t, transposed=False):
    r = lax.broadcasted_iota(jnp.int32, (t, t), 0)
    c = lax.broadcasted_iota(jnp.int32, (t, t), 1)
    return (r <= c) if transposed else (c <= r)


def _rows_at(ref, blk, t):
    return ref[pl.ds(pl.multiple_of(blk * t, t), t), :]


def _flash_fwd(q, k, v, *, name):
    s = q.shape[0]
    t = _att_tile(s)
    nq = s // t

    def body(q_ref, k_ref, v_ref, o_ref, lse_ref):
        i = pl.program_id(1)
        qv = q_ref[...]

        def step(j, carry, diagonal):
            m_old, l_old, acc = carry
            sc = _dot(qv, _rows_at(k_ref, j, t), tb=True)
            if diagonal:
                sc = jnp.where(_tri(t), sc, -jnp.inf)
            m_new = jnp.maximum(m_old, jnp.max(sc, axis=1, keepdims=True))
            alpha = jnp.exp(m_old - m_new)
            p = jnp.exp(sc - m_new)
            return (m_new, alpha * l_old + jnp.sum(p, axis=1, keepdims=True),
                    alpha * acc + _dot(p, _rows_at(v_ref, j, t)))

        init = (jnp.full((t, 1), -jnp.inf, F32), jnp.zeros((t, 1), F32), jnp.zeros((t, LANES), F32))
        carry = lax.fori_loop(0, i, lambda j, c: step(j, c, False), init)
        m_fin, l_fin, acc = step(i, carry, True)
        o_ref[...] = (acc / l_fin).astype(o_ref.dtype)
        lse_ref[0] = m_fin + jnp.log(l_fin)

    q_spec = pl.BlockSpec((t, LANES), lambda h, i: (i, h))
    kv_spec = pl.BlockSpec((s, LANES), lambda h, i: (0, h))
    return pl.pallas_call(
        body, name=name, grid=(N_HEADS, nq),
        in_specs=[q_spec, kv_spec, kv_spec],
        out_specs=[q_spec, pl.BlockSpec((1, t, 1), lambda h, i: (h, i, 0))],
        out_shape=[jax.ShapeDtypeStruct(q.shape, BF16), jax.ShapeDtypeStruct((N_HEADS, s, 1), F32)],
        compiler_params=_cparams(("parallel", "arbitrary")),
    )(q, k, v)


def _flash_bwd_dq(q, k, v, o, do, lse, *, name):
    s = q.shape[0]
    t = _att_tile(s)
    nq = s // t

    def body(q_ref, k_ref, v_ref, o_ref, do_ref, lse_ref, dq_ref, dl_ref):
        i = pl.program_id(1)
        qv, dov, lse = q_ref[...], do_ref[...], lse_ref[0]
        delta = jnp.sum(dov.astype(F32) * o_ref[...].astype(F32), axis=1, keepdims=True)
        dl_ref[0] = delta

        def step(j, acc, diagonal):
            kj = _rows_at(k_ref, j, t)
            p = jnp.exp(_dot(qv, kj, tb=True) - lse)
            if diagonal:
                p = jnp.where(_tri(t), p, 0.0)
            ds = p * (_dot(dov, _rows_at(v_ref, j, t), tb=True) - delta)
            return acc + _dot(ds, kj)

        acc = lax.fori_loop(0, i, lambda j, c: step(j, c, False), jnp.zeros((t, LANES), F32))
        dq_ref[...] = step(i, acc, True) * ATT_SCALE

    q_spec = pl.BlockSpec((t, LANES), lambda h, i: (i, h))
    kv_spec = pl.BlockSpec((s, LANES), lambda h, i: (0, h))
    col_spec = pl.BlockSpec((1, t, 1), lambda h, i: (h, i, 0))
    return pl.pallas_call(
        body, name=name, grid=(N_HEADS, nq),
        in_specs=[q_spec, kv_spec, kv_spec, q_spec, q_spec, col_spec],
        out_specs=[q_spec, col_spec],
        out_shape=[jax.ShapeDtypeStruct(q.shape, F32), jax.ShapeDtypeStruct((N_HEADS, s, 1), F32)],
        compiler_params=_cparams(("parallel", "arbitrary")),
    )(q, k, v, o, do, lse)


def _flash_bwd_dkv(q, k, v, do, lse_row, delta_row, *, name):
    s = q.shape[0]
    t = _att_tile(s)
    nq = s // t

    def body(q_ref, k_ref, v_ref, do_ref, lse_ref, dl_ref, dk_ref, dv_ref):
        j = pl.program_id(1)
        kv, vv = k_ref[...], v_ref[...]

        def step(i, carry, diagonal):
            dk, dv = carry
            qi, doi = _rows_at(q_ref, i, t), _rows_at(do_ref, i, t)
            cols = pl.ds(pl.multiple_of(i * t, t), t)
            p_t = jnp.exp(_dot(kv, qi, tb=True) - lse_ref[0, :, cols])
            if diagonal:
                p_t = jnp.where(_tri(t, transposed=True), p_t, 0.0)
            ds_t = p_t * (_dot(vv, doi, tb=True) - dl_ref[0, :, cols])
            return dk + _dot(ds_t, qi), dv + _dot(p_t, doi)

        zero = jnp.zeros((t, LANES), F32)
        carry = step(j, (zero, zero), True)
        dk, dv = lax.fori_loop(j + 1, nq, lambda i, c: step(i, c, False), carry)
        dk_ref[...] = dk
        dv_ref[...] = dv

    q_spec = pl.BlockSpec((s, LANES), lambda h, j: (0, h))
    kv_spec = pl.BlockSpec((t, LANES), lambda h, j: (j, h))
    row_spec = pl.BlockSpec((1, 1, s), lambda h, j: (h, 0, 0))
    return pl.pallas_call(
        body, name=name, grid=(N_HEADS, nq),
        in_specs=[q_spec, kv_spec, kv_spec, q_spec, row_spec, row_spec],
        out_specs=[kv_spec, kv_spec],
        out_shape=[jax.ShapeDtypeStruct(q.shape, F32)] * 2,
        compiler_params=_cparams(("parallel", "arbitrary")),
    )(q, k, v, do, lse_row, delta_row)


def _rope(v, cos_t, sin_p, sin_m):
    return v * cos_t + pltpu.roll(v, QK_ROPE // 2, 1) * sin_p + pltpu.roll(v, LANES - QK_ROPE // 2, 1) * sin_m


def _rope_t(d, cos_t, sin_p, sin_m):
    return d * cos_t + pltpu.roll(d * sin_p, LANES - QK_ROPE // 2, 1) + pltpu.roll(d * sin_m, QK_ROPE // 2, 1)


def _att_prep(q_pad, kv2, kr, cos_t, sin_p, sin_m, *, name):
    w = N_HEADS * LANES

    def fn(qv, kvv, krv, c, sp, sm):
        kr_rot = _rope(krv, c, sp, sm)
        qs, ks = [], []
        for h in range(N_HEADS):
            blk = slice(h * LANES, (h + 1) * LANES)
            qs.append(_rope(qv[:, blk], c, sp, sm) * ATT_SCALE)
            ks.append(kvv[:, blk] + kr_rot)
        return jnp.concatenate(qs, axis=1), jnp.concatenate(ks, axis=1), kvv[:, w:]

    return _rowwise(fn, [q_pad, kv2, kr, cos_t, sin_p, sin_m], [],
                    [(w, BF16, "row"), (w, BF16, "row"), (w, BF16, "row")], name=name)


def _att_prep_bwd(dq, dk, cos_t, sin_p, sin_m, *, name):
    w = N_HEADS * LANES

    def fn(dqv, dkv, c, sp, sm):
        outs, dkr = [], None
        for h in range(N_HEADS):
            blk = slice(h * LANES, (h + 1) * LANES)
            outs.append(_rope_t(dqv[:, blk], c, sp, sm))
            dkr = dkv[:, blk] if dkr is None else dkr + dkv[:, blk]
        return jnp.concatenate(outs, axis=1), _rope_t(dkr, c, sp, sm)

    return _rowwise(fn, [dq, dk, cos_t, sin_p, sin_m], [], [(w, BF16, "row"), (LANES, F32, "row")], name=name)


_ANY = pl.BlockSpec(memory_space=pl.ANY)
_MESH = pl.DeviceIdType.MESH


def _mesh_pos():
    return lax.axis_index("x"), lax.axis_index("y"), lax.axis_index("c")


def _remote(src, dst, send_sem, recv_sem, dev):
    return pltpu.make_async_remote_copy(src_ref=src, dst_ref=dst, send_sem=send_sem, recv_sem=recv_sem,
                                        device_id=dev, device_id_type=_MESH)


def _other_chips(x, y):
    chips = [(1 - x, y), (x, 1 - y), (1 - x, 1 - y)]
    return chips, [2 * cx + cy for cx, cy in chips]


def _comm_call(body, ins, out_shapes, n_sems, *, name):
    return pl.pallas_call(
        body, name=name, in_specs=[_ANY] * len(ins), out_specs=[_ANY] * len(out_shapes), out_shape=out_shapes,
        scratch_shapes=[pltpu.SemaphoreType.DMA((k,)) for k in n_sems],
    )(*ins)


def _gather_halves(shards):
    n = len(shards)
    halves = [t.shape[0] // 2 for t in shards]

    def body(*refs):
        xs, outs = refs[:n], refs[n:2 * n]
        send_sems, recv_sems, local_sems = refs[2 * n:]
        x, y, c = _mesh_pos()
        k = 2 * x + y
        sibling = (x, y, 1 - c)
        chips, ks = _other_chips(x, y)
        half = lambda w, hf: pl.ds(hf * halves[w], halves[w])
        local = [pltpu.make_async_copy(xs[w], outs[w].at[k], local_sems.at[w]) for w in range(n)]
        for cp in local:
            cp.start()
        first = [_remote(xs[w].at[half(w, c)], outs[w].at[k, half(w, c)], send_sems.at[6 * w + j], recv_sems.at[6 * w + j],
                         (*chips[j], c)) for w in range(n) for j in range(3)]
        for cp in first:
            cp.start()
        passed = []
        for j in range(3):
            for w in range(n):
                land = outs[w].at[ks[j], half(w, c)]
                _remote(land, land, send_sems.at[6 * w + j], recv_sems.at[6 * w + j], sibling).wait_recv()
                passed.append(_remote(land, land, send_sems.at[6 * w + 3 + j], recv_sems.at[6 * w + 3 + j], sibling))
                passed[-1].start()
        for j in range(3):
            for w in range(n):
                land = outs[w].at[ks[j], half(w, 1 - c)]
                _remote(land, land, send_sems.at[6 * w + 3 + j], recv_sems.at[6 * w + 3 + j], sibling).wait_recv()
        for cp in first + passed:
            cp.wait_send()
        for cp in local:
            cp.wait()

    shapes = [jax.ShapeDtypeStruct((4,) + t.shape, t.dtype) for t in shards]
    return _comm_call(body, shards, shapes, (6 * n, 6 * n, n), name="gather_halves")


_HBM = pl.BlockSpec(memory_space=pltpu.HBM)
_SEM = pl.BlockSpec(memory_space=pltpu.SEMAPHORE)
_EFFECT = pltpu.SideEffectType.DATAFLOW_SIDE_EFFECTING


def _push_start(shards):
    n = len(shards)

    def body(*refs):
        xs, lands = refs[:n], refs[n:2 * n]
        send_sems, recv_sems = refs[2 * n], refs[2 * n + 1]
        token = refs[-1]
        x, y, c = _mesh_pos()
        k = 2 * x + y
        chips, _ = _other_chips(x, y)
        for w in range(n):
            for j in range(3):
                _remote(xs[w], lands[w].at[k], send_sems.at[3 * w + j], recv_sems.at[3 * w + j], (*chips[j], c)).start()
        token[...] = jnp.zeros_like(token)

    hbm = lambda shape, dtype: pltpu.with_memory_space_constraint(lax.empty(shape, dtype), pltpu.HBM)
    ins = [pltpu.with_memory_space_constraint(t, pltpu.HBM) for t in shards]
    ins += [hbm((4,) + t.shape, t.dtype) for t in shards]
    out_shape = [pltpu.SemaphoreType.DMA((3 * n,)), pltpu.SemaphoreType.DMA((3 * n,))]
    out_shape += [pltpu.HBM(t.shape, t.dtype) for t in ins]
    out_shape += [jax.ShapeDtypeStruct((8, LANES), F32)]
    res = pl.pallas_call(
        body, name="push_start", out_shape=out_shape, in_specs=[_HBM] * (2 * n),
        out_specs=[_SEM, _SEM] + [_HBM] * (2 * n) + [pl.BlockSpec(memory_space=pltpu.VMEM)],
        input_output_aliases={i: 2 + i for i in range(2 * n)},
        compiler_params=pltpu.CompilerParams(has_side_effects=_EFFECT),
    )(*ins)
    return res[0], res[1], res[2:2 + n], res[2 + n:2 + 2 * n], res[-1]


def _push_wait(send_sems, recv_sems, blocks, lands, after):
    n = len(blocks)

    def body(*refs):
        xs, lands_in = refs[:n], refs[n:2 * n]
        send_sems, recv_sems = refs[2 * n], refs[2 * n + 1]
        x, y, c = _mesh_pos()
        chips, ks = _other_chips(x, y)
        for w in range(n):
            for j in range(3):
                cp = _remote(xs[w], lands_in[w].at[ks[j]], send_sems.at[3 * w + j], recv_sems.at[3 * w + j], (*chips[j], c))
                cp.wait_send()
                cp.wait_recv()

    out_shape = [pltpu.HBM(t.shape, t.dtype) for t in list(blocks) + list(lands)]
    res = pl.pallas_call(
        body, name="push_wait", out_shape=out_shape,
        in_specs=[_HBM] * (2 * n) + [_SEM, _SEM, pl.BlockSpec(memory_space=pl.ANY)], out_specs=[_HBM] * (2 * n),
        input_output_aliases={i: i for i in range(2 * n)},
        compiler_params=pltpu.CompilerParams(has_side_effects=_EFFECT),
    )(*blocks, *lands, send_sems, recv_sems, after)
    return res[n:]


def _send_other_layer(g0s, g1s):
    n = len(g0s)

    def body(*refs):
        g0, g1, outs = refs[:n], refs[n:2 * n], refs[2 * n:3 * n]
        send_sems, recv_sems = refs[3 * n:]
        x, y, c = _mesh_pos()
        sibling = (x, y, 1 - c)

        @pl.when(c == 0)
        def _():
            for w in range(n):
                _remote(g1[w], outs[w], send_sems.at[w], recv_sems.at[w], sibling).start()

        @pl.when(c == 1)
        def _():
            for w in range(n):
                _remote(g0[w], outs[w], send_sems.at[w], recv_sems.at[w], sibling).start()

        for w in range(n):
            done = _remote(outs[w], outs[w], send_sems.at[w], recv_sems.at[w], sibling)
            done.wait_send()
            done.wait_recv()

    shapes = [jax.ShapeDtypeStruct(t.shape, t.dtype) for t in g0s]
    return _comm_call(body, list(g0s) + list(g1s), shapes, (n, n), name="send_other_layer")


def _scatter_join(parts):
    n = len(parts)

    def body(*refs):
        ps, outs = refs[:n], refs[n:2 * n]
        send_sems, recv_sems, local_sems = refs[2 * n:]
        x, y, c = _mesh_pos()
        k = 2 * x + y
        sibling = (x, y, 1 - c)
        chips, ks = _other_chips(x, y)
        sem = lambda w, j: (send_sems.at[7 * w + j], recv_sems.at[7 * w + j])
        local = [pltpu.make_async_copy(ps[w].at[k], outs[w].at[c, k], local_sems.at[w]) for w in range(n)]
        sends = [_remote(ps[w].at[ks[j]], outs[w].at[c, k], *sem(w, j), (*chips[j], c)) for w in range(n) for j in range(3)]
        sends += [_remote(ps[w].at[k], outs[w].at[c, k], *sem(w, 6), sibling) for w in range(n)]
        for cp in local + sends:
            cp.start()
        for j in range(3):
            for w in range(n):
                land = outs[w].at[c, ks[j]]
                _remote(land, land, *sem(w, j), sibling).wait_recv()
                sends.append(_remote(land, land, *sem(w, 3 + j), sibling))
                sends[-1].start()
        for w in range(n):
            land = outs[w].at[1 - c, k]
            _remote(land, land, *sem(w, 6), sibling).wait_recv()
        for j in range(3):
            for w in range(n):
                land = outs[w].at[1 - c, ks[j]]
                _remote(land, land, *sem(w, 3 + j), sibling).wait_recv()
        for cp in sends:
            cp.wait_send()
        for cp in local:
            cp.wait()

    shapes = [jax.ShapeDtypeStruct((2,) + t.shape, t.dtype) for t in parts]
    return _comm_call(body, parts, shapes, (7 * n, 7 * n, n), name="scatter_join")


def _gather_all(vec, *, name):
    r, w = vec.shape

    def body(v_ref, out_ref, send_sems, recv_sems, local_sem):
        x, y, c = _mesh_pos()

        def slot(px, py, pc):
            return out_ref.at[4 * px + 2 * py + pc]

        mine = pltpu.make_async_copy(v_ref, slot(x, y, c), local_sem)
        mine.start()
        peers = []
        for rel in range(1, 8):
            fx, fy, fc = (rel >> 2) & 1, (rel >> 1) & 1, rel & 1
            peers.append((x ^ fx, y ^ fy, c ^ fc))
        cps = [_remote(v_ref, slot(x, y, c), send_sems.at[j], recv_sems.at[j], peer) for j, peer in enumerate(peers)]
        for cp in cps:
            cp.start()
        for j, peer in enumerate(peers):
            _remote(slot(*peer), slot(*peer), send_sems.at[j], recv_sems.at[j], peer).wait_recv()
        for cp in cps:
            cp.wait_send()
        mine.wait()

    return pl.pallas_call(
        body, name=name, in_specs=[_ANY], out_specs=_ANY,
        out_shape=jax.ShapeDtypeStruct((8, r, w), vec.dtype),
        scratch_shapes=[pltpu.SemaphoreType.DMA((7,)), pltpu.SemaphoreType.DMA((7,)), pltpu.SemaphoreType.DMA],
    )(vec)


def _row_tile(rows, row_bytes):
    for tm in (1024, 512, 256, 128, 64, 32, 16):
        if rows % tm == 0 and tm * row_bytes <= ELEMENTWISE_BLOCK_BYTES:
            return tm
    return 16 if rows % 16 == 0 else rows


def _chip_sum(g0, g1, got, c, *, name):
    r, w = g0.shape
    tm = _row_tile(r, w * 4)

    def body(c_ref, g0_ref, g1_ref, o_ref, out_ref):
        mine = jnp.where(c_ref[0] == 0, g0_ref[...], g1_ref[...])
        out_ref[...] = (mine + o_ref[...]).astype(out_ref.dtype)

    return pl.pallas_call(
        body, name=name,
        grid_spec=pltpu.PrefetchScalarGridSpec(
            num_scalar_prefetch=1, grid=(r // tm,),
            in_specs=[pl.BlockSpec((tm, w), lambda i, c_ref: (i * (1 - c_ref[0]), 0)),
                      pl.BlockSpec((tm, w), lambda i, c_ref: (i * c_ref[0], 0)),
                      pl.BlockSpec((tm, w), lambda i, c_ref: (i, 0))],
            out_specs=pl.BlockSpec((tm, w), lambda i, c_ref: (i, 0))),
        out_shape=jax.ShapeDtypeStruct((r, w), BF16),
        compiler_params=_cparams(("arbitrary",)),
    )(jnp.reshape(c, (1,)).astype(jnp.int32), g0, g1, got)


def _sum_slots(stack, *, name):
    n, r, w = stack.shape
    tm = _row_tile(r, n * w * stack.dtype.itemsize)

    def body(s_ref, out_ref):
        acc = s_ref[0].astype(F32)
        for i in range(1, n):
            acc = acc + s_ref[i].astype(F32)
        out_ref[...] = acc

    return pl.pallas_call(
        body, name=name, grid=(r // tm,),
        in_specs=[pl.BlockSpec((n, tm, w), lambda i: (0, i, 0))],
        out_specs=pl.BlockSpec((tm, w), lambda i: (i, 0)),
        out_shape=jax.ShapeDtypeStruct((r, w), F32),
        compiler_params=_cparams(("parallel",)),
    )(stack)


def _adam_math(wv, gv, mv, vv):
    m_new = ADAM_B1 * mv + (1.0 - ADAM_B1) * gv
    v_new = ADAM_B2 * vv + (1.0 - ADAM_B2) * (gv * gv)
    m_hat = m_new / (1.0 - ADAM_B1 ** ADAM_STEP)
    v_hat = v_new / (1.0 - ADAM_B2 ** ADAM_STEP)
    delta = -ADAM_LR * (m_hat / (jnp.sqrt(v_hat) + ADAM_EPS) + ADAM_WD * wv)
    return delta, m_new, v_new


def _adamw(w, g, m, v, *, name):
    shape = w.shape
    cols = shape[-1]
    flat = lambda t: t.reshape(-1, cols)
    rows = flat(w).shape[0]
    tm = _pick(rows, (256, 128, 64, 32, 16, 8))
    outs = _rowwise(_adam_math, [flat(w), flat(g), flat(m), flat(v)], [], [(cols, F32, "row")] * 3, name=name, tm=tm)
    return tuple(o.reshape(shape) for o in outs)


def _adamw_slots(w, slots, m, v, *, name):
    shape = w.shape
    cols = shape[-1]
    v3 = lambda t: t.reshape(2, -1, cols)
    rows = v3(w).shape[1]
    assert slots.shape == (2, 4, rows, cols), (slots.shape, shape)
    tm = _row_tile(rows, cols * 4)

    def body(w_ref, s_ref, m_ref, v_ref, g_ref, d_ref, mo_ref, vo_ref):
        g = s_ref[0, 0].astype(F32)
        for i in range(1, 4):
            g = g + s_ref[0, i].astype(F32)
        delta, m_new, v_new = _adam_math(w_ref[0], g, m_ref[0], v_ref[0])
        g_ref[0], d_ref[0], mo_ref[0], vo_ref[0] = g, delta, m_new, v_new

    blk = pl.BlockSpec((1, tm, cols), lambda l, i: (l, i, 0))
    outs = pl.pallas_call(
        body, name=name, grid=(2, rows // tm),
        in_specs=[blk, pl.BlockSpec((1, 4, tm, cols), lambda l, i: (l, 0, i, 0)), blk, blk],
        out_specs=[blk] * 4, out_shape=[jax.ShapeDtypeStruct((2, rows, cols), F32)] * 4,
        compiler_params=_cparams(("parallel", "parallel")),
    )(v3(w), slots, v3(m), v3(v))
    return tuple(o.reshape(shape) for o in outs)


def _pad_blocks(w, axis, n_blocks, real, to=LANES, offset=0):
    axis = axis % w.ndim
    shp = w.shape
    w = w.reshape(shp[:axis] + (n_blocks, real) + shp[axis + 1:])
    pads = [(0, 0)] * w.ndim
    pads[axis + 1] = (offset, to - real - offset)
    w = jnp.pad(w, pads)
    return w.reshape(shp[:axis] + (n_blocks * to,) + shp[axis + 1:])


def _unpad_blocks(w, axis, n_blocks, real, to=LANES, offset=0):
    axis = axis % w.ndim
    shp = w.shape
    w = w.reshape(shp[:axis] + (n_blocks, to) + shp[axis + 1:])
    w = lax.slice_in_dim(w, offset, offset + real, axis=axis + 1)
    return w.reshape(shp[:axis] + (n_blocks * real,) + shp[axis + 1:])


def _block_diag(w):
    n, a, b = w.shape
    eye = jnp.eye(n, dtype=w.dtype)
    return (eye[:, None, :, None] * w[:, :, None, :]).reshape(n * a, n * b)


def _block_diag_t(d, n):
    a, b = d.shape[0] // n, d.shape[1] // n
    d = d.reshape(n, a, n, b)
    return jnp.stack([d[i, :, i, :] for i in range(n)])


_SPLITS = np.cumsum((0,) + SPLIT_SIZES)


def _w_in_groups(w_in):
    sl = lambda i: w_in[:, _SPLITS[i]:_SPLITS[i + 1]]
    xbc = sl(5)
    xbc_pad = jnp.concatenate([_pad_blocks(xbc[:, :MIX], 1, N_HEADS, HEAD),
                               _pad_blocks(xbc[:, MIX:MIX + 2 * HEAD], 1, 2, HEAD),
                               _pad_blocks(xbc[:, MIX + 2 * HEAD:], 1, 2, HEAD)], axis=1)
    return dict(
        cq=sl(0), ckv=sl(1), kr=_pad_blocks(sl(2), 1, 1, QK_ROPE, offset=HEAD), pool=sl(3),
        z=_pad_blocks(sl(4), 1, N_HEADS, HEAD), xbc=xbc_pad, dt=_pad_blocks(sl(6), 1, 1, N_HEADS),
        lru_g=sl(7), lru_x=sl(8), gates=sl(9))


def _w_in_fused(groups):
    parts, at = [], 0
    for name, off, width in IN_LAYOUT:
        assert groups[name].shape[1] == width and off >= at
        if off > at:
            parts.append(jnp.zeros((groups[name].shape[0], off - at), groups[name].dtype))
        parts.append(groups[name])
        at = off + width
    parts.append(jnp.zeros((parts[0].shape[0], IN_ALL_COLS - at), parts[0].dtype))
    return jnp.concatenate(parts, axis=1)


def _in_cols(arr, name):
    off, width = IN_OFFSETS[name]
    return _Cols(arr, off, width)


def _w_in_ungroup(d):
    xbc = d["xbc"]
    w = N_HEADS * LANES
    xbc_real = jnp.concatenate([_unpad_blocks(xbc[:, :w], 1, N_HEADS, HEAD),
                                _unpad_blocks(xbc[:, w:w + 2 * LANES], 1, 2, HEAD),
                                _unpad_blocks(xbc[:, w + 2 * LANES:], 1, 2, HEAD)], axis=1)
    return jnp.concatenate([d["cq"], d["ckv"], _unpad_blocks(d["kr"], 1, 1, QK_ROPE, offset=HEAD), d["pool"],
                            _unpad_blocks(d["z"], 1, N_HEADS, HEAD), xbc_real, _unpad_blocks(d["dt"], 1, 1, N_HEADS),
                            d["lru_g"], d["lru_x"], d["gates"]], axis=1)


def _pad_xbc_vec(v):
    return jnp.concatenate([_pad_blocks(v[..., :MIX], -1, N_HEADS, HEAD),
                            _pad_blocks(v[..., MIX:MIX + 2 * HEAD], -1, 2, HEAD),
                            _pad_blocks(v[..., MIX + 2 * HEAD:], -1, 2, HEAD)], axis=-1)


def _unpad_xbc_vec(v):
    w = N_HEADS * LANES
    return jnp.concatenate([_unpad_blocks(v[..., :w], -1, N_HEADS, HEAD),
                            _unpad_blocks(v[..., w:w + 2 * LANES], -1, 2, HEAD),
                            _unpad_blocks(v[..., w + 2 * LANES:], -1, 2, HEAD)], axis=-1)


def _layer_weights(p):
    q = dict(p)
    q["in_all"] = _w_in_fused(_w_in_groups(p["w_in"]))
    q["uq"] = _pad_blocks(p["w_uq"], 1, N_HEADS, HEAD + QK_ROPE)
    ukv = p["w_ukv"].reshape(KV_LORA, N_HEADS, 2 * HEAD)
    q["ukv"] = jnp.concatenate([_pad_blocks(ukv[:, :, :HEAD].reshape(KV_LORA, -1), 1, N_HEADS, HEAD),
                                _pad_blocks(ukv[:, :, HEAD:].reshape(KV_LORA, -1), 1, N_HEADS, HEAD)], axis=1)
    q["pool_bd"] = _block_diag(p["w_pool"])
    q["lru_bd"] = jnp.concatenate([_block_diag(p["lru_w_a"]), _block_diag(p["lru_w_i"])], axis=1)
    q["br"] = [_pad_blocks(p["w_branch"][0], 0, N_HEADS, HEAD), p["w_branch"][1],
               _pad_blocks(p["w_branch"][2], 0, N_HEADS, HEAD), p["w_branch"][3]]
    q["ssd_conv_w_pad"] = _pad_xbc_vec(p["ssd_conv_w"])
    q["ssd_conv_b_pad"] = _pad_xbc_vec(p["ssd_conv_b"])[None, :]
    q["ssd_norm_pad"] = _pad_blocks(p["ssd_norm"], 0, N_HEADS, HEAD)[None, :]
    return q


def _row(v):
    return v.reshape(1, -1)


def _scal3(v):
    return v.reshape(N_HEADS, 1, 1)


def _layer_fwd(x, p_emb, w, rope, tag):
    n = lambda s: f"{s}_{tag}"
    sv = {"x": x}
    h = _rms_fwd(x, _row(w["g_mix"]), name=n("rms_mix"))
    sv["h"] = h
    u_all = _mm(h, w["in_all"], name=n("in_proj"))
    u = {k: _in_cols(u_all, k) for k in IN_OFFSETS}
    sv["u"] = u

    cqn = _rms_fwd(u["cq"], _row(w["q_norm"]), name=n("rms_q"))
    ckvn = _rms_fwd(u["ckv"], _row(w["kv_norm"]), name=n("rms_kv"))
    q_pad = _mm(cqn, w["uq"], name=n("uq"))
    kv2 = _mm(ckvn, w["ukv"], name=n("ukv"))
    qc, kc, vc = _att_prep(q_pad, kv2, u["kr"], *rope, name=n("att_prep"))
    y_a, lse = _flash_fwd(qc, kc, vc, name=n("flash_fwd"))
    sv.update(cqn=cqn, ckvn=ckvn, qc=qc, kc=kc, vc=vc, y_a=y_a, lse=lse)

    pool_d = _pool_fwd(u["pool"], name=n("pool_fwd"))
    yb_pre, y_b = _mm(pool_d, w["pool_bd"], epilogue=lambda acc, sc: (acc, acc * sc),
                      rowvecs=[_row(w["pool_scale"])], out_dtypes=(F32, BF16), name=n("pool_mm"))
    sv.update(pool_d=pool_d, yb_pre=yb_pre, y_b=y_b)

    xbc_c = _conv_fwd(u["xbc"], w["ssd_conv_w_pad"], w["ssd_conv_b_pad"], silu=True, name=n("ssd_conv"))
    dt8 = lax.slice_in_dim(u_all, IN_OFFSETS["dt"][0], IN_OFFSETS["dt"][0] + N_HEADS, axis=1)
    dtcol = dt8.T[:, :, None]
    dtrow = dt8.T[:, None, :]
    ssd_par = (_scal3(w["ssd_dt_bias"]), _scal3(w["ssd_a_log"]), _scal3(w["ssd_d"]))
    y_ssd, states = _ssd_fwd(xbc_c, dtcol, dtrow, *ssd_par, name=n("ssd_fwd"))

    def ssd_post(yv, zv, gv):
        xh, _ = _rms_parts(yv * _silu(zv), MIX)
        return xh * gv

    y_c = _rowwise(ssd_post, [y_ssd, u["z"]], [w["ssd_norm_pad"]], [(N_HEADS * LANES, BF16, "row")], name=n("ssd_post"))
    sv.update(xbc_c=xbc_c, dtcol=dtcol, dtrow=dtrow, y_ssd=y_ssd, states=states, y_c=y_c)

    xc = _conv_fwd(u["lru_x"], w["lru_conv_w"], _row(w["lru_conv_b"]), silu=False, name=n("lru_conv"))
    pre = _mm(xc, w["lru_bd"], name=n("lru_mm"))
    lru_par = (_row(w["lru_lambda"]), _row(w["lru_b_a"]), _row(w["lru_b_i"]))
    y_d, h_lru = _lru_fwd(pre, xc, u["lru_g"], *lru_par, name=n("lru_fwd"))
    sv.update(xc=xc, pre=pre, h_lru=h_lru, y_d=y_d)

    ys = [y_a, y_b, y_c, y_d]
    merged, ybs = None, []
    for b in range(4):
        if merged is None:
            merged, yb = _mm(ys[b], w["br"][b], epilogue=lambda acc, gt: (_sigmoid(gt) * acc, acc),
                             tiles=[_Cols(u_all, b * D_MODEL, D_MODEL)], out_dtypes=(F32, F32), name=n(f"branch{b}"))
        else:
            merged, yb = _mm(ys[b], w["br"][b], epilogue=lambda acc, gt, mg: (mg + _sigmoid(gt) * acc, acc),
                             tiles=[_Cols(u_all, b * D_MODEL, D_MODEL), merged], out_dtypes=(F32, F32),
                             name=n(f"branch{b}"))
        ybs.append(yb)
    x1 = _mm(merged, w["w_out"], epilogue=lambda acc, xr: (acc + xr,), tiles=[x], name=n("out_proj"))
    sv.update(ybs=ybs, merged=merged, x1=x1)

    h2 = _rms_fwd(x1, _row(w["g_mlp"]), name=n("rms_mlp"))
    a_ff, f_ff = _mm(h2, w["w_ff1"], epilogue=lambda acc: (acc, jnp.square(jnp.maximum(acc, 0.0))),
                     out_dtypes=(F32, BF16), name=n("ff1"))
    x2 = _mm(f_ff, w["w_ff2"], epilogue=lambda acc, xr: (acc + xr,), tiles=[x1], name=n("ff2"))
    sv.update(h2=h2, a_ff=a_ff, f_ff=f_ff, x2=x2)

    h3 = _rms_fwd(x2, _row(w["g_ple"]), name=n("rms_ple"))
    e_ple = _mm(p_emb, w["w_ple"], name=n("ple_emb"))
    x3, gt_ple = _mm(h3, w["w_ple_gate"], epilogue=lambda acc, ev, xr: (xr + ev * _sigmoid(acc), _sigmoid(acc)),
                     tiles=[e_ple, x2], out_dtypes=(F32, F32), name=n("ple_gate"))
    sv.update(h3=h3, e_ple=e_ple, gt_ple=gt_ple, p_emb=p_emb)
    return x3, sv


def _layer_bwd(dx3, sv, w, rope, tag):
    n = lambda s: f"{s}_{tag}"
    gr = {}
    u = sv["u"]

    de, dpre = _rowwise(lambda d, gt, ev: (d * gt, d * ev * gt * (1.0 - gt)), [dx3, sv["gt_ple"], sv["e_ple"]], [],
                        [(D_MODEL, BF16, "row"), (D_MODEL, BF16, "row")], name=n("ple_bwd"))
    gr["w_ple"] = _mm(sv["p_emb"], de, ta=True, name=n("d_w_ple"))
    gr["w_ple_gate"] = _mm(sv["h3"], dpre, ta=True, name=n("d_w_ple_gate"))
    dh3 = _mm(dpre, w["w_ple_gate"], tb=True, out_dtypes=(BF16,), name=n("d_h3"))
    dx2, dg = _rms_bwd(sv["x2"], _row(w["g_ple"]), dh3, dx3, name=n("rms_ple_bwd"))
    gr["g_ple"] = dg[0]

    gr["w_ff2"] = _mm(sv["f_ff"], dx2, ta=True, name=n("d_w_ff2"))
    da = _mm(dx2, w["w_ff2"], tb=True, epilogue=lambda acc, av: (acc * 2.0 * jnp.maximum(av, 0.0),),
             tiles=[sv["a_ff"]], out_dtypes=(BF16,), name=n("d_a_ff"))
    gr["w_ff1"] = _mm(sv["h2"], da, ta=True, name=n("d_w_ff1"))
    dh2 = _mm(da, w["w_ff1"], tb=True, out_dtypes=(BF16,), name=n("d_h2"))
    dx1, dg = _rms_bwd(sv["x1"], _row(w["g_mlp"]), dh2, dx2, name=n("rms_mlp_bwd"))
    gr["g_mlp"] = dg[0]

    gr["w_out"] = _mm(sv["merged"], dx1, ta=True, name=n("d_w_out"))
    dmerged = _mm(dx1, w["w_out"], tb=True, name=n("d_merged"))

    def merge_bwd(dm, gts, y0, y1, y2, y3):
        dys, dgs = [], []
        for b, yb in enumerate((y0, y1, y2, y3)):
            sg = _sigmoid(gts[:, b * D_MODEL:(b + 1) * D_MODEL])
            dys.append(dm * sg)
            dgs.append(dm * yb * sg * (1.0 - sg))
        return (*dys, jnp.concatenate(dgs, axis=1))

    *dybs, dgates = _rowwise(merge_bwd, [dmerged, u["gates"]] + sv["ybs"], [],
                             [(D_MODEL, BF16, "row")] * 4 + [(4 * D_MODEL, BF16, "row")], name=n("merge_bwd"))
    ys = [sv["y_a"], sv["y_b"], sv["y_c"], sv["y_d"]]
    dwb = [_mm(ys[b], dybs[b], ta=True, name=n(f"d_w_branch{b}")) for b in range(4)]
    gr["w_branch"] = jnp.stack([_unpad_blocks(dwb[0], 0, N_HEADS, HEAD), dwb[1],
                                _unpad_blocks(dwb[2], 0, N_HEADS, HEAD), dwb[3]])
    dy_a = _mm(dybs[0], w["br"][0], tb=True, out_dtypes=(BF16,), name=n("d_y_a"))
    dy_b = _mm(dybs[1], w["br"][1], tb=True, name=n("d_y_b"))
    dy_c = _mm(dybs[2], w["br"][2], tb=True, name=n("d_y_c"))
    dy_d = _mm(dybs[3], w["br"][3], tb=True, name=n("d_y_d"))
    du = {"gates": dgates}

    lru_par = (_row(w["lru_lambda"]), _row(w["lru_b_a"]), _row(w["lru_b_i"]))
    dpa, dpi, dxc_direct, du["lru_g"], dlam, dba, dbi = _lru_bwd(
        sv["pre"], sv["xc"], u["lru_g"], *lru_par, sv["h_lru"], dy_d, name=n("lru_bwd"))
    dpre_lru = jnp.concatenate([dpa, dpi], axis=1)
    d_bd = _mm(sv["xc"], dpre_lru, ta=True, name=n("d_lru_w"))
    gr["lru_w_a"] = _block_diag_t(d_bd[:, :MIX], N_HEADS)
    gr["lru_w_i"] = _block_diag_t(d_bd[:, MIX:], N_HEADS)
    gr["lru_lambda"], gr["lru_b_a"], gr["lru_b_i"] = dlam[0], dba[0], dbi[0]
    dxc = _mm(dpre_lru, w["lru_bd"], tb=True, epilogue=lambda acc, t: (acc + t,), tiles=[dxc_direct], name=n("d_xc"))
    du["lru_x"], gr["lru_conv_w"], dcb = _conv_bwd(u["lru_x"], w["lru_conv_w"], _row(w["lru_conv_b"]), dxc,
                                                  silu=False, name=n("lru_conv_bwd"))
    gr["lru_conv_b"] = dcb[0]

    def ssd_post_bwd(dyc, yv, zv, gv):
        sz = _silu(zv)
        dyz, dgain = _rms_bwd_math(yv * sz, gv, dyc, MIX)
        return dyz * sz, dyz * yv * _silu_grad(zv), dgain

    dy_ssd, du["z"], dgain = _rowwise(ssd_post_bwd, [dy_c, sv["y_ssd"], u["z"]], [w["ssd_norm_pad"]],
                                      [(N_HEADS * LANES, F32, "row"), (N_HEADS * LANES, BF16, "row"),
                                       (N_HEADS * LANES, F32, "acc")], name=n("ssd_post_bwd"))
    gr["ssd_norm"] = _unpad_blocks(dgain[0], 0, N_HEADS, HEAD)
    ssd_par = (_scal3(w["ssd_dt_bias"]), _scal3(w["ssd_a_log"]), _scal3(w["ssd_d"]))
    dxs, dbg, dcg, ddt, dbias, dalog, dd = _ssd_bwd(sv["xbc_c"], sv["dtcol"], sv["dtrow"], *ssd_par, sv["states"],
                                                    dy_ssd, name=n("ssd_bwd"))
    s = dxs.shape[0]
    dxbc_c = jnp.concatenate([dxs, dbg, dcg], axis=1)
    gr["ssd_dt_bias"], gr["ssd_a_log"], gr["ssd_d"] = dbias[:, 0, 0], dalog[:, 0, 0], dd[:, 0, 0]
    du["xbc"], dcw, dcb = _conv_bwd(u["xbc"], w["ssd_conv_w_pad"], w["ssd_conv_b_pad"], dxbc_c, silu=True,
                                    name=n("ssd_conv_bwd"))
    gr["ssd_conv_w"], gr["ssd_conv_b"] = _unpad_xbc_vec(dcw), _unpad_xbc_vec(dcb[0])
    du["dt"] = jnp.pad(ddt[:, :, 0].T, ((0, 0), (0, LANES - N_HEADS)))

    dyb_pre, dscale = _rowwise(lambda d, yp, sc: (d * sc, _colsum(d * yp)), [dy_b, sv["yb_pre"]],
                               [_row(w["pool_scale"])], [(MIX, BF16, "row"), (MIX, F32, "acc")], name=n("pool_scale_bwd"))
    gr["pool_scale"] = dscale[0]
    gr["w_pool"] = _block_diag_t(_mm(sv["pool_d"], dyb_pre, ta=True, name=n("d_w_pool")), 4)
    dd_pool = _mm(dyb_pre, w["pool_bd"], tb=True, name=n("d_pool_d"))
    du["pool"] = _pool_bwd(dd_pool, name=n("pool_bwd"))

    dqc, delta = _flash_bwd_dq(sv["qc"], sv["kc"], sv["vc"], sv["y_a"], dy_a, sv["lse"], name=n("flash_dq"))
    to_row = lambda t: t.reshape(N_HEADS, 1, s)
    dkc, dvc = _flash_bwd_dkv(sv["qc"], sv["kc"], sv["vc"], dy_a, to_row(sv["lse"]), to_row(delta), name=n("flash_dkv"))
    dq_pad, du["kr"] = _att_prep_bwd(dqc, dkc, *rope, name=n("att_prep_bwd"))
    d_uq = _mm(sv["cqn"], dq_pad, ta=True, name=n("d_w_uq"))
    gr["w_uq"] = _unpad_blocks(d_uq, 1, N_HEADS, HEAD + QK_ROPE)
    dcqn = _mm(dq_pad, w["uq"], tb=True, out_dtypes=(BF16,), name=n("d_cqn"))
    du["cq"], dg = _rms_bwd(u["cq"], _row(w["q_norm"]), dcqn, name=n("rms_q_bwd"))
    gr["q_norm"] = dg[0]
    dkv2 = jnp.concatenate([dkc, dvc], axis=1).astype(BF16)
    d_ukv = _mm(sv["ckvn"], dkv2, ta=True, name=n("d_w_ukv"))
    wk = N_HEADS * LANES
    dk_real = _unpad_blocks(d_ukv[:, :wk], 1, N_HEADS, HEAD).reshape(KV_LORA, N_HEADS, HEAD)
    dv_real = _unpad_blocks(d_ukv[:, wk:], 1, N_HEADS, HEAD).reshape(KV_LORA, N_HEADS, HEAD)
    gr["w_ukv"] = jnp.concatenate([dk_real, dv_real], axis=2).reshape(KV_LORA, N_HEADS * 2 * HEAD)
    dckvn = _mm(dkv2, w["ukv"], tb=True, out_dtypes=(BF16,), name=n("d_ckvn"))
    du["ckv"], dg = _rms_bwd(u["ckv"], _row(w["kv_norm"]), dckvn, name=n("rms_kv_bwd"))
    gr["kv_norm"] = dg[0]

    du_all = _w_in_fused({k: v.astype(BF16) for k, v in du.items()})
    dw_all = _mm(sv["h"], du_all, ta=True, name=n("d_w_in"))
    gr["w_in"] = _w_in_ungroup({k: dw_all[:, off:off + width] for k, off, width in IN_LAYOUT})
    dh = _mm(du_all, w["in_all"], tb=True, name=n("d_h"))
    dx, dg = _rms_bwd(sv["x"], _row(w["g_mix"]), dh, dx1, name=n("rms_mix_bwd"))
    gr["g_mix"] = dg[0]
    return dx, gr


def _pack_rows(n_elems):
    per = PACK_W * PACK_ROWS
    return -(-n_elems // per) * PACK_ROWS


def _pack_flat(parts, dtype):
    flat = jnp.concatenate([p.reshape(-1).astype(dtype) for p in parts])
    rows = _pack_rows(flat.shape[0])
    return jnp.pad(flat, (0, rows * PACK_W - flat.shape[0])).reshape(rows, PACK_W)


def _unpack_flat(buf, shapes):
    lead = buf.shape[:-2]
    flat = buf.reshape(lead + (-1,))
    out, off = [], 0
    for shp in shapes:
        size = int(np.prod(shp))
        out.append(flat[..., off:off + size].reshape(lead + tuple(shp)))
        off += size
    return out


def _merge_shards(t, axis):
    return jnp.concatenate([t[i] for i in range(4)], axis=axis)


def _split_shards(t, axis):
    return jnp.stack(jnp.split(t, 4, axis=axis))


def _rope_tables(positions):
    inv = 1.0 / (ROPE_THETA ** (jnp.arange(0, QK_ROPE, 2, dtype=F32) / QK_ROPE))
    ang = positions.astype(F32)[:, None] * inv
    cos, sin = jnp.cos(ang), jnp.sin(ang)
    s = ang.shape[0]
    half = QK_ROPE // 2
    z = lambda n_: jnp.zeros((s, n_), F32)
    cos_t = jnp.concatenate([jnp.ones((s, HEAD), F32), cos, cos, jnp.ones((s, LANES - HEAD - QK_ROPE), F32)], axis=1)
    sin_p = jnp.concatenate([z(HEAD + half), sin, z(LANES - HEAD - QK_ROPE)], axis=1)
    sin_m = jnp.concatenate([z(HEAD), -sin, z(half + LANES - HEAD - QK_ROPE)], axis=1)
    return cos_t, sin_p, sin_m


def _loss_head(x, g, target, *, name):
    d = x.shape[1]

    def fn(xv, tv, gv):
        xh, r = _rms_parts(xv, d)
        y = xh * gv
        err = y - tv
        dy = err * (1.0 / d)
        dxh = dy * gv
        dx = r * (dxh - xh * (jnp.sum(dxh * xh, axis=-1, keepdims=True) * (1.0 / d)))
        return dx, _colsum(dy * xh), _colsum(err * err) * (0.5 / d)

    return _rowwise(fn, [x, target], [g], [(d, F32, "row"), (d, F32, "acc"), (d, F32, "acc")], name=name)


def _step(args):
    x = args["x"][0]
    c_idx = lax.axis_index("c")

    mats = [(nm, ax) for nm, ax in BIG if nm not in CONV_SHARDED]
    mine = [[args[nm][l].astype(BF16) for nm, _ in mats] for l in range(2)]
    gathered0 = _gather_halves(mine[0])
    mine1, gathered0 = lax.optimization_barrier((mine[1], gathered0))
    send_sems, recv_sems, blocks1, lands1, token = _push_start(mine1)
    convs = [(nm, ax) for nm, ax in BIG if nm in CONV_SHARDED]
    conv_all = _gather_all(_pack_flat([args[nm] for nm, _ in convs], F32), name="gather_conv_taps")[0::2]
    full_conv = {nm: _merge_shards(t, ax)
                 for (nm, ax), t in zip(convs, _unpack_flat(conv_all, [args[nm].shape for nm, _ in convs]))}
    rope = _rope_tables(args["positions"][0])

    def layer_weights(l, gathered):
        p = {nm: _merge_shards(t, ax - 1) for (nm, ax), t in zip(mats, gathered)}
        p.update({nm: full_conv[nm][l] for nm in CONV_SHARDED})
        p.update({nm: args[nm][l] for nm in SMALL if nm != "g_final"})
        return _layer_weights(p)

    layers = [layer_weights(0, gathered0), None]
    layers[0]["g_mix"] = layers[0]["g_mix"] + token[0, 0]
    x, sv0 = _layer_fwd(x, args["p"][0, 0], layers[0], rope, "l0")
    landed1 = _push_wait(send_sems, recv_sems, blocks1, lands1, x)
    k_chip = 2 * lax.axis_index("x") + lax.axis_index("y")
    gathered1 = [lax.dynamic_update_index_in_dim(t, own, k_chip, 0) for t, own in zip(landed1, mine[1])]
    layers[1] = layer_weights(1, gathered1)
    x, sv1 = _layer_fwd(x, args["p"][1, 0], layers[1], rope, "l1")
    saved = [sv0, sv1]

    dx, dg_final, loss_part = _loss_head(x, _row(args["g_final"]), args["loss_target"][0], name="loss_head")
    loss = lax.psum(jnp.sum(loss_part), ("x", "y", "c"))

    grads = [None, None]
    for l in (1, 0):
        dx, grads[l] = _layer_bwd(dx, saved[l], layers[l], rope, f"l{l}")

    g_local = {nm: jnp.stack([grads[0][nm], grads[1][nm]]) for nm in SMALL if nm != "g_final"}
    g_local["g_final"] = dg_final[0]

    view2d = lambda t: t.reshape(-1, t.shape[-1])
    names = [nm for nm, _ in BIG]
    got = _send_other_layer([view2d(grads[0][nm]) for nm in names], [view2d(grads[1][nm]) for nm in names])
    parts = []
    for (nm, ax), gt in zip(BIG, got):
        both = _chip_sum(view2d(grads[0][nm]), view2d(grads[1][nm]), gt, c_idx, name="chip_sum_" + nm)
        shards = _split_shards(both.reshape(grads[0][nm].shape), ax - 1)
        parts.append(shards.reshape(4, -1, shards.shape[-1]))
    upd, g_red = {}, {}
    for nm, slots in zip(names, _scatter_join(parts)):
        g_red[nm], *upd[nm] = _adamw_slots(args[nm], slots, args["m_" + nm], args["v_" + nm], name="adamw_" + nm)

    small_shapes = [args[nm].shape for nm in SMALL]
    small_sum = _sum_slots(_gather_all(_pack_flat([g_local[nm] for nm in SMALL], F32), name="gather_small_grads"),
                           name="sum_devices")
    g_red.update(zip(SMALL, _unpack_flat(small_sum, small_shapes)))

    pack_small = lambda pre: _pack_flat([args[pre + nm] for nm in SMALL], F32)
    upd_small = _adamw(pack_small(""), small_sum, pack_small("m_"), pack_small("v_"), name="adamw_small")
    upd.update({nm: trip for nm, trip in zip(SMALL, zip(*[_unpack_flat(t, small_shapes) for t in upd_small]))})

    outs = [loss, dx[None]]
    outs += [g_red[nm] for nm in WEIGHTS]
    for i in range(3):
        outs += [upd[nm][i] for nm in WEIGHTS]
    return tuple(outs)


_ARG_NAMES = ("x", "p", "positions") + WEIGHTS + ("loss_target",) + tuple("m_" + nm for nm in WEIGHTS) \
    + tuple("v_" + nm for nm in WEIGHTS)


def kernel(*arrays):
    assert len(arrays) == len(_ARG_NAMES), len(arrays)
    return _step(dict(zip(_ARG_NAMES, arrays)))
```

```python
import functools
import math

import jax
import jax.numpy as jnp
import numpy as np
from jax import lax
from jax.experimental import pallas as pl
from jax.experimental.pallas import tpu as pltpu

F32 = jnp.float32
BF16 = jnp.bfloat16
MXU_DTYPE = BF16
LANES = 128
VMEM_LIMIT = 56 * 1024 * 1024
MM_VMEM_BUDGET = 36 * 1024 * 1024
ELEMENTWISE_BLOCK_BYTES = 2 * 1024 * 1024

D_MODEL = 1024
N_HEADS = 8
HEAD = 64
QK_ROPE = 32
Q_LORA = 384
KV_LORA = 256
MIX = 512
SSD_CHUNK = 128
CONV_W = 4
POOL_WINDOWS = (2, 4, 8, 16)
LRU_C = 8.0
EPS = 1e-6
ROPE_THETA = 10000.0
ATT_SCALE = (HEAD + QK_ROPE) ** -0.5
SPLIT_SIZES = (Q_LORA, KV_LORA, QK_ROPE, MIX, MIX, 768, N_HEADS, MIX, MIX, 4 * D_MODEL)
IN_LAYOUT = (("gates", 0, 4096), ("z", 4096, 1024), ("pool", 5120, 512), ("lru_g", 5632, 512), ("lru_x", 6144, 512),
             ("cq", 6912, 384), ("ckv", 7424, 256), ("xbc", 7680, 1536), ("kr", 9216, 128), ("dt", 9344, 128))
IN_OFFSETS = {name: (off, width) for name, off, width in IN_LAYOUT}
IN_ALL_COLS = 9728

ADAM_LR, ADAM_B1, ADAM_B2, ADAM_EPS, ADAM_WD, ADAM_STEP = 0.001, 0.9, 0.999, 1e-08, 0.01, 10

BIG = (("w_in", 2), ("w_uq", 2), ("w_ukv", 2), ("ssd_conv_w", 2), ("lru_conv_w", 2), ("w_branch", 3),
       ("w_out", 1), ("w_ff1", 2), ("w_ff2", 1), ("w_ple_gate", 1), ("w_ple", 2))
SMALL = ("g_mix", "q_norm", "kv_norm", "w_pool", "pool_scale", "ssd_conv_b", "ssd_dt_bias", "ssd_a_log",
         "ssd_d", "ssd_norm", "lru_conv_b", "lru_w_a", "lru_b_a", "lru_w_i", "lru_b_i", "lru_lambda",
         "g_mlp", "g_ple", "g_final")
WEIGHTS = ("g_mix", "w_in", "q_norm", "w_uq", "kv_norm", "w_ukv", "w_pool", "pool_scale", "ssd_conv_w",
           "ssd_conv_b", "ssd_dt_bias", "ssd_a_log", "ssd_d", "ssd_norm", "lru_conv_w", "lru_conv_b", "lru_w_a",
           "lru_b_a", "lru_w_i", "lru_b_i", "lru_lambda", "w_branch", "w_out", "g_mlp", "w_ff1", "w_ff2", "g_ple",
           "w_ple_gate", "w_ple", "g_final")
CONV_SHARDED = ("ssd_conv_w", "lru_conv_w")
PACK_W = 1024
PACK_ROWS = 64


def _cparams(sem, vmem=VMEM_LIMIT):
    return pltpu.CompilerParams(dimension_semantics=sem, vmem_limit_bytes=vmem)


def _pick(n, cands):
    for c in cands:
        if n % c == 0:
            return c
    return n


class _Cols:
    def __init__(self, arr, off, width):
        self.arr, self.off, self.width = arr, off, width

    shape = property(lambda self: (self.arr.shape[0], self.width))
    dtype = property(lambda self: self.arr.dtype)


def _arr(x):
    return x.arr if isinstance(x, _Cols) else x


def _off(x, unit):
    off = x.off if isinstance(x, _Cols) else 0
    assert off % unit == 0, (off, unit)
    return off // unit


def _sigmoid(x):
    return 1.0 / (1.0 + jnp.exp(-x))


def _silu(x):
    return x * _sigmoid(x)


def _silu_grad(x):
    s = _sigmoid(x)
    return s * (1.0 + x * (1.0 - s))


def _softplus(x):
    e = jnp.exp(-jnp.abs(x))
    log1p_e = jnp.where(e < 1e-3, e * (1.0 - e * (0.5 - e * (1.0 / 3.0))), jnp.log(1.0 + e))
    return jnp.maximum(x, 0.0) + log1p_e


_GELU_C = math.sqrt(2.0 / math.pi)


def _gelu(x):
    t = jnp.tanh(_GELU_C * (x + 0.044715 * x * x * x))
    return 0.5 * x * (1.0 + t)


def _gelu_grad(x):
    t = jnp.tanh(_GELU_C * (x + 0.044715 * x * x * x))
    return 0.5 * (1.0 + t) + 0.5 * x * (1.0 - t * t) * _GELU_C * (1.0 + 3.0 * 0.044715 * x * x)


def _neg_expm1(x):
    series = -x * (1.0 + 0.5 * x * (1.0 + (1.0 / 3.0) * x * (1.0 + 0.25 * x)))
    return jnp.where(x > -0.05, series, 1.0 - jnp.exp(x))


def _shift_down(x, k, row):
    return jnp.where(row >= k, pltpu.roll(x, k, 0), 0.0)


def _shift_up(x, k, row):
    n = x.shape[0]
    return jnp.where(row < n - k, pltpu.roll(x, n - k, 0), 0.0)


def _cumsum_rows(x, row):
    d = 1
    while d < x.shape[0]:
        x = x + _shift_down(x, d, row)
        d *= 2
    return x


def _rev_cumsum_rows(x, row):
    d = 1
    while d < x.shape[0]:
        x = x + _shift_up(x, d, row)
        d *= 2
    return x


def _cumsum_lanes(x, col):
    d = 1
    while d < x.shape[1]:
        x = x + jnp.where(col >= d, pltpu.roll(x, d, 1), 0.0)
        d *= 2
    return x


def _dot(a, b, ta=False, tb=False):
    dn = (((0 if ta else 1,), (1 if tb else 0,)), ((), ()))
    return lax.dot_general(a.astype(MXU_DTYPE), b.astype(MXU_DTYPE), dn, preferred_element_type=F32)


def _mm_tiles(m, n, k, a_bytes, b_bytes, mn_bytes):
    best = None
    for tm in (1024, 512, 384, 256, 128):
        for tn in (1024, 512, 384, 256, 128):
            for tk in (2048, 1024, 512, 384, 256, 128):
                if m % tm or n % tn or k % tk:
                    continue
                vmem = 2 * (tm * tk * a_bytes + tk * tn * b_bytes) + 2 * tm * tn * mn_bytes + 4 * tm * tn
                vmem += 2 * (tm * tk + tk * tn)
                if vmem > MM_VMEM_BUDGET:
                    continue
                steps = (m // tm) * (n // tn) * (k // tk)
                key = (steps, vmem)
                if best is None or key < best[0]:
                    best = (key, (tm, tn, tk))
    assert best is not None, (m, n, k)
    return best[1]


def _mm(a, b, *, ta=False, tb=False, epilogue=None, tiles=(), rowvecs=(), out_dtypes=(F32,), name):
    m, k = (a.shape[1], a.shape[0]) if ta else a.shape
    n = b.shape[0] if tb else b.shape[1]
    assert (b.shape[1] if tb else b.shape[0]) == k, (a.shape, b.shape, ta, tb)
    mn_bytes = sum(t.dtype.itemsize for t in tiles) + sum(jnp.dtype(dt).itemsize for dt in out_dtypes)
    tm, tn, tk = _mm_tiles(m, n, k, a.dtype.itemsize, b.dtype.itemsize, mn_bytes)
    nk = k // tk
    nt, nr, no = len(tiles), len(rowvecs), len(out_dtypes)

    def body(*refs):
        a_ref, b_ref = refs[0], refs[1]
        tile_refs = refs[2:2 + nt]
        row_refs = refs[2 + nt:2 + nt + nr]
        out_refs = refs[2 + nt + nr:2 + nt + nr + no]
        acc_ref = refs[-1]
        kk = pl.program_id(2)

        @pl.when(kk == 0)
        def _():
            acc_ref[...] = jnp.zeros_like(acc_ref)

        acc_ref[...] += _dot(a_ref[...], b_ref[...], ta, tb)

        @pl.when(kk == nk - 1)
        def _():
            acc = acc_ref[...]
            if epilogue is None:
                outs = (acc,)
            else:
                outs = epilogue(acc, *[t[...] for t in tile_refs], *[r[...] for r in row_refs])
            for o_ref, o in zip(out_refs, outs):
                o_ref[...] = o.astype(o_ref.dtype)

    a_spec = pl.BlockSpec((tk, tm), lambda i, j, kk: (kk, i)) if ta else pl.BlockSpec((tm, tk), lambda i, j, kk: (i, kk))
    b_spec = pl.BlockSpec((tn, tk), lambda i, j, kk: (j, kk)) if tb else pl.BlockSpec((tk, tn), lambda i, j, kk: (kk, j))
    mn_spec = pl.BlockSpec((tm, tn), lambda i, j, kk: (i, j))
    row_spec = pl.BlockSpec((1, tn), lambda i, j, kk: (0, j))
    tile_specs = [pl.BlockSpec((tm, tn), lambda i, j, kk, ob=_off(t, tn): (i, j + ob)) for t in tiles]
    outs = pl.pallas_call(
        body, name=name,
        grid=(m // tm, n // tn, nk),
        in_specs=[a_spec, b_spec] + tile_specs + [row_spec] * nr,
        out_specs=[mn_spec] * no,
        out_shape=[jax.ShapeDtypeStruct((m, n), dt) for dt in out_dtypes],
        scratch_shapes=[pltpu.VMEM((tm, tn), F32)],
        compiler_params=_cparams(("parallel", "parallel", "arbitrary")),
    )(a, b, *[_arr(t) for t in tiles], *rowvecs)
    return outs[0] if no == 1 else tuple(outs)


def _rowwise(fn, rows, fulls, outs, *, name, tm=None):
    r = rows[0].shape[0]
    if tm is None:
        widest = max([x.shape[1] for x in rows] + [o[0] for o in outs])
        tm = _pick(r, (max(8, min(512, (512 * 1024) // widest)), 256, 128, 64, 32, 16, 8))
    nrow, nfull, nout = len(rows), len(fulls), len(outs)

    def body(*refs):
        row_refs = refs[:nrow]
        full_refs = refs[nrow:nrow + nfull]
        out_refs = refs[nrow + nfull:]
        res = fn(*[x[...] for x in row_refs], *[x[...] for x in full_refs])
        if not isinstance(res, (tuple, list)):
            res = (res,)
        step = pl.program_id(0)
        for o_ref, o, spec in zip(out_refs, res, outs):
            if spec[2] == "row":
                o_ref[...] = o.astype(o_ref.dtype)
            else:
                @pl.when(step == 0)
                def _(o_ref=o_ref):
                    o_ref[...] = jnp.zeros_like(o_ref)
                o_ref[...] += o

    in_specs = [pl.BlockSpec((tm, x.shape[1]), lambda i, ob=_off(x, x.shape[1]): (i, ob)) for x in rows]
    in_specs += [pl.BlockSpec(x.shape, lambda i, nd=x.ndim: (0,) * nd) for x in fulls]
    out_specs, out_shape = [], []
    for c, dt, kind in outs:
        if kind == "row":
            out_specs.append(pl.BlockSpec((tm, c), lambda i: (i, 0)))
            out_shape.append(jax.ShapeDtypeStruct((r, c), dt))
        else:
            out_specs.append(pl.BlockSpec((1, c), lambda i: (0, 0)))
            out_shape.append(jax.ShapeDtypeStruct((1, c), F32))
    res = pl.pallas_call(
        body, name=name, grid=(r // tm,), in_specs=in_specs, out_specs=out_specs, out_shape=out_shape,
        compiler_params=_cparams(("arbitrary",)),
    )(*[_arr(x) for x in rows], *fulls)
    return res[0] if nout == 1 else tuple(res)


def _colsum(x):
    return jnp.sum(x, axis=0, keepdims=True)


def _rms_parts(x, n_real):
    r = lax.rsqrt(jnp.sum(x * x, axis=-1, keepdims=True) * (1.0 / n_real) + EPS)
    return x * r, r


def _rms_fwd(x, g, *, n_real=None, out_dtype=BF16, name):
    n_real = n_real or x.shape[1]

    def fn(xv, gv):
        xh, _ = _rms_parts(xv, n_real)
        return xh * gv

    return _rowwise(fn, [x], [g], [(x.shape[1], out_dtype, "row")], name=name)


def _rms_bwd_math(xv, gv, dh, n_real):
    xh, r = _rms_parts(xv, n_real)
    dxh = dh * gv
    dx = r * (dxh - xh * (jnp.sum(dxh * xh, axis=-1, keepdims=True) * (1.0 / n_real)))
    return dx, _colsum(dh * xh)


def _rms_bwd(x, g, dh, res=None, *, name):
    n = x.shape[1]
    if res is None:
        def fn(xv, dhv, gv):
            return _rms_bwd_math(xv, gv, dhv.astype(F32), n)
        rows = [x, dh]
    else:
        def fn(xv, dhv, rv, gv):
            dx, dg = _rms_bwd_math(xv, gv, dhv.astype(F32), n)
            return dx + rv, dg
        rows = [x, dh, res]
    return _rowwise(fn, rows, [g], [(n, F32, "row"), (n, F32, "acc")], name=name)


def _seq_call(body, ins, outs, n_blocks, *, name):
    in_specs, args = [], []
    for x, kind in ins:
        in_specs.append(pl.BlockSpec((x.shape[0], LANES), lambda j, ob=_off(x, LANES): (0, j + ob)))
        args.append(_arr(x))
    out_specs, out_shape = [], []
    for shape, dt in outs:
        out_specs.append(pl.BlockSpec((shape[0], LANES), lambda j: (0, j)))
        out_shape.append(jax.ShapeDtypeStruct(shape, dt))
    res = pl.pallas_call(body, name=name, grid=(n_blocks,), in_specs=in_specs, out_specs=out_specs,
                         out_shape=out_shape, compiler_params=_cparams(("parallel",)))(*args)
    return res[0] if len(outs) == 1 else tuple(res)


def _conv_pre(x, w, b, row):
    acc = x * w[CONV_W - 1:CONV_W, :] + b
    for k in range(CONV_W - 1):
        acc = acc + _shift_down(x, CONV_W - 1 - k, row) * w[k:k + 1, :]
    return acc


def _conv_fwd(x, w, b, *, silu, name):
    s, c = x.shape

    def body(x_ref, w_ref, b_ref, y_ref):
        xv = x_ref[...]
        row = lax.broadcasted_iota(jnp.int32, xv.shape, 0)
        pre = _conv_pre(xv, w_ref[...], b_ref[...], row)
        y_ref[...] = _silu(pre) if silu else pre

    return _seq_call(body, [(x, "seq"), (w, "par"), (b, "par")], [((s, c), F32)], c // LANES, name=name)


def _conv_bwd(x, w, b, dy, *, silu, name):
    s, c = x.shape

    def body(x_ref, w_ref, b_ref, dy_ref, dx_ref, dw_ref, db_ref):
        xv, wv, dv = x_ref[...], w_ref[...], dy_ref[...]
        row = lax.broadcasted_iota(jnp.int32, xv.shape, 0)
        if silu:
            dv = dv * _silu_grad(_conv_pre(xv, wv, b_ref[...], row))
        dx = dv * wv[CONV_W - 1:CONV_W, :]
        dws = [None] * CONV_W
        dws[CONV_W - 1] = _colsum(dv * xv)
        for k in range(CONV_W - 1):
            sh = CONV_W - 1 - k
            dx = dx + _shift_up(dv, sh, row) * wv[k:k + 1, :]
            dws[k] = _colsum(dv * _shift_down(xv, sh, row))
        dx_ref[...] = dx
        for k in range(CONV_W):
            dw_ref[k:k + 1, :] = dws[k]
        db_ref[...] = _colsum(dv)

    return _seq_call(body, [(x, "seq"), (w, "par"), (b, "par"), (dy, "seq")],
                     [((s, c), F32), ((CONV_W, c), F32), ((1, c), F32)], c // LANES, name=name)


def _pool_select(levels):
    g = pl.program_id(0)
    return jnp.where(g == 0, levels[0], jnp.where(g == 1, levels[1], jnp.where(g == 2, levels[2], levels[3])))


def _pool_count(row):
    g = pl.program_id(0)
    w = jnp.where(g == 0, POOL_WINDOWS[0], jnp.where(g == 1, POOL_WINDOWS[1],
                                                     jnp.where(g == 2, POOL_WINDOWS[2], POOL_WINDOWS[3])))
    return jnp.minimum(row + 1, w).astype(F32)


def _pool_fwd(u, *, name):
    def body(u_ref, d_ref):
        uv = u_ref[...]
        row = lax.broadcasted_iota(jnp.int32, uv.shape, 0)
        levels, cur, sh = [], uv, 1
        for _ in POOL_WINDOWS:
            cur = cur + _shift_down(cur, sh, row)
            levels.append(cur)
            sh *= 2
        d_ref[...] = _pool_select(levels) / _pool_count(row) - uv

    return _seq_call(body, [(u, "seq")], [(u.shape, F32)], u.shape[1] // LANES, name=name)


def _pool_bwd(dd, *, name):
    def body(dd_ref, du_ref):
        dv = dd_ref[...]
        row = lax.broadcasted_iota(jnp.int32, dv.shape, 0)
        levels, cur, sh = [], dv / _pool_count(row), 1
        for _ in POOL_WINDOWS:
            cur = cur + _shift_up(cur, sh, row)
            levels.append(cur)
            sh *= 2
        du_ref[...] = _pool_select(levels) - dv

    return _seq_call(body, [(dd, "seq")], [(dd.shape, F32)], dd.shape[1] // LANES, name=name)


def _lru_gates(pre_a, pre_i, xc, lam, b_a, b_i):
    r = _sigmoid(pre_a + b_a)
    i = _sigmoid(pre_i + b_i)
    sp = _softplus(-lam)
    log_a = -LRU_C * r * sp
    a = jnp.exp(log_a)
    mult = jnp.sqrt(_neg_expm1(2.0 * log_a))
    return r, i, sp, a, mult


def _lru_fwd(pre, xc, gate_in, lam, b_a, b_i, *, name):
    s, c = xc.shape
    nb = c // LANES

    def body(pa_ref, pi_ref, xc_ref, g_ref, lam_ref, ba_ref, bi_ref, y_ref, h_ref):
        xv = xc_ref[...]
        row = lax.broadcasted_iota(jnp.int32, xv.shape, 0)
        _, i, _, a, mult = _lru_gates(pa_ref[...], pi_ref[...], xv, lam_ref[...], ba_ref[...], bi_ref[...])
        h = xv * i * mult
        d = 1
        while d < s:
            h = h + a * _shift_down(h, d, row)
            a = a * jnp.where(row >= d, pltpu.roll(a, d, 0), 1.0)
            d *= 2
        h_ref[...] = h
        y_ref[...] = h * _gelu(g_ref[...])

    blk = lambda off: pl.BlockSpec((s, LANES), lambda j: (0, j + off))
    par = pl.BlockSpec((1, LANES), lambda j: (0, j))
    return pl.pallas_call(
        body, name=name, grid=(nb,),
        in_specs=[blk(0), blk(nb), blk(0), blk(_off(gate_in, LANES)), par, par, par],
        out_specs=[blk(0), blk(0)],
        out_shape=[jax.ShapeDtypeStruct((s, c), F32)] * 2,
        compiler_params=_cparams(("parallel",)),
    )(pre, pre, xc, _arr(gate_in), lam, b_a, b_i)


def _lru_bwd(pre, xc, gate_in, lam, b_a, b_i, h, dy, *, name):
    s, c = xc.shape
    nb = c // LANES

    def body(pa_ref, pi_ref, xc_ref, g_ref, lam_ref, ba_ref, bi_ref, h_ref, dy_ref,
             dpa_ref, dpi_ref, dxc_ref, dg_ref, dlam_ref, dba_ref, dbi_ref):
        xv, gv, hv, dv = xc_ref[...], g_ref[...], h_ref[...], dy_ref[...]
        row = lax.broadcasted_iota(jnp.int32, xv.shape, 0)
        r, i, sp, a, mult = _lru_gates(pa_ref[...], pi_ref[...], xv, lam_ref[...], ba_ref[...], bi_ref[...])
        dg_ref[...] = dv * hv * _gelu_grad(gv)
        dh = dv * _gelu(gv)
        an = jnp.where(row < s - 1, pltpu.roll(a, s - 1, 0), 0.0)
        d = 1
        while d < s:
            dh = dh + an * _shift_up(dh, d, row)
            an = an * jnp.where(row < s - d, pltpu.roll(an, s - d, 0), 1.0)
            d *= 2
        da = dh * _shift_down(hv, 1, row)
        dxc_ref[...] = dh * i * mult
        di = dh * xv * mult
        dmult = dh * xv * i
        dlog_a = (da - dmult * a / mult) * a
        dr = dlog_a * (-LRU_C) * sp
        dlam_ref[...] = _colsum(dlog_a * LRU_C * r * _sigmoid(-lam_ref[...]))
        dpa = dr * r * (1.0 - r)
        dpi = di * i * (1.0 - i)
        dpa_ref[...] = dpa
        dpi_ref[...] = dpi
        dba_ref[...] = _colsum(dpa)
        dbi_ref[...] = _colsum(dpi)

    blk = lambda off: pl.BlockSpec((s, LANES), lambda j: (0, j + off))
    par = pl.BlockSpec((1, LANES), lambda j: (0, j))
    sc = jax.ShapeDtypeStruct((s, c), F32)
    pc = jax.ShapeDtypeStruct((1, c), F32)
    dpa, dpi, dxc, dg, dlam, dba, dbi = pl.pallas_call(
        body, name=name, grid=(nb,),
        in_specs=[blk(0), blk(nb), blk(0), blk(_off(gate_in, LANES)), par, par, par, blk(0), blk(0)],
        out_specs=[blk(0), blk(0), blk(0), blk(0), par, par, par],
        out_shape=[sc, sc, sc, sc, pc, pc, pc],
        compiler_params=_cparams(("parallel",)),
    )(pre, pre, xc, _arr(gate_in), lam, b_a, b_i, h, dy)
    return dpa, dpi, dxc, dg, dlam, dba, dbi


GROUP_HEADS = 4


def _ssd_specs(nc, order):
    gw = GROUP_HEADS * LANES
    return dict(
        x=pl.BlockSpec((SSD_CHUNK, gw), lambda g, ci: (order(ci), g)),
        b=pl.BlockSpec((SSD_CHUNK, LANES), lambda g, ci: (order(ci), N_HEADS + g)),
        c=pl.BlockSpec((SSD_CHUNK, LANES), lambda g, ci: (order(ci), N_HEADS + 2 + g)),
        dtcol=pl.BlockSpec((GROUP_HEADS, SSD_CHUNK, 1), lambda g, ci: (g, order(ci), 0)),
        dtrow=pl.BlockSpec((GROUP_HEADS, 1, SSD_CHUNK), lambda g, ci: (g, 0, order(ci))),
        scal=pl.BlockSpec((GROUP_HEADS, 1, 1), lambda g, ci: (g, 0, 0)),
        state=pl.BlockSpec((GROUP_HEADS, 1, LANES, LANES), lambda g, ci: (g, order(ci), 0, 0)),
        group=pl.BlockSpec((SSD_CHUNK, LANES), lambda g, ci: (order(ci), g)),
        pacc=pl.BlockSpec((GROUP_HEADS, 1, LANES), lambda g, ci: (g, 0, 0)),
    )


def _ssd_chunk_terms(dtcol, dtrow, bias, a_log):
    shp = (SSD_CHUNK, SSD_CHUNK)
    row = lax.broadcasted_iota(jnp.int32, shp, 0)
    col = lax.broadcasted_iota(jnp.int32, shp, 1)
    a_head = -jnp.exp(a_log)
    dt_c = jnp.broadcast_to(_softplus(dtcol + bias), shp)
    dt_r = jnp.broadcast_to(_softplus(dtrow + bias), shp)
    cs_c = _cumsum_rows(dt_c * a_head, row)
    cs_r = _cumsum_lanes(dt_r * a_head, col)
    cs_last = jnp.sum(jnp.where(row == SSD_CHUNK - 1, cs_c, 0.0), axis=0, keepdims=True)
    return row, col, a_head, dt_c, cs_c, cs_r, cs_last


def _ssd_fwd(xbc, dtcol, dtrow, bias, a_log, dskip, *, name):
    s = xbc.shape[0]
    nc = s // SSD_CHUNK

    def body(x_ref, b_ref, c_ref, dtc_ref, dtr_ref, bias_ref, alog_ref, d_ref, y_ref, st_ref, state):
        ci = pl.program_id(1)

        @pl.when(ci == 0)
        def _():
            state[...] = jnp.zeros_like(state)

        bm, cm = b_ref[...], c_ref[...]
        cb = _dot(cm, bm, tb=True)
        bm_t = bm.T
        for r in range(GROUP_HEADS):
            lanes = slice(r * LANES, (r + 1) * LANES)
            xv = x_ref[:, lanes]
            row, col, _, dt_c, cs_c, cs_r, cs_last = _ssd_chunk_terms(dtc_ref[r], dtr_ref[r], bias_ref[r], alog_ref[r])
            g = cb * jnp.exp(jnp.where(col <= row, cs_c - cs_r, -jnp.inf))
            xdt = xv * dt_c
            st = state[r]
            st_ref[r, 0] = st
            y_ref[:, lanes] = _dot(g, xdt) + _dot(cm, st) * jnp.exp(cs_c) + xv * d_ref[r]
            state[r] = jnp.exp(cs_last) * st + _dot(bm_t, xdt * jnp.exp(cs_last - cs_c))

    sp = _ssd_specs(nc, lambda ci: ci)
    return pl.pallas_call(
        body, name=name, grid=(N_HEADS // GROUP_HEADS, nc),
        in_specs=[sp["x"], sp["b"], sp["c"], sp["dtcol"], sp["dtrow"], sp["scal"], sp["scal"], sp["scal"]],
        out_specs=[sp["x"], sp["state"]],
        out_shape=[jax.ShapeDtypeStruct((s, N_HEADS * LANES), F32),
                   jax.ShapeDtypeStruct((N_HEADS, nc, LANES, LANES), F32)],
        scratch_shapes=[pltpu.VMEM((GROUP_HEADS, LANES, LANES), F32)],
        compiler_params=_cparams(("parallel", "arbitrary")),
    )(xbc, xbc, xbc, dtcol, dtrow, bias, a_log, dskip)


def _ssd_bwd(xbc, dtcol, dtrow, bias, a_log, dskip, states, dy, *, name):
    s = xbc.shape[0]
    nc = s // SSD_CHUNK

    def body(x_ref, b_ref, c_ref, dtc_ref, dtr_ref, bias_ref, alog_ref, d_ref, st_ref, dy_ref,
             dx_ref, db_ref, dc_ref, ddt_ref, dbias_ref, dalog_ref, dd_ref, dstate):
        ci = pl.program_id(1)

        @pl.when(ci == 0)
        def _():
            dstate[...] = jnp.zeros_like(dstate)
            dbias_ref[...] = jnp.zeros_like(dbias_ref)
            dalog_ref[...] = jnp.zeros_like(dalog_ref)
            dd_ref[...] = jnp.zeros_like(dd_ref)

        bm, cm = b_ref[...], c_ref[...]
        cb = _dot(cm, bm, tb=True)
        cb_t = _dot(bm, cm, tb=True)
        cm_t = cm.T
        rowsum = lambda v: jnp.sum(v, axis=1, keepdims=True)
        tot = lambda v: jnp.broadcast_to(jnp.sum(v, axis=0, keepdims=True), (1, LANES))
        dbm_sum, dcm_sum = None, None
        for r in range(GROUP_HEADS):
            lanes = slice(r * LANES, (r + 1) * LANES)
            xv, dyv, st = x_ref[:, lanes], dy_ref[:, lanes], st_ref[r, 0]
            dtraw_c, bias = dtc_ref[r], bias_ref[r]
            row, col, a_head, dt_c, cs_c, cs_r, cs_last = _ssd_chunk_terms(dtraw_c, dtr_ref[r], bias, alog_ref[r])
            lmat = jnp.exp(jnp.where(col <= row, cs_c - cs_r, -jnp.inf))
            lmat_t = jnp.exp(jnp.where(row <= col, cs_r - cs_c, -jnp.inf))
            g, g_t = cb * lmat, cb_t * lmat_t
            xdt = xv * dt_c
            e_c = jnp.exp(cs_c)
            f_c = jnp.exp(cs_last - cs_c)
            e_last = jnp.exp(cs_last)
            w = xdt * f_c
            dst = dstate[r]

            dg = _dot(dyv, xdt, tb=True)
            dg_t = _dot(xdt, dyv, tb=True)
            dxdt = _dot(g_t, dyv)
            dcs = rowsum(dg * g) - rowsum(dg_t * g_t)
            dcm = _dot(dg * lmat, bm)
            dbm = _dot(dg_t * lmat_t, cm)
            z = _dot(cm, st)
            dz = dyv * e_c
            dcs = dcs + rowsum(dz * z)
            dcm = dcm + _dot(dz, st, tb=True)
            dstate[r] = _dot(cm_t, dz) + e_last * dst
            dcs_last = jnp.sum(rowsum(dst * st), axis=0, keepdims=True) * jnp.max(e_last, axis=1, keepdims=True)
            dbm = dbm + _dot(w, dst, tb=True)
            dw = _dot(bm, dst)
            dxdt = dxdt + dw * f_c
            q = rowsum(dw * w)
            dcs = dcs - q
            dcs_last = dcs_last + jnp.sum(q, axis=0, keepdims=True)
            dx_ref[:, lanes] = dxdt * dt_c + dyv * d_ref[r]
            ddt = rowsum(dxdt * xv)
            dcs_full = jnp.broadcast_to(dcs, (SSD_CHUNK, SSD_CHUNK)) + jnp.where(row == SSD_CHUNK - 1, dcs_last, 0.0)
            da = jnp.max(_rev_cumsum_rows(dcs_full, row), axis=1, keepdims=True)
            dt_col = jnp.max(dt_c, axis=1, keepdims=True)
            draw = (ddt + da * a_head) * _sigmoid(dtraw_c + bias)
            ddt_ref[r] = draw
            dbias_ref[r] += tot(draw)
            dalog_ref[r] += tot(da * dt_col) * a_head
            dd_ref[r] += tot(rowsum(dyv * xv))
            dbm_sum = dbm if dbm_sum is None else dbm_sum + dbm
            dcm_sum = dcm if dcm_sum is None else dcm_sum + dcm
        db_ref[...] = dbm_sum
        dc_ref[...] = dcm_sum

    sp = _ssd_specs(nc, lambda ci: nc - 1 - ci)
    return pl.pallas_call(
        body, name=name, grid=(N_HEADS // GROUP_HEADS, nc),
        in_specs=[sp["x"], sp["b"], sp["c"], sp["dtcol"], sp["dtrow"], sp["scal"], sp["scal"], sp["scal"],
                  sp["state"], sp["x"]],
        out_specs=[sp["x"], sp["group"], sp["group"], sp["dtcol"], sp["pacc"], sp["pacc"], sp["pacc"]],
        out_shape=[jax.ShapeDtypeStruct((s, N_HEADS * LANES), F32),
                   jax.ShapeDtypeStruct((s, 2 * LANES), F32),
                   jax.ShapeDtypeStruct((s, 2 * LANES), F32),
                   jax.ShapeDtypeStruct((N_HEADS, s, 1), F32),
                   jax.ShapeDtypeStruct((N_HEADS, 1, LANES), F32),
                   jax.ShapeDtypeStruct((N_HEADS, 1, LANES), F32),
                   jax.ShapeDtypeStruct((N_HEADS, 1, LANES), F32)],
        scratch_shapes=[pltpu.VMEM((GROUP_HEADS, LANES, LANES), F32)],
        compiler_params=_cparams(("parallel", "arbitrary")),
    )(xbc, xbc, xbc, dtcol, dtrow, bias, a_log, dskip, states, dy)


def _att_tile(s):
    return _pick(s, (512, 256, 128))


def _tri(t, transposed=False):
    r = lax.broadcasted_iota(jnp.int32, (t, t), 0)
    c = lax.broadcasted_iota(jnp.int32, (t, t), 1)
    return (r <= c) if transposed else (c <= r)


def _rows_at(ref, blk, t):
    return ref[pl.ds(pl.multiple_of(blk * t, t), t), :]


def _flash_fwd(q, k, v, *, name):
    s = q.shape[0]
    t = _att_tile(s)
    nq = s // t

    def body(q_ref, k_ref, v_ref, o_ref, lse_ref):
        i = pl.program_id(1)
        qv = q_ref[...]

        def step(j, carry, diagonal):
            m_old, l_old, acc = carry
            sc = _dot(qv, _rows_at(k_ref, j, t), tb=True)
            if diagonal:
                sc = jnp.where(_tri(t), sc, -jnp.inf)
            m_new = jnp.maximum(m_old, jnp.max(sc, axis=1, keepdims=True))
            alpha = jnp.exp(m_old - m_new)
            p = jnp.exp(sc - m_new)
            return (m_new, alpha * l_old + jnp.sum(p, axis=1, keepdims=True),
                    alpha * acc + _dot(p, _rows_at(v_ref, j, t)))

        init = (jnp.full((t, 1), -jnp.inf, F32), jnp.zeros((t, 1), F32), jnp.zeros((t, LANES), F32))
        carry = lax.fori_loop(0, i, lambda j, c: step(j, c, False), init)
        m_fin, l_fin, acc = step(i, carry, True)
        o_ref[...] = (acc / l_fin).astype(o_ref.dtype)
        lse_ref[0] = m_fin + jnp.log(l_fin)

    q_spec = pl.BlockSpec((t, LANES), lambda h, i: (i, h))
    kv_spec = pl.BlockSpec((s, LANES), lambda h, i: (0, h))
    return pl.pallas_call(
        body, name=name, grid=(N_HEADS, nq),
        in_specs=[q_spec, kv_spec, kv_spec],
        out_specs=[q_spec, pl.BlockSpec((1, t, 1), lambda h, i: (h, i, 0))],
        out_shape=[jax.ShapeDtypeStruct(q.shape, BF16), jax.ShapeDtypeStruct((N_HEADS, s, 1), F32)],
        compiler_params=_cparams(("parallel", "arbitrary")),
    )(q, k, v)


def _flash_bwd_dq(q, k, v, o, do, lse, *, name):
    s = q.shape[0]
    t = _att_tile(s)
    nq = s // t

    def body(q_ref, k_ref, v_ref, o_ref, do_ref, lse_ref, dq_ref, dl_ref):
        i = pl.program_id(1)
        qv, dov, lse = q_ref[...], do_ref[...], lse_ref[0]
        delta = jnp.sum(dov.astype(F32) * o_ref[...].astype(F32), axis=1, keepdims=True)
        dl_ref[0] = delta

        def step(j, acc, diagonal):
            kj = _rows_at(k_ref, j, t)
            p = jnp.exp(_dot(qv, kj, tb=True) - lse)
            if diagonal:
                p = jnp.where(_tri(t), p, 0.0)
            ds = p * (_dot(dov, _rows_at(v_ref, j, t), tb=True) - delta)
            return acc + _dot(ds, kj)

        acc = lax.fori_loop(0, i, lambda j, c: step(j, c, False), jnp.zeros((t, LANES), F32))
        dq_ref[...] = step(i, acc, True) * ATT_SCALE

    q_spec = pl.BlockSpec((t, LANES), lambda h, i: (i, h))
    kv_spec = pl.BlockSpec((s, LANES), lambda h, i: (0, h))
    col_spec = pl.BlockSpec((1, t, 1), lambda h, i: (h, i, 0))
    return pl.pallas_call(
        body, name=name, grid=(N_HEADS, nq),
        in_specs=[q_spec, kv_spec, kv_spec, q_spec, q_spec, col_spec],
        out_specs=[q_spec, col_spec],
        out_shape=[jax.ShapeDtypeStruct(q.shape, F32), jax.ShapeDtypeStruct((N_HEADS, s, 1), F32)],
        compiler_params=_cparams(("parallel", "arbitrary")),
    )(q, k, v, o, do, lse)


def _flash_bwd_dkv(q, k, v, do, lse_row, delta_row, *, name):
    s = q.shape[0]
    t = _att_tile(s)
    nq = s // t

    def body(q_ref, k_ref, v_ref, do_ref, lse_ref, dl_ref, dk_ref, dv_ref):
        j = pl.program_id(1)
        kv, vv = k_ref[...], v_ref[...]

        def step(i, carry, diagonal):
            dk, dv = carry
            qi, doi = _rows_at(q_ref, i, t), _rows_at(do_ref, i, t)
            cols = pl.ds(pl.multiple_of(i * t, t), t)
            p_t = jnp.exp(_dot(kv, qi, tb=True) - lse_ref[0, :, cols])
            if diagonal:
                p_t = jnp.where(_tri(t, transposed=True), p_t, 0.0)
            ds_t = p_t * (_dot(vv, doi, tb=True) - dl_ref[0, :, cols])
            return dk + _dot(ds_t, qi), dv + _dot(p_t, doi)

        zero = jnp.zeros((t, LANES), F32)
        carry = step(j, (zero, zero), True)
        dk, dv = lax.fori_loop(j + 1, nq, lambda i, c: step(i, c, False), carry)
        dk_ref[...] = dk
        dv_ref[...] = dv

    q_spec = pl.BlockSpec((s, LANES), lambda h, j: (0, h))
    kv_spec = pl.BlockSpec((t, LANES), lambda h, j: (j, h))
    row_spec = pl.BlockSpec((1, 1, s), lambda h, j: (h, 0, 0))
    return pl.pallas_call(
        body, name=name, grid=(N_HEADS, nq),
        in_specs=[q_spec, kv_spec, kv_spec, q_spec, row_spec, row_spec],
        out_specs=[kv_spec, kv_spec],
        out_shape=[jax.ShapeDtypeStruct(q.shape, F32)] * 2,
        compiler_params=_cparams(("parallel", "arbitrary")),
    )(q, k, v, do, lse_row, delta_row)


def _rope(v, cos_t, sin_p, sin_m):
    return v * cos_t + pltpu.roll(v, QK_ROPE // 2, 1) * sin_p + pltpu.roll(v, LANES - QK_ROPE // 2, 1) * sin_m


def _rope_t(d, cos_t, sin_p, sin_m):
    return d * cos_t + pltpu.roll(d * sin_p, LANES - QK_ROPE // 2, 1) + pltpu.roll(d * sin_m, QK_ROPE // 2, 1)


def _att_prep(q_pad, kv2, kr, cos_t, sin_p, sin_m, *, name):
    w = N_HEADS * LANES

    def fn(qv, kvv, krv, c, sp, sm):
        kr_rot = _rope(krv, c, sp, sm)
        qs, ks = [], []
        for h in range(N_HEADS):
            blk = slice(h * LANES, (h + 1) * LANES)
            qs.append(_rope(qv[:, blk], c, sp, sm) * ATT_SCALE)
            ks.append(kvv[:, blk] + kr_rot)
        return jnp.concatenate(qs, axis=1), jnp.concatenate(ks, axis=1), kvv[:, w:]

    return _rowwise(fn, [q_pad, kv2, kr, cos_t, sin_p, sin_m], [],
                    [(w, BF16, "row"), (w, BF16, "row"), (w, BF16, "row")], name=name)


def _att_prep_bwd(dq, dk, cos_t, sin_p, sin_m, *, name):
    w = N_HEADS * LANES

    def fn(dqv, dkv, c, sp, sm):
        outs, dkr = [], None
        for h in range(N_HEADS):
            blk = slice(h * LANES, (h + 1) * LANES)
            outs.append(_rope_t(dqv[:, blk], c, sp, sm))
            dkr = dkv[:, blk] if dkr is None else dkr + dkv[:, blk]
        return jnp.concatenate(outs, axis=1), _rope_t(dkr, c, sp, sm)

    return _rowwise(fn, [dq, dk, cos_t, sin_p, sin_m], [], [(w, BF16, "row"), (LANES, F32, "row")], name=name)


_ANY = pl.BlockSpec(memory_space=pl.ANY)
_MESH = pl.DeviceIdType.MESH


def _mesh_pos():
    return lax.axis_index("x"), lax.axis_index("y"), lax.axis_index("c")


def _remote(src, dst, send_sem, recv_sem, dev):
    return pltpu.make_async_remote_copy(src_ref=src, dst_ref=dst, send_sem=send_sem, recv_sem=recv_sem,
                                        device_id=dev, device_id_type=_MESH)


def _other_chips(x, y):
    chips = [(1 - x, y), (x, 1 - y), (1 - x, 1 - y)]
    return chips, [2 * cx + cy for cx, cy in chips]


def _comm_call(body, ins, out_shapes, n_sems, *, name):
    return pl.pallas_call(
        body, name=name, in_specs=[_ANY] * len(ins), out_specs=[_ANY] * len(out_shapes), out_shape=out_shapes,
        scratch_shapes=[pltpu.SemaphoreType.DMA((k,)) for k in n_sems],
    )(*ins)


def _gather_halves(shards):
    n = len(shards)
    halves = [t.shape[0] // 2 for t in shards]

    def body(*refs):
        xs, outs = refs[:n], refs[n:2 * n]
        send_sems, recv_sems, local_sems = refs[2 * n:]
        x, y, c = _mesh_pos()
        k = 2 * x + y
        sibling = (x, y, 1 - c)
        chips, ks = _other_chips(x, y)
        half = lambda w, hf: pl.ds(hf * halves[w], halves[w])
        local = [pltpu.make_async_copy(xs[w], outs[w].at[k], local_sems.at[w]) for w in range(n)]
        for cp in local:
            cp.start()
        first = [_remote(xs[w].at[half(w, c)], outs[w].at[k, half(w, c)], send_sems.at[6 * w + j], recv_sems.at[6 * w + j],
                         (*chips[j], c)) for w in range(n) for j in range(3)]
        for cp in first:
            cp.start()
        passed = []
        for j in range(3):
            for w in range(n):
                land = outs[w].at[ks[j], half(w, c)]
                _remote(land, land, send_sems.at[6 * w + j], recv_sems.at[6 * w + j], sibling).wait_recv()
                passed.append(_remote(land, land, send_sems.at[6 * w + 3 + j], recv_sems.at[6 * w + 3 + j], sibling))
                passed[-1].start()
        for j in range(3):
            for w in range(n):
                land = outs[w].at[ks[j], half(w, 1 - c)]
                _remote(land, land, send_sems.at[6 * w + 3 + j], recv_sems.at[6 * w + 3 + j], sibling).wait_recv()
        for cp in first + passed:
            cp.wait_send()
        for cp in local:
            cp.wait()

    shapes = [jax.ShapeDtypeStruct((4,) + t.shape, t.dtype) for t in shards]
    return _comm_call(body, shards, shapes, (6 * n, 6 * n, n), name="gather_halves")


_HBM = pl.BlockSpec(memory_space=pltpu.HBM)
_SEM = pl.BlockSpec(memory_space=pltpu.SEMAPHORE)
_EFFECT = pltpu.SideEffectType.DATAFLOW_SIDE_EFFECTING


def _push_start(blocks, *, scatter, name):
    n = len(blocks)

    def body(*refs):
        xs, lands = refs[:n], refs[n:2 * n]
        send_sems, recv_sems = refs[2 * n], refs[2 * n + 1]
        token = refs[-1]
        x, y, c = _mesh_pos()
        k = 2 * x + y
        chips, ks = _other_chips(x, y)
        for w in range(n):
            for j in range(3):
                src = xs[w].at[ks[j]] if scatter else xs[w]
                _remote(src, lands[w].at[k], send_sems.at[3 * w + j], recv_sems.at[3 * w + j], (*chips[j], c)).start()
        token[...] = jnp.zeros_like(token)

    hbm = lambda shape, dtype: pltpu.with_memory_space_constraint(lax.empty(shape, dtype), pltpu.HBM)
    ins = [pltpu.with_memory_space_constraint(t, pltpu.HBM) for t in blocks]
    ins += [hbm(t.shape if scatter else (4,) + t.shape, t.dtype) for t in blocks]
    out_shape = [pltpu.SemaphoreType.DMA((3 * n,)), pltpu.SemaphoreType.DMA((3 * n,))]
    out_shape += [pltpu.HBM(t.shape, t.dtype) for t in ins]
    out_shape += [jax.ShapeDtypeStruct((8, LANES), F32)]
    res = pl.pallas_call(
        body, name=name, out_shape=out_shape, in_specs=[_HBM] * (2 * n),
        out_specs=[_SEM, _SEM] + [_HBM] * (2 * n) + [pl.BlockSpec(memory_space=pltpu.VMEM)],
        input_output_aliases={i: 2 + i for i in range(2 * n)},
        compiler_params=pltpu.CompilerParams(has_side_effects=_EFFECT),
    )(*ins)
    return res[0], res[1], res[2:2 + n], res[2 + n:2 + 2 * n], res[-1]


def _push_wait(send_sems, recv_sems, blocks, lands, after, *, name):
    n = len(blocks)

    def body(*refs):
        lands_in = refs[n:2 * n]
        send_sems, recv_sems = refs[2 * n], refs[2 * n + 1]
        x, y, c = _mesh_pos()
        chips, ks = _other_chips(x, y)
        for w in range(n):
            for j in range(3):
                slot = lands_in[w].at[ks[j]]
                cp = _remote(slot, slot, send_sems.at[3 * w + j], recv_sems.at[3 * w + j], (*chips[j], c))
                cp.wait_send()
                cp.wait_recv()

    out_shape = [pltpu.HBM(t.shape, t.dtype) for t in list(blocks) + list(lands)]
    res = pl.pallas_call(
        body, name=name, out_shape=out_shape,
        in_specs=[_HBM] * (2 * n) + [_SEM, _SEM, pl.BlockSpec(memory_space=pl.ANY)], out_specs=[_HBM] * (2 * n),
        input_output_aliases={i: i for i in range(2 * n)},
        compiler_params=pltpu.CompilerParams(has_side_effects=_EFFECT),
    )(*blocks, *lands, send_sems, recv_sems, after)
    return res[:n], res[n:]


def _send_half(views, *, name):
    n = len(views)

    def body(*refs):
        vs, outs = refs[:n], refs[n:2 * n]
        send_sems, recv_sems = refs[2 * n:]
        x, y, c = _mesh_pos()
        cps = []
        for w in range(n):
            h = views[w].shape[1] // 2
            cps.append(_remote(vs[w].at[:, pl.ds((1 - c) * h, h), :], outs[w], send_sems.at[w], recv_sems.at[w],
                               (x, y, 1 - c)))
            cps[-1].start()
        for cp in cps:
            cp.wait()

    shapes = [jax.ShapeDtypeStruct((t.shape[0], t.shape[1] // 2, t.shape[2]), t.dtype) for t in views]
    return _comm_call(body, views, shapes, (n, n), name=name)


def _join_halves(mine, *, name):
    n = len(mine)

    def body(*refs):
        hs, outs = refs[:n], refs[n:2 * n]
        send_sems, recv_sems, local_sems = refs[2 * n:]
        x, y, c = _mesh_pos()
        sibling = (x, y, 1 - c)
        local = [pltpu.make_async_copy(hs[w], outs[w].at[c], local_sems.at[w]) for w in range(n)]
        cps = [_remote(hs[w], outs[w].at[c], send_sems.at[w], recv_sems.at[w], sibling) for w in range(n)]
        for cp in local + cps:
            cp.start()
        for w in range(n):
            other = outs[w].at[1 - c]
            _remote(other, other, send_sems.at[w], recv_sems.at[w], sibling).wait_recv()
        for cp in cps:
            cp.wait_send()
        for cp in local:
            cp.wait()

    shapes = [jax.ShapeDtypeStruct((2,) + t.shape, t.dtype) for t in mine]
    return _comm_call(body, mine, shapes, (n, n, n), name=name)


def _gather_all(vec, *, name):
    r, w = vec.shape

    def body(v_ref, out_ref, send_sems, recv_sems, local_sem):
        x, y, c = _mesh_pos()

        def slot(px, py, pc):
            return out_ref.at[4 * px + 2 * py + pc]

        mine = pltpu.make_async_copy(v_ref, slot(x, y, c), local_sem)
        mine.start()
        peers = []
        for rel in range(1, 8):
            fx, fy, fc = (rel >> 2) & 1, (rel >> 1) & 1, rel & 1
            peers.append((x ^ fx, y ^ fy, c ^ fc))
        cps = [_remote(v_ref, slot(x, y, c), send_sems.at[j], recv_sems.at[j], peer) for j, peer in enumerate(peers)]
        for cp in cps:
            cp.start()
        for j, peer in enumerate(peers):
            _remote(slot(*peer), slot(*peer), send_sems.at[j], recv_sems.at[j], peer).wait_recv()
        for cp in cps:
            cp.wait_send()
        mine.wait()

    return pl.pallas_call(
        body, name=name, in_specs=[_ANY], out_specs=_ANY,
        out_shape=jax.ShapeDtypeStruct((8, r, w), vec.dtype),
        scratch_shapes=[pltpu.SemaphoreType.DMA((7,)), pltpu.SemaphoreType.DMA((7,)), pltpu.SemaphoreType.DMA],
    )(vec)


def _row_tile(rows, row_bytes):
    for tm in (1024, 512, 256, 128, 64, 32, 16):
        if rows % tm == 0 and tm * row_bytes <= ELEMENTWISE_BLOCK_BYTES:
            return tm
    return 16 if rows % 16 == 0 else rows


def _chip_sum_half(g, got, c, *, name):
    nb, r, w = g.shape
    half = r // 2
    tm = _row_tile(half, w * 4)
    per = half // tm

    def body(c_ref, g_ref, o_ref, out_ref):
        out_ref[...] = (g_ref[...] + o_ref[...]).astype(out_ref.dtype)

    return pl.pallas_call(
        body, name=name,
        grid_spec=pltpu.PrefetchScalarGridSpec(
            num_scalar_prefetch=1, grid=(nb, per),
            in_specs=[pl.BlockSpec((1, tm, w), lambda b, i, c_ref: (b, c_ref[0] * per + i, 0)),
                      pl.BlockSpec((1, tm, w), lambda b, i, c_ref: (b, i, 0))],
            out_specs=pl.BlockSpec((1, tm, w), lambda b, i, c_ref: (b, i, 0))),
        out_shape=jax.ShapeDtypeStruct((nb, half, w), BF16),
        compiler_params=_cparams(("parallel", "parallel")),
    )(jnp.reshape(c, (1,)).astype(jnp.int32), g, got)


def _sum_slots(stack, *, name):
    n, r, w = stack.shape
    tm = _row_tile(r, n * w * stack.dtype.itemsize)

    def body(s_ref, out_ref):
        acc = s_ref[0].astype(F32)
        for i in range(1, n):
            acc = acc + s_ref[i].astype(F32)
        out_ref[...] = acc

    return pl.pallas_call(
        body, name=name, grid=(r // tm,),
        in_specs=[pl.BlockSpec((n, tm, w), lambda i: (0, i, 0))],
        out_specs=pl.BlockSpec((tm, w), lambda i: (i, 0)),
        out_shape=jax.ShapeDtypeStruct((r, w), F32),
        compiler_params=_cparams(("parallel",)),
    )(stack)


def _adam_math(wv, gv, mv, vv):
    m_new = ADAM_B1 * mv + (1.0 - ADAM_B1) * gv
    v_new = ADAM_B2 * vv + (1.0 - ADAM_B2) * (gv * gv)
    m_hat = m_new / (1.0 - ADAM_B1 ** ADAM_STEP)
    v_hat = v_new / (1.0 - ADAM_B2 ** ADAM_STEP)
    delta = -ADAM_LR * (m_hat / (jnp.sqrt(v_hat) + ADAM_EPS) + ADAM_WD * wv)
    return delta, m_new, v_new


def _adamw(w, g, m, v, *, name):
    shape = w.shape
    cols = shape[-1]
    flat = lambda t: t.reshape(-1, cols)
    rows = flat(w).shape[0]
    tm = _pick(rows, (256, 128, 64, 32, 16, 8))
    outs = _rowwise(_adam_math, [flat(w), flat(g), flat(m), flat(v)], [], [(cols, F32, "row")] * 3, name=name, tm=tm)
    return tuple(o.reshape(shape) for o in outs)


def _adamw_slots(w, slots0, slots1, m, v, *, name):
    shape = w.shape
    cols = shape[-1]
    half = slots0.shape[2]
    v4 = lambda t: t.reshape(2, 2, half, cols)
    assert slots0.shape == slots1.shape == (2, 4, half, cols) and w.size == 4 * half * cols, (slots0.shape, shape)
    tm = _row_tile(half, cols * 4)

    def body(w_ref, s0_ref, s1_ref, m_ref, v_ref, g_ref, d_ref, mo_ref, vo_ref):
        first = pl.program_id(0) == 0
        g = None
        for i in range(4):
            part = jnp.where(first, s0_ref[0, i], s1_ref[0, i]).astype(F32)
            g = part if g is None else g + part
        delta, m_new, v_new = _adam_math(w_ref[0, 0], g, m_ref[0, 0], v_ref[0, 0])
        g_ref[0, 0], d_ref[0, 0], mo_ref[0, 0], vo_ref[0, 0] = g, delta, m_new, v_new

    blk = pl.BlockSpec((1, 1, tm, cols), lambda l, hf, i: (l, hf, i, 0))
    s0 = pl.BlockSpec((1, 4, tm, cols), lambda l, hf, i: (hf * (1 - l), 0, i * (1 - l), 0))
    s1 = pl.BlockSpec((1, 4, tm, cols), lambda l, hf, i: (hf * l, 0, i * l, 0))
    outs = pl.pallas_call(
        body, name=name, grid=(2, 2, half // tm),
        in_specs=[blk, s0, s1, blk, blk],
        out_specs=[blk] * 4, out_shape=[jax.ShapeDtypeStruct((2, 2, half, cols), F32)] * 4,
        compiler_params=_cparams(("arbitrary", "arbitrary", "arbitrary")),
    )(v4(w), slots0, slots1, v4(m), v4(v))
    return tuple(o.reshape(shape) for o in outs)


def _pad_blocks(w, axis, n_blocks, real, to=LANES, offset=0):
    axis = axis % w.ndim
    shp = w.shape
    w = w.reshape(shp[:axis] + (n_blocks, real) + shp[axis + 1:])
    pads = [(0, 0)] * w.ndim
    pads[axis + 1] = (offset, to - real - offset)
    w = jnp.pad(w, pads)
    return w.reshape(shp[:axis] + (n_blocks * to,) + shp[axis + 1:])


def _unpad_blocks(w, axis, n_blocks, real, to=LANES, offset=0):
    axis = axis % w.ndim
    shp = w.shape
    w = w.reshape(shp[:axis] + (n_blocks, to) + shp[axis + 1:])
    w = lax.slice_in_dim(w, offset, offset + real, axis=axis + 1)
    return w.reshape(shp[:axis] + (n_blocks * real,) + shp[axis + 1:])


def _block_diag(w):
    n, a, b = w.shape
    eye = jnp.eye(n, dtype=w.dtype)
    return (eye[:, None, :, None] * w[:, :, None, :]).reshape(n * a, n * b)


def _block_diag_t(d, n):
    a, b = d.shape[0] // n, d.shape[1] // n
    d = d.reshape(n, a, n, b)
    return jnp.stack([d[i, :, i, :] for i in range(n)])


_SPLITS = np.cumsum((0,) + SPLIT_SIZES)


def _w_in_groups(w_in):
    sl = lambda i: w_in[:, _SPLITS[i]:_SPLITS[i + 1]]
    xbc = sl(5)
    xbc_pad = jnp.concatenate([_pad_blocks(xbc[:, :MIX], 1, N_HEADS, HEAD),
                               _pad_blocks(xbc[:, MIX:MIX + 2 * HEAD], 1, 2, HEAD),
                               _pad_blocks(xbc[:, MIX + 2 * HEAD:], 1, 2, HEAD)], axis=1)
    return dict(
        cq=sl(0), ckv=sl(1), kr=_pad_blocks(sl(2), 1, 1, QK_ROPE, offset=HEAD), pool=sl(3),
        z=_pad_blocks(sl(4), 1, N_HEADS, HEAD), xbc=xbc_pad, dt=_pad_blocks(sl(6), 1, 1, N_HEADS),
        lru_g=sl(7), lru_x=sl(8), gates=sl(9))


def _w_in_fused(groups):
    parts, at = [], 0
    for name, off, width in IN_LAYOUT:
        assert groups[name].shape[1] == width and off >= at
        if off > at:
            parts.append(jnp.zeros((groups[name].shape[0], off - at), groups[name].dtype))
        parts.append(groups[name])
        at = off + width
    parts.append(jnp.zeros((parts[0].shape[0], IN_ALL_COLS - at), parts[0].dtype))
    return jnp.concatenate(parts, axis=1)


def _in_cols(arr, name):
    off, width = IN_OFFSETS[name]
    return _Cols(arr, off, width)


def _w_in_ungroup(d):
    xbc = d["xbc"]
    w = N_HEADS * LANES
    xbc_real = jnp.concatenate([_unpad_blocks(xbc[:, :w], 1, N_HEADS, HEAD),
                                _unpad_blocks(xbc[:, w:w + 2 * LANES], 1, 2, HEAD),
                                _unpad_blocks(xbc[:, w + 2 * LANES:], 1, 2, HEAD)], axis=1)
    return jnp.concatenate([d["cq"], d["ckv"], _unpad_blocks(d["kr"], 1, 1, QK_ROPE, offset=HEAD), d["pool"],
                            _unpad_blocks(d["z"], 1, N_HEADS, HEAD), xbc_real, _unpad_blocks(d["dt"], 1, 1, N_HEADS),
                            d["lru_g"], d["lru_x"], d["gates"]], axis=1)


def _pad_xbc_vec(v):
    return jnp.concatenate([_pad_blocks(v[..., :MIX], -1, N_HEADS, HEAD),
                            _pad_blocks(v[..., MIX:MIX + 2 * HEAD], -1, 2, HEAD),
                            _pad_blocks(v[..., MIX + 2 * HEAD:], -1, 2, HEAD)], axis=-1)


def _unpad_xbc_vec(v):
    w = N_HEADS * LANES
    return jnp.concatenate([_unpad_blocks(v[..., :w], -1, N_HEADS, HEAD),
                            _unpad_blocks(v[..., w:w + 2 * LANES], -1, 2, HEAD),
                            _unpad_blocks(v[..., w + 2 * LANES:], -1, 2, HEAD)], axis=-1)


def _layer_weights(p):
    q = dict(p)
    q["in_all"] = _w_in_fused(_w_in_groups(p["w_in"]))
    q["uq"] = _pad_blocks(p["w_uq"], 1, N_HEADS, HEAD + QK_ROPE)
    ukv = p["w_ukv"].reshape(KV_LORA, N_HEADS, 2 * HEAD)
    q["ukv"] = jnp.concatenate([_pad_blocks(ukv[:, :, :HEAD].reshape(KV_LORA, -1), 1, N_HEADS, HEAD),
                                _pad_blocks(ukv[:, :, HEAD:].reshape(KV_LORA, -1), 1, N_HEADS, HEAD)], axis=1)
    q["pool_bd"] = _block_diag(p["w_pool"])
    q["lru_bd"] = jnp.concatenate([_block_diag(p["lru_w_a"]), _block_diag(p["lru_w_i"])], axis=1)
    q["br"] = [_pad_blocks(p["w_branch"][0], 0, N_HEADS, HEAD), p["w_branch"][1],
               _pad_blocks(p["w_branch"][2], 0, N_HEADS, HEAD), p["w_branch"][3]]
    q["ssd_conv_w_pad"] = _pad_xbc_vec(p["ssd_conv_w"])
    q["ssd_conv_b_pad"] = _pad_xbc_vec(p["ssd_conv_b"])[None, :]
    q["ssd_norm_pad"] = _pad_blocks(p["ssd_norm"], 0, N_HEADS, HEAD)[None, :]
    return q


def _row(v):
    return v.reshape(1, -1)


def _scal3(v):
    return v.reshape(N_HEADS, 1, 1)


def _layer_fwd(x, p_emb, w, rope, tag):
    n = lambda s: f"{s}_{tag}"
    sv = {"x": x}
    h = _rms_fwd(x, _row(w["g_mix"]), name=n("rms_mix"))
    sv["h"] = h
    u_all = _mm(h, w["in_all"], name=n("in_proj"))
    u = {k: _in_cols(u_all, k) for k in IN_OFFSETS}
    sv["u"] = u

    cqn = _rms_fwd(u["cq"], _row(w["q_norm"]), name=n("rms_q"))
    ckvn = _rms_fwd(u["ckv"], _row(w["kv_norm"]), name=n("rms_kv"))
    q_pad = _mm(cqn, w["uq"], name=n("uq"))
    kv2 = _mm(ckvn, w["ukv"], name=n("ukv"))
    qc, kc, vc = _att_prep(q_pad, kv2, u["kr"], *rope, name=n("att_prep"))
    y_a, lse = _flash_fwd(qc, kc, vc, name=n("flash_fwd"))
    sv.update(cqn=cqn, ckvn=ckvn, qc=qc, kc=kc, vc=vc, y_a=y_a, lse=lse)

    pool_d = _pool_fwd(u["pool"], name=n("pool_fwd"))
    yb_pre, y_b = _mm(pool_d, w["pool_bd"], epilogue=lambda acc, sc: (acc, acc * sc),
                      rowvecs=[_row(w["pool_scale"])], out_dtypes=(F32, BF16), name=n("pool_mm"))
    sv.update(pool_d=pool_d, yb_pre=yb_pre, y_b=y_b)

    xbc_c = _conv_fwd(u["xbc"], w["ssd_conv_w_pad"], w["ssd_conv_b_pad"], silu=True, name=n("ssd_conv"))
    dt8 = lax.slice_in_dim(u_all, IN_OFFSETS["dt"][0], IN_OFFSETS["dt"][0] + N_HEADS, axis=1)
    dtcol = dt8.T[:, :, None]
    dtrow = dt8.T[:, None, :]
    ssd_par = (_scal3(w["ssd_dt_bias"]), _scal3(w["ssd_a_log"]), _scal3(w["ssd_d"]))
    y_ssd, states = _ssd_fwd(xbc_c, dtcol, dtrow, *ssd_par, name=n("ssd_fwd"))

    def ssd_post(yv, zv, gv):
        xh, _ = _rms_parts(yv * _silu(zv), MIX)
        return xh * gv

    y_c = _rowwise(ssd_post, [y_ssd, u["z"]], [w["ssd_norm_pad"]], [(N_HEADS * LANES, BF16, "row")], name=n("ssd_post"))
    sv.update(xbc_c=xbc_c, dtcol=dtcol, dtrow=dtrow, y_ssd=y_ssd, states=states, y_c=y_c)

    xc = _conv_fwd(u["lru_x"], w["lru_conv_w"], _row(w["lru_conv_b"]), silu=False, name=n("lru_conv"))
    pre = _mm(xc, w["lru_bd"], name=n("lru_mm"))
    lru_par = (_row(w["lru_lambda"]), _row(w["lru_b_a"]), _row(w["lru_b_i"]))
    y_d, h_lru = _lru_fwd(pre, xc, u["lru_g"], *lru_par, name=n("lru_fwd"))
    sv.update(xc=xc, pre=pre, h_lru=h_lru, y_d=y_d)

    ys = [y_a, y_b, y_c, y_d]
    merged, ybs = None, []
    for b in range(4):
        if merged is None:
            merged, yb = _mm(ys[b], w["br"][b], epilogue=lambda acc, gt: (_sigmoid(gt) * acc, acc),
                             tiles=[_Cols(u_all, b * D_MODEL, D_MODEL)], out_dtypes=(F32, F32), name=n(f"branch{b}"))
        else:
            merged, yb = _mm(ys[b], w["br"][b], epilogue=lambda acc, gt, mg: (mg + _sigmoid(gt) * acc, acc),
                             tiles=[_Cols(u_all, b * D_MODEL, D_MODEL), merged], out_dtypes=(F32, F32),
                             name=n(f"branch{b}"))
        ybs.append(yb)
    x1 = _mm(merged, w["w_out"], epilogue=lambda acc, xr: (acc + xr,), tiles=[x], name=n("out_proj"))
    sv.update(ybs=ybs, merged=merged, x1=x1)

    h2 = _rms_fwd(x1, _row(w["g_mlp"]), name=n("rms_mlp"))
    a_ff, f_ff = _mm(h2, w["w_ff1"], epilogue=lambda acc: (acc, jnp.square(jnp.maximum(acc, 0.0))),
                     out_dtypes=(F32, BF16), name=n("ff1"))
    x2 = _mm(f_ff, w["w_ff2"], epilogue=lambda acc, xr: (acc + xr,), tiles=[x1], name=n("ff2"))
    sv.update(h2=h2, a_ff=a_ff, f_ff=f_ff, x2=x2)

    h3 = _rms_fwd(x2, _row(w["g_ple"]), name=n("rms_ple"))
    e_ple = _mm(p_emb, w["w_ple"], name=n("ple_emb"))
    x3, gt_ple = _mm(h3, w["w_ple_gate"], epilogue=lambda acc, ev, xr: (xr + ev * _sigmoid(acc), _sigmoid(acc)),
                     tiles=[e_ple, x2], out_dtypes=(F32, F32), name=n("ple_gate"))
    sv.update(h3=h3, e_ple=e_ple, gt_ple=gt_ple, p_emb=p_emb)
    return x3, sv


def _layer_bwd(dx3, sv, w, rope, tag):
    n = lambda s: f"{s}_{tag}"
    gr = {}
    u = sv["u"]

    de, dpre = _rowwise(lambda d, gt, ev: (d * gt, d * ev * gt * (1.0 - gt)), [dx3, sv["gt_ple"], sv["e_ple"]], [],
                        [(D_MODEL, BF16, "row"), (D_MODEL, BF16, "row")], name=n("ple_bwd"))
    gr["w_ple"] = _mm(sv["p_emb"], de, ta=True, name=n("d_w_ple"))
    gr["w_ple_gate"] = _mm(sv["h3"], dpre, ta=True, name=n("d_w_ple_gate"))
    dh3 = _mm(dpre, w["w_ple_gate"], tb=True, out_dtypes=(BF16,), name=n("d_h3"))
    dx2, dg = _rms_bwd(sv["x2"], _row(w["g_ple"]), dh3, dx3, name=n("rms_ple_bwd"))
    gr["g_ple"] = dg[0]

    gr["w_ff2"] = _mm(sv["f_ff"], dx2, ta=True, name=n("d_w_ff2"))
    da = _mm(dx2, w["w_ff2"], tb=True, epilogue=lambda acc, av: (acc * 2.0 * jnp.maximum(av, 0.0),),
             tiles=[sv["a_ff"]], out_dtypes=(BF16,), name=n("d_a_ff"))
    gr["w_ff1"] = _mm(sv["h2"], da, ta=True, name=n("d_w_ff1"))
    dh2 = _mm(da, w["w_ff1"], tb=True, out_dtypes=(BF16,), name=n("d_h2"))
    dx1, dg = _rms_bwd(sv["x1"], _row(w["g_mlp"]), dh2, dx2, name=n("rms_mlp_bwd"))
    gr["g_mlp"] = dg[0]

    gr["w_out"] = _mm(sv["merged"], dx1, ta=True, name=n("d_w_out"))
    dmerged = _mm(dx1, w["w_out"], tb=True, name=n("d_merged"))

    def merge_bwd(dm, gts, y0, y1, y2, y3):
        dys, dgs = [], []
        for b, yb in enumerate((y0, y1, y2, y3)):
            sg = _sigmoid(gts[:, b * D_MODEL:(b + 1) * D_MODEL])
            dys.append(dm * sg)
            dgs.append(dm * yb * sg * (1.0 - sg))
        return (*dys, jnp.concatenate(dgs, axis=1))

    *dybs, dgates = _rowwise(merge_bwd, [dmerged, u["gates"]] + sv["ybs"], [],
                             [(D_MODEL, BF16, "row")] * 4 + [(4 * D_MODEL, BF16, "row")], name=n("merge_bwd"))
    ys = [sv["y_a"], sv["y_b"], sv["y_c"], sv["y_d"]]
    dwb = [_mm(ys[b], dybs[b], ta=True, name=n(f"d_w_branch{b}")) for b in range(4)]
    gr["w_branch"] = jnp.stack([_unpad_blocks(dwb[0], 0, N_HEADS, HEAD), dwb[1],
                                _unpad_blocks(dwb[2], 0, N_HEADS, HEAD), dwb[3]])
    dy_a = _mm(dybs[0], w["br"][0], tb=True, out_dtypes=(BF16,), name=n("d_y_a"))
    dy_b = _mm(dybs[1], w["br"][1], tb=True, name=n("d_y_b"))
    dy_c = _mm(dybs[2], w["br"][2], tb=True, name=n("d_y_c"))
    dy_d = _mm(dybs[3], w["br"][3], tb=True, name=n("d_y_d"))
    du = {"gates": dgates}

    lru_par = (_row(w["lru_lambda"]), _row(w["lru_b_a"]), _row(w["lru_b_i"]))
    dpa, dpi, dxc_direct, du["lru_g"], dlam, dba, dbi = _lru_bwd(
        sv["pre"], sv["xc"], u["lru_g"], *lru_par, sv["h_lru"], dy_d, name=n("lru_bwd"))
    dpre_lru = jnp.concatenate([dpa, dpi], axis=1)
    d_bd = _mm(sv["xc"], dpre_lru, ta=True, name=n("d_lru_w"))
    gr["lru_w_a"] = _block_diag_t(d_bd[:, :MIX], N_HEADS)
    gr["lru_w_i"] = _block_diag_t(d_bd[:, MIX:], N_HEADS)
    gr["lru_lambda"], gr["lru_b_a"], gr["lru_b_i"] = dlam[0], dba[0], dbi[0]
    dxc = _mm(dpre_lru, w["lru_bd"], tb=True, epilogue=lambda acc, t: (acc + t,), tiles=[dxc_direct], name=n("d_xc"))
    du["lru_x"], gr["lru_conv_w"], dcb = _conv_bwd(u["lru_x"], w["lru_conv_w"], _row(w["lru_conv_b"]), dxc,
                                                  silu=False, name=n("lru_conv_bwd"))
    gr["lru_conv_b"] = dcb[0]

    def ssd_post_bwd(dyc, yv, zv, gv):
        sz = _silu(zv)
        dyz, dgain = _rms_bwd_math(yv * sz, gv, dyc, MIX)
        return dyz * sz, dyz * yv * _silu_grad(zv), dgain

    dy_ssd, du["z"], dgain = _rowwise(ssd_post_bwd, [dy_c, sv["y_ssd"], u["z"]], [w["ssd_norm_pad"]],
                                      [(N_HEADS * LANES, F32, "row"), (N_HEADS * LANES, BF16, "row"),
                                       (N_HEADS * LANES, F32, "acc")], name=n("ssd_post_bwd"))
    gr["ssd_norm"] = _unpad_blocks(dgain[0], 0, N_HEADS, HEAD)
    ssd_par = (_scal3(w["ssd_dt_bias"]), _scal3(w["ssd_a_log"]), _scal3(w["ssd_d"]))
    dxs, dbg, dcg, ddt, dbias, dalog, dd = _ssd_bwd(sv["xbc_c"], sv["dtcol"], sv["dtrow"], *ssd_par, sv["states"],
                                                    dy_ssd, name=n("ssd_bwd"))
    s = dxs.shape[0]
    dxbc_c = jnp.concatenate([dxs, dbg, dcg], axis=1)
    gr["ssd_dt_bias"], gr["ssd_a_log"], gr["ssd_d"] = dbias[:, 0, 0], dalog[:, 0, 0], dd[:, 0, 0]
    du["xbc"], dcw, dcb = _conv_bwd(u["xbc"], w["ssd_conv_w_pad"], w["ssd_conv_b_pad"], dxbc_c, silu=True,
                                    name=n("ssd_conv_bwd"))
    gr["ssd_conv_w"], gr["ssd_conv_b"] = _unpad_xbc_vec(dcw), _unpad_xbc_vec(dcb[0])
    du["dt"] = jnp.pad(ddt[:, :, 0].T, ((0, 0), (0, LANES - N_HEADS)))

    dyb_pre, dscale = _rowwise(lambda d, yp, sc: (d * sc, _colsum(d * yp)), [dy_b, sv["yb_pre"]],
                               [_row(w["pool_scale"])], [(MIX, BF16, "row"), (MIX, F32, "acc")], name=n("pool_scale_bwd"))
    gr["pool_scale"] = dscale[0]
    gr["w_pool"] = _block_diag_t(_mm(sv["pool_d"], dyb_pre, ta=True, name=n("d_w_pool")), 4)
    dd_pool = _mm(dyb_pre, w["pool_bd"], tb=True, name=n("d_pool_d"))
    du["pool"] = _pool_bwd(dd_pool, name=n("pool_bwd"))

    dqc, delta = _flash_bwd_dq(sv["qc"], sv["kc"], sv["vc"], sv["y_a"], dy_a, sv["lse"], name=n("flash_dq"))
    to_row = lambda t: t.reshape(N_HEADS, 1, s)
    dkc, dvc = _flash_bwd_dkv(sv["qc"], sv["kc"], sv["vc"], dy_a, to_row(sv["lse"]), to_row(delta), name=n("flash_dkv"))
    dq_pad, du["kr"] = _att_prep_bwd(dqc, dkc, *rope, name=n("att_prep_bwd"))
    d_uq = _mm(sv["cqn"], dq_pad, ta=True, name=n("d_w_uq"))
    gr["w_uq"] = _unpad_blocks(d_uq, 1, N_HEADS, HEAD + QK_ROPE)
    dcqn = _mm(dq_pad, w["uq"], tb=True, out_dtypes=(BF16,), name=n("d_cqn"))
    du["cq"], dg = _rms_bwd(u["cq"], _row(w["q_norm"]), dcqn, name=n("rms_q_bwd"))
    gr["q_norm"] = dg[0]
    dkv2 = jnp.concatenate([dkc, dvc], axis=1).astype(BF16)
    d_ukv = _mm(sv["ckvn"], dkv2, ta=True, name=n("d_w_ukv"))
    wk = N_HEADS * LANES
    dk_real = _unpad_blocks(d_ukv[:, :wk], 1, N_HEADS, HEAD).reshape(KV_LORA, N_HEADS, HEAD)
    dv_real = _unpad_blocks(d_ukv[:, wk:], 1, N_HEADS, HEAD).reshape(KV_LORA, N_HEADS, HEAD)
    gr["w_ukv"] = jnp.concatenate([dk_real, dv_real], axis=2).reshape(KV_LORA, N_HEADS * 2 * HEAD)
    dckvn = _mm(dkv2, w["ukv"], tb=True, out_dtypes=(BF16,), name=n("d_ckvn"))
    du["ckv"], dg = _rms_bwd(u["ckv"], _row(w["kv_norm"]), dckvn, name=n("rms_kv_bwd"))
    gr["kv_norm"] = dg[0]

    du_all = _w_in_fused({k: v.astype(BF16) for k, v in du.items()})
    dw_all = _mm(sv["h"], du_all, ta=True, name=n("d_w_in"))
    gr["w_in"] = _w_in_ungroup({k: dw_all[:, off:off + width] for k, off, width in IN_LAYOUT})
    dh = _mm(du_all, w["in_all"], tb=True, name=n("d_h"))
    dx, dg = _rms_bwd(sv["x"], _row(w["g_mix"]), dh, dx1, name=n("rms_mix_bwd"))
    gr["g_mix"] = dg[0]
    return dx, gr


def _pack_rows(n_elems):
    per = PACK_W * PACK_ROWS
    return -(-n_elems // per) * PACK_ROWS


def _pack_flat(parts, dtype):
    flat = jnp.concatenate([p.reshape(-1).astype(dtype) for p in parts])
    rows = _pack_rows(flat.shape[0])
    return jnp.pad(flat, (0, rows * PACK_W - flat.shape[0])).reshape(rows, PACK_W)


def _unpack_flat(buf, shapes):
    lead = buf.shape[:-2]
    flat = buf.reshape(lead + (-1,))
    out, off = [], 0
    for shp in shapes:
        size = int(np.prod(shp))
        out.append(flat[..., off:off + size].reshape(lead + tuple(shp)))
        off += size
    return out


def _merge_shards(t, axis):
    return jnp.concatenate([t[i] for i in range(4)], axis=axis)


def _split_shards(t, axis):
    return jnp.stack(jnp.split(t, 4, axis=axis))


def _rope_tables(positions):
    inv = 1.0 / (ROPE_THETA ** (jnp.arange(0, QK_ROPE, 2, dtype=F32) / QK_ROPE))
    ang = positions.astype(F32)[:, None] * inv
    cos, sin = jnp.cos(ang), jnp.sin(ang)
    s = ang.shape[0]
    half = QK_ROPE // 2
    z = lambda n_: jnp.zeros((s, n_), F32)
    cos_t = jnp.concatenate([jnp.ones((s, HEAD), F32), cos, cos, jnp.ones((s, LANES - HEAD - QK_ROPE), F32)], axis=1)
    sin_p = jnp.concatenate([z(HEAD + half), sin, z(LANES - HEAD - QK_ROPE)], axis=1)
    sin_m = jnp.concatenate([z(HEAD), -sin, z(half + LANES - HEAD - QK_ROPE)], axis=1)
    return cos_t, sin_p, sin_m


def _loss_head(x, g, target, *, name):
    d = x.shape[1]

    def fn(xv, tv, gv):
        xh, r = _rms_parts(xv, d)
        y = xh * gv
        err = y - tv
        dy = err * (1.0 / d)
        dxh = dy * gv
        dx = r * (dxh - xh * (jnp.sum(dxh * xh, axis=-1, keepdims=True) * (1.0 / d)))
        return dx, _colsum(dy * xh), _colsum(err * err) * (0.5 / d)

    return _rowwise(fn, [x, target], [g], [(d, F32, "row"), (d, F32, "acc"), (d, F32, "acc")], name=name)


MATS = tuple((nm, ax) for nm, ax in BIG if nm not in CONV_SHARDED)


def _grad_view(g, ax_layer):
    if ax_layer == 0:
        return g.reshape(4, g.shape[0] // 4, g.shape[1])
    return g.reshape(1, -1, g.shape[-1])


def _reduce_start(grads_l, c_idx, tag):
    views = [_grad_view(grads_l[nm], ax - 1) for nm, ax in MATS]
    got = _send_half(views, name="send_half_" + tag)
    parts = []
    for (nm, ax), v, gt in zip(MATS, views, got):
        both = _chip_sum_half(v, gt, c_idx, name=f"chip_sum_{nm}_{tag}")
        parts.append(both if ax == 1 else _split_shards(both[0], 1))
    return _push_start(parts, scatter=True, name="push_grads_" + tag)


def _reduce_finish(state, after, k_chip, tag):
    send_sems, recv_sems, parts, lands, _ = state
    parts, landed = _push_wait(send_sems, recv_sems, parts, lands, after, name="wait_grads_" + tag)
    mine = [lax.dynamic_update_index_in_dim(t, lax.dynamic_index_in_dim(p, k_chip, 0, keepdims=False), k_chip, 0)
            for t, p in zip(landed, parts)]
    return _join_halves(mine, name="join_halves_" + tag)


def _step(args):
    x = args["x"][0]
    c_idx = lax.axis_index("c")
    k_chip = 2 * lax.axis_index("x") + lax.axis_index("y")

    mats = MATS
    mine = [[args[nm][l].astype(BF16) for nm, _ in mats] for l in range(2)]
    gathered0 = _gather_halves(mine[0])
    mine1, gathered0 = lax.optimization_barrier((mine[1], gathered0))
    send_sems, recv_sems, blocks1, lands1, token = _push_start(mine1, scatter=False, name="push_weights_l1")
    convs = [(nm, ax) for nm, ax in BIG if nm in CONV_SHARDED]
    conv_all = _gather_all(_pack_flat([args[nm] for nm, _ in convs], F32), name="gather_conv_taps")[0::2]
    full_conv = {nm: _merge_shards(t, ax)
                 for (nm, ax), t in zip(convs, _unpack_flat(conv_all, [args[nm].shape for nm, _ in convs]))}
    rope = _rope_tables(args["positions"][0])

    def layer_weights(l, gathered):
        p = {nm: _merge_shards(t, ax - 1) for (nm, ax), t in zip(mats, gathered)}
        p.update({nm: full_conv[nm][l] for nm in CONV_SHARDED})
        p.update({nm: args[nm][l] for nm in SMALL if nm != "g_final"})
        return _layer_weights(p)

    layers = [layer_weights(0, gathered0), None]
    layers[0]["g_mix"] = layers[0]["g_mix"] + token[0, 0]
    x, sv0 = _layer_fwd(x, args["p"][0, 0], layers[0], rope, "l0")
    own1, landed1 = _push_wait(send_sems, recv_sems, blocks1, lands1, x, name="wait_weights_l1")
    gathered1 = [lax.dynamic_update_index_in_dim(t, own, k_chip, 0) for t, own in zip(landed1, own1)]
    layers[1] = layer_weights(1, gathered1)
    x, sv1 = _layer_fwd(x, args["p"][1, 0], layers[1], rope, "l1")
    saved = [sv0, sv1]

    dx, dg_final, loss_part = _loss_head(x, _row(args["g_final"]), args["loss_target"][0], name="loss_head")
    loss = lax.psum(jnp.sum(loss_part), ("x", "y", "c"))

    grads = [None, None]
    dx, grads[1] = _layer_bwd(dx, saved[1], layers[1], rope, "l1")
    reduce1 = _reduce_start(grads[1], c_idx, "l1")
    dx, grads[0] = _layer_bwd(dx + reduce1[4][0, 0], saved[0], layers[0], rope, "l0")
    reduce0 = _reduce_start(grads[0], c_idx, "l0")

    g_all = {nm: jnp.stack([grads[0][nm], grads[1][nm]]) for nm in SMALL + CONV_SHARDED if nm != "g_final"}
    g_all["g_final"] = dg_final[0]
    all_names = SMALL + CONV_SHARDED
    all_shapes = [g_all[nm].shape for nm in all_names]
    small_sum = _sum_slots(_gather_all(_pack_flat([g_all[nm] for nm in all_names], F32), name="gather_small_grads"),
                           name="sum_devices")
    g_red = dict(zip(all_names, _unpack_flat(small_sum, all_shapes)))
    for nm, ax in BIG:
        if nm in CONV_SHARDED:
            width = args[nm].shape[ax]
            g_red[nm] = lax.dynamic_slice_in_dim(g_red[nm], k_chip * width, width, axis=ax)
    small_shapes = [args[nm].shape for nm in SMALL]
    pack_small = lambda src: _pack_flat([src(nm) for nm in SMALL], F32)
    upd_small = _adamw(pack_small(lambda nm: args[nm]), pack_small(lambda nm: g_red[nm]),
                       pack_small(lambda nm: args["m_" + nm]), pack_small(lambda nm: args["v_" + nm]), name="adamw_small")
    upd = {nm: trip for nm, trip in zip(SMALL, zip(*[_unpack_flat(t, small_shapes) for t in upd_small]))}
    for nm in CONV_SHARDED:
        upd[nm] = _adamw(args[nm], g_red[nm], args["m_" + nm], args["v_" + nm], name="adamw_" + nm)

    slots = [_reduce_finish(reduce0, upd_small[0], k_chip, "l0"), _reduce_finish(reduce1, dx, k_chip, "l1")]
    for i, (nm, _) in enumerate(MATS):
        g_red[nm], *upd[nm] = _adamw_slots(args[nm], slots[0][i], slots[1][i], args["m_" + nm], args["v_" + nm],
                                           name="adamw_" + nm)

    outs = [loss, dx[None]]
    outs += [g_red[nm] for nm in WEIGHTS]
    for i in range(3):
        outs += [upd[nm][i] for nm in WEIGHTS]
    return tuple(outs)


_ARG_NAMES = ("x", "p", "positions") + WEIGHTS + ("loss_target",) + tuple("m_" + nm for nm in WEIGHTS) \
    + tuple("v_" + nm for nm in WEIGHTS)


def kernel(*arrays):
    assert len(arrays) == len(_ARG_NAMES), len(arrays)
    return _step(dict(zip(_ARG_NAMES, arrays)))
```

```python
import functools
import math

import jax
import jax.numpy as jnp
import numpy as np
from jax import lax
from jax.experimental import pallas as pl
from jax.experimental.pallas import tpu as pltpu

F32 = jnp.float32
BF16 = jnp.bfloat16
MXU_DTYPE = BF16
LANES = 128
VMEM_LIMIT = 56 * 1024 * 1024
MM_VMEM_BUDGET = 36 * 1024 * 1024
ELEMENTWISE_BLOCK_BYTES = 2 * 1024 * 1024

D_MODEL = 1024
N_HEADS = 8
HEAD = 64
QK_ROPE = 32
Q_LORA = 384
KV_LORA = 256
MIX = 512
SSD_CHUNK = 128
CONV_W = 4
POOL_WINDOWS = (2, 4, 8, 16)
LRU_C = 8.0
EPS = 1e-6
ROPE_THETA = 10000.0
ATT_SCALE = (HEAD + QK_ROPE) ** -0.5
SPLIT_SIZES = (Q_LORA, KV_LORA, QK_ROPE, MIX, MIX, 768, N_HEADS, MIX, MIX, 4 * D_MODEL)
IN_LAYOUT = (("gates", 0, 4096), ("z", 4096, 1024), ("pool", 5120, 512), ("lru_g", 5632, 512), ("lru_x", 6144, 512),
             ("cq", 6912, 384), ("ckv", 7424, 256), ("xbc", 7680, 1536), ("kr", 9216, 128), ("dt", 9344, 128))
IN_OFFSETS = {name: (off, width) for name, off, width in IN_LAYOUT}
IN_ALL_COLS = 9728

ADAM_LR, ADAM_B1, ADAM_B2, ADAM_EPS, ADAM_WD, ADAM_STEP = 0.001, 0.9, 0.999, 1e-08, 0.01, 10

BIG = (("w_in", 2), ("w_uq", 2), ("w_ukv", 2), ("ssd_conv_w", 2), ("lru_conv_w", 2), ("w_branch", 3),
       ("w_out", 1), ("w_ff1", 2), ("w_ff2", 1), ("w_ple_gate", 1), ("w_ple", 2))
SMALL = ("g_mix", "q_norm", "kv_norm", "w_pool", "pool_scale", "ssd_conv_b", "ssd_dt_bias", "ssd_a_log",
         "ssd_d", "ssd_norm", "lru_conv_b", "lru_w_a", "lru_b_a", "lru_w_i", "lru_b_i", "lru_lambda",
         "g_mlp", "g_ple", "g_final")
WEIGHTS = ("g_mix", "w_in", "q_norm", "w_uq", "kv_norm", "w_ukv", "w_pool", "pool_scale", "ssd_conv_w",
           "ssd_conv_b", "ssd_dt_bias", "ssd_a_log", "ssd_d", "ssd_norm", "lru_conv_w", "lru_conv_b", "lru_w_a",
           "lru_b_a", "lru_w_i", "lru_b_i", "lru_lambda", "w_branch", "w_out", "g_mlp", "w_ff1", "w_ff2", "g_ple",
           "w_ple_gate", "w_ple", "g_final")
CONV_SHARDED = ("ssd_conv_w", "lru_conv_w")
PACK_W = 1024
PACK_ROWS = 64


def _cparams(sem, vmem=VMEM_LIMIT):
    return pltpu.CompilerParams(dimension_semantics=sem, vmem_limit_bytes=vmem)


def _pick(n, cands):
    for c in cands:
        if n % c == 0:
            return c
    return n


class _Cols:
    def __init__(self, arr, off, width):
        self.arr, self.off, self.width = arr, off, width

    shape = property(lambda self: (self.arr.shape[0], self.width))
    dtype = property(lambda self: self.arr.dtype)


def _arr(x):
    return x.arr if isinstance(x, _Cols) else x


def _off(x, unit):
    off = x.off if isinstance(x, _Cols) else 0
    assert off % unit == 0, (off, unit)
    return off // unit


def _sigmoid(x):
    return 1.0 / (1.0 + jnp.exp(-x))


def _silu(x):
    return x * _sigmoid(x)


def _silu_grad(x):
    s = _sigmoid(x)
    return s * (1.0 + x * (1.0 - s))


def _softplus(x):
    e = jnp.exp(-jnp.abs(x))
    log1p_e = jnp.where(e < 1e-3, e * (1.0 - e * (0.5 - e * (1.0 / 3.0))), jnp.log(1.0 + e))
    return jnp.maximum(x, 0.0) + log1p_e


_GELU_C = math.sqrt(2.0 / math.pi)


def _gelu(x):
    t = jnp.tanh(_GELU_C * (x + 0.044715 * x * x * x))
    return 0.5 * x * (1.0 + t)


def _gelu_grad(x):
    t = jnp.tanh(_GELU_C * (x + 0.044715 * x * x * x))
    return 0.5 * (1.0 + t) + 0.5 * x * (1.0 - t * t) * _GELU_C * (1.0 + 3.0 * 0.044715 * x * x)


def _neg_expm1(x):
    series = -x * (1.0 + 0.5 * x * (1.0 + (1.0 / 3.0) * x * (1.0 + 0.25 * x)))
    return jnp.where(x > -0.05, series, 1.0 - jnp.exp(x))


def _shift_down(x, k, row):
    return jnp.where(row >= k, pltpu.roll(x, k, 0), 0.0)


def _shift_up(x, k, row):
    n = x.shape[0]
    return jnp.where(row < n - k, pltpu.roll(x, n - k, 0), 0.0)


def _cumsum_rows(x, row):
    d = 1
    while d < x.shape[0]:
        x = x + _shift_down(x, d, row)
        d *= 2
    return x


def _rev_cumsum_rows(x, row):
    d = 1
    while d < x.shape[0]:
        x = x + _shift_up(x, d, row)
        d *= 2
    return x


def _cumsum_lanes(x, col):
    d = 1
    while d < x.shape[1]:
        x = x + jnp.where(col >= d, pltpu.roll(x, d, 1), 0.0)
        d *= 2
    return x


def _dot(a, b, ta=False, tb=False):
    dn = (((0 if ta else 1,), (1 if tb else 0,)), ((), ()))
    return lax.dot_general(a.astype(MXU_DTYPE), b.astype(MXU_DTYPE), dn, preferred_element_type=F32)


def _mm_tiles(m, n, k, a_bytes, b_bytes, mn_bytes):
    best = None
    for tm in (1024, 512, 384, 256, 128):
        for tn in (1024, 512, 384, 256, 128):
            for tk in (2048, 1024, 512, 384, 256, 128):
                if m % tm or n % tn or k % tk:
                    continue
                vmem = 2 * (tm * tk * a_bytes + tk * tn * b_bytes) + 2 * tm * tn * mn_bytes + 4 * tm * tn
                vmem += 2 * (tm * tk + tk * tn)
                if vmem > MM_VMEM_BUDGET:
                    continue
                steps = (m // tm) * (n // tn) * (k // tk)
                key = (steps, vmem)
                if best is None or key < best[0]:
                    best = (key, (tm, tn, tk))
    assert best is not None, (m, n, k)
    return best[1]


def _mm(a, b, *, ta=False, tb=False, epilogue=None, tiles=(), rowvecs=(), out_dtypes=(F32,), name):
    m, k = (a.shape[1], a.shape[0]) if ta else a.shape
    n = b.shape[0] if tb else b.shape[1]
    assert (b.shape[1] if tb else b.shape[0]) == k, (a.shape, b.shape, ta, tb)
    mn_bytes = sum(t.dtype.itemsize for t in tiles) + sum(jnp.dtype(dt).itemsize for dt in out_dtypes)
    tm, tn, tk = _mm_tiles(m, n, k, a.dtype.itemsize, b.dtype.itemsize, mn_bytes)
    nk = k // tk
    nt, nr, no = len(tiles), len(rowvecs), len(out_dtypes)

    def body(*refs):
        a_ref, b_ref = refs[0], refs[1]
        tile_refs = refs[2:2 + nt]
        row_refs = refs[2 + nt:2 + nt + nr]
        out_refs = refs[2 + nt + nr:2 + nt + nr + no]
        acc_ref = refs[-1]
        kk = pl.program_id(2)

        @pl.when(kk == 0)
        def _():
            acc_ref[...] = jnp.zeros_like(acc_ref)

        acc_ref[...] += _dot(a_ref[...], b_ref[...], ta, tb)

        @pl.when(kk == nk - 1)
        def _():
            acc = acc_ref[...]
            if epilogue is None:
                outs = (acc,)
            else:
                outs = epilogue(acc, *[t[...] for t in tile_refs], *[r[...] for r in row_refs])
            for o_ref, o in zip(out_refs, outs):
                o_ref[...] = o.astype(o_ref.dtype)

    a_spec = pl.BlockSpec((tk, tm), lambda i, j, kk: (kk, i)) if ta else pl.BlockSpec((tm, tk), lambda i, j, kk: (i, kk))
    b_spec = pl.BlockSpec((tn, tk), lambda i, j, kk: (j, kk)) if tb else pl.BlockSpec((tk, tn), lambda i, j, kk: (kk, j))
    mn_spec = pl.BlockSpec((tm, tn), lambda i, j, kk: (i, j))
    row_spec = pl.BlockSpec((1, tn), lambda i, j, kk: (0, j))
    tile_specs = [pl.BlockSpec((tm, tn), lambda i, j, kk, ob=_off(t, tn): (i, j + ob)) for t in tiles]
    outs = pl.pallas_call(
        body, name=name,
        grid=(m // tm, n // tn, nk),
        in_specs=[a_spec, b_spec] + tile_specs + [row_spec] * nr,
        out_specs=[mn_spec] * no,
        out_shape=[jax.ShapeDtypeStruct((m, n), dt) for dt in out_dtypes],
        scratch_shapes=[pltpu.VMEM((tm, tn), F32)],
        compiler_params=_cparams(("parallel", "parallel", "arbitrary")),
    )(a, b, *[_arr(t) for t in tiles], *rowvecs)
    return outs[0] if no == 1 else tuple(outs)


def _rowwise(fn, rows, fulls, outs, *, name, tm=None):
    r = rows[0].shape[0]
    if tm is None:
        widest = max([x.shape[1] for x in rows] + [o[0] for o in outs])
        tm = _pick(r, (max(8, min(512, (512 * 1024) // widest)), 256, 128, 64, 32, 16, 8))
    nrow, nfull, nout = len(rows), len(fulls), len(outs)

    def body(*refs):
        row_refs = refs[:nrow]
        full_refs = refs[nrow:nrow + nfull]
        out_refs = refs[nrow + nfull:]
        res = fn(*[x[...] for x in row_refs], *[x[...] for x in full_refs])
        if not isinstance(res, (tuple, list)):
            res = (res,)
        step = pl.program_id(0)
        for o_ref, o, spec in zip(out_refs, res, outs):
            if spec[2] == "row":
                o_ref[...] = o.astype(o_ref.dtype)
            else:
                @pl.when(step == 0)
                def _(o_ref=o_ref):
                    o_ref[...] = jnp.zeros_like(o_ref)
                o_ref[...] += o

    in_specs = [pl.BlockSpec((tm, x.shape[1]), lambda i, ob=_off(x, x.shape[1]): (i, ob)) for x in rows]
    in_specs += [pl.BlockSpec(x.shape, lambda i, nd=x.ndim: (0,) * nd) for x in fulls]
    out_specs, out_shape = [], []
    for c, dt, kind in outs:
        if kind == "row":
            out_specs.append(pl.BlockSpec((tm, c), lambda i: (i, 0)))
            out_shape.append(jax.ShapeDtypeStruct((r, c), dt))
        else:
            out_specs.append(pl.BlockSpec((1, c), lambda i: (0, 0)))
            out_shape.append(jax.ShapeDtypeStruct((1, c), F32))
    res = pl.pallas_call(
        body, name=name, grid=(r // tm,), in_specs=in_specs, out_specs=out_specs, out_shape=out_shape,
        compiler_params=_cparams(("arbitrary",)),
    )(*[_arr(x) for x in rows], *fulls)
    return res[0] if nout == 1 else tuple(res)


def _colsum(x):
    return jnp.sum(x, axis=0, keepdims=True)


def _rms_parts(x, n_real):
    r = lax.rsqrt(jnp.sum(x * x, axis=-1, keepdims=True) * (1.0 / n_real) + EPS)
    return x * r, r


def _rms_fwd(x, g, *, n_real=None, out_dtype=BF16, name):
    n_real = n_real or x.shape[1]

    def fn(xv, gv):
        xh, _ = _rms_parts(xv, n_real)
        return xh * gv

    return _rowwise(fn, [x], [g], [(x.shape[1], out_dtype, "row")], name=name)


def _rms_bwd_math(xv, gv, dh, n_real):
    xh, r = _rms_parts(xv, n_real)
    dxh = dh * gv
    dx = r * (dxh - xh * (jnp.sum(dxh * xh, axis=-1, keepdims=True) * (1.0 / n_real)))
    return dx, _colsum(dh * xh)


def _rms_bwd(x, g, dh, res=None, *, name):
    n = x.shape[1]
    if res is None:
        def fn(xv, dhv, gv):
            return _rms_bwd_math(xv, gv, dhv.astype(F32), n)
        rows = [x, dh]
    else:
        def fn(xv, dhv, rv, gv):
            dx, dg = _rms_bwd_math(xv, gv, dhv.astype(F32), n)
            return dx + rv, dg
        rows = [x, dh, res]
    return _rowwise(fn, rows, [g], [(n, F32, "row"), (n, F32, "acc")], name=name)


def _seq_call(body, ins, outs, n_blocks, *, name):
    in_specs, args = [], []
    for x, kind in ins:
        in_specs.append(pl.BlockSpec((x.shape[0], LANES), lambda j, ob=_off(x, LANES): (0, j + ob)))
        args.append(_arr(x))
    out_specs, out_shape = [], []
    for shape, dt in outs:
        out_specs.append(pl.BlockSpec((shape[0], LANES), lambda j: (0, j)))
        out_shape.append(jax.ShapeDtypeStruct(shape, dt))
    res = pl.pallas_call(body, name=name, grid=(n_blocks,), in_specs=in_specs, out_specs=out_specs,
                         out_shape=out_shape, compiler_params=_cparams(("parallel",)))(*args)
    return res[0] if len(outs) == 1 else tuple(res)


def _conv_pre(x, w, b, row):
    acc = x * w[CONV_W - 1:CONV_W, :] + b
    for k in range(CONV_W - 1):
        acc = acc + _shift_down(x, CONV_W - 1 - k, row) * w[k:k + 1, :]
    return acc


def _conv_fwd(x, w, b, *, silu, name):
    s, c = x.shape

    def body(x_ref, w_ref, b_ref, y_ref):
        xv = x_ref[...]
        row = lax.broadcasted_iota(jnp.int32, xv.shape, 0)
        pre = _conv_pre(xv, w_ref[...], b_ref[...], row)
        y_ref[...] = _silu(pre) if silu else pre

    return _seq_call(body, [(x, "seq"), (w, "par"), (b, "par")], [((s, c), F32)], c // LANES, name=name)


def _conv_bwd(x, w, b, dy, *, silu, name):
    s, c = x.shape

    def body(x_ref, w_ref, b_ref, dy_ref, dx_ref, dw_ref, db_ref):
        xv, wv, dv = x_ref[...], w_ref[...], dy_ref[...]
        row = lax.broadcasted_iota(jnp.int32, xv.shape, 0)
        if silu:
            dv = dv * _silu_grad(_conv_pre(xv, wv, b_ref[...], row))
        dx = dv * wv[CONV_W - 1:CONV_W, :]
        dws = [None] * CONV_W
        dws[CONV_W - 1] = _colsum(dv * xv)
        for k in range(CONV_W - 1):
            sh = CONV_W - 1 - k
            dx = dx + _shift_up(dv, sh, row) * wv[k:k + 1, :]
            dws[k] = _colsum(dv * _shift_down(xv, sh, row))
        dx_ref[...] = dx
        for k in range(CONV_W):
            dw_ref[k:k + 1, :] = dws[k]
        db_ref[...] = _colsum(dv)

    return _seq_call(body, [(x, "seq"), (w, "par"), (b, "par"), (dy, "seq")],
                     [((s, c), F32), ((CONV_W, c), F32), ((1, c), F32)], c // LANES, name=name)


def _pool_select(levels):
    g = pl.program_id(0)
    return jnp.where(g == 0, levels[0], jnp.where(g == 1, levels[1], jnp.where(g == 2, levels[2], levels[3])))


def _pool_count(row):
    g = pl.program_id(0)
    w = jnp.where(g == 0, POOL_WINDOWS[0], jnp.where(g == 1, POOL_WINDOWS[1],
                                                     jnp.where(g == 2, POOL_WINDOWS[2], POOL_WINDOWS[3])))
    return jnp.minimum(row + 1, w).astype(F32)


def _pool_fwd(u, *, name):
    def body(u_ref, d_ref):
        uv = u_ref[...]
        row = lax.broadcasted_iota(jnp.int32, uv.shape, 0)
        levels, cur, sh = [], uv, 1
        for _ in POOL_WINDOWS:
            cur = cur + _shift_down(cur, sh, row)
            levels.append(cur)
            sh *= 2
        d_ref[...] = _pool_select(levels) / _pool_count(row) - uv

    return _seq_call(body, [(u, "seq")], [(u.shape, F32)], u.shape[1] // LANES, name=name)


def _pool_bwd(dd, *, name):
    def body(dd_ref, du_ref):
        dv = dd_ref[...]
        row = lax.broadcasted_iota(jnp.int32, dv.shape, 0)
        levels, cur, sh = [], dv / _pool_count(row), 1
        for _ in POOL_WINDOWS:
            cur = cur + _shift_up(cur, sh, row)
            levels.append(cur)
            sh *= 2
        du_ref[...] = _pool_select(levels) - dv

    return _seq_call(body, [(dd, "seq")], [(dd.shape, F32)], dd.shape[1] // LANES, name=name)


def _lru_gates(pre_a, pre_i, xc, lam, b_a, b_i):
    r = _sigmoid(pre_a + b_a)
    i = _sigmoid(pre_i + b_i)
    sp = _softplus(-lam)
    log_a = -LRU_C * r * sp
    a = jnp.exp(log_a)
    mult = jnp.sqrt(_neg_expm1(2.0 * log_a))
    return r, i, sp, a, mult


def _lru_fwd(pre, xc, gate_in, lam, b_a, b_i, *, name):
    s, c = xc.shape
    nb = c // LANES

    def body(pa_ref, pi_ref, xc_ref, g_ref, lam_ref, ba_ref, bi_ref, y_ref, h_ref):
        xv = xc_ref[...]
        row = lax.broadcasted_iota(jnp.int32, xv.shape, 0)
        _, i, _, a, mult = _lru_gates(pa_ref[...], pi_ref[...], xv, lam_ref[...], ba_ref[...], bi_ref[...])
        h = xv * i * mult
        d = 1
        while d < s:
            h = h + a * _shift_down(h, d, row)
            a = a * jnp.where(row >= d, pltpu.roll(a, d, 0), 1.0)
            d *= 2
        h_ref[...] = h
        y_ref[...] = h * _gelu(g_ref[...])

    blk = lambda off: pl.BlockSpec((s, LANES), lambda j: (0, j + off))
    par = pl.BlockSpec((1, LANES), lambda j: (0, j))
    return pl.pallas_call(
        body, name=name, grid=(nb,),
        in_specs=[blk(0), blk(nb), blk(0), blk(_off(gate_in, LANES)), par, par, par],
        out_specs=[blk(0), blk(0)],
        out_shape=[jax.ShapeDtypeStruct((s, c), F32)] * 2,
        compiler_params=_cparams(("parallel",)),
    )(pre, pre, xc, _arr(gate_in), lam, b_a, b_i)


def _lru_bwd(pre, xc, gate_in, lam, b_a, b_i, h, dy, *, name):
    s, c = xc.shape
    nb = c // LANES

    def body(pa_ref, pi_ref, xc_ref, g_ref, lam_ref, ba_ref, bi_ref, h_ref, dy_ref,
             dpa_ref, dpi_ref, dxc_ref, dg_ref, dlam_ref, dba_ref, dbi_ref):
        xv, gv, hv, dv = xc_ref[...], g_ref[...], h_ref[...], dy_ref[...]
        row = lax.broadcasted_iota(jnp.int32, xv.shape, 0)
        r, i, sp, a, mult = _lru_gates(pa_ref[...], pi_ref[...], xv, lam_ref[...], ba_ref[...], bi_ref[...])
        dg_ref[...] = dv * hv * _gelu_grad(gv)
        dh = dv * _gelu(gv)
        an = jnp.where(row < s - 1, pltpu.roll(a, s - 1, 0), 0.0)
        d = 1
        while d < s:
            dh = dh + an * _shift_up(dh, d, row)
            an = an * jnp.where(row < s - d, pltpu.roll(an, s - d, 0), 1.0)
            d *= 2
        da = dh * _shift_down(hv, 1, row)
        dxc_ref[...] = dh * i * mult
        di = dh * xv * mult
        dmult = dh * xv * i
        dlog_a = (da - dmult * a / mult) * a
        dr = dlog_a * (-LRU_C) * sp
        dlam_ref[...] = _colsum(dlog_a * LRU_C * r * _sigmoid(-lam_ref[...]))
        dpa = dr * r * (1.0 - r)
        dpi = di * i * (1.0 - i)
        dpa_ref[...] = dpa
        dpi_ref[...] = dpi
        dba_ref[...] = _colsum(dpa)
        dbi_ref[...] = _colsum(dpi)

    blk = lambda off: pl.BlockSpec((s, LANES), lambda j: (0, j + off))
    par = pl.BlockSpec((1, LANES), lambda j: (0, j))
    sc = jax.ShapeDtypeStruct((s, c), F32)
    pc = jax.ShapeDtypeStruct((1, c), F32)
    dpa, dpi, dxc, dg, dlam, dba, dbi = pl.pallas_call(
        body, name=name, grid=(nb,),
        in_specs=[blk(0), blk(nb), blk(0), blk(_off(gate_in, LANES)), par, par, par, blk(0), blk(0)],
        out_specs=[blk(0), blk(0), blk(0), blk(0), par, par, par],
        out_shape=[sc, sc, sc, sc, pc, pc, pc],
        compiler_params=_cparams(("parallel",)),
    )(pre, pre, xc, _arr(gate_in), lam, b_a, b_i, h, dy)
    return dpa, dpi, dxc, dg, dlam, dba, dbi


GROUP_HEADS = 4


def _ssd_specs(nc, order):
    gw = GROUP_HEADS * LANES
    return dict(
        x=pl.BlockSpec((SSD_CHUNK, gw), lambda g, ci: (order(ci), g)),
        b=pl.BlockSpec((SSD_CHUNK, LANES), lambda g, ci: (order(ci), N_HEADS + g)),
        c=pl.BlockSpec((SSD_CHUNK, LANES), lambda g, ci: (order(ci), N_HEADS + 2 + g)),
        dtcol=pl.BlockSpec((GROUP_HEADS, SSD_CHUNK, 1), lambda g, ci: (g, order(ci), 0)),
        dtrow=pl.BlockSpec((GROUP_HEADS, 1, SSD_CHUNK), lambda g, ci: (g, 0, order(ci))),
        scal=pl.BlockSpec((GROUP_HEADS, 1, 1), lambda g, ci: (g, 0, 0)),
        state=pl.BlockSpec((GROUP_HEADS, 1, LANES, LANES), lambda g, ci: (g, order(ci), 0, 0)),
        group=pl.BlockSpec((SSD_CHUNK, LANES), lambda g, ci: (order(ci), g)),
        pacc=pl.BlockSpec((GROUP_HEADS, 1, LANES), lambda g, ci: (g, 0, 0)),
    )


def _ssd_chunk_terms(dtcol, dtrow, bias, a_log):
    shp = (SSD_CHUNK, SSD_CHUNK)
    row = lax.broadcasted_iota(jnp.int32, shp, 0)
    col = lax.broadcasted_iota(jnp.int32, shp, 1)
    a_head = -jnp.exp(a_log)
    dt_c = jnp.broadcast_to(_softplus(dtcol + bias), shp)
    dt_r = jnp.broadcast_to(_softplus(dtrow + bias), shp)
    cs_c = _cumsum_rows(dt_c * a_head, row)
    cs_r = _cumsum_lanes(dt_r * a_head, col)
    cs_last = jnp.sum(jnp.where(row == SSD_CHUNK - 1, cs_c, 0.0), axis=0, keepdims=True)
    return row, col, a_head, dt_c, cs_c, cs_r, cs_last


def _ssd_fwd(xbc, dtcol, dtrow, bias, a_log, dskip, *, name):
    s = xbc.shape[0]
    nc = s // SSD_CHUNK

    def body(x_ref, b_ref, c_ref, dtc_ref, dtr_ref, bias_ref, alog_ref, d_ref, y_ref, st_ref, state):
        ci = pl.program_id(1)

        @pl.when(ci == 0)
        def _():
            state[...] = jnp.zeros_like(state)

        bm, cm = b_ref[...], c_ref[...]
        cb = _dot(cm, bm, tb=True)
        bm_t = bm.T
        for r in range(GROUP_HEADS):
            lanes = slice(r * LANES, (r + 1) * LANES)
            xv = x_ref[:, lanes]
            row, col, _, dt_c, cs_c, cs_r, cs_last = _ssd_chunk_terms(dtc_ref[r], dtr_ref[r], bias_ref[r], alog_ref[r])
            g = cb * jnp.exp(jnp.where(col <= row, cs_c - cs_r, -jnp.inf))
            xdt = xv * dt_c
            st = state[r]
            st_ref[r, 0] = st
            y_ref[:, lanes] = _dot(g, xdt) + _dot(cm, st) * jnp.exp(cs_c) + xv * d_ref[r]
            state[r] = jnp.exp(cs_last) * st + _dot(bm_t, xdt * jnp.exp(cs_last - cs_c))

    sp = _ssd_specs(nc, lambda ci: ci)
    return pl.pallas_call(
        body, name=name, grid=(N_HEADS // GROUP_HEADS, nc),
        in_specs=[sp["x"], sp["b"], sp["c"], sp["dtcol"], sp["dtrow"], sp["scal"], sp["scal"], sp["scal"]],
        out_specs=[sp["x"], sp["state"]],
        out_shape=[jax.ShapeDtypeStruct((s, N_HEADS * LANES), F32),
                   jax.ShapeDtypeStruct((N_HEADS, nc, LANES, LANES), F32)],
        scratch_shapes=[pltpu.VMEM((GROUP_HEADS, LANES, LANES), F32)],
        compiler_params=_cparams(("parallel", "arbitrary")),
    )(xbc, xbc, xbc, dtcol, dtrow, bias, a_log, dskip)


def _ssd_bwd(xbc, dtcol, dtrow, bias, a_log, dskip, states, dy, *, name):
    s = xbc.shape[0]
    nc = s // SSD_CHUNK

    def body(x_ref, b_ref, c_ref, dtc_ref, dtr_ref, bias_ref, alog_ref, d_ref, st_ref, dy_ref,
             dx_ref, db_ref, dc_ref, ddt_ref, dbias_ref, dalog_ref, dd_ref, dstate):
        ci = pl.program_id(1)

        @pl.when(ci == 0)
        def _():
            dstate[...] = jnp.zeros_like(dstate)
            dbias_ref[...] = jnp.zeros_like(dbias_ref)
            dalog_ref[...] = jnp.zeros_like(dalog_ref)
            dd_ref[...] = jnp.zeros_like(dd_ref)

        bm, cm = b_ref[...], c_ref[...]
        cb = _dot(cm, bm, tb=True)
        cb_t = _dot(bm, cm, tb=True)
        cm_t = cm.T
        rowsum = lambda v: jnp.sum(v, axis=1, keepdims=True)
        tot = lambda v: jnp.broadcast_to(jnp.sum(v, axis=0, keepdims=True), (1, LANES))
        dbm_sum, dcm_sum = None, None
        for r in range(GROUP_HEADS):
            lanes = slice(r * LANES, (r + 1) * LANES)
            xv, dyv, st = x_ref[:, lanes], dy_ref[:, lanes], st_ref[r, 0]
            dtraw_c, bias = dtc_ref[r], bias_ref[r]
            row, col, a_head, dt_c, cs_c, cs_r, cs_last = _ssd_chunk_terms(dtraw_c, dtr_ref[r], bias, alog_ref[r])
            lmat = jnp.exp(jnp.where(col <= row, cs_c - cs_r, -jnp.inf))
            lmat_t = jnp.exp(jnp.where(row <= col, cs_r - cs_c, -jnp.inf))
            g, g_t = cb * lmat, cb_t * lmat_t
            xdt = xv * dt_c
            e_c = jnp.exp(cs_c)
            f_c = jnp.exp(cs_last - cs_c)
            e_last = jnp.exp(cs_last)
            w = xdt * f_c
            dst = dstate[r]

            dg = _dot(dyv, xdt, tb=True)
            dg_t = _dot(xdt, dyv, tb=True)
            dxdt = _dot(g_t, dyv)
            dcs = rowsum(dg * g) - rowsum(dg_t * g_t)
            dcm = _dot(dg * lmat, bm)
            dbm = _dot(dg_t * lmat_t, cm)
            z = _dot(cm, st)
            dz = dyv * e_c
            dcs = dcs + rowsum(dz * z)
            dcm = dcm + _dot(dz, st, tb=True)
            dstate[r] = _dot(cm_t, dz) + e_last * dst
            dcs_last = jnp.sum(rowsum(dst * st), axis=0, keepdims=True) * jnp.max(e_last, axis=1, keepdims=True)
            dbm = dbm + _dot(w, dst, tb=True)
            dw = _dot(bm, dst)
            dxdt = dxdt + dw * f_c
            q = rowsum(dw * w)
            dcs = dcs - q
            dcs_last = dcs_last + jnp.sum(q, axis=0, keepdims=True)
            dx_ref[:, lanes] = dxdt * dt_c + dyv * d_ref[r]
            ddt = rowsum(dxdt * xv)
            dcs_full = jnp.broadcast_to(dcs, (SSD_CHUNK, SSD_CHUNK)) + jnp.where(row == SSD_CHUNK - 1, dcs_last, 0.0)
            da = jnp.max(_rev_cumsum_rows(dcs_full, row), axis=1, keepdims=True)
            dt_col = jnp.max(dt_c, axis=1, keepdims=True)
            draw = (ddt + da * a_head) * _sigmoid(dtraw_c + bias)
            ddt_ref[r] = draw
            dbias_ref[r] += tot(draw)
            dalog_ref[r] += tot(da * dt_col) * a_head
            dd_ref[r] += tot(rowsum(dyv * xv))
            dbm_sum = dbm if dbm_sum is None else dbm_sum + dbm
            dcm_sum = dcm if dcm_sum is None else dcm_sum + dcm
        db_ref[...] = dbm_sum
        dc_ref[...] = dcm_sum

    sp = _ssd_specs(nc, lambda ci: nc - 1 - ci)
    return pl.pallas_call(
        body, name=name, grid=(N_HEADS // GROUP_HEADS, nc),
        in_specs=[sp["x"], sp["b"], sp["c"], sp["dtcol"], sp["dtrow"], sp["scal"], sp["scal"], sp["scal"],
                  sp["state"], sp["x"]],
        out_specs=[sp["x"], sp["group"], sp["group"], sp["dtcol"], sp["pacc"], sp["pacc"], sp["pacc"]],
        out_shape=[jax.ShapeDtypeStruct((s, N_HEADS * LANES), F32),
                   jax.ShapeDtypeStruct((s, 2 * LANES), F32),
                   jax.ShapeDtypeStruct((s, 2 * LANES), F32),
                   jax.ShapeDtypeStruct((N_HEADS, s, 1), F32),
                   jax.ShapeDtypeStruct((N_HEADS, 1, LANES), F32),
                   jax.ShapeDtypeStruct((N_HEADS, 1, LANES), F32),
                   jax.ShapeDtypeStruct((N_HEADS, 1, LANES), F32)],
        scratch_shapes=[pltpu.VMEM((GROUP_HEADS, LANES, LANES), F32)],
        compiler_params=_cparams(("parallel", "arbitrary")),
    )(xbc, xbc, xbc, dtcol, dtrow, bias, a_log, dskip, states, dy)


def _att_tile(s):
    return _pick(s, (512, 256, 128))


def _tri(t, transposed=False):
    r = lax.broadcasted_iota(jnp.int32, (t, t), 0)
    c = lax.broadcasted_iota(jnp.int32, (t, t), 1)
    return (r <= c) if transposed else (c <= r)


def _rows_at(ref, blk, t):
    return ref[pl.ds(pl.multiple_of(blk * t, t), t), :]


def _flash_fwd(q, k, v, *, name):
    s = q.shape[0]
    t = _att_tile(s)
    nq = s // t

    def body(q_ref, k_ref, v_ref, o_ref, lse_ref):
        i = pl.program_id(1)
        qv = q_ref[...]

        def step(j, carry, diagonal):
            m_old, l_old, acc = carry
            sc = _dot(qv, _rows_at(k_ref, j, t), tb=True)
            if diagonal:
                sc = jnp.where(_tri(t), sc, -jnp.inf)
            m_new = jnp.maximum(m_old, jnp.max(sc, axis=1, keepdims=True))
            alpha = jnp.exp(m_old - m_new)
            p = jnp.exp(sc - m_new)
            return (m_new, alpha * l_old + jnp.sum(p, axis=1, keepdims=True),
                    alpha * acc + _dot(p, _rows_at(v_ref, j, t)))

        init = (jnp.full((t, 1), -jnp.inf, F32), jnp.zeros((t, 1), F32), jnp.zeros((t, LANES), F32))
        carry = lax.fori_loop(0, i, lambda j, c: step(j, c, False), init)
        m_fin, l_fin, acc = step(i, carry, True)
        o_ref[...] = (acc / l_fin).astype(o_ref.dtype)
        lse_ref[0] = m_fin + jnp.log(l_fin)

    q_spec = pl.BlockSpec((t, LANES), lambda h, i: (i, h))
    kv_spec = pl.BlockSpec((s, LANES), lambda h, i: (0, h))
    return pl.pallas_call(
        body, name=name, grid=(N_HEADS, nq),
        in_specs=[q_spec, kv_spec, kv_spec],
        out_specs=[q_spec, pl.BlockSpec((1, t, 1), lambda h, i: (h, i, 0))],
        out_shape=[jax.ShapeDtypeStruct(q.shape, BF16), jax.ShapeDtypeStruct((N_HEADS, s, 1), F32)],
        compiler_params=_cparams(("parallel", "arbitrary")),
    )(q, k, v)


def _flash_bwd_dq(q, k, v, o, do, lse, *, name):
    s = q.shape[0]
    t = _att_tile(s)
    nq = s // t

    def body(q_ref, k_ref, v_ref, o_ref, do_ref, lse_ref, dq_ref, dl_ref):
        i = pl.program_id(1)
        qv, dov, lse = q_ref[...], do_ref[...], lse_ref[0]
        delta = jnp.sum(dov.astype(F32) * o_ref[...].astype(F32), axis=1, keepdims=True)
        dl_ref[0] = delta

        def step(j, acc, diagonal):
            kj = _rows_at(k_ref, j, t)
            p = jnp.exp(_dot(qv, kj, tb=True) - lse)
            if diagonal:
                p = jnp.where(_tri(t), p, 0.0)
            ds = p * (_dot(dov, _rows_at(v_ref, j, t), tb=True) - delta)
            return acc + _dot(ds, kj)

        acc = lax.fori_loop(0, i, lambda j, c: step(j, c, False), jnp.zeros((t, LANES), F32))
        dq_ref[...] = step(i, acc, True) * ATT_SCALE

    q_spec = pl.BlockSpec((t, LANES), lambda h, i: (i, h))
    kv_spec = pl.BlockSpec((s, LANES), lambda h, i: (0, h))
    col_spec = pl.BlockSpec((1, t, 1), lambda h, i: (h, i, 0))
    return pl.pallas_call(
        body, name=name, grid=(N_HEADS, nq),
        in_specs=[q_spec, kv_spec, kv_spec, q_spec, q_spec, col_spec],
        out_specs=[q_spec, col_spec],
        out_shape=[jax.ShapeDtypeStruct(q.shape, F32), jax.ShapeDtypeStruct((N_HEADS, s, 1), F32)],
        compiler_params=_cparams(("parallel", "arbitrary")),
    )(q, k, v, o, do, lse)


def _flash_bwd_dkv(q, k, v, do, lse_row, delta_row, *, name):
    s = q.shape[0]
    t = _att_tile(s)
    nq = s // t

    def body(q_ref, k_ref, v_ref, do_ref, lse_ref, dl_ref, dk_ref, dv_ref):
        j = pl.program_id(1)
        kv, vv = k_ref[...], v_ref[...]

        def step(i, carry, diagonal):
            dk, dv = carry
            qi, doi = _rows_at(q_ref, i, t), _rows_at(do_ref, i, t)
            cols = pl.ds(pl.multiple_of(i * t, t), t)
            p_t = jnp.exp(_dot(kv, qi, tb=True) - lse_ref[0, :, cols])
            if diagonal:
                p_t = jnp.where(_tri(t, transposed=True), p_t, 0.0)
            ds_t = p_t * (_dot(vv, doi, tb=True) - dl_ref[0, :, cols])
            return dk + _dot(ds_t, qi), dv + _dot(p_t, doi)

        zero = jnp.zeros((t, LANES), F32)
        carry = step(j, (zero, zero), True)
        dk, dv = lax.fori_loop(j + 1, nq, lambda i, c: step(i, c, False), carry)
        dk_ref[...] = dk
        dv_ref[...] = dv

    q_spec = pl.BlockSpec((s, LANES), lambda h, j: (0, h))
    kv_spec = pl.BlockSpec((t, LANES), lambda h, j: (j, h))
    row_spec = pl.BlockSpec((1, 1, s), lambda h, j: (h, 0, 0))
    return pl.pallas_call(
        body, name=name, grid=(N_HEADS, nq),
        in_specs=[q_spec, kv_spec, kv_spec, q_spec, row_spec, row_spec],
        out_specs=[kv_spec, kv_spec],
        out_shape=[jax.ShapeDtypeStruct(q.shape, F32)] * 2,
        compiler_params=_cparams(("parallel", "arbitrary")),
    )(q, k, v, do, lse_row, delta_row)


def _rope(v, cos_t, sin_p, sin_m):
    return v * cos_t + pltpu.roll(v, QK_ROPE // 2, 1) * sin_p + pltpu.roll(v, LANES - QK_ROPE // 2, 1) * sin_m


def _rope_t(d, cos_t, sin_p, sin_m):
    return d * cos_t + pltpu.roll(d * sin_p, LANES - QK_ROPE // 2, 1) + pltpu.roll(d * sin_m, QK_ROPE // 2, 1)


def _att_prep(q_pad, kv2, kr, cos_t, sin_p, sin_m, *, name):
    w = N_HEADS * LANES

    def fn(qv, kvv, krv, c, sp, sm):
        kr_rot = _rope(krv, c, sp, sm)
        qs, ks = [], []
        for h in range(N_HEADS):
            blk = slice(h * LANES, (h + 1) * LANES)
            qs.append(_rope(qv[:, blk], c, sp, sm) * ATT_SCALE)
            ks.append(kvv[:, blk] + kr_rot)
        return jnp.concatenate(qs, axis=1), jnp.concatenate(ks, axis=1), kvv[:, w:]

    return _rowwise(fn, [q_pad, kv2, kr, cos_t, sin_p, sin_m], [],
                    [(w, BF16, "row"), (w, BF16, "row"), (w, BF16, "row")], name=name)


def _att_prep_bwd(dq, dk, cos_t, sin_p, sin_m, *, name):
    w = N_HEADS * LANES

    def fn(dqv, dkv, c, sp, sm):
        outs, dkr = [], None
        for h in range(N_HEADS):
            blk = slice(h * LANES, (h + 1) * LANES)
            outs.append(_rope_t(dqv[:, blk], c, sp, sm))
            dkr = dkv[:, blk] if dkr is None else dkr + dkv[:, blk]
        return jnp.concatenate(outs, axis=1), _rope_t(dkr, c, sp, sm)

    return _rowwise(fn, [dq, dk, cos_t, sin_p, sin_m], [], [(w, BF16, "row"), (LANES, F32, "row")], name=name)


_ANY = pl.BlockSpec(memory_space=pl.ANY)
_MESH = pl.DeviceIdType.MESH


def _mesh_pos():
    return lax.axis_index("x"), lax.axis_index("y"), lax.axis_index("c")


def _remote(src, dst, send_sem, recv_sem, dev):
    return pltpu.make_async_remote_copy(src_ref=src, dst_ref=dst, send_sem=send_sem, recv_sem=recv_sem,
                                        device_id=dev, device_id_type=_MESH)


def _other_chips(x, y):
    chips = [(1 - x, y), (x, 1 - y), (1 - x, 1 - y)]
    return chips, [2 * cx + cy for cx, cy in chips]


def _comm_call(body, ins, out_shapes, n_sems, *, name):
    return pl.pallas_call(
        body, name=name, in_specs=[_ANY] * len(ins), out_specs=[_ANY] * len(out_shapes), out_shape=out_shapes,
        scratch_shapes=[pltpu.SemaphoreType.DMA((k,)) for k in n_sems],
    )(*ins)


def _gather_halves(shards):
    n = len(shards)
    halves = [t.shape[0] // 2 for t in shards]

    def body(*refs):
        xs, outs = refs[:n], refs[n:2 * n]
        send_sems, recv_sems = refs[2 * n:]
        x, y, c = _mesh_pos()
        k = 2 * x + y
        sibling = (x, y, 1 - c)
        chips, ks = _other_chips(x, y)
        half = lambda w, hf: pl.ds(hf * halves[w], halves[w])
        first = [_remote(xs[w].at[half(w, c)], outs[w].at[k, half(w, c)], send_sems.at[6 * w + j], recv_sems.at[6 * w + j],
                         (*chips[j], c)) for w in range(n) for j in range(3)]
        for cp in first:
            cp.start()
        passed = []
        for j in range(3):
            for w in range(n):
                land = outs[w].at[ks[j], half(w, c)]
                _remote(land, land, send_sems.at[6 * w + j], recv_sems.at[6 * w + j], sibling).wait_recv()
                passed.append(_remote(land, land, send_sems.at[6 * w + 3 + j], recv_sems.at[6 * w + 3 + j], sibling))
                passed[-1].start()
        for j in range(3):
            for w in range(n):
                land = outs[w].at[ks[j], half(w, 1 - c)]
                _remote(land, land, send_sems.at[6 * w + 3 + j], recv_sems.at[6 * w + 3 + j], sibling).wait_recv()
        for cp in first + passed:
            cp.wait_send()

    shapes = [jax.ShapeDtypeStruct((4,) + t.shape, t.dtype) for t in shards]
    return _comm_call(body, shards, shapes, (6 * n, 6 * n), name="gather_halves")


_HBM = pl.BlockSpec(memory_space=pltpu.HBM)
_SEM = pl.BlockSpec(memory_space=pltpu.SEMAPHORE)
_EFFECT = pltpu.SideEffectType.DATAFLOW_SIDE_EFFECTING


def _push_start(blocks, *, scatter, name):
    n = len(blocks)

    def body(*refs):
        xs, lands = refs[:n], refs[n:2 * n]
        send_sems, recv_sems = refs[2 * n], refs[2 * n + 1]
        token = refs[-1]
        x, y, c = _mesh_pos()
        k = 2 * x + y
        chips, ks = _other_chips(x, y)
        for w in range(n):
            for j in range(3):
                src = xs[w].at[ks[j]] if scatter else xs[w]
                _remote(src, lands[w].at[k], send_sems.at[3 * w + j], recv_sems.at[3 * w + j], (*chips[j], c)).start()
        token[...] = jnp.zeros_like(token)

    hbm = lambda shape, dtype: pltpu.with_memory_space_constraint(lax.empty(shape, dtype), pltpu.HBM)
    ins = [pltpu.with_memory_space_constraint(t, pltpu.HBM) for t in blocks]
    ins += [hbm(t.shape if scatter else (4,) + t.shape, t.dtype) for t in blocks]
    out_shape = [pltpu.SemaphoreType.DMA((3 * n,)), pltpu.SemaphoreType.DMA((3 * n,))]
    out_shape += [pltpu.HBM(t.shape, t.dtype) for t in ins]
    out_shape += [jax.ShapeDtypeStruct((8, LANES), F32)]
    res = pl.pallas_call(
        body, name=name, out_shape=out_shape, in_specs=[_HBM] * (2 * n),
        out_specs=[_SEM, _SEM] + [_HBM] * (2 * n) + [pl.BlockSpec(memory_space=pltpu.VMEM)],
        input_output_aliases={i: 2 + i for i in range(2 * n)},
        compiler_params=pltpu.CompilerParams(has_side_effects=_EFFECT),
    )(*ins)
    return res[0], res[1], res[2:2 + n], res[2 + n:2 + 2 * n], res[-1]


def _push_wait(send_sems, recv_sems, blocks, lands, after, *, name):
    n = len(blocks)

    def body(*refs):
        lands_in = refs[n:2 * n]
        send_sems, recv_sems = refs[2 * n], refs[2 * n + 1]
        x, y, c = _mesh_pos()
        chips, ks = _other_chips(x, y)
        for w in range(n):
            for j in range(3):
                slot = lands_in[w].at[ks[j]]
                cp = _remote(slot, slot, send_sems.at[3 * w + j], recv_sems.at[3 * w + j], (*chips[j], c))
                cp.wait_send()
                cp.wait_recv()

    out_shape = [pltpu.HBM(t.shape, t.dtype) for t in list(blocks) + list(lands)]
    res = pl.pallas_call(
        body, name=name, out_shape=out_shape,
        in_specs=[_HBM] * (2 * n) + [_SEM, _SEM, pl.BlockSpec(memory_space=pl.ANY)], out_specs=[_HBM] * (2 * n),
        input_output_aliases={i: i for i in range(2 * n)},
        compiler_params=pltpu.CompilerParams(has_side_effects=_EFFECT),
    )(*blocks, *lands, send_sems, recv_sems, after)
    return res[:n], res[n:]


def _send_half(views, *, name):
    n = len(views)

    def body(*refs):
        vs, outs = refs[:n], refs[n:2 * n]
        send_sems, recv_sems = refs[2 * n:]
        x, y, c = _mesh_pos()
        cps = []
        for w in range(n):
            h = views[w].shape[1] // 2
            cps.append(_remote(vs[w].at[:, pl.ds((1 - c) * h, h), :], outs[w], send_sems.at[w], recv_sems.at[w],
                               (x, y, 1 - c)))
            cps[-1].start()
        for cp in cps:
            cp.wait()

    shapes = [jax.ShapeDtypeStruct((t.shape[0], t.shape[1] // 2, t.shape[2]), t.dtype) for t in views]
    return _comm_call(body, views, shapes, (n, n), name=name)


def _swap_with_sibling(mine, *, name):
    n = len(mine)

    def body(*refs):
        hs, outs = refs[:n], refs[n:2 * n]
        send_sems, recv_sems = refs[2 * n:]
        x, y, c = _mesh_pos()
        cps = [_remote(hs[w], outs[w], send_sems.at[w], recv_sems.at[w], (x, y, 1 - c)) for w in range(n)]
        for cp in cps:
            cp.start()
        for cp in cps:
            cp.wait()

    shapes = [jax.ShapeDtypeStruct(t.shape, t.dtype) for t in mine]
    return _comm_call(body, mine, shapes, (n, n), name=name)


def _gather_all(vec, *, name):
    r, w = vec.shape

    def body(v_ref, out_ref, send_sems, recv_sems):
        x, y, c = _mesh_pos()

        def slot(px, py, pc):
            return out_ref.at[4 * px + 2 * py + pc]

        peers = []
        for rel in range(1, 8):
            fx, fy, fc = (rel >> 2) & 1, (rel >> 1) & 1, rel & 1
            peers.append((x ^ fx, y ^ fy, c ^ fc))
        cps = [_remote(v_ref, slot(x, y, c), send_sems.at[j], recv_sems.at[j], peer) for j, peer in enumerate(peers)]
        for cp in cps:
            cp.start()
        for j, peer in enumerate(peers):
            _remote(slot(*peer), slot(*peer), send_sems.at[j], recv_sems.at[j], peer).wait_recv()
        for cp in cps:
            cp.wait_send()

    others = pl.pallas_call(
        body, name=name, in_specs=[_ANY], out_specs=_ANY,
        out_shape=jax.ShapeDtypeStruct((8, r, w), vec.dtype),
        scratch_shapes=[pltpu.SemaphoreType.DMA((7,)), pltpu.SemaphoreType.DMA((7,))],
    )(vec)
    me = 4 * lax.axis_index("x") + 2 * lax.axis_index("y") + lax.axis_index("c")
    return lax.dynamic_update_index_in_dim(others, vec, me, 0)


def _row_tile(rows, row_bytes):
    for tm in (1024, 512, 256, 128, 64, 32, 16):
        if rows % tm == 0 and tm * row_bytes <= ELEMENTWISE_BLOCK_BYTES:
            return tm
    return 16 if rows % 16 == 0 else rows


def _chip_sum_half(g, got, c, *, name):
    nb, r, w = g.shape
    half = r // 2
    tm = _row_tile(half, w * 4)
    per = half // tm

    def body(c_ref, g_ref, o_ref, out_ref):
        out_ref[...] = (g_ref[...] + o_ref[...]).astype(out_ref.dtype)

    return pl.pallas_call(
        body, name=name,
        grid_spec=pltpu.PrefetchScalarGridSpec(
            num_scalar_prefetch=1, grid=(nb, per),
            in_specs=[pl.BlockSpec((1, tm, w), lambda b, i, c_ref: (b, c_ref[0] * per + i, 0)),
                      pl.BlockSpec((1, tm, w), lambda b, i, c_ref: (b, i, 0))],
            out_specs=pl.BlockSpec((1, tm, w), lambda b, i, c_ref: (b, i, 0))),
        out_shape=jax.ShapeDtypeStruct((nb, half, w), BF16),
        compiler_params=_cparams(("parallel", "parallel")),
    )(jnp.reshape(c, (1,)).astype(jnp.int32), g, got)


def _sum_slots(stack, *, name):
    n, r, w = stack.shape
    tm = _row_tile(r, n * w * stack.dtype.itemsize)

    def body(s_ref, out_ref):
        acc = s_ref[0].astype(F32)
        for i in range(1, n):
            acc = acc + s_ref[i].astype(F32)
        out_ref[...] = acc

    return pl.pallas_call(
        body, name=name, grid=(r // tm,),
        in_specs=[pl.BlockSpec((n, tm, w), lambda i: (0, i, 0))],
        out_specs=pl.BlockSpec((tm, w), lambda i: (i, 0)),
        out_shape=jax.ShapeDtypeStruct((r, w), F32),
        compiler_params=_cparams(("parallel",)),
    )(stack)


def _adam_math(wv, gv, mv, vv):
    m_new = ADAM_B1 * mv + (1.0 - ADAM_B1) * gv
    v_new = ADAM_B2 * vv + (1.0 - ADAM_B2) * (gv * gv)
    m_hat = m_new / (1.0 - ADAM_B1 ** ADAM_STEP)
    v_hat = v_new / (1.0 - ADAM_B2 ** ADAM_STEP)
    delta = -ADAM_LR * (m_hat / (jnp.sqrt(v_hat) + ADAM_EPS) + ADAM_WD * wv)
    return delta, m_new, v_new


def _adamw(w, g, m, v, *, name):
    shape = w.shape
    cols = shape[-1]
    flat = lambda t: t.reshape(-1, cols)
    rows = flat(w).shape[0]
    tm = _pick(rows, (256, 128, 64, 32, 16, 8))
    outs = _rowwise(_adam_math, [flat(w), flat(g), flat(m), flat(v)], [], [(cols, F32, "row")] * 3, name=name, tm=tm)
    return tuple(o.reshape(shape) for o in outs)


def _adamw_slots(w, slots0, slots1, m, v, *, name):
    shape = w.shape
    cols = shape[-1]
    half = slots0.shape[2]
    v4 = lambda t: t.reshape(2, 2, half, cols)
    assert slots0.shape == slots1.shape == (2, 4, half, cols) and w.size == 4 * half * cols, (slots0.shape, shape)
    tm = _row_tile(half, cols * 4)

    def body(w_ref, s0_ref, s1_ref, m_ref, v_ref, g_ref, d_ref, mo_ref, vo_ref):
        first = pl.program_id(0) == 0
        g = None
        for i in range(4):
            part = jnp.where(first, s0_ref[0, i], s1_ref[0, i]).astype(F32)
            g = part if g is None else g + part
        delta, m_new, v_new = _adam_math(w_ref[0, 0], g, m_ref[0, 0], v_ref[0, 0])
        g_ref[0, 0], d_ref[0, 0], mo_ref[0, 0], vo_ref[0, 0] = g, delta, m_new, v_new

    blk = pl.BlockSpec((1, 1, tm, cols), lambda l, hf, i: (l, hf, i, 0))
    s0 = pl.BlockSpec((1, 4, tm, cols), lambda l, hf, i: (hf * (1 - l), 0, i * (1 - l), 0))
    s1 = pl.BlockSpec((1, 4, tm, cols), lambda l, hf, i: (hf * l, 0, i * l, 0))
    outs = pl.pallas_call(
        body, name=name, grid=(2, 2, half // tm),
        in_specs=[blk, s0, s1, blk, blk],
        out_specs=[blk] * 4, out_shape=[jax.ShapeDtypeStruct((2, 2, half, cols), F32)] * 4,
        compiler_params=_cparams(("arbitrary", "arbitrary", "arbitrary")),
    )(v4(w), slots0, slots1, v4(m), v4(v))
    return tuple(o.reshape(shape) for o in outs)


def _pad_blocks(w, axis, n_blocks, real, to=LANES, offset=0):
    axis = axis % w.ndim
    shp = w.shape
    w = w.reshape(shp[:axis] + (n_blocks, real) + shp[axis + 1:])
    pads = [(0, 0)] * w.ndim
    pads[axis + 1] = (offset, to - real - offset)
    w = jnp.pad(w, pads)
    return w.reshape(shp[:axis] + (n_blocks * to,) + shp[axis + 1:])


def _unpad_blocks(w, axis, n_blocks, real, to=LANES, offset=0):
    axis = axis % w.ndim
    shp = w.shape
    w = w.reshape(shp[:axis] + (n_blocks, to) + shp[axis + 1:])
    w = lax.slice_in_dim(w, offset, offset + real, axis=axis + 1)
    return w.reshape(shp[:axis] + (n_blocks * real,) + shp[axis + 1:])


def _block_diag(w):
    n, a, b = w.shape
    eye = jnp.eye(n, dtype=w.dtype)
    return (eye[:, None, :, None] * w[:, :, None, :]).reshape(n * a, n * b)


def _block_diag_t(d, n):
    a, b = d.shape[0] // n, d.shape[1] // n
    d = d.reshape(n, a, n, b)
    return jnp.stack([d[i, :, i, :] for i in range(n)])


_SPLITS = np.cumsum((0,) + SPLIT_SIZES)


def _w_in_groups(w_in):
    sl = lambda i: w_in[:, _SPLITS[i]:_SPLITS[i + 1]]
    xbc = sl(5)
    xbc_pad = jnp.concatenate([_pad_blocks(xbc[:, :MIX], 1, N_HEADS, HEAD),
                               _pad_blocks(xbc[:, MIX:MIX + 2 * HEAD], 1, 2, HEAD),
                               _pad_blocks(xbc[:, MIX + 2 * HEAD:], 1, 2, HEAD)], axis=1)
    return dict(
        cq=sl(0), ckv=sl(1), kr=_pad_blocks(sl(2), 1, 1, QK_ROPE, offset=HEAD), pool=sl(3),
        z=_pad_blocks(sl(4), 1, N_HEADS, HEAD), xbc=xbc_pad, dt=_pad_blocks(sl(6), 1, 1, N_HEADS),
        lru_g=sl(7), lru_x=sl(8), gates=sl(9))


def _w_in_fused(groups):
    parts, at = [], 0
    for name, off, width in IN_LAYOUT:
        assert groups[name].shape[1] == width and off >= at
        if off > at:
            parts.append(jnp.zeros((groups[name].shape[0], off - at), groups[name].dtype))
        parts.append(groups[name])
        at = off + width
    parts.append(jnp.zeros((parts[0].shape[0], IN_ALL_COLS - at), parts[0].dtype))
    return jnp.concatenate(parts, axis=1)


def _in_cols(arr, name):
    off, width = IN_OFFSETS[name]
    return _Cols(arr, off, width)


def _w_in_ungroup(d):
    xbc = d["xbc"]
    w = N_HEADS * LANES
    xbc_real = jnp.concatenate([_unpad_blocks(xbc[:, :w], 1, N_HEADS, HEAD),
                                _unpad_blocks(xbc[:, w:w + 2 * LANES], 1, 2, HEAD),
                                _unpad_blocks(xbc[:, w + 2 * LANES:], 1, 2, HEAD)], axis=1)
    return jnp.concatenate([d["cq"], d["ckv"], _unpad_blocks(d["kr"], 1, 1, QK_ROPE, offset=HEAD), d["pool"],
                            _unpad_blocks(d["z"], 1, N_HEADS, HEAD), xbc_real, _unpad_blocks(d["dt"], 1, 1, N_HEADS),
                            d["lru_g"], d["lru_x"], d["gates"]], axis=1)


def _pad_xbc_vec(v):
    return jnp.concatenate([_pad_blocks(v[..., :MIX], -1, N_HEADS, HEAD),
                            _pad_blocks(v[..., MIX:MIX + 2 * HEAD], -1, 2, HEAD),
                            _pad_blocks(v[..., MIX + 2 * HEAD:], -1, 2, HEAD)], axis=-1)


def _unpad_xbc_vec(v):
    w = N_HEADS * LANES
    return jnp.concatenate([_unpad_blocks(v[..., :w], -1, N_HEADS, HEAD),
                            _unpad_blocks(v[..., w:w + 2 * LANES], -1, 2, HEAD),
                            _unpad_blocks(v[..., w + 2 * LANES:], -1, 2, HEAD)], axis=-1)


def _layer_weights(p):
    q = dict(p)
    q["in_all"] = _w_in_fused(_w_in_groups(p["w_in"]))
    q["uq"] = _pad_blocks(p["w_uq"], 1, N_HEADS, HEAD + QK_ROPE)
    ukv = p["w_ukv"].reshape(KV_LORA, N_HEADS, 2 * HEAD)
    q["ukv"] = jnp.concatenate([_pad_blocks(ukv[:, :, :HEAD].reshape(KV_LORA, -1), 1, N_HEADS, HEAD),
                                _pad_blocks(ukv[:, :, HEAD:].reshape(KV_LORA, -1), 1, N_HEADS, HEAD)], axis=1)
    q["pool_bd"] = _block_diag(p["w_pool"])
    q["lru_bd"] = jnp.concatenate([_block_diag(p["lru_w_a"]), _block_diag(p["lru_w_i"])], axis=1)
    q["br"] = [_pad_blocks(p["w_branch"][0], 0, N_HEADS, HEAD), p["w_branch"][1],
               _pad_blocks(p["w_branch"][2], 0, N_HEADS, HEAD), p["w_branch"][3]]
    q["ssd_conv_w_pad"] = _pad_xbc_vec(p["ssd_conv_w"])
    q["ssd_conv_b_pad"] = _pad_xbc_vec(p["ssd_conv_b"])[None, :]
    q["ssd_norm_pad"] = _pad_blocks(p["ssd_norm"], 0, N_HEADS, HEAD)[None, :]
    return q


def _row(v):
    return v.reshape(1, -1)


def _scal3(v):
    return v.reshape(N_HEADS, 1, 1)


def _layer_fwd(x, p_emb, w, rope, tag):
    n = lambda s: f"{s}_{tag}"
    sv = {"x": x}
    h = _rms_fwd(x, _row(w["g_mix"]), name=n("rms_mix"))
    sv["h"] = h
    u_all = _mm(h, w["in_all"], name=n("in_proj"))
    u = {k: _in_cols(u_all, k) for k in IN_OFFSETS}
    sv["u"] = u

    cqn = _rms_fwd(u["cq"], _row(w["q_norm"]), name=n("rms_q"))
    ckvn = _rms_fwd(u["ckv"], _row(w["kv_norm"]), name=n("rms_kv"))
    q_pad = _mm(cqn, w["uq"], name=n("uq"))
    kv2 = _mm(ckvn, w["ukv"], name=n("ukv"))
    qc, kc, vc = _att_prep(q_pad, kv2, u["kr"], *rope, name=n("att_prep"))
    y_a, lse = _flash_fwd(qc, kc, vc, name=n("flash_fwd"))
    sv.update(cqn=cqn, ckvn=ckvn, qc=qc, kc=kc, vc=vc, y_a=y_a, lse=lse)

    pool_d = _pool_fwd(u["pool"], name=n("pool_fwd"))
    yb_pre, y_b = _mm(pool_d, w["pool_bd"], epilogue=lambda acc, sc: (acc, acc * sc),
                      rowvecs=[_row(w["pool_scale"])], out_dtypes=(F32, BF16), name=n("pool_mm"))
    sv.update(pool_d=pool_d, yb_pre=yb_pre, y_b=y_b)

    xbc_c = _conv_fwd(u["xbc"], w["ssd_conv_w_pad"], w["ssd_conv_b_pad"], silu=True, name=n("ssd_conv"))
    dt8 = lax.slice_in_dim(u_all, IN_OFFSETS["dt"][0], IN_OFFSETS["dt"][0] + N_HEADS, axis=1)
    dtcol = dt8.T[:, :, None]
    dtrow = dt8.T[:, None, :]
    ssd_par = (_scal3(w["ssd_dt_bias"]), _scal3(w["ssd_a_log"]), _scal3(w["ssd_d"]))
    y_ssd, states = _ssd_fwd(xbc_c, dtcol, dtrow, *ssd_par, name=n("ssd_fwd"))

    def ssd_post(yv, zv, gv):
        xh, _ = _rms_parts(yv * _silu(zv), MIX)
        return xh * gv

    y_c = _rowwise(ssd_post, [y_ssd, u["z"]], [w["ssd_norm_pad"]], [(N_HEADS * LANES, BF16, "row")], name=n("ssd_post"))
    sv.update(xbc_c=xbc_c, dtcol=dtcol, dtrow=dtrow, y_ssd=y_ssd, states=states, y_c=y_c)

    xc = _conv_fwd(u["lru_x"], w["lru_conv_w"], _row(w["lru_conv_b"]), silu=False, name=n("lru_conv"))
    pre = _mm(xc, w["lru_bd"], name=n("lru_mm"))
    lru_par = (_row(w["lru_lambda"]), _row(w["lru_b_a"]), _row(w["lru_b_i"]))
    y_d, h_lru = _lru_fwd(pre, xc, u["lru_g"], *lru_par, name=n("lru_fwd"))
    sv.update(xc=xc, pre=pre, h_lru=h_lru, y_d=y_d)

    ys = [y_a, y_b, y_c, y_d]
    merged, ybs = None, []
    for b in range(4):
        if merged is None:
            merged, yb = _mm(ys[b], w["br"][b], epilogue=lambda acc, gt: (_sigmoid(gt) * acc, acc),
                             tiles=[_Cols(u_all, b * D_MODEL, D_MODEL)], out_dtypes=(F32, F32), name=n(f"branch{b}"))
        else:
            merged, yb = _mm(ys[b], w["br"][b], epilogue=lambda acc, gt, mg: (mg + _sigmoid(gt) * acc, acc),
                             tiles=[_Cols(u_all, b * D_MODEL, D_MODEL), merged], out_dtypes=(F32, F32),
                             name=n(f"branch{b}"))
        ybs.append(yb)
    x1 = _mm(merged, w["w_out"], epilogue=lambda acc, xr: (acc + xr,), tiles=[x], name=n("out_proj"))
    sv.update(ybs=ybs, merged=merged, x1=x1)

    h2 = _rms_fwd(x1, _row(w["g_mlp"]), name=n("rms_mlp"))
    a_ff, f_ff = _mm(h2, w["w_ff1"], epilogue=lambda acc: (acc, jnp.square(jnp.maximum(acc, 0.0))),
                     out_dtypes=(F32, BF16), name=n("ff1"))
    x2 = _mm(f_ff, w["w_ff2"], epilogue=lambda acc, xr: (acc + xr,), tiles=[x1], name=n("ff2"))
    sv.update(h2=h2, a_ff=a_ff, f_ff=f_ff, x2=x2)

    h3 = _rms_fwd(x2, _row(w["g_ple"]), name=n("rms_ple"))
    e_ple = _mm(p_emb, w["w_ple"], name=n("ple_emb"))
    x3, gt_ple = _mm(h3, w["w_ple_gate"], epilogue=lambda acc, ev, xr: (xr + ev * _sigmoid(acc), _sigmoid(acc)),
                     tiles=[e_ple, x2], out_dtypes=(F32, F32), name=n("ple_gate"))
    sv.update(h3=h3, e_ple=e_ple, gt_ple=gt_ple, p_emb=p_emb)
    return x3, sv


def _layer_bwd(dx3, sv, w, rope, tag):
    n = lambda s: f"{s}_{tag}"
    gr = {}
    u = sv["u"]

    de, dpre = _rowwise(lambda d, gt, ev: (d * gt, d * ev * gt * (1.0 - gt)), [dx3, sv["gt_ple"], sv["e_ple"]], [],
                        [(D_MODEL, BF16, "row"), (D_MODEL, BF16, "row")], name=n("ple_bwd"))
    gr["w_ple"] = _mm(sv["p_emb"], de, ta=True, name=n("d_w_ple"))
    gr["w_ple_gate"] = _mm(sv["h3"], dpre, ta=True, name=n("d_w_ple_gate"))
    dh3 = _mm(dpre, w["w_ple_gate"], tb=True, out_dtypes=(BF16,), name=n("d_h3"))
    dx2, dg = _rms_bwd(sv["x2"], _row(w["g_ple"]), dh3, dx3, name=n("rms_ple_bwd"))
    gr["g_ple"] = dg[0]

    gr["w_ff2"] = _mm(sv["f_ff"], dx2, ta=True, name=n("d_w_ff2"))
    da = _mm(dx2, w["w_ff2"], tb=True, epilogue=lambda acc, av: (acc * 2.0 * jnp.maximum(av, 0.0),),
             tiles=[sv["a_ff"]], out_dtypes=(BF16,), name=n("d_a_ff"))
    gr["w_ff1"] = _mm(sv["h2"], da, ta=True, name=n("d_w_ff1"))
    dh2 = _mm(da, w["w_ff1"], tb=True, out_dtypes=(BF16,), name=n("d_h2"))
    dx1, dg = _rms_bwd(sv["x1"], _row(w["g_mlp"]), dh2, dx2, name=n("rms_mlp_bwd"))
    gr["g_mlp"] = dg[0]

    gr["w_out"] = _mm(sv["merged"], dx1, ta=True, name=n("d_w_out"))
    dmerged = _mm(dx1, w["w_out"], tb=True, name=n("d_merged"))

    def merge_bwd(dm, gts, y0, y1, y2, y3):
        dys, dgs = [], []
        for b, yb in enumerate((y0, y1, y2, y3)):
            sg = _sigmoid(gts[:, b * D_MODEL:(b + 1) * D_MODEL])
            dys.append(dm * sg)
            dgs.append(dm * yb * sg * (1.0 - sg))
        return (*dys, jnp.concatenate(dgs, axis=1))

    *dybs, dgates = _rowwise(merge_bwd, [dmerged, u["gates"]] + sv["ybs"], [],
                             [(D_MODEL, BF16, "row")] * 4 + [(4 * D_MODEL, BF16, "row")], name=n("merge_bwd"))
    ys = [sv["y_a"], sv["y_b"], sv["y_c"], sv["y_d"]]
    dwb = [_mm(ys[b], dybs[b], ta=True, name=n(f"d_w_branch{b}")) for b in range(4)]
    gr["w_branch"] = jnp.stack([_unpad_blocks(dwb[0], 0, N_HEADS, HEAD), dwb[1],
                                _unpad_blocks(dwb[2], 0, N_HEADS, HEAD), dwb[3]])
    dy_a = _mm(dybs[0], w["br"][0], tb=True, out_dtypes=(BF16,), name=n("d_y_a"))
    dy_b = _mm(dybs[1], w["br"][1], tb=True, name=n("d_y_b"))
    dy_c = _mm(dybs[2], w["br"][2], tb=True, name=n("d_y_c"))
    dy_d = _mm(dybs[3], w["br"][3], tb=True, name=n("d_y_d"))
    du = {"gates": dgates}

    lru_par = (_row(w["lru_lambda"]), _row(w["lru_b_a"]), _row(w["lru_b_i"]))
    dpa, dpi, dxc_direct, du["lru_g"], dlam, dba, dbi = _lru_bwd(
        sv["pre"], sv["xc"], u["lru_g"], *lru_par, sv["h_lru"], dy_d, name=n("lru_bwd"))
    dpre_lru = jnp.concatenate([dpa, dpi], axis=1)
    d_bd = _mm(sv["xc"], dpre_lru, ta=True, name=n("d_lru_w"))
    gr["lru_w_a"] = _block_diag_t(d_bd[:, :MIX], N_HEADS)
    gr["lru_w_i"] = _block_diag_t(d_bd[:, MIX:], N_HEADS)
    gr["lru_lambda"], gr["lru_b_a"], gr["lru_b_i"] = dlam[0], dba[0], dbi[0]
    dxc = _mm(dpre_lru, w["lru_bd"], tb=True, epilogue=lambda acc, t: (acc + t,), tiles=[dxc_direct], name=n("d_xc"))
    du["lru_x"], gr["lru_conv_w"], dcb = _conv_bwd(u["lru_x"], w["lru_conv_w"], _row(w["lru_conv_b"]), dxc,
                                                  silu=False, name=n("lru_conv_bwd"))
    gr["lru_conv_b"] = dcb[0]

    def ssd_post_bwd(dyc, yv, zv, gv):
        sz = _silu(zv)
        dyz, dgain = _rms_bwd_math(yv * sz, gv, dyc, MIX)
        return dyz * sz, dyz * yv * _silu_grad(zv), dgain

    dy_ssd, du["z"], dgain = _rowwise(ssd_post_bwd, [dy_c, sv["y_ssd"], u["z"]], [w["ssd_norm_pad"]],
                                      [(N_HEADS * LANES, F32, "row"), (N_HEADS * LANES, BF16, "row"),
                                       (N_HEADS * LANES, F32, "acc")], name=n("ssd_post_bwd"))
    gr["ssd_norm"] = _unpad_blocks(dgain[0], 0, N_HEADS, HEAD)
    ssd_par = (_scal3(w["ssd_dt_bias"]), _scal3(w["ssd_a_log"]), _scal3(w["ssd_d"]))
    dxs, dbg, dcg, ddt, dbias, dalog, dd = _ssd_bwd(sv["xbc_c"], sv["dtcol"], sv["dtrow"], *ssd_par, sv["states"],
                                                    dy_ssd, name=n("ssd_bwd"))
    s = dxs.shape[0]
    dxbc_c = jnp.concatenate([dxs, dbg, dcg], axis=1)
    gr["ssd_dt_bias"], gr["ssd_a_log"], gr["ssd_d"] = dbias[:, 0, 0], dalog[:, 0, 0], dd[:, 0, 0]
    du["xbc"], dcw, dcb = _conv_bwd(u["xbc"], w["ssd_conv_w_pad"], w["ssd_conv_b_pad"], dxbc_c, silu=True,
                                    name=n("ssd_conv_bwd"))
    gr["ssd_conv_w"], gr["ssd_conv_b"] = _unpad_xbc_vec(dcw), _unpad_xbc_vec(dcb[0])
    du["dt"] = jnp.pad(ddt[:, :, 0].T, ((0, 0), (0, LANES - N_HEADS)))

    dyb_pre, dscale = _rowwise(lambda d, yp, sc: (d * sc, _colsum(d * yp)), [dy_b, sv["yb_pre"]],
                               [_row(w["pool_scale"])], [(MIX, BF16, "row"), (MIX, F32, "acc")], name=n("pool_scale_bwd"))
    gr["pool_scale"] = dscale[0]
    gr["w_pool"] = _block_diag_t(_mm(sv["pool_d"], dyb_pre, ta=True, name=n("d_w_pool")), 4)
    dd_pool = _mm(dyb_pre, w["pool_bd"], tb=True, name=n("d_pool_d"))
    du["pool"] = _pool_bwd(dd_pool, name=n("pool_bwd"))

    dqc, delta = _flash_bwd_dq(sv["qc"], sv["kc"], sv["vc"], sv["y_a"], dy_a, sv["lse"], name=n("flash_dq"))
    to_row = lambda t: t.reshape(N_HEADS, 1, s)
    dkc, dvc = _flash_bwd_dkv(sv["qc"], sv["kc"], sv["vc"], dy_a, to_row(sv["lse"]), to_row(delta), name=n("flash_dkv"))
    dq_pad, du["kr"] = _att_prep_bwd(dqc, dkc, *rope, name=n("att_prep_bwd"))
    d_uq = _mm(sv["cqn"], dq_pad, ta=True, name=n("d_w_uq"))
    gr["w_uq"] = _unpad_blocks(d_uq, 1, N_HEADS, HEAD + QK_ROPE)
    dcqn = _mm(dq_pad, w["uq"], tb=True, out_dtypes=(BF16,), name=n("d_cqn"))
    du["cq"], dg = _rms_bwd(u["cq"], _row(w["q_norm"]), dcqn, name=n("rms_q_bwd"))
    gr["q_norm"] = dg[0]
    dkv2 = jnp.concatenate([dkc, dvc], axis=1).astype(BF16)
    d_ukv = _mm(sv["ckvn"], dkv2, ta=True, name=n("d_w_ukv"))
    wk = N_HEADS * LANES
    dk_real = _unpad_blocks(d_ukv[:, :wk], 1, N_HEADS, HEAD).reshape(KV_LORA, N_HEADS, HEAD)
    dv_real = _unpad_blocks(d_ukv[:, wk:], 1, N_HEADS, HEAD).reshape(KV_LORA, N_HEADS, HEAD)
    gr["w_ukv"] = jnp.concatenate([dk_real, dv_real], axis=2).reshape(KV_LORA, N_HEADS * 2 * HEAD)
    dckvn = _mm(dkv2, w["ukv"], tb=True, out_dtypes=(BF16,), name=n("d_ckvn"))
    du["ckv"], dg = _rms_bwd(u["ckv"], _row(w["kv_norm"]), dckvn, name=n("rms_kv_bwd"))
    gr["kv_norm"] = dg[0]

    du_all = _w_in_fused({k: v.astype(BF16) for k, v in du.items()})
    dw_all = _mm(sv["h"], du_all, ta=True, name=n("d_w_in"))
    gr["w_in"] = _w_in_ungroup({k: dw_all[:, off:off + width] for k, off, width in IN_LAYOUT})
    dh = _mm(du_all, w["in_all"], tb=True, name=n("d_h"))
    dx, dg = _rms_bwd(sv["x"], _row(w["g_mix"]), dh, dx1, name=n("rms_mix_bwd"))
    gr["g_mix"] = dg[0]
    return dx, gr


def _pack_rows(n_elems):
    per = PACK_W * PACK_ROWS
    return -(-n_elems // per) * PACK_ROWS


def _pack_flat(parts, dtype):
    flat = jnp.concatenate([p.reshape(-1).astype(dtype) for p in parts])
    rows = _pack_rows(flat.shape[0])
    return jnp.pad(flat, (0, rows * PACK_W - flat.shape[0])).reshape(rows, PACK_W)


def _unpack_flat(buf, shapes):
    lead = buf.shape[:-2]
    flat = buf.reshape(lead + (-1,))
    out, off = [], 0
    for shp in shapes:
        size = int(np.prod(shp))
        out.append(flat[..., off:off + size].reshape(lead + tuple(shp)))
        off += size
    return out


def _merge_shards(t, axis):
    return jnp.concatenate([t[i] for i in range(4)], axis=axis)


def _split_shards(t, axis):
    return jnp.stack(jnp.split(t, 4, axis=axis))


def _rope_tables(positions):
    inv = 1.0 / (ROPE_THETA ** (jnp.arange(0, QK_ROPE, 2, dtype=F32) / QK_ROPE))
    ang = positions.astype(F32)[:, None] * inv
    cos, sin = jnp.cos(ang), jnp.sin(ang)
    s = ang.shape[0]
    half = QK_ROPE // 2
    z = lambda n_: jnp.zeros((s, n_), F32)
    cos_t = jnp.concatenate([jnp.ones((s, HEAD), F32), cos, cos, jnp.ones((s, LANES - HEAD - QK_ROPE), F32)], axis=1)
    sin_p = jnp.concatenate([z(HEAD + half), sin, z(LANES - HEAD - QK_ROPE)], axis=1)
    sin_m = jnp.concatenate([z(HEAD), -sin, z(half + LANES - HEAD - QK_ROPE)], axis=1)
    return cos_t, sin_p, sin_m


def _loss_head(x, g, target, *, name):
    d = x.shape[1]

    def fn(xv, tv, gv):
        xh, r = _rms_parts(xv, d)
        y = xh * gv
        err = y - tv
        dy = err * (1.0 / d)
        dxh = dy * gv
        dx = r * (dxh - xh * (jnp.sum(dxh * xh, axis=-1, keepdims=True) * (1.0 / d)))
        return dx, _colsum(dy * xh), _colsum(err * err) * (0.5 / d)

    return _rowwise(fn, [x, target], [g], [(d, F32, "row"), (d, F32, "acc"), (d, F32, "acc")], name=name)


MATS = tuple((nm, ax) for nm, ax in BIG if nm not in CONV_SHARDED)


def _grad_view(g, ax_layer):
    if ax_layer == 0:
        return g.reshape(4, g.shape[0] // 4, g.shape[1])
    return g.reshape(1, -1, g.shape[-1])


def _reduce_start(grads_l, c_idx, tag):
    views = [_grad_view(grads_l[nm], ax - 1) for nm, ax in MATS]
    got = _send_half(views, name="send_half_" + tag)
    parts = []
    for (nm, ax), v, gt in zip(MATS, views, got):
        both = _chip_sum_half(v, gt, c_idx, name=f"chip_sum_{nm}_{tag}")
        parts.append(both if ax == 1 else _split_shards(both[0], 1))
    return _push_start(parts, scatter=True, name="push_grads_" + tag)


def _reduce_finish(state, after, k_chip, c_idx, tag):
    send_sems, recv_sems, parts, lands, _ = state
    parts, landed = _push_wait(send_sems, recv_sems, parts, lands, after, name="wait_grads_" + tag)
    mine = [lax.dynamic_update_index_in_dim(t, lax.dynamic_index_in_dim(p, k_chip, 0, keepdims=False), k_chip, 0)
            for t, p in zip(landed, parts)]
    other = _swap_with_sibling(mine, name="swap_halves_" + tag)
    return [jnp.where(c_idx == 0, jnp.stack([a, b]), jnp.stack([b, a])) for a, b in zip(mine, other)]


def _step(args):
    x = args["x"][0]
    c_idx = lax.axis_index("c")
    k_chip = 2 * lax.axis_index("x") + lax.axis_index("y")

    mats = MATS
    mine = [[args[nm][l].astype(BF16) for nm, _ in mats] for l in range(2)]
    gathered0 = _gather_halves(mine[0])
    convs = [(nm, ax) for nm, ax in BIG if nm in CONV_SHARDED]
    conv_all = _gather_all(_pack_flat([args[nm] for nm, _ in convs], F32), name="gather_conv_taps")[0::2]
    mine1, gathered0, conv_all = lax.optimization_barrier((mine[1], gathered0, conv_all))
    gathered0 = [lax.dynamic_update_index_in_dim(t, own, k_chip, 0) for t, own in zip(gathered0, mine[0])]
    send_sems, recv_sems, blocks1, lands1, token = _push_start(mine1, scatter=False, name="push_weights_l1")
    full_conv = {nm: _merge_shards(t, ax)
                 for (nm, ax), t in zip(convs, _unpack_flat(conv_all, [args[nm].shape for nm, _ in convs]))}
    rope = _rope_tables(args["positions"][0])

    def layer_weights(l, gathered):
        p = {nm: _merge_shards(t, ax - 1) for (nm, ax), t in zip(mats, gathered)}
        p.update({nm: full_conv[nm][l] for nm in CONV_SHARDED})
        p.update({nm: args[nm][l] for nm in SMALL if nm != "g_final"})
        return _layer_weights(p)

    layers = [layer_weights(0, gathered0), None]
    layers[0]["g_mix"] = layers[0]["g_mix"] + token[0, 0]
    x, sv0 = _layer_fwd(x, args["p"][0, 0], layers[0], rope, "l0")
    own1, landed1 = _push_wait(send_sems, recv_sems, blocks1, lands1, x, name="wait_weights_l1")
    gathered1 = [lax.dynamic_update_index_in_dim(t, own, k_chip, 0) for t, own in zip(landed1, own1)]
    layers[1] = layer_weights(1, gathered1)
    x, sv1 = _layer_fwd(x, args["p"][1, 0], layers[1], rope, "l1")
    saved = [sv0, sv1]

    dx, dg_final, loss_part = _loss_head(x, _row(args["g_final"]), args["loss_target"][0], name="loss_head")
    loss = lax.psum(jnp.sum(loss_part), ("x", "y", "c"))

    grads = [None, None]
    dx, grads[1] = _layer_bwd(dx, saved[1], layers[1], rope, "l1")
    reduce1 = _reduce_start(grads[1], c_idx, "l1")
    dx, grads[0] = _layer_bwd(dx + reduce1[4][0, 0], saved[0], layers[0], rope, "l0")
    reduce0 = _reduce_start(grads[0], c_idx, "l0")

    g_all = {nm: jnp.stack([grads[0][nm], grads[1][nm]]) for nm in SMALL + CONV_SHARDED if nm != "g_final"}
    g_all["g_final"] = dg_final[0]
    all_names = SMALL + CONV_SHARDED
    all_shapes = [g_all[nm].shape for nm in all_names]
    small_sum = _sum_slots(_gather_all(_pack_flat([g_all[nm] for nm in all_names], F32), name="gather_small_grads"),
                           name="sum_devices")
    g_red = dict(zip(all_names, _unpack_flat(small_sum, all_shapes)))
    for nm, ax in BIG:
        if nm in CONV_SHARDED:
            width = args[nm].shape[ax]
            g_red[nm] = lax.dynamic_slice_in_dim(g_red[nm], k_chip * width, width, axis=ax)
    small_shapes = [args[nm].shape for nm in SMALL]
    pack_small = lambda src: _pack_flat([src(nm) for nm in SMALL], F32)
    upd_small = _adamw(pack_small(lambda nm: args[nm]), pack_small(lambda nm: g_red[nm]),
                       pack_small(lambda nm: args["m_" + nm]), pack_small(lambda nm: args["v_" + nm]), name="adamw_small")
    upd = {nm: trip for nm, trip in zip(SMALL, zip(*[_unpack_flat(t, small_shapes) for t in upd_small]))}
    for nm in CONV_SHARDED:
        upd[nm] = _adamw(args[nm], g_red[nm], args["m_" + nm], args["v_" + nm], name="adamw_" + nm)

    slots = [_reduce_finish(reduce0, upd_small[0], k_chip, c_idx, "l0"),
             _reduce_finish(reduce1, dx, k_chip, c_idx, "l1")]
    for i, (nm, _) in enumerate(MATS):
        g_red[nm], *upd[nm] = _adamw_slots(args[nm], slots[0][i], slots[1][i], args["m_" + nm], args["v_" + nm],
                                           name="adamw_" + nm)

    outs = [loss, dx[None]]
    outs += [g_red[nm] for nm in WEIGHTS]
    for i in range(3):
        outs += [upd[nm][i] for nm in WEIGHTS]
    return tuple(outs)


_ARG_NAMES = ("x", "p", "positions") + WEIGHTS + ("loss_target",) + tuple("m_" + nm for nm in WEIGHTS) \
    + tuple("v_" + nm for nm in WEIGHTS)


def kernel(*arrays):
    assert len(arrays) == len(_ARG_NAMES), len(arrays)
    return _step(dict(zip(_ARG_NAMES, arrays)))
```

```python
import functools
import math

import jax
import jax.numpy as jnp
import numpy as np
from jax import lax
from jax.experimental import pallas as pl
from jax.experimental.pallas import tpu as pltpu

F32 = jnp.float32
BF16 = jnp.bfloat16
MXU_DTYPE = BF16
LANES = 128
VMEM_LIMIT = 56 * 1024 * 1024
MM_VMEM_BUDGET = 36 * 1024 * 1024
ELEMENTWISE_BLOCK_BYTES = 2 * 1024 * 1024

D_MODEL = 1024
N_HEADS = 8
HEAD = 64
QK_ROPE = 32
Q_LORA = 384
KV_LORA = 256
MIX = 512
SSD_CHUNK = 128
CONV_W = 4
POOL_WINDOWS = (2, 4, 8, 16)
LRU_C = 8.0
EPS = 1e-6
ROPE_THETA = 10000.0
ATT_SCALE = (HEAD + QK_ROPE) ** -0.5
SPLIT_SIZES = (Q_LORA, KV_LORA, QK_ROPE, MIX, MIX, 768, N_HEADS, MIX, MIX, 4 * D_MODEL)
IN_LAYOUT = (("gates", 0, 4096), ("z", 4096, 1024), ("pool", 5120, 512), ("lru_g", 5632, 512), ("lru_x", 6144, 512),
             ("cq", 6912, 384), ("ckv", 7424, 256), ("xbc", 7680, 1536), ("kr", 9216, 128), ("dt", 9344, 128))
IN_OFFSETS = {name: (off, width) for name, off, width in IN_LAYOUT}
IN_ALL_COLS = 9728

ADAM_LR, ADAM_B1, ADAM_B2, ADAM_EPS, ADAM_WD, ADAM_STEP = 0.001, 0.9, 0.999, 1e-08, 0.01, 10

BIG = (("w_in", 2), ("w_uq", 2), ("w_ukv", 2), ("ssd_conv_w", 2), ("lru_conv_w", 2), ("w_branch", 3),
       ("w_out", 1), ("w_ff1", 2), ("w_ff2", 1), ("w_ple_gate", 1), ("w_ple", 2))
SMALL = ("g_mix", "q_norm", "kv_norm", "w_pool", "pool_scale", "ssd_conv_b", "ssd_dt_bias", "ssd_a_log",
         "ssd_d", "ssd_norm", "lru_conv_b", "lru_w_a", "lru_b_a", "lru_w_i", "lru_b_i", "lru_lambda",
         "g_mlp", "g_ple", "g_final")
WEIGHTS = ("g_mix", "w_in", "q_norm", "w_uq", "kv_norm", "w_ukv", "w_pool", "pool_scale", "ssd_conv_w",
           "ssd_conv_b", "ssd_dt_bias", "ssd_a_log", "ssd_d", "ssd_norm", "lru_conv_w", "lru_conv_b", "lru_w_a",
           "lru_b_a", "lru_w_i", "lru_b_i", "lru_lambda", "w_branch", "w_out", "g_mlp", "w_ff1", "w_ff2", "g_ple",
           "w_ple_gate", "w_ple", "g_final")
CONV_SHARDED = ("ssd_conv_w", "lru_conv_w")
PACK_W = 1024
PACK_ROWS = 64


def _cparams(sem, vmem=VMEM_LIMIT):
    return pltpu.CompilerParams(dimension_semantics=sem, vmem_limit_bytes=vmem)


def _pick(n, cands):
    for c in cands:
        if n % c == 0:
            return c
    return n


class _Cols:
    def __init__(self, arr, off, width):
        self.arr, self.off, self.width = arr, off, width

    shape = property(lambda self: (self.arr.shape[0], self.width))
    dtype = property(lambda self: self.arr.dtype)


def _arr(x):
    return x.arr if isinstance(x, _Cols) else x


def _off(x, unit):
    off = x.off if isinstance(x, _Cols) else 0
    assert off % unit == 0, (off, unit)
    return off // unit


def _sigmoid(x):
    return 1.0 / (1.0 + jnp.exp(-x))


def _silu(x):
    return x * _sigmoid(x)


def _silu_grad(x):
    s = _sigmoid(x)
    return s * (1.0 + x * (1.0 - s))


def _softplus(x):
    e = jnp.exp(-jnp.abs(x))
    log1p_e = jnp.where(e < 1e-3, e * (1.0 - e * (0.5 - e * (1.0 / 3.0))), jnp.log(1.0 + e))
    return jnp.maximum(x, 0.0) + log1p_e


_GELU_C = math.sqrt(2.0 / math.pi)


def _gelu(x):
    t = jnp.tanh(_GELU_C * (x + 0.044715 * x * x * x))
    return 0.5 * x * (1.0 + t)


def _gelu_grad(x):
    t = jnp.tanh(_GELU_C * (x + 0.044715 * x * x * x))
    return 0.5 * (1.0 + t) + 0.5 * x * (1.0 - t * t) * _GELU_C * (1.0 + 3.0 * 0.044715 * x * x)


def _neg_expm1(x):
    series = -x * (1.0 + 0.5 * x * (1.0 + (1.0 / 3.0) * x * (1.0 + 0.25 * x)))
    return jnp.where(x > -0.05, series, 1.0 - jnp.exp(x))


def _shift_down(x, k, row):
    return jnp.where(row >= k, pltpu.roll(x, k, 0), 0.0)


def _shift_up(x, k, row):
    n = x.shape[0]
    return jnp.where(row < n - k, pltpu.roll(x, n - k, 0), 0.0)


def _cumsum_rows(x, row):
    d = 1
    while d < x.shape[0]:
        x = x + _shift_down(x, d, row)
        d *= 2
    return x


def _rev_cumsum_rows(x, row):
    d = 1
    while d < x.shape[0]:
        x = x + _shift_up(x, d, row)
        d *= 2
    return x


def _cumsum_lanes(x, col):
    d = 1
    while d < x.shape[1]:
        x = x + jnp.where(col >= d, pltpu.roll(x, d, 1), 0.0)
        d *= 2
    return x


def _dot(a, b, ta=False, tb=False):
    dn = (((0 if ta else 1,), (1 if tb else 0,)), ((), ()))
    return lax.dot_general(a.astype(MXU_DTYPE), b.astype(MXU_DTYPE), dn, preferred_element_type=F32)


def _mm_tiles(m, n, k, a_bytes, b_bytes, mn_bytes):
    best = None
    for tm in (1024, 512, 384, 256, 128):
        for tn in (1024, 512, 384, 256, 128):
            for tk in (2048, 1024, 512, 384, 256, 128):
                if m % tm or n % tn or k % tk:
                    continue
                vmem = 2 * (tm * tk * a_bytes + tk * tn * b_bytes) + 2 * tm * tn * mn_bytes + 4 * tm * tn
                vmem += 2 * (tm * tk + tk * tn)
                if vmem > MM_VMEM_BUDGET:
                    continue
                steps = (m // tm) * (n // tn) * (k // tk)
                key = (steps, vmem)
                if best is None or key < best[0]:
                    best = (key, (tm, tn, tk))
    assert best is not None, (m, n, k)
    return best[1]


def _mm(a, b, *, ta=False, tb=False, epilogue=None, tiles=(), rowvecs=(), out_dtypes=(F32,), name):
    m, k = (a.shape[1], a.shape[0]) if ta else a.shape
    n = b.shape[0] if tb else b.shape[1]
    assert (b.shape[1] if tb else b.shape[0]) == k, (a.shape, b.shape, ta, tb)
    mn_bytes = sum(t.dtype.itemsize for t in tiles) + sum(jnp.dtype(dt).itemsize for dt in out_dtypes)
    tm, tn, tk = _mm_tiles(m, n, k, a.dtype.itemsize, b.dtype.itemsize, mn_bytes)
    nk = k // tk
    nt, nr, no = len(tiles), len(rowvecs), len(out_dtypes)

    def body(*refs):
        a_ref, b_ref = refs[0], refs[1]
        tile_refs = refs[2:2 + nt]
        row_refs = refs[2 + nt:2 + nt + nr]
        out_refs = refs[2 + nt + nr:2 + nt + nr + no]
        acc_ref = refs[-1]
        kk = pl.program_id(2)

        @pl.when(kk == 0)
        def _():
            acc_ref[...] = jnp.zeros_like(acc_ref)

        acc_ref[...] += _dot(a_ref[...], b_ref[...], ta, tb)

        @pl.when(kk == nk - 1)
        def _():
            acc = acc_ref[...]
            if epilogue is None:
                outs = (acc,)
            else:
                outs = epilogue(acc, *[t[...] for t in tile_refs], *[r[...] for r in row_refs])
            for o_ref, o in zip(out_refs, outs):
                o_ref[...] = o.astype(o_ref.dtype)

    a_spec = pl.BlockSpec((tk, tm), lambda i, j, kk: (kk, i)) if ta else pl.BlockSpec((tm, tk), lambda i, j, kk: (i, kk))
    b_spec = pl.BlockSpec((tn, tk), lambda i, j, kk: (j, kk)) if tb else pl.BlockSpec((tk, tn), lambda i, j, kk: (kk, j))
    mn_spec = pl.BlockSpec((tm, tn), lambda i, j, kk: (i, j))
    row_spec = pl.BlockSpec((1, tn), lambda i, j, kk: (0, j))
    tile_specs = [pl.BlockSpec((tm, tn), lambda i, j, kk, ob=_off(t, tn): (i, j + ob)) for t in tiles]
    outs = pl.pallas_call(
        body, name=name,
        grid=(m // tm, n // tn, nk),
        in_specs=[a_spec, b_spec] + tile_specs + [row_spec] * nr,
        out_specs=[mn_spec] * no,
        out_shape=[jax.ShapeDtypeStruct((m, n), dt) for dt in out_dtypes],
        scratch_shapes=[pltpu.VMEM((tm, tn), F32)],
        compiler_params=_cparams(("parallel", "parallel", "arbitrary")),
    )(a, b, *[_arr(t) for t in tiles], *rowvecs)
    return outs[0] if no == 1 else tuple(outs)


def _branch_merge(ys, ws, u_all, *, name):
    s, d = ys[0].shape[0], ws[0].shape[1]
    tm, tn = _pick(s, (512, 256, 128)), _pick(d, (512, 256, 128))
    nb = len(ys)

    def body(*refs):
        y_refs, w_refs, g_refs = refs[:nb], refs[nb:2 * nb], refs[2 * nb:3 * nb]
        merged_ref, yb_refs = refs[3 * nb], refs[3 * nb + 1:]
        merged = None
        for y_ref, w_ref, g_ref, yb_ref in zip(y_refs, w_refs, g_refs, yb_refs):
            acc = _dot(y_ref[...], w_ref[...])
            yb_ref[...] = acc.astype(yb_ref.dtype)
            term = _sigmoid(g_ref[...]) * acc
            merged = term if merged is None else merged + term
        merged_ref[...] = merged

    mn = pl.BlockSpec((tm, tn), lambda i, j: (i, j))
    in_specs = [pl.BlockSpec((tm, y.shape[1]), lambda i, j: (i, 0)) for y in ys]
    in_specs += [pl.BlockSpec((w.shape[0], tn), lambda i, j: (0, j)) for w in ws]
    in_specs += [pl.BlockSpec((tm, tn), lambda i, j, ob=b * d // tn: (i, j + ob)) for b in range(nb)]
    return pl.pallas_call(
        body, name=name, grid=(s // tm, d // tn), in_specs=in_specs, out_specs=[mn] * (nb + 1),
        out_shape=[jax.ShapeDtypeStruct((s, d), F32)] + [jax.ShapeDtypeStruct((s, d), BF16)] * nb,
        compiler_params=_cparams(("parallel", "parallel")),
    )(*ys, *ws, *[u_all] * nb)


def _rowwise(fn, rows, fulls, outs, *, name, tm=None):
    r = rows[0].shape[0]
    if tm is None:
        widest = max([x.shape[1] for x in rows] + [o[0] for o in outs])
        tm = _pick(r, (max(8, min(512, (512 * 1024) // widest)), 256, 128, 64, 32, 16, 8))
    nrow, nfull, nout = len(rows), len(fulls), len(outs)

    def body(*refs):
        row_refs = refs[:nrow]
        full_refs = refs[nrow:nrow + nfull]
        out_refs = refs[nrow + nfull:]
        res = fn(*[x[...] for x in row_refs], *[x[...] for x in full_refs])
        if not isinstance(res, (tuple, list)):
            res = (res,)
        step = pl.program_id(0)
        for o_ref, o, spec in zip(out_refs, res, outs):
            if spec[2] == "row":
                o_ref[...] = o.astype(o_ref.dtype)
            else:
                @pl.when(step == 0)
                def _(o_ref=o_ref):
                    o_ref[...] = jnp.zeros_like(o_ref)
                o_ref[...] += o

    in_specs = [pl.BlockSpec((tm, x.shape[1]), lambda i, ob=_off(x, x.shape[1]): (i, ob)) for x in rows]
    in_specs += [pl.BlockSpec(x.shape, lambda i, nd=x.ndim: (0,) * nd) for x in fulls]
    out_specs, out_shape = [], []
    for c, dt, kind in outs:
        if kind == "row":
            out_specs.append(pl.BlockSpec((tm, c), lambda i: (i, 0)))
            out_shape.append(jax.ShapeDtypeStruct((r, c), dt))
        else:
            out_specs.append(pl.BlockSpec((1, c), lambda i: (0, 0)))
            out_shape.append(jax.ShapeDtypeStruct((1, c), F32))
    res = pl.pallas_call(
        body, name=name, grid=(r // tm,), in_specs=in_specs, out_specs=out_specs, out_shape=out_shape,
        compiler_params=_cparams(("arbitrary",)),
    )(*[_arr(x) for x in rows], *fulls)
    return res[0] if nout == 1 else tuple(res)


def _colsum(x):
    return jnp.sum(x, axis=0, keepdims=True)


def _rms_parts(x, n_real):
    r = lax.rsqrt(jnp.sum(x * x, axis=-1, keepdims=True) * (1.0 / n_real) + EPS)
    return x * r, r


def _rms_fwd(x, g, *, n_real=None, out_dtype=BF16, name):
    n_real = n_real or x.shape[1]

    def fn(xv, gv):
        xh, _ = _rms_parts(xv, n_real)
        return xh * gv

    return _rowwise(fn, [x], [g], [(x.shape[1], out_dtype, "row")], name=name)


def _rms_bwd_math(xv, gv, dh, n_real):
    xh, r = _rms_parts(xv, n_real)
    dxh = dh * gv
    dx = r * (dxh - xh * (jnp.sum(dxh * xh, axis=-1, keepdims=True) * (1.0 / n_real)))
    return dx, _colsum(dh * xh)


def _rms_bwd(x, g, dh, res=None, *, name):
    n = x.shape[1]
    if res is None:
        def fn(xv, dhv, gv):
            return _rms_bwd_math(xv, gv, dhv.astype(F32), n)
        rows = [x, dh]
    else:
        def fn(xv, dhv, rv, gv):
            dx, dg = _rms_bwd_math(xv, gv, dhv.astype(F32), n)
            return dx + rv, dg
        rows = [x, dh, res]
    return _rowwise(fn, rows, [g], [(n, F32, "row"), (n, F32, "acc")], name=name)


def _seq_call(body, ins, outs, n_blocks, *, name):
    in_specs, args = [], []
    for x, kind in ins:
        in_specs.append(pl.BlockSpec((x.shape[0], LANES), lambda j, ob=_off(x, LANES): (0, j + ob)))
        args.append(_arr(x))
    out_specs, out_shape = [], []
    for shape, dt in outs:
        out_specs.append(pl.BlockSpec((shape[0], LANES), lambda j: (0, j)))
        out_shape.append(jax.ShapeDtypeStruct(shape, dt))
    res = pl.pallas_call(body, name=name, grid=(n_blocks,), in_specs=in_specs, out_specs=out_specs,
                         out_shape=out_shape, compiler_params=_cparams(("parallel",)))(*args)
    return res[0] if len(outs) == 1 else tuple(res)


def _conv_pre(x, w, b, row):
    acc = x * w[CONV_W - 1:CONV_W, :] + b
    for k in range(CONV_W - 1):
        acc = acc + _shift_down(x, CONV_W - 1 - k, row) * w[k:k + 1, :]
    return acc


def _conv_fwd(x, w, b, *, silu, name):
    s, c = x.shape

    def body(x_ref, w_ref, b_ref, y_ref):
        xv = x_ref[...]
        row = lax.broadcasted_iota(jnp.int32, xv.shape, 0)
        pre = _conv_pre(xv, w_ref[...], b_ref[...], row)
        y_ref[...] = _silu(pre) if silu else pre

    return _seq_call(body, [(x, "seq"), (w, "par"), (b, "par")], [((s, c), F32)], c // LANES, name=name)


def _conv_bwd(x, w, b, dy, *, silu, name):
    s, c = x.shape

    def body(x_ref, w_ref, b_ref, dy_ref, dx_ref, dw_ref, db_ref):
        xv, wv, dv = x_ref[...], w_ref[...], dy_ref[...]
        row = lax.broadcasted_iota(jnp.int32, xv.shape, 0)
        if silu:
            dv = dv * _silu_grad(_conv_pre(xv, wv, b_ref[...], row))
        dx = dv * wv[CONV_W - 1:CONV_W, :]
        dws = [None] * CONV_W
        dws[CONV_W - 1] = _colsum(dv * xv)
        for k in range(CONV_W - 1):
            sh = CONV_W - 1 - k
            dx = dx + _shift_up(dv, sh, row) * wv[k:k + 1, :]
            dws[k] = _colsum(dv * _shift_down(xv, sh, row))
        dx_ref[...] = dx
        for k in range(CONV_W):
            dw_ref[k:k + 1, :] = dws[k]
        db_ref[...] = _colsum(dv)

    return _seq_call(body, [(x, "seq"), (w, "par"), (b, "par"), (dy, "seq")],
                     [((s, c), F32), ((CONV_W, c), F32), ((1, c), F32)], c // LANES, name=name)


def _pool_select(levels):
    g = pl.program_id(0)
    return jnp.where(g == 0, levels[0], jnp.where(g == 1, levels[1], jnp.where(g == 2, levels[2], levels[3])))


def _pool_count(row):
    g = pl.program_id(0)
    w = jnp.where(g == 0, POOL_WINDOWS[0], jnp.where(g == 1, POOL_WINDOWS[1],
                                                     jnp.where(g == 2, POOL_WINDOWS[2], POOL_WINDOWS[3])))
    return jnp.minimum(row + 1, w).astype(F32)


def _pool_fwd(u, *, name):
    def body(u_ref, d_ref):
        uv = u_ref[...]
        row = lax.broadcasted_iota(jnp.int32, uv.shape, 0)
        levels, cur, sh = [], uv, 1
        for _ in POOL_WINDOWS:
            cur = cur + _shift_down(cur, sh, row)
            levels.append(cur)
            sh *= 2
        d_ref[...] = _pool_select(levels) / _pool_count(row) - uv

    return _seq_call(body, [(u, "seq")], [(u.shape, F32)], u.shape[1] // LANES, name=name)


def _pool_bwd(dd, *, name):
    def body(dd_ref, du_ref):
        dv = dd_ref[...]
        row = lax.broadcasted_iota(jnp.int32, dv.shape, 0)
        levels, cur, sh = [], dv / _pool_count(row), 1
        for _ in POOL_WINDOWS:
            cur = cur + _shift_up(cur, sh, row)
            levels.append(cur)
            sh *= 2
        du_ref[...] = _pool_select(levels) - dv

    return _seq_call(body, [(dd, "seq")], [(dd.shape, F32)], dd.shape[1] // LANES, name=name)


def _lru_gates(pre_a, pre_i, xc, lam, b_a, b_i):
    r = _sigmoid(pre_a + b_a)
    i = _sigmoid(pre_i + b_i)
    sp = _softplus(-lam)
    log_a = -LRU_C * r * sp
    a = jnp.exp(log_a)
    mult = jnp.sqrt(_neg_expm1(2.0 * log_a))
    return r, i, sp, a, mult


def _lru_fwd(pre, xc, gate_in, lam, b_a, b_i, *, name):
    s, c = xc.shape
    nb = c // LANES

    def body(pa_ref, pi_ref, xc_ref, g_ref, lam_ref, ba_ref, bi_ref, y_ref, h_ref):
        xv = xc_ref[...]
        row = lax.broadcasted_iota(jnp.int32, xv.shape, 0)
        _, i, _, a, mult = _lru_gates(pa_ref[...], pi_ref[...], xv, lam_ref[...], ba_ref[...], bi_ref[...])
        h = xv * i * mult
        d = 1
        while d < s:
            h = h + a * _shift_down(h, d, row)
            a = a * jnp.where(row >= d, pltpu.roll(a, d, 0), 1.0)
            d *= 2
        h_ref[...] = h
        y_ref[...] = h * _gelu(g_ref[...])

    blk = lambda off: pl.BlockSpec((s, LANES), lambda j: (0, j + off))
    par = pl.BlockSpec((1, LANES), lambda j: (0, j))
    return pl.pallas_call(
        body, name=name, grid=(nb,),
        in_specs=[blk(0), blk(nb), blk(0), blk(_off(gate_in, LANES)), par, par, par],
        out_specs=[blk(0), blk(0)],
        out_shape=[jax.ShapeDtypeStruct((s, c), F32)] * 2,
        compiler_params=_cparams(("parallel",)),
    )(pre, pre, xc, _arr(gate_in), lam, b_a, b_i)


def _lru_bwd(pre, xc, gate_in, lam, b_a, b_i, h, dy, *, name):
    s, c = xc.shape
    nb = c // LANES

    def body(pa_ref, pi_ref, xc_ref, g_ref, lam_ref, ba_ref, bi_ref, h_ref, dy_ref,
             dpa_ref, dpi_ref, dxc_ref, dg_ref, dlam_ref, dba_ref, dbi_ref):
        xv, gv, hv, dv = xc_ref[...], g_ref[...], h_ref[...], dy_ref[...]
        row = lax.broadcasted_iota(jnp.int32, xv.shape, 0)
        r, i, sp, a, mult = _lru_gates(pa_ref[...], pi_ref[...], xv, lam_ref[...], ba_ref[...], bi_ref[...])
        dg_ref[...] = dv * hv * _gelu_grad(gv)
        dh = dv * _gelu(gv)
        an = jnp.where(row < s - 1, pltpu.roll(a, s - 1, 0), 0.0)
        d = 1
        while d < s:
            dh = dh + an * _shift_up(dh, d, row)
            an = an * jnp.where(row < s - d, pltpu.roll(an, s - d, 0), 1.0)
            d *= 2
        da = dh * _shift_down(hv, 1, row)
        dxc_ref[...] = dh * i * mult
        di = dh * xv * mult
        dmult = dh * xv * i
        dlog_a = (da - dmult * a / mult) * a
        dr = dlog_a * (-LRU_C) * sp
        dlam_ref[...] = _colsum(dlog_a * LRU_C * r * _sigmoid(-lam_ref[...]))
        dpa = dr * r * (1.0 - r)
        dpi = di * i * (1.0 - i)
        dpa_ref[...] = dpa
        dpi_ref[...] = dpi
        dba_ref[...] = _colsum(dpa)
        dbi_ref[...] = _colsum(dpi)

    blk = lambda off: pl.BlockSpec((s, LANES), lambda j: (0, j + off))
    par = pl.BlockSpec((1, LANES), lambda j: (0, j))
    sc = jax.ShapeDtypeStruct((s, c), F32)
    pc = jax.ShapeDtypeStruct((1, c), F32)
    dpa, dpi, dxc, dg, dlam, dba, dbi = pl.pallas_call(
        body, name=name, grid=(nb,),
        in_specs=[blk(0), blk(nb), blk(0), blk(_off(gate_in, LANES)), par, par, par, blk(0), blk(0)],
        out_specs=[blk(0), blk(0), blk(0), blk(0), par, par, par],
        out_shape=[sc, sc, sc, sc, pc, pc, pc],
        compiler_params=_cparams(("parallel",)),
    )(pre, pre, xc, _arr(gate_in), lam, b_a, b_i, h, dy)
    return dpa, dpi, dxc, dg, dlam, dba, dbi


GROUP_HEADS = 4


def _ssd_specs(nc, order):
    gw = GROUP_HEADS * LANES
    return dict(
        x=pl.BlockSpec((SSD_CHUNK, gw), lambda g, ci: (order(ci), g)),
        b=pl.BlockSpec((SSD_CHUNK, LANES), lambda g, ci: (order(ci), N_HEADS + g)),
        c=pl.BlockSpec((SSD_CHUNK, LANES), lambda g, ci: (order(ci), N_HEADS + 2 + g)),
        dtcol=pl.BlockSpec((GROUP_HEADS, SSD_CHUNK, 1), lambda g, ci: (g, order(ci), 0)),
        dtrow=pl.BlockSpec((GROUP_HEADS, 1, SSD_CHUNK), lambda g, ci: (g, 0, order(ci))),
        scal=pl.BlockSpec((GROUP_HEADS, 1, 1), lambda g, ci: (g, 0, 0)),
        state=pl.BlockSpec((GROUP_HEADS, 1, LANES, LANES), lambda g, ci: (g, order(ci), 0, 0)),
        group=pl.BlockSpec((SSD_CHUNK, LANES), lambda g, ci: (order(ci), g)),
        pacc=pl.BlockSpec((GROUP_HEADS, 1, LANES), lambda g, ci: (g, 0, 0)),
    )


def _ssd_chunk_terms(dtcol, dtrow, bias, a_log):
    shp = (SSD_CHUNK, SSD_CHUNK)
    row = lax.broadcasted_iota(jnp.int32, shp, 0)
    col = lax.broadcasted_iota(jnp.int32, shp, 1)
    a_head = -jnp.exp(a_log)
    dt_c = jnp.broadcast_to(_softplus(dtcol + bias), shp)
    dt_r = jnp.broadcast_to(_softplus(dtrow + bias), shp)
    cs_c = _cumsum_rows(dt_c * a_head, row)
    cs_r = _cumsum_lanes(dt_r * a_head, col)
    cs_last = jnp.sum(jnp.where(row == SSD_CHUNK - 1, cs_c, 0.0), axis=0, keepdims=True)
    return row, col, a_head, dt_c, cs_c, cs_r, cs_last


def _ssd_fwd(xbc, dtcol, dtrow, bias, a_log, dskip, *, name):
    s = xbc.shape[0]
    nc = s // SSD_CHUNK

    def body(x_ref, b_ref, c_ref, dtc_ref, dtr_ref, bias_ref, alog_ref, d_ref, y_ref, st_ref, state):
        ci = pl.program_id(1)

        @pl.when(ci == 0)
        def _():
            state[...] = jnp.zeros_like(state)

        bm, cm = b_ref[...], c_ref[...]
        cb = _dot(cm, bm, tb=True)
        bm_t = bm.T
        for r in range(GROUP_HEADS):
            lanes = slice(r * LANES, (r + 1) * LANES)
            xv = x_ref[:, lanes]
            row, col, _, dt_c, cs_c, cs_r, cs_last = _ssd_chunk_terms(dtc_ref[r], dtr_ref[r], bias_ref[r], alog_ref[r])
            g = cb * jnp.exp(jnp.where(col <= row, cs_c - cs_r, -jnp.inf))
            xdt = xv * dt_c
            st = state[r]
            st_ref[r, 0] = st
            y_ref[:, lanes] = _dot(g, xdt) + _dot(cm, st) * jnp.exp(cs_c) + xv * d_ref[r]
            state[r] = jnp.exp(cs_last) * st + _dot(bm_t, xdt * jnp.exp(cs_last - cs_c))

    sp = _ssd_specs(nc, lambda ci: ci)
    return pl.pallas_call(
        body, name=name, grid=(N_HEADS // GROUP_HEADS, nc),
        in_specs=[sp["x"], sp["b"], sp["c"], sp["dtcol"], sp["dtrow"], sp["scal"], sp["scal"], sp["scal"]],
        out_specs=[sp["x"], sp["state"]],
        out_shape=[jax.ShapeDtypeStruct((s, N_HEADS * LANES), F32),
                   jax.ShapeDtypeStruct((N_HEADS, nc, LANES, LANES), F32)],
        scratch_shapes=[pltpu.VMEM((GROUP_HEADS, LANES, LANES), F32)],
        compiler_params=_cparams(("parallel", "arbitrary")),
    )(xbc, xbc, xbc, dtcol, dtrow, bias, a_log, dskip)


def _ssd_bwd(xbc, dtcol, dtrow, bias, a_log, dskip, states, dy, *, name):
    s = xbc.shape[0]
    nc = s // SSD_CHUNK

    def body(x_ref, b_ref, c_ref, dtc_ref, dtr_ref, bias_ref, alog_ref, d_ref, st_ref, dy_ref,
             dx_ref, db_ref, dc_ref, ddt_ref, dbias_ref, dalog_ref, dd_ref, dstate):
        ci = pl.program_id(1)

        @pl.when(ci == 0)
        def _():
            dstate[...] = jnp.zeros_like(dstate)
            dbias_ref[...] = jnp.zeros_like(dbias_ref)
            dalog_ref[...] = jnp.zeros_like(dalog_ref)
            dd_ref[...] = jnp.zeros_like(dd_ref)

        bm, cm = b_ref[...], c_ref[...]
        cb = _dot(cm, bm, tb=True)
        cb_t = _dot(bm, cm, tb=True)
        cm_t = cm.T
        rowsum = lambda v: jnp.sum(v, axis=1, keepdims=True)
        tot = lambda v: jnp.broadcast_to(jnp.sum(v, axis=0, keepdims=True), (1, LANES))
        dbm_sum, dcm_sum = None, None
        for r in range(GROUP_HEADS):
            lanes = slice(r * LANES, (r + 1) * LANES)
            xv, dyv, st = x_ref[:, lanes], dy_ref[:, lanes], st_ref[r, 0]
            dtraw_c, bias = dtc_ref[r], bias_ref[r]
            row, col, a_head, dt_c, cs_c, cs_r, cs_last = _ssd_chunk_terms(dtraw_c, dtr_ref[r], bias, alog_ref[r])
            lmat = jnp.exp(jnp.where(col <= row, cs_c - cs_r, -jnp.inf))
            lmat_t = jnp.exp(jnp.where(row <= col, cs_r - cs_c, -jnp.inf))
            g, g_t = cb * lmat, cb_t * lmat_t
            xdt = xv * dt_c
            e_c = jnp.exp(cs_c)
            f_c = jnp.exp(cs_last - cs_c)
            e_last = jnp.exp(cs_last)
            w = xdt * f_c
            dst = dstate[r]

            dg = _dot(dyv, xdt, tb=True)
            dg_t = _dot(xdt, dyv, tb=True)
            dxdt = _dot(g_t, dyv)
            dcs = rowsum(dg * g) - rowsum(dg_t * g_t)
            dcm = _dot(dg * lmat, bm)
            dbm = _dot(dg_t * lmat_t, cm)
            z = _dot(cm, st)
            dz = dyv * e_c
            dcs = dcs + rowsum(dz * z)
            dcm = dcm + _dot(dz, st, tb=True)
            dstate[r] = _dot(cm_t, dz) + e_last * dst
            dcs_last = jnp.sum(rowsum(dst * st), axis=0, keepdims=True) * jnp.max(e_last, axis=1, keepdims=True)
            dbm = dbm + _dot(w, dst, tb=True)
            dw = _dot(bm, dst)
            dxdt = dxdt + dw * f_c
            q = rowsum(dw * w)
            dcs = dcs - q
            dcs_last = dcs_last + jnp.sum(q, axis=0, keepdims=True)
            dx_ref[:, lanes] = dxdt * dt_c + dyv * d_ref[r]
            ddt = rowsum(dxdt * xv)
            dcs_full = jnp.broadcast_to(dcs, (SSD_CHUNK, SSD_CHUNK)) + jnp.where(row == SSD_CHUNK - 1, dcs_last, 0.0)
            da = jnp.max(_rev_cumsum_rows(dcs_full, row), axis=1, keepdims=True)
            dt_col = jnp.max(dt_c, axis=1, keepdims=True)
            draw = (ddt + da * a_head) * _sigmoid(dtraw_c + bias)
            ddt_ref[r] = draw
            dbias_ref[r] += tot(draw)
            dalog_ref[r] += tot(da * dt_col) * a_head
            dd_ref[r] += tot(rowsum(dyv * xv))
            dbm_sum = dbm if dbm_sum is None else dbm_sum + dbm
            dcm_sum = dcm if dcm_sum is None else dcm_sum + dcm
        db_ref[...] = dbm_sum
        dc_ref[...] = dcm_sum

    sp = _ssd_specs(nc, lambda ci: nc - 1 - ci)
    return pl.pallas_call(
        body, name=name, grid=(N_HEADS // GROUP_HEADS, nc),
        in_specs=[sp["x"], sp["b"], sp["c"], sp["dtcol"], sp["dtrow"], sp["scal"], sp["scal"], sp["scal"],
                  sp["state"], sp["x"]],
        out_specs=[sp["x"], sp["group"], sp["group"], sp["dtcol"], sp["pacc"], sp["pacc"], sp["pacc"]],
        out_shape=[jax.ShapeDtypeStruct((s, N_HEADS * LANES), F32),
                   jax.ShapeDtypeStruct((s, 2 * LANES), F32),
                   jax.ShapeDtypeStruct((s, 2 * LANES), F32),
                   jax.ShapeDtypeStruct((N_HEADS, s, 1), F32),
                   jax.ShapeDtypeStruct((N_HEADS, 1, LANES), F32),
                   jax.ShapeDtypeStruct((N_HEADS, 1, LANES), F32),
                   jax.ShapeDtypeStruct((N_HEADS, 1, LANES), F32)],
        scratch_shapes=[pltpu.VMEM((GROUP_HEADS, LANES, LANES), F32)],
        compiler_params=_cparams(("parallel", "arbitrary")),
    )(xbc, xbc, xbc, dtcol, dtrow, bias, a_log, dskip, states, dy)


def _att_tile(s):
    return _pick(s, (512, 256, 128))


def _tri(t, transposed=False):
    r = lax.broadcasted_iota(jnp.int32, (t, t), 0)
    c = lax.broadcasted_iota(jnp.int32, (t, t), 1)
    return (r <= c) if transposed else (c <= r)


def _rows_at(ref, blk, t):
    return ref[pl.ds(pl.multiple_of(blk * t, t), t), :]


def _flash_fwd(q, k, v, *, name):
    s = q.shape[0]
    t = _att_tile(s)
    nq = s // t

    def body(q_ref, k_ref, v_ref, o_ref, lse_ref):
        i = pl.program_id(1)
        qv = q_ref[...]

        def step(j, carry, diagonal):
            m_old, l_old, acc = carry
            sc = _dot(qv, _rows_at(k_ref, j, t), tb=True)
            if diagonal:
                sc = jnp.where(_tri(t), sc, -jnp.inf)
            m_new = jnp.maximum(m_old, jnp.max(sc, axis=1, keepdims=True))
            alpha = jnp.exp(m_old - m_new)
            p = jnp.exp(sc - m_new)
            return (m_new, alpha * l_old + jnp.sum(p, axis=1, keepdims=True),
                    alpha * acc + _dot(p, _rows_at(v_ref, j, t)))

        init = (jnp.full((t, 1), -jnp.inf, F32), jnp.zeros((t, 1), F32), jnp.zeros((t, LANES), F32))
        carry = lax.fori_loop(0, i, lambda j, c: step(j, c, False), init)
        m_fin, l_fin, acc = step(i, carry, True)
        o_ref[...] = (acc / l_fin).astype(o_ref.dtype)
        lse_ref[0] = m_fin + jnp.log(l_fin)

    q_spec = pl.BlockSpec((t, LANES), lambda h, i: (i, h))
    kv_spec = pl.BlockSpec((s, LANES), lambda h, i: (0, h))
    return pl.pallas_call(
        body, name=name, grid=(N_HEADS, nq),
        in_specs=[q_spec, kv_spec, kv_spec],
        out_specs=[q_spec, pl.BlockSpec((1, t, 1), lambda h, i: (h, i, 0))],
        out_shape=[jax.ShapeDtypeStruct(q.shape, BF16), jax.ShapeDtypeStruct((N_HEADS, s, 1), F32)],
        compiler_params=_cparams(("parallel", "arbitrary")),
    )(q, k, v)


def _att_delta(o, do, *, name):
    s = o.shape[0]
    t = _att_tile(s)

    def body(o_ref, do_ref, dl_ref):
        dl_ref[0] = jnp.sum(do_ref[...].astype(F32) * o_ref[...].astype(F32), axis=1, keepdims=True)

    blk = pl.BlockSpec((t, LANES), lambda h, i: (i, h))
    return pl.pallas_call(
        body, name=name, grid=(N_HEADS, s // t), in_specs=[blk, blk],
        out_specs=pl.BlockSpec((1, t, 1), lambda h, i: (h, i, 0)),
        out_shape=jax.ShapeDtypeStruct((N_HEADS, s, 1), F32),
        compiler_params=_cparams(("parallel", "parallel")),
    )(o, do)


def _flash_bwd(q, k, v, do, lse_row, delta_row, *, name):
    s = q.shape[0]
    t = _att_tile(s)
    nq = s // t

    def body(q_ref, k_ref, v_ref, do_ref, lse_ref, dl_ref, dq_ref, dk_ref, dv_ref):
        j = pl.program_id(1)
        kv, vv = k_ref[...], v_ref[...]

        @pl.when(j == 0)
        def _():
            dq_ref[...] = jnp.zeros_like(dq_ref)

        def step(i, carry, diagonal):
            dk, dv = carry
            rows = pl.ds(pl.multiple_of(i * t, t), t)
            qi, doi = q_ref[rows, :], do_ref[rows, :]
            p_t = jnp.exp(_dot(kv, qi, tb=True) - lse_ref[0, :, rows])
            if diagonal:
                p_t = jnp.where(_tri(t, transposed=True), p_t, 0.0)
            ds_t = (p_t * (_dot(vv, doi, tb=True) - dl_ref[0, :, rows])).astype(MXU_DTYPE)
            dq_ref[rows, :] += _dot(ds_t, kv, ta=True)
            return dk + _dot(ds_t, qi), dv + _dot(p_t, doi)

        zero = jnp.zeros((t, LANES), F32)
        carry = step(j, (zero, zero), True)
        dk, dv = lax.fori_loop(j + 1, nq, lambda i, c: step(i, c, False), carry)
        dk_ref[...] = dk
        dv_ref[...] = dv

        @pl.when(j == nq - 1)
        def _():
            dq_ref[...] = dq_ref[...] * ATT_SCALE

    q_spec = pl.BlockSpec((s, LANES), lambda h, j: (0, h))
    kv_spec = pl.BlockSpec((t, LANES), lambda h, j: (j, h))
    row_spec = pl.BlockSpec((1, 1, s), lambda h, j: (h, 0, 0))
    return pl.pallas_call(
        body, name=name, grid=(N_HEADS, nq),
        in_specs=[q_spec, kv_spec, kv_spec, q_spec, row_spec, row_spec],
        out_specs=[q_spec, kv_spec, kv_spec],
        out_shape=[jax.ShapeDtypeStruct(q.shape, F32)] * 3,
        compiler_params=_cparams(("parallel", "arbitrary")),
    )(q, k, v, do, lse_row, delta_row)


def _rope(v, cos_t, sin_p, sin_m):
    return v * cos_t + pltpu.roll(v, QK_ROPE // 2, 1) * sin_p + pltpu.roll(v, LANES - QK_ROPE // 2, 1) * sin_m


def _rope_t(d, cos_t, sin_p, sin_m):
    return d * cos_t + pltpu.roll(d * sin_p, LANES - QK_ROPE // 2, 1) + pltpu.roll(d * sin_m, QK_ROPE // 2, 1)


def _att_prep(q_pad, kv2, kr, cos_t, sin_p, sin_m, *, name):
    w = N_HEADS * LANES

    def fn(qv, kvv, krv, c, sp, sm):
        kr_rot = _rope(krv, c, sp, sm)
        qs, ks = [], []
        for h in range(N_HEADS):
            blk = slice(h * LANES, (h + 1) * LANES)
            qs.append(_rope(qv[:, blk], c, sp, sm) * ATT_SCALE)
            ks.append(kvv[:, blk] + kr_rot)
        return jnp.concatenate(qs, axis=1), jnp.concatenate(ks, axis=1), kvv[:, w:]

    return _rowwise(fn, [q_pad, kv2, kr, cos_t, sin_p, sin_m], [],
                    [(w, BF16, "row"), (w, BF16, "row"), (w, BF16, "row")], name=name)


def _att_prep_bwd(dq, dk, cos_t, sin_p, sin_m, *, name):
    w = N_HEADS * LANES

    def fn(dqv, dkv, c, sp, sm):
        outs, dkr = [], None
        for h in range(N_HEADS):
            blk = slice(h * LANES, (h + 1) * LANES)
            outs.append(_rope_t(dqv[:, blk], c, sp, sm))
            dkr = dkv[:, blk] if dkr is None else dkr + dkv[:, blk]
        return jnp.concatenate(outs, axis=1), _rope_t(dkr, c, sp, sm)

    return _rowwise(fn, [dq, dk, cos_t, sin_p, sin_m], [], [(w, BF16, "row"), (LANES, F32, "row")], name=name)


_ANY = pl.BlockSpec(memory_space=pl.ANY)
_MESH = pl.DeviceIdType.MESH


def _mesh_pos():
    return lax.axis_index("x"), lax.axis_index("y"), lax.axis_index("c")


def _remote(src, dst, send_sem, recv_sem, dev):
    return pltpu.make_async_remote_copy(src_ref=src, dst_ref=dst, send_sem=send_sem, recv_sem=recv_sem,
                                        device_id=dev, device_id_type=_MESH)


def _other_chips(x, y):
    chips = [(1 - x, y), (x, 1 - y), (1 - x, 1 - y)]
    return chips, [2 * cx + cy for cx, cy in chips]


def _comm_call(body, ins, out_shapes, n_sems, *, name):
    return pl.pallas_call(
        body, name=name, in_specs=[_ANY] * len(ins), out_specs=[_ANY] * len(out_shapes), out_shape=out_shapes,
        scratch_shapes=[pltpu.SemaphoreType.DMA((k,)) for k in n_sems],
    )(*ins)


def _gather_halves(shards):
    n = len(shards)
    halves = [t.shape[0] // 2 for t in shards]

    def body(*refs):
        xs, outs = refs[:n], refs[n:2 * n]
        send_sems, recv_sems = refs[2 * n:]
        x, y, c = _mesh_pos()
        k = 2 * x + y
        sibling = (x, y, 1 - c)
        chips, ks = _other_chips(x, y)
        half = lambda w, hf: pl.ds(hf * halves[w], halves[w])
        first = [_remote(xs[w].at[half(w, c)], outs[w].at[k, half(w, c)], send_sems.at[6 * w + j], recv_sems.at[6 * w + j],
                         (*chips[j], c)) for w in range(n) for j in range(3)]
        for cp in first:
            cp.start()
        passed = []
        for j in range(3):
            for w in range(n):
                land = outs[w].at[ks[j], half(w, c)]
                _remote(land, land, send_sems.at[6 * w + j], recv_sems.at[6 * w + j], sibling).wait_recv()
                passed.append(_remote(land, land, send_sems.at[6 * w + 3 + j], recv_sems.at[6 * w + 3 + j], sibling))
                passed[-1].start()
        for j in range(3):
            for w in range(n):
                land = outs[w].at[ks[j], half(w, 1 - c)]
                _remote(land, land, send_sems.at[6 * w + 3 + j], recv_sems.at[6 * w + 3 + j], sibling).wait_recv()
        for cp in first + passed:
            cp.wait_send()

    shapes = [jax.ShapeDtypeStruct((4,) + t.shape, t.dtype) for t in shards]
    return _comm_call(body, shards, shapes, (6 * n, 6 * n), name="gather_halves")


_HBM = pl.BlockSpec(memory_space=pltpu.HBM)
_SEM = pl.BlockSpec(memory_space=pltpu.SEMAPHORE)
_EFFECT = pltpu.SideEffectType.DATAFLOW_SIDE_EFFECTING


def _push_start(blocks, *, scatter, name):
    n = len(blocks)

    def body(*refs):
        xs, lands = refs[:n], refs[n:2 * n]
        send_sems, recv_sems = refs[2 * n], refs[2 * n + 1]
        token = refs[-1]
        x, y, c = _mesh_pos()
        k = 2 * x + y
        chips, ks = _other_chips(x, y)
        for w in range(n):
            for j in range(3):
                src = xs[w].at[ks[j]] if scatter else xs[w]
                _remote(src, lands[w].at[k], send_sems.at[3 * w + j], recv_sems.at[3 * w + j], (*chips[j], c)).start()
        token[...] = jnp.zeros_like(token)

    hbm = lambda shape, dtype: pltpu.with_memory_space_constraint(lax.empty(shape, dtype), pltpu.HBM)
    ins = [pltpu.with_memory_space_constraint(t, pltpu.HBM) for t in blocks]
    ins += [hbm(t.shape if scatter else (4,) + t.shape, t.dtype) for t in blocks]
    out_shape = [pltpu.SemaphoreType.DMA((3 * n,)), pltpu.SemaphoreType.DMA((3 * n,))]
    out_shape += [pltpu.HBM(t.shape, t.dtype) for t in ins]
    out_shape += [jax.ShapeDtypeStruct((8, LANES), F32)]
    res = pl.pallas_call(
        body, name=name, out_shape=out_shape, in_specs=[_HBM] * (2 * n),
        out_specs=[_SEM, _SEM] + [_HBM] * (2 * n) + [pl.BlockSpec(memory_space=pltpu.VMEM)],
        input_output_aliases={i: 2 + i for i in range(2 * n)},
        compiler_params=pltpu.CompilerParams(has_side_effects=_EFFECT),
    )(*ins)
    return res[0], res[1], res[2:2 + n], res[2 + n:2 + 2 * n], res[-1]


def _push_wait(send_sems, recv_sems, blocks, lands, after, *, name):
    n = len(blocks)

    def body(*refs):
        lands_in = refs[n:2 * n]
        send_sems, recv_sems = refs[2 * n], refs[2 * n + 1]
        x, y, c = _mesh_pos()
        chips, ks = _other_chips(x, y)
        for w in range(n):
            for j in range(3):
                slot = lands_in[w].at[ks[j]]
                cp = _remote(slot, slot, send_sems.at[3 * w + j], recv_sems.at[3 * w + j], (*chips[j], c))
                cp.wait_send()
                cp.wait_recv()

    out_shape = [pltpu.HBM(t.shape, t.dtype) for t in list(blocks) + list(lands)]
    res = pl.pallas_call(
        body, name=name, out_shape=out_shape,
        in_specs=[_HBM] * (2 * n) + [_SEM, _SEM, pl.BlockSpec(memory_space=pl.ANY)], out_specs=[_HBM] * (2 * n),
        input_output_aliases={i: i for i in range(2 * n)},
        compiler_params=pltpu.CompilerParams(has_side_effects=_EFFECT),
    )(*blocks, *lands, send_sems, recv_sems, after)
    return res[:n], res[n:]


def _send_half(views, *, name):
    n = len(views)

    def body(*refs):
        vs, outs = refs[:n], refs[n:2 * n]
        send_sems, recv_sems = refs[2 * n:]
        x, y, c = _mesh_pos()
        cps = []
        for w in range(n):
            h = views[w].shape[1] // 2
            cps.append(_remote(vs[w].at[:, pl.ds((1 - c) * h, h), :], outs[w], send_sems.at[w], recv_sems.at[w],
                               (x, y, 1 - c)))
            cps[-1].start()
        for cp in cps:
            cp.wait()

    shapes = [jax.ShapeDtypeStruct((t.shape[0], t.shape[1] // 2, t.shape[2]), t.dtype) for t in views]
    return _comm_call(body, views, shapes, (n, n), name=name)


def _swap_with_sibling(mine, *, name):
    n = len(mine)

    def body(*refs):
        hs, outs = refs[:n], refs[n:2 * n]
        send_sems, recv_sems = refs[2 * n:]
        x, y, c = _mesh_pos()
        cps = [_remote(hs[w], outs[w], send_sems.at[w], recv_sems.at[w], (x, y, 1 - c)) for w in range(n)]
        for cp in cps:
            cp.start()
        for cp in cps:
            cp.wait()

    shapes = [jax.ShapeDtypeStruct(t.shape, t.dtype) for t in mine]
    return _comm_call(body, mine, shapes, (n, n), name=name)


def _gather_all(vec, *, name):
    r, w = vec.shape

    def body(v_ref, out_ref, send_sems, recv_sems):
        x, y, c = _mesh_pos()

        def slot(px, py, pc):
            return out_ref.at[4 * px + 2 * py + pc]

        peers = []
        for rel in range(1, 8):
            fx, fy, fc = (rel >> 2) & 1, (rel >> 1) & 1, rel & 1
            peers.append((x ^ fx, y ^ fy, c ^ fc))
        cps = [_remote(v_ref, slot(x, y, c), send_sems.at[j], recv_sems.at[j], peer) for j, peer in enumerate(peers)]
        for cp in cps:
            cp.start()
        for j, peer in enumerate(peers):
            _remote(slot(*peer), slot(*peer), send_sems.at[j], recv_sems.at[j], peer).wait_recv()
        for cp in cps:
            cp.wait_send()

    others = pl.pallas_call(
        body, name=name, in_specs=[_ANY], out_specs=_ANY,
        out_shape=jax.ShapeDtypeStruct((8, r, w), vec.dtype),
        scratch_shapes=[pltpu.SemaphoreType.DMA((7,)), pltpu.SemaphoreType.DMA((7,))],
    )(vec)
    me = 4 * lax.axis_index("x") + 2 * lax.axis_index("y") + lax.axis_index("c")
    return lax.dynamic_update_index_in_dim(others, vec, me, 0)


def _row_tile(rows, row_bytes):
    for tm in (1024, 512, 256, 128, 64, 32, 16):
        if rows % tm == 0 and tm * row_bytes <= ELEMENTWISE_BLOCK_BYTES:
            return tm
    return 16 if rows % 16 == 0 else rows


def _chip_sum_half(g, got, c, *, name):
    nb, r, w = g.shape
    half = r // 2
    tm = _row_tile(half, w * 4)
    per = half // tm

    def body(c_ref, g_ref, o_ref, out_ref):
        out_ref[...] = (g_ref[...] + o_ref[...]).astype(out_ref.dtype)

    return pl.pallas_call(
        body, name=name,
        grid_spec=pltpu.PrefetchScalarGridSpec(
            num_scalar_prefetch=1, grid=(nb, per),
            in_specs=[pl.BlockSpec((1, tm, w), lambda b, i, c_ref: (b, c_ref[0] * per + i, 0)),
                      pl.BlockSpec((1, tm, w), lambda b, i, c_ref: (b, i, 0))],
            out_specs=pl.BlockSpec((1, tm, w), lambda b, i, c_ref: (b, i, 0))),
        out_shape=jax.ShapeDtypeStruct((nb, half, w), BF16),
        compiler_params=_cparams(("parallel", "parallel")),
    )(jnp.reshape(c, (1,)).astype(jnp.int32), g, got)


def _sum_slots(stack, *, name):
    n, r, w = stack.shape
    tm = _row_tile(r, n * w * stack.dtype.itemsize)

    def body(s_ref, out_ref):
        acc = s_ref[0].astype(F32)
        for i in range(1, n):
            acc = acc + s_ref[i].astype(F32)
        out_ref[...] = acc

    return pl.pallas_call(
        body, name=name, grid=(r // tm,),
        in_specs=[pl.BlockSpec((n, tm, w), lambda i: (0, i, 0))],
        out_specs=pl.BlockSpec((tm, w), lambda i: (i, 0)),
        out_shape=jax.ShapeDtypeStruct((r, w), F32),
        compiler_params=_cparams(("parallel",)),
    )(stack)


def _adam_math(wv, gv, mv, vv):
    m_new = ADAM_B1 * mv + (1.0 - ADAM_B1) * gv
    v_new = ADAM_B2 * vv + (1.0 - ADAM_B2) * (gv * gv)
    m_hat = m_new / (1.0 - ADAM_B1 ** ADAM_STEP)
    v_hat = v_new / (1.0 - ADAM_B2 ** ADAM_STEP)
    delta = -ADAM_LR * (m_hat / (jnp.sqrt(v_hat) + ADAM_EPS) + ADAM_WD * wv)
    return delta, m_new, v_new


def _adamw(w, g, m, v, *, name):
    shape = w.shape
    cols = shape[-1]
    flat = lambda t: t.reshape(-1, cols)
    rows = flat(w).shape[0]
    tm = _pick(rows, (256, 128, 64, 32, 16, 8))
    outs = _rowwise(_adam_math, [flat(w), flat(g), flat(m), flat(v)], [], [(cols, F32, "row")] * 3, name=name, tm=tm)
    return tuple(o.reshape(shape) for o in outs)


def _adamw_slots(w, slots0, slots1, m, v, *, name):
    shape = w.shape
    cols = shape[-1]
    half = slots0.shape[2]
    v4 = lambda t: t.reshape(2, 2, half, cols)
    assert slots0.shape == slots1.shape == (2, 4, half, cols) and w.size == 4 * half * cols, (slots0.shape, shape)
    tm = _row_tile(half, cols * 4)

    def body(w_ref, s0_ref, s1_ref, m_ref, v_ref, g_ref, d_ref, mo_ref, vo_ref):
        first = pl.program_id(0) == 0
        g = None
        for i in range(4):
            part = jnp.where(first, s0_ref[0, i], s1_ref[0, i]).astype(F32)
            g = part if g is None else g + part
        delta, m_new, v_new = _adam_math(w_ref[0, 0], g, m_ref[0, 0], v_ref[0, 0])
        g_ref[0, 0], d_ref[0, 0], mo_ref[0, 0], vo_ref[0, 0] = g, delta, m_new, v_new

    blk = pl.BlockSpec((1, 1, tm, cols), lambda l, hf, i: (l, hf, i, 0))
    s0 = pl.BlockSpec((1, 4, tm, cols), lambda l, hf, i: (hf * (1 - l), 0, i * (1 - l), 0))
    s1 = pl.BlockSpec((1, 4, tm, cols), lambda l, hf, i: (hf * l, 0, i * l, 0))
    outs = pl.pallas_call(
        body, name=name, grid=(2, 2, half // tm),
        in_specs=[blk, s0, s1, blk, blk],
        out_specs=[blk] * 4, out_shape=[jax.ShapeDtypeStruct((2, 2, half, cols), F32)] * 4,
        compiler_params=_cparams(("arbitrary", "arbitrary", "arbitrary")),
    )(v4(w), slots0, slots1, v4(m), v4(v))
    return tuple(o.reshape(shape) for o in outs)


def _pad_blocks(w, axis, n_blocks, real, to=LANES, offset=0):
    axis = axis % w.ndim
    shp = w.shape
    w = w.reshape(shp[:axis] + (n_blocks, real) + shp[axis + 1:])
    pads = [(0, 0)] * w.ndim
    pads[axis + 1] = (offset, to - real - offset)
    w = jnp.pad(w, pads)
    return w.reshape(shp[:axis] + (n_blocks * to,) + shp[axis + 1:])


def _unpad_blocks(w, axis, n_blocks, real, to=LANES, offset=0):
    axis = axis % w.ndim
    shp = w.shape
    w = w.reshape(shp[:axis] + (n_blocks, to) + shp[axis + 1:])
    w = lax.slice_in_dim(w, offset, offset + real, axis=axis + 1)
    return w.reshape(shp[:axis] + (n_blocks * real,) + shp[axis + 1:])


def _block_diag(w):
    n, a, b = w.shape
    eye = jnp.eye(n, dtype=w.dtype)
    return (eye[:, None, :, None] * w[:, :, None, :]).reshape(n * a, n * b)


def _block_diag_t(d, n):
    a, b = d.shape[0] // n, d.shape[1] // n
    d = d.reshape(n, a, n, b)
    return jnp.stack([d[i, :, i, :] for i in range(n)])


_SPLITS = np.cumsum((0,) + SPLIT_SIZES)


def _w_in_groups(w_in):
    sl = lambda i: w_in[:, _SPLITS[i]:_SPLITS[i + 1]]
    xbc = sl(5)
    xbc_pad = jnp.concatenate([_pad_blocks(xbc[:, :MIX], 1, N_HEADS, HEAD),
                               _pad_blocks(xbc[:, MIX:MIX + 2 * HEAD], 1, 2, HEAD),
                               _pad_blocks(xbc[:, MIX + 2 * HEAD:], 1, 2, HEAD)], axis=1)
    return dict(
        cq=sl(0), ckv=sl(1), kr=_pad_blocks(sl(2), 1, 1, QK_ROPE, offset=HEAD), pool=sl(3),
        z=_pad_blocks(sl(4), 1, N_HEADS, HEAD), xbc=xbc_pad, dt=_pad_blocks(sl(6), 1, 1, N_HEADS),
        lru_g=sl(7), lru_x=sl(8), gates=sl(9))


def _w_in_fused(groups):
    parts, at = [], 0
    for name, off, width in IN_LAYOUT:
        assert groups[name].shape[1] == width and off >= at
        if off > at:
            parts.append(jnp.zeros((groups[name].shape[0], off - at), groups[name].dtype))
        parts.append(groups[name])
        at = off + width
    parts.append(jnp.zeros((parts[0].shape[0], IN_ALL_COLS - at), parts[0].dtype))
    return jnp.concatenate(parts, axis=1)


def _in_cols(arr, name):
    off, width = IN_OFFSETS[name]
    return _Cols(arr, off, width)


def _w_in_ungroup(d):
    xbc = d["xbc"]
    w = N_HEADS * LANES
    xbc_real = jnp.concatenate([_unpad_blocks(xbc[:, :w], 1, N_HEADS, HEAD),
                                _unpad_blocks(xbc[:, w:w + 2 * LANES], 1, 2, HEAD),
                                _unpad_blocks(xbc[:, w + 2 * LANES:], 1, 2, HEAD)], axis=1)
    return jnp.concatenate([d["cq"], d["ckv"], _unpad_blocks(d["kr"], 1, 1, QK_ROPE, offset=HEAD), d["pool"],
                            _unpad_blocks(d["z"], 1, N_HEADS, HEAD), xbc_real, _unpad_blocks(d["dt"], 1, 1, N_HEADS),
                            d["lru_g"], d["lru_x"], d["gates"]], axis=1)


def _pad_xbc_vec(v):
    return jnp.concatenate([_pad_blocks(v[..., :MIX], -1, N_HEADS, HEAD),
                            _pad_blocks(v[..., MIX:MIX + 2 * HEAD], -1, 2, HEAD),
                            _pad_blocks(v[..., MIX + 2 * HEAD:], -1, 2, HEAD)], axis=-1)


def _unpad_xbc_vec(v):
    w = N_HEADS * LANES
    return jnp.concatenate([_unpad_blocks(v[..., :w], -1, N_HEADS, HEAD),
                            _unpad_blocks(v[..., w:w + 2 * LANES], -1, 2, HEAD),
                            _unpad_blocks(v[..., w + 2 * LANES:], -1, 2, HEAD)], axis=-1)


def _layer_weights(p):
    q = dict(p)
    q["in_all"] = _w_in_fused(_w_in_groups(p["w_in"]))
    q["uq"] = _pad_blocks(p["w_uq"], 1, N_HEADS, HEAD + QK_ROPE)
    ukv = p["w_ukv"].reshape(KV_LORA, N_HEADS, 2 * HEAD)
    q["ukv"] = jnp.concatenate([_pad_blocks(ukv[:, :, :HEAD].reshape(KV_LORA, -1), 1, N_HEADS, HEAD),
                                _pad_blocks(ukv[:, :, HEAD:].reshape(KV_LORA, -1), 1, N_HEADS, HEAD)], axis=1)
    q["pool_bd"] = _block_diag(p["w_pool"])
    q["lru_bd"] = jnp.concatenate([_block_diag(p["lru_w_a"]), _block_diag(p["lru_w_i"])], axis=1)
    q["br"] = [_pad_blocks(p["w_branch"][0], 0, N_HEADS, HEAD), p["w_branch"][1],
               _pad_blocks(p["w_branch"][2], 0, N_HEADS, HEAD), p["w_branch"][3]]
    q["ssd_conv_w_pad"] = _pad_xbc_vec(p["ssd_conv_w"])
    q["ssd_conv_b_pad"] = _pad_xbc_vec(p["ssd_conv_b"])[None, :]
    q["ssd_norm_pad"] = _pad_blocks(p["ssd_norm"], 0, N_HEADS, HEAD)[None, :]
    return q


def _row(v):
    return v.reshape(1, -1)


def _scal3(v):
    return v.reshape(N_HEADS, 1, 1)


def _layer_fwd(x, p_emb, w, rope, tag):
    n = lambda s: f"{s}_{tag}"
    sv = {"x": x}
    h = _rms_fwd(x, _row(w["g_mix"]), name=n("rms_mix"))
    sv["h"] = h
    u_all = _mm(h, w["in_all"], name=n("in_proj"))
    u = {k: _in_cols(u_all, k) for k in IN_OFFSETS}
    sv["u"] = u

    cqn = _rms_fwd(u["cq"], _row(w["q_norm"]), name=n("rms_q"))
    ckvn = _rms_fwd(u["ckv"], _row(w["kv_norm"]), name=n("rms_kv"))
    q_pad = _mm(cqn, w["uq"], name=n("uq"))
    kv2 = _mm(ckvn, w["ukv"], name=n("ukv"))
    qc, kc, vc = _att_prep(q_pad, kv2, u["kr"], *rope, name=n("att_prep"))
    y_a, lse = _flash_fwd(qc, kc, vc, name=n("flash_fwd"))
    sv.update(cqn=cqn, ckvn=ckvn, qc=qc, kc=kc, vc=vc, y_a=y_a, lse=lse)

    pool_d = _pool_fwd(u["pool"], name=n("pool_fwd"))
    yb_pre, y_b = _mm(pool_d, w["pool_bd"], epilogue=lambda acc, sc: (acc, acc * sc),
                      rowvecs=[_row(w["pool_scale"])], out_dtypes=(F32, BF16), name=n("pool_mm"))
    sv.update(pool_d=pool_d, yb_pre=yb_pre, y_b=y_b)

    xbc_c = _conv_fwd(u["xbc"], w["ssd_conv_w_pad"], w["ssd_conv_b_pad"], silu=True, name=n("ssd_conv"))
    dt8 = lax.slice_in_dim(u_all, IN_OFFSETS["dt"][0], IN_OFFSETS["dt"][0] + N_HEADS, axis=1)
    dtcol = dt8.T[:, :, None]
    dtrow = dt8.T[:, None, :]
    ssd_par = (_scal3(w["ssd_dt_bias"]), _scal3(w["ssd_a_log"]), _scal3(w["ssd_d"]))
    y_ssd, states = _ssd_fwd(xbc_c, dtcol, dtrow, *ssd_par, name=n("ssd_fwd"))

    def ssd_post(yv, zv, gv):
        xh, _ = _rms_parts(yv * _silu(zv), MIX)
        return xh * gv

    y_c = _rowwise(ssd_post, [y_ssd, u["z"]], [w["ssd_norm_pad"]], [(N_HEADS * LANES, BF16, "row")], name=n("ssd_post"))
    sv.update(xbc_c=xbc_c, dtcol=dtcol, dtrow=dtrow, y_ssd=y_ssd, states=states, y_c=y_c)

    xc = _conv_fwd(u["lru_x"], w["lru_conv_w"], _row(w["lru_conv_b"]), silu=False, name=n("lru_conv"))
    pre = _mm(xc, w["lru_bd"], name=n("lru_mm"))
    lru_par = (_row(w["lru_lambda"]), _row(w["lru_b_a"]), _row(w["lru_b_i"]))
    y_d, h_lru = _lru_fwd(pre, xc, u["lru_g"], *lru_par, name=n("lru_fwd"))
    sv.update(xc=xc, pre=pre, h_lru=h_lru, y_d=y_d)

    merged, *ybs = _branch_merge([y_a, y_b, y_c, y_d], w["br"], u_all, name=n("branch_merge"))
    x1 = _mm(merged, w["w_out"], epilogue=lambda acc, xr: (acc + xr,), tiles=[x], name=n("out_proj"))
    sv.update(ybs=ybs, merged=merged, x1=x1)

    h2 = _rms_fwd(x1, _row(w["g_mlp"]), name=n("rms_mlp"))
    a_ff, f_ff = _mm(h2, w["w_ff1"], epilogue=lambda acc: (acc, jnp.square(jnp.maximum(acc, 0.0))),
                     out_dtypes=(F32, BF16), name=n("ff1"))
    x2 = _mm(f_ff, w["w_ff2"], epilogue=lambda acc, xr: (acc + xr,), tiles=[x1], name=n("ff2"))
    sv.update(h2=h2, a_ff=a_ff, f_ff=f_ff, x2=x2)

    h3 = _rms_fwd(x2, _row(w["g_ple"]), name=n("rms_ple"))
    e_ple = _mm(p_emb, w["w_ple"], name=n("ple_emb"))
    x3, gt_ple = _mm(h3, w["w_ple_gate"], epilogue=lambda acc, ev, xr: (xr + ev * _sigmoid(acc), _sigmoid(acc)),
                     tiles=[e_ple, x2], out_dtypes=(F32, F32), name=n("ple_gate"))
    sv.update(h3=h3, e_ple=e_ple, gt_ple=gt_ple, p_emb=p_emb)
    return x3, sv


def _layer_bwd(dx3, sv, w, rope, tag):
    n = lambda s: f"{s}_{tag}"
    gr = {}
    u = sv["u"]

    de, dpre = _rowwise(lambda d, gt, ev: (d * gt, d * ev * gt * (1.0 - gt)), [dx3, sv["gt_ple"], sv["e_ple"]], [],
                        [(D_MODEL, BF16, "row"), (D_MODEL, BF16, "row")], name=n("ple_bwd"))
    gr["w_ple"] = _mm(sv["p_emb"], de, ta=True, name=n("d_w_ple"))
    gr["w_ple_gate"] = _mm(sv["h3"], dpre, ta=True, name=n("d_w_ple_gate"))
    dh3 = _mm(dpre, w["w_ple_gate"], tb=True, out_dtypes=(BF16,), name=n("d_h3"))
    dx2, dg = _rms_bwd(sv["x2"], _row(w["g_ple"]), dh3, dx3, name=n("rms_ple_bwd"))
    gr["g_ple"] = dg[0]

    gr["w_ff2"] = _mm(sv["f_ff"], dx2, ta=True, name=n("d_w_ff2"))
    da = _mm(dx2, w["w_ff2"], tb=True, epilogue=lambda acc, av: (acc * 2.0 * jnp.maximum(av, 0.0),),
             tiles=[sv["a_ff"]], out_dtypes=(BF16,), name=n("d_a_ff"))
    gr["w_ff1"] = _mm(sv["h2"], da, ta=True, name=n("d_w_ff1"))
    dh2 = _mm(da, w["w_ff1"], tb=True, out_dtypes=(BF16,), name=n("d_h2"))
    dx1, dg = _rms_bwd(sv["x1"], _row(w["g_mlp"]), dh2, dx2, name=n("rms_mlp_bwd"))
    gr["g_mlp"] = dg[0]

    gr["w_out"] = _mm(sv["merged"], dx1, ta=True, name=n("d_w_out"))
    dmerged = _mm(dx1, w["w_out"], tb=True, name=n("d_merged"))

    def merge_bwd(dm, gts, y0, y1, y2, y3):
        dys, dgs = [], []
        for b, yb in enumerate((y0, y1, y2, y3)):
            sg = _sigmoid(gts[:, b * D_MODEL:(b + 1) * D_MODEL])
            dys.append(dm * sg)
            dgs.append(dm * yb * sg * (1.0 - sg))
        return (*dys, jnp.concatenate(dgs, axis=1))

    *dybs, dgates = _rowwise(merge_bwd, [dmerged, u["gates"]] + sv["ybs"], [],
                             [(D_MODEL, BF16, "row")] * 4 + [(4 * D_MODEL, BF16, "row")], name=n("merge_bwd"))
    ys = [sv["y_a"], sv["y_b"], sv["y_c"], sv["y_d"]]
    dwb = [_mm(ys[b], dybs[b], ta=True, name=n(f"d_w_branch{b}")) for b in range(4)]
    gr["w_branch"] = jnp.stack([_unpad_blocks(dwb[0], 0, N_HEADS, HEAD), dwb[1],
                                _unpad_blocks(dwb[2], 0, N_HEADS, HEAD), dwb[3]])
    dy_a = _mm(dybs[0], w["br"][0], tb=True, out_dtypes=(BF16,), name=n("d_y_a"))
    dy_b = _mm(dybs[1], w["br"][1], tb=True, name=n("d_y_b"))
    dy_c = _mm(dybs[2], w["br"][2], tb=True, name=n("d_y_c"))
    dy_d = _mm(dybs[3], w["br"][3], tb=True, name=n("d_y_d"))
    du = {"gates": dgates}

    lru_par = (_row(w["lru_lambda"]), _row(w["lru_b_a"]), _row(w["lru_b_i"]))
    dpa, dpi, dxc_direct, du["lru_g"], dlam, dba, dbi = _lru_bwd(
        sv["pre"], sv["xc"], u["lru_g"], *lru_par, sv["h_lru"], dy_d, name=n("lru_bwd"))
    dpre_lru = jnp.concatenate([dpa, dpi], axis=1)
    d_bd = _mm(sv["xc"], dpre_lru, ta=True, name=n("d_lru_w"))
    gr["lru_w_a"] = _block_diag_t(d_bd[:, :MIX], N_HEADS)
    gr["lru_w_i"] = _block_diag_t(d_bd[:, MIX:], N_HEADS)
    gr["lru_lambda"], gr["lru_b_a"], gr["lru_b_i"] = dlam[0], dba[0], dbi[0]
    dxc = _mm(dpre_lru, w["lru_bd"], tb=True, epilogue=lambda acc, t: (acc + t,), tiles=[dxc_direct], name=n("d_xc"))
    du["lru_x"], gr["lru_conv_w"], dcb = _conv_bwd(u["lru_x"], w["lru_conv_w"], _row(w["lru_conv_b"]), dxc,
                                                  silu=False, name=n("lru_conv_bwd"))
    gr["lru_conv_b"] = dcb[0]

    def ssd_post_bwd(dyc, yv, zv, gv):
        sz = _silu(zv)
        dyz, dgain = _rms_bwd_math(yv * sz, gv, dyc, MIX)
        return dyz * sz, dyz * yv * _silu_grad(zv), dgain

    dy_ssd, du["z"], dgain = _rowwise(ssd_post_bwd, [dy_c, sv["y_ssd"], u["z"]], [w["ssd_norm_pad"]],
                                      [(N_HEADS * LANES, F32, "row"), (N_HEADS * LANES, BF16, "row"),
                                       (N_HEADS * LANES, F32, "acc")], name=n("ssd_post_bwd"))
    gr["ssd_norm"] = _unpad_blocks(dgain[0], 0, N_HEADS, HEAD)
    ssd_par = (_scal3(w["ssd_dt_bias"]), _scal3(w["ssd_a_log"]), _scal3(w["ssd_d"]))
    dxs, dbg, dcg, ddt, dbias, dalog, dd = _ssd_bwd(sv["xbc_c"], sv["dtcol"], sv["dtrow"], *ssd_par, sv["states"],
                                                    dy_ssd, name=n("ssd_bwd"))
    s = dxs.shape[0]
    dxbc_c = jnp.concatenate([dxs, dbg, dcg], axis=1)
    gr["ssd_dt_bias"], gr["ssd_a_log"], gr["ssd_d"] = dbias[:, 0, 0], dalog[:, 0, 0], dd[:, 0, 0]
    du["xbc"], dcw, dcb = _conv_bwd(u["xbc"], w["ssd_conv_w_pad"], w["ssd_conv_b_pad"], dxbc_c, silu=True,
                                    name=n("ssd_conv_bwd"))
    gr["ssd_conv_w"], gr["ssd_conv_b"] = _unpad_xbc_vec(dcw), _unpad_xbc_vec(dcb[0])
    du["dt"] = jnp.pad(ddt[:, :, 0].T, ((0, 0), (0, LANES - N_HEADS)))

    dyb_pre, dscale = _rowwise(lambda d, yp, sc: (d * sc, _colsum(d * yp)), [dy_b, sv["yb_pre"]],
                               [_row(w["pool_scale"])], [(MIX, BF16, "row"), (MIX, F32, "acc")], name=n("pool_scale_bwd"))
    gr["pool_scale"] = dscale[0]
    gr["w_pool"] = _block_diag_t(_mm(sv["pool_d"], dyb_pre, ta=True, name=n("d_w_pool")), 4)
    dd_pool = _mm(dyb_pre, w["pool_bd"], tb=True, name=n("d_pool_d"))
    du["pool"] = _pool_bwd(dd_pool, name=n("pool_bwd"))

    delta = _att_delta(sv["y_a"], dy_a, name=n("att_delta"))
    to_row = lambda t: t.reshape(N_HEADS, 1, s)
    dqc, dkc, dvc = _flash_bwd(sv["qc"], sv["kc"], sv["vc"], dy_a, to_row(sv["lse"]), to_row(delta), name=n("flash_bwd"))
    dq_pad, du["kr"] = _att_prep_bwd(dqc, dkc, *rope, name=n("att_prep_bwd"))
    d_uq = _mm(sv["cqn"], dq_pad, ta=True, name=n("d_w_uq"))
    gr["w_uq"] = _unpad_blocks(d_uq, 1, N_HEADS, HEAD + QK_ROPE)
    dcqn = _mm(dq_pad, w["uq"], tb=True, out_dtypes=(BF16,), name=n("d_cqn"))
    du["cq"], dg = _rms_bwd(u["cq"], _row(w["q_norm"]), dcqn, name=n("rms_q_bwd"))
    gr["q_norm"] = dg[0]
    dkv2 = jnp.concatenate([dkc, dvc], axis=1).astype(BF16)
    d_ukv = _mm(sv["ckvn"], dkv2, ta=True, name=n("d_w_ukv"))
    wk = N_HEADS * LANES
    dk_real = _unpad_blocks(d_ukv[:, :wk], 1, N_HEADS, HEAD).reshape(KV_LORA, N_HEADS, HEAD)
    dv_real = _unpad_blocks(d_ukv[:, wk:], 1, N_HEADS, HEAD).reshape(KV_LORA, N_HEADS, HEAD)
    gr["w_ukv"] = jnp.concatenate([dk_real, dv_real], axis=2).reshape(KV_LORA, N_HEADS * 2 * HEAD)
    dckvn = _mm(dkv2, w["ukv"], tb=True, out_dtypes=(BF16,), name=n("d_ckvn"))
    du["ckv"], dg = _rms_bwd(u["ckv"], _row(w["kv_norm"]), dckvn, name=n("rms_kv_bwd"))
    gr["kv_norm"] = dg[0]

    du_all = _w_in_fused({k: v.astype(BF16) for k, v in du.items()})
    dw_all = _mm(sv["h"], du_all, ta=True, name=n("d_w_in"))
    gr["w_in"] = _w_in_ungroup({k: dw_all[:, off:off + width] for k, off, width in IN_LAYOUT})
    dh = _mm(du_all, w["in_all"], tb=True, name=n("d_h"))
    dx, dg = _rms_bwd(sv["x"], _row(w["g_mix"]), dh, dx1, name=n("rms_mix_bwd"))
    gr["g_mix"] = dg[0]
    return dx, gr


def _pack_rows(n_elems):
    per = PACK_W * PACK_ROWS
    return -(-n_elems // per) * PACK_ROWS


def _pack_flat(parts, dtype):
    flat = jnp.concatenate([p.reshape(-1).astype(dtype) for p in parts])
    rows = _pack_rows(flat.shape[0])
    return jnp.pad(flat, (0, rows * PACK_W - flat.shape[0])).reshape(rows, PACK_W)


def _unpack_flat(buf, shapes):
    lead = buf.shape[:-2]
    flat = buf.reshape(lead + (-1,))
    out, off = [], 0
    for shp in shapes:
        size = int(np.prod(shp))
        out.append(flat[..., off:off + size].reshape(lead + tuple(shp)))
        off += size
    return out


def _merge_shards(t, axis):
    return jnp.concatenate([t[i] for i in range(4)], axis=axis)


def _split_shards(t, axis):
    return jnp.stack(jnp.split(t, 4, axis=axis))


def _rope_tables(positions):
    inv = 1.0 / (ROPE_THETA ** (jnp.arange(0, QK_ROPE, 2, dtype=F32) / QK_ROPE))
    ang = positions.astype(F32)[:, None] * inv
    cos, sin = jnp.cos(ang), jnp.sin(ang)
    s = ang.shape[0]
    half = QK_ROPE // 2
    z = lambda n_: jnp.zeros((s, n_), F32)
    cos_t = jnp.concatenate([jnp.ones((s, HEAD), F32), cos, cos, jnp.ones((s, LANES - HEAD - QK_ROPE), F32)], axis=1)
    sin_p = jnp.concatenate([z(HEAD + half), sin, z(LANES - HEAD - QK_ROPE)], axis=1)
    sin_m = jnp.concatenate([z(HEAD), -sin, z(half + LANES - HEAD - QK_ROPE)], axis=1)
    return cos_t, sin_p, sin_m


def _loss_head(x, g, target, *, name):
    d = x.shape[1]

    def fn(xv, tv, gv):
        xh, r = _rms_parts(xv, d)
        y = xh * gv
        err = y - tv
        dy = err * (1.0 / d)
        dxh = dy * gv
        dx = r * (dxh - xh * (jnp.sum(dxh * xh, axis=-1, keepdims=True) * (1.0 / d)))
        return dx, _colsum(dy * xh), _colsum(err * err) * (0.5 / d)

    return _rowwise(fn, [x, target], [g], [(d, F32, "row"), (d, F32, "acc"), (d, F32, "acc")], name=name)


MATS = tuple((nm, ax) for nm, ax in BIG if nm not in CONV_SHARDED)


def _grad_view(g, ax_layer):
    if ax_layer == 0:
        return g.reshape(4, g.shape[0] // 4, g.shape[1])
    return g.reshape(1, -1, g.shape[-1])


def _reduce_start(grads_l, c_idx, tag):
    views = [_grad_view(grads_l[nm], ax - 1) for nm, ax in MATS]
    got = _send_half(views, name="send_half_" + tag)
    parts = []
    for (nm, ax), v, gt in zip(MATS, views, got):
        both = _chip_sum_half(v, gt, c_idx, name=f"chip_sum_{nm}_{tag}")
        parts.append(both if ax == 1 else _split_shards(both[0], 1))
    return _push_start(parts, scatter=True, name="push_grads_" + tag)


def _reduce_finish(state, after, k_chip, c_idx, tag):
    send_sems, recv_sems, parts, lands, _ = state
    parts, landed = _push_wait(send_sems, recv_sems, parts, lands, after, name="wait_grads_" + tag)
    mine = [lax.dynamic_update_index_in_dim(t, lax.dynamic_index_in_dim(p, k_chip, 0, keepdims=False), k_chip, 0)
            for t, p in zip(landed, parts)]
    other = _swap_with_sibling(mine, name="swap_halves_" + tag)
    return [jnp.where(c_idx == 0, jnp.stack([a, b]), jnp.stack([b, a])) for a, b in zip(mine, other)]


def _step(args):
    x = args["x"][0]
    c_idx = lax.axis_index("c")
    k_chip = 2 * lax.axis_index("x") + lax.axis_index("y")

    mats = MATS
    mine = [[args[nm][l].astype(BF16) for nm, _ in mats] for l in range(2)]
    gathered0 = _gather_halves(mine[0])
    convs = [(nm, ax) for nm, ax in BIG if nm in CONV_SHARDED]
    conv_all = _gather_all(_pack_flat([args[nm] for nm, _ in convs], F32), name="gather_conv_taps")[0::2]
    mine1, gathered0, conv_all = lax.optimization_barrier((mine[1], gathered0, conv_all))
    gathered0 = [lax.dynamic_update_index_in_dim(t, own, k_chip, 0) for t, own in zip(gathered0, mine[0])]
    send_sems, recv_sems, blocks1, lands1, token = _push_start(mine1, scatter=False, name="push_weights_l1")
    full_conv = {nm: _merge_shards(t, ax)
                 for (nm, ax), t in zip(convs, _unpack_flat(conv_all, [args[nm].shape for nm, _ in convs]))}
    rope = _rope_tables(args["positions"][0])

    def layer_weights(l, gathered):
        p = {nm: _merge_shards(t, ax - 1) for (nm, ax), t in zip(mats, gathered)}
        p.update({nm: full_conv[nm][l] for nm in CONV_SHARDED})
        p.update({nm: args[nm][l] for nm in SMALL if nm != "g_final"})
        return _layer_weights(p)

    layers = [layer_weights(0, gathered0), None]
    layers[0]["g_mix"] = layers[0]["g_mix"] + token[0, 0]
    x, sv0 = _layer_fwd(x, args["p"][0, 0], layers[0], rope, "l0")
    own1, landed1 = _push_wait(send_sems, recv_sems, blocks1, lands1, x, name="wait_weights_l1")
    gathered1 = [lax.dynamic_update_index_in_dim(t, own, k_chip, 0) for t, own in zip(landed1, own1)]
    layers[1] = layer_weights(1, gathered1)
    x, sv1 = _layer_fwd(x, args["p"][1, 0], layers[1], rope, "l1")
    saved = [sv0, sv1]

    dx, dg_final, loss_part = _loss_head(x, _row(args["g_final"]), args["loss_target"][0], name="loss_head")
    loss = lax.psum(jnp.sum(loss_part), ("x", "y", "c"))

    grads = [None, None]
    dx, grads[1] = _layer_bwd(dx, saved[1], layers[1], rope, "l1")
    reduce1 = _reduce_start(grads[1], c_idx, "l1")
    dx, grads[0] = _layer_bwd(dx + reduce1[4][0, 0], saved[0], layers[0], rope, "l0")
    reduce0 = _reduce_start(grads[0], c_idx, "l0")

    g_all = {nm: jnp.stack([grads[0][nm], grads[1][nm]]) for nm in SMALL + CONV_SHARDED if nm != "g_final"}
    g_all["g_final"] = dg_final[0]
    all_names = SMALL + CONV_SHARDED
    all_shapes = [g_all[nm].shape for nm in all_names]
    small_sum = _sum_slots(_gather_all(_pack_flat([g_all[nm] for nm in all_names], F32), name="gather_small_grads"),
                           name="sum_devices")
    g_red = dict(zip(all_names, _unpack_flat(small_sum, all_shapes)))
    for nm, ax in BIG:
        if nm in CONV_SHARDED:
            width = args[nm].shape[ax]
            g_red[nm] = lax.dynamic_slice_in_dim(g_red[nm], k_chip * width, width, axis=ax)
    small_shapes = [args[nm].shape for nm in SMALL]
    pack_small = lambda src: _pack_flat([src(nm) for nm in SMALL], F32)
    upd_small = _adamw(pack_small(lambda nm: args[nm]), pack_small(lambda nm: g_red[nm]),
                       pack_small(lambda nm: args["m_" + nm]), pack_small(lambda nm: args["v_" + nm]), name="adamw_small")
    upd = {nm: trip for nm, trip in zip(SMALL, zip(*[_unpack_flat(t, small_shapes) for t in upd_small]))}
    for nm in CONV_SHARDED:
        upd[nm] = _adamw(args[nm], g_red[nm], args["m_" + nm], args["v_" + nm], name="adamw_" + nm)

    slots = [_reduce_finish(reduce0, upd_small[0], k_chip, c_idx, "l0"),
             _reduce_finish(reduce1, dx, k_chip, c_idx, "l1")]
    for i, (nm, _) in enumerate(MATS):
        g_red[nm], *upd[nm] = _adamw_slots(args[nm], slots[0][i], slots[1][i], args["m_" + nm], args["v_" + nm],
                                           name="adamw_" + nm)

    outs = [loss, dx[None]]
    outs += [g_red[nm] for nm in WEIGHTS]
    for i in range(3):
        outs += [upd[nm][i] for nm in WEIGHTS]
    return tuple(outs)


_ARG_NAMES = ("x", "p", "positions") + WEIGHTS + ("loss_target",) + tuple("m_" + nm for nm in WEIGHTS) \
    + tuple("v_" + nm for nm in WEIGHTS)


def kernel(*arrays):
    assert len(arrays) == len(_ARG_NAMES), len(arrays)
    return _step(dict(zip(_ARG_NAMES, arrays)))
```

```python
import functools
import math

import jax
import jax.numpy as jnp
import numpy as np
from jax import lax
from jax.experimental import pallas as pl
from jax.experimental.pallas import tpu as pltpu

F32 = jnp.float32
BF16 = jnp.bfloat16
MXU_DTYPE = BF16
LANES = 128
VMEM_LIMIT = 56 * 1024 * 1024
MM_VMEM_BUDGET = 36 * 1024 * 1024
ELEMENTWISE_BLOCK_BYTES = 2 * 1024 * 1024

D_MODEL = 1024
N_HEADS = 8
HEAD = 64
QK_ROPE = 32
Q_LORA = 384
KV_LORA = 256
MIX = 512
SSD_CHUNK = 128
CONV_W = 4
POOL_WINDOWS = (2, 4, 8, 16)
LRU_C = 8.0
EPS = 1e-6
ROPE_THETA = 10000.0
ATT_SCALE = (HEAD + QK_ROPE) ** -0.5
SPLIT_SIZES = (Q_LORA, KV_LORA, QK_ROPE, MIX, MIX, 768, N_HEADS, MIX, MIX, 4 * D_MODEL)
IN_LAYOUT = (("gates", 0, 4096), ("z", 4096, 1024), ("pool", 5120, 512), ("lru_g", 5632, 512), ("lru_x", 6144, 512),
             ("cq", 6912, 384), ("ckv", 7424, 256), ("xbc", 7680, 1536), ("kr", 9216, 128), ("dt", 9344, 128))
IN_OFFSETS = {name: (off, width) for name, off, width in IN_LAYOUT}
IN_ALL_COLS = 9728

ADAM_LR, ADAM_B1, ADAM_B2, ADAM_EPS, ADAM_WD, ADAM_STEP = 0.001, 0.9, 0.999, 1e-08, 0.01, 10

BIG = (("w_in", 2), ("w_uq", 2), ("w_ukv", 2), ("ssd_conv_w", 2), ("lru_conv_w", 2), ("w_branch", 3),
       ("w_out", 1), ("w_ff1", 2), ("w_ff2", 1), ("w_ple_gate", 1), ("w_ple", 2))
SMALL = ("g_mix", "q_norm", "kv_norm", "w_pool", "pool_scale", "ssd_conv_b", "ssd_dt_bias", "ssd_a_log",
         "ssd_d", "ssd_norm", "lru_conv_b", "lru_w_a", "lru_b_a", "lru_w_i", "lru_b_i", "lru_lambda",
         "g_mlp", "g_ple", "g_final")
WEIGHTS = ("g_mix", "w_in", "q_norm", "w_uq", "kv_norm", "w_ukv", "w_pool", "pool_scale", "ssd_conv_w",
           "ssd_conv_b", "ssd_dt_bias", "ssd_a_log", "ssd_d", "ssd_norm", "lru_conv_w", "lru_conv_b", "lru_w_a",
           "lru_b_a", "lru_w_i", "lru_b_i", "lru_lambda", "w_branch", "w_out", "g_mlp", "w_ff1", "w_ff2", "g_ple",
           "w_ple_gate", "w_ple", "g_final")
CONV_SHARDED = ("ssd_conv_w", "lru_conv_w")
PACK_W = 1024
PACK_ROWS = 64


def _cparams(sem, vmem=VMEM_LIMIT):
    return pltpu.CompilerParams(dimension_semantics=sem, vmem_limit_bytes=vmem)


def _pick(n, cands):
    for c in cands:
        if n % c == 0:
            return c
    return n


class _Cols:
    def __init__(self, arr, off, width):
        self.arr, self.off, self.width = arr, off, width

    shape = property(lambda self: (self.arr.shape[0], self.width))
    dtype = property(lambda self: self.arr.dtype)


def _arr(x):
    return x.arr if isinstance(x, _Cols) else x


def _off(x, unit):
    off = x.off if isinstance(x, _Cols) else 0
    assert off % unit == 0, (off, unit)
    return off // unit


def _sigmoid(x):
    return 1.0 / (1.0 + jnp.exp(-x))


def _silu(x):
    return x * _sigmoid(x)


def _silu_grad(x):
    s = _sigmoid(x)
    return s * (1.0 + x * (1.0 - s))


def _softplus(x):
    e = jnp.exp(-jnp.abs(x))
    log1p_e = jnp.where(e < 1e-3, e * (1.0 - e * (0.5 - e * (1.0 / 3.0))), jnp.log(1.0 + e))
    return jnp.maximum(x, 0.0) + log1p_e


_GELU_C = math.sqrt(2.0 / math.pi)


def _gelu(x):
    t = jnp.tanh(_GELU_C * (x + 0.044715 * x * x * x))
    return 0.5 * x * (1.0 + t)


def _gelu_grad(x):
    t = jnp.tanh(_GELU_C * (x + 0.044715 * x * x * x))
    return 0.5 * (1.0 + t) + 0.5 * x * (1.0 - t * t) * _GELU_C * (1.0 + 3.0 * 0.044715 * x * x)


def _neg_expm1(x):
    series = -x * (1.0 + 0.5 * x * (1.0 + (1.0 / 3.0) * x * (1.0 + 0.25 * x)))
    return jnp.where(x > -0.05, series, 1.0 - jnp.exp(x))


def _shift_down(x, k, row):
    return jnp.where(row >= k, pltpu.roll(x, k, 0), 0.0)


def _shift_up(x, k, row):
    n = x.shape[0]
    return jnp.where(row < n - k, pltpu.roll(x, n - k, 0), 0.0)


def _cumsum_rows(x, row):
    d = 1
    while d < x.shape[0]:
        x = x + _shift_down(x, d, row)
        d *= 2
    return x


def _rev_cumsum_rows(x, row):
    d = 1
    while d < x.shape[0]:
        x = x + _shift_up(x, d, row)
        d *= 2
    return x


def _cumsum_lanes(x, col):
    d = 1
    while d < x.shape[1]:
        x = x + jnp.where(col >= d, pltpu.roll(x, d, 1), 0.0)
        d *= 2
    return x


def _dot(a, b, ta=False, tb=False):
    dn = (((0 if ta else 1,), (1 if tb else 0,)), ((), ()))
    return lax.dot_general(a.astype(MXU_DTYPE), b.astype(MXU_DTYPE), dn, preferred_element_type=F32)


def _mm_tiles(m, n, k, a_bytes, b_bytes, mn_bytes):
    best = None
    for tm in (1024, 512, 384, 256, 128):
        for tn in (1024, 512, 384, 256, 128):
            for tk in (2048, 1024, 512, 384, 256, 128):
                if m % tm or n % tn or k % tk:
                    continue
                vmem = 2 * (tm * tk * a_bytes + tk * tn * b_bytes) + 2 * tm * tn * mn_bytes + 4 * tm * tn
                vmem += 2 * (tm * tk + tk * tn)
                if vmem > MM_VMEM_BUDGET:
                    continue
                steps = (m // tm) * (n // tn) * (k // tk)
                key = (steps, vmem)
                if best is None or key < best[0]:
                    best = (key, (tm, tn, tk))
    assert best is not None, (m, n, k)
    return best[1]


def _mm(a, b, *, ta=False, tb=False, epilogue=None, tiles=(), rowvecs=(), out_dtypes=(F32,), name):
    m, k = (a.shape[1], a.shape[0]) if ta else a.shape
    n = b.shape[0] if tb else b.shape[1]
    assert (b.shape[1] if tb else b.shape[0]) == k, (a.shape, b.shape, ta, tb)
    mn_bytes = sum(t.dtype.itemsize for t in tiles) + sum(jnp.dtype(dt).itemsize for dt in out_dtypes)
    tm, tn, tk = _mm_tiles(m, n, k, a.dtype.itemsize, b.dtype.itemsize, mn_bytes)
    nk = k // tk
    nt, nr, no = len(tiles), len(rowvecs), len(out_dtypes)

    def body(*refs):
        a_ref, b_ref = refs[0], refs[1]
        tile_refs = refs[2:2 + nt]
        row_refs = refs[2 + nt:2 + nt + nr]
        out_refs = refs[2 + nt + nr:2 + nt + nr + no]
        acc_ref = refs[-1]
        kk = pl.program_id(2)

        @pl.when(kk == 0)
        def _():
            acc_ref[...] = jnp.zeros_like(acc_ref)

        acc_ref[...] += _dot(a_ref[...], b_ref[...], ta, tb)

        @pl.when(kk == nk - 1)
        def _():
            acc = acc_ref[...]
            if epilogue is None:
                outs = (acc,)
            else:
                outs = epilogue(acc, *[t[...] for t in tile_refs], *[r[...] for r in row_refs])
            for o_ref, o in zip(out_refs, outs):
                o_ref[...] = o.astype(o_ref.dtype)

    a_spec = pl.BlockSpec((tk, tm), lambda i, j, kk: (kk, i)) if ta else pl.BlockSpec((tm, tk), lambda i, j, kk: (i, kk))
    b_spec = pl.BlockSpec((tn, tk), lambda i, j, kk: (j, kk)) if tb else pl.BlockSpec((tk, tn), lambda i, j, kk: (kk, j))
    mn_spec = pl.BlockSpec((tm, tn), lambda i, j, kk: (i, j))
    row_spec = pl.BlockSpec((1, tn), lambda i, j, kk: (0, j))
    tile_specs = [pl.BlockSpec((tm, tn), lambda i, j, kk, ob=_off(t, tn): (i, j + ob)) for t in tiles]
    outs = pl.pallas_call(
        body, name=name,
        grid=(m // tm, n // tn, nk),
        in_specs=[a_spec, b_spec] + tile_specs + [row_spec] * nr,
        out_specs=[mn_spec] * no,
        out_shape=[jax.ShapeDtypeStruct((m, n), dt) for dt in out_dtypes],
        scratch_shapes=[pltpu.VMEM((tm, tn), F32)],
        compiler_params=_cparams(("parallel", "parallel", "arbitrary")),
    )(a, b, *[_arr(t) for t in tiles], *rowvecs)
    return outs[0] if no == 1 else tuple(outs)


def _branch_merge(ys, ws, u_all, *, name):
    s, d = ys[0].shape[0], ws[0].shape[1]
    tm, tn = _pick(s, (512, 256, 128)), _pick(d, (512, 256, 128))
    nb = len(ys)

    def body(*refs):
        y_refs, w_refs, g_refs = refs[:nb], refs[nb:2 * nb], refs[2 * nb:3 * nb]
        merged_ref, yb_refs = refs[3 * nb], refs[3 * nb + 1:]
        merged = None
        for y_ref, w_ref, g_ref, yb_ref in zip(y_refs, w_refs, g_refs, yb_refs):
            acc = _dot(y_ref[...], w_ref[...])
            yb_ref[...] = acc.astype(yb_ref.dtype)
            term = _sigmoid(g_ref[...]) * acc
            merged = term if merged is None else merged + term
        merged_ref[...] = merged

    mn = pl.BlockSpec((tm, tn), lambda i, j: (i, j))
    in_specs = [pl.BlockSpec((tm, y.shape[1]), lambda i, j: (i, 0)) for y in ys]
    in_specs += [pl.BlockSpec((w.shape[0], tn), lambda i, j: (0, j)) for w in ws]
    in_specs += [pl.BlockSpec((tm, tn), lambda i, j, ob=b * d // tn: (i, j + ob)) for b in range(nb)]
    return pl.pallas_call(
        body, name=name, grid=(s // tm, d // tn), in_specs=in_specs, out_specs=[mn] * (nb + 1),
        out_shape=[jax.ShapeDtypeStruct((s, d), F32)] + [jax.ShapeDtypeStruct((s, d), BF16)] * nb,
        compiler_params=_cparams(("parallel", "parallel")),
    )(*ys, *ws, *[u_all] * nb)


def _rowwise(fn, rows, fulls, outs, *, name, tm=None):
    r = rows[0].shape[0]
    if tm is None:
        widest = max([x.shape[1] for x in rows] + [o[0] for o in outs])
        tm = _pick(r, (max(8, min(512, (512 * 1024) // widest)), 256, 128, 64, 32, 16, 8))
    nrow, nfull, nout = len(rows), len(fulls), len(outs)

    def body(*refs):
        row_refs = refs[:nrow]
        full_refs = refs[nrow:nrow + nfull]
        out_refs = refs[nrow + nfull:]
        res = fn(*[x[...] for x in row_refs], *[x[...] for x in full_refs])
        if not isinstance(res, (tuple, list)):
            res = (res,)
        step = pl.program_id(0)
        for o_ref, o, spec in zip(out_refs, res, outs):
            if spec[2] == "row":
                o_ref[...] = o.astype(o_ref.dtype)
            else:
                @pl.when(step == 0)
                def _(o_ref=o_ref):
                    o_ref[...] = jnp.zeros_like(o_ref)
                o_ref[...] += o

    in_specs = [pl.BlockSpec((tm, x.shape[1]), lambda i, ob=_off(x, x.shape[1]): (i, ob)) for x in rows]
    in_specs += [pl.BlockSpec(x.shape, lambda i, nd=x.ndim: (0,) * nd) for x in fulls]
    out_specs, out_shape = [], []
    for c, dt, kind in outs:
        if kind == "row":
            out_specs.append(pl.BlockSpec((tm, c), lambda i: (i, 0)))
            out_shape.append(jax.ShapeDtypeStruct((r, c), dt))
        else:
            out_specs.append(pl.BlockSpec((1, c), lambda i: (0, 0)))
            out_shape.append(jax.ShapeDtypeStruct((1, c), F32))
    res = pl.pallas_call(
        body, name=name, grid=(r // tm,), in_specs=in_specs, out_specs=out_specs, out_shape=out_shape,
        compiler_params=_cparams(("arbitrary",)),
    )(*[_arr(x) for x in rows], *fulls)
    return res[0] if nout == 1 else tuple(res)


def _colsum(x):
    return jnp.sum(x, axis=0, keepdims=True)


def _rms_parts(x, n_real):
    r = lax.rsqrt(jnp.sum(x * x, axis=-1, keepdims=True) * (1.0 / n_real) + EPS)
    return x * r, r


def _rms_fwd(x, g, *, n_real=None, out_dtype=BF16, name):
    n_real = n_real or x.shape[1]

    def fn(xv, gv):
        xh, _ = _rms_parts(xv, n_real)
        return xh * gv

    return _rowwise(fn, [x], [g], [(x.shape[1], out_dtype, "row")], name=name)


def _rms_bwd_math(xv, gv, dh, n_real):
    xh, r = _rms_parts(xv, n_real)
    dxh = dh * gv
    dx = r * (dxh - xh * (jnp.sum(dxh * xh, axis=-1, keepdims=True) * (1.0 / n_real)))
    return dx, _colsum(dh * xh)


def _rms_bwd(x, g, dh, res=None, *, name):
    n = x.shape[1]
    if res is None:
        def fn(xv, dhv, gv):
            return _rms_bwd_math(xv, gv, dhv.astype(F32), n)
        rows = [x, dh]
    else:
        def fn(xv, dhv, rv, gv):
            dx, dg = _rms_bwd_math(xv, gv, dhv.astype(F32), n)
            return dx + rv, dg
        rows = [x, dh, res]
    return _rowwise(fn, rows, [g], [(n, F32, "row"), (n, F32, "acc")], name=name)


def _seq_call(body, ins, outs, n_blocks, *, name):
    in_specs, args = [], []
    for x, kind in ins:
        in_specs.append(pl.BlockSpec((x.shape[0], LANES), lambda j, ob=_off(x, LANES): (0, j + ob)))
        args.append(_arr(x))
    out_specs, out_shape = [], []
    for shape, dt in outs:
        out_specs.append(pl.BlockSpec((shape[0], LANES), lambda j: (0, j)))
        out_shape.append(jax.ShapeDtypeStruct(shape, dt))
    res = pl.pallas_call(body, name=name, grid=(n_blocks,), in_specs=in_specs, out_specs=out_specs,
                         out_shape=out_shape, compiler_params=_cparams(("parallel",)))(*args)
    return res[0] if len(outs) == 1 else tuple(res)


def _conv_pre(x, w, b, row):
    acc = x * w[CONV_W - 1:CONV_W, :] + b
    for k in range(CONV_W - 1):
        acc = acc + _shift_down(x, CONV_W - 1 - k, row) * w[k:k + 1, :]
    return acc


def _conv_fwd(x, w, b, *, silu, name):
    s, c = x.shape

    def body(x_ref, w_ref, b_ref, y_ref):
        xv = x_ref[...]
        row = lax.broadcasted_iota(jnp.int32, xv.shape, 0)
        pre = _conv_pre(xv, w_ref[...], b_ref[...], row)
        y_ref[...] = _silu(pre) if silu else pre

    return _seq_call(body, [(x, "seq"), (w, "par"), (b, "par")], [((s, c), F32)], c // LANES, name=name)


def _conv_bwd(x, w, b, dy, *, silu, name):
    s, c = x.shape

    def body(x_ref, w_ref, b_ref, dy_ref, dx_ref, dw_ref, db_ref):
        xv, wv, dv = x_ref[...], w_ref[...], dy_ref[...]
        row = lax.broadcasted_iota(jnp.int32, xv.shape, 0)
        if silu:
            dv = dv * _silu_grad(_conv_pre(xv, wv, b_ref[...], row))
        dx = dv * wv[CONV_W - 1:CONV_W, :]
        dws = [None] * CONV_W
        dws[CONV_W - 1] = _colsum(dv * xv)
        for k in range(CONV_W - 1):
            sh = CONV_W - 1 - k
            dx = dx + _shift_up(dv, sh, row) * wv[k:k + 1, :]
            dws[k] = _colsum(dv * _shift_down(xv, sh, row))
        dx_ref[...] = dx
        for k in range(CONV_W):
            dw_ref[k:k + 1, :] = dws[k]
        db_ref[...] = _colsum(dv)

    return _seq_call(body, [(x, "seq"), (w, "par"), (b, "par"), (dy, "seq")],
                     [((s, c), F32), ((CONV_W, c), F32), ((1, c), F32)], c // LANES, name=name)


def _pool_select(levels):
    g = pl.program_id(0)
    return jnp.where(g == 0, levels[0], jnp.where(g == 1, levels[1], jnp.where(g == 2, levels[2], levels[3])))


def _pool_count(row):
    g = pl.program_id(0)
    w = jnp.where(g == 0, POOL_WINDOWS[0], jnp.where(g == 1, POOL_WINDOWS[1],
                                                     jnp.where(g == 2, POOL_WINDOWS[2], POOL_WINDOWS[3])))
    return jnp.minimum(row + 1, w).astype(F32)


def _pool_fwd(u, *, name):
    def body(u_ref, d_ref):
        uv = u_ref[...]
        row = lax.broadcasted_iota(jnp.int32, uv.shape, 0)
        levels, cur, sh = [], uv, 1
        for _ in POOL_WINDOWS:
            cur = cur + _shift_down(cur, sh, row)
            levels.append(cur)
            sh *= 2
        d_ref[...] = _pool_select(levels) / _pool_count(row) - uv

    return _seq_call(body, [(u, "seq")], [(u.shape, F32)], u.shape[1] // LANES, name=name)


def _pool_bwd(dd, *, name):
    def body(dd_ref, du_ref):
        dv = dd_ref[...]
        row = lax.broadcasted_iota(jnp.int32, dv.shape, 0)
        levels, cur, sh = [], dv / _pool_count(row), 1
        for _ in POOL_WINDOWS:
            cur = cur + _shift_up(cur, sh, row)
            levels.append(cur)
            sh *= 2
        du_ref[...] = _pool_select(levels) - dv

    return _seq_call(body, [(dd, "seq")], [(dd.shape, F32)], dd.shape[1] // LANES, name=name)


def _lru_gates(pre_a, pre_i, xc, lam, b_a, b_i):
    r = _sigmoid(pre_a + b_a)
    i = _sigmoid(pre_i + b_i)
    sp = _softplus(-lam)
    log_a = -LRU_C * r * sp
    a = jnp.exp(log_a)
    mult = jnp.sqrt(_neg_expm1(2.0 * log_a))
    return r, i, sp, a, mult


def _lru_fwd(pre, xc, gate_in, lam, b_a, b_i, *, name):
    s, c = xc.shape
    nb = c // LANES

    def body(pa_ref, pi_ref, xc_ref, g_ref, lam_ref, ba_ref, bi_ref, y_ref, h_ref):
        xv = xc_ref[...]
        row = lax.broadcasted_iota(jnp.int32, xv.shape, 0)
        _, i, _, a, mult = _lru_gates(pa_ref[...], pi_ref[...], xv, lam_ref[...], ba_ref[...], bi_ref[...])
        h = xv * i * mult
        d = 1
        while d < s:
            h = h + a * _shift_down(h, d, row)
            a = a * jnp.where(row >= d, pltpu.roll(a, d, 0), 1.0)
            d *= 2
        h_ref[...] = h
        y_ref[...] = h * _gelu(g_ref[...])

    blk = lambda off: pl.BlockSpec((s, LANES), lambda j: (0, j + off))
    par = pl.BlockSpec((1, LANES), lambda j: (0, j))
    return pl.pallas_call(
        body, name=name, grid=(nb,),
        in_specs=[blk(0), blk(nb), blk(0), blk(_off(gate_in, LANES)), par, par, par],
        out_specs=[blk(0), blk(0)],
        out_shape=[jax.ShapeDtypeStruct((s, c), F32)] * 2,
        compiler_params=_cparams(("parallel",)),
    )(pre, pre, xc, _arr(gate_in), lam, b_a, b_i)


def _lru_bwd(pre, xc, gate_in, lam, b_a, b_i, h, dy, *, name):
    s, c = xc.shape
    nb = c // LANES

    def body(pa_ref, pi_ref, xc_ref, g_ref, lam_ref, ba_ref, bi_ref, h_ref, dy_ref,
             dpa_ref, dpi_ref, dxc_ref, dg_ref, dlam_ref, dba_ref, dbi_ref):
        xv, gv, hv, dv = xc_ref[...], g_ref[...], h_ref[...], dy_ref[...]
        row = lax.broadcasted_iota(jnp.int32, xv.shape, 0)
        r, i, sp, a, mult = _lru_gates(pa_ref[...], pi_ref[...], xv, lam_ref[...], ba_ref[...], bi_ref[...])
        dg_ref[...] = dv * hv * _gelu_grad(gv)
        dh = dv * _gelu(gv)
        an = jnp.where(row < s - 1, pltpu.roll(a, s - 1, 0), 0.0)
        d = 1
        while d < s:
            dh = dh + an * _shift_up(dh, d, row)
            an = an * jnp.where(row < s - d, pltpu.roll(an, s - d, 0), 1.0)
            d *= 2
        da = dh * _shift_down(hv, 1, row)
        dxc_ref[...] = dh * i * mult
        di = dh * xv * mult
        dmult = dh * xv * i
        dlog_a = (da - dmult * a / mult) * a
        dr = dlog_a * (-LRU_C) * sp
        dlam_ref[...] = _colsum(dlog_a * LRU_C * r * _sigmoid(-lam_ref[...]))
        dpa = dr * r * (1.0 - r)
        dpi = di * i * (1.0 - i)
        dpa_ref[...] = dpa
        dpi_ref[...] = dpi
        dba_ref[...] = _colsum(dpa)
        dbi_ref[...] = _colsum(dpi)

    blk = lambda off: pl.BlockSpec((s, LANES), lambda j: (0, j + off))
    par = pl.BlockSpec((1, LANES), lambda j: (0, j))
    sc = jax.ShapeDtypeStruct((s, c), F32)
    pc = jax.ShapeDtypeStruct((1, c), F32)
    dpa, dpi, dxc, dg, dlam, dba, dbi = pl.pallas_call(
        body, name=name, grid=(nb,),
        in_specs=[blk(0), blk(nb), blk(0), blk(_off(gate_in, LANES)), par, par, par, blk(0), blk(0)],
        out_specs=[blk(0), blk(0), blk(0), blk(0), par, par, par],
        out_shape=[sc, sc, sc, sc, pc, pc, pc],
        compiler_params=_cparams(("parallel",)),
    )(pre, pre, xc, _arr(gate_in), lam, b_a, b_i, h, dy)
    return dpa, dpi, dxc, dg, dlam, dba, dbi


GROUP_HEADS = 4


def _ssd_specs(nc, order):
    gw = GROUP_HEADS * LANES
    return dict(
        x=pl.BlockSpec((SSD_CHUNK, gw), lambda g, ci: (order(ci), g)),
        b=pl.BlockSpec((SSD_CHUNK, LANES), lambda g, ci: (order(ci), N_HEADS + g)),
        c=pl.BlockSpec((SSD_CHUNK, LANES), lambda g, ci: (order(ci), N_HEADS + 2 + g)),
        dtcol=pl.BlockSpec((GROUP_HEADS, SSD_CHUNK, 1), lambda g, ci: (g, order(ci), 0)),
        dtrow=pl.BlockSpec((GROUP_HEADS, 1, SSD_CHUNK), lambda g, ci: (g, 0, order(ci))),
        scal=pl.BlockSpec((GROUP_HEADS, 1, 1), lambda g, ci: (g, 0, 0)),
        state=pl.BlockSpec((GROUP_HEADS, 1, LANES, LANES), lambda g, ci: (g, order(ci), 0, 0)),
        group=pl.BlockSpec((SSD_CHUNK, LANES), lambda g, ci: (order(ci), g)),
        pacc=pl.BlockSpec((GROUP_HEADS, 1, LANES), lambda g, ci: (g, 0, 0)),
    )


def _ssd_chunk_terms(dtcol, dtrow, bias, a_log):
    shp = (SSD_CHUNK, SSD_CHUNK)
    row = lax.broadcasted_iota(jnp.int32, shp, 0)
    col = lax.broadcasted_iota(jnp.int32, shp, 1)
    a_head = -jnp.exp(a_log)
    dt_c = jnp.broadcast_to(_softplus(dtcol + bias), shp)
    dt_r = jnp.broadcast_to(_softplus(dtrow + bias), shp)
    cs_c = _cumsum_rows(dt_c * a_head, row)
    cs_r = _cumsum_lanes(dt_r * a_head, col)
    cs_last = jnp.sum(jnp.where(row == SSD_CHUNK - 1, cs_c, 0.0), axis=0, keepdims=True)
    return row, col, a_head, dt_c, cs_c, cs_r, cs_last


def _ssd_fwd(xbc, dtcol, dtrow, bias, a_log, dskip, *, name):
    s = xbc.shape[0]
    nc = s // SSD_CHUNK

    def body(x_ref, b_ref, c_ref, dtc_ref, dtr_ref, bias_ref, alog_ref, d_ref, y_ref, st_ref, state):
        ci = pl.program_id(1)

        @pl.when(ci == 0)
        def _():
            state[...] = jnp.zeros_like(state)

        bm, cm = b_ref[...], c_ref[...]
        cb = _dot(cm, bm, tb=True)
        bm_t = bm.T
        for r in range(GROUP_HEADS):
            lanes = slice(r * LANES, (r + 1) * LANES)
            xv = x_ref[:, lanes]
            row, col, _, dt_c, cs_c, cs_r, cs_last = _ssd_chunk_terms(dtc_ref[r], dtr_ref[r], bias_ref[r], alog_ref[r])
            g = cb * jnp.exp(jnp.where(col <= row, cs_c - cs_r, -jnp.inf))
            xdt = xv * dt_c
            st = state[r]
            st_ref[r, 0] = st
            y_ref[:, lanes] = _dot(g, xdt) + _dot(cm, st) * jnp.exp(cs_c) + xv * d_ref[r]
            state[r] = jnp.exp(cs_last) * st + _dot(bm_t, xdt * jnp.exp(cs_last - cs_c))

    sp = _ssd_specs(nc, lambda ci: ci)
    return pl.pallas_call(
        body, name=name, grid=(N_HEADS // GROUP_HEADS, nc),
        in_specs=[sp["x"], sp["b"], sp["c"], sp["dtcol"], sp["dtrow"], sp["scal"], sp["scal"], sp["scal"]],
        out_specs=[sp["x"], sp["state"]],
        out_shape=[jax.ShapeDtypeStruct((s, N_HEADS * LANES), F32),
                   jax.ShapeDtypeStruct((N_HEADS, nc, LANES, LANES), F32)],
        scratch_shapes=[pltpu.VMEM((GROUP_HEADS, LANES, LANES), F32)],
        compiler_params=_cparams(("parallel", "arbitrary")),
    )(xbc, xbc, xbc, dtcol, dtrow, bias, a_log, dskip)


def _ssd_bwd(xbc, dtcol, dtrow, bias, a_log, dskip, states, dy, *, name):
    s = xbc.shape[0]
    nc = s // SSD_CHUNK

    def body(x_ref, b_ref, c_ref, dtc_ref, dtr_ref, bias_ref, alog_ref, d_ref, st_ref, dy_ref,
             dx_ref, db_ref, dc_ref, ddt_ref, dbias_ref, dalog_ref, dd_ref, dstate):
        ci = pl.program_id(1)

        @pl.when(ci == 0)
        def _():
            dstate[...] = jnp.zeros_like(dstate)
            dbias_ref[...] = jnp.zeros_like(dbias_ref)
            dalog_ref[...] = jnp.zeros_like(dalog_ref)
            dd_ref[...] = jnp.zeros_like(dd_ref)

        bm, cm = b_ref[...], c_ref[...]
        cb = _dot(cm, bm, tb=True)
        cb_t = _dot(bm, cm, tb=True)
        cm_t = cm.T
        rowsum = lambda v: jnp.sum(v, axis=1, keepdims=True)
        tot = lambda v: jnp.broadcast_to(jnp.sum(v, axis=0, keepdims=True), (1, LANES))
        dbm_sum, dcm_sum = None, None
        for r in range(GROUP_HEADS):
            lanes = slice(r * LANES, (r + 1) * LANES)
            xv, dyv, st = x_ref[:, lanes], dy_ref[:, lanes], st_ref[r, 0]
            dtraw_c, bias = dtc_ref[r], bias_ref[r]
            row, col, a_head, dt_c, cs_c, cs_r, cs_last = _ssd_chunk_terms(dtraw_c, dtr_ref[r], bias, alog_ref[r])
            lmat = jnp.exp(jnp.where(col <= row, cs_c - cs_r, -jnp.inf))
            lmat_t = jnp.exp(jnp.where(row <= col, cs_r - cs_c, -jnp.inf))
            g, g_t = cb * lmat, cb_t * lmat_t
            xdt = xv * dt_c
            e_c = jnp.exp(cs_c)
            f_c = jnp.exp(cs_last - cs_c)
            e_last = jnp.exp(cs_last)
            w = xdt * f_c
            dst = dstate[r]

            dg = _dot(dyv, xdt, tb=True)
            dg_t = _dot(xdt, dyv, tb=True)
            dxdt = _dot(g_t, dyv)
            dcs = rowsum(dg * g) - rowsum(dg_t * g_t)
            dcm = _dot(dg * lmat, bm)
            dbm = _dot(dg_t * lmat_t, cm)
            z = _dot(cm, st)
            dz = dyv * e_c
            dcs = dcs + rowsum(dz * z)
            dcm = dcm + _dot(dz, st, tb=True)
            dstate[r] = _dot(cm_t, dz) + e_last * dst
            dcs_last = jnp.sum(rowsum(dst * st), axis=0, keepdims=True) * jnp.max(e_last, axis=1, keepdims=True)
            dbm = dbm + _dot(w, dst, tb=True)
            dw = _dot(bm, dst)
            dxdt = dxdt + dw * f_c
            q = rowsum(dw * w)
            dcs = dcs - q
            dcs_last = dcs_last + jnp.sum(q, axis=0, keepdims=True)
            dx_ref[:, lanes] = dxdt * dt_c + dyv * d_ref[r]
            ddt = rowsum(dxdt * xv)
            dcs_full = jnp.broadcast_to(dcs, (SSD_CHUNK, SSD_CHUNK)) + jnp.where(row == SSD_CHUNK - 1, dcs_last, 0.0)
            da = jnp.max(_rev_cumsum_rows(dcs_full, row), axis=1, keepdims=True)
            dt_col = jnp.max(dt_c, axis=1, keepdims=True)
            draw = (ddt + da * a_head) * _sigmoid(dtraw_c + bias)
            ddt_ref[r] = draw
            dbias_ref[r] += tot(draw)
            dalog_ref[r] += tot(da * dt_col) * a_head
            dd_ref[r] += tot(rowsum(dyv * xv))
            dbm_sum = dbm if dbm_sum is None else dbm_sum + dbm
            dcm_sum = dcm if dcm_sum is None else dcm_sum + dcm
        db_ref[...] = dbm_sum
        dc_ref[...] = dcm_sum

    sp = _ssd_specs(nc, lambda ci: nc - 1 - ci)
    return pl.pallas_call(
        body, name=name, grid=(N_HEADS // GROUP_HEADS, nc),
        in_specs=[sp["x"], sp["b"], sp["c"], sp["dtcol"], sp["dtrow"], sp["scal"], sp["scal"], sp["scal"],
                  sp["state"], sp["x"]],
        out_specs=[sp["x"], sp["group"], sp["group"], sp["dtcol"], sp["pacc"], sp["pacc"], sp["pacc"]],
        out_shape=[jax.ShapeDtypeStruct((s, N_HEADS * LANES), F32),
                   jax.ShapeDtypeStruct((s, 2 * LANES), F32),
                   jax.ShapeDtypeStruct((s, 2 * LANES), F32),
                   jax.ShapeDtypeStruct((N_HEADS, s, 1), F32),
                   jax.ShapeDtypeStruct((N_HEADS, 1, LANES), F32),
                   jax.ShapeDtypeStruct((N_HEADS, 1, LANES), F32),
                   jax.ShapeDtypeStruct((N_HEADS, 1, LANES), F32)],
        scratch_shapes=[pltpu.VMEM((GROUP_HEADS, LANES, LANES), F32)],
        compiler_params=_cparams(("parallel", "arbitrary")),
    )(xbc, xbc, xbc, dtcol, dtrow, bias, a_log, dskip, states, dy)


def _att_tile(s):
    return _pick(s, (512, 256, 128))


def _tri(t, transposed=False):
    r = lax.broadcasted_iota(jnp.int32, (t, t), 0)
    c = lax.broadcasted_iota(jnp.int32, (t, t), 1)
    return (r <= c) if transposed else (c <= r)


def _rows_at(ref, blk, t):
    return ref[pl.ds(pl.multiple_of(blk * t, t), t), :]


def _flash_fwd(q, k, v, *, name):
    s = q.shape[0]
    t = _att_tile(s)
    nq = s // t

    def body(q_ref, k_ref, v_ref, o_ref, lse_ref):
        i = pl.program_id(1)
        qv = q_ref[...]

        def step(j, carry, diagonal):
            m_old, l_old, acc = carry
            sc = _dot(qv, _rows_at(k_ref, j, t), tb=True)
            if diagonal:
                sc = jnp.where(_tri(t), sc, -jnp.inf)
            m_new = jnp.maximum(m_old, jnp.max(sc, axis=1, keepdims=True))
            alpha = jnp.exp(m_old - m_new)
            p = jnp.exp(sc - m_new)
            return (m_new, alpha * l_old + jnp.sum(p, axis=1, keepdims=True),
                    alpha * acc + _dot(p, _rows_at(v_ref, j, t)))

        init = (jnp.full((t, 1), -jnp.inf, F32), jnp.zeros((t, 1), F32), jnp.zeros((t, LANES), F32))
        carry = lax.fori_loop(0, i, lambda j, c: step(j, c, False), init)
        m_fin, l_fin, acc = step(i, carry, True)
        o_ref[...] = (acc / l_fin).astype(o_ref.dtype)
        lse_ref[0] = m_fin + jnp.log(l_fin)

    q_spec = pl.BlockSpec((t, LANES), lambda h, i: (i, h))
    kv_spec = pl.BlockSpec((s, LANES), lambda h, i: (0, h))
    return pl.pallas_call(
        body, name=name, grid=(N_HEADS, nq),
        in_specs=[q_spec, kv_spec, kv_spec],
        out_specs=[q_spec, pl.BlockSpec((1, t, 1), lambda h, i: (h, i, 0))],
        out_shape=[jax.ShapeDtypeStruct(q.shape, BF16), jax.ShapeDtypeStruct((N_HEADS, s, 1), F32)],
        compiler_params=_cparams(("parallel", "arbitrary")),
    )(q, k, v)


def _att_delta(o, do, *, name):
    s = o.shape[0]
    t = _att_tile(s)

    def body(o_ref, do_ref, dl_ref):
        dl_ref[0] = jnp.sum(do_ref[...].astype(F32) * o_ref[...].astype(F32), axis=1, keepdims=True)

    blk = pl.BlockSpec((t, LANES), lambda h, i: (i, h))
    return pl.pallas_call(
        body, name=name, grid=(N_HEADS, s // t), in_specs=[blk, blk],
        out_specs=pl.BlockSpec((1, t, 1), lambda h, i: (h, i, 0)),
        out_shape=jax.ShapeDtypeStruct((N_HEADS, s, 1), F32),
        compiler_params=_cparams(("parallel", "parallel")),
    )(o, do)


def _flash_bwd(q, k, v, do, lse_row, delta_row, *, name):
    s = q.shape[0]
    t = _att_tile(s)
    nq = s // t

    def body(q_ref, k_ref, v_ref, do_ref, lse_ref, dl_ref, dq_ref, dk_ref, dv_ref):
        j = pl.program_id(1)
        kv, vv = k_ref[...], v_ref[...]

        @pl.when(j == 0)
        def _():
            dq_ref[...] = jnp.zeros_like(dq_ref)

        def step(i, carry, diagonal):
            dk, dv = carry
            rows = pl.ds(pl.multiple_of(i * t, t), t)
            qi, doi = q_ref[rows, :], do_ref[rows, :]
            p_t = jnp.exp(_dot(kv, qi, tb=True) - lse_ref[0, :, rows])
            if diagonal:
                p_t = jnp.where(_tri(t, transposed=True), p_t, 0.0)
            ds_t = (p_t * (_dot(vv, doi, tb=True) - dl_ref[0, :, rows])).astype(MXU_DTYPE)
            dq_ref[rows, :] += _dot(ds_t, kv, ta=True)
            return dk + _dot(ds_t, qi), dv + _dot(p_t, doi)

        zero = jnp.zeros((t, LANES), F32)
        carry = step(j, (zero, zero), True)
        dk, dv = lax.fori_loop(j + 1, nq, lambda i, c: step(i, c, False), carry)
        dk_ref[...] = dk
        dv_ref[...] = dv

        @pl.when(j == nq - 1)
        def _():
            dq_ref[...] = dq_ref[...] * ATT_SCALE

    q_spec = pl.BlockSpec((s, LANES), lambda h, j: (0, h))
    kv_spec = pl.BlockSpec((t, LANES), lambda h, j: (j, h))
    row_spec = pl.BlockSpec((1, 1, s), lambda h, j: (h, 0, 0))
    return pl.pallas_call(
        body, name=name, grid=(N_HEADS, nq),
        in_specs=[q_spec, kv_spec, kv_spec, q_spec, row_spec, row_spec],
        out_specs=[q_spec, kv_spec, kv_spec],
        out_shape=[jax.ShapeDtypeStruct(q.shape, F32)] * 3,
        compiler_params=_cparams(("parallel", "arbitrary")),
    )(q, k, v, do, lse_row, delta_row)


def _rope(v, cos_t, sin_p, sin_m):
    return v * cos_t + pltpu.roll(v, QK_ROPE // 2, 1) * sin_p + pltpu.roll(v, LANES - QK_ROPE // 2, 1) * sin_m


def _rope_t(d, cos_t, sin_p, sin_m):
    return d * cos_t + pltpu.roll(d * sin_p, LANES - QK_ROPE // 2, 1) + pltpu.roll(d * sin_m, QK_ROPE // 2, 1)


def _att_prep(q_pad, kv2, kr, cos_t, sin_p, sin_m, *, name):
    w = N_HEADS * LANES

    def fn(qv, kvv, krv, c, sp, sm):
        kr_rot = _rope(krv, c, sp, sm)
        qs, ks = [], []
        for h in range(N_HEADS):
            blk = slice(h * LANES, (h + 1) * LANES)
            qs.append(_rope(qv[:, blk], c, sp, sm) * ATT_SCALE)
            ks.append(kvv[:, blk] + kr_rot)
        return jnp.concatenate(qs, axis=1), jnp.concatenate(ks, axis=1), kvv[:, w:]

    return _rowwise(fn, [q_pad, kv2, kr, cos_t, sin_p, sin_m], [],
                    [(w, BF16, "row"), (w, BF16, "row"), (w, BF16, "row")], name=name)


def _att_prep_bwd(dq, dk, cos_t, sin_p, sin_m, *, name):
    w = N_HEADS * LANES

    def fn(dqv, dkv, c, sp, sm):
        outs, dkr = [], None
        for h in range(N_HEADS):
            blk = slice(h * LANES, (h + 1) * LANES)
            outs.append(_rope_t(dqv[:, blk], c, sp, sm))
            dkr = dkv[:, blk] if dkr is None else dkr + dkv[:, blk]
        return jnp.concatenate(outs, axis=1), _rope_t(dkr, c, sp, sm)

    return _rowwise(fn, [dq, dk, cos_t, sin_p, sin_m], [], [(w, BF16, "row"), (LANES, F32, "row")], name=name)


_ANY = pl.BlockSpec(memory_space=pl.ANY)
_MESH = pl.DeviceIdType.MESH


def _mesh_pos():
    return lax.axis_index("x"), lax.axis_index("y"), lax.axis_index("c")


def _remote(src, dst, send_sem, recv_sem, dev):
    return pltpu.make_async_remote_copy(src_ref=src, dst_ref=dst, send_sem=send_sem, recv_sem=recv_sem,
                                        device_id=dev, device_id_type=_MESH)


def _other_chips(x, y):
    chips = [(1 - x, y), (x, 1 - y), (1 - x, 1 - y)]
    return chips, [2 * cx + cy for cx, cy in chips]


def _comm_call(body, ins, out_shapes, n_sems, *, name):
    return pl.pallas_call(
        body, name=name, in_specs=[_ANY] * len(ins), out_specs=[_ANY] * len(out_shapes), out_shape=out_shapes,
        scratch_shapes=[pltpu.SemaphoreType.DMA((k,)) for k in n_sems],
    )(*ins)


def _gather_halves(shards):
    n = len(shards)
    halves = [t.shape[0] // 2 for t in shards]

    def body(*refs):
        xs, outs = refs[:n], refs[n:2 * n]
        send_sems, recv_sems = refs[2 * n:]
        x, y, c = _mesh_pos()
        k = 2 * x + y
        sibling = (x, y, 1 - c)
        chips, ks = _other_chips(x, y)
        half = lambda w, hf: pl.ds(hf * halves[w], halves[w])
        first = [_remote(xs[w].at[half(w, c)], outs[w].at[k, half(w, c)], send_sems.at[6 * w + j], recv_sems.at[6 * w + j],
                         (*chips[j], c)) for w in range(n) for j in range(3)]
        for cp in first:
            cp.start()
        passed = []
        for j in range(3):
            for w in range(n):
                land = outs[w].at[ks[j], half(w, c)]
                _remote(land, land, send_sems.at[6 * w + j], recv_sems.at[6 * w + j], sibling).wait_recv()
                passed.append(_remote(land, land, send_sems.at[6 * w + 3 + j], recv_sems.at[6 * w + 3 + j], sibling))
                passed[-1].start()
        for j in range(3):
            for w in range(n):
                land = outs[w].at[ks[j], half(w, 1 - c)]
                _remote(land, land, send_sems.at[6 * w + 3 + j], recv_sems.at[6 * w + 3 + j], sibling).wait_recv()
        for cp in first + passed:
            cp.wait_send()

    shapes = [jax.ShapeDtypeStruct((4,) + t.shape, t.dtype) for t in shards]
    return _comm_call(body, shards, shapes, (6 * n, 6 * n), name="gather_halves")


_HBM = pl.BlockSpec(memory_space=pltpu.HBM)
_SEM = pl.BlockSpec(memory_space=pltpu.SEMAPHORE)
_EFFECT = pltpu.SideEffectType.DATAFLOW_SIDE_EFFECTING


def _push_start(blocks, *, scatter, name):
    n = len(blocks)

    def body(*refs):
        xs, lands = refs[:n], refs[n:2 * n]
        send_sems, recv_sems = refs[2 * n], refs[2 * n + 1]
        token = refs[-1]
        x, y, c = _mesh_pos()
        k = 2 * x + y
        chips, ks = _other_chips(x, y)
        for w in range(n):
            for j in range(3):
                src = xs[w].at[ks[j]] if scatter else xs[w]
                _remote(src, lands[w].at[k], send_sems.at[3 * w + j], recv_sems.at[3 * w + j], (*chips[j], c)).start()
        token[...] = jnp.zeros_like(token)

    hbm = lambda shape, dtype: pltpu.with_memory_space_constraint(lax.empty(shape, dtype), pltpu.HBM)
    ins = [pltpu.with_memory_space_constraint(t, pltpu.HBM) for t in blocks]
    ins += [hbm(t.shape if scatter else (4,) + t.shape, t.dtype) for t in blocks]
    out_shape = [pltpu.SemaphoreType.DMA((3 * n,)), pltpu.SemaphoreType.DMA((3 * n,))]
    out_shape += [pltpu.HBM(t.shape, t.dtype) for t in ins]
    out_shape += [jax.ShapeDtypeStruct((8, LANES), F32)]
    res = pl.pallas_call(
        body, name=name, out_shape=out_shape, in_specs=[_HBM] * (2 * n),
        out_specs=[_SEM, _SEM] + [_HBM] * (2 * n) + [pl.BlockSpec(memory_space=pltpu.VMEM)],
        input_output_aliases={i: 2 + i for i in range(2 * n)},
        compiler_params=pltpu.CompilerParams(has_side_effects=_EFFECT),
    )(*ins)
    return res[0], res[1], res[2:2 + n], res[2 + n:2 + 2 * n], res[-1]


def _push_wait(send_sems, recv_sems, blocks, lands, after, *, name):
    n = len(blocks)

    def body(*refs):
        lands_in = refs[n:2 * n]
        send_sems, recv_sems = refs[2 * n], refs[2 * n + 1]
        x, y, c = _mesh_pos()
        chips, ks = _other_chips(x, y)
        for w in range(n):
            for j in range(3):
                slot = lands_in[w].at[ks[j]]
                cp = _remote(slot, slot, send_sems.at[3 * w + j], recv_sems.at[3 * w + j], (*chips[j], c))
                cp.wait_send()
                cp.wait_recv()

    out_shape = [pltpu.HBM(t.shape, t.dtype) for t in list(blocks) + list(lands)]
    res = pl.pallas_call(
        body, name=name, out_shape=out_shape,
        in_specs=[_HBM] * (2 * n) + [_SEM, _SEM, pl.BlockSpec(memory_space=pl.ANY)], out_specs=[_HBM] * (2 * n),
        input_output_aliases={i: i for i in range(2 * n)},
        compiler_params=pltpu.CompilerParams(has_side_effects=_EFFECT),
    )(*blocks, *lands, send_sems, recv_sems, after)
    return res[:n], res[n:]


def _send_half(views, *, name):
    n = len(views)

    def body(*refs):
        vs, outs = refs[:n], refs[n:2 * n]
        send_sems, recv_sems = refs[2 * n:]
        x, y, c = _mesh_pos()
        cps = []
        for w in range(n):
            h = views[w].shape[1] // 2
            cps.append(_remote(vs[w].at[:, pl.ds((1 - c) * h, h), :], outs[w], send_sems.at[w], recv_sems.at[w],
                               (x, y, 1 - c)))
            cps[-1].start()
        for cp in cps:
            cp.wait()

    shapes = [jax.ShapeDtypeStruct((t.shape[0], t.shape[1] // 2, t.shape[2]), t.dtype) for t in views]
    return _comm_call(body, views, shapes, (n, n), name=name)


def _swap_with_sibling(mine, *, name):
    n = len(mine)

    def body(*refs):
        hs, outs = refs[:n], refs[n:2 * n]
        send_sems, recv_sems = refs[2 * n:]
        x, y, c = _mesh_pos()
        cps = [_remote(hs[w], outs[w], send_sems.at[w], recv_sems.at[w], (x, y, 1 - c)) for w in range(n)]
        for cp in cps:
            cp.start()
        for cp in cps:
            cp.wait()

    shapes = [jax.ShapeDtypeStruct(t.shape, t.dtype) for t in mine]
    return _comm_call(body, mine, shapes, (n, n), name=name)


def _gather_all(vec, *, name):
    r, w = vec.shape

    def body(v_ref, out_ref, send_sems, recv_sems):
        x, y, c = _mesh_pos()

        def slot(px, py, pc):
            return out_ref.at[4 * px + 2 * py + pc]

        peers = []
        for rel in range(1, 8):
            fx, fy, fc = (rel >> 2) & 1, (rel >> 1) & 1, rel & 1
            peers.append((x ^ fx, y ^ fy, c ^ fc))
        cps = [_remote(v_ref, slot(x, y, c), send_sems.at[j], recv_sems.at[j], peer) for j, peer in enumerate(peers)]
        for cp in cps:
            cp.start()
        for j, peer in enumerate(peers):
            _remote(slot(*peer), slot(*peer), send_sems.at[j], recv_sems.at[j], peer).wait_recv()
        for cp in cps:
            cp.wait_send()

    others = pl.pallas_call(
        body, name=name, in_specs=[_ANY], out_specs=_ANY,
        out_shape=jax.ShapeDtypeStruct((8, r, w), vec.dtype),
        scratch_shapes=[pltpu.SemaphoreType.DMA((7,)), pltpu.SemaphoreType.DMA((7,))],
    )(vec)
    me = 4 * lax.axis_index("x") + 2 * lax.axis_index("y") + lax.axis_index("c")
    return lax.dynamic_update_index_in_dim(others, vec, me, 0)


def _row_tile(rows, row_bytes):
    for tm in (1024, 512, 256, 128, 64, 32, 16):
        if rows % tm == 0 and tm * row_bytes <= ELEMENTWISE_BLOCK_BYTES:
            return tm
    return 16 if rows % 16 == 0 else rows


def _chip_sum_half(g, got, c, *, name):
    nb, r, w = g.shape
    half = r // 2
    tm = _row_tile(half, w * 4)
    per = half // tm

    def body(c_ref, g_ref, o_ref, out_ref):
        out_ref[...] = (g_ref[...] + o_ref[...]).astype(out_ref.dtype)

    return pl.pallas_call(
        body, name=name,
        grid_spec=pltpu.PrefetchScalarGridSpec(
            num_scalar_prefetch=1, grid=(nb, per),
            in_specs=[pl.BlockSpec((1, tm, w), lambda b, i, c_ref: (b, c_ref[0] * per + i, 0)),
                      pl.BlockSpec((1, tm, w), lambda b, i, c_ref: (b, i, 0))],
            out_specs=pl.BlockSpec((1, tm, w), lambda b, i, c_ref: (b, i, 0))),
        out_shape=jax.ShapeDtypeStruct((nb, half, w), BF16),
        compiler_params=_cparams(("parallel", "parallel")),
    )(jnp.reshape(c, (1,)).astype(jnp.int32), g, got)


def _sum_slots(stack, *, name):
    n, r, w = stack.shape
    tm = _row_tile(r, n * w * stack.dtype.itemsize)

    def body(s_ref, out_ref):
        acc = s_ref[0].astype(F32)
        for i in range(1, n):
            acc = acc + s_ref[i].astype(F32)
        out_ref[...] = acc

    return pl.pallas_call(
        body, name=name, grid=(r // tm,),
        in_specs=[pl.BlockSpec((n, tm, w), lambda i: (0, i, 0))],
        out_specs=pl.BlockSpec((tm, w), lambda i: (i, 0)),
        out_shape=jax.ShapeDtypeStruct((r, w), F32),
        compiler_params=_cparams(("parallel",)),
    )(stack)


def _adam_math(wv, gv, mv, vv):
    m_new = ADAM_B1 * mv + (1.0 - ADAM_B1) * gv
    v_new = ADAM_B2 * vv + (1.0 - ADAM_B2) * (gv * gv)
    m_hat = m_new / (1.0 - ADAM_B1 ** ADAM_STEP)
    v_hat = v_new / (1.0 - ADAM_B2 ** ADAM_STEP)
    delta = -ADAM_LR * (m_hat / (jnp.sqrt(v_hat) + ADAM_EPS) + ADAM_WD * wv)
    return delta, m_new, v_new


def _adamw(w, g, m, v, *, name):
    shape = w.shape
    cols = shape[-1]
    flat = lambda t: t.reshape(-1, cols)
    rows = flat(w).shape[0]
    tm = _pick(rows, (256, 128, 64, 32, 16, 8))
    outs = _rowwise(_adam_math, [flat(w), flat(g), flat(m), flat(v)], [], [(cols, F32, "row")] * 3, name=name, tm=tm)
    return tuple(o.reshape(shape) for o in outs)


def _adamw_slots(w, slots, m, v, c, *, name):
    shape = w.shape
    cols = shape[-1]
    half = slots[0][0].shape[1]
    v4 = lambda t: t.reshape(2, 2, half, cols)
    assert all(s.shape == (4, half, cols) for pair in slots for s in pair) and w.size == 4 * half * cols
    tm = _row_tile(half, cols * 4 * 4)

    def body(c_ref, w_ref, m0_ref, o0_ref, m1_ref, o1_ref, m_ref, v_ref, g_ref, d_ref, mo_ref, vo_ref):
        first = pl.program_id(0) == 0
        own = pl.program_id(1) == c_ref[0]
        g = None
        for i in range(4):
            part = jnp.where(first, jnp.where(own, m0_ref[i], o0_ref[i]), jnp.where(own, m1_ref[i], o1_ref[i]))
            g = part.astype(F32) if g is None else g + part.astype(F32)
        delta, m_new, v_new = _adam_math(w_ref[0, 0], g, m_ref[0, 0], v_ref[0, 0])
        g_ref[0, 0], d_ref[0, 0], mo_ref[0, 0], vo_ref[0, 0] = g, delta, m_new, v_new

    blk = pl.BlockSpec((1, 1, tm, cols), lambda l, hf, i, c_ref: (l, hf, i, 0))

    def slot_spec(layer, mine):
        def index(l, hf, i, c_ref):
            same_half = hf * c_ref[0] + (1 - hf) * (1 - c_ref[0])
            use = (l if layer else 1 - l) * (same_half if mine else 1 - same_half)
            return (0, i * use, 0)
        return pl.BlockSpec((4, tm, cols), index)

    outs = pl.pallas_call(
        body, name=name,
        grid_spec=pltpu.PrefetchScalarGridSpec(
            num_scalar_prefetch=1, grid=(2, 2, half // tm),
            in_specs=[blk, slot_spec(0, True), slot_spec(0, False), slot_spec(1, True), slot_spec(1, False), blk, blk],
            out_specs=[blk] * 4),
        out_shape=[jax.ShapeDtypeStruct((2, 2, half, cols), F32)] * 4,
        compiler_params=_cparams(("arbitrary", "arbitrary", "arbitrary")),
    )(jnp.reshape(c, (1,)).astype(jnp.int32), v4(w), slots[0][0], slots[0][1], slots[1][0], slots[1][1], v4(m), v4(v))
    return tuple(o.reshape(shape) for o in outs)


def _pad_blocks(w, axis, n_blocks, real, to=LANES, offset=0):
    axis = axis % w.ndim
    shp = w.shape
    w = w.reshape(shp[:axis] + (n_blocks, real) + shp[axis + 1:])
    pads = [(0, 0)] * w.ndim
    pads[axis + 1] = (offset, to - real - offset)
    w = jnp.pad(w, pads)
    return w.reshape(shp[:axis] + (n_blocks * to,) + shp[axis + 1:])


def _unpad_blocks(w, axis, n_blocks, real, to=LANES, offset=0):
    axis = axis % w.ndim
    shp = w.shape
    w = w.reshape(shp[:axis] + (n_blocks, to) + shp[axis + 1:])
    w = lax.slice_in_dim(w, offset, offset + real, axis=axis + 1)
    return w.reshape(shp[:axis] + (n_blocks * real,) + shp[axis + 1:])


def _block_diag(w):
    n, a, b = w.shape
    eye = jnp.eye(n, dtype=w.dtype)
    return (eye[:, None, :, None] * w[:, :, None, :]).reshape(n * a, n * b)


def _block_diag_t(d, n):
    a, b = d.shape[0] // n, d.shape[1] // n
    d = d.reshape(n, a, n, b)
    return jnp.stack([d[i, :, i, :] for i in range(n)])


_SPLITS = np.cumsum((0,) + SPLIT_SIZES)


def _w_in_groups(w_in):
    sl = lambda i: w_in[:, _SPLITS[i]:_SPLITS[i + 1]]
    xbc = sl(5)
    xbc_pad = jnp.concatenate([_pad_blocks(xbc[:, :MIX], 1, N_HEADS, HEAD),
                               _pad_blocks(xbc[:, MIX:MIX + 2 * HEAD], 1, 2, HEAD),
                               _pad_blocks(xbc[:, MIX + 2 * HEAD:], 1, 2, HEAD)], axis=1)
    return dict(
        cq=sl(0), ckv=sl(1), kr=_pad_blocks(sl(2), 1, 1, QK_ROPE, offset=HEAD), pool=sl(3),
        z=_pad_blocks(sl(4), 1, N_HEADS, HEAD), xbc=xbc_pad, dt=_pad_blocks(sl(6), 1, 1, N_HEADS),
        lru_g=sl(7), lru_x=sl(8), gates=sl(9))


def _w_in_fused(groups):
    parts, at = [], 0
    for name, off, width in IN_LAYOUT:
        assert groups[name].shape[1] == width and off >= at
        if off > at:
            parts.append(jnp.zeros((groups[name].shape[0], off - at), groups[name].dtype))
        parts.append(groups[name])
        at = off + width
    parts.append(jnp.zeros((parts[0].shape[0], IN_ALL_COLS - at), parts[0].dtype))
    return jnp.concatenate(parts, axis=1)


def _in_cols(arr, name):
    off, width = IN_OFFSETS[name]
    return _Cols(arr, off, width)


def _w_in_ungroup(d):
    xbc = d["xbc"]
    w = N_HEADS * LANES
    xbc_real = jnp.concatenate([_unpad_blocks(xbc[:, :w], 1, N_HEADS, HEAD),
                                _unpad_blocks(xbc[:, w:w + 2 * LANES], 1, 2, HEAD),
                                _unpad_blocks(xbc[:, w + 2 * LANES:], 1, 2, HEAD)], axis=1)
    return jnp.concatenate([d["cq"], d["ckv"], _unpad_blocks(d["kr"], 1, 1, QK_ROPE, offset=HEAD), d["pool"],
                            _unpad_blocks(d["z"], 1, N_HEADS, HEAD), xbc_real, _unpad_blocks(d["dt"], 1, 1, N_HEADS),
                            d["lru_g"], d["lru_x"], d["gates"]], axis=1)


def _pad_xbc_vec(v):
    return jnp.concatenate([_pad_blocks(v[..., :MIX], -1, N_HEADS, HEAD),
                            _pad_blocks(v[..., MIX:MIX + 2 * HEAD], -1, 2, HEAD),
                            _pad_blocks(v[..., MIX + 2 * HEAD:], -1, 2, HEAD)], axis=-1)


def _unpad_xbc_vec(v):
    w = N_HEADS * LANES
    return jnp.concatenate([_unpad_blocks(v[..., :w], -1, N_HEADS, HEAD),
                            _unpad_blocks(v[..., w:w + 2 * LANES], -1, 2, HEAD),
                            _unpad_blocks(v[..., w + 2 * LANES:], -1, 2, HEAD)], axis=-1)


def _layer_weights(p):
    q = dict(p)
    q["in_all"] = _w_in_fused(_w_in_groups(p["w_in"]))
    q["uq"] = _pad_blocks(p["w_uq"], 1, N_HEADS, HEAD + QK_ROPE)
    ukv = p["w_ukv"].reshape(KV_LORA, N_HEADS, 2 * HEAD)
    q["ukv"] = jnp.concatenate([_pad_blocks(ukv[:, :, :HEAD].reshape(KV_LORA, -1), 1, N_HEADS, HEAD),
                                _pad_blocks(ukv[:, :, HEAD:].reshape(KV_LORA, -1), 1, N_HEADS, HEAD)], axis=1)
    q["pool_bd"] = _block_diag(p["w_pool"])
    q["lru_bd"] = jnp.concatenate([_block_diag(p["lru_w_a"]), _block_diag(p["lru_w_i"])], axis=1)
    q["br"] = [_pad_blocks(p["w_branch"][0], 0, N_HEADS, HEAD), p["w_branch"][1],
               _pad_blocks(p["w_branch"][2], 0, N_HEADS, HEAD), p["w_branch"][3]]
    q["ssd_conv_w_pad"] = _pad_xbc_vec(p["ssd_conv_w"])
    q["ssd_conv_b_pad"] = _pad_xbc_vec(p["ssd_conv_b"])[None, :]
    q["ssd_norm_pad"] = _pad_blocks(p["ssd_norm"], 0, N_HEADS, HEAD)[None, :]
    return q


def _row(v):
    return v.reshape(1, -1)


def _scal3(v):
    return v.reshape(N_HEADS, 1, 1)


def _in_proj(x, g_mix, in_all, tag):
    h = _rms_fwd(x, _row(g_mix), name=f"rms_mix_{tag}")
    return h, _mm(h, in_all, name=f"in_proj_{tag}")


def _layer_fwd(x, p_emb, w, rope, tag, pre=None):
    n = lambda s: f"{s}_{tag}"
    sv = {"x": x}
    h, u_all = pre if pre is not None else _in_proj(x, w["g_mix"], w["in_all"], tag)
    sv["h"] = h
    u = {k: _in_cols(u_all, k) for k in IN_OFFSETS}
    sv["u"] = u

    cqn = _rms_fwd(u["cq"], _row(w["q_norm"]), name=n("rms_q"))
    ckvn = _rms_fwd(u["ckv"], _row(w["kv_norm"]), name=n("rms_kv"))
    q_pad = _mm(cqn, w["uq"], name=n("uq"))
    kv2 = _mm(ckvn, w["ukv"], name=n("ukv"))
    qc, kc, vc = _att_prep(q_pad, kv2, u["kr"], *rope, name=n("att_prep"))
    y_a, lse = _flash_fwd(qc, kc, vc, name=n("flash_fwd"))
    sv.update(cqn=cqn, ckvn=ckvn, qc=qc, kc=kc, vc=vc, y_a=y_a, lse=lse)

    pool_d = _pool_fwd(u["pool"], name=n("pool_fwd"))
    yb_pre, y_b = _mm(pool_d, w["pool_bd"], epilogue=lambda acc, sc: (acc, acc * sc),
                      rowvecs=[_row(w["pool_scale"])], out_dtypes=(F32, BF16), name=n("pool_mm"))
    sv.update(pool_d=pool_d, yb_pre=yb_pre, y_b=y_b)

    xbc_c = _conv_fwd(u["xbc"], w["ssd_conv_w_pad"], w["ssd_conv_b_pad"], silu=True, name=n("ssd_conv"))
    dt8 = lax.slice_in_dim(u_all, IN_OFFSETS["dt"][0], IN_OFFSETS["dt"][0] + N_HEADS, axis=1)
    dtcol = dt8.T[:, :, None]
    dtrow = dt8.T[:, None, :]
    ssd_par = (_scal3(w["ssd_dt_bias"]), _scal3(w["ssd_a_log"]), _scal3(w["ssd_d"]))
    y_ssd, states = _ssd_fwd(xbc_c, dtcol, dtrow, *ssd_par, name=n("ssd_fwd"))

    def ssd_post(yv, zv, gv):
        xh, _ = _rms_parts(yv * _silu(zv), MIX)
        return xh * gv

    y_c = _rowwise(ssd_post, [y_ssd, u["z"]], [w["ssd_norm_pad"]], [(N_HEADS * LANES, BF16, "row")], name=n("ssd_post"))
    sv.update(xbc_c=xbc_c, dtcol=dtcol, dtrow=dtrow, y_ssd=y_ssd, states=states, y_c=y_c)

    xc = _conv_fwd(u["lru_x"], w["lru_conv_w"], _row(w["lru_conv_b"]), silu=False, name=n("lru_conv"))
    pre = _mm(xc, w["lru_bd"], name=n("lru_mm"))
    lru_par = (_row(w["lru_lambda"]), _row(w["lru_b_a"]), _row(w["lru_b_i"]))
    y_d, h_lru = _lru_fwd(pre, xc, u["lru_g"], *lru_par, name=n("lru_fwd"))
    sv.update(xc=xc, pre=pre, h_lru=h_lru, y_d=y_d)

    merged, *ybs = _branch_merge([y_a, y_b, y_c, y_d], w["br"], u_all, name=n("branch_merge"))
    x1 = _mm(merged, w["w_out"], epilogue=lambda acc, xr: (acc + xr,), tiles=[x], name=n("out_proj"))
    sv.update(ybs=ybs, merged=merged, x1=x1)

    h2 = _rms_fwd(x1, _row(w["g_mlp"]), name=n("rms_mlp"))
    a_ff, f_ff = _mm(h2, w["w_ff1"], epilogue=lambda acc: (acc, jnp.square(jnp.maximum(acc, 0.0))),
                     out_dtypes=(BF16, BF16), name=n("ff1"))
    x2 = _mm(f_ff, w["w_ff2"], epilogue=lambda acc, xr: (acc + xr,), tiles=[x1], name=n("ff2"))
    sv.update(h2=h2, a_ff=a_ff, f_ff=f_ff, x2=x2)

    h3 = _rms_fwd(x2, _row(w["g_ple"]), name=n("rms_ple"))
    e_ple = _mm(p_emb, w["w_ple"], name=n("ple_emb"))
    x3, gt_ple = _mm(h3, w["w_ple_gate"], epilogue=lambda acc, ev, xr: (xr + ev * _sigmoid(acc), _sigmoid(acc)),
                     tiles=[e_ple, x2], out_dtypes=(F32, F32), name=n("ple_gate"))
    sv.update(h3=h3, e_ple=e_ple, gt_ple=gt_ple, p_emb=p_emb)
    return x3, sv


def _layer_bwd(dx3, sv, w, rope, tag):
    n = lambda s: f"{s}_{tag}"
    gr = {}
    u = sv["u"]

    de, dpre = _rowwise(lambda d, gt, ev: (d * gt, d * ev * gt * (1.0 - gt)), [dx3, sv["gt_ple"], sv["e_ple"]], [],
                        [(D_MODEL, BF16, "row"), (D_MODEL, BF16, "row")], name=n("ple_bwd"))
    gr["w_ple"] = _mm(sv["p_emb"], de, ta=True, name=n("d_w_ple"))
    gr["w_ple_gate"] = _mm(sv["h3"], dpre, ta=True, name=n("d_w_ple_gate"))
    dh3 = _mm(dpre, w["w_ple_gate"], tb=True, out_dtypes=(BF16,), name=n("d_h3"))
    dx2, dg = _rms_bwd(sv["x2"], _row(w["g_ple"]), dh3, dx3, name=n("rms_ple_bwd"))
    gr["g_ple"] = dg[0]

    gr["w_ff2"] = _mm(sv["f_ff"], dx2, ta=True, name=n("d_w_ff2"))
    da = _mm(dx2, w["w_ff2"], tb=True, epilogue=lambda acc, av: (acc * 2.0 * jnp.maximum(av, 0.0),),
             tiles=[sv["a_ff"]], out_dtypes=(BF16,), name=n("d_a_ff"))
    gr["w_ff1"] = _mm(sv["h2"], da, ta=True, name=n("d_w_ff1"))
    dh2 = _mm(da, w["w_ff1"], tb=True, out_dtypes=(BF16,), name=n("d_h2"))
    dx1, dg = _rms_bwd(sv["x1"], _row(w["g_mlp"]), dh2, dx2, name=n("rms_mlp_bwd"))
    gr["g_mlp"] = dg[0]

    gr["w_out"] = _mm(sv["merged"], dx1, ta=True, name=n("d_w_out"))
    dmerged = _mm(dx1, w["w_out"], tb=True, name=n("d_merged"))

    def merge_bwd(dm, gts, y0, y1, y2, y3):
        dys, dgs = [], []
        for b, yb in enumerate((y0, y1, y2, y3)):
            sg = _sigmoid(gts[:, b * D_MODEL:(b + 1) * D_MODEL])
            dys.append(dm * sg)
            dgs.append(dm * yb * sg * (1.0 - sg))
        return (*dys, jnp.concatenate(dgs, axis=1))

    *dybs, dgates = _rowwise(merge_bwd, [dmerged, u["gates"]] + sv["ybs"], [],
                             [(D_MODEL, BF16, "row")] * 4 + [(4 * D_MODEL, BF16, "row")], name=n("merge_bwd"))
    ys = [sv["y_a"], sv["y_b"], sv["y_c"], sv["y_d"]]
    dwb = [_mm(ys[b], dybs[b], ta=True, name=n(f"d_w_branch{b}")) for b in range(4)]
    gr["w_branch"] = jnp.stack([_unpad_blocks(dwb[0], 0, N_HEADS, HEAD), dwb[1],
                                _unpad_blocks(dwb[2], 0, N_HEADS, HEAD), dwb[3]])
    dy_a = _mm(dybs[0], w["br"][0], tb=True, out_dtypes=(BF16,), name=n("d_y_a"))
    dy_b = _mm(dybs[1], w["br"][1], tb=True, name=n("d_y_b"))
    dy_c = _mm(dybs[2], w["br"][2], tb=True, name=n("d_y_c"))
    dy_d = _mm(dybs[3], w["br"][3], tb=True, name=n("d_y_d"))
    du = {"gates": dgates}

    lru_par = (_row(w["lru_lambda"]), _row(w["lru_b_a"]), _row(w["lru_b_i"]))
    dpa, dpi, dxc_direct, du["lru_g"], dlam, dba, dbi = _lru_bwd(
        sv["pre"], sv["xc"], u["lru_g"], *lru_par, sv["h_lru"], dy_d, name=n("lru_bwd"))
    dpre_lru = jnp.concatenate([dpa, dpi], axis=1)
    d_bd = _mm(sv["xc"], dpre_lru, ta=True, name=n("d_lru_w"))
    gr["lru_w_a"] = _block_diag_t(d_bd[:, :MIX], N_HEADS)
    gr["lru_w_i"] = _block_diag_t(d_bd[:, MIX:], N_HEADS)
    gr["lru_lambda"], gr["lru_b_a"], gr["lru_b_i"] = dlam[0], dba[0], dbi[0]
    dxc = _mm(dpre_lru, w["lru_bd"], tb=True, epilogue=lambda acc, t: (acc + t,), tiles=[dxc_direct], name=n("d_xc"))
    du["lru_x"], gr["lru_conv_w"], dcb = _conv_bwd(u["lru_x"], w["lru_conv_w"], _row(w["lru_conv_b"]), dxc,
                                                  silu=False, name=n("lru_conv_bwd"))
    gr["lru_conv_b"] = dcb[0]

    def ssd_post_bwd(dyc, yv, zv, gv):
        sz = _silu(zv)
        dyz, dgain = _rms_bwd_math(yv * sz, gv, dyc, MIX)
        return dyz * sz, dyz * yv * _silu_grad(zv), dgain

    dy_ssd, du["z"], dgain = _rowwise(ssd_post_bwd, [dy_c, sv["y_ssd"], u["z"]], [w["ssd_norm_pad"]],
                                      [(N_HEADS * LANES, F32, "row"), (N_HEADS * LANES, BF16, "row"),
                                       (N_HEADS * LANES, F32, "acc")], name=n("ssd_post_bwd"))
    gr["ssd_norm"] = _unpad_blocks(dgain[0], 0, N_HEADS, HEAD)
    ssd_par = (_scal3(w["ssd_dt_bias"]), _scal3(w["ssd_a_log"]), _scal3(w["ssd_d"]))
    dxs, dbg, dcg, ddt, dbias, dalog, dd = _ssd_bwd(sv["xbc_c"], sv["dtcol"], sv["dtrow"], *ssd_par, sv["states"],
                                                    dy_ssd, name=n("ssd_bwd"))
    s = dxs.shape[0]
    dxbc_c = jnp.concatenate([dxs, dbg, dcg], axis=1)
    gr["ssd_dt_bias"], gr["ssd_a_log"], gr["ssd_d"] = dbias[:, 0, 0], dalog[:, 0, 0], dd[:, 0, 0]
    du["xbc"], dcw, dcb = _conv_bwd(u["xbc"], w["ssd_conv_w_pad"], w["ssd_conv_b_pad"], dxbc_c, silu=True,
                                    name=n("ssd_conv_bwd"))
    gr["ssd_conv_w"], gr["ssd_conv_b"] = _unpad_xbc_vec(dcw), _unpad_xbc_vec(dcb[0])
    du["dt"] = jnp.pad(ddt[:, :, 0].T, ((0, 0), (0, LANES - N_HEADS)))

    dyb_pre, dscale = _rowwise(lambda d, yp, sc: (d * sc, _colsum(d * yp)), [dy_b, sv["yb_pre"]],
                               [_row(w["pool_scale"])], [(MIX, BF16, "row"), (MIX, F32, "acc")], name=n("pool_scale_bwd"))
    gr["pool_scale"] = dscale[0]
    gr["w_pool"] = _block_diag_t(_mm(sv["pool_d"], dyb_pre, ta=True, name=n("d_w_pool")), 4)
    dd_pool = _mm(dyb_pre, w["pool_bd"], tb=True, name=n("d_pool_d"))
    du["pool"] = _pool_bwd(dd_pool, name=n("pool_bwd"))

    delta = _att_delta(sv["y_a"], dy_a, name=n("att_delta"))
    to_row = lambda t: t.reshape(N_HEADS, 1, s)
    dqc, dkc, dvc = _flash_bwd(sv["qc"], sv["kc"], sv["vc"], dy_a, to_row(sv["lse"]), to_row(delta), name=n("flash_bwd"))
    dq_pad, du["kr"] = _att_prep_bwd(dqc, dkc, *rope, name=n("att_prep_bwd"))
    d_uq = _mm(sv["cqn"], dq_pad, ta=True, name=n("d_w_uq"))
    gr["w_uq"] = _unpad_blocks(d_uq, 1, N_HEADS, HEAD + QK_ROPE)
    dcqn = _mm(dq_pad, w["uq"], tb=True, out_dtypes=(BF16,), name=n("d_cqn"))
    du["cq"], dg = _rms_bwd(u["cq"], _row(w["q_norm"]), dcqn, name=n("rms_q_bwd"))
    gr["q_norm"] = dg[0]
    dkv2 = jnp.concatenate([dkc, dvc], axis=1).astype(BF16)
    d_ukv = _mm(sv["ckvn"], dkv2, ta=True, name=n("d_w_ukv"))
    wk = N_HEADS * LANES
    dk_real = _unpad_blocks(d_ukv[:, :wk], 1, N_HEADS, HEAD).reshape(KV_LORA, N_HEADS, HEAD)
    dv_real = _unpad_blocks(d_ukv[:, wk:], 1, N_HEADS, HEAD).reshape(KV_LORA, N_HEADS, HEAD)
    gr["w_ukv"] = jnp.concatenate([dk_real, dv_real], axis=2).reshape(KV_LORA, N_HEADS * 2 * HEAD)
    dckvn = _mm(dkv2, w["ukv"], tb=True, out_dtypes=(BF16,), name=n("d_ckvn"))
    du["ckv"], dg = _rms_bwd(u["ckv"], _row(w["kv_norm"]), dckvn, name=n("rms_kv_bwd"))
    gr["kv_norm"] = dg[0]

    du_all = _w_in_fused({k: v.astype(BF16) for k, v in du.items()})
    dw_all = _mm(sv["h"], du_all, ta=True, name=n("d_w_in"))
    gr["w_in"] = _w_in_ungroup({k: dw_all[:, off:off + width] for k, off, width in IN_LAYOUT})
    dh = _mm(du_all, w["in_all"], tb=True, name=n("d_h"))
    dx, dg = _rms_bwd(sv["x"], _row(w["g_mix"]), dh, dx1, name=n("rms_mix_bwd"))
    gr["g_mix"] = dg[0]
    return dx, gr


def _pack_rows(n_elems):
    per = PACK_W * PACK_ROWS
    return -(-n_elems // per) * PACK_ROWS


def _pack_flat(parts, dtype):
    flat = jnp.concatenate([p.reshape(-1).astype(dtype) for p in parts])
    rows = _pack_rows(flat.shape[0])
    return jnp.pad(flat, (0, rows * PACK_W - flat.shape[0])).reshape(rows, PACK_W)


def _unpack_flat(buf, shapes):
    lead = buf.shape[:-2]
    flat = buf.reshape(lead + (-1,))
    out, off = [], 0
    for shp in shapes:
        size = int(np.prod(shp))
        out.append(flat[..., off:off + size].reshape(lead + tuple(shp)))
        off += size
    return out


def _merge_shards(t, axis):
    return jnp.concatenate([t[i] for i in range(4)], axis=axis)


def _split_shards(t, axis):
    return jnp.stack(jnp.split(t, 4, axis=axis))


def _rope_tables(positions):
    inv = 1.0 / (ROPE_THETA ** (jnp.arange(0, QK_ROPE, 2, dtype=F32) / QK_ROPE))
    ang = positions.astype(F32)[:, None] * inv
    cos, sin = jnp.cos(ang), jnp.sin(ang)
    s = ang.shape[0]
    half = QK_ROPE // 2
    z = lambda n_: jnp.zeros((s, n_), F32)
    cos_t = jnp.concatenate([jnp.ones((s, HEAD), F32), cos, cos, jnp.ones((s, LANES - HEAD - QK_ROPE), F32)], axis=1)
    sin_p = jnp.concatenate([z(HEAD + half), sin, z(LANES - HEAD - QK_ROPE)], axis=1)
    sin_m = jnp.concatenate([z(HEAD), -sin, z(half + LANES - HEAD - QK_ROPE)], axis=1)
    return cos_t, sin_p, sin_m


def _loss_head(x, g, target, *, name):
    d = x.shape[1]

    def fn(xv, tv, gv):
        xh, r = _rms_parts(xv, d)
        y = xh * gv
        err = y - tv
        dy = err * (1.0 / d)
        dxh = dy * gv
        dx = r * (dxh - xh * (jnp.sum(dxh * xh, axis=-1, keepdims=True) * (1.0 / d)))
        return dx, _colsum(dy * xh), _colsum(err * err) * (0.5 / d)

    return _rowwise(fn, [x, target], [g], [(d, F32, "row"), (d, F32, "acc"), (d, F32, "acc")], name=name)


MATS = tuple((nm, ax) for nm, ax in BIG if nm not in CONV_SHARDED)


def _grad_view(g, ax_layer):
    if ax_layer == 0:
        return g.reshape(4, g.shape[0] // 4, g.shape[1])
    return g.reshape(1, -1, g.shape[-1])


def _reduce_start(grads_l, c_idx, tag):
    views = [_grad_view(grads_l[nm], ax - 1) for nm, ax in MATS]
    got = _send_half(views, name="send_half_" + tag)
    parts = []
    for (nm, ax), v, gt in zip(MATS, views, got):
        both = _chip_sum_half(v, gt, c_idx, name=f"chip_sum_{nm}_{tag}")
        parts.append(both if ax == 1 else _split_shards(both[0], 1))
    return _push_start(parts, scatter=True, name="push_grads_" + tag)


def _reduce_finish(state, after, k_chip, tag):
    send_sems, recv_sems, parts, lands, _ = state
    parts, landed = _push_wait(send_sems, recv_sems, parts, lands, after, name="wait_grads_" + tag)
    mine = [lax.dynamic_update_index_in_dim(t, lax.dynamic_index_in_dim(p, k_chip, 0, keepdims=False), k_chip, 0)
            for t, p in zip(landed, parts)]
    return list(zip(mine, _swap_with_sibling(mine, name="swap_halves_" + tag)))


def _step(args):
    x = args["x"][0]
    c_idx = lax.axis_index("c")
    k_chip = 2 * lax.axis_index("x") + lax.axis_index("y")

    mats = MATS
    assert mats[0][0] == "w_in"
    mine = [[args[nm][l].astype(BF16) for nm, _ in mats] for l in range(2)]
    convs = [(nm, ax) for nm, ax in BIG if nm in CONV_SHARDED]
    conv_all = _gather_all(_pack_flat([args[nm] for nm, _ in convs], F32), name="gather_conv_taps")[0::2]
    first, conv_all = lax.optimization_barrier((mine[0][:1], conv_all))
    push_a = _push_start(first, scatter=False, name="push_w_in_l0")
    push_b = _push_start(lax.optimization_barrier((mine[0][1:], push_a[4]))[0], scatter=False, name="push_weights_l0")
    push_c = _push_start(lax.optimization_barrier((mine[1], push_b[4]))[0], scatter=False, name="push_weights_l1")
    full_conv = {nm: _merge_shards(t, ax)
                 for (nm, ax), t in zip(convs, _unpack_flat(conv_all, [args[nm].shape for nm, _ in convs]))}
    rope = _rope_tables(args["positions"][0])

    def landed(push, after, name):
        own, lands = _push_wait(push[0], push[1], push[2], push[3], after, name=name)
        return [lax.dynamic_update_index_in_dim(t, blk, k_chip, 0) for t, blk in zip(lands, own)]

    def layer_weights(l, gathered):
        p = {nm: _merge_shards(t, ax - 1) for (nm, ax), t in zip(mats, gathered)}
        p.update({nm: full_conv[nm][l] for nm in CONV_SHARDED})
        p.update({nm: args[nm][l] for nm in SMALL if nm != "g_final"})
        return _layer_weights(p)

    w_in0 = landed(push_a, push_c[4], "wait_w_in_l0")
    in_all0 = _w_in_fused(_w_in_groups(_merge_shards(w_in0[0], 1)))
    pre0 = _in_proj(x, args["g_mix"][0], in_all0, "l0")
    layers = [layer_weights(0, w_in0 + landed(push_b, pre0[1], "wait_weights_l0")), None]
    layers[0]["in_all"] = in_all0
    x, sv0 = _layer_fwd(x, args["p"][0, 0], layers[0], rope, "l0", pre=pre0)
    layers[1] = layer_weights(1, landed(push_c, x, "wait_weights_l1"))
    x, sv1 = _layer_fwd(x, args["p"][1, 0], layers[1], rope, "l1")
    saved = [sv0, sv1]

    dx, dg_final, loss_part = _loss_head(x, _row(args["g_final"]), args["loss_target"][0], name="loss_head")
    loss = lax.psum(jnp.sum(loss_part), ("x", "y", "c"))

    grads = [None, None]
    dx, grads[1] = _layer_bwd(dx, saved[1], layers[1], rope, "l1")
    reduce1 = _reduce_start(grads[1], c_idx, "l1")
    dx, grads[0] = _layer_bwd(dx + reduce1[4][0, 0], saved[0], layers[0], rope, "l0")
    reduce0 = _reduce_start(grads[0], c_idx, "l0")

    g_all = {nm: jnp.stack([grads[0][nm], grads[1][nm]]) for nm in SMALL + CONV_SHARDED if nm != "g_final"}
    g_all["g_final"] = dg_final[0]
    all_names = SMALL + CONV_SHARDED
    all_shapes = [g_all[nm].shape for nm in all_names]
    small_sum = _sum_slots(_gather_all(_pack_flat([g_all[nm] for nm in all_names], F32), name="gather_small_grads"),
                           name="sum_devices")
    g_red = dict(zip(all_names, _unpack_flat(small_sum, all_shapes)))
    for nm, ax in BIG:
        if nm in CONV_SHARDED:
            width = args[nm].shape[ax]
            g_red[nm] = lax.dynamic_slice_in_dim(g_red[nm], k_chip * width, width, axis=ax)
    small_shapes = [args[nm].shape for nm in SMALL]
    pack_small = lambda src: _pack_flat([src(nm) for nm in SMALL], F32)
    upd_small = _adamw(pack_small(lambda nm: args[nm]), pack_small(lambda nm: g_red[nm]),
                       pack_small(lambda nm: args["m_" + nm]), pack_small(lambda nm: args["v_" + nm]), name="adamw_small")
    upd = {nm: trip for nm, trip in zip(SMALL, zip(*[_unpack_flat(t, small_shapes) for t in upd_small]))}
    for nm in CONV_SHARDED:
        upd[nm] = _adamw(args[nm], g_red[nm], args["m_" + nm], args["v_" + nm], name="adamw_" + nm)

    slots = [_reduce_finish(reduce0, upd_small[0], k_chip, "l0"), _reduce_finish(reduce1, dx, k_chip, "l1")]
    for i, (nm, _) in enumerate(MATS):
        g_red[nm], *upd[nm] = _adamw_slots(args[nm], [slots[0][i], slots[1][i]], args["m_" + nm], args["v_" + nm],
                                           c_idx, name="adamw_" + nm)

    outs = [loss, dx[None]]
    outs += [g_red[nm] for nm in WEIGHTS]
    for i in range(3):
        outs += [upd[nm][i] for nm in WEIGHTS]
    return tuple(outs)


_ARG_NAMES = ("x", "p", "positions") + WEIGHTS + ("loss_target",) + tuple("m_" + nm for nm in WEIGHTS) \
    + tuple("v_" + nm for nm in WEIGHTS)


def kernel(*arrays):
    assert len(arrays) == len(_ARG_NAMES), len(arrays)
    return _step(dict(zip(_ARG_NAMES, arrays)))
```

```python
import functools
import math

import jax
import jax.numpy as jnp
import numpy as np
from jax import lax
from jax.experimental import pallas as pl
from jax.experimental.pallas import tpu as pltpu

F32 = jnp.float32
BF16 = jnp.bfloat16
MXU_DTYPE = BF16
LANES = 128
VMEM_LIMIT = 56 * 1024 * 1024
MM_VMEM_BUDGET = 36 * 1024 * 1024
ELEMENTWISE_BLOCK_BYTES = 2 * 1024 * 1024

D_MODEL = 1024
N_HEADS = 8
HEAD = 64
QK_ROPE = 32
Q_LORA = 384
KV_LORA = 256
MIX = 512
SSD_CHUNK = 128
CONV_W = 4
POOL_WINDOWS = (2, 4, 8, 16)
LRU_C = 8.0
EPS = 1e-6
ROPE_THETA = 10000.0
ATT_SCALE = (HEAD + QK_ROPE) ** -0.5
SPLIT_SIZES = (Q_LORA, KV_LORA, QK_ROPE, MIX, MIX, 768, N_HEADS, MIX, MIX, 4 * D_MODEL)
IN_LAYOUT = (("gates", 0, 4096), ("z", 4096, 1024), ("pool", 5120, 512), ("lru_g", 5632, 512), ("lru_x", 6144, 512),
             ("cq", 6912, 384), ("ckv", 7424, 256), ("xbc", 7680, 1536), ("kr", 9216, 128), ("dt", 9344, 128))
IN_OFFSETS = {name: (off, width) for name, off, width in IN_LAYOUT}
IN_ALL_COLS = 9728

ADAM_LR, ADAM_B1, ADAM_B2, ADAM_EPS, ADAM_WD, ADAM_STEP = 0.001, 0.9, 0.999, 1e-08, 0.01, 10

BIG = (("w_in", 2), ("w_uq", 2), ("w_ukv", 2), ("ssd_conv_w", 2), ("lru_conv_w", 2), ("w_branch", 3),
       ("w_out", 1), ("w_ff1", 2), ("w_ff2", 1), ("w_ple_gate", 1), ("w_ple", 2))
SMALL = ("g_mix", "q_norm", "kv_norm", "w_pool", "pool_scale", "ssd_conv_b", "ssd_dt_bias", "ssd_a_log",
         "ssd_d", "ssd_norm", "lru_conv_b", "lru_w_a", "lru_b_a", "lru_w_i", "lru_b_i", "lru_lambda",
         "g_mlp", "g_ple", "g_final")
WEIGHTS = ("g_mix", "w_in", "q_norm", "w_uq", "kv_norm", "w_ukv", "w_pool", "pool_scale", "ssd_conv_w",
           "ssd_conv_b", "ssd_dt_bias", "ssd_a_log", "ssd_d", "ssd_norm", "lru_conv_w", "lru_conv_b", "lru_w_a",
           "lru_b_a", "lru_w_i", "lru_b_i", "lru_lambda", "w_branch", "w_out", "g_mlp", "w_ff1", "w_ff2", "g_ple",
           "w_ple_gate", "w_ple", "g_final")
CONV_SHARDED = ("ssd_conv_w", "lru_conv_w")
PACK_W = 1024
PACK_ROWS = 64


def _cparams(sem, vmem=VMEM_LIMIT):
    return pltpu.CompilerParams(dimension_semantics=sem, vmem_limit_bytes=vmem)


def _pick(n, cands):
    for c in cands:
        if n % c == 0:
            return c
    return n


class _Cols:
    def __init__(self, arr, off, width):
        self.arr, self.off, self.width = arr, off, width

    shape = property(lambda self: (self.arr.shape[0], self.width))
    dtype = property(lambda self: self.arr.dtype)


def _arr(x):
    return x.arr if isinstance(x, _Cols) else x


def _off(x, unit):
    off = x.off if isinstance(x, _Cols) else 0
    assert off % unit == 0, (off, unit)
    return off // unit


def _sigmoid(x):
    return 1.0 / (1.0 + jnp.exp(-x))


def _silu(x):
    return x * _sigmoid(x)


def _silu_grad(x):
    s = _sigmoid(x)
    return s * (1.0 + x * (1.0 - s))


def _softplus(x):
    e = jnp.exp(-jnp.abs(x))
    log1p_e = jnp.where(e < 1e-3, e * (1.0 - e * (0.5 - e * (1.0 / 3.0))), jnp.log(1.0 + e))
    return jnp.maximum(x, 0.0) + log1p_e


_GELU_C = math.sqrt(2.0 / math.pi)


def _gelu(x):
    t = jnp.tanh(_GELU_C * (x + 0.044715 * x * x * x))
    return 0.5 * x * (1.0 + t)


def _gelu_grad(x):
    t = jnp.tanh(_GELU_C * (x + 0.044715 * x * x * x))
    return 0.5 * (1.0 + t) + 0.5 * x * (1.0 - t * t) * _GELU_C * (1.0 + 3.0 * 0.044715 * x * x)


def _neg_expm1(x):
    series = -x * (1.0 + 0.5 * x * (1.0 + (1.0 / 3.0) * x * (1.0 + 0.25 * x)))
    return jnp.where(x > -0.05, series, 1.0 - jnp.exp(x))


def _shift_down(x, k, row):
    return jnp.where(row >= k, pltpu.roll(x, k, 0), 0.0)


def _shift_up(x, k, row):
    n = x.shape[0]
    return jnp.where(row < n - k, pltpu.roll(x, n - k, 0), 0.0)


def _cumsum_rows(x, row):
    d = 1
    while d < x.shape[0]:
        x = x + _shift_down(x, d, row)
        d *= 2
    return x


def _rev_cumsum_rows(x, row):
    d = 1
    while d < x.shape[0]:
        x = x + _shift_up(x, d, row)
        d *= 2
    return x


def _cumsum_lanes(x, col):
    d = 1
    while d < x.shape[1]:
        x = x + jnp.where(col >= d, pltpu.roll(x, d, 1), 0.0)
        d *= 2
    return x


def _dot(a, b, ta=False, tb=False):
    dn = (((0 if ta else 1,), (1 if tb else 0,)), ((), ()))
    return lax.dot_general(a.astype(MXU_DTYPE), b.astype(MXU_DTYPE), dn, preferred_element_type=F32)


def _mm_tiles(m, n, k, a_bytes, b_bytes, mn_bytes):
    best = None
    for tm in (1024, 512, 384, 256, 128):
        for tn in (1024, 512, 384, 256, 128):
            for tk in (2048, 1024, 512, 384, 256, 128):
                if m % tm or n % tn or k % tk:
                    continue
                vmem = 2 * (tm * tk * a_bytes + tk * tn * b_bytes) + 2 * tm * tn * mn_bytes + 4 * tm * tn
                vmem += 2 * (tm * tk + tk * tn)
                if vmem > MM_VMEM_BUDGET:
                    continue
                steps = (m // tm) * (n // tn) * (k // tk)
                key = (steps, vmem)
                if best is None or key < best[0]:
                    best = (key, (tm, tn, tk))
    assert best is not None, (m, n, k)
    return best[1]


def _mm(a, b, *, ta=False, tb=False, epilogue=None, tiles=(), rowvecs=(), out_dtypes=(F32,), name):
    m, k = (a.shape[1], a.shape[0]) if ta else a.shape
    n = b.shape[0] if tb else b.shape[1]
    assert (b.shape[1] if tb else b.shape[0]) == k, (a.shape, b.shape, ta, tb)
    mn_bytes = sum(t.dtype.itemsize for t in tiles) + sum(jnp.dtype(dt).itemsize for dt in out_dtypes)
    tm, tn, tk = _mm_tiles(m, n, k, a.dtype.itemsize, b.dtype.itemsize, mn_bytes)
    nk = k // tk
    nt, nr, no = len(tiles), len(rowvecs), len(out_dtypes)

    def body(*refs):
        a_ref, b_ref = refs[0], refs[1]
        tile_refs = refs[2:2 + nt]
        row_refs = refs[2 + nt:2 + nt + nr]
        out_refs = refs[2 + nt + nr:2 + nt + nr + no]
        acc_ref = refs[-1]
        kk = pl.program_id(2)

        @pl.when(kk == 0)
        def _():
            acc_ref[...] = jnp.zeros_like(acc_ref)

        acc_ref[...] += _dot(a_ref[...], b_ref[...], ta, tb)

        @pl.when(kk == nk - 1)
        def _():
            acc = acc_ref[...]
            if epilogue is None:
                outs = (acc,)
            else:
                outs = epilogue(acc, *[t[...] for t in tile_refs], *[r[...] for r in row_refs])
            for o_ref, o in zip(out_refs, outs):
                o_ref[...] = o.astype(o_ref.dtype)

    a_spec = pl.BlockSpec((tk, tm), lambda i, j, kk: (kk, i)) if ta else pl.BlockSpec((tm, tk), lambda i, j, kk: (i, kk))
    b_spec = pl.BlockSpec((tn, tk), lambda i, j, kk: (j, kk)) if tb else pl.BlockSpec((tk, tn), lambda i, j, kk: (kk, j))
    mn_spec = pl.BlockSpec((tm, tn), lambda i, j, kk: (i, j))
    row_spec = pl.BlockSpec((1, tn), lambda i, j, kk: (0, j))
    tile_specs = [pl.BlockSpec((tm, tn), lambda i, j, kk, ob=_off(t, tn): (i, j + ob)) for t in tiles]
    outs = pl.pallas_call(
        body, name=name,
        grid=(m // tm, n // tn, nk),
        in_specs=[a_spec, b_spec] + tile_specs + [row_spec] * nr,
        out_specs=[mn_spec] * no,
        out_shape=[jax.ShapeDtypeStruct((m, n), dt) for dt in out_dtypes],
        scratch_shapes=[pltpu.VMEM((tm, tn), F32)],
        compiler_params=_cparams(("parallel", "parallel", "arbitrary")),
    )(a, b, *[_arr(t) for t in tiles], *rowvecs)
    return outs[0] if no == 1 else tuple(outs)


def _branch_merge(ys, ws, u_all, *, name):
    s, d = ys[0].shape[0], ws[0].shape[1]
    tm, tn = _pick(s, (512, 256, 128)), _pick(d, (512, 256, 128))
    nb = len(ys)

    def body(*refs):
        y_refs, w_refs, g_refs = refs[:nb], refs[nb:2 * nb], refs[2 * nb:3 * nb]
        merged_ref, yb_refs = refs[3 * nb], refs[3 * nb + 1:]
        merged = None
        for y_ref, w_ref, g_ref, yb_ref in zip(y_refs, w_refs, g_refs, yb_refs):
            acc = _dot(y_ref[...], w_ref[...])
            yb_ref[...] = acc.astype(yb_ref.dtype)
            term = _sigmoid(g_ref[...]) * acc
            merged = term if merged is None else merged + term
        merged_ref[...] = merged

    mn = pl.BlockSpec((tm, tn), lambda i, j: (i, j))
    in_specs = [pl.BlockSpec((tm, y.shape[1]), lambda i, j: (i, 0)) for y in ys]
    in_specs += [pl.BlockSpec((w.shape[0], tn), lambda i, j: (0, j)) for w in ws]
    in_specs += [pl.BlockSpec((tm, tn), lambda i, j, ob=b * d // tn: (i, j + ob)) for b in range(nb)]
    return pl.pallas_call(
        body, name=name, grid=(s // tm, d // tn), in_specs=in_specs, out_specs=[mn] * (nb + 1),
        out_shape=[jax.ShapeDtypeStruct((s, d), F32)] + [jax.ShapeDtypeStruct((s, d), BF16)] * nb,
        compiler_params=_cparams(("parallel", "parallel")),
    )(*ys, *ws, *[u_all] * nb)


def _rowwise(fn, rows, fulls, outs, *, name, tm=None):
    r = rows[0].shape[0]
    if tm is None:
        widest = max([x.shape[1] for x in rows] + [o[0] for o in outs])
        tm = _pick(r, (max(8, min(512, (512 * 1024) // widest)), 256, 128, 64, 32, 16, 8))
    nrow, nfull, nout = len(rows), len(fulls), len(outs)

    def body(*refs):
        row_refs = refs[:nrow]
        full_refs = refs[nrow:nrow + nfull]
        out_refs = refs[nrow + nfull:]
        res = fn(*[x[...] for x in row_refs], *[x[...] for x in full_refs])
        if not isinstance(res, (tuple, list)):
            res = (res,)
        step = pl.program_id(0)
        for o_ref, o, spec in zip(out_refs, res, outs):
            if spec[2] == "row":
                o_ref[...] = o.astype(o_ref.dtype)
            else:
                @pl.when(step == 0)
                def _(o_ref=o_ref):
                    o_ref[...] = jnp.zeros_like(o_ref)
                o_ref[...] += o

    in_specs = [pl.BlockSpec((tm, x.shape[1]), lambda i, ob=_off(x, x.shape[1]): (i, ob)) for x in rows]
    in_specs += [pl.BlockSpec(x.shape, lambda i, nd=x.ndim: (0,) * nd) for x in fulls]
    out_specs, out_shape = [], []
    for c, dt, kind in outs:
        if kind == "row":
            out_specs.append(pl.BlockSpec((tm, c), lambda i: (i, 0)))
            out_shape.append(jax.ShapeDtypeStruct((r, c), dt))
        else:
            out_specs.append(pl.BlockSpec((1, c), lambda i: (0, 0)))
            out_shape.append(jax.ShapeDtypeStruct((1, c), F32))
    res = pl.pallas_call(
        body, name=name, grid=(r // tm,), in_specs=in_specs, out_specs=out_specs, out_shape=out_shape,
        compiler_params=_cparams(("arbitrary",)),
    )(*[_arr(x) for x in rows], *fulls)
    return res[0] if nout == 1 else tuple(res)


def _colsum(x):
    return jnp.sum(x, axis=0, keepdims=True)


def _rms_parts(x, n_real):
    r = lax.rsqrt(jnp.sum(x * x, axis=-1, keepdims=True) * (1.0 / n_real) + EPS)
    return x * r, r


def _rms_fwd(x, g, *, n_real=None, out_dtype=BF16, name):
    n_real = n_real or x.shape[1]

    def fn(xv, gv):
        xh, _ = _rms_parts(xv, n_real)
        return xh * gv

    return _rowwise(fn, [x], [g], [(x.shape[1], out_dtype, "row")], name=name)


def _rms_bwd_math(xv, gv, dh, n_real):
    xh, r = _rms_parts(xv, n_real)
    dxh = dh * gv
    dx = r * (dxh - xh * (jnp.sum(dxh * xh, axis=-1, keepdims=True) * (1.0 / n_real)))
    return dx, _colsum(dh * xh)


def _rms_bwd(x, g, dh, res=None, *, name):
    n = x.shape[1]
    if res is None:
        def fn(xv, dhv, gv):
            return _rms_bwd_math(xv, gv, dhv.astype(F32), n)
        rows = [x, dh]
    else:
        def fn(xv, dhv, rv, gv):
            dx, dg = _rms_bwd_math(xv, gv, dhv.astype(F32), n)
            return dx + rv, dg
        rows = [x, dh, res]
    return _rowwise(fn, rows, [g], [(n, F32, "row"), (n, F32, "acc")], name=name)


def _seq_call(body, ins, outs, n_blocks, *, name):
    in_specs, args = [], []
    for x, kind in ins:
        in_specs.append(pl.BlockSpec((x.shape[0], LANES), lambda j, ob=_off(x, LANES): (0, j + ob)))
        args.append(_arr(x))
    out_specs, out_shape = [], []
    for shape, dt in outs:
        out_specs.append(pl.BlockSpec((shape[0], LANES), lambda j: (0, j)))
        out_shape.append(jax.ShapeDtypeStruct(shape, dt))
    res = pl.pallas_call(body, name=name, grid=(n_blocks,), in_specs=in_specs, out_specs=out_specs,
                         out_shape=out_shape, compiler_params=_cparams(("parallel",)))(*args)
    return res[0] if len(outs) == 1 else tuple(res)


def _conv_pre(x, w, b, row):
    acc = x * w[CONV_W - 1:CONV_W, :] + b
    for k in range(CONV_W - 1):
        acc = acc + _shift_down(x, CONV_W - 1 - k, row) * w[k:k + 1, :]
    return acc


def _conv_fwd(x, w, b, *, silu, name):
    s, c = x.shape

    def body(x_ref, w_ref, b_ref, y_ref):
        xv = x_ref[...]
        row = lax.broadcasted_iota(jnp.int32, xv.shape, 0)
        pre = _conv_pre(xv, w_ref[...], b_ref[...], row)
        y_ref[...] = _silu(pre) if silu else pre

    return _seq_call(body, [(x, "seq"), (w, "par"), (b, "par")], [((s, c), F32)], c // LANES, name=name)


def _conv_bwd(x, w, b, dy, *, silu, name):
    s, c = x.shape

    def body(x_ref, w_ref, b_ref, dy_ref, dx_ref, dw_ref, db_ref):
        xv, wv, dv = x_ref[...], w_ref[...], dy_ref[...]
        row = lax.broadcasted_iota(jnp.int32, xv.shape, 0)
        if silu:
            dv = dv * _silu_grad(_conv_pre(xv, wv, b_ref[...], row))
        dx = dv * wv[CONV_W - 1:CONV_W, :]
        dws = [None] * CONV_W
        dws[CONV_W - 1] = _colsum(dv * xv)
        for k in range(CONV_W - 1):
            sh = CONV_W - 1 - k
            dx = dx + _shift_up(dv, sh, row) * wv[k:k + 1, :]
            dws[k] = _colsum(dv * _shift_down(xv, sh, row))
        dx_ref[...] = dx
        for k in range(CONV_W):
            dw_ref[k:k + 1, :] = dws[k]
        db_ref[...] = _colsum(dv)

    return _seq_call(body, [(x, "seq"), (w, "par"), (b, "par"), (dy, "seq")],
                     [((s, c), F32), ((CONV_W, c), F32), ((1, c), F32)], c // LANES, name=name)


def _pool_select(levels):
    g = pl.program_id(0)
    return jnp.where(g == 0, levels[0], jnp.where(g == 1, levels[1], jnp.where(g == 2, levels[2], levels[3])))


def _pool_count(row):
    g = pl.program_id(0)
    w = jnp.where(g == 0, POOL_WINDOWS[0], jnp.where(g == 1, POOL_WINDOWS[1],
                                                     jnp.where(g == 2, POOL_WINDOWS[2], POOL_WINDOWS[3])))
    return jnp.minimum(row + 1, w).astype(F32)


def _pool_fwd(u, *, name):
    def body(u_ref, d_ref):
        uv = u_ref[...]
        row = lax.broadcasted_iota(jnp.int32, uv.shape, 0)
        levels, cur, sh = [], uv, 1
        for _ in POOL_WINDOWS:
            cur = cur + _shift_down(cur, sh, row)
            levels.append(cur)
            sh *= 2
        d_ref[...] = _pool_select(levels) / _pool_count(row) - uv

    return _seq_call(body, [(u, "seq")], [(u.shape, F32)], u.shape[1] // LANES, name=name)


def _pool_bwd(dd, *, name):
    def body(dd_ref, du_ref):
        dv = dd_ref[...]
        row = lax.broadcasted_iota(jnp.int32, dv.shape, 0)
        levels, cur, sh = [], dv / _pool_count(row), 1
        for _ in POOL_WINDOWS:
            cur = cur + _shift_up(cur, sh, row)
            levels.append(cur)
            sh *= 2
        du_ref[...] = _pool_select(levels) - dv

    return _seq_call(body, [(dd, "seq")], [(dd.shape, F32)], dd.shape[1] // LANES, name=name)


def _lru_gates(pre_a, pre_i, xc, lam, b_a, b_i):
    r = _sigmoid(pre_a + b_a)
    i = _sigmoid(pre_i + b_i)
    sp = _softplus(-lam)
    log_a = -LRU_C * r * sp
    a = jnp.exp(log_a)
    mult = jnp.sqrt(_neg_expm1(2.0 * log_a))
    return r, i, sp, a, mult


def _lru_fwd(pre, xc, gate_in, lam, b_a, b_i, *, name):
    s, c = xc.shape
    nb = c // LANES

    def body(pa_ref, pi_ref, xc_ref, g_ref, lam_ref, ba_ref, bi_ref, y_ref, h_ref):
        xv = xc_ref[...]
        row = lax.broadcasted_iota(jnp.int32, xv.shape, 0)
        _, i, _, a, mult = _lru_gates(pa_ref[...], pi_ref[...], xv, lam_ref[...], ba_ref[...], bi_ref[...])
        h = xv * i * mult
        d = 1
        while d < s:
            h = h + a * _shift_down(h, d, row)
            a = a * jnp.where(row >= d, pltpu.roll(a, d, 0), 1.0)
            d *= 2
        h_ref[...] = h
        y_ref[...] = h * _gelu(g_ref[...])

    blk = lambda off: pl.BlockSpec((s, LANES), lambda j: (0, j + off))
    par = pl.BlockSpec((1, LANES), lambda j: (0, j))
    return pl.pallas_call(
        body, name=name, grid=(nb,),
        in_specs=[blk(0), blk(nb), blk(0), blk(_off(gate_in, LANES)), par, par, par],
        out_specs=[blk(0), blk(0)],
        out_shape=[jax.ShapeDtypeStruct((s, c), F32)] * 2,
        compiler_params=_cparams(("parallel",)),
    )(pre, pre, xc, _arr(gate_in), lam, b_a, b_i)


def _lru_bwd(pre, xc, gate_in, lam, b_a, b_i, h, dy, *, name):
    s, c = xc.shape
    nb = c // LANES

    def body(pa_ref, pi_ref, xc_ref, g_ref, lam_ref, ba_ref, bi_ref, h_ref, dy_ref,
             dpa_ref, dpi_ref, dxc_ref, dg_ref, dlam_ref, dba_ref, dbi_ref):
        xv, gv, hv, dv = xc_ref[...], g_ref[...], h_ref[...], dy_ref[...]
        row = lax.broadcasted_iota(jnp.int32, xv.shape, 0)
        r, i, sp, a, mult = _lru_gates(pa_ref[...], pi_ref[...], xv, lam_ref[...], ba_ref[...], bi_ref[...])
        dg_ref[...] = dv * hv * _gelu_grad(gv)
        dh = dv * _gelu(gv)
        an = jnp.where(row < s - 1, pltpu.roll(a, s - 1, 0), 0.0)
        d = 1
        while d < s:
            dh = dh + an * _shift_up(dh, d, row)
            an = an * jnp.where(row < s - d, pltpu.roll(an, s - d, 0), 1.0)
            d *= 2
        da = dh * _shift_down(hv, 1, row)
        dxc_ref[...] = dh * i * mult
        di = dh * xv * mult
        dmult = dh * xv * i
        dlog_a = (da - dmult * a / mult) * a
        dr = dlog_a * (-LRU_C) * sp
        dlam_ref[...] = _colsum(dlog_a * LRU_C * r * _sigmoid(-lam_ref[...]))
        dpa = dr * r * (1.0 - r)
        dpi = di * i * (1.0 - i)
        dpa_ref[...] = dpa
        dpi_ref[...] = dpi
        dba_ref[...] = _colsum(dpa)
        dbi_ref[...] = _colsum(dpi)

    blk = lambda off: pl.BlockSpec((s, LANES), lambda j: (0, j + off))
    par = pl.BlockSpec((1, LANES), lambda j: (0, j))
    sc = jax.ShapeDtypeStruct((s, c), F32)
    pc = jax.ShapeDtypeStruct((1, c), F32)
    dpa, dpi, dxc, dg, dlam, dba, dbi = pl.pallas_call(
        body, name=name, grid=(nb,),
        in_specs=[blk(0), blk(nb), blk(0), blk(_off(gate_in, LANES)), par, par, par, blk(0), blk(0)],
        out_specs=[blk(0), blk(0), blk(0), blk(0), par, par, par],
        out_shape=[sc, sc, sc, sc, pc, pc, pc],
        compiler_params=_cparams(("parallel",)),
    )(pre, pre, xc, _arr(gate_in), lam, b_a, b_i, h, dy)
    return dpa, dpi, dxc, dg, dlam, dba, dbi


GROUP_HEADS = 4


def _ssd_specs(nc, order):
    gw = GROUP_HEADS * LANES
    return dict(
        x=pl.BlockSpec((SSD_CHUNK, gw), lambda g, ci: (order(ci), g)),
        b=pl.BlockSpec((SSD_CHUNK, LANES), lambda g, ci: (order(ci), N_HEADS + g)),
        c=pl.BlockSpec((SSD_CHUNK, LANES), lambda g, ci: (order(ci), N_HEADS + 2 + g)),
        dtcol=pl.BlockSpec((GROUP_HEADS, SSD_CHUNK, 1), lambda g, ci: (g, order(ci), 0)),
        dtrow=pl.BlockSpec((GROUP_HEADS, 1, SSD_CHUNK), lambda g, ci: (g, 0, order(ci))),
        scal=pl.BlockSpec((GROUP_HEADS, 1, 1), lambda g, ci: (g, 0, 0)),
        state=pl.BlockSpec((GROUP_HEADS, 1, LANES, LANES), lambda g, ci: (g, order(ci), 0, 0)),
        group=pl.BlockSpec((SSD_CHUNK, LANES), lambda g, ci: (order(ci), g)),
        pacc=pl.BlockSpec((GROUP_HEADS, 1, LANES), lambda g, ci: (g, 0, 0)),
    )


def _ssd_chunk_terms(dtcol, dtrow, bias, a_log):
    shp = (SSD_CHUNK, SSD_CHUNK)
    row = lax.broadcasted_iota(jnp.int32, shp, 0)
    col = lax.broadcasted_iota(jnp.int32, shp, 1)
    a_head = -jnp.exp(a_log)
    dt_c = jnp.broadcast_to(_softplus(dtcol + bias), shp)
    dt_r = jnp.broadcast_to(_softplus(dtrow + bias), shp)
    cs_c = _cumsum_rows(dt_c * a_head, row)
    cs_r = _cumsum_lanes(dt_r * a_head, col)
    cs_last = jnp.sum(jnp.where(row == SSD_CHUNK - 1, cs_c, 0.0), axis=0, keepdims=True)
    return row, col, a_head, dt_c, cs_c, cs_r, cs_last


def _ssd_fwd(xbc, dtcol, dtrow, bias, a_log, dskip, *, name):
    s = xbc.shape[0]
    nc = s // SSD_CHUNK

    def body(x_ref, b_ref, c_ref, dtc_ref, dtr_ref, bias_ref, alog_ref, d_ref, y_ref, st_ref, state):
        ci = pl.program_id(1)

        @pl.when(ci == 0)
        def _():
            state[...] = jnp.zeros_like(state)

        bm, cm = b_ref[...], c_ref[...]
        cb = _dot(cm, bm, tb=True)
        bm_t = bm.T
        for r in range(GROUP_HEADS):
            lanes = slice(r * LANES, (r + 1) * LANES)
            xv = x_ref[:, lanes]
            row, col, _, dt_c, cs_c, cs_r, cs_last = _ssd_chunk_terms(dtc_ref[r], dtr_ref[r], bias_ref[r], alog_ref[r])
            g = cb * jnp.exp(jnp.where(col <= row, cs_c - cs_r, -jnp.inf))
            xdt = xv * dt_c
            st = state[r]
            st_ref[r, 0] = st
            y_ref[:, lanes] = _dot(g, xdt) + _dot(cm, st) * jnp.exp(cs_c) + xv * d_ref[r]
            state[r] = jnp.exp(cs_last) * st + _dot(bm_t, xdt * jnp.exp(cs_last - cs_c))

    sp = _ssd_specs(nc, lambda ci: ci)
    return pl.pallas_call(
        body, name=name, grid=(N_HEADS // GROUP_HEADS, nc),
        in_specs=[sp["x"], sp["b"], sp["c"], sp["dtcol"], sp["dtrow"], sp["scal"], sp["scal"], sp["scal"]],
        out_specs=[sp["x"], sp["state"]],
        out_shape=[jax.ShapeDtypeStruct((s, N_HEADS * LANES), F32),
                   jax.ShapeDtypeStruct((N_HEADS, nc, LANES, LANES), F32)],
        scratch_shapes=[pltpu.VMEM((GROUP_HEADS, LANES, LANES), F32)],
        compiler_params=_cparams(("parallel", "arbitrary")),
    )(xbc, xbc, xbc, dtcol, dtrow, bias, a_log, dskip)


def _ssd_bwd(xbc, dtcol, dtrow, bias, a_log, dskip, states, dy, *, name):
    s = xbc.shape[0]
    nc = s // SSD_CHUNK

    def body(x_ref, b_ref, c_ref, dtc_ref, dtr_ref, bias_ref, alog_ref, d_ref, st_ref, dy_ref,
             dx_ref, db_ref, dc_ref, ddt_ref, dbias_ref, dalog_ref, dd_ref, dstate):
        ci = pl.program_id(1)

        @pl.when(ci == 0)
        def _():
            dstate[...] = jnp.zeros_like(dstate)
            dbias_ref[...] = jnp.zeros_like(dbias_ref)
            dalog_ref[...] = jnp.zeros_like(dalog_ref)
            dd_ref[...] = jnp.zeros_like(dd_ref)

        bm, cm = b_ref[...], c_ref[...]
        cb = _dot(cm, bm, tb=True)
        cb_t = _dot(bm, cm, tb=True)
        cm_t = cm.T
        rowsum = lambda v: jnp.sum(v, axis=1, keepdims=True)
        tot = lambda v: jnp.broadcast_to(jnp.sum(v, axis=0, keepdims=True), (1, LANES))
        dbm_sum, dcm_sum = None, None
        for r in range(GROUP_HEADS):
            lanes = slice(r * LANES, (r + 1) * LANES)
            xv, dyv, st = x_ref[:, lanes], dy_ref[:, lanes], st_ref[r, 0]
            dtraw_c, bias = dtc_ref[r], bias_ref[r]
            row, col, a_head, dt_c, cs_c, cs_r, cs_last = _ssd_chunk_terms(dtraw_c, dtr_ref[r], bias, alog_ref[r])
            lmat = jnp.exp(jnp.where(col <= row, cs_c - cs_r, -jnp.inf))
            lmat_t = jnp.exp(jnp.where(row <= col, cs_r - cs_c, -jnp.inf))
            g, g_t = cb * lmat, cb_t * lmat_t
            xdt = xv * dt_c
            e_c = jnp.exp(cs_c)
            f_c = jnp.exp(cs_last - cs_c)
            e_last = jnp.exp(cs_last)
            w = xdt * f_c
            dst = dstate[r]

            dg = _dot(dyv, xdt, tb=True)
            dg_t = _dot(xdt, dyv, tb=True)
            dxdt = _dot(g_t, dyv)
            dcs = rowsum(dg * g) - rowsum(dg_t * g_t)
            dcm = _dot(dg * lmat, bm)
            dbm = _dot(dg_t * lmat_t, cm)
            z = _dot(cm, st)
            dz = dyv * e_c
            dcs = dcs + rowsum(dz * z)
            dcm = dcm + _dot(dz, st, tb=True)
            dstate[r] = _dot(cm_t, dz) + e_last * dst
            dcs_last = jnp.sum(rowsum(dst * st), axis=0, keepdims=True) * jnp.max(e_last, axis=1, keepdims=True)
            dbm = dbm + _dot(w, dst, tb=True)
            dw = _dot(bm, dst)
            dxdt = dxdt + dw * f_c
            q = rowsum(dw * w)
            dcs = dcs - q
            dcs_last = dcs_last + jnp.sum(q, axis=0, keepdims=True)
            dx_ref[:, lanes] = dxdt * dt_c + dyv * d_ref[r]
            ddt = rowsum(dxdt * xv)
            dcs_full = jnp.broadcast_to(dcs, (SSD_CHUNK, SSD_CHUNK)) + jnp.where(row == SSD_CHUNK - 1, dcs_last, 0.0)
            da = jnp.max(_rev_cumsum_rows(dcs_full, row), axis=1, keepdims=True)
            dt_col = jnp.max(dt_c, axis=1, keepdims=True)
            draw = (ddt + da * a_head) * _sigmoid(dtraw_c + bias)
            ddt_ref[r] = draw
            dbias_ref[r] += tot(draw)
            dalog_ref[r] += tot(da * dt_col) * a_head
            dd_ref[r] += tot(rowsum(dyv * xv))
            dbm_sum = dbm if dbm_sum is None else dbm_sum + dbm
            dcm_sum = dcm if dcm_sum is None else dcm_sum + dcm
        db_ref[...] = dbm_sum
        dc_ref[...] = dcm_sum

    sp = _ssd_specs(nc, lambda ci: nc - 1 - ci)
    return pl.pallas_call(
        body, name=name, grid=(N_HEADS // GROUP_HEADS, nc),
        in_specs=[sp["x"], sp["b"], sp["c"], sp["dtcol"], sp["dtrow"], sp["scal"], sp["scal"], sp["scal"],
                  sp["state"], sp["x"]],
        out_specs=[sp["x"], sp["group"], sp["group"], sp["dtcol"], sp["pacc"], sp["pacc"], sp["pacc"]],
        out_shape=[jax.ShapeDtypeStruct((s, N_HEADS * LANES), F32),
                   jax.ShapeDtypeStruct((s, 2 * LANES), F32),
                   jax.ShapeDtypeStruct((s, 2 * LANES), F32),
                   jax.ShapeDtypeStruct((N_HEADS, s, 1), F32),
                   jax.ShapeDtypeStruct((N_HEADS, 1, LANES), F32),
                   jax.ShapeDtypeStruct((N_HEADS, 1, LANES), F32),
                   jax.ShapeDtypeStruct((N_HEADS, 1, LANES), F32)],
        scratch_shapes=[pltpu.VMEM((GROUP_HEADS, LANES, LANES), F32)],
        compiler_params=_cparams(("parallel", "arbitrary")),
    )(xbc, xbc, xbc, dtcol, dtrow, bias, a_log, dskip, states, dy)


def _att_tile(s):
    return _pick(s, (512, 256, 128))


def _tri(t, transposed=False):
    r = lax.broadcasted_iota(jnp.int32, (t, t), 0)
    c = lax.broadcasted_iota(jnp.int32, (t, t), 1)
    return (r <= c) if transposed else (c <= r)


def _rows_at(ref, blk, t):
    return ref[pl.ds(pl.multiple_of(blk * t, t), t), :]


def _flash_fwd(q, k, v, *, name):
    s = q.shape[0]
    t = _att_tile(s)
    nq = s // t

    def body(q_ref, k_ref, v_ref, o_ref, lse_ref):
        i = pl.program_id(1)
        qv = q_ref[...]

        def step(j, carry, diagonal):
            m_old, l_old, acc = carry
            sc = _dot(qv, _rows_at(k_ref, j, t), tb=True)
            if diagonal:
                sc = jnp.where(_tri(t), sc, -jnp.inf)
            m_new = jnp.maximum(m_old, jnp.max(sc, axis=1, keepdims=True))
            alpha = jnp.exp(m_old - m_new)
            p = jnp.exp(sc - m_new)
            return (m_new, alpha * l_old + jnp.sum(p, axis=1, keepdims=True),
                    alpha * acc + _dot(p, _rows_at(v_ref, j, t)))

        init = (jnp.full((t, 1), -jnp.inf, F32), jnp.zeros((t, 1), F32), jnp.zeros((t, LANES), F32))
        carry = lax.fori_loop(0, i, lambda j, c: step(j, c, False), init)
        m_fin, l_fin, acc = step(i, carry, True)
        o_ref[...] = (acc / l_fin).astype(o_ref.dtype)
        lse_ref[0] = m_fin + jnp.log(l_fin)

    q_spec = pl.BlockSpec((t, LANES), lambda h, i: (i, h))
    kv_spec = pl.BlockSpec((s, LANES), lambda h, i: (0, h))
    return pl.pallas_call(
        body, name=name, grid=(N_HEADS, nq),
        in_specs=[q_spec, kv_spec, kv_spec],
        out_specs=[q_spec, pl.BlockSpec((1, t, 1), lambda h, i: (h, i, 0))],
        out_shape=[jax.ShapeDtypeStruct(q.shape, BF16), jax.ShapeDtypeStruct((N_HEADS, s, 1), F32)],
        compiler_params=_cparams(("parallel", "arbitrary")),
    )(q, k, v)


def _att_delta(o, do, *, name):
    s = o.shape[0]
    t = _att_tile(s)

    def body(o_ref, do_ref, dl_ref):
        dl_ref[0] = jnp.sum(do_ref[...].astype(F32) * o_ref[...].astype(F32), axis=1, keepdims=True)

    blk = pl.BlockSpec((t, LANES), lambda h, i: (i, h))
    return pl.pallas_call(
        body, name=name, grid=(N_HEADS, s // t), in_specs=[blk, blk],
        out_specs=pl.BlockSpec((1, t, 1), lambda h, i: (h, i, 0)),
        out_shape=jax.ShapeDtypeStruct((N_HEADS, s, 1), F32),
        compiler_params=_cparams(("parallel", "parallel")),
    )(o, do)


def _flash_bwd(q, k, v, do, lse_row, delta_row, *, name):
    s = q.shape[0]
    t = _att_tile(s)
    nq = s // t

    def body(q_ref, k_ref, v_ref, do_ref, lse_ref, dl_ref, dq_ref, dk_ref, dv_ref):
        j = pl.program_id(1)
        kv, vv = k_ref[...], v_ref[...]

        @pl.when(j == 0)
        def _():
            dq_ref[...] = jnp.zeros_like(dq_ref)

        def step(i, carry, diagonal):
            dk, dv = carry
            rows = pl.ds(pl.multiple_of(i * t, t), t)
            qi, doi = q_ref[rows, :], do_ref[rows, :]
            p_t = jnp.exp(_dot(kv, qi, tb=True) - lse_ref[0, :, rows])
            if diagonal:
                p_t = jnp.where(_tri(t, transposed=True), p_t, 0.0)
            ds_t = (p_t * (_dot(vv, doi, tb=True) - dl_ref[0, :, rows])).astype(MXU_DTYPE)
            dq_ref[rows, :] += _dot(ds_t, kv, ta=True)
            return dk + _dot(ds_t, qi), dv + _dot(p_t, doi)

        zero = jnp.zeros((t, LANES), F32)
        carry = step(j, (zero, zero), True)
        dk, dv = lax.fori_loop(j + 1, nq, lambda i, c: step(i, c, False), carry)
        dk_ref[...] = dk
        dv_ref[...] = dv

        @pl.when(j == nq - 1)
        def _():
            dq_ref[...] = dq_ref[...] * ATT_SCALE

    q_spec = pl.BlockSpec((s, LANES), lambda h, j: (0, h))
    kv_spec = pl.BlockSpec((t, LANES), lambda h, j: (j, h))
    row_spec = pl.BlockSpec((1, 1, s), lambda h, j: (h, 0, 0))
    return pl.pallas_call(
        body, name=name, grid=(N_HEADS, nq),
        in_specs=[q_spec, kv_spec, kv_spec, q_spec, row_spec, row_spec],
        out_specs=[q_spec, kv_spec, kv_spec],
        out_shape=[jax.ShapeDtypeStruct(q.shape, F32)] * 3,
        compiler_params=_cparams(("parallel", "arbitrary")),
    )(q, k, v, do, lse_row, delta_row)


def _rope(v, cos_t, sin_p, sin_m):
    return v * cos_t + pltpu.roll(v, QK_ROPE // 2, 1) * sin_p + pltpu.roll(v, LANES - QK_ROPE // 2, 1) * sin_m


def _rope_t(d, cos_t, sin_p, sin_m):
    return d * cos_t + pltpu.roll(d * sin_p, LANES - QK_ROPE // 2, 1) + pltpu.roll(d * sin_m, QK_ROPE // 2, 1)


def _att_prep(q_pad, kv2, kr, cos_t, sin_p, sin_m, *, name):
    w = N_HEADS * LANES

    def fn(qv, kvv, krv, c, sp, sm):
        kr_rot = _rope(krv, c, sp, sm)
        qs, ks = [], []
        for h in range(N_HEADS):
            blk = slice(h * LANES, (h + 1) * LANES)
            qs.append(_rope(qv[:, blk], c, sp, sm) * ATT_SCALE)
            ks.append(kvv[:, blk] + kr_rot)
        return jnp.concatenate(qs, axis=1), jnp.concatenate(ks, axis=1), kvv[:, w:]

    return _rowwise(fn, [q_pad, kv2, kr, cos_t, sin_p, sin_m], [],
                    [(w, BF16, "row"), (w, BF16, "row"), (w, BF16, "row")], name=name)


def _att_prep_bwd(dq, dk, cos_t, sin_p, sin_m, *, name):
    w = N_HEADS * LANES

    def fn(dqv, dkv, c, sp, sm):
        outs, dkr = [], None
        for h in range(N_HEADS):
            blk = slice(h * LANES, (h + 1) * LANES)
            outs.append(_rope_t(dqv[:, blk], c, sp, sm))
            dkr = dkv[:, blk] if dkr is None else dkr + dkv[:, blk]
        return jnp.concatenate(outs, axis=1), _rope_t(dkr, c, sp, sm)

    return _rowwise(fn, [dq, dk, cos_t, sin_p, sin_m], [], [(w, BF16, "row"), (LANES, F32, "row")], name=name)


_ANY = pl.BlockSpec(memory_space=pl.ANY)
_MESH = pl.DeviceIdType.MESH


def _mesh_pos():
    return lax.axis_index("x"), lax.axis_index("y"), lax.axis_index("c")


def _remote(src, dst, send_sem, recv_sem, dev):
    return pltpu.make_async_remote_copy(src_ref=src, dst_ref=dst, send_sem=send_sem, recv_sem=recv_sem,
                                        device_id=dev, device_id_type=_MESH)


def _other_chips(x, y):
    chips = [(1 - x, y), (x, 1 - y), (1 - x, 1 - y)]
    return chips, [2 * cx + cy for cx, cy in chips]


def _comm_call(body, ins, out_shapes, n_sems, *, name):
    return pl.pallas_call(
        body, name=name, in_specs=[_ANY] * len(ins), out_specs=[_ANY] * len(out_shapes), out_shape=out_shapes,
        scratch_shapes=[pltpu.SemaphoreType.DMA((k,)) for k in n_sems],
    )(*ins)


def _gather_halves(shards):
    n = len(shards)
    halves = [t.shape[0] // 2 for t in shards]

    def body(*refs):
        xs, outs = refs[:n], refs[n:2 * n]
        send_sems, recv_sems = refs[2 * n:]
        x, y, c = _mesh_pos()
        k = 2 * x + y
        sibling = (x, y, 1 - c)
        chips, ks = _other_chips(x, y)
        half = lambda w, hf: pl.ds(hf * halves[w], halves[w])
        first = [_remote(xs[w].at[half(w, c)], outs[w].at[k, half(w, c)], send_sems.at[6 * w + j], recv_sems.at[6 * w + j],
                         (*chips[j], c)) for w in range(n) for j in range(3)]
        for cp in first:
            cp.start()
        passed = []
        for j in range(3):
            for w in range(n):
                land = outs[w].at[ks[j], half(w, c)]
                _remote(land, land, send_sems.at[6 * w + j], recv_sems.at[6 * w + j], sibling).wait_recv()
                passed.append(_remote(land, land, send_sems.at[6 * w + 3 + j], recv_sems.at[6 * w + 3 + j], sibling))
                passed[-1].start()
        for j in range(3):
            for w in range(n):
                land = outs[w].at[ks[j], half(w, 1 - c)]
                _remote(land, land, send_sems.at[6 * w + 3 + j], recv_sems.at[6 * w + 3 + j], sibling).wait_recv()
        for cp in first + passed:
            cp.wait_send()

    shapes = [jax.ShapeDtypeStruct((4,) + t.shape, t.dtype) for t in shards]
    return _comm_call(body, shards, shapes, (6 * n, 6 * n), name="gather_halves")


_HBM = pl.BlockSpec(memory_space=pltpu.HBM)
_SEM = pl.BlockSpec(memory_space=pltpu.SEMAPHORE)
_EFFECT = pltpu.SideEffectType.DATAFLOW_SIDE_EFFECTING


def _push_start(blocks, *, scatter, name):
    n = len(blocks)

    def body(*refs):
        xs, lands = refs[:n], refs[n:2 * n]
        send_sems, recv_sems = refs[2 * n], refs[2 * n + 1]
        token = refs[-1]
        x, y, c = _mesh_pos()
        k = 2 * x + y
        chips, ks = _other_chips(x, y)
        for w in range(n):
            for j in range(3):
                src = xs[w].at[ks[j]] if scatter else xs[w]
                _remote(src, lands[w].at[k], send_sems.at[3 * w + j], recv_sems.at[3 * w + j], (*chips[j], c)).start()
        token[...] = jnp.zeros_like(token)

    hbm = lambda shape, dtype: pltpu.with_memory_space_constraint(lax.empty(shape, dtype), pltpu.HBM)
    ins = [pltpu.with_memory_space_constraint(t, pltpu.HBM) for t in blocks]
    ins += [hbm(t.shape if scatter else (4,) + t.shape, t.dtype) for t in blocks]
    out_shape = [pltpu.SemaphoreType.DMA((3 * n,)), pltpu.SemaphoreType.DMA((3 * n,))]
    out_shape += [pltpu.HBM(t.shape, t.dtype) for t in ins]
    out_shape += [jax.ShapeDtypeStruct((8, LANES), F32)]
    res = pl.pallas_call(
        body, name=name, out_shape=out_shape, in_specs=[_HBM] * (2 * n),
        out_specs=[_SEM, _SEM] + [_HBM] * (2 * n) + [pl.BlockSpec(memory_space=pltpu.VMEM)],
        input_output_aliases={i: 2 + i for i in range(2 * n)},
        compiler_params=pltpu.CompilerParams(has_side_effects=_EFFECT),
    )(*ins)
    return res[0], res[1], res[2:2 + n], res[2 + n:2 + 2 * n], res[-1]


def _push_wait(send_sems, recv_sems, blocks, lands, after, *, name):
    n = len(blocks)

    def body(*refs):
        lands_in = refs[n:2 * n]
        send_sems, recv_sems = refs[2 * n], refs[2 * n + 1]
        x, y, c = _mesh_pos()
        chips, ks = _other_chips(x, y)
        for w in range(n):
            for j in range(3):
                slot = lands_in[w].at[ks[j]]
                cp = _remote(slot, slot, send_sems.at[3 * w + j], recv_sems.at[3 * w + j], (*chips[j], c))
                cp.wait_send()
                cp.wait_recv()

    out_shape = [pltpu.HBM(t.shape, t.dtype) for t in list(blocks) + list(lands)]
    res = pl.pallas_call(
        body, name=name, out_shape=out_shape,
        in_specs=[_HBM] * (2 * n) + [_SEM, _SEM, pl.BlockSpec(memory_space=pl.ANY)], out_specs=[_HBM] * (2 * n),
        input_output_aliases={i: i for i in range(2 * n)},
        compiler_params=pltpu.CompilerParams(has_side_effects=_EFFECT),
    )(*blocks, *lands, send_sems, recv_sems, after)
    return res[:n], res[n:]


def _send_half(views, *, name):
    n = len(views)

    def body(*refs):
        vs, outs = refs[:n], refs[n:2 * n]
        send_sems, recv_sems = refs[2 * n:]
        x, y, c = _mesh_pos()
        cps = []
        for w in range(n):
            h = views[w].shape[1] // 2
            cps.append(_remote(vs[w].at[:, pl.ds((1 - c) * h, h), :], outs[w], send_sems.at[w], recv_sems.at[w],
                               (x, y, 1 - c)))
            cps[-1].start()
        for cp in cps:
            cp.wait()

    shapes = [jax.ShapeDtypeStruct((t.shape[0], t.shape[1] // 2, t.shape[2]), t.dtype) for t in views]
    return _comm_call(body, views, shapes, (n, n), name=name)


def _swap_with_sibling(mine, *, name):
    n = len(mine)

    def body(*refs):
        hs, outs = refs[:n], refs[n:2 * n]
        send_sems, recv_sems = refs[2 * n:]
        x, y, c = _mesh_pos()
        cps = [_remote(hs[w], outs[w], send_sems.at[w], recv_sems.at[w], (x, y, 1 - c)) for w in range(n)]
        for cp in cps:
            cp.start()
        for cp in cps:
            cp.wait()

    shapes = [jax.ShapeDtypeStruct(t.shape, t.dtype) for t in mine]
    return _comm_call(body, mine, shapes, (n, n), name=name)


def _gather_all(vec, *, name):
    r, w = vec.shape

    def body(v_ref, out_ref, send_sems, recv_sems):
        x, y, c = _mesh_pos()

        def slot(px, py, pc):
            return out_ref.at[4 * px + 2 * py + pc]

        peers = []
        for rel in range(1, 8):
            fx, fy, fc = (rel >> 2) & 1, (rel >> 1) & 1, rel & 1
            peers.append((x ^ fx, y ^ fy, c ^ fc))
        cps = [_remote(v_ref, slot(x, y, c), send_sems.at[j], recv_sems.at[j], peer) for j, peer in enumerate(peers)]
        for cp in cps:
            cp.start()
        for j, peer in enumerate(peers):
            _remote(slot(*peer), slot(*peer), send_sems.at[j], recv_sems.at[j], peer).wait_recv()
        for cp in cps:
            cp.wait_send()

    others = pl.pallas_call(
        body, name=name, in_specs=[_ANY], out_specs=_ANY,
        out_shape=jax.ShapeDtypeStruct((8, r, w), vec.dtype),
        scratch_shapes=[pltpu.SemaphoreType.DMA((7,)), pltpu.SemaphoreType.DMA((7,))],
    )(vec)
    me = 4 * lax.axis_index("x") + 2 * lax.axis_index("y") + lax.axis_index("c")
    return lax.dynamic_update_index_in_dim(others, vec, me, 0)


def _row_tile(rows, row_bytes):
    for tm in (1024, 512, 256, 128, 64, 32, 16):
        if rows % tm == 0 and tm * row_bytes <= ELEMENTWISE_BLOCK_BYTES:
            return tm
    return 16 if rows % 16 == 0 else rows


def _chip_sum_half(g, got, c, *, name):
    nb, r, w = g.shape
    half = r // 2
    tm = _row_tile(half, w * 4)
    per = half // tm

    def body(c_ref, g_ref, o_ref, out_ref):
        out_ref[...] = (g_ref[...] + o_ref[...]).astype(out_ref.dtype)

    return pl.pallas_call(
        body, name=name,
        grid_spec=pltpu.PrefetchScalarGridSpec(
            num_scalar_prefetch=1, grid=(nb, per),
            in_specs=[pl.BlockSpec((1, tm, w), lambda b, i, c_ref: (b, c_ref[0] * per + i, 0)),
                      pl.BlockSpec((1, tm, w), lambda b, i, c_ref: (b, i, 0))],
            out_specs=pl.BlockSpec((1, tm, w), lambda b, i, c_ref: (b, i, 0))),
        out_shape=jax.ShapeDtypeStruct((nb, half, w), BF16),
        compiler_params=_cparams(("parallel", "parallel")),
    )(jnp.reshape(c, (1,)).astype(jnp.int32), g, got)


def _sum_slots(stack, *, name):
    n, r, w = stack.shape
    tm = _row_tile(r, n * w * stack.dtype.itemsize)

    def body(s_ref, out_ref):
        acc = s_ref[0].astype(F32)
        for i in range(1, n):
            acc = acc + s_ref[i].astype(F32)
        out_ref[...] = acc

    return pl.pallas_call(
        body, name=name, grid=(r // tm,),
        in_specs=[pl.BlockSpec((n, tm, w), lambda i: (0, i, 0))],
        out_specs=pl.BlockSpec((tm, w), lambda i: (i, 0)),
        out_shape=jax.ShapeDtypeStruct((r, w), F32),
        compiler_params=_cparams(("parallel",)),
    )(stack)


def _adam_math(wv, gv, mv, vv):
    m_new = ADAM_B1 * mv + (1.0 - ADAM_B1) * gv
    v_new = ADAM_B2 * vv + (1.0 - ADAM_B2) * (gv * gv)
    m_hat = m_new / (1.0 - ADAM_B1 ** ADAM_STEP)
    v_hat = v_new / (1.0 - ADAM_B2 ** ADAM_STEP)
    delta = -ADAM_LR * (m_hat / (jnp.sqrt(v_hat) + ADAM_EPS) + ADAM_WD * wv)
    return delta, m_new, v_new


def _adamw(w, g, m, v, *, name):
    shape = w.shape
    cols = shape[-1]
    flat = lambda t: t.reshape(-1, cols)
    rows = flat(w).shape[0]
    tm = _pick(rows, (256, 128, 64, 32, 16, 8))
    outs = _rowwise(_adam_math, [flat(w), flat(g), flat(m), flat(v)], [], [(cols, F32, "row")] * 3, name=name, tm=tm)
    return tuple(o.reshape(shape) for o in outs)


def _adamw_slots(w, slots, m, v, c, *, name):
    shape = w.shape
    cols = shape[-1]
    half = slots[0][0].shape[1]
    v4 = lambda t: t.reshape(2, 2, half, cols)
    assert all(s.shape == (4, half, cols) for pair in slots for s in pair) and w.size == 4 * half * cols
    tm = _row_tile(half, cols * 4 * 4)

    def body(c_ref, w_ref, m0_ref, o0_ref, m1_ref, o1_ref, m_ref, v_ref, g_ref, d_ref, mo_ref, vo_ref):
        first = pl.program_id(0) == 0
        own = pl.program_id(1) == c_ref[0]
        g = None
        for i in range(4):
            part = jnp.where(first, jnp.where(own, m0_ref[i], o0_ref[i]), jnp.where(own, m1_ref[i], o1_ref[i]))
            g = part.astype(F32) if g is None else g + part.astype(F32)
        delta, m_new, v_new = _adam_math(w_ref[0, 0], g, m_ref[0, 0], v_ref[0, 0])
        g_ref[0, 0], d_ref[0, 0], mo_ref[0, 0], vo_ref[0, 0] = g, delta, m_new, v_new

    blk = pl.BlockSpec((1, 1, tm, cols), lambda l, hf, i, c_ref: (l, hf, i, 0))

    def slot_spec(layer, mine):
        def index(l, hf, i, c_ref):
            same_half = hf * c_ref[0] + (1 - hf) * (1 - c_ref[0])
            use = (l if layer else 1 - l) * (same_half if mine else 1 - same_half)
            return (0, i * use, 0)
        return pl.BlockSpec((4, tm, cols), index)

    outs = pl.pallas_call(
        body, name=name,
        grid_spec=pltpu.PrefetchScalarGridSpec(
            num_scalar_prefetch=1, grid=(2, 2, half // tm),
            in_specs=[blk, slot_spec(0, True), slot_spec(0, False), slot_spec(1, True), slot_spec(1, False), blk, blk],
            out_specs=[blk] * 4),
        out_shape=[jax.ShapeDtypeStruct((2, 2, half, cols), F32)] * 4,
        compiler_params=_cparams(("arbitrary", "arbitrary", "arbitrary")),
    )(jnp.reshape(c, (1,)).astype(jnp.int32), v4(w), slots[0][0], slots[0][1], slots[1][0], slots[1][1], v4(m), v4(v))
    return tuple(o.reshape(shape) for o in outs)


def _pad_blocks(w, axis, n_blocks, real, to=LANES, offset=0):
    axis = axis % w.ndim
    shp = w.shape
    w = w.reshape(shp[:axis] + (n_blocks, real) + shp[axis + 1:])
    pads = [(0, 0)] * w.ndim
    pads[axis + 1] = (offset, to - real - offset)
    w = jnp.pad(w, pads)
    return w.reshape(shp[:axis] + (n_blocks * to,) + shp[axis + 1:])


def _unpad_blocks(w, axis, n_blocks, real, to=LANES, offset=0):
    axis = axis % w.ndim
    shp = w.shape
    w = w.reshape(shp[:axis] + (n_blocks, to) + shp[axis + 1:])
    w = lax.slice_in_dim(w, offset, offset + real, axis=axis + 1)
    return w.reshape(shp[:axis] + (n_blocks * real,) + shp[axis + 1:])


def _block_diag(w):
    n, a, b = w.shape
    eye = jnp.eye(n, dtype=w.dtype)
    return (eye[:, None, :, None] * w[:, :, None, :]).reshape(n * a, n * b)


def _block_diag_t(d, n):
    a, b = d.shape[0] // n, d.shape[1] // n
    d = d.reshape(n, a, n, b)
    return jnp.stack([d[i, :, i, :] for i in range(n)])


_SPLITS = np.cumsum((0,) + SPLIT_SIZES)


def _w_in_groups(w_in):
    sl = lambda i: w_in[:, _SPLITS[i]:_SPLITS[i + 1]]
    xbc = sl(5)
    xbc_pad = jnp.concatenate([_pad_blocks(xbc[:, :MIX], 1, N_HEADS, HEAD),
                               _pad_blocks(xbc[:, MIX:MIX + 2 * HEAD], 1, 2, HEAD),
                               _pad_blocks(xbc[:, MIX + 2 * HEAD:], 1, 2, HEAD)], axis=1)
    return dict(
        cq=sl(0), ckv=sl(1), kr=_pad_blocks(sl(2), 1, 1, QK_ROPE, offset=HEAD), pool=sl(3),
        z=_pad_blocks(sl(4), 1, N_HEADS, HEAD), xbc=xbc_pad, dt=_pad_blocks(sl(6), 1, 1, N_HEADS),
        lru_g=sl(7), lru_x=sl(8), gates=sl(9))


def _w_in_fused(groups):
    parts, at = [], 0
    for name, off, width in IN_LAYOUT:
        assert groups[name].shape[1] == width and off >= at
        if off > at:
            parts.append(jnp.zeros((groups[name].shape[0], off - at), groups[name].dtype))
        parts.append(groups[name])
        at = off + width
    parts.append(jnp.zeros((parts[0].shape[0], IN_ALL_COLS - at), parts[0].dtype))
    return jnp.concatenate(parts, axis=1)


def _in_cols(arr, name):
    off, width = IN_OFFSETS[name]
    return _Cols(arr, off, width)


def _w_in_ungroup(d):
    xbc = d["xbc"]
    w = N_HEADS * LANES
    xbc_real = jnp.concatenate([_unpad_blocks(xbc[:, :w], 1, N_HEADS, HEAD),
                                _unpad_blocks(xbc[:, w:w + 2 * LANES], 1, 2, HEAD),
                                _unpad_blocks(xbc[:, w + 2 * LANES:], 1, 2, HEAD)], axis=1)
    return jnp.concatenate([d["cq"], d["ckv"], _unpad_blocks(d["kr"], 1, 1, QK_ROPE, offset=HEAD), d["pool"],
                            _unpad_blocks(d["z"], 1, N_HEADS, HEAD), xbc_real, _unpad_blocks(d["dt"], 1, 1, N_HEADS),
                            d["lru_g"], d["lru_x"], d["gates"]], axis=1)


def _pad_xbc_vec(v):
    return jnp.concatenate([_pad_blocks(v[..., :MIX], -1, N_HEADS, HEAD),
                            _pad_blocks(v[..., MIX:MIX + 2 * HEAD], -1, 2, HEAD),
                            _pad_blocks(v[..., MIX + 2 * HEAD:], -1, 2, HEAD)], axis=-1)


def _unpad_xbc_vec(v):
    w = N_HEADS * LANES
    return jnp.concatenate([_unpad_blocks(v[..., :w], -1, N_HEADS, HEAD),
                            _unpad_blocks(v[..., w:w + 2 * LANES], -1, 2, HEAD),
                            _unpad_blocks(v[..., w + 2 * LANES:], -1, 2, HEAD)], axis=-1)


def _layer_weights(p):
    q = dict(p)
    q["in_all"] = _w_in_fused(_w_in_groups(p["w_in"]))
    q["uq"] = _pad_blocks(p["w_uq"], 1, N_HEADS, HEAD + QK_ROPE)
    ukv = p["w_ukv"].reshape(KV_LORA, N_HEADS, 2 * HEAD)
    q["ukv"] = jnp.concatenate([_pad_blocks(ukv[:, :, :HEAD].reshape(KV_LORA, -1), 1, N_HEADS, HEAD),
                                _pad_blocks(ukv[:, :, HEAD:].reshape(KV_LORA, -1), 1, N_HEADS, HEAD)], axis=1)
    q["pool_bd"] = _block_diag(p["w_pool"])
    q["lru_bd"] = jnp.concatenate([_block_diag(p["lru_w_a"]), _block_diag(p["lru_w_i"])], axis=1)
    q["br"] = [_pad_blocks(p["w_branch"][0], 0, N_HEADS, HEAD), p["w_branch"][1],
               _pad_blocks(p["w_branch"][2], 0, N_HEADS, HEAD), p["w_branch"][3]]
    q["ssd_conv_w_pad"] = _pad_xbc_vec(p["ssd_conv_w"])
    q["ssd_conv_b_pad"] = _pad_xbc_vec(p["ssd_conv_b"])[None, :]
    q["ssd_norm_pad"] = _pad_blocks(p["ssd_norm"], 0, N_HEADS, HEAD)[None, :]
    return q


def _row(v):
    return v.reshape(1, -1)


def _scal3(v):
    return v.reshape(N_HEADS, 1, 1)


def _layer_fwd(x, p_emb, w, rope, tag):
    n = lambda s: f"{s}_{tag}"
    sv = {"x": x}
    h = _rms_fwd(x, _row(w["g_mix"]), name=n("rms_mix"))
    sv["h"] = h
    u_all = _mm(h, w["in_all"], name=n("in_proj"))
    u = {k: _in_cols(u_all, k) for k in IN_OFFSETS}
    sv["u"] = u

    cqn = _rms_fwd(u["cq"], _row(w["q_norm"]), name=n("rms_q"))
    ckvn = _rms_fwd(u["ckv"], _row(w["kv_norm"]), name=n("rms_kv"))
    q_pad = _mm(cqn, w["uq"], name=n("uq"))
    kv2 = _mm(ckvn, w["ukv"], name=n("ukv"))
    qc, kc, vc = _att_prep(q_pad, kv2, u["kr"], *rope, name=n("att_prep"))
    y_a, lse = _flash_fwd(qc, kc, vc, name=n("flash_fwd"))
    sv.update(cqn=cqn, ckvn=ckvn, qc=qc, kc=kc, vc=vc, y_a=y_a, lse=lse)

    pool_d = _pool_fwd(u["pool"], name=n("pool_fwd"))
    yb_pre, y_b = _mm(pool_d, w["pool_bd"], epilogue=lambda acc, sc: (acc, acc * sc),
                      rowvecs=[_row(w["pool_scale"])], out_dtypes=(F32, BF16), name=n("pool_mm"))
    sv.update(pool_d=pool_d, yb_pre=yb_pre, y_b=y_b)

    xbc_c = _conv_fwd(u["xbc"], w["ssd_conv_w_pad"], w["ssd_conv_b_pad"], silu=True, name=n("ssd_conv"))
    dt8 = lax.slice_in_dim(u_all, IN_OFFSETS["dt"][0], IN_OFFSETS["dt"][0] + N_HEADS, axis=1)
    dtcol = dt8.T[:, :, None]
    dtrow = dt8.T[:, None, :]
    ssd_par = (_scal3(w["ssd_dt_bias"]), _scal3(w["ssd_a_log"]), _scal3(w["ssd_d"]))
    y_ssd, states = _ssd_fwd(xbc_c, dtcol, dtrow, *ssd_par, name=n("ssd_fwd"))

    def ssd_post(yv, zv, gv):
        xh, _ = _rms_parts(yv * _silu(zv), MIX)
        return xh * gv

    y_c = _rowwise(ssd_post, [y_ssd, u["z"]], [w["ssd_norm_pad"]], [(N_HEADS * LANES, BF16, "row")], name=n("ssd_post"))
    sv.update(xbc_c=xbc_c, dtcol=dtcol, dtrow=dtrow, y_ssd=y_ssd, states=states, y_c=y_c)

    xc = _conv_fwd(u["lru_x"], w["lru_conv_w"], _row(w["lru_conv_b"]), silu=False, name=n("lru_conv"))
    pre = _mm(xc, w["lru_bd"], name=n("lru_mm"))
    lru_par = (_row(w["lru_lambda"]), _row(w["lru_b_a"]), _row(w["lru_b_i"]))
    y_d, h_lru = _lru_fwd(pre, xc, u["lru_g"], *lru_par, name=n("lru_fwd"))
    sv.update(xc=xc, pre=pre, h_lru=h_lru, y_d=y_d)

    merged, *ybs = _branch_merge([y_a, y_b, y_c, y_d], w["br"], u_all, name=n("branch_merge"))
    x1 = _mm(merged, w["w_out"], epilogue=lambda acc, xr: (acc + xr,), tiles=[x], name=n("out_proj"))
    sv.update(ybs=ybs, merged=merged, x1=x1)

    h2 = _rms_fwd(x1, _row(w["g_mlp"]), name=n("rms_mlp"))
    a_ff, f_ff = _mm(h2, w["w_ff1"], epilogue=lambda acc: (acc, jnp.square(jnp.maximum(acc, 0.0))),
                     out_dtypes=(BF16, BF16), name=n("ff1"))
    x2 = _mm(f_ff, w["w_ff2"], epilogue=lambda acc, xr: (acc + xr,), tiles=[x1], name=n("ff2"))
    sv.update(h2=h2, a_ff=a_ff, f_ff=f_ff, x2=x2)

    h3 = _rms_fwd(x2, _row(w["g_ple"]), name=n("rms_ple"))
    e_ple = _mm(p_emb, w["w_ple"], name=n("ple_emb"))
    x3, gt_ple = _mm(h3, w["w_ple_gate"], epilogue=lambda acc, ev, xr: (xr + ev * _sigmoid(acc), _sigmoid(acc)),
                     tiles=[e_ple, x2], out_dtypes=(F32, F32), name=n("ple_gate"))
    sv.update(h3=h3, e_ple=e_ple, gt_ple=gt_ple, p_emb=p_emb)
    return x3, sv


def _layer_bwd(dx3, sv, w, rope, tag):
    n = lambda s: f"{s}_{tag}"
    gr = {}
    u = sv["u"]

    de, dpre = _rowwise(lambda d, gt, ev: (d * gt, d * ev * gt * (1.0 - gt)), [dx3, sv["gt_ple"], sv["e_ple"]], [],
                        [(D_MODEL, BF16, "row"), (D_MODEL, BF16, "row")], name=n("ple_bwd"))
    gr["w_ple"] = _mm(sv["p_emb"], de, ta=True, name=n("d_w_ple"))
    gr["w_ple_gate"] = _mm(sv["h3"], dpre, ta=True, name=n("d_w_ple_gate"))
    dh3 = _mm(dpre, w["w_ple_gate"], tb=True, out_dtypes=(BF16,), name=n("d_h3"))
    dx2, dg = _rms_bwd(sv["x2"], _row(w["g_ple"]), dh3, dx3, name=n("rms_ple_bwd"))
    gr["g_ple"] = dg[0]

    gr["w_ff2"] = _mm(sv["f_ff"], dx2, ta=True, name=n("d_w_ff2"))
    da = _mm(dx2, w["w_ff2"], tb=True, epilogue=lambda acc, av: (acc * 2.0 * jnp.maximum(av, 0.0),),
             tiles=[sv["a_ff"]], out_dtypes=(BF16,), name=n("d_a_ff"))
    gr["w_ff1"] = _mm(sv["h2"], da, ta=True, name=n("d_w_ff1"))
    dh2 = _mm(da, w["w_ff1"], tb=True, out_dtypes=(BF16,), name=n("d_h2"))
    dx1, dg = _rms_bwd(sv["x1"], _row(w["g_mlp"]), dh2, dx2, name=n("rms_mlp_bwd"))
    gr["g_mlp"] = dg[0]

    gr["w_out"] = _mm(sv["merged"], dx1, ta=True, name=n("d_w_out"))
    dmerged = _mm(dx1, w["w_out"], tb=True, name=n("d_merged"))

    def merge_bwd(dm, gts, y0, y1, y2, y3):
        dys, dgs = [], []
        for b, yb in enumerate((y0, y1, y2, y3)):
            sg = _sigmoid(gts[:, b * D_MODEL:(b + 1) * D_MODEL])
            dys.append(dm * sg)
            dgs.append(dm * yb * sg * (1.0 - sg))
        return (*dys, jnp.concatenate(dgs, axis=1))

    *dybs, dgates = _rowwise(merge_bwd, [dmerged, u["gates"]] + sv["ybs"], [],
                             [(D_MODEL, BF16, "row")] * 4 + [(4 * D_MODEL, BF16, "row")], name=n("merge_bwd"))
    ys = [sv["y_a"], sv["y_b"], sv["y_c"], sv["y_d"]]
    dwb = [_mm(ys[b], dybs[b], ta=True, name=n(f"d_w_branch{b}")) for b in range(4)]
    gr["w_branch"] = jnp.stack([_unpad_blocks(dwb[0], 0, N_HEADS, HEAD), dwb[1],
                                _unpad_blocks(dwb[2], 0, N_HEADS, HEAD), dwb[3]])
    dy_a = _mm(dybs[0], w["br"][0], tb=True, out_dtypes=(BF16,), name=n("d_y_a"))
    dy_b = _mm(dybs[1], w["br"][1], tb=True, name=n("d_y_b"))
    dy_c = _mm(dybs[2], w["br"][2], tb=True, name=n("d_y_c"))
    dy_d = _mm(dybs[3], w["br"][3], tb=True, name=n("d_y_d"))
    du = {"gates": dgates}

    lru_par = (_row(w["lru_lambda"]), _row(w["lru_b_a"]), _row(w["lru_b_i"]))
    dpa, dpi, dxc_direct, du["lru_g"], dlam, dba, dbi = _lru_bwd(
        sv["pre"], sv["xc"], u["lru_g"], *lru_par, sv["h_lru"], dy_d, name=n("lru_bwd"))
    dpre_lru = jnp.concatenate([dpa, dpi], axis=1)
    d_bd = _mm(sv["xc"], dpre_lru, ta=True, name=n("d_lru_w"))
    gr["lru_w_a"] = _block_diag_t(d_bd[:, :MIX], N_HEADS)
    gr["lru_w_i"] = _block_diag_t(d_bd[:, MIX:], N_HEADS)
    gr["lru_lambda"], gr["lru_b_a"], gr["lru_b_i"] = dlam[0], dba[0], dbi[0]
    dxc = _mm(dpre_lru, w["lru_bd"], tb=True, epilogue=lambda acc, t: (acc + t,), tiles=[dxc_direct], name=n("d_xc"))
    du["lru_x"], gr["lru_conv_w"], dcb = _conv_bwd(u["lru_x"], w["lru_conv_w"], _row(w["lru_conv_b"]), dxc,
                                                  silu=False, name=n("lru_conv_bwd"))
    gr["lru_conv_b"] = dcb[0]

    def ssd_post_bwd(dyc, yv, zv, gv):
        sz = _silu(zv)
        dyz, dgain = _rms_bwd_math(yv * sz, gv, dyc, MIX)
        return dyz * sz, dyz * yv * _silu_grad(zv), dgain

    dy_ssd, du["z"], dgain = _rowwise(ssd_post_bwd, [dy_c, sv["y_ssd"], u["z"]], [w["ssd_norm_pad"]],
                                      [(N_HEADS * LANES, F32, "row"), (N_HEADS * LANES, BF16, "row"),
                                       (N_HEADS * LANES, F32, "acc")], name=n("ssd_post_bwd"))
    gr["ssd_norm"] = _unpad_blocks(dgain[0], 0, N_HEADS, HEAD)
    ssd_par = (_scal3(w["ssd_dt_bias"]), _scal3(w["ssd_a_log"]), _scal3(w["ssd_d"]))
    dxs, dbg, dcg, ddt, dbias, dalog, dd = _ssd_bwd(sv["xbc_c"], sv["dtcol"], sv["dtrow"], *ssd_par, sv["states"],
                                                    dy_ssd, name=n("ssd_bwd"))
    s = dxs.shape[0]
    dxbc_c = jnp.concatenate([dxs, dbg, dcg], axis=1)
    gr["ssd_dt_bias"], gr["ssd_a_log"], gr["ssd_d"] = dbias[:, 0, 0], dalog[:, 0, 0], dd[:, 0, 0]
    du["xbc"], dcw, dcb = _conv_bwd(u["xbc"], w["ssd_conv_w_pad"], w["ssd_conv_b_pad"], dxbc_c, silu=True,
                                    name=n("ssd_conv_bwd"))
    gr["ssd_conv_w"], gr["ssd_conv_b"] = _unpad_xbc_vec(dcw), _unpad_xbc_vec(dcb[0])
    du["dt"] = jnp.pad(ddt[:, :, 0].T, ((0, 0), (0, LANES - N_HEADS)))

    dyb_pre, dscale = _rowwise(lambda d, yp, sc: (d * sc, _colsum(d * yp)), [dy_b, sv["yb_pre"]],
                               [_row(w["pool_scale"])], [(MIX, BF16, "row"), (MIX, F32, "acc")], name=n("pool_scale_bwd"))
    gr["pool_scale"] = dscale[0]
    gr["w_pool"] = _block_diag_t(_mm(sv["pool_d"], dyb_pre, ta=True, name=n("d_w_pool")), 4)
    dd_pool = _mm(dyb_pre, w["pool_bd"], tb=True, name=n("d_pool_d"))
    du["pool"] = _pool_bwd(dd_pool, name=n("pool_bwd"))

    delta = _att_delta(sv["y_a"], dy_a, name=n("att_delta"))
    to_row = lambda t: t.reshape(N_HEADS, 1, s)
    dqc, dkc, dvc = _flash_bwd(sv["qc"], sv["kc"], sv["vc"], dy_a, to_row(sv["lse"]), to_row(delta), name=n("flash_bwd"))
    dq_pad, du["kr"] = _att_prep_bwd(dqc, dkc, *rope, name=n("att_prep_bwd"))
    d_uq = _mm(sv["cqn"], dq_pad, ta=True, name=n("d_w_uq"))
    gr["w_uq"] = _unpad_blocks(d_uq, 1, N_HEADS, HEAD + QK_ROPE)
    dcqn = _mm(dq_pad, w["uq"], tb=True, out_dtypes=(BF16,), name=n("d_cqn"))
    du["cq"], dg = _rms_bwd(u["cq"], _row(w["q_norm"]), dcqn, name=n("rms_q_bwd"))
    gr["q_norm"] = dg[0]
    dkv2 = jnp.concatenate([dkc, dvc], axis=1).astype(BF16)
    d_ukv = _mm(sv["ckvn"], dkv2, ta=True, name=n("d_w_ukv"))
    wk = N_HEADS * LANES
    dk_real = _unpad_blocks(d_ukv[:, :wk], 1, N_HEADS, HEAD).reshape(KV_LORA, N_HEADS, HEAD)
    dv_real = _unpad_blocks(d_ukv[:, wk:], 1, N_HEADS, HEAD).reshape(KV_LORA, N_HEADS, HEAD)
    gr["w_ukv"] = jnp.concatenate([dk_real, dv_real], axis=2).reshape(KV_LORA, N_HEADS * 2 * HEAD)
    dckvn = _mm(dkv2, w["ukv"], tb=True, out_dtypes=(BF16,), name=n("d_ckvn"))
    du["ckv"], dg = _rms_bwd(u["ckv"], _row(w["kv_norm"]), dckvn, name=n("rms_kv_bwd"))
    gr["kv_norm"] = dg[0]

    du_all = _w_in_fused({k: v.astype(BF16) for k, v in du.items()})
    dw_all = _mm(sv["h"], du_all, ta=True, name=n("d_w_in"))
    gr["w_in"] = _w_in_ungroup({k: dw_all[:, off:off + width] for k, off, width in IN_LAYOUT})
    dh = _mm(du_all, w["in_all"], tb=True, name=n("d_h"))
    dx, dg = _rms_bwd(sv["x"], _row(w["g_mix"]), dh, dx1, name=n("rms_mix_bwd"))
    gr["g_mix"] = dg[0]
    return dx, gr


def _pack_rows(n_elems):
    per = PACK_W * PACK_ROWS
    return -(-n_elems // per) * PACK_ROWS


def _pack_flat(parts, dtype):
    flat = jnp.concatenate([p.reshape(-1).astype(dtype) for p in parts])
    rows = _pack_rows(flat.shape[0])
    return jnp.pad(flat, (0, rows * PACK_W - flat.shape[0])).reshape(rows, PACK_W)


def _unpack_flat(buf, shapes):
    lead = buf.shape[:-2]
    flat = buf.reshape(lead + (-1,))
    out, off = [], 0
    for shp in shapes:
        size = int(np.prod(shp))
        out.append(flat[..., off:off + size].reshape(lead + tuple(shp)))
        off += size
    return out


def _merge_shards(t, axis):
    return jnp.concatenate([t[i] for i in range(4)], axis=axis)


def _split_shards(t, axis):
    return jnp.stack(jnp.split(t, 4, axis=axis))


def _rope_tables(positions):
    inv = 1.0 / (ROPE_THETA ** (jnp.arange(0, QK_ROPE, 2, dtype=F32) / QK_ROPE))
    ang = positions.astype(F32)[:, None] * inv
    cos, sin = jnp.cos(ang), jnp.sin(ang)
    s = ang.shape[0]
    half = QK_ROPE // 2
    z = lambda n_: jnp.zeros((s, n_), F32)
    cos_t = jnp.concatenate([jnp.ones((s, HEAD), F32), cos, cos, jnp.ones((s, LANES - HEAD - QK_ROPE), F32)], axis=1)
    sin_p = jnp.concatenate([z(HEAD + half), sin, z(LANES - HEAD - QK_ROPE)], axis=1)
    sin_m = jnp.concatenate([z(HEAD), -sin, z(half + LANES - HEAD - QK_ROPE)], axis=1)
    return cos_t, sin_p, sin_m


def _loss_head(x, g, target, *, name):
    d = x.shape[1]

    def fn(xv, tv, gv):
        xh, r = _rms_parts(xv, d)
        y = xh * gv
        err = y - tv
        dy = err * (1.0 / d)
        dxh = dy * gv
        dx = r * (dxh - xh * (jnp.sum(dxh * xh, axis=-1, keepdims=True) * (1.0 / d)))
        return dx, _colsum(dy * xh), _colsum(err * err) * (0.5 / d)

    return _rowwise(fn, [x, target], [g], [(d, F32, "row"), (d, F32, "acc"), (d, F32, "acc")], name=name)


MATS = tuple((nm, ax) for nm, ax in BIG if nm not in CONV_SHARDED)


def _grad_view(g, ax_layer):
    if ax_layer == 0:
        return g.reshape(4, g.shape[0] // 4, g.shape[1])
    return g.reshape(1, -1, g.shape[-1])


def _reduce_start(grads_l, c_idx, tag):
    views = [_grad_view(grads_l[nm], ax - 1) for nm, ax in MATS]
    got = _send_half(views, name="send_half_" + tag)
    parts = []
    for (nm, ax), v, gt in zip(MATS, views, got):
        both = _chip_sum_half(v, gt, c_idx, name=f"chip_sum_{nm}_{tag}")
        parts.append(both if ax == 1 else _split_shards(both[0], 1))
    return _push_start(parts, scatter=True, name="push_grads_" + tag)


def _reduce_finish(state, after, k_chip, tag):
    send_sems, recv_sems, parts, lands, _ = state
    parts, landed = _push_wait(send_sems, recv_sems, parts, lands, after, name="wait_grads_" + tag)
    mine = [lax.dynamic_update_index_in_dim(t, lax.dynamic_index_in_dim(p, k_chip, 0, keepdims=False), k_chip, 0)
            for t, p in zip(landed, parts)]
    return list(zip(mine, _swap_with_sibling(mine, name="swap_halves_" + tag)))


def _step(args):
    x = args["x"][0]
    c_idx = lax.axis_index("c")
    k_chip = 2 * lax.axis_index("x") + lax.axis_index("y")

    mats = MATS
    mine = [[args[nm][l].astype(BF16) for nm, _ in mats] for l in range(2)]
    gathered0 = _gather_halves(mine[0])
    convs = [(nm, ax) for nm, ax in BIG if nm in CONV_SHARDED]
    conv_all = _gather_all(_pack_flat([args[nm] for nm, _ in convs], F32), name="gather_conv_taps")[0::2]
    mine1, gathered0, conv_all = lax.optimization_barrier((mine[1], gathered0, conv_all))
    gathered0 = [lax.dynamic_update_index_in_dim(t, own, k_chip, 0) for t, own in zip(gathered0, mine[0])]
    push1 = _push_start(mine1, scatter=False, name="push_weights_l1")
    full_conv = {nm: _merge_shards(t, ax)
                 for (nm, ax), t in zip(convs, _unpack_flat(conv_all, [args[nm].shape for nm, _ in convs]))}
    rope = _rope_tables(args["positions"][0])

    def layer_weights(l, gathered):
        p = {nm: _merge_shards(t, ax - 1) for (nm, ax), t in zip(mats, gathered)}
        p.update({nm: full_conv[nm][l] for nm in CONV_SHARDED})
        p.update({nm: args[nm][l] for nm in SMALL if nm != "g_final"})
        return _layer_weights(p)

    layers = [layer_weights(0, gathered0), None]
    layers[0]["g_mix"] = layers[0]["g_mix"] + push1[4][0, 0]
    x, sv0 = _layer_fwd(x, args["p"][0, 0], layers[0], rope, "l0")
    own1, landed1 = _push_wait(push1[0], push1[1], push1[2], push1[3], x, name="wait_weights_l1")
    layers[1] = layer_weights(1, [lax.dynamic_update_index_in_dim(t, own, k_chip, 0) for t, own in zip(landed1, own1)])
    x, sv1 = _layer_fwd(x, args["p"][1, 0], layers[1], rope, "l1")
    saved = [sv0, sv1]

    dx, dg_final, loss_part = _loss_head(x, _row(args["g_final"]), args["loss_target"][0], name="loss_head")
    loss = lax.psum(jnp.sum(loss_part), ("x", "y", "c"))

    grads = [None, None]
    dx, grads[1] = _layer_bwd(dx, saved[1], layers[1], rope, "l1")
    reduce1 = _reduce_start(grads[1], c_idx, "l1")
    dx, grads[0] = _layer_bwd(dx + reduce1[4][0, 0], saved[0], layers[0], rope, "l0")
    g_all = {nm: jnp.stack([grads[0][nm], grads[1][nm]]) for nm in SMALL + CONV_SHARDED if nm != "g_final"}
    g_all["g_final"] = dg_final[0]
    all_names = SMALL + CONV_SHARDED
    all_shapes = [g_all[nm].shape for nm in all_names]
    small_all = _gather_all(_pack_flat([g_all[nm] for nm in all_names], F32), name="gather_small_grads")
    g0_mats, small_all = lax.optimization_barrier(({nm: grads[0][nm] for nm, _ in MATS}, small_all))
    reduce0 = _reduce_start(g0_mats, c_idx, "l0")
    small_sum = _sum_slots(small_all, name="sum_devices")
    g_red = dict(zip(all_names, _unpack_flat(small_sum, all_shapes)))
    for nm, ax in BIG:
        if nm in CONV_SHARDED:
            width = args[nm].shape[ax]
            g_red[nm] = lax.dynamic_slice_in_dim(g_red[nm], k_chip * width, width, axis=ax)
    small_shapes = [args[nm].shape for nm in SMALL]
    pack_small = lambda src: _pack_flat([src(nm) for nm in SMALL], F32)
    upd_small = _adamw(pack_small(lambda nm: args[nm]), pack_small(lambda nm: g_red[nm]),
                       pack_small(lambda nm: args["m_" + nm]), pack_small(lambda nm: args["v_" + nm]), name="adamw_small")
    upd = {nm: trip for nm, trip in zip(SMALL, zip(*[_unpack_flat(t, small_shapes) for t in upd_small]))}
    for nm in CONV_SHARDED:
        upd[nm] = _adamw(args[nm], g_red[nm], args["m_" + nm], args["v_" + nm], name="adamw_" + nm)

    slots = [_reduce_finish(reduce0, upd_small[0], k_chip, "l0"), _reduce_finish(reduce1, dx, k_chip, "l1")]
    for i, (nm, _) in enumerate(MATS):
        g_red[nm], *upd[nm] = _adamw_slots(args[nm], [slots[0][i], slots[1][i]], args["m_" + nm], args["v_" + nm],
                                           c_idx, name="adamw_" + nm)

    outs = [loss, dx[None]]
    outs += [g_red[nm] for nm in WEIGHTS]
    for i in range(3):
        outs += [upd[nm][i] for nm in WEIGHTS]
    return tuple(outs)


_ARG_NAMES = ("x", "p", "positions") + WEIGHTS + ("loss_target",) + tuple("m_" + nm for nm in WEIGHTS) \
    + tuple("v_" + nm for nm in WEIGHTS)


def kernel(*arrays):
    assert len(arrays) == len(_ARG_NAMES), len(arrays)
    return _step(dict(zip(_ARG_NAMES, arrays)))
```

```python
import functools
import math

import jax
import jax.numpy as jnp
import numpy as np
from jax import lax
from jax.experimental import pallas as pl
from jax.experimental.pallas import tpu as pltpu

F32 = jnp.float32
BF16 = jnp.bfloat16
MXU_DTYPE = BF16
LANES = 128
VMEM_LIMIT = 56 * 1024 * 1024
MM_VMEM_BUDGET = 36 * 1024 * 1024
ELEMENTWISE_BLOCK_BYTES = 2 * 1024 * 1024

D_MODEL = 1024
N_HEADS = 8
HEAD = 64
QK_ROPE = 32
Q_LORA = 384
KV_LORA = 256
MIX = 512
SSD_CHUNK = 128
CONV_W = 4
POOL_WINDOWS = (2, 4, 8, 16)
LRU_C = 8.0
EPS = 1e-6
ROPE_THETA = 10000.0
ATT_SCALE = (HEAD + QK_ROPE) ** -0.5
SPLIT_SIZES = (Q_LORA, KV_LORA, QK_ROPE, MIX, MIX, 768, N_HEADS, MIX, MIX, 4 * D_MODEL)
IN_LAYOUT = (("gates", 0, 4096), ("z", 4096, 1024), ("pool", 5120, 512), ("lru_g", 5632, 512), ("lru_x", 6144, 512),
             ("cq", 6912, 384), ("ckv", 7424, 256), ("xbc", 7680, 1536), ("kr", 9216, 128), ("dt", 9344, 128))
IN_OFFSETS = {name: (off, width) for name, off, width in IN_LAYOUT}
IN_ALL_COLS = 9728

ADAM_LR, ADAM_B1, ADAM_B2, ADAM_EPS, ADAM_WD, ADAM_STEP = 0.001, 0.9, 0.999, 1e-08, 0.01, 10

BIG = (("w_in", 2), ("w_uq", 2), ("w_ukv", 2), ("ssd_conv_w", 2), ("lru_conv_w", 2), ("w_branch", 3),
       ("w_out", 1), ("w_ff1", 2), ("w_ff2", 1), ("w_ple_gate", 1), ("w_ple", 2))
SMALL = ("g_mix", "q_norm", "kv_norm", "w_pool", "pool_scale", "ssd_conv_b", "ssd_dt_bias", "ssd_a_log",
         "ssd_d", "ssd_norm", "lru_conv_b", "lru_w_a", "lru_b_a", "lru_w_i", "lru_b_i", "lru_lambda",
         "g_mlp", "g_ple", "g_final")
WEIGHTS = ("g_mix", "w_in", "q_norm", "w_uq", "kv_norm", "w_ukv", "w_pool", "pool_scale", "ssd_conv_w",
           "ssd_conv_b", "ssd_dt_bias", "ssd_a_log", "ssd_d", "ssd_norm", "lru_conv_w", "lru_conv_b", "lru_w_a",
           "lru_b_a", "lru_w_i", "lru_b_i", "lru_lambda", "w_branch", "w_out", "g_mlp", "w_ff1", "w_ff2", "g_ple",
           "w_ple_gate", "w_ple", "g_final")
CONV_SHARDED = ("ssd_conv_w", "lru_conv_w")
PACK_W = 1024
PACK_ROWS = 64


def _cparams(sem, vmem=VMEM_LIMIT):
    return pltpu.CompilerParams(dimension_semantics=sem, vmem_limit_bytes=vmem)


def _pick(n, cands):
    for c in cands:
        if n % c == 0:
            return c
    return n


class _Cols:
    def __init__(self, arr, off, width):
        self.arr, self.off, self.width = arr, off, width

    shape = property(lambda self: (self.arr.shape[0], self.width))
    dtype = property(lambda self: self.arr.dtype)


def _arr(x):
    return x.arr if isinstance(x, _Cols) else x


def _off(x, unit):
    off = x.off if isinstance(x, _Cols) else 0
    assert off % unit == 0, (off, unit)
    return off // unit


def _sigmoid(x):
    return 1.0 / (1.0 + jnp.exp(-x))


def _silu(x):
    return x * _sigmoid(x)


def _silu_grad(x):
    s = _sigmoid(x)
    return s * (1.0 + x * (1.0 - s))


def _softplus(x):
    e = jnp.exp(-jnp.abs(x))
    log1p_e = jnp.where(e < 1e-3, e * (1.0 - e * (0.5 - e * (1.0 / 3.0))), jnp.log(1.0 + e))
    return jnp.maximum(x, 0.0) + log1p_e


_GELU_C = math.sqrt(2.0 / math.pi)


def _gelu(x):
    t = jnp.tanh(_GELU_C * (x + 0.044715 * x * x * x))
    return 0.5 * x * (1.0 + t)


def _gelu_grad(x):
    t = jnp.tanh(_GELU_C * (x + 0.044715 * x * x * x))
    return 0.5 * (1.0 + t) + 0.5 * x * (1.0 - t * t) * _GELU_C * (1.0 + 3.0 * 0.044715 * x * x)


def _neg_expm1(x):
    series = -x * (1.0 + 0.5 * x * (1.0 + (1.0 / 3.0) * x * (1.0 + 0.25 * x)))
    return jnp.where(x > -0.05, series, 1.0 - jnp.exp(x))


def _shift_down(x, k, row):
    return jnp.where(row >= k, pltpu.roll(x, k, 0), 0.0)


def _shift_up(x, k, row):
    n = x.shape[0]
    return jnp.where(row < n - k, pltpu.roll(x, n - k, 0), 0.0)


def _cumsum_rows(x, row):
    d = 1
    while d < x.shape[0]:
        x = x + _shift_down(x, d, row)
        d *= 2
    return x


def _rev_cumsum_rows(x, row):
    d = 1
    while d < x.shape[0]:
        x = x + _shift_up(x, d, row)
        d *= 2
    return x


def _cumsum_lanes(x, col):
    d = 1
    while d < x.shape[1]:
        x = x + jnp.where(col >= d, pltpu.roll(x, d, 1), 0.0)
        d *= 2
    return x


def _dot(a, b, ta=False, tb=False):
    dn = (((0 if ta else 1,), (1 if tb else 0,)), ((), ()))
    return lax.dot_general(a.astype(MXU_DTYPE), b.astype(MXU_DTYPE), dn, preferred_element_type=F32)


def _mm_tiles(m, n, k, a_bytes, b_bytes, mn_bytes):
    best = None
    for tm in (1024, 512, 384, 256, 128):
        for tn in (1024, 512, 384, 256, 128):
            for tk in (2048, 1024, 512, 384, 256, 128):
                if m % tm or n % tn or k % tk:
                    continue
                vmem = 2 * (tm * tk * a_bytes + tk * tn * b_bytes) + 2 * tm * tn * mn_bytes + 4 * tm * tn
                vmem += 2 * (tm * tk + tk * tn)
                if vmem > MM_VMEM_BUDGET:
                    continue
                steps = (m // tm) * (n // tn) * (k // tk)
                key = (steps, vmem)
                if best is None or key < best[0]:
                    best = (key, (tm, tn, tk))
    assert best is not None, (m, n, k)
    return best[1]


def _mm(a, b, *, ta=False, tb=False, epilogue=None, tiles=(), rowvecs=(), out_dtypes=(F32,), name):
    m, k = (a.shape[1], a.shape[0]) if ta else a.shape
    n = b.shape[0] if tb else b.shape[1]
    assert (b.shape[1] if tb else b.shape[0]) == k, (a.shape, b.shape, ta, tb)
    mn_bytes = sum(t.dtype.itemsize for t in tiles) + sum(jnp.dtype(dt).itemsize for dt in out_dtypes)
    tm, tn, tk = _mm_tiles(m, n, k, a.dtype.itemsize, b.dtype.itemsize, mn_bytes)
    nk = k // tk
    nt, nr, no = len(tiles), len(rowvecs), len(out_dtypes)

    def body(*refs):
        a_ref, b_ref = refs[0], refs[1]
        tile_refs = refs[2:2 + nt]
        row_refs = refs[2 + nt:2 + nt + nr]
        out_refs = refs[2 + nt + nr:2 + nt + nr + no]
        acc_ref = refs[-1]
        kk = pl.program_id(2)

        @pl.when(kk == 0)
        def _():
            acc_ref[...] = jnp.zeros_like(acc_ref)

        acc_ref[...] += _dot(a_ref[...], b_ref[...], ta, tb)

        @pl.when(kk == nk - 1)
        def _():
            acc = acc_ref[...]
            if epilogue is None:
                outs = (acc,)
            else:
                outs = epilogue(acc, *[t[...] for t in tile_refs], *[r[...] for r in row_refs])
            for o_ref, o in zip(out_refs, outs):
                o_ref[...] = o.astype(o_ref.dtype)

    a_spec = pl.BlockSpec((tk, tm), lambda i, j, kk: (kk, i)) if ta else pl.BlockSpec((tm, tk), lambda i, j, kk: (i, kk))
    b_spec = pl.BlockSpec((tn, tk), lambda i, j, kk: (j, kk)) if tb else pl.BlockSpec((tk, tn), lambda i, j, kk: (kk, j))
    mn_spec = pl.BlockSpec((tm, tn), lambda i, j, kk: (i, j))
    row_spec = pl.BlockSpec((1, tn), lambda i, j, kk: (0, j))
    tile_specs = [pl.BlockSpec((tm, tn), lambda i, j, kk, ob=_off(t, tn): (i, j + ob)) for t in tiles]
    outs = pl.pallas_call(
        body, name=name,
        grid=(m // tm, n // tn, nk),
        in_specs=[a_spec, b_spec] + tile_specs + [row_spec] * nr,
        out_specs=[mn_spec] * no,
        out_shape=[jax.ShapeDtypeStruct((m, n), dt) for dt in out_dtypes],
        scratch_shapes=[pltpu.VMEM((tm, tn), F32)],
        compiler_params=_cparams(("parallel", "parallel", "arbitrary")),
    )(a, b, *[_arr(t) for t in tiles], *rowvecs)
    return outs[0] if no == 1 else tuple(outs)


def _branch_merge(ys, ws, u_all, *, name):
    s, d = ys[0].shape[0], ws[0].shape[1]
    tm, tn = _pick(s, (512, 256, 128)), _pick(d, (512, 256, 128))
    nb = len(ys)

    def body(*refs):
        y_refs, w_refs, g_refs = refs[:nb], refs[nb:2 * nb], refs[2 * nb:3 * nb]
        merged_ref, yb_refs = refs[3 * nb], refs[3 * nb + 1:]
        merged = None
        for y_ref, w_ref, g_ref, yb_ref in zip(y_refs, w_refs, g_refs, yb_refs):
            acc = _dot(y_ref[...], w_ref[...])
            yb_ref[...] = acc.astype(yb_ref.dtype)
            term = _sigmoid(g_ref[...]) * acc
            merged = term if merged is None else merged + term
        merged_ref[...] = merged

    mn = pl.BlockSpec((tm, tn), lambda i, j: (i, j))
    in_specs = [pl.BlockSpec((tm, y.shape[1]), lambda i, j: (i, 0)) for y in ys]
    in_specs += [pl.BlockSpec((w.shape[0], tn), lambda i, j: (0, j)) for w in ws]
    in_specs += [pl.BlockSpec((tm, tn), lambda i, j, ob=b * d // tn: (i, j + ob)) for b in range(nb)]
    return pl.pallas_call(
        body, name=name, grid=(s // tm, d // tn), in_specs=in_specs, out_specs=[mn] * (nb + 1),
        out_shape=[jax.ShapeDtypeStruct((s, d), F32)] + [jax.ShapeDtypeStruct((s, d), BF16)] * nb,
        compiler_params=_cparams(("parallel", "parallel")),
    )(*ys, *ws, *[u_all] * nb)


def _rowwise(fn, rows, fulls, outs, *, name, tm=None):
    r = rows[0].shape[0]
    if tm is None:
        widest = max([x.shape[1] for x in rows] + [o[0] for o in outs])
        tm = _pick(r, (max(8, min(512, (512 * 1024) // widest)), 256, 128, 64, 32, 16, 8))
    nrow, nfull, nout = len(rows), len(fulls), len(outs)

    def body(*refs):
        row_refs = refs[:nrow]
        full_refs = refs[nrow:nrow + nfull]
        out_refs = refs[nrow + nfull:]
        res = fn(*[x[...] for x in row_refs], *[x[...] for x in full_refs])
        if not isinstance(res, (tuple, list)):
            res = (res,)
        step = pl.program_id(0)
        for o_ref, o, spec in zip(out_refs, res, outs):
            if spec[2] == "row":
                o_ref[...] = o.astype(o_ref.dtype)
            else:
                @pl.when(step == 0)
                def _(o_ref=o_ref):
                    o_ref[...] = jnp.zeros_like(o_ref)
                o_ref[...] += o

    in_specs = [pl.BlockSpec((tm, x.shape[1]), lambda i, ob=_off(x, x.shape[1]): (i, ob)) for x in rows]
    in_specs += [pl.BlockSpec(x.shape, lambda i, nd=x.ndim: (0,) * nd) for x in fulls]
    out_specs, out_shape = [], []
    for c, dt, kind in outs:
        if kind == "row":
            out_specs.append(pl.BlockSpec((tm, c), lambda i: (i, 0)))
            out_shape.append(jax.ShapeDtypeStruct((r, c), dt))
        else:
            out_specs.append(pl.BlockSpec((1, c), lambda i: (0, 0)))
            out_shape.append(jax.ShapeDtypeStruct((1, c), F32))
    res = pl.pallas_call(
        body, name=name, grid=(r // tm,), in_specs=in_specs, out_specs=out_specs, out_shape=out_shape,
        compiler_params=_cparams(("arbitrary",)),
    )(*[_arr(x) for x in rows], *fulls)
    return res[0] if nout == 1 else tuple(res)


def _colsum(x):
    return jnp.sum(x, axis=0, keepdims=True)


def _rms_parts(x, n_real):
    r = lax.rsqrt(jnp.sum(x * x, axis=-1, keepdims=True) * (1.0 / n_real) + EPS)
    return x * r, r


def _rms_fwd(x, g, *, n_real=None, out_dtype=BF16, name):
    n_real = n_real or x.shape[1]

    def fn(xv, gv):
        xh, _ = _rms_parts(xv, n_real)
        return xh * gv

    return _rowwise(fn, [x], [g], [(x.shape[1], out_dtype, "row")], name=name)


def _rms_bwd_math(xv, gv, dh, n_real):
    xh, r = _rms_parts(xv, n_real)
    dxh = dh * gv
    dx = r * (dxh - xh * (jnp.sum(dxh * xh, axis=-1, keepdims=True) * (1.0 / n_real)))
    return dx, _colsum(dh * xh)


def _rms_bwd(x, g, dh, res=None, *, name):
    n = x.shape[1]
    if res is None:
        def fn(xv, dhv, gv):
            return _rms_bwd_math(xv, gv, dhv.astype(F32), n)
        rows = [x, dh]
    else:
        def fn(xv, dhv, rv, gv):
            dx, dg = _rms_bwd_math(xv, gv, dhv.astype(F32), n)
            return dx + rv, dg
        rows = [x, dh, res]
    return _rowwise(fn, rows, [g], [(n, F32, "row"), (n, F32, "acc")], name=name)


def _seq_call(body, ins, outs, n_blocks, *, name):
    in_specs, args = [], []
    for x, kind in ins:
        in_specs.append(pl.BlockSpec((x.shape[0], LANES), lambda j, ob=_off(x, LANES): (0, j + ob)))
        args.append(_arr(x))
    out_specs, out_shape = [], []
    for shape, dt in outs:
        out_specs.append(pl.BlockSpec((shape[0], LANES), lambda j: (0, j)))
        out_shape.append(jax.ShapeDtypeStruct(shape, dt))
    res = pl.pallas_call(body, name=name, grid=(n_blocks,), in_specs=in_specs, out_specs=out_specs,
                         out_shape=out_shape, compiler_params=_cparams(("parallel",)))(*args)
    return res[0] if len(outs) == 1 else tuple(res)


def _conv_pre(x, w, b, row):
    acc = x * w[CONV_W - 1:CONV_W, :] + b
    for k in range(CONV_W - 1):
        acc = acc + _shift_down(x, CONV_W - 1 - k, row) * w[k:k + 1, :]
    return acc


def _conv_fwd(x, w, b, *, silu, name):
    s, c = x.shape

    def body(x_ref, w_ref, b_ref, y_ref):
        xv = x_ref[...]
        row = lax.broadcasted_iota(jnp.int32, xv.shape, 0)
        pre = _conv_pre(xv, w_ref[...], b_ref[...], row)
        y_ref[...] = _silu(pre) if silu else pre

    return _seq_call(body, [(x, "seq"), (w, "par"), (b, "par")], [((s, c), F32)], c // LANES, name=name)


def _conv_bwd(x, w, b, dy, *, silu, name):
    s, c = x.shape

    def body(x_ref, w_ref, b_ref, dy_ref, dx_ref, dw_ref, db_ref):
        xv, wv, dv = x_ref[...], w_ref[...], dy_ref[...]
        row = lax.broadcasted_iota(jnp.int32, xv.shape, 0)
        if silu:
            dv = dv * _silu_grad(_conv_pre(xv, wv, b_ref[...], row))
        dx = dv * wv[CONV_W - 1:CONV_W, :]
        dws = [None] * CONV_W
        dws[CONV_W - 1] = _colsum(dv * xv)
        for k in range(CONV_W - 1):
            sh = CONV_W - 1 - k
            dx = dx + _shift_up(dv, sh, row) * wv[k:k + 1, :]
            dws[k] = _colsum(dv * _shift_down(xv, sh, row))
        dx_ref[...] = dx
        for k in range(CONV_W):
            dw_ref[k:k + 1, :] = dws[k]
        db_ref[...] = _colsum(dv)

    return _seq_call(body, [(x, "seq"), (w, "par"), (b, "par"), (dy, "seq")],
                     [((s, c), F32), ((CONV_W, c), F32), ((1, c), F32)], c // LANES, name=name)


def _pool_select(levels):
    g = pl.program_id(0)
    return jnp.where(g == 0, levels[0], jnp.where(g == 1, levels[1], jnp.where(g == 2, levels[2], levels[3])))


def _pool_count(row):
    g = pl.program_id(0)
    w = jnp.where(g == 0, POOL_WINDOWS[0], jnp.where(g == 1, POOL_WINDOWS[1],
                                                     jnp.where(g == 2, POOL_WINDOWS[2], POOL_WINDOWS[3])))
    return jnp.minimum(row + 1, w).astype(F32)


def _pool_fwd(u, *, name):
    def body(u_ref, d_ref):
        uv = u_ref[...]
        row = lax.broadcasted_iota(jnp.int32, uv.shape, 0)
        levels, cur, sh = [], uv, 1
        for _ in POOL_WINDOWS:
            cur = cur + _shift_down(cur, sh, row)
            levels.append(cur)
            sh *= 2
        d_ref[...] = _pool_select(levels) / _pool_count(row) - uv

    return _seq_call(body, [(u, "seq")], [(u.shape, F32)], u.shape[1] // LANES, name=name)


def _pool_bwd(dd, *, name):
    def body(dd_ref, du_ref):
        dv = dd_ref[...]
        row = lax.broadcasted_iota(jnp.int32, dv.shape, 0)
        levels, cur, sh = [], dv / _pool_count(row), 1
        for _ in POOL_WINDOWS:
            cur = cur + _shift_up(cur, sh, row)
            levels.append(cur)
            sh *= 2
        du_ref[...] = _pool_select(levels) - dv

    return _seq_call(body, [(dd, "seq")], [(dd.shape, F32)], dd.shape[1] // LANES, name=name)


def _lru_gates(pre_a, pre_i, xc, lam, b_a, b_i):
    r = _sigmoid(pre_a + b_a)
    i = _sigmoid(pre_i + b_i)
    sp = _softplus(-lam)
    log_a = -LRU_C * r * sp
    a = jnp.exp(log_a)
    mult = jnp.sqrt(_neg_expm1(2.0 * log_a))
    return r, i, sp, a, mult


def _lru_fwd(pre, xc, gate_in, lam, b_a, b_i, *, name):
    s, c = xc.shape
    nb = c // LANES

    def body(pa_ref, pi_ref, xc_ref, g_ref, lam_ref, ba_ref, bi_ref, y_ref, h_ref):
        xv = xc_ref[...]
        row = lax.broadcasted_iota(jnp.int32, xv.shape, 0)
        _, i, _, a, mult = _lru_gates(pa_ref[...], pi_ref[...], xv, lam_ref[...], ba_ref[...], bi_ref[...])
        h = xv * i * mult
        d = 1
        while d < s:
            h = h + a * _shift_down(h, d, row)
            a = a * jnp.where(row >= d, pltpu.roll(a, d, 0), 1.0)
            d *= 2
        h_ref[...] = h
        y_ref[...] = h * _gelu(g_ref[...])

    blk = lambda off: pl.BlockSpec((s, LANES), lambda j: (0, j + off))
    par = pl.BlockSpec((1, LANES), lambda j: (0, j))
    return pl.pallas_call(
        body, name=name, grid=(nb,),
        in_specs=[blk(0), blk(nb), blk(0), blk(_off(gate_in, LANES)), par, par, par],
        out_specs=[blk(0), blk(0)],
        out_shape=[jax.ShapeDtypeStruct((s, c), F32)] * 2,
        compiler_params=_cparams(("parallel",)),
    )(pre, pre, xc, _arr(gate_in), lam, b_a, b_i)


def _lru_bwd(pre, xc, gate_in, lam, b_a, b_i, h, dy, *, name):
    s, c = xc.shape
    nb = c // LANES

    def body(pa_ref, pi_ref, xc_ref, g_ref, lam_ref, ba_ref, bi_ref, h_ref, dy_ref,
             dpa_ref, dpi_ref, dxc_ref, dg_ref, dlam_ref, dba_ref, dbi_ref):
        xv, gv, hv, dv = xc_ref[...], g_ref[...], h_ref[...], dy_ref[...]
        row = lax.broadcasted_iota(jnp.int32, xv.shape, 0)
        r, i, sp, a, mult = _lru_gates(pa_ref[...], pi_ref[...], xv, lam_ref[...], ba_ref[...], bi_ref[...])
        dg_ref[...] = dv * hv * _gelu_grad(gv)
        dh = dv * _gelu(gv)
        an = jnp.where(row < s - 1, pltpu.roll(a, s - 1, 0), 0.0)
        d = 1
        while d < s:
            dh = dh + an * _shift_up(dh, d, row)
            an = an * jnp.where(row < s - d, pltpu.roll(an, s - d, 0), 1.0)
            d *= 2
        da = dh * _shift_down(hv, 1, row)
        dxc_ref[...] = dh * i * mult
        di = dh * xv * mult
        dmult = dh * xv * i
        dlog_a = (da - dmult * a / mult) * a
        dr = dlog_a * (-LRU_C) * sp
        dlam_ref[...] = _colsum(dlog_a * LRU_C * r * _sigmoid(-lam_ref[...]))
        dpa = dr * r * (1.0 - r)
        dpi = di * i * (1.0 - i)
        dpa_ref[...] = dpa
        dpi_ref[...] = dpi
        dba_ref[...] = _colsum(dpa)
        dbi_ref[...] = _colsum(dpi)

    blk = lambda off: pl.BlockSpec((s, LANES), lambda j: (0, j + off))
    par = pl.BlockSpec((1, LANES), lambda j: (0, j))
    sc = jax.ShapeDtypeStruct((s, c), F32)
    pc = jax.ShapeDtypeStruct((1, c), F32)
    dpa, dpi, dxc, dg, dlam, dba, dbi = pl.pallas_call(
        body, name=name, grid=(nb,),
        in_specs=[blk(0), blk(nb), blk(0), blk(_off(gate_in, LANES)), par, par, par, blk(0), blk(0)],
        out_specs=[blk(0), blk(0), blk(0), blk(0), par, par, par],
        out_shape=[sc, sc, sc, sc, pc, pc, pc],
        compiler_params=_cparams(("parallel",)),
    )(pre, pre, xc, _arr(gate_in), lam, b_a, b_i, h, dy)
    return dpa, dpi, dxc, dg, dlam, dba, dbi


GROUP_HEADS = 4
SSD_GROUPS = 2


def _ssd_specs(nc, order):
    hw, gw = N_HEADS * LANES, SSD_GROUPS * LANES
    return dict(
        x=pl.BlockSpec((SSD_CHUNK, hw), lambda ci: (order(ci), 0)),
        b=pl.BlockSpec((SSD_CHUNK, gw), lambda ci: (order(ci), hw // gw)),
        c=pl.BlockSpec((SSD_CHUNK, gw), lambda ci: (order(ci), hw // gw + 1)),
        dtcol=pl.BlockSpec((N_HEADS, SSD_CHUNK, 1), lambda ci: (0, order(ci), 0)),
        dtrow=pl.BlockSpec((N_HEADS, 1, SSD_CHUNK), lambda ci: (0, 0, order(ci))),
        scal=pl.BlockSpec((N_HEADS, 1, 1), lambda ci: (0, 0, 0)),
        state=pl.BlockSpec((N_HEADS, 1, LANES, LANES), lambda ci: (0, order(ci), 0, 0)),
        group=pl.BlockSpec((SSD_CHUNK, gw), lambda ci: (order(ci), 0)),
        pacc=pl.BlockSpec((N_HEADS, 1, LANES), lambda ci: (0, 0, 0)),
    )


def _ssd_chunk_terms(dtcol, dtrow, bias, a_log):
    shp = (SSD_CHUNK, SSD_CHUNK)
    row = lax.broadcasted_iota(jnp.int32, shp, 0)
    col = lax.broadcasted_iota(jnp.int32, shp, 1)
    a_head = -jnp.exp(a_log)
    dt_c = jnp.broadcast_to(_softplus(dtcol + bias), shp)
    dt_r = jnp.broadcast_to(_softplus(dtrow + bias), shp)
    cs_c = _cumsum_rows(dt_c * a_head, row)
    cs_r = _cumsum_lanes(dt_r * a_head, col)
    cs_last = jnp.sum(jnp.where(row == SSD_CHUNK - 1, cs_c, 0.0), axis=0, keepdims=True)
    return row, col, a_head, dt_c, cs_c, cs_r, cs_last


def _ssd_fwd(xbc, dtcol, dtrow, bias, a_log, dskip, *, name):
    s = xbc.shape[0]
    nc = s // SSD_CHUNK

    def body(x_ref, b_ref, c_ref, dtc_ref, dtr_ref, bias_ref, alog_ref, d_ref, y_ref, st_ref, state):
        ci = pl.program_id(0)

        @pl.when(ci == 0)
        def _():
            state[...] = jnp.zeros_like(state)

        for gi in range(SSD_GROUPS):
            glanes = slice(gi * LANES, (gi + 1) * LANES)
            bm, cm = b_ref[:, glanes], c_ref[:, glanes]
            cb = _dot(cm, bm, tb=True)
            bm_t = bm.T
            for r in range(gi * GROUP_HEADS, (gi + 1) * GROUP_HEADS):
                lanes = slice(r * LANES, (r + 1) * LANES)
                xv = x_ref[:, lanes]
                row, col, _, dt_c, cs_c, cs_r, cs_last = _ssd_chunk_terms(dtc_ref[r], dtr_ref[r], bias_ref[r], alog_ref[r])
                g = cb * jnp.exp(jnp.where(col <= row, cs_c - cs_r, -jnp.inf))
                xdt = xv * dt_c
                st = state[r]
                st_ref[r, 0] = st
                y_ref[:, lanes] = _dot(g, xdt) + _dot(cm, st) * jnp.exp(cs_c) + xv * d_ref[r]
                state[r] = jnp.exp(cs_last) * st + _dot(bm_t, xdt * jnp.exp(cs_last - cs_c))

    sp = _ssd_specs(nc, lambda ci: ci)
    return pl.pallas_call(
        body, name=name, grid=(nc,),
        in_specs=[sp["x"], sp["b"], sp["c"], sp["dtcol"], sp["dtrow"], sp["scal"], sp["scal"], sp["scal"]],
        out_specs=[sp["x"], sp["state"]],
        out_shape=[jax.ShapeDtypeStruct((s, N_HEADS * LANES), F32),
                   jax.ShapeDtypeStruct((N_HEADS, nc, LANES, LANES), F32)],
        scratch_shapes=[pltpu.VMEM((N_HEADS, LANES, LANES), F32)],
        compiler_params=_cparams(("arbitrary",)),
    )(xbc, xbc, xbc, dtcol, dtrow, bias, a_log, dskip)


def _ssd_bwd(xbc, dtcol, dtrow, bias, a_log, dskip, states, dy, *, name):
    s = xbc.shape[0]
    nc = s // SSD_CHUNK

    def body(x_ref, b_ref, c_ref, dtc_ref, dtr_ref, bias_ref, alog_ref, d_ref, st_ref, dy_ref,
             dx_ref, db_ref, dc_ref, ddt_ref, dbias_ref, dalog_ref, dd_ref, dstate):
        ci = pl.program_id(0)

        @pl.when(ci == 0)
        def _():
            dstate[...] = jnp.zeros_like(dstate)
            dbias_ref[...] = jnp.zeros_like(dbias_ref)
            dalog_ref[...] = jnp.zeros_like(dalog_ref)
            dd_ref[...] = jnp.zeros_like(dd_ref)

        rowsum = lambda v: jnp.sum(v, axis=1, keepdims=True)
        tot = lambda v: jnp.broadcast_to(jnp.sum(v, axis=0, keepdims=True), (1, LANES))
        for gi in range(SSD_GROUPS):
            glanes = slice(gi * LANES, (gi + 1) * LANES)
            bm, cm = b_ref[:, glanes], c_ref[:, glanes]
            cb = _dot(cm, bm, tb=True)
            cb_t = _dot(bm, cm, tb=True)
            cm_t = cm.T
            dbm_sum, dcm_sum = None, None
            for r in range(gi * GROUP_HEADS, (gi + 1) * GROUP_HEADS):
                lanes = slice(r * LANES, (r + 1) * LANES)
                xv, dyv, st = x_ref[:, lanes], dy_ref[:, lanes], st_ref[r, 0]
                dtraw_c, bias = dtc_ref[r], bias_ref[r]
                row, col, a_head, dt_c, cs_c, cs_r, cs_last = _ssd_chunk_terms(dtraw_c, dtr_ref[r], bias, alog_ref[r])
                lmat = jnp.exp(jnp.where(col <= row, cs_c - cs_r, -jnp.inf))
                lmat_t = jnp.exp(jnp.where(row <= col, cs_r - cs_c, -jnp.inf))
                g, g_t = cb * lmat, cb_t * lmat_t
                xdt = xv * dt_c
                e_c = jnp.exp(cs_c)
                f_c = jnp.exp(cs_last - cs_c)
                e_last = jnp.exp(cs_last)
                w = xdt * f_c
                dst = dstate[r]

                dg = _dot(dyv, xdt, tb=True)
                dg_t = _dot(xdt, dyv, tb=True)
                dxdt = _dot(g_t, dyv)
                dcs = rowsum(dg * g) - rowsum(dg_t * g_t)
                dcm = _dot(dg * lmat, bm)
                dbm = _dot(dg_t * lmat_t, cm)
                z = _dot(cm, st)
                dz = dyv * e_c
                dcs = dcs + rowsum(dz * z)
                dcm = dcm + _dot(dz, st, tb=True)
                dstate[r] = _dot(cm_t, dz) + e_last * dst
                dcs_last = jnp.sum(rowsum(dst * st), axis=0, keepdims=True) * jnp.max(e_last, axis=1, keepdims=True)
                dbm = dbm + _dot(w, dst, tb=True)
                dw = _dot(bm, dst)
                dxdt = dxdt + dw * f_c
                q = rowsum(dw * w)
                dcs = dcs - q
                dcs_last = dcs_last + jnp.sum(q, axis=0, keepdims=True)
                dx_ref[:, lanes] = dxdt * dt_c + dyv * d_ref[r]
                ddt = rowsum(dxdt * xv)
                dcs_full = jnp.broadcast_to(dcs, (SSD_CHUNK, SSD_CHUNK)) + jnp.where(row == SSD_CHUNK - 1, dcs_last, 0.0)
                da = jnp.max(_rev_cumsum_rows(dcs_full, row), axis=1, keepdims=True)
                dt_col = jnp.max(dt_c, axis=1, keepdims=True)
                draw = (ddt + da * a_head) * _sigmoid(dtraw_c + bias)
                ddt_ref[r] = draw
                dbias_ref[r] += tot(draw)
                dalog_ref[r] += tot(da * dt_col) * a_head
                dd_ref[r] += tot(rowsum(dyv * xv))
                dbm_sum = dbm if dbm_sum is None else dbm_sum + dbm
                dcm_sum = dcm if dcm_sum is None else dcm_sum + dcm
            db_ref[:, glanes] = dbm_sum
            dc_ref[:, glanes] = dcm_sum

    sp = _ssd_specs(nc, lambda ci: nc - 1 - ci)
    return pl.pallas_call(
        body, name=name, grid=(nc,),
        in_specs=[sp["x"], sp["b"], sp["c"], sp["dtcol"], sp["dtrow"], sp["scal"], sp["scal"], sp["scal"],
                  sp["state"], sp["x"]],
        out_specs=[sp["x"], sp["group"], sp["group"], sp["dtcol"], sp["pacc"], sp["pacc"], sp["pacc"]],
        out_shape=[jax.ShapeDtypeStruct((s, N_HEADS * LANES), F32),
                   jax.ShapeDtypeStruct((s, 2 * LANES), F32),
                   jax.ShapeDtypeStruct((s, 2 * LANES), F32),
                   jax.ShapeDtypeStruct((N_HEADS, s, 1), F32),
                   jax.ShapeDtypeStruct((N_HEADS, 1, LANES), F32),
                   jax.ShapeDtypeStruct((N_HEADS, 1, LANES), F32),
                   jax.ShapeDtypeStruct((N_HEADS, 1, LANES), F32)],
        scratch_shapes=[pltpu.VMEM((N_HEADS, LANES, LANES), F32)],
        compiler_params=_cparams(("arbitrary",)),
    )(xbc, xbc, xbc, dtcol, dtrow, bias, a_log, dskip, states, dy)


def _att_tile(s):
    return _pick(s, (512, 256, 128))


def _tri(t, transposed=False):
    r = lax.broadcasted_iota(jnp.int32, (t, t), 0)
    c = lax.broadcasted_iota(jnp.int32, (t, t), 1)
    return (r <= c) if transposed else (c <= r)


def _rows_at(ref, blk, t):
    return ref[pl.ds(pl.multiple_of(blk * t, t), t), :]


def _flash_fwd(q, k, v, *, name):
    s = q.shape[0]
    t = _att_tile(s)
    nq = s // t

    def body(q_ref, k_ref, v_ref, o_ref, lse_ref):
        i = pl.program_id(1)
        qv = q_ref[...]

        def step(j, carry, diagonal):
            m_old, l_old, acc = carry
            sc = _dot(qv, _rows_at(k_ref, j, t), tb=True)
            if diagonal:
                sc = jnp.where(_tri(t), sc, -jnp.inf)
            m_new = jnp.maximum(m_old, jnp.max(sc, axis=1, keepdims=True))
            alpha = jnp.exp(m_old - m_new)
            p = jnp.exp(sc - m_new)
            return (m_new, alpha * l_old + jnp.sum(p, axis=1, keepdims=True),
                    alpha * acc + _dot(p, _rows_at(v_ref, j, t)))

        init = (jnp.full((t, 1), -jnp.inf, F32), jnp.zeros((t, 1), F32), jnp.zeros((t, LANES), F32))
        carry = lax.fori_loop(0, i, lambda j, c: step(j, c, False), init)
        m_fin, l_fin, acc = step(i, carry, True)
        o_ref[...] = (acc / l_fin).astype(o_ref.dtype)
        lse_ref[0] = m_fin + jnp.log(l_fin)

    q_spec = pl.BlockSpec((t, LANES), lambda h, i: (i, h))
    kv_spec = pl.BlockSpec((s, LANES), lambda h, i: (0, h))
    return pl.pallas_call(
        body, name=name, grid=(N_HEADS, nq),
        in_specs=[q_spec, kv_spec, kv_spec],
        out_specs=[q_spec, pl.BlockSpec((1, t, 1), lambda h, i: (h, i, 0))],
        out_shape=[jax.ShapeDtypeStruct(q.shape, BF16), jax.ShapeDtypeStruct((N_HEADS, s, 1), F32)],
        compiler_params=_cparams(("parallel", "arbitrary")),
    )(q, k, v)


def _att_delta(o, do, *, name):
    s = o.shape[0]
    t = _att_tile(s)

    def body(o_ref, do_ref, dl_ref):
        dl_ref[0] = jnp.sum(do_ref[...].astype(F32) * o_ref[...].astype(F32), axis=1, keepdims=True)

    blk = pl.BlockSpec((t, LANES), lambda h, i: (i, h))
    return pl.pallas_call(
        body, name=name, grid=(N_HEADS, s // t), in_specs=[blk, blk],
        out_specs=pl.BlockSpec((1, t, 1), lambda h, i: (h, i, 0)),
        out_shape=jax.ShapeDtypeStruct((N_HEADS, s, 1), F32),
        compiler_params=_cparams(("parallel", "parallel")),
    )(o, do)


def _flash_bwd(q, k, v, do, lse_row, delta_row, *, name):
    s = q.shape[0]
    t = _att_tile(s)
    nq = s // t

    def body(q_ref, k_ref, v_ref, do_ref, lse_ref, dl_ref, dq_ref, dk_ref, dv_ref):
        j = pl.program_id(1)
        kv, vv = k_ref[...], v_ref[...]

        @pl.when(j == 0)
        def _():
            dq_ref[...] = jnp.zeros_like(dq_ref)

        def step(i, carry, diagonal):
            dk, dv = carry
            rows = pl.ds(pl.multiple_of(i * t, t), t)
            qi, doi = q_ref[rows, :], do_ref[rows, :]
            p_t = jnp.exp(_dot(kv, qi, tb=True) - lse_ref[0, :, rows])
            if diagonal:
                p_t = jnp.where(_tri(t, transposed=True), p_t, 0.0)
            ds_t = (p_t * (_dot(vv, doi, tb=True) - dl_ref[0, :, rows])).astype(MXU_DTYPE)
            dq_ref[rows, :] += _dot(ds_t, kv, ta=True)
            return dk + _dot(ds_t, qi), dv + _dot(p_t, doi)

        zero = jnp.zeros((t, LANES), F32)
        carry = step(j, (zero, zero), True)
        dk, dv = lax.fori_loop(j + 1, nq, lambda i, c: step(i, c, False), carry)
        dk_ref[...] = dk
        dv_ref[...] = dv

        @pl.when(j == nq - 1)
        def _():
            dq_ref[...] = dq_ref[...] * ATT_SCALE

    q_spec = pl.BlockSpec((s, LANES), lambda h, j: (0, h))
    kv_spec = pl.BlockSpec((t, LANES), lambda h, j: (j, h))
    row_spec = pl.BlockSpec((1, 1, s), lambda h, j: (h, 0, 0))
    return pl.pallas_call(
        body, name=name, grid=(N_HEADS, nq),
        in_specs=[q_spec, kv_spec, kv_spec, q_spec, row_spec, row_spec],
        out_specs=[q_spec, kv_spec, kv_spec],
        out_shape=[jax.ShapeDtypeStruct(q.shape, F32)] * 3,
        compiler_params=_cparams(("parallel", "arbitrary")),
    )(q, k, v, do, lse_row, delta_row)


def _rope(v, cos_t, sin_p, sin_m):
    return v * cos_t + pltpu.roll(v, QK_ROPE // 2, 1) * sin_p + pltpu.roll(v, LANES - QK_ROPE // 2, 1) * sin_m


def _rope_t(d, cos_t, sin_p, sin_m):
    return d * cos_t + pltpu.roll(d * sin_p, LANES - QK_ROPE // 2, 1) + pltpu.roll(d * sin_m, QK_ROPE // 2, 1)


def _att_prep(q_pad, kv2, kr, cos_t, sin_p, sin_m, *, name):
    w = N_HEADS * LANES

    def fn(qv, kvv, krv, c, sp, sm):
        kr_rot = _rope(krv, c, sp, sm)
        qs, ks = [], []
        for h in range(N_HEADS):
            blk = slice(h * LANES, (h + 1) * LANES)
            qs.append(_rope(qv[:, blk], c, sp, sm) * ATT_SCALE)
            ks.append(kvv[:, blk] + kr_rot)
        return jnp.concatenate(qs, axis=1), jnp.concatenate(ks, axis=1), kvv[:, w:]

    return _rowwise(fn, [q_pad, kv2, kr, cos_t, sin_p, sin_m], [],
                    [(w, BF16, "row"), (w, BF16, "row"), (w, BF16, "row")], name=name)


def _att_prep_bwd(dq, dk, cos_t, sin_p, sin_m, *, name):
    w = N_HEADS * LANES

    def fn(dqv, dkv, c, sp, sm):
        outs, dkr = [], None
        for h in range(N_HEADS):
            blk = slice(h * LANES, (h + 1) * LANES)
            outs.append(_rope_t(dqv[:, blk], c, sp, sm))
            dkr = dkv[:, blk] if dkr is None else dkr + dkv[:, blk]
        return jnp.concatenate(outs, axis=1), _rope_t(dkr, c, sp, sm)

    return _rowwise(fn, [dq, dk, cos_t, sin_p, sin_m], [], [(w, BF16, "row"), (LANES, F32, "row")], name=name)


_ANY = pl.BlockSpec(memory_space=pl.ANY)
_MESH = pl.DeviceIdType.MESH


def _mesh_pos():
    return lax.axis_index("x"), lax.axis_index("y"), lax.axis_index("c")


def _remote(src, dst, send_sem, recv_sem, dev):
    return pltpu.make_async_remote_copy(src_ref=src, dst_ref=dst, send_sem=send_sem, recv_sem=recv_sem,
                                        device_id=dev, device_id_type=_MESH)


def _other_chips(x, y):
    chips = [(1 - x, y), (x, 1 - y), (1 - x, 1 - y)]
    return chips, [2 * cx + cy for cx, cy in chips]


def _comm_call(body, ins, out_shapes, n_sems, *, name):
    return pl.pallas_call(
        body, name=name, in_specs=[_ANY] * len(ins), out_specs=[_ANY] * len(out_shapes), out_shape=out_shapes,
        scratch_shapes=[pltpu.SemaphoreType.DMA((k,)) for k in n_sems],
    )(*ins)


def _gather_halves(shards):
    n = len(shards)
    halves = [t.shape[0] // 2 for t in shards]

    def body(*refs):
        xs, outs = refs[:n], refs[n:2 * n]
        send_sems, recv_sems = refs[2 * n:]
        x, y, c = _mesh_pos()
        k = 2 * x + y
        sibling = (x, y, 1 - c)
        chips, ks = _other_chips(x, y)
        half = lambda w, hf: pl.ds(hf * halves[w], halves[w])
        first = [_remote(xs[w].at[half(w, c)], outs[w].at[k, half(w, c)], send_sems.at[6 * w + j], recv_sems.at[6 * w + j],
                         (*chips[j], c)) for w in range(n) for j in range(3)]
        for cp in first:
            cp.start()
        passed = []
        for j in range(3):
            for w in range(n):
                land = outs[w].at[ks[j], half(w, c)]
                _remote(land, land, send_sems.at[6 * w + j], recv_sems.at[6 * w + j], sibling).wait_recv()
                passed.append(_remote(land, land, send_sems.at[6 * w + 3 + j], recv_sems.at[6 * w + 3 + j], sibling))
                passed[-1].start()
        for j in range(3):
            for w in range(n):
                land = outs[w].at[ks[j], half(w, 1 - c)]
                _remote(land, land, send_sems.at[6 * w + 3 + j], recv_sems.at[6 * w + 3 + j], sibling).wait_recv()
        for cp in first + passed:
            cp.wait_send()

    shapes = [jax.ShapeDtypeStruct((4,) + t.shape, t.dtype) for t in shards]
    return _comm_call(body, shards, shapes, (6 * n, 6 * n), name="gather_halves")


_HBM = pl.BlockSpec(memory_space=pltpu.HBM)
_SEM = pl.BlockSpec(memory_space=pltpu.SEMAPHORE)
_EFFECT = pltpu.SideEffectType.DATAFLOW_SIDE_EFFECTING


def _push_start(blocks, *, scatter, name):
    n = len(blocks)

    def body(*refs):
        xs, lands = refs[:n], refs[n:2 * n]
        send_sems, recv_sems = refs[2 * n], refs[2 * n + 1]
        token = refs[-1]
        x, y, c = _mesh_pos()
        k = 2 * x + y
        chips, ks = _other_chips(x, y)
        for w in range(n):
            for j in range(3):
                src = xs[w].at[ks[j]] if scatter else xs[w]
                _remote(src, lands[w].at[k], send_sems.at[3 * w + j], recv_sems.at[3 * w + j], (*chips[j], c)).start()
        token[...] = jnp.zeros_like(token)

    hbm = lambda shape, dtype: pltpu.with_memory_space_constraint(lax.empty(shape, dtype), pltpu.HBM)
    ins = [pltpu.with_memory_space_constraint(t, pltpu.HBM) for t in blocks]
    ins += [hbm(t.shape if scatter else (4,) + t.shape, t.dtype) for t in blocks]
    out_shape = [pltpu.SemaphoreType.DMA((3 * n,)), pltpu.SemaphoreType.DMA((3 * n,))]
    out_shape += [pltpu.HBM(t.shape, t.dtype) for t in ins]
    out_shape += [jax.ShapeDtypeStruct((8, LANES), F32)]
    res = pl.pallas_call(
        body, name=name, out_shape=out_shape, in_specs=[_HBM] * (2 * n),
        out_specs=[_SEM, _SEM] + [_HBM] * (2 * n) + [pl.BlockSpec(memory_space=pltpu.VMEM)],
        input_output_aliases={i: 2 + i for i in range(2 * n)},
        compiler_params=pltpu.CompilerParams(has_side_effects=_EFFECT),
    )(*ins)
    return res[0], res[1], res[2:2 + n], res[2 + n:2 + 2 * n], res[-1]


def _push_wait(send_sems, recv_sems, blocks, lands, after, *, name):
    n = len(blocks)

    def body(*refs):
        lands_in = refs[n:2 * n]
        send_sems, recv_sems = refs[2 * n], refs[2 * n + 1]
        x, y, c = _mesh_pos()
        chips, ks = _other_chips(x, y)
        for w in range(n):
            for j in range(3):
                slot = lands_in[w].at[ks[j]]
                cp = _remote(slot, slot, send_sems.at[3 * w + j], recv_sems.at[3 * w + j], (*chips[j], c))
                cp.wait_send()
                cp.wait_recv()

    out_shape = [pltpu.HBM(t.shape, t.dtype) for t in list(blocks) + list(lands)]
    res = pl.pallas_call(
        body, name=name, out_shape=out_shape,
        in_specs=[_HBM] * (2 * n) + [_SEM, _SEM, pl.BlockSpec(memory_space=pl.ANY)], out_specs=[_HBM] * (2 * n),
        input_output_aliases={i: i for i in range(2 * n)},
        compiler_params=pltpu.CompilerParams(has_side_effects=_EFFECT),
    )(*blocks, *lands, send_sems, recv_sems, after)
    return res[:n], res[n:]


def _send_half(views, *, name):
    n = len(views)

    def body(*refs):
        vs, outs = refs[:n], refs[n:2 * n]
        send_sems, recv_sems = refs[2 * n:]
        x, y, c = _mesh_pos()
        cps = []
        for w in range(n):
            h = views[w].shape[1] // 2
            cps.append(_remote(vs[w].at[:, pl.ds((1 - c) * h, h), :], outs[w], send_sems.at[w], recv_sems.at[w],
                               (x, y, 1 - c)))
            cps[-1].start()
        for cp in cps:
            cp.wait()

    shapes = [jax.ShapeDtypeStruct((t.shape[0], t.shape[1] // 2, t.shape[2]), t.dtype) for t in views]
    return _comm_call(body, views, shapes, (n, n), name=name)


def _swap_with_sibling(mine, *, name):
    n = len(mine)

    def body(*refs):
        hs, outs = refs[:n], refs[n:2 * n]
        send_sems, recv_sems = refs[2 * n:]
        x, y, c = _mesh_pos()
        cps = [_remote(hs[w], outs[w], send_sems.at[w], recv_sems.at[w], (x, y, 1 - c)) for w in range(n)]
        for cp in cps:
            cp.start()
        for cp in cps:
            cp.wait()

    shapes = [jax.ShapeDtypeStruct(t.shape, t.dtype) for t in mine]
    return _comm_call(body, mine, shapes, (n, n), name=name)


def _gather_all(vec, *, name):
    r, w = vec.shape

    def body(v_ref, out_ref, send_sems, recv_sems):
        x, y, c = _mesh_pos()

        def slot(px, py, pc):
            return out_ref.at[4 * px + 2 * py + pc]

        peers = []
        for rel in range(1, 8):
            fx, fy, fc = (rel >> 2) & 1, (rel >> 1) & 1, rel & 1
            peers.append((x ^ fx, y ^ fy, c ^ fc))
        cps = [_remote(v_ref, slot(x, y, c), send_sems.at[j], recv_sems.at[j], peer) for j, peer in enumerate(peers)]
        for cp in cps:
            cp.start()
        for j, peer in enumerate(peers):
            _remote(slot(*peer), slot(*peer), send_sems.at[j], recv_sems.at[j], peer).wait_recv()
        for cp in cps:
            cp.wait_send()

    others = pl.pallas_call(
        body, name=name, in_specs=[_ANY], out_specs=_ANY,
        out_shape=jax.ShapeDtypeStruct((8, r, w), vec.dtype),
        scratch_shapes=[pltpu.SemaphoreType.DMA((7,)), pltpu.SemaphoreType.DMA((7,))],
    )(vec)
    me = 4 * lax.axis_index("x") + 2 * lax.axis_index("y") + lax.axis_index("c")
    return lax.dynamic_update_index_in_dim(others, vec, me, 0)


def _gather_same_core(vec, *, name):
    r, w = vec.shape

    def body(v_ref, out_ref, send_sems, recv_sems):
        x, y, c = _mesh_pos()
        k = 2 * x + y
        chips, ks = _other_chips(x, y)
        cps = [_remote(v_ref, out_ref.at[k], send_sems.at[j], recv_sems.at[j], (*chips[j], c)) for j in range(3)]
        for cp in cps:
            cp.start()
        for j in range(3):
            slot = out_ref.at[ks[j]]
            _remote(slot, slot, send_sems.at[j], recv_sems.at[j], (*chips[j], c)).wait_recv()
        for cp in cps:
            cp.wait_send()

    others = pl.pallas_call(
        body, name=name, in_specs=[_ANY], out_specs=_ANY,
        out_shape=jax.ShapeDtypeStruct((4, r, w), vec.dtype),
        scratch_shapes=[pltpu.SemaphoreType.DMA((3,)), pltpu.SemaphoreType.DMA((3,))],
    )(vec)
    k_chip = 2 * lax.axis_index("x") + lax.axis_index("y")
    return lax.dynamic_update_index_in_dim(others, vec, k_chip, 0)


def _row_tile(rows, row_bytes):
    for tm in (1024, 512, 256, 128, 64, 32, 16):
        if rows % tm == 0 and tm * row_bytes <= ELEMENTWISE_BLOCK_BYTES:
            return tm
    return 16 if rows % 16 == 0 else rows


def _chip_sum_half(g, got, c, *, name):
    nb, r, w = g.shape
    half = r // 2
    tm = _row_tile(half, w * 4)
    per = half // tm

    def body(c_ref, g_ref, o_ref, out_ref):
        out_ref[...] = (g_ref[...] + o_ref[...]).astype(out_ref.dtype)

    return pl.pallas_call(
        body, name=name,
        grid_spec=pltpu.PrefetchScalarGridSpec(
            num_scalar_prefetch=1, grid=(nb, per),
            in_specs=[pl.BlockSpec((1, tm, w), lambda b, i, c_ref: (b, c_ref[0] * per + i, 0)),
                      pl.BlockSpec((1, tm, w), lambda b, i, c_ref: (b, i, 0))],
            out_specs=pl.BlockSpec((1, tm, w), lambda b, i, c_ref: (b, i, 0))),
        out_shape=jax.ShapeDtypeStruct((nb, half, w), BF16),
        compiler_params=_cparams(("parallel", "parallel")),
    )(jnp.reshape(c, (1,)).astype(jnp.int32), g, got)


def _sum_slots(stack, *, name):
    n, r, w = stack.shape
    tm = _row_tile(r, n * w * stack.dtype.itemsize)

    def body(s_ref, out_ref):
        acc = s_ref[0].astype(F32)
        for i in range(1, n):
            acc = acc + s_ref[i].astype(F32)
        out_ref[...] = acc

    return pl.pallas_call(
        body, name=name, grid=(r // tm,),
        in_specs=[pl.BlockSpec((n, tm, w), lambda i: (0, i, 0))],
        out_specs=pl.BlockSpec((tm, w), lambda i: (i, 0)),
        out_shape=jax.ShapeDtypeStruct((r, w), F32),
        compiler_params=_cparams(("parallel",)),
    )(stack)


def _adam_math(wv, gv, mv, vv):
    m_new = ADAM_B1 * mv + (1.0 - ADAM_B1) * gv
    v_new = ADAM_B2 * vv + (1.0 - ADAM_B2) * (gv * gv)
    m_hat = m_new / (1.0 - ADAM_B1 ** ADAM_STEP)
    v_hat = v_new / (1.0 - ADAM_B2 ** ADAM_STEP)
    delta = -ADAM_LR * (m_hat / (jnp.sqrt(v_hat) + ADAM_EPS) + ADAM_WD * wv)
    return delta, m_new, v_new


def _adamw(w, g, m, v, *, name):
    shape = w.shape
    cols = shape[-1]
    flat = lambda t: t.reshape(-1, cols)
    rows = flat(w).shape[0]
    tm = _pick(rows, (256, 128, 64, 32, 16, 8))
    outs = _rowwise(_adam_math, [flat(w), flat(g), flat(m), flat(v)], [], [(cols, F32, "row")] * 3, name=name, tm=tm)
    return tuple(o.reshape(shape) for o in outs)


def _adamw_slots(w, slots, m, v, c, *, name):
    shape = w.shape
    cols = shape[-1]
    half = slots[0][0].shape[1]
    v4 = lambda t: t.reshape(2, 2, half, cols)
    assert all(s.shape == (4, half, cols) for pair in slots for s in pair) and w.size == 4 * half * cols
    tm = _row_tile(half, cols * 4 * 4)

    def body(c_ref, w_ref, m0_ref, o0_ref, m1_ref, o1_ref, m_ref, v_ref, g_ref, d_ref, mo_ref, vo_ref):
        first = pl.program_id(0) == 0
        own = pl.program_id(1) == c_ref[0]
        g = None
        for i in range(4):
            part = jnp.where(first, jnp.where(own, m0_ref[i], o0_ref[i]), jnp.where(own, m1_ref[i], o1_ref[i]))
            g = part.astype(F32) if g is None else g + part.astype(F32)
        delta, m_new, v_new = _adam_math(w_ref[0, 0], g, m_ref[0, 0], v_ref[0, 0])
        g_ref[0, 0], d_ref[0, 0], mo_ref[0, 0], vo_ref[0, 0] = g, delta, m_new, v_new

    blk = pl.BlockSpec((1, 1, tm, cols), lambda l, hf, i, c_ref: (l, hf, i, 0))

    def slot_spec(layer, mine):
        def index(l, hf, i, c_ref):
            same_half = hf * c_ref[0] + (1 - hf) * (1 - c_ref[0])
            use = (l if layer else 1 - l) * (same_half if mine else 1 - same_half)
            return (0, i * use, 0)
        return pl.BlockSpec((4, tm, cols), index)

    outs = pl.pallas_call(
        body, name=name,
        grid_spec=pltpu.PrefetchScalarGridSpec(
            num_scalar_prefetch=1, grid=(2, 2, half // tm),
            in_specs=[blk, slot_spec(0, True), slot_spec(0, False), slot_spec(1, True), slot_spec(1, False), blk, blk],
            out_specs=[blk] * 4),
        out_shape=[jax.ShapeDtypeStruct((2, 2, half, cols), F32)] * 4,
        compiler_params=_cparams(("arbitrary", "arbitrary", "arbitrary")),
    )(jnp.reshape(c, (1,)).astype(jnp.int32), v4(w), slots[0][0], slots[0][1], slots[1][0], slots[1][1], v4(m), v4(v))
    return tuple(o.reshape(shape) for o in outs)


def _pad_blocks(w, axis, n_blocks, real, to=LANES, offset=0):
    axis = axis % w.ndim
    shp = w.shape
    w = w.reshape(shp[:axis] + (n_blocks, real) + shp[axis + 1:])
    pads = [(0, 0)] * w.ndim
    pads[axis + 1] = (offset, to - real - offset)
    w = jnp.pad(w, pads)
    return w.reshape(shp[:axis] + (n_blocks * to,) + shp[axis + 1:])


def _unpad_blocks(w, axis, n_blocks, real, to=LANES, offset=0):
    axis = axis % w.ndim
    shp = w.shape
    w = w.reshape(shp[:axis] + (n_blocks, to) + shp[axis + 1:])
    w = lax.slice_in_dim(w, offset, offset + real, axis=axis + 1)
    return w.reshape(shp[:axis] + (n_blocks * real,) + shp[axis + 1:])


def _block_diag(w):
    n, a, b = w.shape
    eye = jnp.eye(n, dtype=w.dtype)
    return (eye[:, None, :, None] * w[:, :, None, :]).reshape(n * a, n * b)


def _block_diag_t(d, n):
    a, b = d.shape[0] // n, d.shape[1] // n
    d = d.reshape(n, a, n, b)
    return jnp.stack([d[i, :, i, :] for i in range(n)])


_SPLITS = np.cumsum((0,) + SPLIT_SIZES)


def _w_in_groups(w_in):
    sl = lambda i: w_in[:, _SPLITS[i]:_SPLITS[i + 1]]
    xbc = sl(5)
    xbc_pad = jnp.concatenate([_pad_blocks(xbc[:, :MIX], 1, N_HEADS, HEAD),
                               _pad_blocks(xbc[:, MIX:MIX + 2 * HEAD], 1, 2, HEAD),
                               _pad_blocks(xbc[:, MIX + 2 * HEAD:], 1, 2, HEAD)], axis=1)
    return dict(
        cq=sl(0), ckv=sl(1), kr=_pad_blocks(sl(2), 1, 1, QK_ROPE, offset=HEAD), pool=sl(3),
        z=_pad_blocks(sl(4), 1, N_HEADS, HEAD), xbc=xbc_pad, dt=_pad_blocks(sl(6), 1, 1, N_HEADS),
        lru_g=sl(7), lru_x=sl(8), gates=sl(9))


def _w_in_fused(groups):
    parts, at = [], 0
    for name, off, width in IN_LAYOUT:
        assert groups[name].shape[1] == width and off >= at
        if off > at:
            parts.append(jnp.zeros((groups[name].shape[0], off - at), groups[name].dtype))
        parts.append(groups[name])
        at = off + width
    parts.append(jnp.zeros((parts[0].shape[0], IN_ALL_COLS - at), parts[0].dtype))
    return jnp.concatenate(parts, axis=1)


def _in_cols(arr, name):
    off, width = IN_OFFSETS[name]
    return _Cols(arr, off, width)


def _w_in_ungroup(d):
    xbc = d["xbc"]
    w = N_HEADS * LANES
    xbc_real = jnp.concatenate([_unpad_blocks(xbc[:, :w], 1, N_HEADS, HEAD),
                                _unpad_blocks(xbc[:, w:w + 2 * LANES], 1, 2, HEAD),
                                _unpad_blocks(xbc[:, w + 2 * LANES:], 1, 2, HEAD)], axis=1)
    return jnp.concatenate([d["cq"], d["ckv"], _unpad_blocks(d["kr"], 1, 1, QK_ROPE, offset=HEAD), d["pool"],
                            _unpad_blocks(d["z"], 1, N_HEADS, HEAD), xbc_real, _unpad_blocks(d["dt"], 1, 1, N_HEADS),
                            d["lru_g"], d["lru_x"], d["gates"]], axis=1)


def _pad_xbc_vec(v):
    return jnp.concatenate([_pad_blocks(v[..., :MIX], -1, N_HEADS, HEAD),
                            _pad_blocks(v[..., MIX:MIX + 2 * HEAD], -1, 2, HEAD),
                            _pad_blocks(v[..., MIX + 2 * HEAD:], -1, 2, HEAD)], axis=-1)


def _unpad_xbc_vec(v):
    w = N_HEADS * LANES
    return jnp.concatenate([_unpad_blocks(v[..., :w], -1, N_HEADS, HEAD),
                            _unpad_blocks(v[..., w:w + 2 * LANES], -1, 2, HEAD),
                            _unpad_blocks(v[..., w + 2 * LANES:], -1, 2, HEAD)], axis=-1)


def _layer_weights(p):
    q = dict(p)
    q["in_all"] = _w_in_fused(_w_in_groups(p["w_in"]))
    q["uq"] = _pad_blocks(p["w_uq"], 1, N_HEADS, HEAD + QK_ROPE)
    ukv = p["w_ukv"].reshape(KV_LORA, N_HEADS, 2 * HEAD)
    q["ukv"] = jnp.concatenate([_pad_blocks(ukv[:, :, :HEAD].reshape(KV_LORA, -1), 1, N_HEADS, HEAD),
                                _pad_blocks(ukv[:, :, HEAD:].reshape(KV_LORA, -1), 1, N_HEADS, HEAD)], axis=1)
    q["pool_bd"] = _block_diag(p["w_pool"])
    q["lru_bd"] = jnp.concatenate([_block_diag(p["lru_w_a"]), _block_diag(p["lru_w_i"])], axis=1)
    q["br"] = [_pad_blocks(p["w_branch"][0], 0, N_HEADS, HEAD), p["w_branch"][1],
               _pad_blocks(p["w_branch"][2], 0, N_HEADS, HEAD), p["w_branch"][3]]
    q["ssd_conv_w_pad"] = _pad_xbc_vec(p["ssd_conv_w"])
    q["ssd_conv_b_pad"] = _pad_xbc_vec(p["ssd_conv_b"])[None, :]
    q["ssd_norm_pad"] = _pad_blocks(p["ssd_norm"], 0, N_HEADS, HEAD)[None, :]
    return q


def _row(v):
    return v.reshape(1, -1)


def _scal3(v):
    return v.reshape(N_HEADS, 1, 1)


def _layer_fwd(x, p_emb, w, rope, tag):
    n = lambda s: f"{s}_{tag}"
    sv = {"x": x}
    h = _rms_fwd(x, _row(w["g_mix"]), name=n("rms_mix"))
    sv["h"] = h
    u_all = _mm(h, w["in_all"], name=n("in_proj"))
    u = {k: _in_cols(u_all, k) for k in IN_OFFSETS}
    sv["u"] = u

    cqn = _rms_fwd(u["cq"], _row(w["q_norm"]), name=n("rms_q"))
    ckvn = _rms_fwd(u["ckv"], _row(w["kv_norm"]), name=n("rms_kv"))
    q_pad = _mm(cqn, w["uq"], name=n("uq"))
    kv2 = _mm(ckvn, w["ukv"], name=n("ukv"))
    qc, kc, vc = _att_prep(q_pad, kv2, u["kr"], *rope, name=n("att_prep"))
    y_a, lse = _flash_fwd(qc, kc, vc, name=n("flash_fwd"))
    sv.update(cqn=cqn, ckvn=ckvn, qc=qc, kc=kc, vc=vc, y_a=y_a, lse=lse)

    pool_d = _pool_fwd(u["pool"], name=n("pool_fwd"))
    yb_pre, y_b = _mm(pool_d, w["pool_bd"], epilogue=lambda acc, sc: (acc, acc * sc),
                      rowvecs=[_row(w["pool_scale"])], out_dtypes=(F32, BF16), name=n("pool_mm"))
    sv.update(pool_d=pool_d, yb_pre=yb_pre, y_b=y_b)

    xbc_c = _conv_fwd(u["xbc"], w["ssd_conv_w_pad"], w["ssd_conv_b_pad"], silu=True, name=n("ssd_conv"))
    dt8 = lax.slice_in_dim(u_all, IN_OFFSETS["dt"][0], IN_OFFSETS["dt"][0] + N_HEADS, axis=1)
    dtcol = dt8.T[:, :, None]
    dtrow = dt8.T[:, None, :]
    ssd_par = (_scal3(w["ssd_dt_bias"]), _scal3(w["ssd_a_log"]), _scal3(w["ssd_d"]))
    y_ssd, states = _ssd_fwd(xbc_c, dtcol, dtrow, *ssd_par, name=n("ssd_fwd"))

    def ssd_post(yv, zv, gv):
        xh, _ = _rms_parts(yv * _silu(zv), MIX)
        return xh * gv

    y_c = _rowwise(ssd_post, [y_ssd, u["z"]], [w["ssd_norm_pad"]], [(N_HEADS * LANES, BF16, "row")], name=n("ssd_post"))
    sv.update(xbc_c=xbc_c, dtcol=dtcol, dtrow=dtrow, y_ssd=y_ssd, states=states, y_c=y_c)

    xc = _conv_fwd(u["lru_x"], w["lru_conv_w"], _row(w["lru_conv_b"]), silu=False, name=n("lru_conv"))
    pre = _mm(xc, w["lru_bd"], name=n("lru_mm"))
    lru_par = (_row(w["lru_lambda"]), _row(w["lru_b_a"]), _row(w["lru_b_i"]))
    y_d, h_lru = _lru_fwd(pre, xc, u["lru_g"], *lru_par, name=n("lru_fwd"))
    sv.update(xc=xc, pre=pre, h_lru=h_lru, y_d=y_d)

    merged, *ybs = _branch_merge([y_a, y_b, y_c, y_d], w["br"], u_all, name=n("branch_merge"))
    x1 = _mm(merged, w["w_out"], epilogue=lambda acc, xr: (acc + xr,), tiles=[x], name=n("out_proj"))
    sv.update(ybs=ybs, merged=merged, x1=x1)

    h2 = _rms_fwd(x1, _row(w["g_mlp"]), name=n("rms_mlp"))
    a_ff, f_ff = _mm(h2, w["w_ff1"], epilogue=lambda acc: (acc, jnp.square(jnp.maximum(acc, 0.0))),
                     out_dtypes=(BF16, BF16), name=n("ff1"))
    x2 = _mm(f_ff, w["w_ff2"], epilogue=lambda acc, xr: (acc + xr,), tiles=[x1], name=n("ff2"))
    sv.update(h2=h2, a_ff=a_ff, f_ff=f_ff, x2=x2)

    h3 = _rms_fwd(x2, _row(w["g_ple"]), name=n("rms_ple"))
    e_ple = _mm(p_emb, w["w_ple"], name=n("ple_emb"))
    x3, gt_ple = _mm(h3, w["w_ple_gate"], epilogue=lambda acc, ev, xr: (xr + ev * _sigmoid(acc), _sigmoid(acc)),
                     tiles=[e_ple, x2], out_dtypes=(F32, F32), name=n("ple_gate"))
    sv.update(h3=h3, e_ple=e_ple, gt_ple=gt_ple, p_emb=p_emb)
    return x3, sv


def _layer_bwd(dx3, sv, w, rope, tag):
    n = lambda s: f"{s}_{tag}"
    gr = {}
    u = sv["u"]

    de, dpre = _rowwise(lambda d, gt, ev: (d * gt, d * ev * gt * (1.0 - gt)), [dx3, sv["gt_ple"], sv["e_ple"]], [],
                        [(D_MODEL, BF16, "row"), (D_MODEL, BF16, "row")], name=n("ple_bwd"))
    gr["w_ple"] = _mm(sv["p_emb"], de, ta=True, name=n("d_w_ple"))
    gr["w_ple_gate"] = _mm(sv["h3"], dpre, ta=True, name=n("d_w_ple_gate"))
    dh3 = _mm(dpre, w["w_ple_gate"], tb=True, out_dtypes=(BF16,), name=n("d_h3"))
    dx2, dg = _rms_bwd(sv["x2"], _row(w["g_ple"]), dh3, dx3, name=n("rms_ple_bwd"))
    gr["g_ple"] = dg[0]

    gr["w_ff2"] = _mm(sv["f_ff"], dx2, ta=True, name=n("d_w_ff2"))
    da = _mm(dx2, w["w_ff2"], tb=True, epilogue=lambda acc, av: (acc * 2.0 * jnp.maximum(av, 0.0),),
             tiles=[sv["a_ff"]], out_dtypes=(BF16,), name=n("d_a_ff"))
    gr["w_ff1"] = _mm(sv["h2"], da, ta=True, name=n("d_w_ff1"))
    dh2 = _mm(da, w["w_ff1"], tb=True, out_dtypes=(BF16,), name=n("d_h2"))
    dx1, dg = _rms_bwd(sv["x1"], _row(w["g_mlp"]), dh2, dx2, name=n("rms_mlp_bwd"))
    gr["g_mlp"] = dg[0]

    gr["w_out"] = _mm(sv["merged"], dx1, ta=True, name=n("d_w_out"))
    dmerged = _mm(dx1, w["w_out"], tb=True, name=n("d_merged"))

    def merge_bwd(dm, gts, y0, y1, y2, y3):
        dys, dgs = [], []
        for b, yb in enumerate((y0, y1, y2, y3)):
            sg = _sigmoid(gts[:, b * D_MODEL:(b + 1) * D_MODEL])
            dys.append(dm * sg)
            dgs.append(dm * yb * sg * (1.0 - sg))
        return (*dys, jnp.concatenate(dgs, axis=1))

    *dybs, dgates = _rowwise(merge_bwd, [dmerged, u["gates"]] + sv["ybs"], [],
                             [(D_MODEL, BF16, "row")] * 4 + [(4 * D_MODEL, BF16, "row")], name=n("merge_bwd"))
    ys = [sv["y_a"], sv["y_b"], sv["y_c"], sv["y_d"]]
    dwb = [_mm(ys[b], dybs[b], ta=True, name=n(f"d_w_branch{b}")) for b in range(4)]
    gr["w_branch"] = jnp.stack([_unpad_blocks(dwb[0], 0, N_HEADS, HEAD), dwb[1],
                                _unpad_blocks(dwb[2], 0, N_HEADS, HEAD), dwb[3]])
    dy_a = _mm(dybs[0], w["br"][0], tb=True, out_dtypes=(BF16,), name=n("d_y_a"))
    dy_b = _mm(dybs[1], w["br"][1], tb=True, name=n("d_y_b"))
    dy_c = _mm(dybs[2], w["br"][2], tb=True, name=n("d_y_c"))
    dy_d = _mm(dybs[3], w["br"][3], tb=True, name=n("d_y_d"))
    du = {"gates": dgates}

    lru_par = (_row(w["lru_lambda"]), _row(w["lru_b_a"]), _row(w["lru_b_i"]))
    dpa, dpi, dxc_direct, du["lru_g"], dlam, dba, dbi = _lru_bwd(
        sv["pre"], sv["xc"], u["lru_g"], *lru_par, sv["h_lru"], dy_d, name=n("lru_bwd"))
    dpre_lru = jnp.concatenate([dpa, dpi], axis=1)
    d_bd = _mm(sv["xc"], dpre_lru, ta=True, name=n("d_lru_w"))
    gr["lru_w_a"] = _block_diag_t(d_bd[:, :MIX], N_HEADS)
    gr["lru_w_i"] = _block_diag_t(d_bd[:, MIX:], N_HEADS)
    gr["lru_lambda"], gr["lru_b_a"], gr["lru_b_i"] = dlam[0], dba[0], dbi[0]
    dxc = _mm(dpre_lru, w["lru_bd"], tb=True, epilogue=lambda acc, t: (acc + t,), tiles=[dxc_direct], name=n("d_xc"))
    du["lru_x"], gr["lru_conv_w"], dcb = _conv_bwd(u["lru_x"], w["lru_conv_w"], _row(w["lru_conv_b"]), dxc,
                                                  silu=False, name=n("lru_conv_bwd"))
    gr["lru_conv_b"] = dcb[0]

    def ssd_post_bwd(dyc, yv, zv, gv):
        sz = _silu(zv)
        dyz, dgain = _rms_bwd_math(yv * sz, gv, dyc, MIX)
        return dyz * sz, dyz * yv * _silu_grad(zv), dgain

    dy_ssd, du["z"], dgain = _rowwise(ssd_post_bwd, [dy_c, sv["y_ssd"], u["z"]], [w["ssd_norm_pad"]],
                                      [(N_HEADS * LANES, F32, "row"), (N_HEADS * LANES, BF16, "row"),
                                       (N_HEADS * LANES, F32, "acc")], name=n("ssd_post_bwd"))
    gr["ssd_norm"] = _unpad_blocks(dgain[0], 0, N_HEADS, HEAD)
    ssd_par = (_scal3(w["ssd_dt_bias"]), _scal3(w["ssd_a_log"]), _scal3(w["ssd_d"]))
    dxs, dbg, dcg, ddt, dbias, dalog, dd = _ssd_bwd(sv["xbc_c"], sv["dtcol"], sv["dtrow"], *ssd_par, sv["states"],
                                                    dy_ssd, name=n("ssd_bwd"))
    s = dxs.shape[0]
    dxbc_c = jnp.concatenate([dxs, dbg, dcg], axis=1)
    gr["ssd_dt_bias"], gr["ssd_a_log"], gr["ssd_d"] = dbias[:, 0, 0], dalog[:, 0, 0], dd[:, 0, 0]
    du["xbc"], dcw, dcb = _conv_bwd(u["xbc"], w["ssd_conv_w_pad"], w["ssd_conv_b_pad"], dxbc_c, silu=True,
                                    name=n("ssd_conv_bwd"))
    gr["ssd_conv_w"], gr["ssd_conv_b"] = _unpad_xbc_vec(dcw), _unpad_xbc_vec(dcb[0])
    du["dt"] = jnp.pad(ddt[:, :, 0].T, ((0, 0), (0, LANES - N_HEADS)))

    dyb_pre, dscale = _rowwise(lambda d, yp, sc: (d * sc, _colsum(d * yp)), [dy_b, sv["yb_pre"]],
                               [_row(w["pool_scale"])], [(MIX, BF16, "row"), (MIX, F32, "acc")], name=n("pool_scale_bwd"))
    gr["pool_scale"] = dscale[0]
    gr["w_pool"] = _block_diag_t(_mm(sv["pool_d"], dyb_pre, ta=True, name=n("d_w_pool")), 4)
    dd_pool = _mm(dyb_pre, w["pool_bd"], tb=True, name=n("d_pool_d"))
    du["pool"] = _pool_bwd(dd_pool, name=n("pool_bwd"))

    delta = _att_delta(sv["y_a"], dy_a, name=n("att_delta"))
    to_row = lambda t: t.reshape(N_HEADS, 1, s)
    dqc, dkc, dvc = _flash_bwd(sv["qc"], sv["kc"], sv["vc"], dy_a, to_row(sv["lse"]), to_row(delta), name=n("flash_bwd"))
    dq_pad, du["kr"] = _att_prep_bwd(dqc, dkc, *rope, name=n("att_prep_bwd"))
    d_uq = _mm(sv["cqn"], dq_pad, ta=True, name=n("d_w_uq"))
    gr["w_uq"] = _unpad_blocks(d_uq, 1, N_HEADS, HEAD + QK_ROPE)
    dcqn = _mm(dq_pad, w["uq"], tb=True, out_dtypes=(BF16,), name=n("d_cqn"))
    du["cq"], dg = _rms_bwd(u["cq"], _row(w["q_norm"]), dcqn, name=n("rms_q_bwd"))
    gr["q_norm"] = dg[0]
    dkv2 = jnp.concatenate([dkc, dvc], axis=1).astype(BF16)
    d_ukv = _mm(sv["ckvn"], dkv2, ta=True, name=n("d_w_ukv"))
    wk = N_HEADS * LANES
    dk_real = _unpad_blocks(d_ukv[:, :wk], 1, N_HEADS, HEAD).reshape(KV_LORA, N_HEADS, HEAD)
    dv_real = _unpad_blocks(d_ukv[:, wk:], 1, N_HEADS, HEAD).reshape(KV_LORA, N_HEADS, HEAD)
    gr["w_ukv"] = jnp.concatenate([dk_real, dv_real], axis=2).reshape(KV_LORA, N_HEADS * 2 * HEAD)
    dckvn = _mm(dkv2, w["ukv"], tb=True, out_dtypes=(BF16,), name=n("d_ckvn"))
    du["ckv"], dg = _rms_bwd(u["ckv"], _row(w["kv_norm"]), dckvn, name=n("rms_kv_bwd"))
    gr["kv_norm"] = dg[0]

    du_all = _w_in_fused({k: v.astype(BF16) for k, v in du.items()})
    dw_all = _mm(sv["h"], du_all, ta=True, name=n("d_w_in"))
    gr["w_in"] = _w_in_ungroup({k: dw_all[:, off:off + width] for k, off, width in IN_LAYOUT})
    dh = _mm(du_all, w["in_all"], tb=True, name=n("d_h"))
    dx, dg = _rms_bwd(sv["x"], _row(w["g_mix"]), dh, dx1, name=n("rms_mix_bwd"))
    gr["g_mix"] = dg[0]
    return dx, gr


def _pack_rows(n_elems):
    per = PACK_W * PACK_ROWS
    return -(-n_elems // per) * PACK_ROWS


def _pack_flat(parts, dtype):
    flat = jnp.concatenate([p.reshape(-1).astype(dtype) for p in parts])
    rows = _pack_rows(flat.shape[0])
    return jnp.pad(flat, (0, rows * PACK_W - flat.shape[0])).reshape(rows, PACK_W)


def _unpack_flat(buf, shapes):
    lead = buf.shape[:-2]
    flat = buf.reshape(lead + (-1,))
    out, off = [], 0
    for shp in shapes:
        size = int(np.prod(shp))
        out.append(flat[..., off:off + size].reshape(lead + tuple(shp)))
        off += size
    return out


def _merge_shards(t, axis):
    return jnp.concatenate([t[i] for i in range(4)], axis=axis)


def _split_shards(t, axis):
    return jnp.stack(jnp.split(t, 4, axis=axis))


def _rope_tables(positions):
    inv = 1.0 / (ROPE_THETA ** (jnp.arange(0, QK_ROPE, 2, dtype=F32) / QK_ROPE))
    ang = positions.astype(F32)[:, None] * inv
    cos, sin = jnp.cos(ang), jnp.sin(ang)
    s = ang.shape[0]
    half = QK_ROPE // 2
    z = lambda n_: jnp.zeros((s, n_), F32)
    cos_t = jnp.concatenate([jnp.ones((s, HEAD), F32), cos, cos, jnp.ones((s, LANES - HEAD - QK_ROPE), F32)], axis=1)
    sin_p = jnp.concatenate([z(HEAD + half), sin, z(LANES - HEAD - QK_ROPE)], axis=1)
    sin_m = jnp.concatenate([z(HEAD), -sin, z(half + LANES - HEAD - QK_ROPE)], axis=1)
    return cos_t, sin_p, sin_m


def _loss_head(x, g, target, *, name):
    d = x.shape[1]

    def fn(xv, tv, gv):
        xh, r = _rms_parts(xv, d)
        y = xh * gv
        err = y - tv
        dy = err * (1.0 / d)
        dxh = dy * gv
        dx = r * (dxh - xh * (jnp.sum(dxh * xh, axis=-1, keepdims=True) * (1.0 / d)))
        return dx, _colsum(dy * xh), _colsum(err * err) * (0.5 / d)

    return _rowwise(fn, [x, target], [g], [(d, F32, "row"), (d, F32, "acc"), (d, F32, "acc")], name=name)


MATS = tuple((nm, ax) for nm, ax in BIG if nm not in CONV_SHARDED)


def _grad_view(g, ax_layer):
    if ax_layer == 0:
        return g.reshape(4, g.shape[0] // 4, g.shape[1])
    return g.reshape(1, -1, g.shape[-1])


def _reduce_start(grads_l, c_idx, tag):
    views = [_grad_view(grads_l[nm], ax - 1) for nm, ax in MATS]
    got = _send_half(views, name="send_half_" + tag)
    parts = []
    for (nm, ax), v, gt in zip(MATS, views, got):
        both = _chip_sum_half(v, gt, c_idx, name=f"chip_sum_{nm}_{tag}")
        parts.append(both if ax == 1 else _split_shards(both[0], 1))
    return _push_start(parts, scatter=True, name="push_grads_" + tag)


def _reduce_finish(state, after, k_chip, tag):
    send_sems, recv_sems, parts, lands, _ = state
    parts, landed = _push_wait(send_sems, recv_sems, parts, lands, after, name="wait_grads_" + tag)
    mine = [lax.dynamic_update_index_in_dim(t, lax.dynamic_index_in_dim(p, k_chip, 0, keepdims=False), k_chip, 0)
            for t, p in zip(landed, parts)]
    return list(zip(mine, _swap_with_sibling(mine, name="swap_halves_" + tag)))


def _step(args):
    x = args["x"][0]
    c_idx = lax.axis_index("c")
    k_chip = 2 * lax.axis_index("x") + lax.axis_index("y")

    mats = MATS
    mine = [[args[nm][l].astype(BF16) for nm, _ in mats] for l in range(2)]
    gathered0 = _gather_halves(mine[0])
    convs = [(nm, ax) for nm, ax in BIG if nm in CONV_SHARDED]
    conv_all = _gather_all(_pack_flat([args[nm] for nm, _ in convs], F32), name="gather_conv_taps")[0::2]
    mine1, gathered0, conv_all = lax.optimization_barrier((mine[1], gathered0, conv_all))
    gathered0 = [lax.dynamic_update_index_in_dim(t, own, k_chip, 0) for t, own in zip(gathered0, mine[0])]
    push1 = _push_start(mine1, scatter=False, name="push_weights_l1")
    full_conv = {nm: _merge_shards(t, ax)
                 for (nm, ax), t in zip(convs, _unpack_flat(conv_all, [args[nm].shape for nm, _ in convs]))}
    rope = _rope_tables(args["positions"][0])

    def layer_weights(l, gathered):
        p = {nm: _merge_shards(t, ax - 1) for (nm, ax), t in zip(mats, gathered)}
        p.update({nm: full_conv[nm][l] for nm in CONV_SHARDED})
        p.update({nm: args[nm][l] for nm in SMALL if nm != "g_final"})
        return _layer_weights(p)

    layers = [layer_weights(0, gathered0), None]
    layers[0]["g_mix"] = layers[0]["g_mix"] + push1[4][0, 0]
    x, sv0 = _layer_fwd(x, args["p"][0, 0], layers[0], rope, "l0")
    own1, landed1 = _push_wait(push1[0], push1[1], push1[2], push1[3], x, name="wait_weights_l1")
    layers[1] = layer_weights(1, [lax.dynamic_update_index_in_dim(t, own, k_chip, 0) for t, own in zip(landed1, own1)])
    x, sv1 = _layer_fwd(x, args["p"][1, 0], layers[1], rope, "l1")
    saved = [sv0, sv1]

    dx, dg_final, loss_part = _loss_head(x, _row(args["g_final"]), args["loss_target"][0], name="loss_head")
    loss = lax.psum(jnp.sum(loss_part), ("x", "y", "c"))

    grads = [None, None]
    dx, grads[1] = _layer_bwd(dx, saved[1], layers[1], rope, "l1")
    reduce1 = _reduce_start(grads[1], c_idx, "l1")
    dx, grads[0] = _layer_bwd(dx + reduce1[4][0, 0], saved[0], layers[0], rope, "l0")
    g_all = {nm: jnp.stack([grads[0][nm], grads[1][nm]]) for nm in SMALL + CONV_SHARDED if nm != "g_final"}
    g_all["g_final"] = dg_final[0]
    all_names = SMALL + CONV_SHARDED
    all_shapes = [g_all[nm].shape for nm in all_names]
    packed = _pack_flat([g_all[nm] for nm in all_names], F32)
    sibling = _swap_with_sibling([packed], name="swap_small_grads")[0]
    pair = jnp.where(c_idx == 0, jnp.stack([packed, sibling]), jnp.stack([sibling, packed]))
    small_all = _gather_same_core(_sum_slots(pair, name="sum_cores"), name="gather_small_grads")
    g0_mats, small_all = lax.optimization_barrier(({nm: grads[0][nm] for nm, _ in MATS}, small_all))
    reduce0 = _reduce_start(g0_mats, c_idx, "l0")
    small_sum = _sum_slots(small_all, name="sum_chips")
    g_red = dict(zip(all_names, _unpack_flat(small_sum, all_shapes)))
    for nm, ax in BIG:
        if nm in CONV_SHARDED:
            width = args[nm].shape[ax]
            g_red[nm] = lax.dynamic_slice_in_dim(g_red[nm], k_chip * width, width, axis=ax)
    small_shapes = [args[nm].shape for nm in SMALL]
    pack_small = lambda src: _pack_flat([src(nm) for nm in SMALL], F32)
    upd_small = _adamw(pack_small(lambda nm: args[nm]), pack_small(lambda nm: g_red[nm]),
                       pack_small(lambda nm: args["m_" + nm]), pack_small(lambda nm: args["v_" + nm]), name="adamw_small")
    upd = {nm: trip for nm, trip in zip(SMALL, zip(*[_unpack_flat(t, small_shapes) for t in upd_small]))}
    for nm in CONV_SHARDED:
        upd[nm] = _adamw(args[nm], g_red[nm], args["m_" + nm], args["v_" + nm], name="adamw_" + nm)

    slots = [_reduce_finish(reduce0, upd_small[0], k_chip, "l0"), _reduce_finish(reduce1, dx, k_chip, "l1")]
    for i, (nm, _) in enumerate(MATS):
        g_red[nm], *upd[nm] = _adamw_slots(args[nm], [slots[0][i], slots[1][i]], args["m_" + nm], args["v_" + nm],
                                           c_idx, name="adamw_" + nm)

    outs = [loss, dx[None]]
    outs += [g_red[nm] for nm in WEIGHTS]
    for i in range(3):
        outs += [upd[nm][i] for nm in WEIGHTS]
    return tuple(outs)


_ARG_NAMES = ("x", "p", "positions") + WEIGHTS + ("loss_target",) + tuple("m_" + nm for nm in WEIGHTS) \
    + tuple("v_" + nm for nm in WEIGHTS)


def kernel(*arrays):
    assert len(arrays) == len(_ARG_NAMES), len(arrays)
    return _step(dict(zip(_ARG_NAMES, arrays)))
```

```python
import functools
import math

import jax
import jax.numpy as jnp
import numpy as np
from jax import lax
from jax.experimental import pallas as pl
from jax.experimental.pallas import tpu as pltpu

F32 = jnp.float32
BF16 = jnp.bfloat16
MXU_DTYPE = BF16
LANES = 128
VMEM_LIMIT = 56 * 1024 * 1024
MM_VMEM_BUDGET = 36 * 1024 * 1024
ELEMENTWISE_BLOCK_BYTES = 2 * 1024 * 1024

D_MODEL = 1024
N_HEADS = 8
HEAD = 64
QK_ROPE = 32
Q_LORA = 384
KV_LORA = 256
MIX = 512
SSD_CHUNK = 128
CONV_W = 4
POOL_WINDOWS = (2, 4, 8, 16)
LRU_C = 8.0
EPS = 1e-6
ROPE_THETA = 10000.0
ATT_SCALE = (HEAD + QK_ROPE) ** -0.5
SPLIT_SIZES = (Q_LORA, KV_LORA, QK_ROPE, MIX, MIX, 768, N_HEADS, MIX, MIX, 4 * D_MODEL)
IN_LAYOUT = (("gates", 0, 4096), ("z", 4096, 1024), ("pool", 5120, 512), ("lru_g", 5632, 512), ("lru_x", 6144, 512),
             ("cq", 6912, 384), ("ckv", 7424, 256), ("xbc", 7680, 1536), ("kr", 9216, 128), ("dt", 9344, 128))
IN_OFFSETS = {name: (off, width) for name, off, width in IN_LAYOUT}
IN_ALL_COLS = 9728

ADAM_LR, ADAM_B1, ADAM_B2, ADAM_EPS, ADAM_WD, ADAM_STEP = 0.001, 0.9, 0.999, 1e-08, 0.01, 10

BIG = (("w_in", 2), ("w_uq", 2), ("w_ukv", 2), ("ssd_conv_w", 2), ("lru_conv_w", 2), ("w_branch", 3),
       ("w_out", 1), ("w_ff1", 2), ("w_ff2", 1), ("w_ple_gate", 1), ("w_ple", 2))
SMALL = ("g_mix", "q_norm", "kv_norm", "w_pool", "pool_scale", "ssd_conv_b", "ssd_dt_bias", "ssd_a_log",
         "ssd_d", "ssd_norm", "lru_conv_b", "lru_w_a", "lru_b_a", "lru_w_i", "lru_b_i", "lru_lambda",
         "g_mlp", "g_ple", "g_final")
WEIGHTS = ("g_mix", "w_in", "q_norm", "w_uq", "kv_norm", "w_ukv", "w_pool", "pool_scale", "ssd_conv_w",
           "ssd_conv_b", "ssd_dt_bias", "ssd_a_log", "ssd_d", "ssd_norm", "lru_conv_w", "lru_conv_b", "lru_w_a",
           "lru_b_a", "lru_w_i", "lru_b_i", "lru_lambda", "w_branch", "w_out", "g_mlp", "w_ff1", "w_ff2", "g_ple",
           "w_ple_gate", "w_ple", "g_final")
CONV_SHARDED = ("ssd_conv_w", "lru_conv_w")
PACK_W = 1024
PACK_ROWS = 64


def _cparams(sem, vmem=VMEM_LIMIT):
    return pltpu.CompilerParams(dimension_semantics=sem, vmem_limit_bytes=vmem)


def _pick(n, cands):
    for c in cands:
        if n % c == 0:
            return c
    return n


class _Cols:
    def __init__(self, arr, off, width):
        self.arr, self.off, self.width = arr, off, width

    shape = property(lambda self: (self.arr.shape[0], self.width))
    dtype = property(lambda self: self.arr.dtype)


def _arr(x):
    return x.arr if isinstance(x, _Cols) else x


def _off(x, unit):
    off = x.off if isinstance(x, _Cols) else 0
    assert off % unit == 0, (off, unit)
    return off // unit


def _sigmoid(x):
    return 1.0 / (1.0 + jnp.exp(-x))


def _silu(x):
    return x * _sigmoid(x)


def _silu_grad(x):
    s = _sigmoid(x)
    return s * (1.0 + x * (1.0 - s))


def _softplus(x):
    e = jnp.exp(-jnp.abs(x))
    log1p_e = jnp.where(e < 1e-3, e * (1.0 - e * (0.5 - e * (1.0 / 3.0))), jnp.log(1.0 + e))
    return jnp.maximum(x, 0.0) + log1p_e


_GELU_C = math.sqrt(2.0 / math.pi)


def _gelu(x):
    t = jnp.tanh(_GELU_C * (x + 0.044715 * x * x * x))
    return 0.5 * x * (1.0 + t)


def _gelu_grad(x):
    t = jnp.tanh(_GELU_C * (x + 0.044715 * x * x * x))
    return 0.5 * (1.0 + t) + 0.5 * x * (1.0 - t * t) * _GELU_C * (1.0 + 3.0 * 0.044715 * x * x)


def _neg_expm1(x):
    series = -x * (1.0 + 0.5 * x * (1.0 + (1.0 / 3.0) * x * (1.0 + 0.25 * x)))
    return jnp.where(x > -0.05, series, 1.0 - jnp.exp(x))


def _shift_down(x, k, row):
    return jnp.where(row >= k, pltpu.roll(x, k, 0), 0.0)


def _shift_up(x, k, row):
    n = x.shape[0]
    return jnp.where(row < n - k, pltpu.roll(x, n - k, 0), 0.0)


def _cumsum_rows(x, row):
    d = 1
    while d < x.shape[0]:
        x = x + _shift_down(x, d, row)
        d *= 2
    return x


def _rev_cumsum_rows(x, row):
    d = 1
    while d < x.shape[0]:
        x = x + _shift_up(x, d, row)
        d *= 2
    return x


def _cumsum_lanes(x, col):
    d = 1
    while d < x.shape[1]:
        x = x + jnp.where(col >= d, pltpu.roll(x, d, 1), 0.0)
        d *= 2
    return x


def _dot(a, b, ta=False, tb=False):
    dn = (((0 if ta else 1,), (1 if tb else 0,)), ((), ()))
    return lax.dot_general(a.astype(MXU_DTYPE), b.astype(MXU_DTYPE), dn, preferred_element_type=F32)


def _mm_tiles(m, n, k, a_bytes, b_bytes, mn_bytes):
    best = None
    for tm in (1024, 512, 384, 256, 128):
        for tn in (1024, 512, 384, 256, 128):
            for tk in (2048, 1024, 512, 384, 256, 128):
                if m % tm or n % tn or k % tk:
                    continue
                vmem = 2 * (tm * tk * a_bytes + tk * tn * b_bytes) + 2 * tm * tn * mn_bytes + 4 * tm * tn
                vmem += 2 * (tm * tk + tk * tn)
                if vmem > MM_VMEM_BUDGET:
                    continue
                steps = (m // tm) * (n // tn) * (k // tk)
                key = (steps, vmem)
                if best is None or key < best[0]:
                    best = (key, (tm, tn, tk))
    assert best is not None, (m, n, k)
    return best[1]


def _mm(a, b, *, ta=False, tb=False, epilogue=None, tiles=(), rowvecs=(), out_dtypes=(F32,), name):
    m, k = (a.shape[1], a.shape[0]) if ta else a.shape
    n = b.shape[0] if tb else b.shape[1]
    assert (b.shape[1] if tb else b.shape[0]) == k, (a.shape, b.shape, ta, tb)
    mn_bytes = sum(t.dtype.itemsize for t in tiles) + sum(jnp.dtype(dt).itemsize for dt in out_dtypes)
    tm, tn, tk = _mm_tiles(m, n, k, a.dtype.itemsize, b.dtype.itemsize, mn_bytes)
    nk = k // tk
    nt, nr, no = len(tiles), len(rowvecs), len(out_dtypes)

    def body(*refs):
        a_ref, b_ref = refs[0], refs[1]
        tile_refs = refs[2:2 + nt]
        row_refs = refs[2 + nt:2 + nt + nr]
        out_refs = refs[2 + nt + nr:2 + nt + nr + no]
        acc_ref = refs[-1]
        kk = pl.program_id(2)

        @pl.when(kk == 0)
        def _():
            acc_ref[...] = jnp.zeros_like(acc_ref)

        acc_ref[...] += _dot(a_ref[...], b_ref[...], ta, tb)

        @pl.when(kk == nk - 1)
        def _():
            acc = acc_ref[...]
            if epilogue is None:
                outs = (acc,)
            else:
                outs = epilogue(acc, *[t[...] for t in tile_refs], *[r[...] for r in row_refs])
            for o_ref, o in zip(out_refs, outs):
                o_ref[...] = o.astype(o_ref.dtype)

    a_spec = pl.BlockSpec((tk, tm), lambda i, j, kk: (kk, i)) if ta else pl.BlockSpec((tm, tk), lambda i, j, kk: (i, kk))
    b_spec = pl.BlockSpec((tn, tk), lambda i, j, kk: (j, kk)) if tb else pl.BlockSpec((tk, tn), lambda i, j, kk: (kk, j))
    mn_spec = pl.BlockSpec((tm, tn), lambda i, j, kk: (i, j))
    row_spec = pl.BlockSpec((1, tn), lambda i, j, kk: (0, j))
    tile_specs = [pl.BlockSpec((tm, tn), lambda i, j, kk, ob=_off(t, tn): (i, j + ob)) for t in tiles]
    outs = pl.pallas_call(
        body, name=name,
        grid=(m // tm, n // tn, nk),
        in_specs=[a_spec, b_spec] + tile_specs + [row_spec] * nr,
        out_specs=[mn_spec] * no,
        out_shape=[jax.ShapeDtypeStruct((m, n), dt) for dt in out_dtypes],
        scratch_shapes=[pltpu.VMEM((tm, tn), F32)],
        compiler_params=_cparams(("parallel", "parallel", "arbitrary")),
    )(a, b, *[_arr(t) for t in tiles], *rowvecs)
    return outs[0] if no == 1 else tuple(outs)


def _branch_merge(ys, ws, u_all, *, name):
    s, d = ys[0].shape[0], ws[0].shape[1]
    tm, tn = _pick(s, (512, 256, 128)), _pick(d, (512, 256, 128))
    nb = len(ys)

    def body(*refs):
        y_refs, w_refs, g_refs = refs[:nb], refs[nb:2 * nb], refs[2 * nb:3 * nb]
        merged_ref, yb_refs = refs[3 * nb], refs[3 * nb + 1:]
        merged = None
        for y_ref, w_ref, g_ref, yb_ref in zip(y_refs, w_refs, g_refs, yb_refs):
            acc = _dot(y_ref[...], w_ref[...])
            yb_ref[...] = acc.astype(yb_ref.dtype)
            term = _sigmoid(g_ref[...]) * acc
            merged = term if merged is None else merged + term
        merged_ref[...] = merged

    mn = pl.BlockSpec((tm, tn), lambda i, j: (i, j))
    in_specs = [pl.BlockSpec((tm, y.shape[1]), lambda i, j: (i, 0)) for y in ys]
    in_specs += [pl.BlockSpec((w.shape[0], tn), lambda i, j: (0, j)) for w in ws]
    in_specs += [pl.BlockSpec((tm, tn), lambda i, j, ob=b * d // tn: (i, j + ob)) for b in range(nb)]
    return pl.pallas_call(
        body, name=name, grid=(s // tm, d // tn), in_specs=in_specs, out_specs=[mn] * (nb + 1),
        out_shape=[jax.ShapeDtypeStruct((s, d), F32)] + [jax.ShapeDtypeStruct((s, d), BF16)] * nb,
        compiler_params=_cparams(("parallel", "parallel")),
    )(*ys, *ws, *[u_all] * nb)


def _rowwise(fn, rows, fulls, outs, *, name, tm=None):
    r = rows[0].shape[0]
    if tm is None:
        widest = max([x.shape[1] for x in rows] + [o[0] for o in outs])
        tm = _pick(r, (max(8, min(512, (512 * 1024) // widest)), 256, 128, 64, 32, 16, 8))
    nrow, nfull, nout = len(rows), len(fulls), len(outs)

    def body(*refs):
        row_refs = refs[:nrow]
        full_refs = refs[nrow:nrow + nfull]
        out_refs = refs[nrow + nfull:]
        res = fn(*[x[...] for x in row_refs], *[x[...] for x in full_refs])
        if not isinstance(res, (tuple, list)):
            res = (res,)
        step = pl.program_id(0)
        for o_ref, o, spec in zip(out_refs, res, outs):
            if spec[2] == "row":
                o_ref[...] = o.astype(o_ref.dtype)
            else:
                @pl.when(step == 0)
                def _(o_ref=o_ref):
                    o_ref[...] = jnp.zeros_like(o_ref)
                o_ref[...] += o

    in_specs = [pl.BlockSpec((tm, x.shape[1]), lambda i, ob=_off(x, x.shape[1]): (i, ob)) for x in rows]
    in_specs += [pl.BlockSpec(x.shape, lambda i, nd=x.ndim: (0,) * nd) for x in fulls]
    out_specs, out_shape = [], []
    for c, dt, kind in outs:
        if kind == "row":
            out_specs.append(pl.BlockSpec((tm, c), lambda i: (i, 0)))
            out_shape.append(jax.ShapeDtypeStruct((r, c), dt))
        else:
            out_specs.append(pl.BlockSpec((1, c), lambda i: (0, 0)))
            out_shape.append(jax.ShapeDtypeStruct((1, c), F32))
    res = pl.pallas_call(
        body, name=name, grid=(r // tm,), in_specs=in_specs, out_specs=out_specs, out_shape=out_shape,
        compiler_params=_cparams(("arbitrary",)),
    )(*[_arr(x) for x in rows], *fulls)
    return res[0] if nout == 1 else tuple(res)


def _colsum(x):
    return jnp.sum(x, axis=0, keepdims=True)


def _rms_parts(x, n_real):
    r = lax.rsqrt(jnp.sum(x * x, axis=-1, keepdims=True) * (1.0 / n_real) + EPS)
    return x * r, r


def _rms_fwd(x, g, *, n_real=None, out_dtype=BF16, name):
    n_real = n_real or x.shape[1]

    def fn(xv, gv):
        xh, _ = _rms_parts(xv, n_real)
        return xh * gv

    return _rowwise(fn, [x], [g], [(x.shape[1], out_dtype, "row")], name=name)


def _rms_bwd_math(xv, gv, dh, n_real):
    xh, r = _rms_parts(xv, n_real)
    dxh = dh * gv
    dx = r * (dxh - xh * (jnp.sum(dxh * xh, axis=-1, keepdims=True) * (1.0 / n_real)))
    return dx, _colsum(dh * xh)


def _rms_bwd(x, g, dh, res=None, *, name):
    n = x.shape[1]
    if res is None:
        def fn(xv, dhv, gv):
            return _rms_bwd_math(xv, gv, dhv.astype(F32), n)
        rows = [x, dh]
    else:
        def fn(xv, dhv, rv, gv):
            dx, dg = _rms_bwd_math(xv, gv, dhv.astype(F32), n)
            return dx + rv, dg
        rows = [x, dh, res]
    return _rowwise(fn, rows, [g], [(n, F32, "row"), (n, F32, "acc")], name=name)


def _seq_call(body, ins, outs, n_blocks, *, name):
    in_specs, args = [], []
    for x, kind in ins:
        in_specs.append(pl.BlockSpec((x.shape[0], LANES), lambda j, ob=_off(x, LANES): (0, j + ob)))
        args.append(_arr(x))
    out_specs, out_shape = [], []
    for shape, dt in outs:
        out_specs.append(pl.BlockSpec((shape[0], LANES), lambda j: (0, j)))
        out_shape.append(jax.ShapeDtypeStruct(shape, dt))
    res = pl.pallas_call(body, name=name, grid=(n_blocks,), in_specs=in_specs, out_specs=out_specs,
                         out_shape=out_shape, compiler_params=_cparams(("parallel",)))(*args)
    return res[0] if len(outs) == 1 else tuple(res)


def _conv_pre(x, w, b, row):
    acc = x * w[CONV_W - 1:CONV_W, :] + b
    for k in range(CONV_W - 1):
        acc = acc + _shift_down(x, CONV_W - 1 - k, row) * w[k:k + 1, :]
    return acc


def _conv_fwd(x, w, b, *, silu, name):
    s, c = x.shape

    def body(x_ref, w_ref, b_ref, y_ref):
        xv = x_ref[...]
        row = lax.broadcasted_iota(jnp.int32, xv.shape, 0)
        pre = _conv_pre(xv, w_ref[...], b_ref[...], row)
        y_ref[...] = _silu(pre) if silu else pre

    return _seq_call(body, [(x, "seq"), (w, "par"), (b, "par")], [((s, c), F32)], c // LANES, name=name)


def _conv_bwd(x, w, b, dy, *, silu, name):
    s, c = x.shape

    def body(x_ref, w_ref, b_ref, dy_ref, dx_ref, dw_ref, db_ref):
        xv, wv, dv = x_ref[...], w_ref[...], dy_ref[...]
        row = lax.broadcasted_iota(jnp.int32, xv.shape, 0)
        if silu:
            dv = dv * _silu_grad(_conv_pre(xv, wv, b_ref[...], row))
        dx = dv * wv[CONV_W - 1:CONV_W, :]
        dws = [None] * CONV_W
        dws[CONV_W - 1] = _colsum(dv * xv)
        for k in range(CONV_W - 1):
            sh = CONV_W - 1 - k
            dx = dx + _shift_up(dv, sh, row) * wv[k:k + 1, :]
            dws[k] = _colsum(dv * _shift_down(xv, sh, row))
        dx_ref[...] = dx
        for k in range(CONV_W):
            dw_ref[k:k + 1, :] = dws[k]
        db_ref[...] = _colsum(dv)

    return _seq_call(body, [(x, "seq"), (w, "par"), (b, "par"), (dy, "seq")],
                     [((s, c), F32), ((CONV_W, c), F32), ((1, c), F32)], c // LANES, name=name)


def _pool_select(levels):
    g = pl.program_id(0)
    return jnp.where(g == 0, levels[0], jnp.where(g == 1, levels[1], jnp.where(g == 2, levels[2], levels[3])))


def _pool_count(row):
    g = pl.program_id(0)
    w = jnp.where(g == 0, POOL_WINDOWS[0], jnp.where(g == 1, POOL_WINDOWS[1],
                                                     jnp.where(g == 2, POOL_WINDOWS[2], POOL_WINDOWS[3])))
    return jnp.minimum(row + 1, w).astype(F32)


def _pool_fwd(u, *, name):
    def body(u_ref, d_ref):
        uv = u_ref[...]
        row = lax.broadcasted_iota(jnp.int32, uv.shape, 0)
        levels, cur, sh = [], uv, 1
        for _ in POOL_WINDOWS:
            cur = cur + _shift_down(cur, sh, row)
            levels.append(cur)
            sh *= 2
        d_ref[...] = _pool_select(levels) / _pool_count(row) - uv

    return _seq_call(body, [(u, "seq")], [(u.shape, F32)], u.shape[1] // LANES, name=name)


def _pool_bwd(dd, *, name):
    def body(dd_ref, du_ref):
        dv = dd_ref[...]
        row = lax.broadcasted_iota(jnp.int32, dv.shape, 0)
        levels, cur, sh = [], dv / _pool_count(row), 1
        for _ in POOL_WINDOWS:
            cur = cur + _shift_up(cur, sh, row)
            levels.append(cur)
            sh *= 2
        du_ref[...] = _pool_select(levels) - dv

    return _seq_call(body, [(dd, "seq")], [(dd.shape, F32)], dd.shape[1] // LANES, name=name)


def _lru_gates(pre_a, pre_i, xc, lam, b_a, b_i):
    r = _sigmoid(pre_a + b_a)
    i = _sigmoid(pre_i + b_i)
    sp = _softplus(-lam)
    log_a = -LRU_C * r * sp
    a = jnp.exp(log_a)
    mult = jnp.sqrt(_neg_expm1(2.0 * log_a))
    return r, i, sp, a, mult


def _lru_fwd(pre, xc, gate_in, lam, b_a, b_i, *, name):
    s, c = xc.shape
    nb = c // LANES

    def body(pa_ref, pi_ref, xc_ref, g_ref, lam_ref, ba_ref, bi_ref, y_ref, h_ref):
        xv = xc_ref[...]
        row = lax.broadcasted_iota(jnp.int32, xv.shape, 0)
        _, i, _, a, mult = _lru_gates(pa_ref[...], pi_ref[...], xv, lam_ref[...], ba_ref[...], bi_ref[...])
        h = xv * i * mult
        d = 1
        while d < s:
            h = h + a * _shift_down(h, d, row)
            a = a * jnp.where(row >= d, pltpu.roll(a, d, 0), 1.0)
            d *= 2
        h_ref[...] = h
        y_ref[...] = h * _gelu(g_ref[...])

    blk = lambda off: pl.BlockSpec((s, LANES), lambda j: (0, j + off))
    par = pl.BlockSpec((1, LANES), lambda j: (0, j))
    return pl.pallas_call(
        body, name=name, grid=(nb,),
        in_specs=[blk(0), blk(nb), blk(0), blk(_off(gate_in, LANES)), par, par, par],
        out_specs=[blk(0), blk(0)],
        out_shape=[jax.ShapeDtypeStruct((s, c), F32)] * 2,
        compiler_params=_cparams(("parallel",)),
    )(pre, pre, xc, _arr(gate_in), lam, b_a, b_i)


def _lru_bwd(pre, xc, gate_in, lam, b_a, b_i, h, dy, *, name):
    s, c = xc.shape
    nb = c // LANES

    def body(pa_ref, pi_ref, xc_ref, g_ref, lam_ref, ba_ref, bi_ref, h_ref, dy_ref,
             dpa_ref, dpi_ref, dxc_ref, dg_ref, dlam_ref, dba_ref, dbi_ref):
        xv, gv, hv, dv = xc_ref[...], g_ref[...], h_ref[...], dy_ref[...]
        row = lax.broadcasted_iota(jnp.int32, xv.shape, 0)
        r, i, sp, a, mult = _lru_gates(pa_ref[...], pi_ref[...], xv, lam_ref[...], ba_ref[...], bi_ref[...])
        dg_ref[...] = dv * hv * _gelu_grad(gv)
        dh = dv * _gelu(gv)
        an = jnp.where(row < s - 1, pltpu.roll(a, s - 1, 0), 0.0)
        d = 1
        while d < s:
            dh = dh + an * _shift_up(dh, d, row)
            an = an * jnp.where(row < s - d, pltpu.roll(an, s - d, 0), 1.0)
            d *= 2
        da = dh * _shift_down(hv, 1, row)
        dxc_ref[...] = dh * i * mult
        di = dh * xv * mult
        dmult = dh * xv * i
        dlog_a = (da - dmult * a / mult) * a
        dr = dlog_a * (-LRU_C) * sp
        dlam_ref[...] = _colsum(dlog_a * LRU_C * r * _sigmoid(-lam_ref[...]))
        dpa = dr * r * (1.0 - r)
        dpi = di * i * (1.0 - i)
        dpa_ref[...] = dpa
        dpi_ref[...] = dpi
        dba_ref[...] = _colsum(dpa)
        dbi_ref[...] = _colsum(dpi)

    blk = lambda off: pl.BlockSpec((s, LANES), lambda j: (0, j + off))
    par = pl.BlockSpec((1, LANES), lambda j: (0, j))
    sc = jax.ShapeDtypeStruct((s, c), F32)
    pc = jax.ShapeDtypeStruct((1, c), F32)
    dpa, dpi, dxc, dg, dlam, dba, dbi = pl.pallas_call(
        body, name=name, grid=(nb,),
        in_specs=[blk(0), blk(nb), blk(0), blk(_off(gate_in, LANES)), par, par, par, blk(0), blk(0)],
        out_specs=[blk(0), blk(0), blk(0), blk(0), par, par, par],
        out_shape=[sc, sc, sc, sc, pc, pc, pc],
        compiler_params=_cparams(("parallel",)),
    )(pre, pre, xc, _arr(gate_in), lam, b_a, b_i, h, dy)
    return dpa, dpi, dxc, dg, dlam, dba, dbi


GROUP_HEADS = 4
SSD_GROUPS = 2


def _ssd_specs(nc, order):
    hw, gw = N_HEADS * LANES, SSD_GROUPS * LANES
    return dict(
        x=pl.BlockSpec((SSD_CHUNK, hw), lambda ci: (order(ci), 0)),
        b=pl.BlockSpec((SSD_CHUNK, gw), lambda ci: (order(ci), hw // gw)),
        c=pl.BlockSpec((SSD_CHUNK, gw), lambda ci: (order(ci), hw // gw + 1)),
        dtcol=pl.BlockSpec((N_HEADS, SSD_CHUNK, 1), lambda ci: (0, order(ci), 0)),
        dtrow=pl.BlockSpec((N_HEADS, 1, SSD_CHUNK), lambda ci: (0, 0, order(ci))),
        scal=pl.BlockSpec((N_HEADS, 1, 1), lambda ci: (0, 0, 0)),
        state=pl.BlockSpec((N_HEADS, 1, LANES, LANES), lambda ci: (0, order(ci), 0, 0)),
        group=pl.BlockSpec((SSD_CHUNK, gw), lambda ci: (order(ci), 0)),
        pacc=pl.BlockSpec((N_HEADS, 1, LANES), lambda ci: (0, 0, 0)),
    )


def _ssd_chunk_terms(dtcol, dtrow, bias, a_log):
    shp = (SSD_CHUNK, SSD_CHUNK)
    row = lax.broadcasted_iota(jnp.int32, shp, 0)
    col = lax.broadcasted_iota(jnp.int32, shp, 1)
    a_head = -jnp.exp(a_log)
    dt_c = jnp.broadcast_to(_softplus(dtcol + bias), shp)
    dt_r = jnp.broadcast_to(_softplus(dtrow + bias), shp)
    cs_c = _cumsum_rows(dt_c * a_head, row)
    cs_r = _cumsum_lanes(dt_r * a_head, col)
    cs_last = jnp.sum(jnp.where(row == SSD_CHUNK - 1, cs_c, 0.0), axis=0, keepdims=True)
    return row, col, a_head, dt_c, cs_c, cs_r, cs_last


def _ssd_fwd(xbc, dtcol, dtrow, bias, a_log, dskip, *, name):
    s = xbc.shape[0]
    nc = s // SSD_CHUNK

    def body(x_ref, b_ref, c_ref, dtc_ref, dtr_ref, bias_ref, alog_ref, d_ref, y_ref, st_ref, state):
        ci = pl.program_id(0)

        @pl.when(ci == 0)
        def _():
            state[...] = jnp.zeros_like(state)

        for gi in range(SSD_GROUPS):
            glanes = slice(gi * LANES, (gi + 1) * LANES)
            bm, cm = b_ref[:, glanes], c_ref[:, glanes]
            cb = _dot(cm, bm, tb=True)
            bm_t = bm.T
            for r in range(gi * GROUP_HEADS, (gi + 1) * GROUP_HEADS):
                lanes = slice(r * LANES, (r + 1) * LANES)
                xv = x_ref[:, lanes]
                row, col, _, dt_c, cs_c, cs_r, cs_last = _ssd_chunk_terms(dtc_ref[r], dtr_ref[r], bias_ref[r], alog_ref[r])
                g = cb * jnp.exp(jnp.where(col <= row, cs_c - cs_r, -jnp.inf))
                xdt = xv * dt_c
                st = state[r]
                st_ref[r, 0] = st
                y_ref[:, lanes] = _dot(g, xdt) + _dot(cm, st) * jnp.exp(cs_c) + xv * d_ref[r]
                state[r] = jnp.exp(cs_last) * st + _dot(bm_t, xdt * jnp.exp(cs_last - cs_c))

    sp = _ssd_specs(nc, lambda ci: ci)
    return pl.pallas_call(
        body, name=name, grid=(nc,),
        in_specs=[sp["x"], sp["b"], sp["c"], sp["dtcol"], sp["dtrow"], sp["scal"], sp["scal"], sp["scal"]],
        out_specs=[sp["x"], sp["state"]],
        out_shape=[jax.ShapeDtypeStruct((s, N_HEADS * LANES), F32),
                   jax.ShapeDtypeStruct((N_HEADS, nc, LANES, LANES), F32)],
        scratch_shapes=[pltpu.VMEM((N_HEADS, LANES, LANES), F32)],
        compiler_params=_cparams(("arbitrary",)),
    )(xbc, xbc, xbc, dtcol, dtrow, bias, a_log, dskip)


def _ssd_bwd(xbc, dtcol, dtrow, bias, a_log, dskip, states, dy, *, name):
    s = xbc.shape[0]
    nc = s // SSD_CHUNK

    def body(x_ref, b_ref, c_ref, dtc_ref, dtr_ref, bias_ref, alog_ref, d_ref, st_ref, dy_ref,
             dx_ref, db_ref, dc_ref, ddt_ref, dbias_ref, dalog_ref, dd_ref, dstate):
        ci = pl.program_id(0)

        @pl.when(ci == 0)
        def _():
            dstate[...] = jnp.zeros_like(dstate)
            dbias_ref[...] = jnp.zeros_like(dbias_ref)
            dalog_ref[...] = jnp.zeros_like(dalog_ref)
            dd_ref[...] = jnp.zeros_like(dd_ref)

        rowsum = lambda v: jnp.sum(v, axis=1, keepdims=True)
        tot = lambda v: jnp.broadcast_to(jnp.sum(v, axis=0, keepdims=True), (1, LANES))
        for gi in range(SSD_GROUPS):
            glanes = slice(gi * LANES, (gi + 1) * LANES)
            bm, cm = b_ref[:, glanes], c_ref[:, glanes]
            cb = _dot(cm, bm, tb=True)
            cb_t = _dot(bm, cm, tb=True)
            cm_t = cm.T
            dbm_sum, dcm_sum = None, None
            for r in range(gi * GROUP_HEADS, (gi + 1) * GROUP_HEADS):
                lanes = slice(r * LANES, (r + 1) * LANES)
                xv, dyv, st = x_ref[:, lanes], dy_ref[:, lanes], st_ref[r, 0]
                dtraw_c, bias = dtc_ref[r], bias_ref[r]
                row, col, a_head, dt_c, cs_c, cs_r, cs_last = _ssd_chunk_terms(dtraw_c, dtr_ref[r], bias, alog_ref[r])
                lmat = jnp.exp(jnp.where(col <= row, cs_c - cs_r, -jnp.inf))
                lmat_t = jnp.exp(jnp.where(row <= col, cs_r - cs_c, -jnp.inf))
                g, g_t = cb * lmat, cb_t * lmat_t
                xdt = xv * dt_c
                e_c = jnp.exp(cs_c)
                f_c = jnp.exp(cs_last - cs_c)
                e_last = jnp.exp(cs_last)
                w = xdt * f_c
                dst = dstate[r]

                dg = _dot(dyv, xdt, tb=True)
                dg_t = _dot(xdt, dyv, tb=True)
                dxdt = _dot(g_t, dyv)
                dcs = rowsum(dg * g) - rowsum(dg_t * g_t)
                dcm = _dot(dg * lmat, bm)
                dbm = _dot(dg_t * lmat_t, cm)
                z = _dot(cm, st)
                dz = dyv * e_c
                dcs = dcs + rowsum(dz * z)
                dcm = dcm + _dot(dz, st, tb=True)
                dstate[r] = _dot(cm_t, dz) + e_last * dst
                dcs_last = jnp.sum(rowsum(dst * st), axis=0, keepdims=True) * jnp.max(e_last, axis=1, keepdims=True)
                dbm = dbm + _dot(w, dst, tb=True)
                dw = _dot(bm, dst)
                dxdt = dxdt + dw * f_c
                q = rowsum(dw * w)
                dcs = dcs - q
                dcs_last = dcs_last + jnp.sum(q, axis=0, keepdims=True)
                dx_ref[:, lanes] = dxdt * dt_c + dyv * d_ref[r]
                ddt = rowsum(dxdt * xv)
                dcs_full = jnp.broadcast_to(dcs, (SSD_CHUNK, SSD_CHUNK)) + jnp.where(row == SSD_CHUNK - 1, dcs_last, 0.0)
                da = jnp.max(_rev_cumsum_rows(dcs_full, row), axis=1, keepdims=True)
                dt_col = jnp.max(dt_c, axis=1, keepdims=True)
                draw = (ddt + da * a_head) * _sigmoid(dtraw_c + bias)
                ddt_ref[r] = draw
                dbias_ref[r] += tot(draw)
                dalog_ref[r] += tot(da * dt_col) * a_head
                dd_ref[r] += tot(rowsum(dyv * xv))
                dbm_sum = dbm if dbm_sum is None else dbm_sum + dbm
                dcm_sum = dcm if dcm_sum is None else dcm_sum + dcm
            db_ref[:, glanes] = dbm_sum
            dc_ref[:, glanes] = dcm_sum

    sp = _ssd_specs(nc, lambda ci: nc - 1 - ci)
    return pl.pallas_call(
        body, name=name, grid=(nc,),
        in_specs=[sp["x"], sp["b"], sp["c"], sp["dtcol"], sp["dtrow"], sp["scal"], sp["scal"], sp["scal"],
                  sp["state"], sp["x"]],
        out_specs=[sp["x"], sp["group"], sp["group"], sp["dtcol"], sp["pacc"], sp["pacc"], sp["pacc"]],
        out_shape=[jax.ShapeDtypeStruct((s, N_HEADS * LANES), F32),
                   jax.ShapeDtypeStruct((s, 2 * LANES), F32),
                   jax.ShapeDtypeStruct((s, 2 * LANES), F32),
                   jax.ShapeDtypeStruct((N_HEADS, s, 1), F32),
                   jax.ShapeDtypeStruct((N_HEADS, 1, LANES), F32),
                   jax.ShapeDtypeStruct((N_HEADS, 1, LANES), F32),
                   jax.ShapeDtypeStruct((N_HEADS, 1, LANES), F32)],
        scratch_shapes=[pltpu.VMEM((N_HEADS, LANES, LANES), F32)],
        compiler_params=_cparams(("arbitrary",)),
    )(xbc, xbc, xbc, dtcol, dtrow, bias, a_log, dskip, states, dy)


def _att_tile(s):
    return _pick(s, (512, 256, 128))


def _tri(t, transposed=False):
    r = lax.broadcasted_iota(jnp.int32, (t, t), 0)
    c = lax.broadcasted_iota(jnp.int32, (t, t), 1)
    return (r <= c) if transposed else (c <= r)


def _rows_at(ref, blk, t):
    return ref[pl.ds(pl.multiple_of(blk * t, t), t), :]


def _flash_fwd(q, k, v, *, name):
    s = q.shape[0]
    t = _att_tile(s)
    nq = s // t

    def body(q_ref, k_ref, v_ref, o_ref, lse_ref):
        i = pl.program_id(1)
        qv = q_ref[...]

        def step(j, carry, diagonal):
            m_old, l_old, acc = carry
            sc = _dot(qv, _rows_at(k_ref, j, t), tb=True)
            if diagonal:
                sc = jnp.where(_tri(t), sc, -jnp.inf)
            m_new = jnp.maximum(m_old, jnp.max(sc, axis=1, keepdims=True))
            alpha = jnp.exp(m_old - m_new)
            p = jnp.exp(sc - m_new)
            return (m_new, alpha * l_old + jnp.sum(p, axis=1, keepdims=True),
                    alpha * acc + _dot(p, _rows_at(v_ref, j, t)))

        init = (jnp.full((t, 1), -jnp.inf, F32), jnp.zeros((t, 1), F32), jnp.zeros((t, LANES), F32))
        carry = lax.fori_loop(0, i, lambda j, c: step(j, c, False), init)
        m_fin, l_fin, acc = step(i, carry, True)
        o_ref[...] = (acc / l_fin).astype(o_ref.dtype)
        lse_ref[0] = m_fin + jnp.log(l_fin)

    q_spec = pl.BlockSpec((t, LANES), lambda h, i: (i, h))
    kv_spec = pl.BlockSpec((s, LANES), lambda h, i: (0, h))
    return pl.pallas_call(
        body, name=name, grid=(N_HEADS, nq),
        in_specs=[q_spec, kv_spec, kv_spec],
        out_specs=[q_spec, pl.BlockSpec((1, t, 1), lambda h, i: (h, i, 0))],
        out_shape=[jax.ShapeDtypeStruct(q.shape, BF16), jax.ShapeDtypeStruct((N_HEADS, s, 1), F32)],
        compiler_params=_cparams(("parallel", "arbitrary")),
    )(q, k, v)


def _att_delta(o, do, *, name):
    s = o.shape[0]
    t = _att_tile(s)

    def body(o_ref, do_ref, dl_ref):
        dl_ref[0] = jnp.sum(do_ref[...].astype(F32) * o_ref[...].astype(F32), axis=1, keepdims=True)

    blk = pl.BlockSpec((t, LANES), lambda h, i: (i, h))
    return pl.pallas_call(
        body, name=name, grid=(N_HEADS, s // t), in_specs=[blk, blk],
        out_specs=pl.BlockSpec((1, t, 1), lambda h, i: (h, i, 0)),
        out_shape=jax.ShapeDtypeStruct((N_HEADS, s, 1), F32),
        compiler_params=_cparams(("parallel", "parallel")),
    )(o, do)


def _flash_bwd(q, k, v, do, lse_row, delta_row, *, name):
    s = q.shape[0]
    t = _att_tile(s)
    nq = s // t

    def body(q_ref, k_ref, v_ref, do_ref, lse_ref, dl_ref, dq_ref, dk_ref, dv_ref):
        j = pl.program_id(1)
        kv, vv = k_ref[...], v_ref[...]

        @pl.when(j == 0)
        def _():
            dq_ref[...] = jnp.zeros_like(dq_ref)

        def step(i, carry, diagonal):
            dk, dv = carry
            rows = pl.ds(pl.multiple_of(i * t, t), t)
            qi, doi = q_ref[rows, :], do_ref[rows, :]
            p_t = jnp.exp(_dot(kv, qi, tb=True) - lse_ref[0, :, rows])
            if diagonal:
                p_t = jnp.where(_tri(t, transposed=True), p_t, 0.0)
            ds_t = (p_t * (_dot(vv, doi, tb=True) - dl_ref[0, :, rows])).astype(MXU_DTYPE)
            dq_ref[rows, :] += _dot(ds_t, kv, ta=True)
            return dk + _dot(ds_t, qi), dv + _dot(p_t, doi)

        zero = jnp.zeros((t, LANES), F32)
        carry = step(j, (zero, zero), True)
        dk, dv = lax.fori_loop(j + 1, nq, lambda i, c: step(i, c, False), carry)
        dk_ref[...] = dk
        dv_ref[...] = dv

        @pl.when(j == nq - 1)
        def _():
            dq_ref[...] = dq_ref[...] * ATT_SCALE

    q_spec = pl.BlockSpec((s, LANES), lambda h, j: (0, h))
    kv_spec = pl.BlockSpec((t, LANES), lambda h, j: (j, h))
    row_spec = pl.BlockSpec((1, 1, s), lambda h, j: (h, 0, 0))
    return pl.pallas_call(
        body, name=name, grid=(N_HEADS, nq),
        in_specs=[q_spec, kv_spec, kv_spec, q_spec, row_spec, row_spec],
        out_specs=[q_spec, kv_spec, kv_spec],
        out_shape=[jax.ShapeDtypeStruct(q.shape, F32)] * 3,
        compiler_params=_cparams(("parallel", "arbitrary")),
    )(q, k, v, do, lse_row, delta_row)


def _rope(v, cos_t, sin_p, sin_m):
    return v * cos_t + pltpu.roll(v, QK_ROPE // 2, 1) * sin_p + pltpu.roll(v, LANES - QK_ROPE // 2, 1) * sin_m


def _rope_t(d, cos_t, sin_p, sin_m):
    return d * cos_t + pltpu.roll(d * sin_p, LANES - QK_ROPE // 2, 1) + pltpu.roll(d * sin_m, QK_ROPE // 2, 1)


def _att_prep(q_pad, kv2, kr, cos_t, sin_p, sin_m, *, name):
    w = N_HEADS * LANES

    def fn(qv, kvv, krv, c, sp, sm):
        kr_rot = _rope(krv, c, sp, sm)
        qs, ks = [], []
        for h in range(N_HEADS):
            blk = slice(h * LANES, (h + 1) * LANES)
            qs.append(_rope(qv[:, blk], c, sp, sm) * ATT_SCALE)
            ks.append(kvv[:, blk] + kr_rot)
        return jnp.concatenate(qs, axis=1), jnp.concatenate(ks, axis=1), kvv[:, w:]

    return _rowwise(fn, [q_pad, kv2, kr, cos_t, sin_p, sin_m], [],
                    [(w, BF16, "row"), (w, BF16, "row"), (w, BF16, "row")], name=name)


def _att_prep_bwd(dq, dk, cos_t, sin_p, sin_m, *, name):
    w = N_HEADS * LANES

    def fn(dqv, dkv, c, sp, sm):
        outs, dkr = [], None
        for h in range(N_HEADS):
            blk = slice(h * LANES, (h + 1) * LANES)
            outs.append(_rope_t(dqv[:, blk], c, sp, sm))
            dkr = dkv[:, blk] if dkr is None else dkr + dkv[:, blk]
        return jnp.concatenate(outs, axis=1), _rope_t(dkr, c, sp, sm)

    return _rowwise(fn, [dq, dk, cos_t, sin_p, sin_m], [], [(w, BF16, "row"), (LANES, F32, "row")], name=name)


_ANY = pl.BlockSpec(memory_space=pl.ANY)
_MESH = pl.DeviceIdType.MESH


def _mesh_pos():
    return lax.axis_index("x"), lax.axis_index("y"), lax.axis_index("c")


def _remote(src, dst, send_sem, recv_sem, dev):
    return pltpu.make_async_remote_copy(src_ref=src, dst_ref=dst, send_sem=send_sem, recv_sem=recv_sem,
                                        device_id=dev, device_id_type=_MESH)


def _other_chips(x, y):
    chips = [(1 - x, y), (x, 1 - y), (1 - x, 1 - y)]
    return chips, [2 * cx + cy for cx, cy in chips]


def _comm_call(body, ins, out_shapes, n_sems, *, name):
    return pl.pallas_call(
        body, name=name, in_specs=[_ANY] * len(ins), out_specs=[_ANY] * len(out_shapes), out_shape=out_shapes,
        scratch_shapes=[pltpu.SemaphoreType.DMA((k,)) for k in n_sems],
    )(*ins)


def _gather_halves(shards):
    n = len(shards)
    halves = [t.shape[0] // 2 for t in shards]

    def body(*refs):
        xs, outs = refs[:n], refs[n:2 * n]
        send_sems, recv_sems = refs[2 * n:]
        x, y, c = _mesh_pos()
        k = 2 * x + y
        sibling = (x, y, 1 - c)
        chips, ks = _other_chips(x, y)
        half = lambda w, hf: pl.ds(hf * halves[w], halves[w])
        first = [_remote(xs[w].at[half(w, c)], outs[w].at[k, half(w, c)], send_sems.at[6 * w + j], recv_sems.at[6 * w + j],
                         (*chips[j], c)) for w in range(n) for j in range(3)]
        for cp in first:
            cp.start()
        passed = []
        for j in range(3):
            for w in range(n):
                land = outs[w].at[ks[j], half(w, c)]
                _remote(land, land, send_sems.at[6 * w + j], recv_sems.at[6 * w + j], sibling).wait_recv()
                passed.append(_remote(land, land, send_sems.at[6 * w + 3 + j], recv_sems.at[6 * w + 3 + j], sibling))
                passed[-1].start()
        for j in range(3):
            for w in range(n):
                land = outs[w].at[ks[j], half(w, 1 - c)]
                _remote(land, land, send_sems.at[6 * w + 3 + j], recv_sems.at[6 * w + 3 + j], sibling).wait_recv()
        for cp in first + passed:
            cp.wait_send()

    shapes = [jax.ShapeDtypeStruct((4,) + t.shape, t.dtype) for t in shards]
    return _comm_call(body, shards, shapes, (6 * n, 6 * n), name="gather_halves")


_HBM = pl.BlockSpec(memory_space=pltpu.HBM)
_SEM = pl.BlockSpec(memory_space=pltpu.SEMAPHORE)
_EFFECT = pltpu.SideEffectType.DATAFLOW_SIDE_EFFECTING


def _push_start(blocks, *, scatter, name):
    n = len(blocks)

    def body(*refs):
        xs, lands = refs[:n], refs[n:2 * n]
        send_sems, recv_sems = refs[2 * n], refs[2 * n + 1]
        token = refs[-1]
        x, y, c = _mesh_pos()
        k = 2 * x + y
        chips, ks = _other_chips(x, y)
        for w in range(n):
            for j in range(3):
                src = xs[w].at[ks[j]] if scatter else xs[w]
                _remote(src, lands[w].at[k], send_sems.at[3 * w + j], recv_sems.at[3 * w + j], (*chips[j], c)).start()
        token[...] = jnp.zeros_like(token)

    hbm = lambda shape, dtype: pltpu.with_memory_space_constraint(lax.empty(shape, dtype), pltpu.HBM)
    ins = [pltpu.with_memory_space_constraint(t, pltpu.HBM) for t in blocks]
    ins += [hbm(t.shape if scatter else (4,) + t.shape, t.dtype) for t in blocks]
    out_shape = [pltpu.SemaphoreType.DMA((3 * n,)), pltpu.SemaphoreType.DMA((3 * n,))]
    out_shape += [pltpu.HBM(t.shape, t.dtype) for t in ins]
    out_shape += [jax.ShapeDtypeStruct((8, LANES), F32)]
    res = pl.pallas_call(
        body, name=name, out_shape=out_shape, in_specs=[_HBM] * (2 * n),
        out_specs=[_SEM, _SEM] + [_HBM] * (2 * n) + [pl.BlockSpec(memory_space=pltpu.VMEM)],
        input_output_aliases={i: 2 + i for i in range(2 * n)},
        compiler_params=pltpu.CompilerParams(has_side_effects=_EFFECT),
    )(*ins)
    return res[0], res[1], res[2:2 + n], res[2 + n:2 + 2 * n], res[-1]


def _push_wait(send_sems, recv_sems, blocks, lands, after, *, name):
    n = len(blocks)

    def body(*refs):
        lands_in = refs[n:2 * n]
        send_sems, recv_sems = refs[2 * n], refs[2 * n + 1]
        x, y, c = _mesh_pos()
        chips, ks = _other_chips(x, y)
        for w in range(n):
            for j in range(3):
                slot = lands_in[w].at[ks[j]]
                cp = _remote(slot, slot, send_sems.at[3 * w + j], recv_sems.at[3 * w + j], (*chips[j], c))
                cp.wait_send()
                cp.wait_recv()

    out_shape = [pltpu.HBM(t.shape, t.dtype) for t in list(blocks) + list(lands)]
    res = pl.pallas_call(
        body, name=name, out_shape=out_shape,
        in_specs=[_HBM] * (2 * n) + [_SEM, _SEM, pl.BlockSpec(memory_space=pl.ANY)], out_specs=[_HBM] * (2 * n),
        input_output_aliases={i: i for i in range(2 * n)},
        compiler_params=pltpu.CompilerParams(has_side_effects=_EFFECT),
    )(*blocks, *lands, send_sems, recv_sems, after)
    return res[:n], res[n:]


def _send_half(views, *, name):
    n = len(views)

    def body(*refs):
        vs, outs = refs[:n], refs[n:2 * n]
        send_sems, recv_sems = refs[2 * n:]
        x, y, c = _mesh_pos()
        cps = []
        for w in range(n):
            h = views[w].shape[1] // 2
            cps.append(_remote(vs[w].at[:, pl.ds((1 - c) * h, h), :], outs[w], send_sems.at[w], recv_sems.at[w],
                               (x, y, 1 - c)))
            cps[-1].start()
        for cp in cps:
            cp.wait()

    shapes = [jax.ShapeDtypeStruct((t.shape[0], t.shape[1] // 2, t.shape[2]), t.dtype) for t in views]
    return _comm_call(body, views, shapes, (n, n), name=name)


def _swap_with_sibling(mine, *, name):
    n = len(mine)

    def body(*refs):
        hs, outs = refs[:n], refs[n:2 * n]
        send_sems, recv_sems = refs[2 * n:]
        x, y, c = _mesh_pos()
        cps = [_remote(hs[w], outs[w], send_sems.at[w], recv_sems.at[w], (x, y, 1 - c)) for w in range(n)]
        for cp in cps:
            cp.start()
        for cp in cps:
            cp.wait()

    shapes = [jax.ShapeDtypeStruct(t.shape, t.dtype) for t in mine]
    return _comm_call(body, mine, shapes, (n, n), name=name)


def _gather_all(vec, *, name):
    r, w = vec.shape

    def body(v_ref, out_ref, send_sems, recv_sems):
        x, y, c = _mesh_pos()

        def slot(px, py, pc):
            return out_ref.at[4 * px + 2 * py + pc]

        peers = []
        for rel in range(1, 8):
            fx, fy, fc = (rel >> 2) & 1, (rel >> 1) & 1, rel & 1
            peers.append((x ^ fx, y ^ fy, c ^ fc))
        cps = [_remote(v_ref, slot(x, y, c), send_sems.at[j], recv_sems.at[j], peer) for j, peer in enumerate(peers)]
        for cp in cps:
            cp.start()
        for j, peer in enumerate(peers):
            _remote(slot(*peer), slot(*peer), send_sems.at[j], recv_sems.at[j], peer).wait_recv()
        for cp in cps:
            cp.wait_send()

    others = pl.pallas_call(
        body, name=name, in_specs=[_ANY], out_specs=_ANY,
        out_shape=jax.ShapeDtypeStruct((8, r, w), vec.dtype),
        scratch_shapes=[pltpu.SemaphoreType.DMA((7,)), pltpu.SemaphoreType.DMA((7,))],
    )(vec)
    me = 4 * lax.axis_index("x") + 2 * lax.axis_index("y") + lax.axis_index("c")
    return lax.dynamic_update_index_in_dim(others, vec, me, 0)


def _gather_same_core(vec, *, name):
    r, w = vec.shape

    def body(v_ref, out_ref, send_sems, recv_sems):
        x, y, c = _mesh_pos()
        k = 2 * x + y
        chips, ks = _other_chips(x, y)
        cps = [_remote(v_ref, out_ref.at[k], send_sems.at[j], recv_sems.at[j], (*chips[j], c)) for j in range(3)]
        for cp in cps:
            cp.start()
        for j in range(3):
            slot = out_ref.at[ks[j]]
            _remote(slot, slot, send_sems.at[j], recv_sems.at[j], (*chips[j], c)).wait_recv()
        for cp in cps:
            cp.wait_send()

    others = pl.pallas_call(
        body, name=name, in_specs=[_ANY], out_specs=_ANY,
        out_shape=jax.ShapeDtypeStruct((4, r, w), vec.dtype),
        scratch_shapes=[pltpu.SemaphoreType.DMA((3,)), pltpu.SemaphoreType.DMA((3,))],
    )(vec)
    k_chip = 2 * lax.axis_index("x") + lax.axis_index("y")
    return lax.dynamic_update_index_in_dim(others, vec, k_chip, 0)


def _row_tile(rows, row_bytes):
    for tm in (1024, 512, 256, 128, 64, 32, 16):
        if rows % tm == 0 and tm * row_bytes <= ELEMENTWISE_BLOCK_BYTES:
            return tm
    return 16 if rows % 16 == 0 else rows


def _chip_sum_half(g, got, c, *, name):
    nb, r, w = g.shape
    half = r // 2
    tm = _row_tile(half, w * 4)
    per = half // tm

    def body(c_ref, g_ref, o_ref, out_ref):
        out_ref[...] = (g_ref[...] + o_ref[...]).astype(out_ref.dtype)

    return pl.pallas_call(
        body, name=name,
        grid_spec=pltpu.PrefetchScalarGridSpec(
            num_scalar_prefetch=1, grid=(nb, per),
            in_specs=[pl.BlockSpec((1, tm, w), lambda b, i, c_ref: (b, c_ref[0] * per + i, 0)),
                      pl.BlockSpec((1, tm, w), lambda b, i, c_ref: (b, i, 0))],
            out_specs=pl.BlockSpec((1, tm, w), lambda b, i, c_ref: (b, i, 0))),
        out_shape=jax.ShapeDtypeStruct((nb, half, w), BF16),
        compiler_params=_cparams(("parallel", "parallel")),
    )(jnp.reshape(c, (1,)).astype(jnp.int32), g, got)


def _sum_slots(stack, *, name):
    n, r, w = stack.shape
    tm = _row_tile(r, n * w * stack.dtype.itemsize)

    def body(s_ref, out_ref):
        acc = s_ref[0].astype(F32)
        for i in range(1, n):
            acc = acc + s_ref[i].astype(F32)
        out_ref[...] = acc

    return pl.pallas_call(
        body, name=name, grid=(r // tm,),
        in_specs=[pl.BlockSpec((n, tm, w), lambda i: (0, i, 0))],
        out_specs=pl.BlockSpec((tm, w), lambda i: (i, 0)),
        out_shape=jax.ShapeDtypeStruct((r, w), F32),
        compiler_params=_cparams(("parallel",)),
    )(stack)


def _adam_math(wv, gv, mv, vv):
    m_new = ADAM_B1 * mv + (1.0 - ADAM_B1) * gv
    v_new = ADAM_B2 * vv + (1.0 - ADAM_B2) * (gv * gv)
    m_hat = m_new / (1.0 - ADAM_B1 ** ADAM_STEP)
    v_hat = v_new / (1.0 - ADAM_B2 ** ADAM_STEP)
    delta = -ADAM_LR * (m_hat / (jnp.sqrt(v_hat) + ADAM_EPS) + ADAM_WD * wv)
    return delta, m_new, v_new


def _adamw(w, g, m, v, *, name):
    shape = w.shape
    cols = shape[-1]
    flat = lambda t: t.reshape(-1, cols)
    rows = flat(w).shape[0]
    tm = _pick(rows, (256, 128, 64, 32, 16, 8))
    outs = _rowwise(_adam_math, [flat(w), flat(g), flat(m), flat(v)], [], [(cols, F32, "row")] * 3, name=name, tm=tm)
    return tuple(o.reshape(shape) for o in outs)


def _adamw_slots(w, slots, m, v, c, *, name):
    shape = w.shape
    cols = shape[-1]
    half = slots[0][0].shape[1]
    v4 = lambda t: t.reshape(2, 2, half, cols)
    assert all(s.shape == (4, half, cols) for pair in slots for s in pair) and w.size == 4 * half * cols
    tm = _row_tile(half, cols * 4 * 4)

    def body(c_ref, w_ref, m0_ref, o0_ref, m1_ref, o1_ref, m_ref, v_ref, g_ref, d_ref, mo_ref, vo_ref):
        first = pl.program_id(0) == 0
        own = pl.program_id(1) == c_ref[0]
        g = None
        for i in range(4):
            part = jnp.where(first, jnp.where(own, m0_ref[i], o0_ref[i]), jnp.where(own, m1_ref[i], o1_ref[i]))
            g = part.astype(F32) if g is None else g + part.astype(F32)
        delta, m_new, v_new = _adam_math(w_ref[0, 0], g, m_ref[0, 0], v_ref[0, 0])
        g_ref[0, 0], d_ref[0, 0], mo_ref[0, 0], vo_ref[0, 0] = g, delta, m_new, v_new

    blk = pl.BlockSpec((1, 1, tm, cols), lambda l, hf, i, c_ref: (l, hf, i, 0))

    def slot_spec(layer, mine):
        def index(l, hf, i, c_ref):
            same_half = hf * c_ref[0] + (1 - hf) * (1 - c_ref[0])
            use = (l if layer else 1 - l) * (same_half if mine else 1 - same_half)
            return (0, i * use, 0)
        return pl.BlockSpec((4, tm, cols), index)

    outs = pl.pallas_call(
        body, name=name,
        grid_spec=pltpu.PrefetchScalarGridSpec(
            num_scalar_prefetch=1, grid=(2, 2, half // tm),
            in_specs=[blk, slot_spec(0, True), slot_spec(0, False), slot_spec(1, True), slot_spec(1, False), blk, blk],
            out_specs=[blk] * 4),
        out_shape=[jax.ShapeDtypeStruct((2, 2, half, cols), F32)] * 4,
        compiler_params=_cparams(("arbitrary", "arbitrary", "arbitrary")),
    )(jnp.reshape(c, (1,)).astype(jnp.int32), v4(w), slots[0][0], slots[0][1], slots[1][0], slots[1][1], v4(m), v4(v))
    return tuple(o.reshape(shape) for o in outs)


def _pad_blocks(w, axis, n_blocks, real, to=LANES, offset=0):
    axis = axis % w.ndim
    shp = w.shape
    w = w.reshape(shp[:axis] + (n_blocks, real) + shp[axis + 1:])
    pads = [(0, 0)] * w.ndim
    pads[axis + 1] = (offset, to - real - offset)
    w = jnp.pad(w, pads)
    return w.reshape(shp[:axis] + (n_blocks * to,) + shp[axis + 1:])


def _unpad_blocks(w, axis, n_blocks, real, to=LANES, offset=0):
    axis = axis % w.ndim
    shp = w.shape
    w = w.reshape(shp[:axis] + (n_blocks, to) + shp[axis + 1:])
    w = lax.slice_in_dim(w, offset, offset + real, axis=axis + 1)
    return w.reshape(shp[:axis] + (n_blocks * real,) + shp[axis + 1:])


def _block_diag(w):
    n, a, b = w.shape
    eye = jnp.eye(n, dtype=w.dtype)
    return (eye[:, None, :, None] * w[:, :, None, :]).reshape(n * a, n * b)


def _block_diag_t(d, n):
    a, b = d.shape[0] // n, d.shape[1] // n
    d = d.reshape(n, a, n, b)
    return jnp.stack([d[i, :, i, :] for i in range(n)])


_SPLITS = np.cumsum((0,) + SPLIT_SIZES)


def _w_in_groups(w_in):
    sl = lambda i: w_in[:, _SPLITS[i]:_SPLITS[i + 1]]
    xbc = sl(5)
    xbc_pad = jnp.concatenate([_pad_blocks(xbc[:, :MIX], 1, N_HEADS, HEAD),
                               _pad_blocks(xbc[:, MIX:MIX + 2 * HEAD], 1, 2, HEAD),
                               _pad_blocks(xbc[:, MIX + 2 * HEAD:], 1, 2, HEAD)], axis=1)
    return dict(
        cq=sl(0), ckv=sl(1), kr=_pad_blocks(sl(2), 1, 1, QK_ROPE, offset=HEAD), pool=sl(3),
        z=_pad_blocks(sl(4), 1, N_HEADS, HEAD), xbc=xbc_pad, dt=_pad_blocks(sl(6), 1, 1, N_HEADS),
        lru_g=sl(7), lru_x=sl(8), gates=sl(9))


def _w_in_fused(groups):
    parts, at = [], 0
    for name, off, width in IN_LAYOUT:
        assert groups[name].shape[1] == width and off >= at
        if off > at:
            parts.append(jnp.zeros((groups[name].shape[0], off - at), groups[name].dtype))
        parts.append(groups[name])
        at = off + width
    parts.append(jnp.zeros((parts[0].shape[0], IN_ALL_COLS - at), parts[0].dtype))
    return jnp.concatenate(parts, axis=1)


def _in_cols(arr, name):
    off, width = IN_OFFSETS[name]
    return _Cols(arr, off, width)


def _w_in_ungroup(d):
    xbc = d["xbc"]
    w = N_HEADS * LANES
    xbc_real = jnp.concatenate([_unpad_blocks(xbc[:, :w], 1, N_HEADS, HEAD),
                                _unpad_blocks(xbc[:, w:w + 2 * LANES], 1, 2, HEAD),
                                _unpad_blocks(xbc[:, w + 2 * LANES:], 1, 2, HEAD)], axis=1)
    return jnp.concatenate([d["cq"], d["ckv"], _unpad_blocks(d["kr"], 1, 1, QK_ROPE, offset=HEAD), d["pool"],
                            _unpad_blocks(d["z"], 1, N_HEADS, HEAD), xbc_real, _unpad_blocks(d["dt"], 1, 1, N_HEADS),
                            d["lru_g"], d["lru_x"], d["gates"]], axis=1)


def _pad_xbc_vec(v):
    return jnp.concatenate([_pad_blocks(v[..., :MIX], -1, N_HEADS, HEAD),
                            _pad_blocks(v[..., MIX:MIX + 2 * HEAD], -1, 2, HEAD),
                            _pad_blocks(v[..., MIX + 2 * HEAD:], -1, 2, HEAD)], axis=-1)


def _unpad_xbc_vec(v):
    w = N_HEADS * LANES
    return jnp.concatenate([_unpad_blocks(v[..., :w], -1, N_HEADS, HEAD),
                            _unpad_blocks(v[..., w:w + 2 * LANES], -1, 2, HEAD),
                            _unpad_blocks(v[..., w + 2 * LANES:], -1, 2, HEAD)], axis=-1)


def _layer_weights(p):
    q = dict(p)
    q["in_all"] = _w_in_fused(_w_in_groups(p["w_in"]))
    q["uq"] = _pad_blocks(p["w_uq"], 1, N_HEADS, HEAD + QK_ROPE)
    ukv = p["w_ukv"].reshape(KV_LORA, N_HEADS, 2 * HEAD)
    q["ukv"] = jnp.concatenate([_pad_blocks(ukv[:, :, :HEAD].reshape(KV_LORA, -1), 1, N_HEADS, HEAD),
                                _pad_blocks(ukv[:, :, HEAD:].reshape(KV_LORA, -1), 1, N_HEADS, HEAD)], axis=1)
    q["pool_bd"] = _block_diag(p["w_pool"])
    q["lru_bd"] = jnp.concatenate([_block_diag(p["lru_w_a"]), _block_diag(p["lru_w_i"])], axis=1)
    q["br"] = [_pad_blocks(p["w_branch"][0], 0, N_HEADS, HEAD), p["w_branch"][1],
               _pad_blocks(p["w_branch"][2], 0, N_HEADS, HEAD), p["w_branch"][3]]
    q["ssd_conv_w_pad"] = _pad_xbc_vec(p["ssd_conv_w"])
    q["ssd_conv_b_pad"] = _pad_xbc_vec(p["ssd_conv_b"])[None, :]
    q["ssd_norm_pad"] = _pad_blocks(p["ssd_norm"], 0, N_HEADS, HEAD)[None, :]
    return q


def _row(v):
    return v.reshape(1, -1)


def _scal3(v):
    return v.reshape(N_HEADS, 1, 1)


def _layer_fwd(x, p_emb, w, rope, tag):
    n = lambda s: f"{s}_{tag}"
    sv = {"x": x}
    h = _rms_fwd(x, _row(w["g_mix"]), name=n("rms_mix"))
    sv["h"] = h
    u_all = _mm(h, w["in_all"], name=n("in_proj"))
    u = {k: _in_cols(u_all, k) for k in IN_OFFSETS}
    sv["u"] = u

    cqn = _rms_fwd(u["cq"], _row(w["q_norm"]), name=n("rms_q"))
    ckvn = _rms_fwd(u["ckv"], _row(w["kv_norm"]), name=n("rms_kv"))
    q_pad = _mm(cqn, w["uq"], name=n("uq"))
    kv2 = _mm(ckvn, w["ukv"], name=n("ukv"))
    qc, kc, vc = _att_prep(q_pad, kv2, u["kr"], *rope, name=n("att_prep"))
    y_a, lse = _flash_fwd(qc, kc, vc, name=n("flash_fwd"))
    sv.update(cqn=cqn, ckvn=ckvn, qc=qc, kc=kc, vc=vc, y_a=y_a, lse=lse)

    pool_d = _pool_fwd(u["pool"], name=n("pool_fwd"))
    yb_pre, y_b = _mm(pool_d, w["pool_bd"], epilogue=lambda acc, sc: (acc, acc * sc),
                      rowvecs=[_row(w["pool_scale"])], out_dtypes=(F32, BF16), name=n("pool_mm"))
    sv.update(pool_d=pool_d, yb_pre=yb_pre, y_b=y_b)

    xbc_c = _conv_fwd(u["xbc"], w["ssd_conv_w_pad"], w["ssd_conv_b_pad"], silu=True, name=n("ssd_conv"))
    dt8 = lax.slice_in_dim(u_all, IN_OFFSETS["dt"][0], IN_OFFSETS["dt"][0] + N_HEADS, axis=1)
    dtcol = dt8.T[:, :, None]
    dtrow = dt8.T[:, None, :]
    ssd_par = (_scal3(w["ssd_dt_bias"]), _scal3(w["ssd_a_log"]), _scal3(w["ssd_d"]))
    y_ssd, states = _ssd_fwd(xbc_c, dtcol, dtrow, *ssd_par, name=n("ssd_fwd"))

    def ssd_post(yv, zv, gv):
        xh, _ = _rms_parts(yv * _silu(zv), MIX)
        return xh * gv

    y_c = _rowwise(ssd_post, [y_ssd, u["z"]], [w["ssd_norm_pad"]], [(N_HEADS * LANES, BF16, "row")], name=n("ssd_post"))
    sv.update(xbc_c=xbc_c, dtcol=dtcol, dtrow=dtrow, y_ssd=y_ssd, states=states, y_c=y_c)

    xc = _conv_fwd(u["lru_x"], w["lru_conv_w"], _row(w["lru_conv_b"]), silu=False, name=n("lru_conv"))
    pre = _mm(xc, w["lru_bd"], name=n("lru_mm"))
    lru_par = (_row(w["lru_lambda"]), _row(w["lru_b_a"]), _row(w["lru_b_i"]))
    y_d, h_lru = _lru_fwd(pre, xc, u["lru_g"], *lru_par, name=n("lru_fwd"))
    sv.update(xc=xc, pre=pre, h_lru=h_lru, y_d=y_d)

    merged, *ybs = _branch_merge([y_a, y_b, y_c, y_d], w["br"], u_all, name=n("branch_merge"))
    x1 = _mm(merged, w["w_out"], epilogue=lambda acc, xr: (acc + xr,), tiles=[x], name=n("out_proj"))
    sv.update(ybs=ybs, merged=merged, x1=x1)

    if "late" in w:
        w.update(w.pop("late")(x1))
    h2 = _rms_fwd(x1, _row(w["g_mlp"]), name=n("rms_mlp"))
    a_ff, f_ff = _mm(h2, w["w_ff1"], epilogue=lambda acc: (acc, jnp.square(jnp.maximum(acc, 0.0))),
                     out_dtypes=(BF16, BF16), name=n("ff1"))
    x2 = _mm(f_ff, w["w_ff2"], epilogue=lambda acc, xr: (acc + xr,), tiles=[x1], name=n("ff2"))
    sv.update(h2=h2, a_ff=a_ff, f_ff=f_ff, x2=x2)

    h3 = _rms_fwd(x2, _row(w["g_ple"]), name=n("rms_ple"))
    e_ple = _mm(p_emb, w["w_ple"], name=n("ple_emb"))
    x3, gt_ple = _mm(h3, w["w_ple_gate"], epilogue=lambda acc, ev, xr: (xr + ev * _sigmoid(acc), _sigmoid(acc)),
                     tiles=[e_ple, x2], out_dtypes=(F32, F32), name=n("ple_gate"))
    sv.update(h3=h3, e_ple=e_ple, gt_ple=gt_ple, p_emb=p_emb)
    return x3, sv


def _layer_bwd(dx3, sv, w, rope, tag):
    n = lambda s: f"{s}_{tag}"
    gr = {}
    u = sv["u"]

    de, dpre = _rowwise(lambda d, gt, ev: (d * gt, d * ev * gt * (1.0 - gt)), [dx3, sv["gt_ple"], sv["e_ple"]], [],
                        [(D_MODEL, BF16, "row"), (D_MODEL, BF16, "row")], name=n("ple_bwd"))
    gr["w_ple"] = _mm(sv["p_emb"], de, ta=True, name=n("d_w_ple"))
    gr["w_ple_gate"] = _mm(sv["h3"], dpre, ta=True, name=n("d_w_ple_gate"))
    dh3 = _mm(dpre, w["w_ple_gate"], tb=True, out_dtypes=(BF16,), name=n("d_h3"))
    dx2, dg = _rms_bwd(sv["x2"], _row(w["g_ple"]), dh3, dx3, name=n("rms_ple_bwd"))
    gr["g_ple"] = dg[0]

    gr["w_ff2"] = _mm(sv["f_ff"], dx2, ta=True, name=n("d_w_ff2"))
    da = _mm(dx2, w["w_ff2"], tb=True, epilogue=lambda acc, av: (acc * 2.0 * jnp.maximum(av, 0.0),),
             tiles=[sv["a_ff"]], out_dtypes=(BF16,), name=n("d_a_ff"))
    gr["w_ff1"] = _mm(sv["h2"], da, ta=True, name=n("d_w_ff1"))
    dh2 = _mm(da, w["w_ff1"], tb=True, out_dtypes=(BF16,), name=n("d_h2"))
    dx1, dg = _rms_bwd(sv["x1"], _row(w["g_mlp"]), dh2, dx2, name=n("rms_mlp_bwd"))
    gr["g_mlp"] = dg[0]

    gr["w_out"] = _mm(sv["merged"], dx1, ta=True, name=n("d_w_out"))
    dmerged = _mm(dx1, w["w_out"], tb=True, name=n("d_merged"))

    def merge_bwd(dm, gts, y0, y1, y2, y3):
        dys, dgs = [], []
        for b, yb in enumerate((y0, y1, y2, y3)):
            sg = _sigmoid(gts[:, b * D_MODEL:(b + 1) * D_MODEL])
            dys.append(dm * sg)
            dgs.append(dm * yb * sg * (1.0 - sg))
        return (*dys, jnp.concatenate(dgs, axis=1))

    *dybs, dgates = _rowwise(merge_bwd, [dmerged, u["gates"]] + sv["ybs"], [],
                             [(D_MODEL, BF16, "row")] * 4 + [(4 * D_MODEL, BF16, "row")], name=n("merge_bwd"))
    ys = [sv["y_a"], sv["y_b"], sv["y_c"], sv["y_d"]]
    dwb = [_mm(ys[b], dybs[b], ta=True, name=n(f"d_w_branch{b}")) for b in range(4)]
    gr["w_branch"] = jnp.stack([_unpad_blocks(dwb[0], 0, N_HEADS, HEAD), dwb[1],
                                _unpad_blocks(dwb[2], 0, N_HEADS, HEAD), dwb[3]])
    dy_a = _mm(dybs[0], w["br"][0], tb=True, out_dtypes=(BF16,), name=n("d_y_a"))
    dy_b = _mm(dybs[1], w["br"][1], tb=True, name=n("d_y_b"))
    dy_c = _mm(dybs[2], w["br"][2], tb=True, name=n("d_y_c"))
    dy_d = _mm(dybs[3], w["br"][3], tb=True, name=n("d_y_d"))
    du = {"gates": dgates}

    lru_par = (_row(w["lru_lambda"]), _row(w["lru_b_a"]), _row(w["lru_b_i"]))
    dpa, dpi, dxc_direct, du["lru_g"], dlam, dba, dbi = _lru_bwd(
        sv["pre"], sv["xc"], u["lru_g"], *lru_par, sv["h_lru"], dy_d, name=n("lru_bwd"))
    dpre_lru = jnp.concatenate([dpa, dpi], axis=1)
    d_bd = _mm(sv["xc"], dpre_lru, ta=True, name=n("d_lru_w"))
    gr["lru_w_a"] = _block_diag_t(d_bd[:, :MIX], N_HEADS)
    gr["lru_w_i"] = _block_diag_t(d_bd[:, MIX:], N_HEADS)
    gr["lru_lambda"], gr["lru_b_a"], gr["lru_b_i"] = dlam[0], dba[0], dbi[0]
    dxc = _mm(dpre_lru, w["lru_bd"], tb=True, epilogue=lambda acc, t: (acc + t,), tiles=[dxc_direct], name=n("d_xc"))
    du["lru_x"], gr["lru_conv_w"], dcb = _conv_bwd(u["lru_x"], w["lru_conv_w"], _row(w["lru_conv_b"]), dxc,
                                                  silu=False, name=n("lru_conv_bwd"))
    gr["lru_conv_b"] = dcb[0]

    def ssd_post_bwd(dyc, yv, zv, gv):
        sz = _silu(zv)
        dyz, dgain = _rms_bwd_math(yv * sz, gv, dyc, MIX)
        return dyz * sz, dyz * yv * _silu_grad(zv), dgain

    dy_ssd, du["z"], dgain = _rowwise(ssd_post_bwd, [dy_c, sv["y_ssd"], u["z"]], [w["ssd_norm_pad"]],
                                      [(N_HEADS * LANES, F32, "row"), (N_HEADS * LANES, BF16, "row"),
                                       (N_HEADS * LANES, F32, "acc")], name=n("ssd_post_bwd"))
    gr["ssd_norm"] = _unpad_blocks(dgain[0], 0, N_HEADS, HEAD)
    ssd_par = (_scal3(w["ssd_dt_bias"]), _scal3(w["ssd_a_log"]), _scal3(w["ssd_d"]))
    dxs, dbg, dcg, ddt, dbias, dalog, dd = _ssd_bwd(sv["xbc_c"], sv["dtcol"], sv["dtrow"], *ssd_par, sv["states"],
                                                    dy_ssd, name=n("ssd_bwd"))
    s = dxs.shape[0]
    dxbc_c = jnp.concatenate([dxs, dbg, dcg], axis=1)
    gr["ssd_dt_bias"], gr["ssd_a_log"], gr["ssd_d"] = dbias[:, 0, 0], dalog[:, 0, 0], dd[:, 0, 0]
    du["xbc"], dcw, dcb = _conv_bwd(u["xbc"], w["ssd_conv_w_pad"], w["ssd_conv_b_pad"], dxbc_c, silu=True,
                                    name=n("ssd_conv_bwd"))
    gr["ssd_conv_w"], gr["ssd_conv_b"] = _unpad_xbc_vec(dcw), _unpad_xbc_vec(dcb[0])
    du["dt"] = jnp.pad(ddt[:, :, 0].T, ((0, 0), (0, LANES - N_HEADS)))

    dyb_pre, dscale = _rowwise(lambda d, yp, sc: (d * sc, _colsum(d * yp)), [dy_b, sv["yb_pre"]],
                               [_row(w["pool_scale"])], [(MIX, BF16, "row"), (MIX, F32, "acc")], name=n("pool_scale_bwd"))
    gr["pool_scale"] = dscale[0]
    gr["w_pool"] = _block_diag_t(_mm(sv["pool_d"], dyb_pre, ta=True, name=n("d_w_pool")), 4)
    dd_pool = _mm(dyb_pre, w["pool_bd"], tb=True, name=n("d_pool_d"))
    du["pool"] = _pool_bwd(dd_pool, name=n("pool_bwd"))

    delta = _att_delta(sv["y_a"], dy_a, name=n("att_delta"))
    to_row = lambda t: t.reshape(N_HEADS, 1, s)
    dqc, dkc, dvc = _flash_bwd(sv["qc"], sv["kc"], sv["vc"], dy_a, to_row(sv["lse"]), to_row(delta), name=n("flash_bwd"))
    dq_pad, du["kr"] = _att_prep_bwd(dqc, dkc, *rope, name=n("att_prep_bwd"))
    d_uq = _mm(sv["cqn"], dq_pad, ta=True, name=n("d_w_uq"))
    gr["w_uq"] = _unpad_blocks(d_uq, 1, N_HEADS, HEAD + QK_ROPE)
    dcqn = _mm(dq_pad, w["uq"], tb=True, out_dtypes=(BF16,), name=n("d_cqn"))
    du["cq"], dg = _rms_bwd(u["cq"], _row(w["q_norm"]), dcqn, name=n("rms_q_bwd"))
    gr["q_norm"] = dg[0]
    dkv2 = jnp.concatenate([dkc, dvc], axis=1).astype(BF16)
    d_ukv = _mm(sv["ckvn"], dkv2, ta=True, name=n("d_w_ukv"))
    wk = N_HEADS * LANES
    dk_real = _unpad_blocks(d_ukv[:, :wk], 1, N_HEADS, HEAD).reshape(KV_LORA, N_HEADS, HEAD)
    dv_real = _unpad_blocks(d_ukv[:, wk:], 1, N_HEADS, HEAD).reshape(KV_LORA, N_HEADS, HEAD)
    gr["w_ukv"] = jnp.concatenate([dk_real, dv_real], axis=2).reshape(KV_LORA, N_HEADS * 2 * HEAD)
    dckvn = _mm(dkv2, w["ukv"], tb=True, out_dtypes=(BF16,), name=n("d_ckvn"))
    du["ckv"], dg = _rms_bwd(u["ckv"], _row(w["kv_norm"]), dckvn, name=n("rms_kv_bwd"))
    gr["kv_norm"] = dg[0]

    du_all = _w_in_fused({k: v.astype(BF16) for k, v in du.items()})
    dw_all = _mm(sv["h"], du_all, ta=True, name=n("d_w_in"))
    gr["w_in"] = _w_in_ungroup({k: dw_all[:, off:off + width] for k, off, width in IN_LAYOUT})
    dh = _mm(du_all, w["in_all"], tb=True, name=n("d_h"))
    dx, dg = _rms_bwd(sv["x"], _row(w["g_mix"]), dh, dx1, name=n("rms_mix_bwd"))
    gr["g_mix"] = dg[0]
    return dx, gr


def _pack_rows(n_elems):
    per = PACK_W * PACK_ROWS
    return -(-n_elems // per) * PACK_ROWS


def _pack_flat(parts, dtype):
    flat = jnp.concatenate([p.reshape(-1).astype(dtype) for p in parts])
    rows = _pack_rows(flat.shape[0])
    return jnp.pad(flat, (0, rows * PACK_W - flat.shape[0])).reshape(rows, PACK_W)


def _unpack_flat(buf, shapes):
    lead = buf.shape[:-2]
    flat = buf.reshape(lead + (-1,))
    out, off = [], 0
    for shp in shapes:
        size = int(np.prod(shp))
        out.append(flat[..., off:off + size].reshape(lead + tuple(shp)))
        off += size
    return out


def _merge_shards(t, axis):
    return jnp.concatenate([t[i] for i in range(4)], axis=axis)


def _split_shards(t, axis):
    return jnp.stack(jnp.split(t, 4, axis=axis))


def _rope_tables(positions):
    inv = 1.0 / (ROPE_THETA ** (jnp.arange(0, QK_ROPE, 2, dtype=F32) / QK_ROPE))
    ang = positions.astype(F32)[:, None] * inv
    cos, sin = jnp.cos(ang), jnp.sin(ang)
    s = ang.shape[0]
    half = QK_ROPE // 2
    z = lambda n_: jnp.zeros((s, n_), F32)
    cos_t = jnp.concatenate([jnp.ones((s, HEAD), F32), cos, cos, jnp.ones((s, LANES - HEAD - QK_ROPE), F32)], axis=1)
    sin_p = jnp.concatenate([z(HEAD + half), sin, z(LANES - HEAD - QK_ROPE)], axis=1)
    sin_m = jnp.concatenate([z(HEAD), -sin, z(half + LANES - HEAD - QK_ROPE)], axis=1)
    return cos_t, sin_p, sin_m


def _loss_head(x, g, target, *, name):
    d = x.shape[1]

    def fn(xv, tv, gv):
        xh, r = _rms_parts(xv, d)
        y = xh * gv
        err = y - tv
        dy = err * (1.0 / d)
        dxh = dy * gv
        dx = r * (dxh - xh * (jnp.sum(dxh * xh, axis=-1, keepdims=True) * (1.0 / d)))
        return dx, _colsum(dy * xh), _colsum(err * err) * (0.5 / d)

    return _rowwise(fn, [x, target], [g], [(d, F32, "row"), (d, F32, "acc"), (d, F32, "acc")], name=name)


MATS = tuple((nm, ax) for nm, ax in BIG if nm not in CONV_SHARDED)
LATE_L0 = ("w_ff1", "w_ff2", "w_ple_gate", "w_ple")


def _grad_view(g, ax_layer):
    if ax_layer == 0:
        return g.reshape(4, g.shape[0] // 4, g.shape[1])
    return g.reshape(1, -1, g.shape[-1])


def _reduce_start(grads_l, c_idx, tag):
    views = [_grad_view(grads_l[nm], ax - 1) for nm, ax in MATS]
    got = _send_half(views, name="send_half_" + tag)
    parts = []
    for (nm, ax), v, gt in zip(MATS, views, got):
        both = _chip_sum_half(v, gt, c_idx, name=f"chip_sum_{nm}_{tag}")
        parts.append(both if ax == 1 else _split_shards(both[0], 1))
    return _push_start(parts, scatter=True, name="push_grads_" + tag)


def _reduce_finish(state, after, k_chip, tag):
    send_sems, recv_sems, parts, lands, _ = state
    parts, landed = _push_wait(send_sems, recv_sems, parts, lands, after, name="wait_grads_" + tag)
    mine = [lax.dynamic_update_index_in_dim(t, lax.dynamic_index_in_dim(p, k_chip, 0, keepdims=False), k_chip, 0)
            for t, p in zip(landed, parts)]
    return list(zip(mine, _swap_with_sibling(mine, name="swap_halves_" + tag)))


def _step(args):
    x = args["x"][0]
    c_idx = lax.axis_index("c")
    k_chip = 2 * lax.axis_index("x") + lax.axis_index("y")

    early = [(nm, ax) for nm, ax in MATS if nm not in LATE_L0]
    late = [(nm, ax) for nm, ax in MATS if nm in LATE_L0]
    mine = [{nm: args[nm][l].astype(BF16) for nm, _ in MATS} for l in range(2)]
    gathered0 = _gather_halves([mine[0][nm] for nm, _ in early])
    convs = [(nm, ax) for nm, ax in BIG if nm in CONV_SHARDED]
    conv_all = _gather_all(_pack_flat([args[nm] for nm, _ in convs], F32), name="gather_conv_taps")[0::2]
    late0, gathered0, conv_all = lax.optimization_barrier(([mine[0][nm] for nm, _ in late], gathered0, conv_all))
    push0 = _push_start(late0, scatter=False, name="push_weights_l0")
    push1 = _push_start(lax.optimization_barrier(([mine[1][nm] for nm, _ in MATS], push0[4]))[0], scatter=False,
                        name="push_weights_l1")
    full_conv = {nm: _merge_shards(t, ax)
                 for (nm, ax), t in zip(convs, _unpack_flat(conv_all, [args[nm].shape for nm, _ in convs]))}
    rope = _rope_tables(args["positions"][0])

    def full(which, lands, own):
        return {nm: _merge_shards(lax.dynamic_update_index_in_dim(t, blk, k_chip, 0), ax - 1)
                for (nm, ax), t, blk in zip(which, lands, own)}

    def landed(push, which, name):
        def wait(after):
            own, lands = _push_wait(push[0], push[1], push[2], push[3], after, name=name)
            return full(which, lands, own)
        return wait

    def layer_weights(l, mats_full):
        p = dict(mats_full)
        p.update({nm: full_conv[nm][l] for nm in CONV_SHARDED})
        p.update({nm: args[nm][l] for nm in SMALL if nm != "g_final"})
        return _layer_weights(p)

    layers = [layer_weights(0, full(early, gathered0, [mine[0][nm] for nm, _ in early])), None]
    layers[0]["late"] = landed(push0, late, "wait_weights_l0")
    layers[0]["g_mix"] = layers[0]["g_mix"] + push1[4][0, 0]
    x, sv0 = _layer_fwd(x, args["p"][0, 0], layers[0], rope, "l0")
    layers[1] = layer_weights(1, landed(push1, MATS, "wait_weights_l1")(x))
    x, sv1 = _layer_fwd(x, args["p"][1, 0], layers[1], rope, "l1")
    saved = [sv0, sv1]

    dx, dg_final, loss_part = _loss_head(x, _row(args["g_final"]), args["loss_target"][0], name="loss_head")
    loss = lax.psum(jnp.sum(loss_part), ("x", "y", "c"))

    grads = [None, None]
    dx, grads[1] = _layer_bwd(dx, saved[1], layers[1], rope, "l1")
    reduce1 = _reduce_start(grads[1], c_idx, "l1")
    dx, grads[0] = _layer_bwd(dx + reduce1[4][0, 0], saved[0], layers[0], rope, "l0")
    g_all = {nm: jnp.stack([grads[0][nm], grads[1][nm]]) for nm in SMALL + CONV_SHARDED if nm != "g_final"}
    g_all["g_final"] = dg_final[0]
    all_names = SMALL + CONV_SHARDED
    all_shapes = [g_all[nm].shape for nm in all_names]
    packed = _pack_flat([g_all[nm] for nm in all_names], F32)
    sibling = _swap_with_sibling([packed], name="swap_small_grads")[0]
    pair = jnp.where(c_idx == 0, jnp.stack([packed, sibling]), jnp.stack([sibling, packed]))
    small_all = _gather_same_core(_sum_slots(pair, name="sum_cores"), name="gather_small_grads")
    g0_mats, small_all = lax.optimization_barrier(({nm: grads[0][nm] for nm, _ in MATS}, small_all))
    reduce0 = _reduce_start(g0_mats, c_idx, "l0")
    small_sum = _sum_slots(small_all, name="sum_chips")
    g_red = dict(zip(all_names, _unpack_flat(small_sum, all_shapes)))
    for nm, ax in BIG:
        if nm in CONV_SHARDED:
            width = args[nm].shape[ax]
            g_red[nm] = lax.dynamic_slice_in_dim(g_red[nm], k_chip * width, width, axis=ax)
    small_shapes = [args[nm].shape for nm in SMALL]
    pack_small = lambda src: _pack_flat([src(nm) for nm in SMALL], F32)
    upd_small = _adamw(pack_small(lambda nm: args[nm]), pack_small(lambda nm: g_red[nm]),
                       pack_small(lambda nm: args["m_" + nm]), pack_small(lambda nm: args["v_" + nm]), name="adamw_small")
    upd = {nm: trip for nm, trip in zip(SMALL, zip(*[_unpack_flat(t, small_shapes) for t in upd_small]))}
    for nm in CONV_SHARDED:
        upd[nm] = _adamw(args[nm], g_red[nm], args["m_" + nm], args["v_" + nm], name="adamw_" + nm)

    slots = [_reduce_finish(reduce0, upd_small[0], k_chip, "l0"), _reduce_finish(reduce1, dx, k_chip, "l1")]
    for i, (nm, _) in enumerate(MATS):
        g_red[nm], *upd[nm] = _adamw_slots(args[nm], [slots[0][i], slots[1][i]], args["m_" + nm], args["v_" + nm],
                                           c_idx, name="adamw_" + nm)

    outs = [loss, dx[None]]
    outs += [g_red[nm] for nm in WEIGHTS]
    for i in range(3):
        outs += [upd[nm][i] for nm in WEIGHTS]
    return tuple(outs)


_ARG_NAMES = ("x", "p", "positions") + WEIGHTS + ("loss_target",) + tuple("m_" + nm for nm in WEIGHTS) \
    + tuple("v_" + nm for nm in WEIGHTS)


def kernel(*arrays):
    assert len(arrays) == len(_ARG_NAMES), len(arrays)
    return _step(dict(zip(_ARG_NAMES, arrays)))
```

```python
import functools
import math

import jax
import jax.numpy as jnp
import numpy as np
from jax import lax
from jax.experimental import pallas as pl
from jax.experimental.pallas import tpu as pltpu

F32 = jnp.float32
BF16 = jnp.bfloat16
MXU_DTYPE = BF16
LANES = 128
VMEM_LIMIT = 56 * 1024 * 1024
MM_VMEM_BUDGET = 36 * 1024 * 1024
ELEMENTWISE_BLOCK_BYTES = 2 * 1024 * 1024

D_MODEL = 1024
N_HEADS = 8
HEAD = 64
QK_ROPE = 32
Q_LORA = 384
KV_LORA = 256
MIX = 512
SSD_CHUNK = 128
CONV_W = 4
POOL_WINDOWS = (2, 4, 8, 16)
LRU_C = 8.0
EPS = 1e-6
ROPE_THETA = 10000.0
ATT_SCALE = (HEAD + QK_ROPE) ** -0.5
SPLIT_SIZES = (Q_LORA, KV_LORA, QK_ROPE, MIX, MIX, 768, N_HEADS, MIX, MIX, 4 * D_MODEL)
IN_LAYOUT = (("gates", 0, 4096), ("z", 4096, 1024), ("pool", 5120, 512), ("lru_g", 5632, 512), ("lru_x", 6144, 512),
             ("cq", 6912, 384), ("ckv", 7424, 256), ("xbc", 7680, 1536), ("kr", 9216, 128), ("dt", 9344, 128))
IN_OFFSETS = {name: (off, width) for name, off, width in IN_LAYOUT}
IN_ALL_COLS = 9728

ADAM_LR, ADAM_B1, ADAM_B2, ADAM_EPS, ADAM_WD, ADAM_STEP = 0.001, 0.9, 0.999, 1e-08, 0.01, 10

BIG = (("w_in", 2), ("w_uq", 2), ("w_ukv", 2), ("ssd_conv_w", 2), ("lru_conv_w", 2), ("w_branch", 3),
       ("w_out", 1), ("w_ff1", 2), ("w_ff2", 1), ("w_ple_gate", 1), ("w_ple", 2))
SMALL = ("g_mix", "q_norm", "kv_norm", "w_pool", "pool_scale", "ssd_conv_b", "ssd_dt_bias", "ssd_a_log",
         "ssd_d", "ssd_norm", "lru_conv_b", "lru_w_a", "lru_b_a", "lru_w_i", "lru_b_i", "lru_lambda",
         "g_mlp", "g_ple", "g_final")
WEIGHTS = ("g_mix", "w_in", "q_norm", "w_uq", "kv_norm", "w_ukv", "w_pool", "pool_scale", "ssd_conv_w",
           "ssd_conv_b", "ssd_dt_bias", "ssd_a_log", "ssd_d", "ssd_norm", "lru_conv_w", "lru_conv_b", "lru_w_a",
           "lru_b_a", "lru_w_i", "lru_b_i", "lru_lambda", "w_branch", "w_out", "g_mlp", "w_ff1", "w_ff2", "g_ple",
           "w_ple_gate", "w_ple", "g_final")
CONV_SHARDED = ("ssd_conv_w", "lru_conv_w")
PACK_W = 1024
PACK_ROWS = 64


def _cparams(sem, vmem=VMEM_LIMIT):
    return pltpu.CompilerParams(dimension_semantics=sem, vmem_limit_bytes=vmem)


def _pick(n, cands):
    for c in cands:
        if n % c == 0:
            return c
    return n


class _Cols:
    def __init__(self, arr, off, width):
        self.arr, self.off, self.width = arr, off, width

    shape = property(lambda self: (self.arr.shape[0], self.width))
    dtype = property(lambda self: self.arr.dtype)


def _arr(x):
    return x.arr if isinstance(x, _Cols) else x


def _off(x, unit):
    off = x.off if isinstance(x, _Cols) else 0
    assert off % unit == 0, (off, unit)
    return off // unit


def _sigmoid(x):
    return 1.0 / (1.0 + jnp.exp(-x))


def _silu(x):
    return x * _sigmoid(x)


def _silu_grad(x):
    s = _sigmoid(x)
    return s * (1.0 + x * (1.0 - s))


def _softplus(x):
    e = jnp.exp(-jnp.abs(x))
    log1p_e = jnp.where(e < 1e-3, e * (1.0 - e * (0.5 - e * (1.0 / 3.0))), jnp.log(1.0 + e))
    return jnp.maximum(x, 0.0) + log1p_e


_GELU_C = math.sqrt(2.0 / math.pi)


def _gelu(x):
    t = jnp.tanh(_GELU_C * (x + 0.044715 * x * x * x))
    return 0.5 * x * (1.0 + t)


def _gelu_grad(x):
    t = jnp.tanh(_GELU_C * (x + 0.044715 * x * x * x))
    return 0.5 * (1.0 + t) + 0.5 * x * (1.0 - t * t) * _GELU_C * (1.0 + 3.0 * 0.044715 * x * x)


def _neg_expm1(x):
    series = -x * (1.0 + 0.5 * x * (1.0 + (1.0 / 3.0) * x * (1.0 + 0.25 * x)))
    return jnp.where(x > -0.05, series, 1.0 - jnp.exp(x))


def _shift_down(x, k, row):
    return jnp.where(row >= k, pltpu.roll(x, k, 0), 0.0)


def _shift_up(x, k, row):
    n = x.shape[0]
    return jnp.where(row < n - k, pltpu.roll(x, n - k, 0), 0.0)


def _cumsum_rows(x, row):
    d = 1
    while d < x.shape[0]:
        x = x + _shift_down(x, d, row)
        d *= 2
    return x


def _rev_cumsum_rows(x, row):
    d = 1
    while d < x.shape[0]:
        x = x + _shift_up(x, d, row)
        d *= 2
    return x


def _cumsum_lanes(x, col):
    d = 1
    while d < x.shape[1]:
        x = x + jnp.where(col >= d, pltpu.roll(x, d, 1), 0.0)
        d *= 2
    return x


def _dot(a, b, ta=False, tb=False):
    dn = (((0 if ta else 1,), (1 if tb else 0,)), ((), ()))
    return lax.dot_general(a.astype(MXU_DTYPE), b.astype(MXU_DTYPE), dn, preferred_element_type=F32)


def _mm_tiles(m, n, k, a_bytes, b_bytes, mn_bytes):
    best = None
    for tm in (1024, 512, 384, 256, 128):
        for tn in (1024, 512, 384, 256, 128):
            for tk in (2048, 1024, 512, 384, 256, 128):
                if m % tm or n % tn or k % tk:
                    continue
                vmem = 2 * (tm * tk * a_bytes + tk * tn * b_bytes) + 2 * tm * tn * mn_bytes + 4 * tm * tn
                vmem += 2 * (tm * tk + tk * tn)
                if vmem > MM_VMEM_BUDGET:
                    continue
                steps = (m // tm) * (n // tn) * (k // tk)
                key = (steps, vmem)
                if best is None or key < best[0]:
                    best = (key, (tm, tn, tk))
    assert best is not None, (m, n, k)
    return best[1]


def _mm(a, b, *, ta=False, tb=False, epilogue=None, tiles=(), rowvecs=(), out_dtypes=(F32,), name):
    m, k = (a.shape[1], a.shape[0]) if ta else a.shape
    n = b.shape[0] if tb else b.shape[1]
    assert (b.shape[1] if tb else b.shape[0]) == k, (a.shape, b.shape, ta, tb)
    mn_bytes = sum(t.dtype.itemsize for t in tiles) + sum(jnp.dtype(dt).itemsize for dt in out_dtypes)
    tm, tn, tk = _mm_tiles(m, n, k, a.dtype.itemsize, b.dtype.itemsize, mn_bytes)
    nk = k // tk
    nt, nr, no = len(tiles), len(rowvecs), len(out_dtypes)

    def body(*refs):
        a_ref, b_ref = refs[0], refs[1]
        tile_refs = refs[2:2 + nt]
        row_refs = refs[2 + nt:2 + nt + nr]
        out_refs = refs[2 + nt + nr:2 + nt + nr + no]
        acc_ref = refs[-1]
        kk = pl.program_id(2)

        @pl.when(kk == 0)
        def _():
            acc_ref[...] = jnp.zeros_like(acc_ref)

        acc_ref[...] += _dot(a_ref[...], b_ref[...], ta, tb)

        @pl.when(kk == nk - 1)
        def _():
            acc = acc_ref[...]
            if epilogue is None:
                outs = (acc,)
            else:
                outs = epilogue(acc, *[t[...] for t in tile_refs], *[r[...] for r in row_refs])
            for o_ref, o in zip(out_refs, outs):
                o_ref[...] = o.astype(o_ref.dtype)

    a_spec = pl.BlockSpec((tk, tm), lambda i, j, kk: (kk, i)) if ta else pl.BlockSpec((tm, tk), lambda i, j, kk: (i, kk))
    b_spec = pl.BlockSpec((tn, tk), lambda i, j, kk: (j, kk)) if tb else pl.BlockSpec((tk, tn), lambda i, j, kk: (kk, j))
    mn_spec = pl.BlockSpec((tm, tn), lambda i, j, kk: (i, j))
    row_spec = pl.BlockSpec((1, tn), lambda i, j, kk: (0, j))
    tile_specs = [pl.BlockSpec((tm, tn), lambda i, j, kk, ob=_off(t, tn): (i, j + ob)) for t in tiles]
    outs = pl.pallas_call(
        body, name=name,
        grid=(m // tm, n // tn, nk),
        in_specs=[a_spec, b_spec] + tile_specs + [row_spec] * nr,
        out_specs=[mn_spec] * no,
        out_shape=[jax.ShapeDtypeStruct((m, n), dt) for dt in out_dtypes],
        scratch_shapes=[pltpu.VMEM((tm, tn), F32)],
        compiler_params=_cparams(("parallel", "parallel", "arbitrary")),
    )(a, b, *[_arr(t) for t in tiles], *rowvecs)
    return outs[0] if no == 1 else tuple(outs)


def _branch_merge(ys, ws, u_all, *, name):
    s, d = ys[0].shape[0], ws[0].shape[1]
    tm, tn = _pick(s, (512, 256, 128)), _pick(d, (512, 256, 128))
    nb = len(ys)

    def body(*refs):
        y_refs, w_refs, g_refs = refs[:nb], refs[nb:2 * nb], refs[2 * nb:3 * nb]
        merged_ref, yb_refs = refs[3 * nb], refs[3 * nb + 1:]
        merged = None
        for y_ref, w_ref, g_ref, yb_ref in zip(y_refs, w_refs, g_refs, yb_refs):
            acc = _dot(y_ref[...], w_ref[...])
            yb_ref[...] = acc.astype(yb_ref.dtype)
            term = _sigmoid(g_ref[...]) * acc
            merged = term if merged is None else merged + term
        merged_ref[...] = merged

    mn = pl.BlockSpec((tm, tn), lambda i, j: (i, j))
    in_specs = [pl.BlockSpec((tm, y.shape[1]), lambda i, j: (i, 0)) for y in ys]
    in_specs += [pl.BlockSpec((w.shape[0], tn), lambda i, j: (0, j)) for w in ws]
    in_specs += [pl.BlockSpec((tm, tn), lambda i, j, ob=b * d // tn: (i, j + ob)) for b in range(nb)]
    return pl.pallas_call(
        body, name=name, grid=(s // tm, d // tn), in_specs=in_specs, out_specs=[mn] * (nb + 1),
        out_shape=[jax.ShapeDtypeStruct((s, d), F32)] + [jax.ShapeDtypeStruct((s, d), BF16)] * nb,
        compiler_params=_cparams(("parallel", "parallel")),
    )(*ys, *ws, *[u_all] * nb)


def _rowwise(fn, rows, fulls, outs, *, name, tm=None):
    r = rows[0].shape[0]
    if tm is None:
        widest = max([x.shape[1] for x in rows] + [o[0] for o in outs])
        tm = _pick(r, (max(8, min(512, (512 * 1024) // widest)), 256, 128, 64, 32, 16, 8))
    nrow, nfull, nout = len(rows), len(fulls), len(outs)

    def body(*refs):
        row_refs = refs[:nrow]
        full_refs = refs[nrow:nrow + nfull]
        out_refs = refs[nrow + nfull:]
        res = fn(*[x[...] for x in row_refs], *[x[...] for x in full_refs])
        if not isinstance(res, (tuple, list)):
            res = (res,)
        step = pl.program_id(0)
        for o_ref, o, spec in zip(out_refs, res, outs):
            if spec[2] == "row":
                o_ref[...] = o.astype(o_ref.dtype)
            else:
                @pl.when(step == 0)
                def _(o_ref=o_ref):
                    o_ref[...] = jnp.zeros_like(o_ref)
                o_ref[...] += o

    in_specs = [pl.BlockSpec((tm, x.shape[1]), lambda i, ob=_off(x, x.shape[1]): (i, ob)) for x in rows]
    in_specs += [pl.BlockSpec(x.shape, lambda i, nd=x.ndim: (0,) * nd) for x in fulls]
    out_specs, out_shape = [], []
    for c, dt, kind in outs:
        if kind == "row":
            out_specs.append(pl.BlockSpec((tm, c), lambda i: (i, 0)))
            out_shape.append(jax.ShapeDtypeStruct((r, c), dt))
        else:
            out_specs.append(pl.BlockSpec((1, c), lambda i: (0, 0)))
            out_shape.append(jax.ShapeDtypeStruct((1, c), F32))
    res = pl.pallas_call(
        body, name=name, grid=(r // tm,), in_specs=in_specs, out_specs=out_specs, out_shape=out_shape,
        compiler_params=_cparams(("arbitrary",)),
    )(*[_arr(x) for x in rows], *fulls)
    return res[0] if nout == 1 else tuple(res)


def _colsum(x):
    return jnp.sum(x, axis=0, keepdims=True)


def _rms_parts(x, n_real):
    r = lax.rsqrt(jnp.sum(x * x, axis=-1, keepdims=True) * (1.0 / n_real) + EPS)
    return x * r, r


def _rms_fwd(x, g, *, n_real=None, out_dtype=BF16, name):
    n_real = n_real or x.shape[1]

    def fn(xv, gv):
        xh, _ = _rms_parts(xv, n_real)
        return xh * gv

    return _rowwise(fn, [x], [g], [(x.shape[1], out_dtype, "row")], name=name)


def _rms_bwd_math(xv, gv, dh, n_real):
    xh, r = _rms_parts(xv, n_real)
    dxh = dh * gv
    dx = r * (dxh - xh * (jnp.sum(dxh * xh, axis=-1, keepdims=True) * (1.0 / n_real)))
    return dx, _colsum(dh * xh)


def _rms_bwd(x, g, dh, res=None, *, name):
    n = x.shape[1]
    if res is None:
        def fn(xv, dhv, gv):
            return _rms_bwd_math(xv, gv, dhv.astype(F32), n)
        rows = [x, dh]
    else:
        def fn(xv, dhv, rv, gv):
            dx, dg = _rms_bwd_math(xv, gv, dhv.astype(F32), n)
            return dx + rv, dg
        rows = [x, dh, res]
    return _rowwise(fn, rows, [g], [(n, F32, "row"), (n, F32, "acc")], name=name)


def _seq_call(body, ins, outs, n_blocks, *, name):
    in_specs, args = [], []
    for x, kind in ins:
        in_specs.append(pl.BlockSpec((x.shape[0], LANES), lambda j, ob=_off(x, LANES): (0, j + ob)))
        args.append(_arr(x))
    out_specs, out_shape = [], []
    for shape, dt in outs:
        out_specs.append(pl.BlockSpec((shape[0], LANES), lambda j: (0, j)))
        out_shape.append(jax.ShapeDtypeStruct(shape, dt))
    res = pl.pallas_call(body, name=name, grid=(n_blocks,), in_specs=in_specs, out_specs=out_specs,
                         out_shape=out_shape, compiler_params=_cparams(("parallel",)))(*args)
    return res[0] if len(outs) == 1 else tuple(res)


def _conv_pre(x, w, b, row):
    acc = x * w[CONV_W - 1:CONV_W, :] + b
    for k in range(CONV_W - 1):
        acc = acc + _shift_down(x, CONV_W - 1 - k, row) * w[k:k + 1, :]
    return acc


def _conv_fwd(x, w, b, *, silu, name):
    s, c = x.shape

    def body(x_ref, w_ref, b_ref, y_ref):
        xv = x_ref[...]
        row = lax.broadcasted_iota(jnp.int32, xv.shape, 0)
        pre = _conv_pre(xv, w_ref[...], b_ref[...], row)
        y_ref[...] = _silu(pre) if silu else pre

    return _seq_call(body, [(x, "seq"), (w, "par"), (b, "par")], [((s, c), F32)], c // LANES, name=name)


def _conv_bwd(x, w, b, dy, *, silu, name):
    s, c = x.shape

    def body(x_ref, w_ref, b_ref, dy_ref, dx_ref, dw_ref, db_ref):
        xv, wv, dv = x_ref[...], w_ref[...], dy_ref[...]
        row = lax.broadcasted_iota(jnp.int32, xv.shape, 0)
        if silu:
            dv = dv * _silu_grad(_conv_pre(xv, wv, b_ref[...], row))
        dx = dv * wv[CONV_W - 1:CONV_W, :]
        dws = [None] * CONV_W
        dws[CONV_W - 1] = _colsum(dv * xv)
        for k in range(CONV_W - 1):
            sh = CONV_W - 1 - k
            dx = dx + _shift_up(dv, sh, row) * wv[k:k + 1, :]
            dws[k] = _colsum(dv * _shift_down(xv, sh, row))
        dx_ref[...] = dx
        for k in range(CONV_W):
            dw_ref[k:k + 1, :] = dws[k]
        db_ref[...] = _colsum(dv)

    return _seq_call(body, [(x, "seq"), (w, "par"), (b, "par"), (dy, "seq")],
                     [((s, c), F32), ((CONV_W, c), F32), ((1, c), F32)], c // LANES, name=name)


def _pool_select(levels):
    g = pl.program_id(0)
    return jnp.where(g == 0, levels[0], jnp.where(g == 1, levels[1], jnp.where(g == 2, levels[2], levels[3])))


def _pool_count(row):
    g = pl.program_id(0)
    w = jnp.where(g == 0, POOL_WINDOWS[0], jnp.where(g == 1, POOL_WINDOWS[1],
                                                     jnp.where(g == 2, POOL_WINDOWS[2], POOL_WINDOWS[3])))
    return jnp.minimum(row + 1, w).astype(F32)


def _pool_fwd(u, *, name):
    def body(u_ref, d_ref):
        uv = u_ref[...]
        row = lax.broadcasted_iota(jnp.int32, uv.shape, 0)
        levels, cur, sh = [], uv, 1
        for _ in POOL_WINDOWS:
            cur = cur + _shift_down(cur, sh, row)
            levels.append(cur)
            sh *= 2
        d_ref[...] = _pool_select(levels) / _pool_count(row) - uv

    return _seq_call(body, [(u, "seq")], [(u.shape, F32)], u.shape[1] // LANES, name=name)


def _pool_bwd(dd, *, name):
    def body(dd_ref, du_ref):
        dv = dd_ref[...]
        row = lax.broadcasted_iota(jnp.int32, dv.shape, 0)
        levels, cur, sh = [], dv / _pool_count(row), 1
        for _ in POOL_WINDOWS:
            cur = cur + _shift_up(cur, sh, row)
            levels.append(cur)
            sh *= 2
        du_ref[...] = _pool_select(levels) - dv

    return _seq_call(body, [(dd, "seq")], [(dd.shape, F32)], dd.shape[1] // LANES, name=name)


def _lru_gates(pre_a, pre_i, xc, lam, b_a, b_i):
    r = _sigmoid(pre_a + b_a)
    i = _sigmoid(pre_i + b_i)
    sp = _softplus(-lam)
    log_a = -LRU_C * r * sp
    a = jnp.exp(log_a)
    mult = jnp.sqrt(_neg_expm1(2.0 * log_a))
    return r, i, sp, a, mult


def _lru_fwd(pre, xc, gate_in, lam, b_a, b_i, *, name):
    s, c = xc.shape
    nb = c // LANES

    def body(pa_ref, pi_ref, xc_ref, g_ref, lam_ref, ba_ref, bi_ref, y_ref, h_ref):
        xv = xc_ref[...]
        row = lax.broadcasted_iota(jnp.int32, xv.shape, 0)
        _, i, _, a, mult = _lru_gates(pa_ref[...], pi_ref[...], xv, lam_ref[...], ba_ref[...], bi_ref[...])
        h = xv * i * mult
        d = 1
        while d < s:
            h = h + a * _shift_down(h, d, row)
            a = a * jnp.where(row >= d, pltpu.roll(a, d, 0), 1.0)
            d *= 2
        h_ref[...] = h
        y_ref[...] = h * _gelu(g_ref[...])

    blk = lambda off: pl.BlockSpec((s, LANES), lambda j: (0, j + off))
    par = pl.BlockSpec((1, LANES), lambda j: (0, j))
    return pl.pallas_call(
        body, name=name, grid=(nb,),
        in_specs=[blk(0), blk(nb), blk(0), blk(_off(gate_in, LANES)), par, par, par],
        out_specs=[blk(0), blk(0)],
        out_shape=[jax.ShapeDtypeStruct((s, c), F32)] * 2,
        compiler_params=_cparams(("parallel",)),
    )(pre, pre, xc, _arr(gate_in), lam, b_a, b_i)


def _lru_bwd(pre, xc, gate_in, lam, b_a, b_i, h, dy, *, name):
    s, c = xc.shape
    nb = c // LANES

    def body(pa_ref, pi_ref, xc_ref, g_ref, lam_ref, ba_ref, bi_ref, h_ref, dy_ref,
             dpa_ref, dpi_ref, dxc_ref, dg_ref, dlam_ref, dba_ref, dbi_ref):
        xv, gv, hv, dv = xc_ref[...], g_ref[...], h_ref[...], dy_ref[...]
        row = lax.broadcasted_iota(jnp.int32, xv.shape, 0)
        r, i, sp, a, mult = _lru_gates(pa_ref[...], pi_ref[...], xv, lam_ref[...], ba_ref[...], bi_ref[...])
        dg_ref[...] = dv * hv * _gelu_grad(gv)
        dh = dv * _gelu(gv)
        an = jnp.where(row < s - 1, pltpu.roll(a, s - 1, 0), 0.0)
        d = 1
        while d < s:
            dh = dh + an * _shift_up(dh, d, row)
            an = an * jnp.where(row < s - d, pltpu.roll(an, s - d, 0), 1.0)
            d *= 2
        da = dh * _shift_down(hv, 1, row)
        dxc_ref[...] = dh * i * mult
        di = dh * xv * mult
        dmult = dh * xv * i
        dlog_a = (da - dmult * a / mult) * a
        dr = dlog_a * (-LRU_C) * sp
        dlam_ref[...] = _colsum(dlog_a * LRU_C * r * _sigmoid(-lam_ref[...]))
        dpa = dr * r * (1.0 - r)
        dpi = di * i * (1.0 - i)
        dpa_ref[...] = dpa
        dpi_ref[...] = dpi
        dba_ref[...] = _colsum(dpa)
        dbi_ref[...] = _colsum(dpi)

    blk = lambda off: pl.BlockSpec((s, LANES), lambda j: (0, j + off))
    par = pl.BlockSpec((1, LANES), lambda j: (0, j))
    sc = jax.ShapeDtypeStruct((s, c), F32)
    pc = jax.ShapeDtypeStruct((1, c), F32)
    dpa, dpi, dxc, dg, dlam, dba, dbi = pl.pallas_call(
        body, name=name, grid=(nb,),
        in_specs=[blk(0), blk(nb), blk(0), blk(_off(gate_in, LANES)), par, par, par, blk(0), blk(0)],
        out_specs=[blk(0), blk(0), blk(0), blk(0), par, par, par],
        out_shape=[sc, sc, sc, sc, pc, pc, pc],
        compiler_params=_cparams(("parallel",)),
    )(pre, pre, xc, _arr(gate_in), lam, b_a, b_i, h, dy)
    return dpa, dpi, dxc, dg, dlam, dba, dbi


GROUP_HEADS = 4
SSD_GROUPS = 2


def _ssd_specs(nc, order):
    hw, gw = N_HEADS * LANES, SSD_GROUPS * LANES
    return dict(
        x=pl.BlockSpec((SSD_CHUNK, hw), lambda ci: (order(ci), 0)),
        b=pl.BlockSpec((SSD_CHUNK, gw), lambda ci: (order(ci), hw // gw)),
        c=pl.BlockSpec((SSD_CHUNK, gw), lambda ci: (order(ci), hw // gw + 1)),
        dtcol=pl.BlockSpec((N_HEADS, SSD_CHUNK, 1), lambda ci: (0, order(ci), 0)),
        dtrow=pl.BlockSpec((N_HEADS, 1, SSD_CHUNK), lambda ci: (0, 0, order(ci))),
        scal=pl.BlockSpec((N_HEADS, 1, 1), lambda ci: (0, 0, 0)),
        state=pl.BlockSpec((N_HEADS, 1, LANES, LANES), lambda ci: (0, order(ci), 0, 0)),
        group=pl.BlockSpec((SSD_CHUNK, gw), lambda ci: (order(ci), 0)),
        pacc=pl.BlockSpec((N_HEADS, 1, LANES), lambda ci: (0, 0, 0)),
    )


def _ssd_chunk_terms(dtcol, dtrow, bias, a_log):
    shp = (SSD_CHUNK, SSD_CHUNK)
    row = lax.broadcasted_iota(jnp.int32, shp, 0)
    col = lax.broadcasted_iota(jnp.int32, shp, 1)
    a_head = -jnp.exp(a_log)
    dt_c = jnp.broadcast_to(_softplus(dtcol + bias), shp)
    dt_r = jnp.broadcast_to(_softplus(dtrow + bias), shp)
    cs_c = _cumsum_rows(dt_c * a_head, row)
    cs_r = _cumsum_lanes(dt_r * a_head, col)
    cs_last = jnp.sum(jnp.where(row == SSD_CHUNK - 1, cs_c, 0.0), axis=0, keepdims=True)
    return row, col, a_head, dt_c, cs_c, cs_r, cs_last


def _ssd_fwd(xbc, dtcol, dtrow, bias, a_log, dskip, *, name):
    s = xbc.shape[0]
    nc = s // SSD_CHUNK

    def body(x_ref, b_ref, c_ref, dtc_ref, dtr_ref, bias_ref, alog_ref, d_ref, y_ref, st_ref, state):
        ci = pl.program_id(0)

        @pl.when(ci == 0)
        def _():
            state[...] = jnp.zeros_like(state)

        for gi in range(SSD_GROUPS):
            glanes = slice(gi * LANES, (gi + 1) * LANES)
            bm, cm = b_ref[:, glanes], c_ref[:, glanes]
            cb = _dot(cm, bm, tb=True)
            bm_t = bm.T
            for r in range(gi * GROUP_HEADS, (gi + 1) * GROUP_HEADS):
                lanes = slice(r * LANES, (r + 1) * LANES)
                xv = x_ref[:, lanes]
                row, col, _, dt_c, cs_c, cs_r, cs_last = _ssd_chunk_terms(dtc_ref[r], dtr_ref[r], bias_ref[r], alog_ref[r])
                g = cb * jnp.exp(jnp.where(col <= row, cs_c - cs_r, -jnp.inf))
                xdt = xv * dt_c
                st = state[r]
                st_ref[r, 0] = st
                y_ref[:, lanes] = _dot(g, xdt) + _dot(cm, st) * jnp.exp(cs_c) + xv * d_ref[r]
                state[r] = jnp.exp(cs_last) * st + _dot(bm_t, xdt * jnp.exp(cs_last - cs_c))

    sp = _ssd_specs(nc, lambda ci: ci)
    return pl.pallas_call(
        body, name=name, grid=(nc,),
        in_specs=[sp["x"], sp["b"], sp["c"], sp["dtcol"], sp["dtrow"], sp["scal"], sp["scal"], sp["scal"]],
        out_specs=[sp["x"], sp["state"]],
        out_shape=[jax.ShapeDtypeStruct((s, N_HEADS * LANES), F32),
                   jax.ShapeDtypeStruct((N_HEADS, nc, LANES, LANES), F32)],
        scratch_shapes=[pltpu.VMEM((N_HEADS, LANES, LANES), F32)],
        compiler_params=_cparams(("arbitrary",)),
    )(xbc, xbc, xbc, dtcol, dtrow, bias, a_log, dskip)


def _ssd_bwd(xbc, dtcol, dtrow, bias, a_log, dskip, states, dy, *, name):
    s = xbc.shape[0]
    nc = s // SSD_CHUNK

    def body(x_ref, b_ref, c_ref, dtc_ref, dtr_ref, bias_ref, alog_ref, d_ref, st_ref, dy_ref,
             dx_ref, db_ref, dc_ref, ddt_ref, dbias_ref, dalog_ref, dd_ref, dstate):
        ci = pl.program_id(0)

        @pl.when(ci == 0)
        def _():
            dstate[...] = jnp.zeros_like(dstate)
            dbias_ref[...] = jnp.zeros_like(dbias_ref)
            dalog_ref[...] = jnp.zeros_like(dalog_ref)
            dd_ref[...] = jnp.zeros_like(dd_ref)

        rowsum = lambda v: jnp.sum(v, axis=1, keepdims=True)
        tot = lambda v: jnp.broadcast_to(jnp.sum(v, axis=0, keepdims=True), (1, LANES))
        for gi in range(SSD_GROUPS):
            glanes = slice(gi * LANES, (gi + 1) * LANES)
            bm, cm = b_ref[:, glanes], c_ref[:, glanes]
            cb = _dot(cm, bm, tb=True)
            cb_t = _dot(bm, cm, tb=True)
            cm_t = cm.T
            dbm_sum, dcm_sum = None, None
            for r in range(gi * GROUP_HEADS, (gi + 1) * GROUP_HEADS):
                lanes = slice(r * LANES, (r + 1) * LANES)
                xv, dyv, st = x_ref[:, lanes], dy_ref[:, lanes], st_ref[r, 0]
                dtraw_c, bias = dtc_ref[r], bias_ref[r]
                row, col, a_head, dt_c, cs_c, cs_r, cs_last = _ssd_chunk_terms(dtraw_c, dtr_ref[r], bias, alog_ref[r])
                lmat = jnp.exp(jnp.where(col <= row, cs_c - cs_r, -jnp.inf))
                lmat_t = jnp.exp(jnp.where(row <= col, cs_r - cs_c, -jnp.inf))
                g, g_t = cb * lmat, cb_t * lmat_t
                xdt = xv * dt_c
                e_c = jnp.exp(cs_c)
                f_c = jnp.exp(cs_last - cs_c)
                e_last = jnp.exp(cs_last)
                w = xdt * f_c
                dst = dstate[r]

                dg = _dot(dyv, xdt, tb=True)
                dg_t = _dot(xdt, dyv, tb=True)
                dxdt = _dot(g_t, dyv)
                dcs = rowsum(dg * g) - rowsum(dg_t * g_t)
                dcm = _dot(dg * lmat, bm)
                dbm = _dot(dg_t * lmat_t, cm)
                z = _dot(cm, st)
                dz = dyv * e_c
                dcs = dcs + rowsum(dz * z)
                dcm = dcm + _dot(dz, st, tb=True)
                dstate[r] = _dot(cm_t, dz) + e_last * dst
                dcs_last = jnp.sum(rowsum(dst * st), axis=0, keepdims=True) * jnp.max(e_last, axis=1, keepdims=True)
                dbm = dbm + _dot(w, dst, tb=True)
                dw = _dot(bm, dst)
                dxdt = dxdt + dw * f_c
                q = rowsum(dw * w)
                dcs = dcs - q
                dcs_last = dcs_last + jnp.sum(q, axis=0, keepdims=True)
                dx_ref[:, lanes] = dxdt * dt_c + dyv * d_ref[r]
                ddt = rowsum(dxdt * xv)
                dcs_full = jnp.broadcast_to(dcs, (SSD_CHUNK, SSD_CHUNK)) + jnp.where(row == SSD_CHUNK - 1, dcs_last, 0.0)
                da = jnp.max(_rev_cumsum_rows(dcs_full, row), axis=1, keepdims=True)
                dt_col = jnp.max(dt_c, axis=1, keepdims=True)
                draw = (ddt + da * a_head) * _sigmoid(dtraw_c + bias)
                ddt_ref[r] = draw
                dbias_ref[r] += tot(draw)
                dalog_ref[r] += tot(da * dt_col) * a_head
                dd_ref[r] += tot(rowsum(dyv * xv))
                dbm_sum = dbm if dbm_sum is None else dbm_sum + dbm
                dcm_sum = dcm if dcm_sum is None else dcm_sum + dcm
            db_ref[:, glanes] = dbm_sum
            dc_ref[:, glanes] = dcm_sum

    sp = _ssd_specs(nc, lambda ci: nc - 1 - ci)
    return pl.pallas_call(
        body, name=name, grid=(nc,),
        in_specs=[sp["x"], sp["b"], sp["c"], sp["dtcol"], sp["dtrow"], sp["scal"], sp["scal"], sp["scal"],
                  sp["state"], sp["x"]],
        out_specs=[sp["x"], sp["group"], sp["group"], sp["dtcol"], sp["pacc"], sp["pacc"], sp["pacc"]],
        out_shape=[jax.ShapeDtypeStruct((s, N_HEADS * LANES), F32),
                   jax.ShapeDtypeStruct((s, 2 * LANES), F32),
                   jax.ShapeDtypeStruct((s, 2 * LANES), F32),
                   jax.ShapeDtypeStruct((N_HEADS, s, 1), F32),
                   jax.ShapeDtypeStruct((N_HEADS, 1, LANES), F32),
                   jax.ShapeDtypeStruct((N_HEADS, 1, LANES), F32),
                   jax.ShapeDtypeStruct((N_HEADS, 1, LANES), F32)],
        scratch_shapes=[pltpu.VMEM((N_HEADS, LANES, LANES), F32)],
        compiler_params=_cparams(("arbitrary",)),
    )(xbc, xbc, xbc, dtcol, dtrow, bias, a_log, dskip, states, dy)


def _att_tile(s):
    return _pick(s, (512, 256, 128))


def _tri(t, transposed=False):
    r = lax.broadcasted_iota(jnp.int32, (t, t), 0)
    c = lax.broadcasted_iota(jnp.int32, (t, t), 1)
    return (r <= c) if transposed else (c <= r)


def _rows_at(ref, blk, t):
    return ref[pl.ds(pl.multiple_of(blk * t, t), t), :]


def _flash_fwd(q, k, v, *, name):
    s = q.shape[0]
    t = _att_tile(s)
    nq = s // t

    def body(q_ref, k_ref, v_ref, o_ref, lse_ref):
        i = pl.program_id(1)
        qv = q_ref[...]

        def step(j, carry, diagonal):
            m_old, l_old, acc = carry
            sc = _dot(qv, _rows_at(k_ref, j, t), tb=True)
            if diagonal:
                sc = jnp.where(_tri(t), sc, -jnp.inf)
            m_new = jnp.maximum(m_old, jnp.max(sc, axis=1, keepdims=True))
            alpha = jnp.exp(m_old - m_new)
            p = jnp.exp(sc - m_new)
            return (m_new, alpha * l_old + jnp.sum(p, axis=1, keepdims=True),
                    alpha * acc + _dot(p, _rows_at(v_ref, j, t)))

        init = (jnp.full((t, 1), -jnp.inf, F32), jnp.zeros((t, 1), F32), jnp.zeros((t, LANES), F32))
        carry = lax.fori_loop(0, i, lambda j, c: step(j, c, False), init)
        m_fin, l_fin, acc = step(i, carry, True)
        o_ref[...] = (acc / l_fin).astype(o_ref.dtype)
        lse_ref[0] = m_fin + jnp.log(l_fin)

    q_spec = pl.BlockSpec((t, LANES), lambda h, i: (i, h))
    kv_spec = pl.BlockSpec((s, LANES), lambda h, i: (0, h))
    return pl.pallas_call(
        body, name=name, grid=(N_HEADS, nq),
        in_specs=[q_spec, kv_spec, kv_spec],
        out_specs=[q_spec, pl.BlockSpec((1, t, 1), lambda h, i: (h, i, 0))],
        out_shape=[jax.ShapeDtypeStruct(q.shape, BF16), jax.ShapeDtypeStruct((N_HEADS, s, 1), F32)],
        compiler_params=_cparams(("parallel", "arbitrary")),
    )(q, k, v)


def _att_delta(o, do, *, name):
    s = o.shape[0]
    t = _att_tile(s)

    def body(o_ref, do_ref, dl_ref):
        dl_ref[0] = jnp.sum(do_ref[...].astype(F32) * o_ref[...].astype(F32), axis=1, keepdims=True)

    blk = pl.BlockSpec((t, LANES), lambda h, i: (i, h))
    return pl.pallas_call(
        body, name=name, grid=(N_HEADS, s // t), in_specs=[blk, blk],
        out_specs=pl.BlockSpec((1, t, 1), lambda h, i: (h, i, 0)),
        out_shape=jax.ShapeDtypeStruct((N_HEADS, s, 1), F32),
        compiler_params=_cparams(("parallel", "parallel")),
    )(o, do)


def _flash_bwd(q, k, v, do, lse_row, delta_row, *, name):
    s = q.shape[0]
    t = _att_tile(s)
    nq = s // t

    def body(q_ref, k_ref, v_ref, do_ref, lse_ref, dl_ref, dq_ref, dk_ref, dv_ref):
        j = pl.program_id(1)
        kv, vv = k_ref[...], v_ref[...]

        @pl.when(j == 0)
        def _():
            dq_ref[...] = jnp.zeros_like(dq_ref)

        def step(i, carry, diagonal):
            dk, dv = carry
            rows = pl.ds(pl.multiple_of(i * t, t), t)
            qi, doi = q_ref[rows, :], do_ref[rows, :]
            p_t = jnp.exp(_dot(kv, qi, tb=True) - lse_ref[0, :, rows])
            if diagonal:
                p_t = jnp.where(_tri(t, transposed=True), p_t, 0.0)
            ds_t = (p_t * (_dot(vv, doi, tb=True) - dl_ref[0, :, rows])).astype(MXU_DTYPE)
            dq_ref[rows, :] += _dot(ds_t, kv, ta=True)
            return dk + _dot(ds_t, qi), dv + _dot(p_t, doi)

        zero = jnp.zeros((t, LANES), F32)
        carry = step(j, (zero, zero), True)
        dk, dv = lax.fori_loop(j + 1, nq, lambda i, c: step(i, c, False), carry)
        dk_ref[...] = dk
        dv_ref[...] = dv

        @pl.when(j == nq - 1)
        def _():
            dq_ref[...] = dq_ref[...] * ATT_SCALE

    q_spec = pl.BlockSpec((s, LANES), lambda h, j: (0, h))
    kv_spec = pl.BlockSpec((t, LANES), lambda h, j: (j, h))
    row_spec = pl.BlockSpec((1, 1, s), lambda h, j: (h, 0, 0))
    return pl.pallas_call(
        body, name=name, grid=(N_HEADS, nq),
        in_specs=[q_spec, kv_spec, kv_spec, q_spec, row_spec, row_spec],
        out_specs=[q_spec, kv_spec, kv_spec],
        out_shape=[jax.ShapeDtypeStruct(q.shape, F32)] * 3,
        compiler_params=_cparams(("parallel", "arbitrary")),
    )(q, k, v, do, lse_row, delta_row)


def _rope(v, cos_t, sin_p, sin_m):
    return v * cos_t + pltpu.roll(v, QK_ROPE // 2, 1) * sin_p + pltpu.roll(v, LANES - QK_ROPE // 2, 1) * sin_m


def _rope_t(d, cos_t, sin_p, sin_m):
    return d * cos_t + pltpu.roll(d * sin_p, LANES - QK_ROPE // 2, 1) + pltpu.roll(d * sin_m, QK_ROPE // 2, 1)


def _att_prep(q_pad, kv2, kr, cos_t, sin_p, sin_m, *, name):
    w = N_HEADS * LANES

    def fn(qv, kvv, krv, c, sp, sm):
        kr_rot = _rope(krv, c, sp, sm)
        qs, ks = [], []
        for h in range(N_HEADS):
            blk = slice(h * LANES, (h + 1) * LANES)
            qs.append(_rope(qv[:, blk], c, sp, sm) * ATT_SCALE)
            ks.append(kvv[:, blk] + kr_rot)
        return jnp.concatenate(qs, axis=1), jnp.concatenate(ks, axis=1), kvv[:, w:]

    return _rowwise(fn, [q_pad, kv2, kr, cos_t, sin_p, sin_m], [],
                    [(w, BF16, "row"), (w, BF16, "row"), (w, BF16, "row")], name=name)


def _att_prep_bwd(dq, dk, cos_t, sin_p, sin_m, *, name):
    w = N_HEADS * LANES

    def fn(dqv, dkv, c, sp, sm):
        outs, dkr = [], None
        for h in range(N_HEADS):
            blk = slice(h * LANES, (h + 1) * LANES)
            outs.append(_rope_t(dqv[:, blk], c, sp, sm))
            dkr = dkv[:, blk] if dkr is None else dkr + dkv[:, blk]
        return jnp.concatenate(outs, axis=1), _rope_t(dkr, c, sp, sm)

    return _rowwise(fn, [dq, dk, cos_t, sin_p, sin_m], [], [(w, BF16, "row"), (LANES, F32, "row")], name=name)


_ANY = pl.BlockSpec(memory_space=pl.ANY)
_MESH = pl.DeviceIdType.MESH


def _mesh_pos():
    return lax.axis_index("x"), lax.axis_index("y"), lax.axis_index("c")


def _remote(src, dst, send_sem, recv_sem, dev):
    return pltpu.make_async_remote_copy(src_ref=src, dst_ref=dst, send_sem=send_sem, recv_sem=recv_sem,
                                        device_id=dev, device_id_type=_MESH)


def _other_chips(x, y):
    chips = [(1 - x, y), (x, 1 - y), (1 - x, 1 - y)]
    return chips, [2 * cx + cy for cx, cy in chips]


def _comm_call(body, ins, out_shapes, n_sems, *, name):
    return pl.pallas_call(
        body, name=name, in_specs=[_ANY] * len(ins), out_specs=[_ANY] * len(out_shapes), out_shape=out_shapes,
        scratch_shapes=[pltpu.SemaphoreType.DMA((k,)) for k in n_sems],
    )(*ins)


def _gather_halves(shards):
    n = len(shards)
    halves = [t.shape[0] // 2 for t in shards]

    def body(*refs):
        xs, outs = refs[:n], refs[n:2 * n]
        send_sems, recv_sems = refs[2 * n:]
        x, y, c = _mesh_pos()
        k = 2 * x + y
        sibling = (x, y, 1 - c)
        chips, ks = _other_chips(x, y)
        half = lambda w, hf: pl.ds(hf * halves[w], halves[w])
        first = [_remote(xs[w].at[half(w, c)], outs[w].at[k, half(w, c)], send_sems.at[6 * w + j], recv_sems.at[6 * w + j],
                         (*chips[j], c)) for w in range(n) for j in range(3)]
        for cp in first:
            cp.start()
        passed = []
        for j in range(3):
            for w in range(n):
                land = outs[w].at[ks[j], half(w, c)]
                _remote(land, land, send_sems.at[6 * w + j], recv_sems.at[6 * w + j], sibling).wait_recv()
                passed.append(_remote(land, land, send_sems.at[6 * w + 3 + j], recv_sems.at[6 * w + 3 + j], sibling))
                passed[-1].start()
        for j in range(3):
            for w in range(n):
                land = outs[w].at[ks[j], half(w, 1 - c)]
                _remote(land, land, send_sems.at[6 * w + 3 + j], recv_sems.at[6 * w + 3 + j], sibling).wait_recv()
        for cp in first + passed:
            cp.wait_send()

    shapes = [jax.ShapeDtypeStruct((4,) + t.shape, t.dtype) for t in shards]
    return _comm_call(body, shards, shapes, (6 * n, 6 * n), name="gather_halves")


_HBM = pl.BlockSpec(memory_space=pltpu.HBM)
_SEM = pl.BlockSpec(memory_space=pltpu.SEMAPHORE)
_EFFECT = pltpu.SideEffectType.DATAFLOW_SIDE_EFFECTING


def _push_start(blocks, *, scatter, name):
    n = len(blocks)

    def body(*refs):
        xs, lands = refs[:n], refs[n:2 * n]
        send_sems, recv_sems = refs[2 * n], refs[2 * n + 1]
        token = refs[-1]
        x, y, c = _mesh_pos()
        k = 2 * x + y
        chips, ks = _other_chips(x, y)
        for w in range(n):
            for j in range(3):
                src = xs[w].at[ks[j]] if scatter else xs[w]
                _remote(src, lands[w].at[k], send_sems.at[3 * w + j], recv_sems.at[3 * w + j], (*chips[j], c)).start()
        token[...] = jnp.zeros_like(token)

    hbm = lambda shape, dtype: pltpu.with_memory_space_constraint(lax.empty(shape, dtype), pltpu.HBM)
    ins = [pltpu.with_memory_space_constraint(t, pltpu.HBM) for t in blocks]
    ins += [hbm(t.shape if scatter else (4,) + t.shape, t.dtype) for t in blocks]
    out_shape = [pltpu.SemaphoreType.DMA((3 * n,)), pltpu.SemaphoreType.DMA((3 * n,))]
    out_shape += [pltpu.HBM(t.shape, t.dtype) for t in ins]
    out_shape += [jax.ShapeDtypeStruct((8, LANES), F32)]
    res = pl.pallas_call(
        body, name=name, out_shape=out_shape, in_specs=[_HBM] * (2 * n),
        out_specs=[_SEM, _SEM] + [_HBM] * (2 * n) + [pl.BlockSpec(memory_space=pltpu.VMEM)],
        input_output_aliases={i: 2 + i for i in range(2 * n)},
        compiler_params=pltpu.CompilerParams(has_side_effects=_EFFECT),
    )(*ins)
    return res[0], res[1], res[2:2 + n], res[2 + n:2 + 2 * n], res[-1]


def _push_wait(send_sems, recv_sems, blocks, lands, after, *, name):
    n = len(blocks)

    def body(*refs):
        lands_in = refs[n:2 * n]
        send_sems, recv_sems = refs[2 * n], refs[2 * n + 1]
        x, y, c = _mesh_pos()
        chips, ks = _other_chips(x, y)
        for w in range(n):
            for j in range(3):
                slot = lands_in[w].at[ks[j]]
                cp = _remote(slot, slot, send_sems.at[3 * w + j], recv_sems.at[3 * w + j], (*chips[j], c))
                cp.wait_send()
                cp.wait_recv()

    out_shape = [pltpu.HBM(t.shape, t.dtype) for t in list(blocks) + list(lands)]
    res = pl.pallas_call(
        body, name=name, out_shape=out_shape,
        in_specs=[_HBM] * (2 * n) + [_SEM, _SEM, pl.BlockSpec(memory_space=pl.ANY)], out_specs=[_HBM] * (2 * n),
        input_output_aliases={i: i for i in range(2 * n)},
        compiler_params=pltpu.CompilerParams(has_side_effects=_EFFECT),
    )(*blocks, *lands, send_sems, recv_sems, after)
    return res[:n], res[n:]


def _send_half(views, *, name):
    n = len(views)

    def body(*refs):
        vs, outs = refs[:n], refs[n:2 * n]
        send_sems, recv_sems = refs[2 * n:]
        x, y, c = _mesh_pos()
        cps = []
        for w in range(n):
            h = views[w].shape[1] // 2
            cps.append(_remote(vs[w].at[:, pl.ds((1 - c) * h, h), :], outs[w], send_sems.at[w], recv_sems.at[w],
                               (x, y, 1 - c)))
            cps[-1].start()
        for cp in cps:
            cp.wait()

    shapes = [jax.ShapeDtypeStruct((t.shape[0], t.shape[1] // 2, t.shape[2]), t.dtype) for t in views]
    return _comm_call(body, views, shapes, (n, n), name=name)


def _swap_with_sibling(mine, *, name):
    n = len(mine)

    def body(*refs):
        hs, outs = refs[:n], refs[n:2 * n]
        send_sems, recv_sems = refs[2 * n:]
        x, y, c = _mesh_pos()
        cps = [_remote(hs[w], outs[w], send_sems.at[w], recv_sems.at[w], (x, y, 1 - c)) for w in range(n)]
        for cp in cps:
            cp.start()
        for cp in cps:
            cp.wait()

    shapes = [jax.ShapeDtypeStruct(t.shape, t.dtype) for t in mine]
    return _comm_call(body, mine, shapes, (n, n), name=name)


def _gather_all(vec, *, name):
    r, w = vec.shape

    def body(v_ref, out_ref, send_sems, recv_sems):
        x, y, c = _mesh_pos()

        def slot(px, py, pc):
            return out_ref.at[4 * px + 2 * py + pc]

        peers = []
        for rel in range(1, 8):
            fx, fy, fc = (rel >> 2) & 1, (rel >> 1) & 1, rel & 1
            peers.append((x ^ fx, y ^ fy, c ^ fc))
        cps = [_remote(v_ref, slot(x, y, c), send_sems.at[j], recv_sems.at[j], peer) for j, peer in enumerate(peers)]
        for cp in cps:
            cp.start()
        for j, peer in enumerate(peers):
            _remote(slot(*peer), slot(*peer), send_sems.at[j], recv_sems.at[j], peer).wait_recv()
        for cp in cps:
            cp.wait_send()

    others = pl.pallas_call(
        body, name=name, in_specs=[_ANY], out_specs=_ANY,
        out_shape=jax.ShapeDtypeStruct((8, r, w), vec.dtype),
        scratch_shapes=[pltpu.SemaphoreType.DMA((7,)), pltpu.SemaphoreType.DMA((7,))],
    )(vec)
    me = 4 * lax.axis_index("x") + 2 * lax.axis_index("y") + lax.axis_index("c")
    return lax.dynamic_update_index_in_dim(others, vec, me, 0)


def _gather_same_core(vec, *, name):
    r, w = vec.shape

    def body(v_ref, out_ref, send_sems, recv_sems):
        x, y, c = _mesh_pos()
        k = 2 * x + y
        chips, ks = _other_chips(x, y)
        cps = [_remote(v_ref, out_ref.at[k], send_sems.at[j], recv_sems.at[j], (*chips[j], c)) for j in range(3)]
        for cp in cps:
            cp.start()
        for j in range(3):
            slot = out_ref.at[ks[j]]
            _remote(slot, slot, send_sems.at[j], recv_sems.at[j], (*chips[j], c)).wait_recv()
        for cp in cps:
            cp.wait_send()

    others = pl.pallas_call(
        body, name=name, in_specs=[_ANY], out_specs=_ANY,
        out_shape=jax.ShapeDtypeStruct((4, r, w), vec.dtype),
        scratch_shapes=[pltpu.SemaphoreType.DMA((3,)), pltpu.SemaphoreType.DMA((3,))],
    )(vec)
    k_chip = 2 * lax.axis_index("x") + lax.axis_index("y")
    return lax.dynamic_update_index_in_dim(others, vec, k_chip, 0)


def _row_tile(rows, row_bytes):
    for tm in (1024, 512, 256, 128, 64, 32, 16):
        if rows % tm == 0 and tm * row_bytes <= ELEMENTWISE_BLOCK_BYTES:
            return tm
    return 16 if rows % 16 == 0 else rows


def _chip_sum_half(g, got, c, *, name):
    nb, r, w = g.shape
    half = r // 2
    tm = _row_tile(half, w * 4)
    per = half // tm

    def body(c_ref, g_ref, o_ref, out_ref):
        out_ref[...] = (g_ref[...] + o_ref[...]).astype(out_ref.dtype)

    return pl.pallas_call(
        body, name=name,
        grid_spec=pltpu.PrefetchScalarGridSpec(
            num_scalar_prefetch=1, grid=(nb, per),
            in_specs=[pl.BlockSpec((1, tm, w), lambda b, i, c_ref: (b, c_ref[0] * per + i, 0)),
                      pl.BlockSpec((1, tm, w), lambda b, i, c_ref: (b, i, 0))],
            out_specs=pl.BlockSpec((1, tm, w), lambda b, i, c_ref: (b, i, 0))),
        out_shape=jax.ShapeDtypeStruct((nb, half, w), BF16),
        compiler_params=_cparams(("parallel", "parallel")),
    )(jnp.reshape(c, (1,)).astype(jnp.int32), g, got)


def _sum_slots(stack, *, name):
    n, r, w = stack.shape
    tm = _row_tile(r, n * w * stack.dtype.itemsize)

    def body(s_ref, out_ref):
        acc = s_ref[0].astype(F32)
        for i in range(1, n):
            acc = acc + s_ref[i].astype(F32)
        out_ref[...] = acc

    return pl.pallas_call(
        body, name=name, grid=(r // tm,),
        in_specs=[pl.BlockSpec((n, tm, w), lambda i: (0, i, 0))],
        out_specs=pl.BlockSpec((tm, w), lambda i: (i, 0)),
        out_shape=jax.ShapeDtypeStruct((r, w), F32),
        compiler_params=_cparams(("parallel",)),
    )(stack)


def _adam_math(wv, gv, mv, vv):
    m_new = ADAM_B1 * mv + (1.0 - ADAM_B1) * gv
    v_new = ADAM_B2 * vv + (1.0 - ADAM_B2) * (gv * gv)
    m_hat = m_new / (1.0 - ADAM_B1 ** ADAM_STEP)
    v_hat = v_new / (1.0 - ADAM_B2 ** ADAM_STEP)
    delta = -ADAM_LR * (m_hat / (jnp.sqrt(v_hat) + ADAM_EPS) + ADAM_WD * wv)
    return delta, m_new, v_new


def _adamw(w, g, m, v, *, name):
    shape = w.shape
    cols = shape[-1]
    flat = lambda t: t.reshape(-1, cols)
    rows = flat(w).shape[0]
    tm = _pick(rows, (256, 128, 64, 32, 16, 8))
    outs = _rowwise(_adam_math, [flat(w), flat(g), flat(m), flat(v)], [], [(cols, F32, "row")] * 3, name=name, tm=tm)
    return tuple(o.reshape(shape) for o in outs)


def _adamw_slots(w, slots, m, v, c, *, name):
    shape = w.shape
    cols = shape[-1]
    half = slots[0][0].shape[1]
    v4 = lambda t: t.reshape(2, 2, half, cols)
    assert all(s.shape == (4, half, cols) for pair in slots for s in pair) and w.size == 4 * half * cols
    tm = _row_tile(half, cols * 4 * 4)

    def body(c_ref, w_ref, m0_ref, o0_ref, m1_ref, o1_ref, m_ref, v_ref, g_ref, d_ref, mo_ref, vo_ref):
        first = pl.program_id(0) == 0
        own = pl.program_id(1) == c_ref[0]
        g = None
        for i in range(4):
            part = jnp.where(first, jnp.where(own, m0_ref[i], o0_ref[i]), jnp.where(own, m1_ref[i], o1_ref[i]))
            g = part.astype(F32) if g is None else g + part.astype(F32)
        delta, m_new, v_new = _adam_math(w_ref[0, 0], g, m_ref[0, 0], v_ref[0, 0])
        g_ref[0, 0], d_ref[0, 0], mo_ref[0, 0], vo_ref[0, 0] = g, delta, m_new, v_new

    blk = pl.BlockSpec((1, 1, tm, cols), lambda l, hf, i, c_ref: (l, hf, i, 0))

    def slot_spec(layer, mine):
        def index(l, hf, i, c_ref):
            same_half = hf * c_ref[0] + (1 - hf) * (1 - c_ref[0])
            use = (l if layer else 1 - l) * (same_half if mine else 1 - same_half)
            return (0, i * use, 0)
        return pl.BlockSpec((4, tm, cols), index)

    outs = pl.pallas_call(
        body, name=name,
        grid_spec=pltpu.PrefetchScalarGridSpec(
            num_scalar_prefetch=1, grid=(2, 2, half // tm),
            in_specs=[blk, slot_spec(0, True), slot_spec(0, False), slot_spec(1, True), slot_spec(1, False), blk, blk],
            out_specs=[blk] * 4),
        out_shape=[jax.ShapeDtypeStruct((2, 2, half, cols), F32)] * 4,
        compiler_params=_cparams(("arbitrary", "arbitrary", "arbitrary")),
    )(jnp.reshape(c, (1,)).astype(jnp.int32), v4(w), slots[0][0], slots[0][1], slots[1][0], slots[1][1], v4(m), v4(v))
    return tuple(o.reshape(shape) for o in outs)


def _pad_blocks(w, axis, n_blocks, real, to=LANES, offset=0):
    axis = axis % w.ndim
    shp = w.shape
    w = w.reshape(shp[:axis] + (n_blocks, real) + shp[axis + 1:])
    pads = [(0, 0)] * w.ndim
    pads[axis + 1] = (offset, to - real - offset)
    w = jnp.pad(w, pads)
    return w.reshape(shp[:axis] + (n_blocks * to,) + shp[axis + 1:])


def _unpad_blocks(w, axis, n_blocks, real, to=LANES, offset=0):
    axis = axis % w.ndim
    shp = w.shape
    w = w.reshape(shp[:axis] + (n_blocks, to) + shp[axis + 1:])
    w = lax.slice_in_dim(w, offset, offset + real, axis=axis + 1)
    return w.reshape(shp[:axis] + (n_blocks * real,) + shp[axis + 1:])


def _block_diag(w):
    n, a, b = w.shape
    eye = jnp.eye(n, dtype=w.dtype)
    return (eye[:, None, :, None] * w[:, :, None, :]).reshape(n * a, n * b)


def _block_diag_t(d, n):
    a, b = d.shape[0] // n, d.shape[1] // n
    d = d.reshape(n, a, n, b)
    return jnp.stack([d[i, :, i, :] for i in range(n)])


_SPLITS = np.cumsum((0,) + SPLIT_SIZES)


def _w_in_groups(w_in):
    sl = lambda i: w_in[:, _SPLITS[i]:_SPLITS[i + 1]]
    xbc = sl(5)
    xbc_pad = jnp.concatenate([_pad_blocks(xbc[:, :MIX], 1, N_HEADS, HEAD),
                               _pad_blocks(xbc[:, MIX:MIX + 2 * HEAD], 1, 2, HEAD),
                               _pad_blocks(xbc[:, MIX + 2 * HEAD:], 1, 2, HEAD)], axis=1)
    return dict(
        cq=sl(0), ckv=sl(1), kr=_pad_blocks(sl(2), 1, 1, QK_ROPE, offset=HEAD), pool=sl(3),
        z=_pad_blocks(sl(4), 1, N_HEADS, HEAD), xbc=xbc_pad, dt=_pad_blocks(sl(6), 1, 1, N_HEADS),
        lru_g=sl(7), lru_x=sl(8), gates=sl(9))


def _w_in_fused(groups):
    parts, at = [], 0
    for name, off, width in IN_LAYOUT:
        assert groups[name].shape[1] == width and off >= at
        if off > at:
            parts.append(jnp.zeros((groups[name].shape[0], off - at), groups[name].dtype))
        parts.append(groups[name])
        at = off + width
    parts.append(jnp.zeros((parts[0].shape[0], IN_ALL_COLS - at), parts[0].dtype))
    return jnp.concatenate(parts, axis=1)


def _in_cols(arr, name):
    off, width = IN_OFFSETS[name]
    return _Cols(arr, off, width)


def _w_in_ungroup(d):
    xbc = d["xbc"]
    w = N_HEADS * LANES
    xbc_real = jnp.concatenate([_unpad_blocks(xbc[:, :w], 1, N_HEADS, HEAD),
                                _unpad_blocks(xbc[:, w:w + 2 * LANES], 1, 2, HEAD),
                                _unpad_blocks(xbc[:, w + 2 * LANES:], 1, 2, HEAD)], axis=1)
    return jnp.concatenate([d["cq"], d["ckv"], _unpad_blocks(d["kr"], 1, 1, QK_ROPE, offset=HEAD), d["pool"],
                            _unpad_blocks(d["z"], 1, N_HEADS, HEAD), xbc_real, _unpad_blocks(d["dt"], 1, 1, N_HEADS),
                            d["lru_g"], d["lru_x"], d["gates"]], axis=1)


def _pad_xbc_vec(v):
    return jnp.concatenate([_pad_blocks(v[..., :MIX], -1, N_HEADS, HEAD),
                            _pad_blocks(v[..., MIX:MIX + 2 * HEAD], -1, 2, HEAD),
                            _pad_blocks(v[..., MIX + 2 * HEAD:], -1, 2, HEAD)], axis=-1)


def _unpad_xbc_vec(v):
    w = N_HEADS * LANES
    return jnp.concatenate([_unpad_blocks(v[..., :w], -1, N_HEADS, HEAD),
                            _unpad_blocks(v[..., w:w + 2 * LANES], -1, 2, HEAD),
                            _unpad_blocks(v[..., w + 2 * LANES:], -1, 2, HEAD)], axis=-1)


def _layer_weights(p):
    q = dict(p)
    q["in_all"] = _w_in_fused(_w_in_groups(p["w_in"]))
    q["uq"] = _pad_blocks(p["w_uq"], 1, N_HEADS, HEAD + QK_ROPE)
    ukv = p["w_ukv"].reshape(KV_LORA, N_HEADS, 2 * HEAD)
    q["ukv"] = jnp.concatenate([_pad_blocks(ukv[:, :, :HEAD].reshape(KV_LORA, -1), 1, N_HEADS, HEAD),
                                _pad_blocks(ukv[:, :, HEAD:].reshape(KV_LORA, -1), 1, N_HEADS, HEAD)], axis=1)
    q["pool_bd"] = _block_diag(p["w_pool"])
    q["lru_bd"] = jnp.concatenate([_block_diag(p["lru_w_a"]), _block_diag(p["lru_w_i"])], axis=1)
    q["br"] = [_pad_blocks(p["w_branch"][0], 0, N_HEADS, HEAD), p["w_branch"][1],
               _pad_blocks(p["w_branch"][2], 0, N_HEADS, HEAD), p["w_branch"][3]]
    q["ssd_conv_w_pad"] = _pad_xbc_vec(p["ssd_conv_w"])
    q["ssd_conv_b_pad"] = _pad_xbc_vec(p["ssd_conv_b"])[None, :]
    q["ssd_norm_pad"] = _pad_blocks(p["ssd_norm"], 0, N_HEADS, HEAD)[None, :]
    return q


def _row(v):
    return v.reshape(1, -1)


def _scal3(v):
    return v.reshape(N_HEADS, 1, 1)


def _layer_fwd(x, p_emb, w, rope, tag):
    n = lambda s: f"{s}_{tag}"
    sv = {"x": x}
    h = _rms_fwd(x, _row(w["g_mix"]), name=n("rms_mix"))
    sv["h"] = h
    u_all = _mm(h, w["in_all"], name=n("in_proj"))
    u = {k: _in_cols(u_all, k) for k in IN_OFFSETS}
    sv["u"] = u

    cqn = _rms_fwd(u["cq"], _row(w["q_norm"]), name=n("rms_q"))
    ckvn = _rms_fwd(u["ckv"], _row(w["kv_norm"]), name=n("rms_kv"))
    q_pad = _mm(cqn, w["uq"], name=n("uq"))
    kv2 = _mm(ckvn, w["ukv"], name=n("ukv"))
    qc, kc, vc = _att_prep(q_pad, kv2, u["kr"], *rope, name=n("att_prep"))
    y_a, lse = _flash_fwd(qc, kc, vc, name=n("flash_fwd"))
    sv.update(cqn=cqn, ckvn=ckvn, qc=qc, kc=kc, vc=vc, y_a=y_a, lse=lse)

    pool_d = _pool_fwd(u["pool"], name=n("pool_fwd"))
    yb_pre, y_b = _mm(pool_d, w["pool_bd"], epilogue=lambda acc, sc: (acc, acc * sc),
                      rowvecs=[_row(w["pool_scale"])], out_dtypes=(F32, BF16), name=n("pool_mm"))
    sv.update(pool_d=pool_d, yb_pre=yb_pre, y_b=y_b)

    xbc_c = _conv_fwd(u["xbc"], w["ssd_conv_w_pad"], w["ssd_conv_b_pad"], silu=True, name=n("ssd_conv"))
    dt8 = lax.slice_in_dim(u_all, IN_OFFSETS["dt"][0], IN_OFFSETS["dt"][0] + N_HEADS, axis=1)
    dtcol = dt8.T[:, :, None]
    dtrow = dt8.T[:, None, :]
    ssd_par = (_scal3(w["ssd_dt_bias"]), _scal3(w["ssd_a_log"]), _scal3(w["ssd_d"]))
    y_ssd, states = _ssd_fwd(xbc_c, dtcol, dtrow, *ssd_par, name=n("ssd_fwd"))

    def ssd_post(yv, zv, gv):
        xh, _ = _rms_parts(yv * _silu(zv), MIX)
        return xh * gv

    y_c = _rowwise(ssd_post, [y_ssd, u["z"]], [w["ssd_norm_pad"]], [(N_HEADS * LANES, BF16, "row")], name=n("ssd_post"))
    sv.update(xbc_c=xbc_c, dtcol=dtcol, dtrow=dtrow, y_ssd=y_ssd, states=states, y_c=y_c)

    xc = _conv_fwd(u["lru_x"], w["lru_conv_w"], _row(w["lru_conv_b"]), silu=False, name=n("lru_conv"))
    pre = _mm(xc, w["lru_bd"], name=n("lru_mm"))
    lru_par = (_row(w["lru_lambda"]), _row(w["lru_b_a"]), _row(w["lru_b_i"]))
    y_d, h_lru = _lru_fwd(pre, xc, u["lru_g"], *lru_par, name=n("lru_fwd"))
    sv.update(xc=xc, pre=pre, h_lru=h_lru, y_d=y_d)

    merged, *ybs = _branch_merge([y_a, y_b, y_c, y_d], w["br"], u_all, name=n("branch_merge"))
    x1 = _mm(merged, w["w_out"], epilogue=lambda acc, xr: (acc + xr,), tiles=[x], name=n("out_proj"))
    sv.update(ybs=ybs, merged=merged, x1=x1)

    h2 = _rms_fwd(x1, _row(w["g_mlp"]), name=n("rms_mlp"))
    a_ff, f_ff = _mm(h2, w["w_ff1"], epilogue=lambda acc: (acc, jnp.square(jnp.maximum(acc, 0.0))),
                     out_dtypes=(BF16, BF16), name=n("ff1"))
    x2 = _mm(f_ff, w["w_ff2"], epilogue=lambda acc, xr: (acc + xr,), tiles=[x1], name=n("ff2"))
    sv.update(h2=h2, a_ff=a_ff, f_ff=f_ff, x2=x2)

    h3 = _rms_fwd(x2, _row(w["g_ple"]), name=n("rms_ple"))
    e_ple = _mm(p_emb, w["w_ple"], name=n("ple_emb"))
    x3, gt_ple = _mm(h3, w["w_ple_gate"], epilogue=lambda acc, ev, xr: (xr + ev * _sigmoid(acc), _sigmoid(acc)),
                     tiles=[e_ple, x2], out_dtypes=(F32, F32), name=n("ple_gate"))
    sv.update(h3=h3, e_ple=e_ple, gt_ple=gt_ple, p_emb=p_emb)
    return x3, sv


def _layer_bwd(dx3, sv, w, rope, tag, early=None):
    n = lambda s: f"{s}_{tag}"
    gr = {}
    u = sv["u"]

    de, dpre = _rowwise(lambda d, gt, ev: (d * gt, d * ev * gt * (1.0 - gt)), [dx3, sv["gt_ple"], sv["e_ple"]], [],
                        [(D_MODEL, BF16, "row"), (D_MODEL, BF16, "row")], name=n("ple_bwd"))
    gr["w_ple"] = _mm(sv["p_emb"], de, ta=True, name=n("d_w_ple"))
    gr["w_ple_gate"] = _mm(sv["h3"], dpre, ta=True, name=n("d_w_ple_gate"))
    dh3 = _mm(dpre, w["w_ple_gate"], tb=True, out_dtypes=(BF16,), name=n("d_h3"))
    dx2, dg = _rms_bwd(sv["x2"], _row(w["g_ple"]), dh3, dx3, name=n("rms_ple_bwd"))
    gr["g_ple"] = dg[0]

    gr["w_ff2"] = _mm(sv["f_ff"], dx2, ta=True, name=n("d_w_ff2"))
    da = _mm(dx2, w["w_ff2"], tb=True, epilogue=lambda acc, av: (acc * 2.0 * jnp.maximum(av, 0.0),),
             tiles=[sv["a_ff"]], out_dtypes=(BF16,), name=n("d_a_ff"))
    gr["w_ff1"] = _mm(sv["h2"], da, ta=True, name=n("d_w_ff1"))
    dh2 = _mm(da, w["w_ff1"], tb=True, out_dtypes=(BF16,), name=n("d_h2"))
    dx1, dg = _rms_bwd(sv["x1"], _row(w["g_mlp"]), dh2, dx2, name=n("rms_mlp_bwd"))
    gr["g_mlp"] = dg[0]
    if early is not None:
        dx1 = early(gr, dx1)

    gr["w_out"] = _mm(sv["merged"], dx1, ta=True, name=n("d_w_out"))
    dmerged = _mm(dx1, w["w_out"], tb=True, name=n("d_merged"))

    def merge_bwd(dm, gts, y0, y1, y2, y3):
        dys, dgs = [], []
        for b, yb in enumerate((y0, y1, y2, y3)):
            sg = _sigmoid(gts[:, b * D_MODEL:(b + 1) * D_MODEL])
            dys.append(dm * sg)
            dgs.append(dm * yb * sg * (1.0 - sg))
        return (*dys, jnp.concatenate(dgs, axis=1))

    *dybs, dgates = _rowwise(merge_bwd, [dmerged, u["gates"]] + sv["ybs"], [],
                             [(D_MODEL, BF16, "row")] * 4 + [(4 * D_MODEL, BF16, "row")], name=n("merge_bwd"))
    ys = [sv["y_a"], sv["y_b"], sv["y_c"], sv["y_d"]]
    dwb = [_mm(ys[b], dybs[b], ta=True, name=n(f"d_w_branch{b}")) for b in range(4)]
    gr["w_branch"] = jnp.stack([_unpad_blocks(dwb[0], 0, N_HEADS, HEAD), dwb[1],
                                _unpad_blocks(dwb[2], 0, N_HEADS, HEAD), dwb[3]])
    dy_a = _mm(dybs[0], w["br"][0], tb=True, out_dtypes=(BF16,), name=n("d_y_a"))
    dy_b = _mm(dybs[1], w["br"][1], tb=True, name=n("d_y_b"))
    dy_c = _mm(dybs[2], w["br"][2], tb=True, name=n("d_y_c"))
    dy_d = _mm(dybs[3], w["br"][3], tb=True, name=n("d_y_d"))
    du = {"gates": dgates}

    lru_par = (_row(w["lru_lambda"]), _row(w["lru_b_a"]), _row(w["lru_b_i"]))
    dpa, dpi, dxc_direct, du["lru_g"], dlam, dba, dbi = _lru_bwd(
        sv["pre"], sv["xc"], u["lru_g"], *lru_par, sv["h_lru"], dy_d, name=n("lru_bwd"))
    dpre_lru = jnp.concatenate([dpa, dpi], axis=1)
    d_bd = _mm(sv["xc"], dpre_lru, ta=True, name=n("d_lru_w"))
    gr["lru_w_a"] = _block_diag_t(d_bd[:, :MIX], N_HEADS)
    gr["lru_w_i"] = _block_diag_t(d_bd[:, MIX:], N_HEADS)
    gr["lru_lambda"], gr["lru_b_a"], gr["lru_b_i"] = dlam[0], dba[0], dbi[0]
    dxc = _mm(dpre_lru, w["lru_bd"], tb=True, epilogue=lambda acc, t: (acc + t,), tiles=[dxc_direct], name=n("d_xc"))
    du["lru_x"], gr["lru_conv_w"], dcb = _conv_bwd(u["lru_x"], w["lru_conv_w"], _row(w["lru_conv_b"]), dxc,
                                                  silu=False, name=n("lru_conv_bwd"))
    gr["lru_conv_b"] = dcb[0]

    def ssd_post_bwd(dyc, yv, zv, gv):
        sz = _silu(zv)
        dyz, dgain = _rms_bwd_math(yv * sz, gv, dyc, MIX)
        return dyz * sz, dyz * yv * _silu_grad(zv), dgain

    dy_ssd, du["z"], dgain = _rowwise(ssd_post_bwd, [dy_c, sv["y_ssd"], u["z"]], [w["ssd_norm_pad"]],
                                      [(N_HEADS * LANES, F32, "row"), (N_HEADS * LANES, BF16, "row"),
                                       (N_HEADS * LANES, F32, "acc")], name=n("ssd_post_bwd"))
    gr["ssd_norm"] = _unpad_blocks(dgain[0], 0, N_HEADS, HEAD)
    ssd_par = (_scal3(w["ssd_dt_bias"]), _scal3(w["ssd_a_log"]), _scal3(w["ssd_d"]))
    dxs, dbg, dcg, ddt, dbias, dalog, dd = _ssd_bwd(sv["xbc_c"], sv["dtcol"], sv["dtrow"], *ssd_par, sv["states"],
                                                    dy_ssd, name=n("ssd_bwd"))
    s = dxs.shape[0]
    dxbc_c = jnp.concatenate([dxs, dbg, dcg], axis=1)
    gr["ssd_dt_bias"], gr["ssd_a_log"], gr["ssd_d"] = dbias[:, 0, 0], dalog[:, 0, 0], dd[:, 0, 0]
    du["xbc"], dcw, dcb = _conv_bwd(u["xbc"], w["ssd_conv_w_pad"], w["ssd_conv_b_pad"], dxbc_c, silu=True,
                                    name=n("ssd_conv_bwd"))
    gr["ssd_conv_w"], gr["ssd_conv_b"] = _unpad_xbc_vec(dcw), _unpad_xbc_vec(dcb[0])
    du["dt"] = jnp.pad(ddt[:, :, 0].T, ((0, 0), (0, LANES - N_HEADS)))

    dyb_pre, dscale = _rowwise(lambda d, yp, sc: (d * sc, _colsum(d * yp)), [dy_b, sv["yb_pre"]],
                               [_row(w["pool_scale"])], [(MIX, BF16, "row"), (MIX, F32, "acc")], name=n("pool_scale_bwd"))
    gr["pool_scale"] = dscale[0]
    gr["w_pool"] = _block_diag_t(_mm(sv["pool_d"], dyb_pre, ta=True, name=n("d_w_pool")), 4)
    dd_pool = _mm(dyb_pre, w["pool_bd"], tb=True, name=n("d_pool_d"))
    du["pool"] = _pool_bwd(dd_pool, name=n("pool_bwd"))

    delta = _att_delta(sv["y_a"], dy_a, name=n("att_delta"))
    to_row = lambda t: t.reshape(N_HEADS, 1, s)
    dqc, dkc, dvc = _flash_bwd(sv["qc"], sv["kc"], sv["vc"], dy_a, to_row(sv["lse"]), to_row(delta), name=n("flash_bwd"))
    dq_pad, du["kr"] = _att_prep_bwd(dqc, dkc, *rope, name=n("att_prep_bwd"))
    d_uq = _mm(sv["cqn"], dq_pad, ta=True, name=n("d_w_uq"))
    gr["w_uq"] = _unpad_blocks(d_uq, 1, N_HEADS, HEAD + QK_ROPE)
    dcqn = _mm(dq_pad, w["uq"], tb=True, out_dtypes=(BF16,), name=n("d_cqn"))
    du["cq"], dg = _rms_bwd(u["cq"], _row(w["q_norm"]), dcqn, name=n("rms_q_bwd"))
    gr["q_norm"] = dg[0]
    dkv2 = jnp.concatenate([dkc, dvc], axis=1).astype(BF16)
    d_ukv = _mm(sv["ckvn"], dkv2, ta=True, name=n("d_w_ukv"))
    wk = N_HEADS * LANES
    dk_real = _unpad_blocks(d_ukv[:, :wk], 1, N_HEADS, HEAD).reshape(KV_LORA, N_HEADS, HEAD)
    dv_real = _unpad_blocks(d_ukv[:, wk:], 1, N_HEADS, HEAD).reshape(KV_LORA, N_HEADS, HEAD)
    gr["w_ukv"] = jnp.concatenate([dk_real, dv_real], axis=2).reshape(KV_LORA, N_HEADS * 2 * HEAD)
    dckvn = _mm(dkv2, w["ukv"], tb=True, out_dtypes=(BF16,), name=n("d_ckvn"))
    du["ckv"], dg = _rms_bwd(u["ckv"], _row(w["kv_norm"]), dckvn, name=n("rms_kv_bwd"))
    gr["kv_norm"] = dg[0]

    du_all = _w_in_fused({k: v.astype(BF16) for k, v in du.items()})
    dw_all = _mm(sv["h"], du_all, ta=True, name=n("d_w_in"))
    gr["w_in"] = _w_in_ungroup({k: dw_all[:, off:off + width] for k, off, width in IN_LAYOUT})
    dh = _mm(du_all, w["in_all"], tb=True, name=n("d_h"))
    dx, dg = _rms_bwd(sv["x"], _row(w["g_mix"]), dh, dx1, name=n("rms_mix_bwd"))
    gr["g_mix"] = dg[0]
    return dx, gr


def _pack_rows(n_elems):
    per = PACK_W * PACK_ROWS
    return -(-n_elems // per) * PACK_ROWS


def _pack_flat(parts, dtype):
    flat = jnp.concatenate([p.reshape(-1).astype(dtype) for p in parts])
    rows = _pack_rows(flat.shape[0])
    return jnp.pad(flat, (0, rows * PACK_W - flat.shape[0])).reshape(rows, PACK_W)


def _unpack_flat(buf, shapes):
    lead = buf.shape[:-2]
    flat = buf.reshape(lead + (-1,))
    out, off = [], 0
    for shp in shapes:
        size = int(np.prod(shp))
        out.append(flat[..., off:off + size].reshape(lead + tuple(shp)))
        off += size
    return out


def _merge_shards(t, axis):
    return jnp.concatenate([t[i] for i in range(4)], axis=axis)


def _split_shards(t, axis):
    return jnp.stack(jnp.split(t, 4, axis=axis))


def _rope_tables(positions):
    inv = 1.0 / (ROPE_THETA ** (jnp.arange(0, QK_ROPE, 2, dtype=F32) / QK_ROPE))
    ang = positions.astype(F32)[:, None] * inv
    cos, sin = jnp.cos(ang), jnp.sin(ang)
    s = ang.shape[0]
    half = QK_ROPE // 2
    z = lambda n_: jnp.zeros((s, n_), F32)
    cos_t = jnp.concatenate([jnp.ones((s, HEAD), F32), cos, cos, jnp.ones((s, LANES - HEAD - QK_ROPE), F32)], axis=1)
    sin_p = jnp.concatenate([z(HEAD + half), sin, z(LANES - HEAD - QK_ROPE)], axis=1)
    sin_m = jnp.concatenate([z(HEAD), -sin, z(half + LANES - HEAD - QK_ROPE)], axis=1)
    return cos_t, sin_p, sin_m


def _loss_head(x, g, target, *, name):
    d = x.shape[1]

    def fn(xv, tv, gv):
        xh, r = _rms_parts(xv, d)
        y = xh * gv
        err = y - tv
        dy = err * (1.0 / d)
        dxh = dy * gv
        dx = r * (dxh - xh * (jnp.sum(dxh * xh, axis=-1, keepdims=True) * (1.0 / d)))
        return dx, _colsum(dy * xh), _colsum(err * err) * (0.5 / d)

    return _rowwise(fn, [x, target], [g], [(d, F32, "row"), (d, F32, "acc"), (d, F32, "acc")], name=name)


MATS = tuple((nm, ax) for nm, ax in BIG if nm not in CONV_SHARDED)
EARLY_L0 = ("w_ple", "w_ple_gate", "w_ff2", "w_ff1")


def _grad_view(g, ax_layer):
    if ax_layer == 0:
        return g.reshape(4, g.shape[0] // 4, g.shape[1])
    return g.reshape(1, -1, g.shape[-1])


def _reduce_start(grads_l, mats, c_idx, tag):
    views = [_grad_view(grads_l[nm], ax - 1) for nm, ax in mats]
    got = _send_half(views, name="send_half_" + tag)
    parts = []
    for (nm, ax), v, gt in zip(mats, views, got):
        both = _chip_sum_half(v, gt, c_idx, name=f"chip_sum_{nm}_{tag}")
        parts.append(both if ax == 1 else _split_shards(both[0], 1))
    return _push_start(parts, scatter=True, name="push_grads_" + tag)


def _reduce_finish(state, after, k_chip, tag):
    send_sems, recv_sems, parts, lands, _ = state
    parts, landed = _push_wait(send_sems, recv_sems, parts, lands, after, name="wait_grads_" + tag)
    mine = [lax.dynamic_update_index_in_dim(t, lax.dynamic_index_in_dim(p, k_chip, 0, keepdims=False), k_chip, 0)
            for t, p in zip(landed, parts)]
    return list(zip(mine, _swap_with_sibling(mine, name="swap_halves_" + tag)))


def _step(args):
    x = args["x"][0]
    c_idx = lax.axis_index("c")
    k_chip = 2 * lax.axis_index("x") + lax.axis_index("y")

    mats = MATS
    mine = [[args[nm][l].astype(BF16) for nm, _ in mats] for l in range(2)]
    gathered0 = _gather_halves(mine[0])
    convs = [(nm, ax) for nm, ax in BIG if nm in CONV_SHARDED]
    conv_all = _gather_all(_pack_flat([args[nm] for nm, _ in convs], F32), name="gather_conv_taps")[0::2]
    mine1, gathered0, conv_all = lax.optimization_barrier((mine[1], gathered0, conv_all))
    gathered0 = [lax.dynamic_update_index_in_dim(t, own, k_chip, 0) for t, own in zip(gathered0, mine[0])]
    push1 = _push_start(mine1, scatter=False, name="push_weights_l1")
    full_conv = {nm: _merge_shards(t, ax)
                 for (nm, ax), t in zip(convs, _unpack_flat(conv_all, [args[nm].shape for nm, _ in convs]))}
    rope = _rope_tables(args["positions"][0])

    def layer_weights(l, gathered):
        p = {nm: _merge_shards(t, ax - 1) for (nm, ax), t in zip(mats, gathered)}
        p.update({nm: full_conv[nm][l] for nm in CONV_SHARDED})
        p.update({nm: args[nm][l] for nm in SMALL if nm != "g_final"})
        return _layer_weights(p)

    layers = [layer_weights(0, gathered0), None]
    layers[0]["g_mix"] = layers[0]["g_mix"] + push1[4][0, 0]
    x, sv0 = _layer_fwd(x, args["p"][0, 0], layers[0], rope, "l0")
    own1, landed1 = _push_wait(push1[0], push1[1], push1[2], push1[3], x, name="wait_weights_l1")
    layers[1] = layer_weights(1, [lax.dynamic_update_index_in_dim(t, own, k_chip, 0) for t, own in zip(landed1, own1)])
    x, sv1 = _layer_fwd(x, args["p"][1, 0], layers[1], rope, "l1")
    saved = [sv0, sv1]

    dx, dg_final, loss_part = _loss_head(x, _row(args["g_final"]), args["loss_target"][0], name="loss_head")
    loss = lax.psum(jnp.sum(loss_part), ("x", "y", "c"))

    early_mats = tuple(mt for mt in MATS if mt[0] in EARLY_L0)
    rest_mats = tuple(mt for mt in MATS if mt[0] not in EARLY_L0)
    grads, reduce0_early = [None, None], []

    def early0(gr, dx1):
        reduce0_early.append(_reduce_start(gr, early_mats, c_idx, "l0e"))
        return dx1 + reduce0_early[0][4][0, 0]

    dx, grads[1] = _layer_bwd(dx, saved[1], layers[1], rope, "l1")
    reduce1 = _reduce_start(grads[1], MATS, c_idx, "l1")
    dx, grads[0] = _layer_bwd(dx + reduce1[4][0, 0], saved[0], layers[0], rope, "l0", early=early0)
    g_all = {nm: jnp.stack([grads[0][nm], grads[1][nm]]) for nm in SMALL + CONV_SHARDED if nm != "g_final"}
    g_all["g_final"] = dg_final[0]
    all_names = SMALL + CONV_SHARDED
    all_shapes = [g_all[nm].shape for nm in all_names]
    packed = _pack_flat([g_all[nm] for nm in all_names], F32)
    sibling = _swap_with_sibling([packed], name="swap_small_grads")[0]
    pair = jnp.where(c_idx == 0, jnp.stack([packed, sibling]), jnp.stack([sibling, packed]))
    small_all = _gather_same_core(_sum_slots(pair, name="sum_cores"), name="gather_small_grads")
    g0_mats, small_all = lax.optimization_barrier(({nm: grads[0][nm] for nm, _ in rest_mats}, small_all))
    reduce0 = _reduce_start(g0_mats, rest_mats, c_idx, "l0")
    small_sum = _sum_slots(small_all, name="sum_chips")
    g_red = dict(zip(all_names, _unpack_flat(small_sum, all_shapes)))
    for nm, ax in BIG:
        if nm in CONV_SHARDED:
            width = args[nm].shape[ax]
            g_red[nm] = lax.dynamic_slice_in_dim(g_red[nm], k_chip * width, width, axis=ax)
    small_shapes = [args[nm].shape for nm in SMALL]
    pack_small = lambda src: _pack_flat([src(nm) for nm in SMALL], F32)
    upd_small = _adamw(pack_small(lambda nm: args[nm]), pack_small(lambda nm: g_red[nm]),
                       pack_small(lambda nm: args["m_" + nm]), pack_small(lambda nm: args["v_" + nm]), name="adamw_small")
    upd = {nm: trip for nm, trip in zip(SMALL, zip(*[_unpack_flat(t, small_shapes) for t in upd_small]))}
    for nm in CONV_SHARDED:
        upd[nm] = _adamw(args[nm], g_red[nm], args["m_" + nm], args["v_" + nm], name="adamw_" + nm)

    names = lambda mats_: [nm for nm, _ in mats_]
    slots0 = dict(zip(names(early_mats), _reduce_finish(reduce0_early[0], dx, k_chip, "l0e")))
    slots0.update(zip(names(rest_mats), _reduce_finish(reduce0, upd_small[0], k_chip, "l0")))
    slots1 = dict(zip(names(MATS), _reduce_finish(reduce1, dx, k_chip, "l1")))
    for nm, _ in MATS:
        g_red[nm], *upd[nm] = _adamw_slots(args[nm], [slots0[nm], slots1[nm]], args["m_" + nm], args["v_" + nm],
                                           c_idx, name="adamw_" + nm)

    outs = [loss, dx[None]]
    outs += [g_red[nm] for nm in WEIGHTS]
    for i in range(3):
        outs += [upd[nm][i] for nm in WEIGHTS]
    return tuple(outs)


_ARG_NAMES = ("x", "p", "positions") + WEIGHTS + ("loss_target",) + tuple("m_" + nm for nm in WEIGHTS) \
    + tuple("v_" + nm for nm in WEIGHTS)


def kernel(*arrays):
    assert len(arrays) == len(_ARG_NAMES), len(arrays)
    return _step(dict(zip(_ARG_NAMES, arrays)))
```

```python
import math

import jax
import jax.numpy as jnp
import numpy as np
from jax import lax
from jax.experimental import pallas as pl
from jax.experimental.pallas import tpu as pltpu

F32 = jnp.float32
BF16 = jnp.bfloat16
MXU_DTYPE = BF16
LANES = 128
VMEM_LIMIT = 56 * 1024 * 1024
MM_VMEM_BUDGET = 36 * 1024 * 1024
ELEMENTWISE_BLOCK_BYTES = 2 * 1024 * 1024

D_MODEL = 1024
N_HEADS = 8
HEAD = 64
QK_ROPE = 32
Q_LORA = 384
KV_LORA = 256
MIX = 512
SSD_CHUNK = 128
CONV_W = 4
POOL_WINDOWS = (2, 4, 8, 16)
LRU_C = 8.0
EPS = 1e-6
ROPE_THETA = 10000.0
ATT_SCALE = (HEAD + QK_ROPE) ** -0.5
SPLIT_SIZES = (Q_LORA, KV_LORA, QK_ROPE, MIX, MIX, 768, N_HEADS, MIX, MIX, 4 * D_MODEL)
IN_LAYOUT = (("gates", 0, 4096), ("z", 4096, 1024), ("pool", 5120, 512), ("lru_g", 5632, 512), ("lru_x", 6144, 512),
             ("cq", 6912, 384), ("ckv", 7424, 256), ("xbc", 7680, 1536), ("kr", 9216, 128), ("dt", 9344, 128))
IN_OFFSETS = {name: (off, width) for name, off, width in IN_LAYOUT}
IN_ALL_COLS = 9728

ADAM_LR, ADAM_B1, ADAM_B2, ADAM_EPS, ADAM_WD, ADAM_STEP = 0.001, 0.9, 0.999, 1e-08, 0.01, 10

BIG = (("w_in", 2), ("w_uq", 2), ("w_ukv", 2), ("ssd_conv_w", 2), ("lru_conv_w", 2), ("w_branch", 3),
       ("w_out", 1), ("w_ff1", 2), ("w_ff2", 1), ("w_ple_gate", 1), ("w_ple", 2))
SMALL = ("g_mix", "q_norm", "kv_norm", "w_pool", "pool_scale", "ssd_conv_b", "ssd_dt_bias", "ssd_a_log",
         "ssd_d", "ssd_norm", "lru_conv_b", "lru_w_a", "lru_b_a", "lru_w_i", "lru_b_i", "lru_lambda",
         "g_mlp", "g_ple", "g_final")
WEIGHTS = ("g_mix", "w_in", "q_norm", "w_uq", "kv_norm", "w_ukv", "w_pool", "pool_scale", "ssd_conv_w",
           "ssd_conv_b", "ssd_dt_bias", "ssd_a_log", "ssd_d", "ssd_norm", "lru_conv_w", "lru_conv_b", "lru_w_a",
           "lru_b_a", "lru_w_i", "lru_b_i", "lru_lambda", "w_branch", "w_out", "g_mlp", "w_ff1", "w_ff2", "g_ple",
           "w_ple_gate", "w_ple", "g_final")
CONV_SHARDED = ("ssd_conv_w", "lru_conv_w")
PACK_W = 1024
PACK_ROWS = 64


def _cparams(sem, vmem=VMEM_LIMIT):
    return pltpu.CompilerParams(dimension_semantics=sem, vmem_limit_bytes=vmem)


def _pick(n, cands):
    for c in cands:
        if n % c == 0:
            return c
    return n


class _Cols:
    def __init__(self, arr, off, width):
        self.arr, self.off, self.width = arr, off, width

    shape = property(lambda self: (self.arr.shape[0], self.width))
    dtype = property(lambda self: self.arr.dtype)


def _arr(x):
    return x.arr if isinstance(x, _Cols) else x


def _off(x, unit):
    off = x.off if isinstance(x, _Cols) else 0
    assert off % unit == 0, (off, unit)
    return off // unit


def _sigmoid(x):
    return 1.0 / (1.0 + jnp.exp(-x))


def _silu(x):
    return x * _sigmoid(x)


def _silu_grad(x):
    s = _sigmoid(x)
    return s * (1.0 + x * (1.0 - s))


def _softplus(x):
    e = jnp.exp(-jnp.abs(x))
    log1p_e = jnp.where(e < 1e-3, e * (1.0 - e * (0.5 - e * (1.0 / 3.0))), jnp.log(1.0 + e))
    return jnp.maximum(x, 0.0) + log1p_e


_GELU_C = math.sqrt(2.0 / math.pi)


def _gelu(x):
    t = jnp.tanh(_GELU_C * (x + 0.044715 * x * x * x))
    return 0.5 * x * (1.0 + t)


def _gelu_grad(x):
    t = jnp.tanh(_GELU_C * (x + 0.044715 * x * x * x))
    return 0.5 * (1.0 + t) + 0.5 * x * (1.0 - t * t) * _GELU_C * (1.0 + 3.0 * 0.044715 * x * x)


def _neg_expm1(x):
    series = -x * (1.0 + 0.5 * x * (1.0 + (1.0 / 3.0) * x * (1.0 + 0.25 * x)))
    return jnp.where(x > -0.05, series, 1.0 - jnp.exp(x))


def _shift_down(x, k, row):
    return jnp.where(row >= k, pltpu.roll(x, k, 0), 0.0)


def _shift_up(x, k, row):
    n = x.shape[0]
    return jnp.where(row < n - k, pltpu.roll(x, n - k, 0), 0.0)


def _cumsum_rows(x, row):
    d = 1
    while d < x.shape[0]:
        x = x + _shift_down(x, d, row)
        d *= 2
    return x


def _rev_cumsum_rows(x, row):
    d = 1
    while d < x.shape[0]:
        x = x + _shift_up(x, d, row)
        d *= 2
    return x


def _cumsum_lanes(x, col):
    d = 1
    while d < x.shape[1]:
        x = x + jnp.where(col >= d, pltpu.roll(x, d, 1), 0.0)
        d *= 2
    return x


def _dot(a, b, ta=False, tb=False):
    dn = (((0 if ta else 1,), (1 if tb else 0,)), ((), ()))
    return lax.dot_general(a.astype(MXU_DTYPE), b.astype(MXU_DTYPE), dn, preferred_element_type=F32)


def _mm_tiles(m, n, k, a_bytes, b_bytes, mn_bytes):
    best = None
    for tm in (1024, 512, 384, 256, 128):
        for tn in (1024, 512, 384, 256, 128):
            for tk in (2048, 1024, 512, 384, 256, 128):
                if m % tm or n % tn or k % tk:
                    continue
                vmem = 2 * (tm * tk * a_bytes + tk * tn * b_bytes) + 2 * tm * tn * mn_bytes + 4 * tm * tn
                vmem += 2 * (tm * tk + tk * tn)
                if vmem > MM_VMEM_BUDGET:
                    continue
                steps = (m // tm) * (n // tn) * (k // tk)
                key = (steps, vmem)
                if best is None or key < best[0]:
                    best = (key, (tm, tn, tk))
    assert best is not None, (m, n, k)
    return best[1]


def _mm(a, b, *, ta=False, tb=False, epilogue=None, tiles=(), rowvecs=(), out_dtypes=(F32,), name):
    m, k = (a.shape[1], a.shape[0]) if ta else a.shape
    n = b.shape[0] if tb else b.shape[1]
    assert (b.shape[1] if tb else b.shape[0]) == k, (a.shape, b.shape, ta, tb)
    mn_bytes = sum(t.dtype.itemsize for t in tiles) + sum(jnp.dtype(dt).itemsize for dt in out_dtypes)
    tm, tn, tk = _mm_tiles(m, n, k, a.dtype.itemsize, b.dtype.itemsize, mn_bytes)
    nk = k // tk
    nt, nr, no = len(tiles), len(rowvecs), len(out_dtypes)

    def body(*refs):
        a_ref, b_ref = refs[0], refs[1]
        tile_refs = refs[2:2 + nt]
        row_refs = refs[2 + nt:2 + nt + nr]
        out_refs = refs[2 + nt + nr:2 + nt + nr + no]
        acc_ref = refs[-1]
        kk = pl.program_id(2)

        @pl.when(kk == 0)
        def _():
            acc_ref[...] = jnp.zeros_like(acc_ref)

        acc_ref[...] += _dot(a_ref[...], b_ref[...], ta, tb)

        @pl.when(kk == nk - 1)
        def _():
            acc = acc_ref[...]
            if epilogue is None:
                outs = (acc,)
            else:
                outs = epilogue(acc, *[t[...] for t in tile_refs], *[r[...] for r in row_refs])
            for o_ref, o in zip(out_refs, outs):
                o_ref[...] = o.astype(o_ref.dtype)

    a_spec = pl.BlockSpec((tk, tm), lambda i, j, kk: (kk, i)) if ta else pl.BlockSpec((tm, tk), lambda i, j, kk: (i, kk))
    b_spec = pl.BlockSpec((tn, tk), lambda i, j, kk: (j, kk)) if tb else pl.BlockSpec((tk, tn), lambda i, j, kk: (kk, j))
    mn_spec = pl.BlockSpec((tm, tn), lambda i, j, kk: (i, j))
    row_spec = pl.BlockSpec((1, tn), lambda i, j, kk: (0, j))
    tile_specs = [pl.BlockSpec((tm, tn), lambda i, j, kk, ob=_off(t, tn): (i, j + ob)) for t in tiles]
    outs = pl.pallas_call(
        body, name=name,
        grid=(m // tm, n // tn, nk),
        in_specs=[a_spec, b_spec] + tile_specs + [row_spec] * nr,
        out_specs=[mn_spec] * no,
        out_shape=[jax.ShapeDtypeStruct((m, n), dt) for dt in out_dtypes],
        scratch_shapes=[pltpu.VMEM((tm, tn), F32)],
        compiler_params=_cparams(("parallel", "parallel", "arbitrary")),
    )(a, b, *[_arr(t) for t in tiles], *rowvecs)
    return outs[0] if no == 1 else tuple(outs)


def _branch_merge(ys, ws, u_all, *, name):
    s, d = ys[0].shape[0], ws[0].shape[1]
    tm, tn = _pick(s, (512, 256, 128)), _pick(d, (512, 256, 128))
    nb = len(ys)

    def body(*refs):
        y_refs, w_refs, g_refs = refs[:nb], refs[nb:2 * nb], refs[2 * nb:3 * nb]
        merged_ref, yb_refs = refs[3 * nb], refs[3 * nb + 1:]
        merged = None
        for y_ref, w_ref, g_ref, yb_ref in zip(y_refs, w_refs, g_refs, yb_refs):
            acc = _dot(y_ref[...], w_ref[...])
            yb_ref[...] = acc.astype(yb_ref.dtype)
            term = _sigmoid(g_ref[...]) * acc
            merged = term if merged is None else merged + term
        merged_ref[...] = merged

    mn = pl.BlockSpec((tm, tn), lambda i, j: (i, j))
    in_specs = [pl.BlockSpec((tm, y.shape[1]), lambda i, j: (i, 0)) for y in ys]
    in_specs += [pl.BlockSpec((w.shape[0], tn), lambda i, j: (0, j)) for w in ws]
    in_specs += [pl.BlockSpec((tm, tn), lambda i, j, ob=b * d // tn: (i, j + ob)) for b in range(nb)]
    return pl.pallas_call(
        body, name=name, grid=(s // tm, d // tn), in_specs=in_specs, out_specs=[mn] * (nb + 1),
        out_shape=[jax.ShapeDtypeStruct((s, d), F32)] + [jax.ShapeDtypeStruct((s, d), BF16)] * nb,
        compiler_params=_cparams(("parallel", "parallel")),
    )(*ys, *ws, *[u_all] * nb)


def _rowwise(fn, rows, fulls, outs, *, name, tm=None):
    r = rows[0].shape[0]
    if tm is None:
        widest = max([x.shape[1] for x in rows] + [o[0] for o in outs])
        tm = _pick(r, (max(8, min(512, (512 * 1024) // widest)), 256, 128, 64, 32, 16, 8))
    nrow, nfull, nout = len(rows), len(fulls), len(outs)

    def body(*refs):
        row_refs = refs[:nrow]
        full_refs = refs[nrow:nrow + nfull]
        out_refs = refs[nrow + nfull:]
        res = fn(*[x[...] for x in row_refs], *[x[...] for x in full_refs])
        if not isinstance(res, (tuple, list)):
            res = (res,)
        step = pl.program_id(0)
        for o_ref, o, spec in zip(out_refs, res, outs):
            if spec[2] == "row":
                o_ref[...] = o.astype(o_ref.dtype)
            else:
                @pl.when(step == 0)
                def _(o_ref=o_ref):
                    o_ref[...] = jnp.zeros_like(o_ref)
                o_ref[...] += o

    in_specs = [pl.BlockSpec((tm, x.shape[1]), lambda i, ob=_off(x, x.shape[1]): (i, ob)) for x in rows]
    in_specs += [pl.BlockSpec(x.shape, lambda i, nd=x.ndim: (0,) * nd) for x in fulls]
    out_specs, out_shape = [], []
    for c, dt, kind in outs:
        if kind == "row":
            out_specs.append(pl.BlockSpec((tm, c), lambda i: (i, 0)))
            out_shape.append(jax.ShapeDtypeStruct((r, c), dt))
        else:
            out_specs.append(pl.BlockSpec((1, c), lambda i: (0, 0)))
            out_shape.append(jax.ShapeDtypeStruct((1, c), F32))
    res = pl.pallas_call(
        body, name=name, grid=(r // tm,), in_specs=in_specs, out_specs=out_specs, out_shape=out_shape,
        compiler_params=_cparams(("arbitrary",)),
    )(*[_arr(x) for x in rows], *fulls)
    return res[0] if nout == 1 else tuple(res)


def _colsum(x):
    return jnp.sum(x, axis=0, keepdims=True)


def _rms_parts(x, n_real):
    r = lax.rsqrt(jnp.sum(x * x, axis=-1, keepdims=True) * (1.0 / n_real) + EPS)
    return x * r, r


def _rms_fwd(x, g, *, n_real=None, out_dtype=BF16, name):
    n_real = n_real or x.shape[1]

    def fn(xv, gv):
        xh, _ = _rms_parts(xv, n_real)
        return xh * gv

    return _rowwise(fn, [x], [g], [(x.shape[1], out_dtype, "row")], name=name)


def _rms_bwd_math(xv, gv, dh, n_real):
    xh, r = _rms_parts(xv, n_real)
    dxh = dh * gv
    dx = r * (dxh - xh * (jnp.sum(dxh * xh, axis=-1, keepdims=True) * (1.0 / n_real)))
    return dx, _colsum(dh * xh)


def _rms_bwd(x, g, dh, res=None, *, name):
    n = x.shape[1]
    if res is None:
        def fn(xv, dhv, gv):
            return _rms_bwd_math(xv, gv, dhv.astype(F32), n)
        rows = [x, dh]
    else:
        def fn(xv, dhv, rv, gv):
            dx, dg = _rms_bwd_math(xv, gv, dhv.astype(F32), n)
            return dx + rv, dg
        rows = [x, dh, res]
    return _rowwise(fn, rows, [g], [(n, F32, "row"), (n, F32, "acc")], name=name)


def _seq_call(body, ins, outs, n_blocks, *, name):
    in_specs, args = [], []
    for x, kind in ins:
        in_specs.append(pl.BlockSpec((x.shape[0], LANES), lambda j, ob=_off(x, LANES): (0, j + ob)))
        args.append(_arr(x))
    out_specs, out_shape = [], []
    for shape, dt in outs:
        out_specs.append(pl.BlockSpec((shape[0], LANES), lambda j: (0, j)))
        out_shape.append(jax.ShapeDtypeStruct(shape, dt))
    res = pl.pallas_call(body, name=name, grid=(n_blocks,), in_specs=in_specs, out_specs=out_specs,
                         out_shape=out_shape, compiler_params=_cparams(("parallel",)))(*args)
    return res[0] if len(outs) == 1 else tuple(res)


def _conv_pre(x, w, b, row):
    acc = x * w[CONV_W - 1:CONV_W, :] + b
    for k in range(CONV_W - 1):
        acc = acc + _shift_down(x, CONV_W - 1 - k, row) * w[k:k + 1, :]
    return acc


def _conv_fwd(x, w, b, *, silu, name):
    s, c = x.shape

    def body(x_ref, w_ref, b_ref, y_ref):
        xv = x_ref[...]
        row = lax.broadcasted_iota(jnp.int32, xv.shape, 0)
        pre = _conv_pre(xv, w_ref[...], b_ref[...], row)
        y_ref[...] = _silu(pre) if silu else pre

    return _seq_call(body, [(x, "seq"), (w, "par"), (b, "par")], [((s, c), F32)], c // LANES, name=name)


def _conv_bwd(x, w, b, dy, *, silu, name):
    s, c = x.shape

    def body(x_ref, w_ref, b_ref, dy_ref, dx_ref, dw_ref, db_ref):
        xv, wv, dv = x_ref[...], w_ref[...], dy_ref[...]
        row = lax.broadcasted_iota(jnp.int32, xv.shape, 0)
        if silu:
            dv = dv * _silu_grad(_conv_pre(xv, wv, b_ref[...], row))
        dx = dv * wv[CONV_W - 1:CONV_W, :]
        dws = [None] * CONV_W
        dws[CONV_W - 1] = _colsum(dv * xv)
        for k in range(CONV_W - 1):
            sh = CONV_W - 1 - k
            dx = dx + _shift_up(dv, sh, row) * wv[k:k + 1, :]
            dws[k] = _colsum(dv * _shift_down(xv, sh, row))
        dx_ref[...] = dx
        for k in range(CONV_W):
            dw_ref[k:k + 1, :] = dws[k]
        db_ref[...] = _colsum(dv)

    return _seq_call(body, [(x, "seq"), (w, "par"), (b, "par"), (dy, "seq")],
                     [((s, c), F32), ((CONV_W, c), F32), ((1, c), F32)], c // LANES, name=name)


def _pool_select(levels):
    g = pl.program_id(0)
    return jnp.where(g == 0, levels[0], jnp.where(g == 1, levels[1], jnp.where(g == 2, levels[2], levels[3])))


def _pool_count(row):
    g = pl.program_id(0)
    w = jnp.where(g == 0, POOL_WINDOWS[0], jnp.where(g == 1, POOL_WINDOWS[1],
                                                     jnp.where(g == 2, POOL_WINDOWS[2], POOL_WINDOWS[3])))
    return jnp.minimum(row + 1, w).astype(F32)


def _pool_fwd(u, *, name):
    def body(u_ref, d_ref):
        uv = u_ref[...]
        row = lax.broadcasted_iota(jnp.int32, uv.shape, 0)
        levels, cur, sh = [], uv, 1
        for _ in POOL_WINDOWS:
            cur = cur + _shift_down(cur, sh, row)
            levels.append(cur)
            sh *= 2
        d_ref[...] = _pool_select(levels) / _pool_count(row) - uv

    return _seq_call(body, [(u, "seq")], [(u.shape, F32)], u.shape[1] // LANES, name=name)


def _pool_bwd(dd, *, name):
    def body(dd_ref, du_ref):
        dv = dd_ref[...]
        row = lax.broadcasted_iota(jnp.int32, dv.shape, 0)
        levels, cur, sh = [], dv / _pool_count(row), 1
        for _ in POOL_WINDOWS:
            cur = cur + _shift_up(cur, sh, row)
            levels.append(cur)
            sh *= 2
        du_ref[...] = _pool_select(levels) - dv

    return _seq_call(body, [(dd, "seq")], [(dd.shape, F32)], dd.shape[1] // LANES, name=name)


def _lru_gates(pre_a, pre_i, xc, lam, b_a, b_i):
    r = _sigmoid(pre_a + b_a)
    i = _sigmoid(pre_i + b_i)
    sp = _softplus(-lam)
    log_a = -LRU_C * r * sp
    a = jnp.exp(log_a)
    mult = jnp.sqrt(_neg_expm1(2.0 * log_a))
    return r, i, sp, a, mult


def _lru_fwd(pre, xc, gate_in, lam, b_a, b_i, *, name):
    s, c = xc.shape
    nb = c // LANES

    def body(pa_ref, pi_ref, xc_ref, g_ref, lam_ref, ba_ref, bi_ref, y_ref, h_ref):
        xv = xc_ref[...]
        row = lax.broadcasted_iota(jnp.int32, xv.shape, 0)
        _, i, _, a, mult = _lru_gates(pa_ref[...], pi_ref[...], xv, lam_ref[...], ba_ref[...], bi_ref[...])
        h = xv * i * mult
        d = 1
        while d < s:
            h = h + a * _shift_down(h, d, row)
            a = a * jnp.where(row >= d, pltpu.roll(a, d, 0), 1.0)
            d *= 2
        h_ref[...] = h
        y_ref[...] = h * _gelu(g_ref[...])

    blk = lambda off: pl.BlockSpec((s, LANES), lambda j: (0, j + off))
    par = pl.BlockSpec((1, LANES), lambda j: (0, j))
    return pl.pallas_call(
        body, name=name, grid=(nb,),
        in_specs=[blk(0), blk(nb), blk(0), blk(_off(gate_in, LANES)), par, par, par],
        out_specs=[blk(0), blk(0)],
        out_shape=[jax.ShapeDtypeStruct((s, c), F32)] * 2,
        compiler_params=_cparams(("parallel",)),
    )(pre, pre, xc, _arr(gate_in), lam, b_a, b_i)


def _lru_bwd(pre, xc, gate_in, lam, b_a, b_i, h, dy, *, name):
    s, c = xc.shape
    nb = c // LANES

    def body(pa_ref, pi_ref, xc_ref, g_ref, lam_ref, ba_ref, bi_ref, h_ref, dy_ref,
             dpa_ref, dpi_ref, dxc_ref, dg_ref, dlam_ref, dba_ref, dbi_ref):
        xv, gv, hv, dv = xc_ref[...], g_ref[...], h_ref[...], dy_ref[...]
        row = lax.broadcasted_iota(jnp.int32, xv.shape, 0)
        r, i, sp, a, mult = _lru_gates(pa_ref[...], pi_ref[...], xv, lam_ref[...], ba_ref[...], bi_ref[...])
        dg_ref[...] = dv * hv * _gelu_grad(gv)
        dh = dv * _gelu(gv)
        an = jnp.where(row < s - 1, pltpu.roll(a, s - 1, 0), 0.0)
        d = 1
        while d < s:
            dh = dh + an * _shift_up(dh, d, row)
            an = an * jnp.where(row < s - d, pltpu.roll(an, s - d, 0), 1.0)
            d *= 2
        da = dh * _shift_down(hv, 1, row)
        dxc_ref[...] = dh * i * mult
        di = dh * xv * mult
        dmult = dh * xv * i
        dlog_a = (da - dmult * a / mult) * a
        dr = dlog_a * (-LRU_C) * sp
        dlam_ref[...] = _colsum(dlog_a * LRU_C * r * _sigmoid(-lam_ref[...]))
        dpa = dr * r * (1.0 - r)
        dpi = di * i * (1.0 - i)
        dpa_ref[...] = dpa
        dpi_ref[...] = dpi
        dba_ref[...] = _colsum(dpa)
        dbi_ref[...] = _colsum(dpi)

    blk = lambda off: pl.BlockSpec((s, LANES), lambda j: (0, j + off))
    par = pl.BlockSpec((1, LANES), lambda j: (0, j))
    sc = jax.ShapeDtypeStruct((s, c), F32)
    pc = jax.ShapeDtypeStruct((1, c), F32)
    dpa, dpi, dxc, dg, dlam, dba, dbi = pl.pallas_call(
        body, name=name, grid=(nb,),
        in_specs=[blk(0), blk(nb), blk(0), blk(_off(gate_in, LANES)), par, par, par, blk(0), blk(0)],
        out_specs=[blk(0), blk(0), blk(0), blk(0), par, par, par],
        out_shape=[sc, sc, sc, sc, pc, pc, pc],
        compiler_params=_cparams(("parallel",)),
    )(pre, pre, xc, _arr(gate_in), lam, b_a, b_i, h, dy)
    return dpa, dpi, dxc, dg, dlam, dba, dbi


GROUP_HEADS = 4
SSD_GROUPS = 2


def _ssd_specs(nc, order):
    hw, gw = N_HEADS * LANES, SSD_GROUPS * LANES
    return dict(
        x=pl.BlockSpec((SSD_CHUNK, hw), lambda ci: (order(ci), 0)),
        b=pl.BlockSpec((SSD_CHUNK, gw), lambda ci: (order(ci), hw // gw)),
        c=pl.BlockSpec((SSD_CHUNK, gw), lambda ci: (order(ci), hw // gw + 1)),
        dtcol=pl.BlockSpec((N_HEADS, SSD_CHUNK, 1), lambda ci: (0, order(ci), 0)),
        dtrow=pl.BlockSpec((N_HEADS, 1, SSD_CHUNK), lambda ci: (0, 0, order(ci))),
        scal=pl.BlockSpec((N_HEADS, 1, 1), lambda ci: (0, 0, 0)),
        state=pl.BlockSpec((N_HEADS, 1, LANES, LANES), lambda ci: (0, order(ci), 0, 0)),
        group=pl.BlockSpec((SSD_CHUNK, gw), lambda ci: (order(ci), 0)),
        pacc=pl.BlockSpec((N_HEADS, 1, LANES), lambda ci: (0, 0, 0)),
    )


def _ssd_chunk_terms(dtcol, dtrow, bias, a_log):
    shp = (SSD_CHUNK, SSD_CHUNK)
    row = lax.broadcasted_iota(jnp.int32, shp, 0)
    col = lax.broadcasted_iota(jnp.int32, shp, 1)
    a_head = -jnp.exp(a_log)
    dt_c = jnp.broadcast_to(_softplus(dtcol + bias), shp)
    dt_r = jnp.broadcast_to(_softplus(dtrow + bias), shp)
    cs_c = _cumsum_rows(dt_c * a_head, row)
    cs_r = _cumsum_lanes(dt_r * a_head, col)
    cs_last = jnp.sum(jnp.where(row == SSD_CHUNK - 1, cs_c, 0.0), axis=0, keepdims=True)
    return row, col, a_head, dt_c, cs_c, cs_r, cs_last


def _ssd_fwd(xbc, dtcol, dtrow, bias, a_log, dskip, *, name):
    s = xbc.shape[0]
    nc = s // SSD_CHUNK

    def body(x_ref, b_ref, c_ref, dtc_ref, dtr_ref, bias_ref, alog_ref, d_ref, y_ref, st_ref, state):
        ci = pl.program_id(0)

        @pl.when(ci == 0)
        def _():
            state[...] = jnp.zeros_like(state)

        for gi in range(SSD_GROUPS):
            glanes = slice(gi * LANES, (gi + 1) * LANES)
            bm, cm = b_ref[:, glanes], c_ref[:, glanes]
            cb = _dot(cm, bm, tb=True)
            bm_t = bm.T
            for r in range(gi * GROUP_HEADS, (gi + 1) * GROUP_HEADS):
                lanes = slice(r * LANES, (r + 1) * LANES)
                xv = x_ref[:, lanes]
                row, col, _, dt_c, cs_c, cs_r, cs_last = _ssd_chunk_terms(dtc_ref[r], dtr_ref[r], bias_ref[r], alog_ref[r])
                g = cb * jnp.exp(jnp.where(col <= row, cs_c - cs_r, -jnp.inf))
                xdt = xv * dt_c
                st = state[r]
                st_ref[r, 0] = st
                y_ref[:, lanes] = _dot(g, xdt) + _dot(cm, st) * jnp.exp(cs_c) + xv * d_ref[r]
                state[r] = jnp.exp(cs_last) * st + _dot(bm_t, xdt * jnp.exp(cs_last - cs_c))

    sp = _ssd_specs(nc, lambda ci: ci)
    return pl.pallas_call(
        body, name=name, grid=(nc,),
        in_specs=[sp["x"], sp["b"], sp["c"], sp["dtcol"], sp["dtrow"], sp["scal"], sp["scal"], sp["scal"]],
        out_specs=[sp["x"], sp["state"]],
        out_shape=[jax.ShapeDtypeStruct((s, N_HEADS * LANES), F32),
                   jax.ShapeDtypeStruct((N_HEADS, nc, LANES, LANES), F32)],
        scratch_shapes=[pltpu.VMEM((N_HEADS, LANES, LANES), F32)],
        compiler_params=_cparams(("arbitrary",)),
    )(xbc, xbc, xbc, dtcol, dtrow, bias, a_log, dskip)


def _ssd_bwd(xbc, dtcol, dtrow, bias, a_log, dskip, states, dy, *, name):
    s = xbc.shape[0]
    nc = s // SSD_CHUNK

    def body(x_ref, b_ref, c_ref, dtc_ref, dtr_ref, bias_ref, alog_ref, d_ref, st_ref, dy_ref,
             dx_ref, db_ref, dc_ref, ddt_ref, dbias_ref, dalog_ref, dd_ref, dstate):
        ci = pl.program_id(0)

        @pl.when(ci == 0)
        def _():
            dstate[...] = jnp.zeros_like(dstate)
            dbias_ref[...] = jnp.zeros_like(dbias_ref)
            dalog_ref[...] = jnp.zeros_like(dalog_ref)
            dd_ref[...] = jnp.zeros_like(dd_ref)

        rowsum = lambda v: jnp.sum(v, axis=1, keepdims=True)
        tot = lambda v: jnp.broadcast_to(jnp.sum(v, axis=0, keepdims=True), (1, LANES))
        for gi in range(SSD_GROUPS):
            glanes = slice(gi * LANES, (gi + 1) * LANES)
            bm, cm = b_ref[:, glanes], c_ref[:, glanes]
            cb = _dot(cm, bm, tb=True)
            cb_t = _dot(bm, cm, tb=True)
            cm_t = cm.T
            dbm_sum, dcm_sum = None, None
            for r in range(gi * GROUP_HEADS, (gi + 1) * GROUP_HEADS):
                lanes = slice(r * LANES, (r + 1) * LANES)
                xv, dyv, st = x_ref[:, lanes], dy_ref[:, lanes], st_ref[r, 0]
                dtraw_c, bias = dtc_ref[r], bias_ref[r]
                row, col, a_head, dt_c, cs_c, cs_r, cs_last = _ssd_chunk_terms(dtraw_c, dtr_ref[r], bias, alog_ref[r])
                lmat = jnp.exp(jnp.where(col <= row, cs_c - cs_r, -jnp.inf))
                lmat_t = jnp.exp(jnp.where(row <= col, cs_r - cs_c, -jnp.inf))
                g, g_t = cb * lmat, cb_t * lmat_t
                xdt = xv * dt_c
                e_c = jnp.exp(cs_c)
                f_c = jnp.exp(cs_last - cs_c)
                e_last = jnp.exp(cs_last)
                w = xdt * f_c
                dst = dstate[r]

                dg = _dot(dyv, xdt, tb=True)
                dg_t = _dot(xdt, dyv, tb=True)
                dxdt = _dot(g_t, dyv)
                dcs = rowsum(dg * g) - rowsum(dg_t * g_t)
                dcm = _dot(dg * lmat, bm)
                dbm = _dot(dg_t * lmat_t, cm)
                z = _dot(cm, st)
                dz = dyv * e_c
                dcs = dcs + rowsum(dz * z)
                dcm = dcm + _dot(dz, st, tb=True)
                dstate[r] = _dot(cm_t, dz) + e_last * dst
                dcs_last = jnp.sum(rowsum(dst * st), axis=0, keepdims=True) * jnp.max(e_last, axis=1, keepdims=True)
                dbm = dbm + _dot(w, dst, tb=True)
                dw = _dot(bm, dst)
                dxdt = dxdt + dw * f_c
                q = rowsum(dw * w)
                dcs = dcs - q
                dcs_last = dcs_last + jnp.sum(q, axis=0, keepdims=True)
                dx_ref[:, lanes] = dxdt * dt_c + dyv * d_ref[r]
                ddt = rowsum(dxdt * xv)
                dcs_full = jnp.broadcast_to(dcs, (SSD_CHUNK, SSD_CHUNK)) + jnp.where(row == SSD_CHUNK - 1, dcs_last, 0.0)
                da = jnp.max(_rev_cumsum_rows(dcs_full, row), axis=1, keepdims=True)
                dt_col = jnp.max(dt_c, axis=1, keepdims=True)
                draw = (ddt + da * a_head) * _sigmoid(dtraw_c + bias)
                ddt_ref[r] = draw
                dbias_ref[r] += tot(draw)
                dalog_ref[r] += tot(da * dt_col) * a_head
                dd_ref[r] += tot(rowsum(dyv * xv))
                dbm_sum = dbm if dbm_sum is None else dbm_sum + dbm
                dcm_sum = dcm if dcm_sum is None else dcm_sum + dcm
            db_ref[:, glanes] = dbm_sum
            dc_ref[:, glanes] = dcm_sum

    sp = _ssd_specs(nc, lambda ci: nc - 1 - ci)
    return pl.pallas_call(
        body, name=name, grid=(nc,),
        in_specs=[sp["x"], sp["b"], sp["c"], sp["dtcol"], sp["dtrow"], sp["scal"], sp["scal"], sp["scal"],
                  sp["state"], sp["x"]],
        out_specs=[sp["x"], sp["group"], sp["group"], sp["dtcol"], sp["pacc"], sp["pacc"], sp["pacc"]],
        out_shape=[jax.ShapeDtypeStruct((s, N_HEADS * LANES), F32),
                   jax.ShapeDtypeStruct((s, 2 * LANES), F32),
                   jax.ShapeDtypeStruct((s, 2 * LANES), F32),
                   jax.ShapeDtypeStruct((N_HEADS, s, 1), F32),
                   jax.ShapeDtypeStruct((N_HEADS, 1, LANES), F32),
                   jax.ShapeDtypeStruct((N_HEADS, 1, LANES), F32),
                   jax.ShapeDtypeStruct((N_HEADS, 1, LANES), F32)],
        scratch_shapes=[pltpu.VMEM((N_HEADS, LANES, LANES), F32)],
        compiler_params=_cparams(("arbitrary",)),
    )(xbc, xbc, xbc, dtcol, dtrow, bias, a_log, dskip, states, dy)


def _att_tile(s):
    return _pick(s, (512, 256, 128))


def _tri(t, transposed=False):
    r = lax.broadcasted_iota(jnp.int32, (t, t), 0)
    c = lax.broadcasted_iota(jnp.int32, (t, t), 1)
    return (r <= c) if transposed else (c <= r)


def _rows_at(ref, blk, t):
    return ref[pl.ds(pl.multiple_of(blk * t, t), t), :]


def _flash_fwd(q, k, v, *, name):
    s = q.shape[0]
    t = _att_tile(s)
    nq = s // t

    def body(q_ref, k_ref, v_ref, o_ref, lse_ref):
        i = pl.program_id(1)
        qv = q_ref[...]

        def step(j, carry, diagonal):
            m_old, l_old, acc = carry
            sc = _dot(qv, _rows_at(k_ref, j, t), tb=True)
            if diagonal:
                sc = jnp.where(_tri(t), sc, -jnp.inf)
            m_new = jnp.maximum(m_old, jnp.max(sc, axis=1, keepdims=True))
            alpha = jnp.exp(m_old - m_new)
            p = jnp.exp(sc - m_new)
            return (m_new, alpha * l_old + jnp.sum(p, axis=1, keepdims=True),
                    alpha * acc + _dot(p, _rows_at(v_ref, j, t)))

        init = (jnp.full((t, 1), -jnp.inf, F32), jnp.zeros((t, 1), F32), jnp.zeros((t, LANES), F32))
        carry = lax.fori_loop(0, i, lambda j, c: step(j, c, False), init)
        m_fin, l_fin, acc = step(i, carry, True)
        o_ref[...] = (acc / l_fin).astype(o_ref.dtype)
        lse_ref[0] = m_fin + jnp.log(l_fin)

    q_spec = pl.BlockSpec((t, LANES), lambda h, i: (i, h))
    kv_spec = pl.BlockSpec((s, LANES), lambda h, i: (0, h))
    return pl.pallas_call(
        body, name=name, grid=(N_HEADS, nq),
        in_specs=[q_spec, kv_spec, kv_spec],
        out_specs=[q_spec, pl.BlockSpec((1, t, 1), lambda h, i: (h, i, 0))],
        out_shape=[jax.ShapeDtypeStruct(q.shape, BF16), jax.ShapeDtypeStruct((N_HEADS, s, 1), F32)],
        compiler_params=_cparams(("parallel", "arbitrary")),
    )(q, k, v)


def _att_delta(o, do, *, name):
    s = o.shape[0]
    t = _att_tile(s)

    def body(o_ref, do_ref, dl_ref):
        dl_ref[0] = jnp.sum(do_ref[...].astype(F32) * o_ref[...].astype(F32), axis=1, keepdims=True)

    blk = pl.BlockSpec((t, LANES), lambda h, i: (i, h))
    return pl.pallas_call(
        body, name=name, grid=(N_HEADS, s // t), in_specs=[blk, blk],
        out_specs=pl.BlockSpec((1, t, 1), lambda h, i: (h, i, 0)),
        out_shape=jax.ShapeDtypeStruct((N_HEADS, s, 1), F32),
        compiler_params=_cparams(("parallel", "parallel")),
    )(o, do)


def _flash_bwd(q, k, v, do, lse_row, delta_row, *, name):
    s = q.shape[0]
    t = _att_tile(s)
    nq = s // t

    def body(q_ref, k_ref, v_ref, do_ref, lse_ref, dl_ref, dq_ref, dk_ref, dv_ref):
        j = pl.program_id(1)
        kv, vv = k_ref[...], v_ref[...]

        @pl.when(j == 0)
        def _():
            dq_ref[...] = jnp.zeros_like(dq_ref)

        def step(i, carry, diagonal):
            dk, dv = carry
            rows = pl.ds(pl.multiple_of(i * t, t), t)
            qi, doi = q_ref[rows, :], do_ref[rows, :]
            p_t = jnp.exp(_dot(kv, qi, tb=True) - lse_ref[0, :, rows])
            if diagonal:
                p_t = jnp.where(_tri(t, transposed=True), p_t, 0.0)
            ds_t = (p_t * (_dot(vv, doi, tb=True) - dl_ref[0, :, rows])).astype(MXU_DTYPE)
            dq_ref[rows, :] += _dot(ds_t, kv, ta=True)
            return dk + _dot(ds_t, qi), dv + _dot(p_t, doi)

        zero = jnp.zeros((t, LANES), F32)
        carry = step(j, (zero, zero), True)
        dk, dv = lax.fori_loop(j + 1, nq, lambda i, c: step(i, c, False), carry)
        dk_ref[...] = dk
        dv_ref[...] = dv

        @pl.when(j == nq - 1)
        def _():
            dq_ref[...] = dq_ref[...] * ATT_SCALE

    q_spec = pl.BlockSpec((s, LANES), lambda h, j: (0, h))
    kv_spec = pl.BlockSpec((t, LANES), lambda h, j: (j, h))
    row_spec = pl.BlockSpec((1, 1, s), lambda h, j: (h, 0, 0))
    return pl.pallas_call(
        body, name=name, grid=(N_HEADS, nq),
        in_specs=[q_spec, kv_spec, kv_spec, q_spec, row_spec, row_spec],
        out_specs=[q_spec, kv_spec, kv_spec],
        out_shape=[jax.ShapeDtypeStruct(q.shape, F32)] * 3,
        compiler_params=_cparams(("parallel", "arbitrary")),
    )(q, k, v, do, lse_row, delta_row)


def _rope(v, cos_t, sin_p, sin_m):
    return v * cos_t + pltpu.roll(v, QK_ROPE // 2, 1) * sin_p + pltpu.roll(v, LANES - QK_ROPE // 2, 1) * sin_m


def _rope_t(d, cos_t, sin_p, sin_m):
    return d * cos_t + pltpu.roll(d * sin_p, LANES - QK_ROPE // 2, 1) + pltpu.roll(d * sin_m, QK_ROPE // 2, 1)


def _att_prep(q_pad, kv2, kr, cos_t, sin_p, sin_m, *, name):
    w = N_HEADS * LANES

    def fn(qv, kvv, krv, c, sp, sm):
        kr_rot = _rope(krv, c, sp, sm)
        qs, ks = [], []
        for h in range(N_HEADS):
            blk = slice(h * LANES, (h + 1) * LANES)
            qs.append(_rope(qv[:, blk], c, sp, sm) * ATT_SCALE)
            ks.append(kvv[:, blk] + kr_rot)
        return jnp.concatenate(qs, axis=1), jnp.concatenate(ks, axis=1), kvv[:, w:]

    return _rowwise(fn, [q_pad, kv2, kr, cos_t, sin_p, sin_m], [],
                    [(w, BF16, "row"), (w, BF16, "row"), (w, BF16, "row")], name=name)


def _att_prep_bwd(dq, dk, cos_t, sin_p, sin_m, *, name):
    w = N_HEADS * LANES

    def fn(dqv, dkv, c, sp, sm):
        outs, dkr = [], None
        for h in range(N_HEADS):
            blk = slice(h * LANES, (h + 1) * LANES)
            outs.append(_rope_t(dqv[:, blk], c, sp, sm))
            dkr = dkv[:, blk] if dkr is None else dkr + dkv[:, blk]
        return jnp.concatenate(outs, axis=1), _rope_t(dkr, c, sp, sm)

    return _rowwise(fn, [dq, dk, cos_t, sin_p, sin_m], [], [(w, BF16, "row"), (LANES, F32, "row")], name=name)


_ANY = pl.BlockSpec(memory_space=pl.ANY)
_MESH = pl.DeviceIdType.MESH


def _mesh_pos():
    return lax.axis_index("x"), lax.axis_index("y"), lax.axis_index("c")


def _remote(src, dst, send_sem, recv_sem, dev):
    return pltpu.make_async_remote_copy(src_ref=src, dst_ref=dst, send_sem=send_sem, recv_sem=recv_sem,
                                        device_id=dev, device_id_type=_MESH)


def _other_chips(x, y):
    chips = [(1 - x, y), (x, 1 - y), (1 - x, 1 - y)]
    return chips, [2 * cx + cy for cx, cy in chips]


def _comm_call(body, ins, out_shapes, n_sems, *, name):
    return pl.pallas_call(
        body, name=name, in_specs=[_ANY] * len(ins), out_specs=[_ANY] * len(out_shapes), out_shape=out_shapes,
        scratch_shapes=[pltpu.SemaphoreType.DMA((k,)) for k in n_sems],
    )(*ins)


def _gather_halves(shards):
    n = len(shards)
    halves = [t.shape[0] // 2 for t in shards]

    def body(*refs):
        xs, outs = refs[:n], refs[n:2 * n]
        send_sems, recv_sems = refs[2 * n:]
        x, y, c = _mesh_pos()
        k = 2 * x + y
        sibling = (x, y, 1 - c)
        chips, ks = _other_chips(x, y)
        half = lambda w, hf: pl.ds(hf * halves[w], halves[w])
        first = [_remote(xs[w].at[half(w, c)], outs[w].at[k, half(w, c)], send_sems.at[6 * w + j], recv_sems.at[6 * w + j],
                         (*chips[j], c)) for w in range(n) for j in range(3)]
        for cp in first:
            cp.start()
        passed = []
        for j in range(3):
            for w in range(n):
                land = outs[w].at[ks[j], half(w, c)]
                _remote(land, land, send_sems.at[6 * w + j], recv_sems.at[6 * w + j], sibling).wait_recv()
                passed.append(_remote(land, land, send_sems.at[6 * w + 3 + j], recv_sems.at[6 * w + 3 + j], sibling))
                passed[-1].start()
        for j in range(3):
            for w in range(n):
                land = outs[w].at[ks[j], half(w, 1 - c)]
                _remote(land, land, send_sems.at[6 * w + 3 + j], recv_sems.at[6 * w + 3 + j], sibling).wait_recv()
        for cp in first + passed:
            cp.wait_send()

    shapes = [jax.ShapeDtypeStruct((4,) + t.shape, t.dtype) for t in shards]
    return _comm_call(body, shards, shapes, (6 * n, 6 * n), name="gather_halves")


_HBM = pl.BlockSpec(memory_space=pltpu.HBM)
_SEM = pl.BlockSpec(memory_space=pltpu.SEMAPHORE)
_EFFECT = pltpu.SideEffectType.DATAFLOW_SIDE_EFFECTING


def _push_start(blocks, *, scatter, name):
    n = len(blocks)

    def body(*refs):
        xs, lands = refs[:n], refs[n:2 * n]
        send_sems, recv_sems = refs[2 * n], refs[2 * n + 1]
        token = refs[-1]
        x, y, c = _mesh_pos()
        k = 2 * x + y
        chips, ks = _other_chips(x, y)
        for w in range(n):
            for j in range(3):
                src = xs[w].at[ks[j]] if scatter else xs[w]
                _remote(src, lands[w].at[k], send_sems.at[3 * w + j], recv_sems.at[3 * w + j], (*chips[j], c)).start()
        token[...] = jnp.zeros_like(token)

    hbm = lambda shape, dtype: pltpu.with_memory_space_constraint(lax.empty(shape, dtype), pltpu.HBM)
    ins = [pltpu.with_memory_space_constraint(t, pltpu.HBM) for t in blocks]
    ins += [hbm(t.shape if scatter else (4,) + t.shape, t.dtype) for t in blocks]
    out_shape = [pltpu.SemaphoreType.DMA((3 * n,)), pltpu.SemaphoreType.DMA((3 * n,))]
    out_shape += [pltpu.HBM(t.shape, t.dtype) for t in ins]
    out_shape += [jax.ShapeDtypeStruct((8, LANES), F32)]
    res = pl.pallas_call(
        body, name=name, out_shape=out_shape, in_specs=[_HBM] * (2 * n),
        out_specs=[_SEM, _SEM] + [_HBM] * (2 * n) + [pl.BlockSpec(memory_space=pltpu.VMEM)],
        input_output_aliases={i: 2 + i for i in range(2 * n)},
        compiler_params=pltpu.CompilerParams(has_side_effects=_EFFECT),
    )(*ins)
    return res[0], res[1], res[2:2 + n], res[2 + n:2 + 2 * n], res[-1]


def _push_wait(send_sems, recv_sems, blocks, lands, after, *, name):
    n = len(blocks)

    def body(*refs):
        lands_in = refs[n:2 * n]
        send_sems, recv_sems = refs[2 * n], refs[2 * n + 1]
        x, y, c = _mesh_pos()
        chips, ks = _other_chips(x, y)
        for w in range(n):
            for j in range(3):
                slot = lands_in[w].at[ks[j]]
                cp = _remote(slot, slot, send_sems.at[3 * w + j], recv_sems.at[3 * w + j], (*chips[j], c))
                cp.wait_send()
                cp.wait_recv()

    out_shape = [pltpu.HBM(t.shape, t.dtype) for t in list(blocks) + list(lands)]
    res = pl.pallas_call(
        body, name=name, out_shape=out_shape,
        in_specs=[_HBM] * (2 * n) + [_SEM, _SEM, pl.BlockSpec(memory_space=pl.ANY)], out_specs=[_HBM] * (2 * n),
        input_output_aliases={i: i for i in range(2 * n)},
        compiler_params=pltpu.CompilerParams(has_side_effects=_EFFECT),
    )(*blocks, *lands, send_sems, recv_sems, after)
    return res[:n], res[n:]


def _send_half(views, *, name):
    n = len(views)

    def body(*refs):
        vs, outs = refs[:n], refs[n:2 * n]
        send_sems, recv_sems = refs[2 * n:]
        x, y, c = _mesh_pos()
        cps = []
        for w in range(n):
            h = views[w].shape[1] // 2
            cps.append(_remote(vs[w].at[:, pl.ds((1 - c) * h, h), :], outs[w], send_sems.at[w], recv_sems.at[w],
                               (x, y, 1 - c)))
            cps[-1].start()
        for cp in cps:
            cp.wait()

    shapes = [jax.ShapeDtypeStruct((t.shape[0], t.shape[1] // 2, t.shape[2]), t.dtype) for t in views]
    return _comm_call(body, views, shapes, (n, n), name=name)


def _swap_with_sibling(mine, *, name):
    n = len(mine)

    def body(*refs):
        hs, outs = refs[:n], refs[n:2 * n]
        send_sems, recv_sems = refs[2 * n:]
        x, y, c = _mesh_pos()
        cps = [_remote(hs[w], outs[w], send_sems.at[w], recv_sems.at[w], (x, y, 1 - c)) for w in range(n)]
        for cp in cps:
            cp.start()
        for cp in cps:
            cp.wait()

    shapes = [jax.ShapeDtypeStruct(t.shape, t.dtype) for t in mine]
    return _comm_call(body, mine, shapes, (n, n), name=name)


def _gather_all(vec, *, name):
    r, w = vec.shape

    def body(v_ref, out_ref, send_sems, recv_sems):
        x, y, c = _mesh_pos()

        def slot(px, py, pc):
            return out_ref.at[4 * px + 2 * py + pc]

        peers = []
        for rel in range(1, 8):
            fx, fy, fc = (rel >> 2) & 1, (rel >> 1) & 1, rel & 1
            peers.append((x ^ fx, y ^ fy, c ^ fc))
        cps = [_remote(v_ref, slot(x, y, c), send_sems.at[j], recv_sems.at[j], peer) for j, peer in enumerate(peers)]
        for cp in cps:
            cp.start()
        for j, peer in enumerate(peers):
            _remote(slot(*peer), slot(*peer), send_sems.at[j], recv_sems.at[j], peer).wait_recv()
        for cp in cps:
            cp.wait_send()

    others = pl.pallas_call(
        body, name=name, in_specs=[_ANY], out_specs=_ANY,
        out_shape=jax.ShapeDtypeStruct((8, r, w), vec.dtype),
        scratch_shapes=[pltpu.SemaphoreType.DMA((7,)), pltpu.SemaphoreType.DMA((7,))],
    )(vec)
    me = 4 * lax.axis_index("x") + 2 * lax.axis_index("y") + lax.axis_index("c")
    return lax.dynamic_update_index_in_dim(others, vec, me, 0)


def _gather_same_core(vec, *, name):
    r, w = vec.shape

    def body(v_ref, out_ref, send_sems, recv_sems):
        x, y, c = _mesh_pos()
        k = 2 * x + y
        chips, ks = _other_chips(x, y)
        cps = [_remote(v_ref, out_ref.at[k], send_sems.at[j], recv_sems.at[j], (*chips[j], c)) for j in range(3)]
        for cp in cps:
            cp.start()
        for j in range(3):
            slot = out_ref.at[ks[j]]
            _remote(slot, slot, send_sems.at[j], recv_sems.at[j], (*chips[j], c)).wait_recv()
        for cp in cps:
            cp.wait_send()

    others = pl.pallas_call(
        body, name=name, in_specs=[_ANY], out_specs=_ANY,
        out_shape=jax.ShapeDtypeStruct((4, r, w), vec.dtype),
        scratch_shapes=[pltpu.SemaphoreType.DMA((3,)), pltpu.SemaphoreType.DMA((3,))],
    )(vec)
    k_chip = 2 * lax.axis_index("x") + lax.axis_index("y")
    return lax.dynamic_update_index_in_dim(others, vec, k_chip, 0)


def _row_tile(rows, row_bytes):
    for tm in (1024, 512, 256, 128, 64, 32, 16):
        if rows % tm == 0 and tm * row_bytes <= ELEMENTWISE_BLOCK_BYTES:
            return tm
    return 16 if rows % 16 == 0 else rows


def _chip_sum_half(g, got, c, *, name):
    nb, r, w = g.shape
    half = r // 2
    tm = _row_tile(half, w * 4)
    per = half // tm

    def body(c_ref, g_ref, o_ref, out_ref):
        out_ref[...] = (g_ref[...] + o_ref[...]).astype(out_ref.dtype)

    return pl.pallas_call(
        body, name=name,
        grid_spec=pltpu.PrefetchScalarGridSpec(
            num_scalar_prefetch=1, grid=(nb, per),
            in_specs=[pl.BlockSpec((1, tm, w), lambda b, i, c_ref: (b, c_ref[0] * per + i, 0)),
                      pl.BlockSpec((1, tm, w), lambda b, i, c_ref: (b, i, 0))],
            out_specs=pl.BlockSpec((1, tm, w), lambda b, i, c_ref: (b, i, 0))),
        out_shape=jax.ShapeDtypeStruct((nb, half, w), BF16),
        compiler_params=_cparams(("parallel", "parallel")),
    )(jnp.reshape(c, (1,)).astype(jnp.int32), g, got)


def _sum_slots(stack, *, name):
    n, r, w = stack.shape
    tm = _row_tile(r, n * w * stack.dtype.itemsize)

    def body(s_ref, out_ref):
        acc = s_ref[0].astype(F32)
        for i in range(1, n):
            acc = acc + s_ref[i].astype(F32)
        out_ref[...] = acc

    return pl.pallas_call(
        body, name=name, grid=(r // tm,),
        in_specs=[pl.BlockSpec((n, tm, w), lambda i: (0, i, 0))],
        out_specs=pl.BlockSpec((tm, w), lambda i: (i, 0)),
        out_shape=jax.ShapeDtypeStruct((r, w), F32),
        compiler_params=_cparams(("parallel",)),
    )(stack)


def _adam_math(wv, gv, mv, vv):
    m_new = ADAM_B1 * mv + (1.0 - ADAM_B1) * gv
    v_new = ADAM_B2 * vv + (1.0 - ADAM_B2) * (gv * gv)
    m_hat = m_new / (1.0 - ADAM_B1 ** ADAM_STEP)
    v_hat = v_new / (1.0 - ADAM_B2 ** ADAM_STEP)
    delta = -ADAM_LR * (m_hat / (jnp.sqrt(v_hat) + ADAM_EPS) + ADAM_WD * wv)
    return delta, m_new, v_new


def _adamw(w, g, m, v, *, name):
    shape = w.shape
    cols = shape[-1]
    flat = lambda t: t.reshape(-1, cols)
    rows = flat(w).shape[0]
    tm = _pick(rows, (256, 128, 64, 32, 16, 8))
    outs = _rowwise(_adam_math, [flat(w), flat(g), flat(m), flat(v)], [], [(cols, F32, "row")] * 3, name=name, tm=tm)
    return tuple(o.reshape(shape) for o in outs)


def _adamw_slots(w, slots, m, v, c, *, name):
    shape = w.shape
    cols = shape[-1]
    half = slots[0][0].shape[1]
    v4 = lambda t: t.reshape(2, 2, half, cols)
    assert all(s.shape == (4, half, cols) for pair in slots for s in pair) and w.size == 4 * half * cols
    tm = _row_tile(half, cols * 4 * 4)

    def body(c_ref, w_ref, m0_ref, o0_ref, m1_ref, o1_ref, m_ref, v_ref, g_ref, d_ref, mo_ref, vo_ref):
        first = pl.program_id(0) == 0
        own = pl.program_id(1) == c_ref[0]
        g = None
        for i in range(4):
            part = jnp.where(first, jnp.where(own, m0_ref[i], o0_ref[i]), jnp.where(own, m1_ref[i], o1_ref[i]))
            g = part.astype(F32) if g is None else g + part.astype(F32)
        delta, m_new, v_new = _adam_math(w_ref[0, 0], g, m_ref[0, 0], v_ref[0, 0])
        g_ref[0, 0], d_ref[0, 0], mo_ref[0, 0], vo_ref[0, 0] = g, delta, m_new, v_new

    blk = pl.BlockSpec((1, 1, tm, cols), lambda l, hf, i, c_ref: (l, hf, i, 0))

    def slot_spec(layer, mine):
        def index(l, hf, i, c_ref):
            same_half = hf * c_ref[0] + (1 - hf) * (1 - c_ref[0])
            use = (l if layer else 1 - l) * (same_half if mine else 1 - same_half)
            return (0, i * use, 0)
        return pl.BlockSpec((4, tm, cols), index)

    outs = pl.pallas_call(
        body, name=name,
        grid_spec=pltpu.PrefetchScalarGridSpec(
            num_scalar_prefetch=1, grid=(2, 2, half // tm),
            in_specs=[blk, slot_spec(0, True), slot_spec(0, False), slot_spec(1, True), slot_spec(1, False), blk, blk],
            out_specs=[blk] * 4),
        out_shape=[jax.ShapeDtypeStruct((2, 2, half, cols), F32)] * 4,
        compiler_params=_cparams(("arbitrary", "arbitrary", "arbitrary")),
    )(jnp.reshape(c, (1,)).astype(jnp.int32), v4(w), slots[0][0], slots[0][1], slots[1][0], slots[1][1], v4(m), v4(v))
    return tuple(o.reshape(shape) for o in outs)


def _pad_blocks(w, axis, n_blocks, real, to=LANES, offset=0):
    axis = axis % w.ndim
    shp = w.shape
    w = w.reshape(shp[:axis] + (n_blocks, real) + shp[axis + 1:])
    pads = [(0, 0)] * w.ndim
    pads[axis + 1] = (offset, to - real - offset)
    w = jnp.pad(w, pads)
    return w.reshape(shp[:axis] + (n_blocks * to,) + shp[axis + 1:])


def _unpad_blocks(w, axis, n_blocks, real, to=LANES, offset=0):
    axis = axis % w.ndim
    shp = w.shape
    w = w.reshape(shp[:axis] + (n_blocks, to) + shp[axis + 1:])
    w = lax.slice_in_dim(w, offset, offset + real, axis=axis + 1)
    return w.reshape(shp[:axis] + (n_blocks * real,) + shp[axis + 1:])


def _block_diag(w):
    n, a, b = w.shape
    eye = jnp.eye(n, dtype=w.dtype)
    return (eye[:, None, :, None] * w[:, :, None, :]).reshape(n * a, n * b)


def _block_diag_t(d, n):
    a, b = d.shape[0] // n, d.shape[1] // n
    d = d.reshape(n, a, n, b)
    return jnp.stack([d[i, :, i, :] for i in range(n)])


_SPLITS = np.cumsum((0,) + SPLIT_SIZES)


def _w_in_groups(w_in):
    sl = lambda i: w_in[:, _SPLITS[i]:_SPLITS[i + 1]]
    xbc = sl(5)
    xbc_pad = jnp.concatenate([_pad_blocks(xbc[:, :MIX], 1, N_HEADS, HEAD),
                               _pad_blocks(xbc[:, MIX:MIX + 2 * HEAD], 1, 2, HEAD),
                               _pad_blocks(xbc[:, MIX + 2 * HEAD:], 1, 2, HEAD)], axis=1)
    return dict(
        cq=sl(0), ckv=sl(1), kr=_pad_blocks(sl(2), 1, 1, QK_ROPE, offset=HEAD), pool=sl(3),
        z=_pad_blocks(sl(4), 1, N_HEADS, HEAD), xbc=xbc_pad, dt=_pad_blocks(sl(6), 1, 1, N_HEADS),
        lru_g=sl(7), lru_x=sl(8), gates=sl(9))


def _w_in_fused(groups):
    parts, at = [], 0
    for name, off, width in IN_LAYOUT:
        assert groups[name].shape[1] == width and off >= at
        if off > at:
            parts.append(jnp.zeros((groups[name].shape[0], off - at), groups[name].dtype))
        parts.append(groups[name])
        at = off + width
    parts.append(jnp.zeros((parts[0].shape[0], IN_ALL_COLS - at), parts[0].dtype))
    return jnp.concatenate(parts, axis=1)


def _in_cols(arr, name):
    off, width = IN_OFFSETS[name]
    return _Cols(arr, off, width)


def _w_in_ungroup(d):
    xbc = d["xbc"]
    w = N_HEADS * LANES
    xbc_real = jnp.concatenate([_unpad_blocks(xbc[:, :w], 1, N_HEADS, HEAD),
                                _unpad_blocks(xbc[:, w:w + 2 * LANES], 1, 2, HEAD),
                                _unpad_blocks(xbc[:, w + 2 * LANES:], 1, 2, HEAD)], axis=1)
    return jnp.concatenate([d["cq"], d["ckv"], _unpad_blocks(d["kr"], 1, 1, QK_ROPE, offset=HEAD), d["pool"],
                            _unpad_blocks(d["z"], 1, N_HEADS, HEAD), xbc_real, _unpad_blocks(d["dt"], 1, 1, N_HEADS),
                            d["lru_g"], d["lru_x"], d["gates"]], axis=1)


def _pad_xbc_vec(v):
    return jnp.concatenate([_pad_blocks(v[..., :MIX], -1, N_HEADS, HEAD),
                            _pad_blocks(v[..., MIX:MIX + 2 * HEAD], -1, 2, HEAD),
                            _pad_blocks(v[..., MIX + 2 * HEAD:], -1, 2, HEAD)], axis=-1)


def _unpad_xbc_vec(v):
    w = N_HEADS * LANES
    return jnp.concatenate([_unpad_blocks(v[..., :w], -1, N_HEADS, HEAD),
                            _unpad_blocks(v[..., w:w + 2 * LANES], -1, 2, HEAD),
                            _unpad_blocks(v[..., w + 2 * LANES:], -1, 2, HEAD)], axis=-1)


def _layer_weights(p):
    q = dict(p)
    q["in_all"] = _w_in_fused(_w_in_groups(p["w_in"]))
    q["uq"] = _pad_blocks(p["w_uq"], 1, N_HEADS, HEAD + QK_ROPE)
    ukv = p["w_ukv"].reshape(KV_LORA, N_HEADS, 2 * HEAD)
    q["ukv"] = jnp.concatenate([_pad_blocks(ukv[:, :, :HEAD].reshape(KV_LORA, -1), 1, N_HEADS, HEAD),
                                _pad_blocks(ukv[:, :, HEAD:].reshape(KV_LORA, -1), 1, N_HEADS, HEAD)], axis=1)
    q["pool_bd"] = _block_diag(p["w_pool"])
    q["lru_bd"] = jnp.concatenate([_block_diag(p["lru_w_a"]), _block_diag(p["lru_w_i"])], axis=1)
    q["br"] = [_pad_blocks(p["w_branch"][0], 0, N_HEADS, HEAD), p["w_branch"][1],
               _pad_blocks(p["w_branch"][2], 0, N_HEADS, HEAD), p["w_branch"][3]]
    q["ssd_conv_w_pad"] = _pad_xbc_vec(p["ssd_conv_w"])
    q["ssd_conv_b_pad"] = _pad_xbc_vec(p["ssd_conv_b"])[None, :]
    q["ssd_norm_pad"] = _pad_blocks(p["ssd_norm"], 0, N_HEADS, HEAD)[None, :]
    return q


def _row(v):
    return v.reshape(1, -1)


def _scal3(v):
    return v.reshape(N_HEADS, 1, 1)


def _layer_fwd(x, p_emb, w, rope, tag):
    n = lambda s: f"{s}_{tag}"
    sv = {"x": x}
    h = _rms_fwd(x, _row(w["g_mix"]), name=n("rms_mix"))
    sv["h"] = h
    u_all = _mm(h, w["in_all"], name=n("in_proj"))
    u = {k: _in_cols(u_all, k) for k in IN_OFFSETS}
    sv["u"] = u

    cqn = _rms_fwd(u["cq"], _row(w["q_norm"]), name=n("rms_q"))
    ckvn = _rms_fwd(u["ckv"], _row(w["kv_norm"]), name=n("rms_kv"))
    q_pad = _mm(cqn, w["uq"], name=n("uq"))
    kv2 = _mm(ckvn, w["ukv"], name=n("ukv"))
    qc, kc, vc = _att_prep(q_pad, kv2, u["kr"], *rope, name=n("att_prep"))
    y_a, lse = _flash_fwd(qc, kc, vc, name=n("flash_fwd"))
    sv.update(cqn=cqn, ckvn=ckvn, qc=qc, kc=kc, vc=vc, y_a=y_a, lse=lse)

    pool_d = _pool_fwd(u["pool"], name=n("pool_fwd"))
    yb_pre, y_b = _mm(pool_d, w["pool_bd"], epilogue=lambda acc, sc: (acc, acc * sc),
                      rowvecs=[_row(w["pool_scale"])], out_dtypes=(F32, BF16), name=n("pool_mm"))
    sv.update(pool_d=pool_d, yb_pre=yb_pre, y_b=y_b)

    xbc_c = _conv_fwd(u["xbc"], w["ssd_conv_w_pad"], w["ssd_conv_b_pad"], silu=True, name=n("ssd_conv"))
    dt8 = lax.slice_in_dim(u_all, IN_OFFSETS["dt"][0], IN_OFFSETS["dt"][0] + N_HEADS, axis=1)
    dtcol = dt8.T[:, :, None]
    dtrow = dt8.T[:, None, :]
    ssd_par = (_scal3(w["ssd_dt_bias"]), _scal3(w["ssd_a_log"]), _scal3(w["ssd_d"]))
    y_ssd, states = _ssd_fwd(xbc_c, dtcol, dtrow, *ssd_par, name=n("ssd_fwd"))

    def ssd_post(yv, zv, gv):
        xh, _ = _rms_parts(yv * _silu(zv), MIX)
        return xh * gv

    y_c = _rowwise(ssd_post, [y_ssd, u["z"]], [w["ssd_norm_pad"]], [(N_HEADS * LANES, BF16, "row")], name=n("ssd_post"))
    sv.update(xbc_c=xbc_c, dtcol=dtcol, dtrow=dtrow, y_ssd=y_ssd, states=states, y_c=y_c)

    xc = _conv_fwd(u["lru_x"], w["lru_conv_w"], _row(w["lru_conv_b"]), silu=False, name=n("lru_conv"))
    pre = _mm(xc, w["lru_bd"], name=n("lru_mm"))
    lru_par = (_row(w["lru_lambda"]), _row(w["lru_b_a"]), _row(w["lru_b_i"]))
    y_d, h_lru = _lru_fwd(pre, xc, u["lru_g"], *lru_par, name=n("lru_fwd"))
    sv.update(xc=xc, pre=pre, h_lru=h_lru, y_d=y_d)

    merged, *ybs = _branch_merge([y_a, y_b, y_c, y_d], w["br"], u_all, name=n("branch_merge"))
    x1 = _mm(merged, w["w_out"], epilogue=lambda acc, xr: (acc + xr,), tiles=[x], name=n("out_proj"))
    sv.update(ybs=ybs, merged=merged, x1=x1)

    h2 = _rms_fwd(x1, _row(w["g_mlp"]), name=n("rms_mlp"))
    a_ff, f_ff = _mm(h2, w["w_ff1"], epilogue=lambda acc: (acc, jnp.square(jnp.maximum(acc, 0.0))),
                     out_dtypes=(BF16, BF16), name=n("ff1"))
    x2 = _mm(f_ff, w["w_ff2"], epilogue=lambda acc, xr: (acc + xr,), tiles=[x1], name=n("ff2"))
    sv.update(h2=h2, a_ff=a_ff, f_ff=f_ff, x2=x2)

    h3 = _rms_fwd(x2, _row(w["g_ple"]), name=n("rms_ple"))
    e_ple = _mm(p_emb, w["w_ple"], name=n("ple_emb"))
    x3, gt_ple = _mm(h3, w["w_ple_gate"], epilogue=lambda acc, ev, xr: (xr + ev * _sigmoid(acc), _sigmoid(acc)),
                     tiles=[e_ple, x2], out_dtypes=(F32, F32), name=n("ple_gate"))
    sv.update(h3=h3, e_ple=e_ple, gt_ple=gt_ple, p_emb=p_emb)
    return x3, sv


def _layer_bwd(dx3, sv, w, rope, tag, early=None, mid=None):
    n = lambda s: f"{s}_{tag}"
    gr = {}
    u = sv["u"]

    de, dpre = _rowwise(lambda d, gt, ev: (d * gt, d * ev * gt * (1.0 - gt)), [dx3, sv["gt_ple"], sv["e_ple"]], [],
                        [(D_MODEL, BF16, "row"), (D_MODEL, BF16, "row")], name=n("ple_bwd"))
    gr["w_ple"] = _mm(sv["p_emb"], de, ta=True, name=n("d_w_ple"))
    gr["w_ple_gate"] = _mm(sv["h3"], dpre, ta=True, name=n("d_w_ple_gate"))
    dh3 = _mm(dpre, w["w_ple_gate"], tb=True, out_dtypes=(BF16,), name=n("d_h3"))
    dx2, dg = _rms_bwd(sv["x2"], _row(w["g_ple"]), dh3, dx3, name=n("rms_ple_bwd"))
    gr["g_ple"] = dg[0]

    gr["w_ff2"] = _mm(sv["f_ff"], dx2, ta=True, name=n("d_w_ff2"))
    da = _mm(dx2, w["w_ff2"], tb=True, epilogue=lambda acc, av: (acc * 2.0 * jnp.maximum(av, 0.0),),
             tiles=[sv["a_ff"]], out_dtypes=(BF16,), name=n("d_a_ff"))
    gr["w_ff1"] = _mm(sv["h2"], da, ta=True, name=n("d_w_ff1"))
    dh2 = _mm(da, w["w_ff1"], tb=True, out_dtypes=(BF16,), name=n("d_h2"))
    dx1, dg = _rms_bwd(sv["x1"], _row(w["g_mlp"]), dh2, dx2, name=n("rms_mlp_bwd"))
    gr["g_mlp"] = dg[0]
    if early is not None:
        dx1 = early(gr, dx1)

    gr["w_out"] = _mm(sv["merged"], dx1, ta=True, name=n("d_w_out"))
    dmerged = _mm(dx1, w["w_out"], tb=True, name=n("d_merged"))

    def merge_bwd(dm, gts, y0, y1, y2, y3):
        dys, dgs = [], []
        for b, yb in enumerate((y0, y1, y2, y3)):
            sg = _sigmoid(gts[:, b * D_MODEL:(b + 1) * D_MODEL])
            dys.append(dm * sg)
            dgs.append(dm * yb * sg * (1.0 - sg))
        return (*dys, jnp.concatenate(dgs, axis=1))

    *dybs, dgates = _rowwise(merge_bwd, [dmerged, u["gates"]] + sv["ybs"], [],
                             [(D_MODEL, BF16, "row")] * 4 + [(4 * D_MODEL, BF16, "row")], name=n("merge_bwd"))
    ys = [sv["y_a"], sv["y_b"], sv["y_c"], sv["y_d"]]
    dwb = [_mm(ys[b], dybs[b], ta=True, name=n(f"d_w_branch{b}")) for b in range(4)]
    gr["w_branch"] = jnp.stack([_unpad_blocks(dwb[0], 0, N_HEADS, HEAD), dwb[1],
                                _unpad_blocks(dwb[2], 0, N_HEADS, HEAD), dwb[3]])
    dy_a = _mm(dybs[0], w["br"][0], tb=True, out_dtypes=(BF16,), name=n("d_y_a"))
    dy_b = _mm(dybs[1], w["br"][1], tb=True, name=n("d_y_b"))
    dy_c = _mm(dybs[2], w["br"][2], tb=True, name=n("d_y_c"))
    dy_d = _mm(dybs[3], w["br"][3], tb=True, name=n("d_y_d"))
    du = {"gates": dgates}

    lru_par = (_row(w["lru_lambda"]), _row(w["lru_b_a"]), _row(w["lru_b_i"]))
    dpa, dpi, dxc_direct, du["lru_g"], dlam, dba, dbi = _lru_bwd(
        sv["pre"], sv["xc"], u["lru_g"], *lru_par, sv["h_lru"], dy_d, name=n("lru_bwd"))
    dpre_lru = jnp.concatenate([dpa, dpi], axis=1)
    d_bd = _mm(sv["xc"], dpre_lru, ta=True, name=n("d_lru_w"))
    gr["lru_w_a"] = _block_diag_t(d_bd[:, :MIX], N_HEADS)
    gr["lru_w_i"] = _block_diag_t(d_bd[:, MIX:], N_HEADS)
    gr["lru_lambda"], gr["lru_b_a"], gr["lru_b_i"] = dlam[0], dba[0], dbi[0]
    dxc = _mm(dpre_lru, w["lru_bd"], tb=True, epilogue=lambda acc, t: (acc + t,), tiles=[dxc_direct], name=n("d_xc"))
    du["lru_x"], gr["lru_conv_w"], dcb = _conv_bwd(u["lru_x"], w["lru_conv_w"], _row(w["lru_conv_b"]), dxc,
                                                  silu=False, name=n("lru_conv_bwd"))
    gr["lru_conv_b"] = dcb[0]

    def ssd_post_bwd(dyc, yv, zv, gv):
        sz = _silu(zv)
        dyz, dgain = _rms_bwd_math(yv * sz, gv, dyc, MIX)
        return dyz * sz, dyz * yv * _silu_grad(zv), dgain

    dy_ssd, du["z"], dgain = _rowwise(ssd_post_bwd, [dy_c, sv["y_ssd"], u["z"]], [w["ssd_norm_pad"]],
                                      [(N_HEADS * LANES, F32, "row"), (N_HEADS * LANES, BF16, "row"),
                                       (N_HEADS * LANES, F32, "acc")], name=n("ssd_post_bwd"))
    gr["ssd_norm"] = _unpad_blocks(dgain[0], 0, N_HEADS, HEAD)
    ssd_par = (_scal3(w["ssd_dt_bias"]), _scal3(w["ssd_a_log"]), _scal3(w["ssd_d"]))
    dxs, dbg, dcg, ddt, dbias, dalog, dd = _ssd_bwd(sv["xbc_c"], sv["dtcol"], sv["dtrow"], *ssd_par, sv["states"],
                                                    dy_ssd, name=n("ssd_bwd"))
    s = dxs.shape[0]
    dxbc_c = jnp.concatenate([dxs, dbg, dcg], axis=1)
    gr["ssd_dt_bias"], gr["ssd_a_log"], gr["ssd_d"] = dbias[:, 0, 0], dalog[:, 0, 0], dd[:, 0, 0]
    du["xbc"], dcw, dcb = _conv_bwd(u["xbc"], w["ssd_conv_w_pad"], w["ssd_conv_b_pad"], dxbc_c, silu=True,
                                    name=n("ssd_conv_bwd"))
    gr["ssd_conv_w"], gr["ssd_conv_b"] = _unpad_xbc_vec(dcw), _unpad_xbc_vec(dcb[0])
    du["dt"] = jnp.pad(ddt[:, :, 0].T, ((0, 0), (0, LANES - N_HEADS)))

    dyb_pre, dscale = _rowwise(lambda d, yp, sc: (d * sc, _colsum(d * yp)), [dy_b, sv["yb_pre"]],
                               [_row(w["pool_scale"])], [(MIX, BF16, "row"), (MIX, F32, "acc")], name=n("pool_scale_bwd"))
    gr["pool_scale"] = dscale[0]
    gr["w_pool"] = _block_diag_t(_mm(sv["pool_d"], dyb_pre, ta=True, name=n("d_w_pool")), 4)
    dd_pool = _mm(dyb_pre, w["pool_bd"], tb=True, name=n("d_pool_d"))
    du["pool"] = _pool_bwd(dd_pool, name=n("pool_bwd"))

    delta = _att_delta(sv["y_a"], dy_a, name=n("att_delta"))
    to_row = lambda t: t.reshape(N_HEADS, 1, s)
    dqc, dkc, dvc = _flash_bwd(sv["qc"], sv["kc"], sv["vc"], dy_a, to_row(sv["lse"]), to_row(delta), name=n("flash_bwd"))
    dq_pad, du["kr"] = _att_prep_bwd(dqc, dkc, *rope, name=n("att_prep_bwd"))
    d_uq = _mm(sv["cqn"], dq_pad, ta=True, name=n("d_w_uq"))
    gr["w_uq"] = _unpad_blocks(d_uq, 1, N_HEADS, HEAD + QK_ROPE)
    dcqn = _mm(dq_pad, w["uq"], tb=True, out_dtypes=(BF16,), name=n("d_cqn"))
    du["cq"], dg = _rms_bwd(u["cq"], _row(w["q_norm"]), dcqn, name=n("rms_q_bwd"))
    gr["q_norm"] = dg[0]
    dkv2 = jnp.concatenate([dkc, dvc], axis=1).astype(BF16)
    d_ukv = _mm(sv["ckvn"], dkv2, ta=True, name=n("d_w_ukv"))
    wk = N_HEADS * LANES
    dk_real = _unpad_blocks(d_ukv[:, :wk], 1, N_HEADS, HEAD).reshape(KV_LORA, N_HEADS, HEAD)
    dv_real = _unpad_blocks(d_ukv[:, wk:], 1, N_HEADS, HEAD).reshape(KV_LORA, N_HEADS, HEAD)
    gr["w_ukv"] = jnp.concatenate([dk_real, dv_real], axis=2).reshape(KV_LORA, N_HEADS * 2 * HEAD)
    dckvn = _mm(dkv2, w["ukv"], tb=True, out_dtypes=(BF16,), name=n("d_ckvn"))
    du["ckv"], dg = _rms_bwd(u["ckv"], _row(w["kv_norm"]), dckvn, name=n("rms_kv_bwd"))
    gr["kv_norm"] = dg[0]

    if mid is not None:
        du["dt"] = du["dt"] + mid(gr)
    du_all = _w_in_fused({k: v.astype(BF16) for k, v in du.items()})
    dw_all = _mm(sv["h"], du_all, ta=True, name=n("d_w_in"))
    gr["w_in"] = _w_in_ungroup({k: dw_all[:, off:off + width] for k, off, width in IN_LAYOUT})
    dh = _mm(du_all, w["in_all"], tb=True, name=n("d_h"))
    dx, dg = _rms_bwd(sv["x"], _row(w["g_mix"]), dh, dx1, name=n("rms_mix_bwd"))
    gr["g_mix"] = dg[0]
    return dx, gr


def _pack_rows(n_elems):
    per = PACK_W * PACK_ROWS
    return -(-n_elems // per) * PACK_ROWS


def _pack_flat(parts, dtype):
    flat = jnp.concatenate([p.reshape(-1).astype(dtype) for p in parts])
    rows = _pack_rows(flat.shape[0])
    return jnp.pad(flat, (0, rows * PACK_W - flat.shape[0])).reshape(rows, PACK_W)


def _unpack_flat(buf, shapes):
    lead = buf.shape[:-2]
    flat = buf.reshape(lead + (-1,))
    out, off = [], 0
    for shp in shapes:
        size = int(np.prod(shp))
        out.append(flat[..., off:off + size].reshape(lead + tuple(shp)))
        off += size
    return out


def _merge_shards(t, axis):
    return jnp.concatenate([t[i] for i in range(4)], axis=axis)


def _split_shards(t, axis):
    return jnp.stack(jnp.split(t, 4, axis=axis))


def _rope_tables(positions):
    inv = 1.0 / (ROPE_THETA ** (jnp.arange(0, QK_ROPE, 2, dtype=F32) / QK_ROPE))
    ang = positions.astype(F32)[:, None] * inv
    cos, sin = jnp.cos(ang), jnp.sin(ang)
    s = ang.shape[0]
    half = QK_ROPE // 2
    z = lambda n_: jnp.zeros((s, n_), F32)
    cos_t = jnp.concatenate([jnp.ones((s, HEAD), F32), cos, cos, jnp.ones((s, LANES - HEAD - QK_ROPE), F32)], axis=1)
    sin_p = jnp.concatenate([z(HEAD + half), sin, z(LANES - HEAD - QK_ROPE)], axis=1)
    sin_m = jnp.concatenate([z(HEAD), -sin, z(half + LANES - HEAD - QK_ROPE)], axis=1)
    return cos_t, sin_p, sin_m


def _loss_head(x, g, target, *, name):
    d = x.shape[1]

    def fn(xv, tv, gv):
        xh, r = _rms_parts(xv, d)
        y = xh * gv
        err = y - tv
        dy = err * (1.0 / d)
        dxh = dy * gv
        dx = r * (dxh - xh * (jnp.sum(dxh * xh, axis=-1, keepdims=True) * (1.0 / d)))
        return dx, _colsum(dy * xh), _colsum(err * err) * (0.5 / d)

    return _rowwise(fn, [x, target], [g], [(d, F32, "row"), (d, F32, "acc"), (d, F32, "acc")], name=name)


MATS = tuple((nm, ax) for nm, ax in BIG if nm not in CONV_SHARDED)
EARLY_L0 = ("w_ple", "w_ple_gate", "w_ff2", "w_ff1")


def _grad_view(g, ax_layer):
    if ax_layer == 0:
        return g.reshape(4, g.shape[0] // 4, g.shape[1])
    return g.reshape(1, -1, g.shape[-1])


def _reduce_start(grads_l, mats, c_idx, tag):
    views = [_grad_view(grads_l[nm], ax - 1) for nm, ax in mats]
    got = _send_half(views, name="send_half_" + tag)
    parts = []
    for (nm, ax), v, gt in zip(mats, views, got):
        both = _chip_sum_half(v, gt, c_idx, name=f"chip_sum_{nm}_{tag}")
        parts.append(both if ax == 1 else _split_shards(both[0], 1))
    return _push_start(parts, scatter=True, name="push_grads_" + tag)


def _reduce_finish(state, after, k_chip, tag):
    send_sems, recv_sems, parts, lands, _ = state
    parts, landed = _push_wait(send_sems, recv_sems, parts, lands, after, name="wait_grads_" + tag)
    mine = [lax.dynamic_update_index_in_dim(t, lax.dynamic_index_in_dim(p, k_chip, 0, keepdims=False), k_chip, 0)
            for t, p in zip(landed, parts)]
    return list(zip(mine, _swap_with_sibling(mine, name="swap_halves_" + tag)))


def _step(args):
    x = args["x"][0]
    c_idx = lax.axis_index("c")
    k_chip = 2 * lax.axis_index("x") + lax.axis_index("y")

    mats = MATS
    mine = [[args[nm][l].astype(BF16) for nm, _ in mats] for l in range(2)]
    gathered0 = _gather_halves(mine[0])
    convs = [(nm, ax) for nm, ax in BIG if nm in CONV_SHARDED]
    conv_all = _gather_all(_pack_flat([args[nm] for nm, _ in convs], F32), name="gather_conv_taps")[0::2]
    mine1, gathered0, conv_all = lax.optimization_barrier((mine[1], gathered0, conv_all))
    gathered0 = [lax.dynamic_update_index_in_dim(t, own, k_chip, 0) for t, own in zip(gathered0, mine[0])]
    push1 = _push_start(mine1, scatter=False, name="push_weights_l1")
    full_conv = {nm: _merge_shards(t, ax)
                 for (nm, ax), t in zip(convs, _unpack_flat(conv_all, [args[nm].shape for nm, _ in convs]))}
    rope = _rope_tables(args["positions"][0])

    def layer_weights(l, gathered):
        p = {nm: _merge_shards(t, ax - 1) for (nm, ax), t in zip(mats, gathered)}
        p.update({nm: full_conv[nm][l] for nm in CONV_SHARDED})
        p.update({nm: args[nm][l] for nm in SMALL if nm != "g_final"})
        return _layer_weights(p)

    layers = [layer_weights(0, gathered0), None]
    layers[0]["g_mix"] = layers[0]["g_mix"] + push1[4][0, 0]
    x, sv0 = _layer_fwd(x, args["p"][0, 0], layers[0], rope, "l0")
    own1, landed1 = _push_wait(push1[0], push1[1], push1[2], push1[3], x, name="wait_weights_l1")
    layers[1] = layer_weights(1, [lax.dynamic_update_index_in_dim(t, own, k_chip, 0) for t, own in zip(landed1, own1)])
    x, sv1 = _layer_fwd(x, args["p"][1, 0], layers[1], rope, "l1")
    saved = [sv0, sv1]

    dx, dg_final, loss_part = _loss_head(x, _row(args["g_final"]), args["loss_target"][0], name="loss_head")
    loss = lax.psum(jnp.sum(loss_part), ("x", "y", "c"))

    early_mats = tuple(mt for mt in MATS if mt[0] in EARLY_L0)
    rest_mats = tuple(mt for mt in MATS if mt[0] == "w_in")
    mid_mats = tuple(mt for mt in MATS if mt not in early_mats + rest_mats)
    grads, reduce0_early, reduce0_mid = [None, None], [], []

    def early0(gr, dx1):
        reduce0_early.append(_reduce_start(gr, early_mats, c_idx, "l0e"))
        return dx1 + reduce0_early[0][4][0, 0]

    def mid0(gr):
        reduce0_mid.append(_reduce_start(gr, mid_mats, c_idx, "l0m"))
        return reduce0_mid[0][4][0, 0]

    dx, grads[1] = _layer_bwd(dx, saved[1], layers[1], rope, "l1")
    reduce1 = _reduce_start(grads[1], MATS, c_idx, "l1")
    dx, grads[0] = _layer_bwd(dx + reduce1[4][0, 0], saved[0], layers[0], rope, "l0", early=early0, mid=mid0)
    g_all = {nm: jnp.stack([grads[0][nm], grads[1][nm]]) for nm in SMALL + CONV_SHARDED if nm != "g_final"}
    g_all["g_final"] = dg_final[0]
    all_names = SMALL + CONV_SHARDED
    all_shapes = [g_all[nm].shape for nm in all_names]
    packed = _pack_flat([g_all[nm] for nm in all_names], F32)
    sibling = _swap_with_sibling([packed], name="swap_small_grads")[0]
    pair = jnp.where(c_idx == 0, jnp.stack([packed, sibling]), jnp.stack([sibling, packed]))
    small_all = _gather_same_core(_sum_slots(pair, name="sum_cores"), name="gather_small_grads")
    g0_mats, small_all = lax.optimization_barrier(({nm: grads[0][nm] for nm, _ in rest_mats}, small_all))
    reduce0 = _reduce_start(g0_mats, rest_mats, c_idx, "l0")
    small_sum = _sum_slots(small_all, name="sum_chips")
    g_red = dict(zip(all_names, _unpack_flat(small_sum, all_shapes)))
    for nm, ax in BIG:
        if nm in CONV_SHARDED:
            width = args[nm].shape[ax]
            g_red[nm] = lax.dynamic_slice_in_dim(g_red[nm], k_chip * width, width, axis=ax)
    small_shapes = [args[nm].shape for nm in SMALL]
    pack_small = lambda src: _pack_flat([src(nm) for nm in SMALL], F32)
    upd_small = _adamw(pack_small(lambda nm: args[nm]), pack_small(lambda nm: g_red[nm]),
                       pack_small(lambda nm: args["m_" + nm]), pack_small(lambda nm: args["v_" + nm]), name="adamw_small")
    upd = {nm: trip for nm, trip in zip(SMALL, zip(*[_unpack_flat(t, small_shapes) for t in upd_small]))}
    for nm in CONV_SHARDED:
        upd[nm] = _adamw(args[nm], g_red[nm], args["m_" + nm], args["v_" + nm], name="adamw_" + nm)

    names = lambda mats_: [nm for nm, _ in mats_]
    slots0 = dict(zip(names(early_mats), _reduce_finish(reduce0_early[0], dx, k_chip, "l0e")))
    slots0.update(zip(names(mid_mats), _reduce_finish(reduce0_mid[0], dx, k_chip, "l0m")))
    slots0.update(zip(names(rest_mats), _reduce_finish(reduce0, upd_small[0], k_chip, "l0")))
    slots1 = dict(zip(names(MATS), _reduce_finish(reduce1, dx, k_chip, "l1")))
    for nm, _ in MATS:
        g_red[nm], *upd[nm] = _adamw_slots(args[nm], [slots0[nm], slots1[nm]], args["m_" + nm], args["v_" + nm],
                                           c_idx, name="adamw_" + nm)

    outs = [loss, dx[None]]
    outs += [g_red[nm] for nm in WEIGHTS]
    for i in range(3):
        outs += [upd[nm][i] for nm in WEIGHTS]
    return tuple(outs)


_ARG_NAMES = ("x", "p", "positions") + WEIGHTS + ("loss_target",) + tuple("m_" + nm for nm in WEIGHTS) \
    + tuple("v_" + nm for nm in WEIGHTS)


def kernel(*arrays):
    assert len(arrays) == len(_ARG_NAMES), len(arrays)
    return _step(dict(zip(_ARG_NAMES, arrays)))
```

```python
import math

import jax
import jax.numpy as jnp
import numpy as np
from jax import lax
from jax.experimental import pallas as pl
from jax.experimental.pallas import tpu as pltpu

F32 = jnp.float32
BF16 = jnp.bfloat16
MXU_DTYPE = BF16
LANES = 128
VMEM_LIMIT = 56 * 1024 * 1024
MM_VMEM_BUDGET = 36 * 1024 * 1024
ELEMENTWISE_BLOCK_BYTES = 2 * 1024 * 1024

D_MODEL = 1024
N_HEADS = 8
HEAD = 64
QK_ROPE = 32
Q_LORA = 384
KV_LORA = 256
MIX = 512
SSD_CHUNK = 128
CONV_W = 4
POOL_WINDOWS = (2, 4, 8, 16)
LRU_C = 8.0
EPS = 1e-6
ROPE_THETA = 10000.0
ATT_SCALE = (HEAD + QK_ROPE) ** -0.5
SPLIT_SIZES = (Q_LORA, KV_LORA, QK_ROPE, MIX, MIX, 768, N_HEADS, MIX, MIX, 4 * D_MODEL)
IN_LAYOUT = (("gates", 0, 4096), ("z", 4096, 1024), ("pool", 5120, 512), ("lru_g", 5632, 512), ("lru_x", 6144, 512),
             ("cq", 6912, 384), ("ckv", 7424, 256), ("xbc", 7680, 1536), ("kr", 9216, 128), ("dt", 9344, 128))
IN_OFFSETS = {name: (off, width) for name, off, width in IN_LAYOUT}
IN_ALL_COLS = 9728

ADAM_LR, ADAM_B1, ADAM_B2, ADAM_EPS, ADAM_WD, ADAM_STEP = 0.001, 0.9, 0.999, 1e-08, 0.01, 10

BIG = (("w_in", 2), ("w_uq", 2), ("w_ukv", 2), ("ssd_conv_w", 2), ("lru_conv_w", 2), ("w_branch", 3),
       ("w_out", 1), ("w_ff1", 2), ("w_ff2", 1), ("w_ple_gate", 1), ("w_ple", 2))
SMALL = ("g_mix", "q_norm", "kv_norm", "w_pool", "pool_scale", "ssd_conv_b", "ssd_dt_bias", "ssd_a_log",
         "ssd_d", "ssd_norm", "lru_conv_b", "lru_w_a", "lru_b_a", "lru_w_i", "lru_b_i", "lru_lambda",
         "g_mlp", "g_ple", "g_final")
WEIGHTS = ("g_mix", "w_in", "q_norm", "w_uq", "kv_norm", "w_ukv", "w_pool", "pool_scale", "ssd_conv_w",
           "ssd_conv_b", "ssd_dt_bias", "ssd_a_log", "ssd_d", "ssd_norm", "lru_conv_w", "lru_conv_b", "lru_w_a",
           "lru_b_a", "lru_w_i", "lru_b_i", "lru_lambda", "w_branch", "w_out", "g_mlp", "w_ff1", "w_ff2", "g_ple",
           "w_ple_gate", "w_ple", "g_final")
CONV_SHARDED = ("ssd_conv_w", "lru_conv_w")
PACK_W = 1024
PACK_ROWS = 64


def _cparams(sem, vmem=VMEM_LIMIT):
    return pltpu.CompilerParams(dimension_semantics=sem, vmem_limit_bytes=vmem)


def _pick(n, cands):
    for c in cands:
        if n % c == 0:
            return c
    return n


class _Cols:
    def __init__(self, arr, off, width):
        self.arr, self.off, self.width = arr, off, width

    shape = property(lambda self: (self.arr.shape[0], self.width))
    dtype = property(lambda self: self.arr.dtype)


def _arr(x):
    return x.arr if isinstance(x, _Cols) else x


def _off(x, unit):
    off = x.off if isinstance(x, _Cols) else 0
    assert off % unit == 0, (off, unit)
    return off // unit


def _sigmoid(x):
    return 1.0 / (1.0 + jnp.exp(-x))


def _silu(x):
    return x * _sigmoid(x)


def _silu_grad(x):
    s = _sigmoid(x)
    return s * (1.0 + x * (1.0 - s))


def _softplus(x):
    e = jnp.exp(-jnp.abs(x))
    log1p_e = jnp.where(e < 1e-3, e * (1.0 - e * (0.5 - e * (1.0 / 3.0))), jnp.log(1.0 + e))
    return jnp.maximum(x, 0.0) + log1p_e


_GELU_C = math.sqrt(2.0 / math.pi)


def _gelu(x):
    t = jnp.tanh(_GELU_C * (x + 0.044715 * x * x * x))
    return 0.5 * x * (1.0 + t)


def _gelu_grad(x):
    t = jnp.tanh(_GELU_C * (x + 0.044715 * x * x * x))
    return 0.5 * (1.0 + t) + 0.5 * x * (1.0 - t * t) * _GELU_C * (1.0 + 3.0 * 0.044715 * x * x)


def _neg_expm1(x):
    series = -x * (1.0 + 0.5 * x * (1.0 + (1.0 / 3.0) * x * (1.0 + 0.25 * x)))
    return jnp.where(x > -0.05, series, 1.0 - jnp.exp(x))


def _shift_down(x, k, row):
    return jnp.where(row >= k, pltpu.roll(x, k, 0), 0.0)


def _shift_up(x, k, row):
    n = x.shape[0]
    return jnp.where(row < n - k, pltpu.roll(x, n - k, 0), 0.0)


def _cumsum_rows(x, row):
    d = 1
    while d < x.shape[0]:
        x = x + _shift_down(x, d, row)
        d *= 2
    return x


def _rev_cumsum_rows(x, row):
    d = 1
    while d < x.shape[0]:
        x = x + _shift_up(x, d, row)
        d *= 2
    return x


def _cumsum_lanes(x, col):
    d = 1
    while d < x.shape[1]:
        x = x + jnp.where(col >= d, pltpu.roll(x, d, 1), 0.0)
        d *= 2
    return x


def _dot(a, b, ta=False, tb=False):
    dn = (((0 if ta else 1,), (1 if tb else 0,)), ((), ()))
    return lax.dot_general(a.astype(MXU_DTYPE), b.astype(MXU_DTYPE), dn, preferred_element_type=F32)


def _mm_tiles(m, n, k, a_bytes, b_bytes, mn_bytes):
    best = None
    for tm in (1024, 512, 384, 256, 128):
        for tn in (1024, 512, 384, 256, 128):
            for tk in (2048, 1024, 512, 384, 256, 128):
                if m % tm or n % tn or k % tk:
                    continue
                vmem = 2 * (tm * tk * a_bytes + tk * tn * b_bytes) + 2 * tm * tn * mn_bytes + 4 * tm * tn
                vmem += 2 * (tm * tk + tk * tn)
                if vmem > MM_VMEM_BUDGET:
                    continue
                steps = (m // tm) * (n // tn) * (k // tk)
                key = (steps, vmem)
                if best is None or key < best[0]:
                    best = (key, (tm, tn, tk))
    assert best is not None, (m, n, k)
    return best[1]


def _mm(a, b, *, ta=False, tb=False, epilogue=None, tiles=(), rowvecs=(), out_dtypes=(F32,), name):
    m, k = (a.shape[1], a.shape[0]) if ta else a.shape
    n = b.shape[0] if tb else b.shape[1]
    assert (b.shape[1] if tb else b.shape[0]) == k, (a.shape, b.shape, ta, tb)
    mn_bytes = sum(t.dtype.itemsize for t in tiles) + sum(jnp.dtype(dt).itemsize for dt in out_dtypes)
    tm, tn, tk = _mm_tiles(m, n, k, a.dtype.itemsize, b.dtype.itemsize, mn_bytes)
    nk = k // tk
    nt, nr, no = len(tiles), len(rowvecs), len(out_dtypes)

    def body(*refs):
        a_ref, b_ref = refs[0], refs[1]
        tile_refs = refs[2:2 + nt]
        row_refs = refs[2 + nt:2 + nt + nr]
        out_refs = refs[2 + nt + nr:2 + nt + nr + no]
        acc_ref = refs[-1]
        kk = pl.program_id(2)

        @pl.when(kk == 0)
        def _():
            acc_ref[...] = jnp.zeros_like(acc_ref)

        acc_ref[...] += _dot(a_ref[...], b_ref[...], ta, tb)

        @pl.when(kk == nk - 1)
        def _():
            acc = acc_ref[...]
            if epilogue is None:
                outs = (acc,)
            else:
                outs = epilogue(acc, *[t[...] for t in tile_refs], *[r[...] for r in row_refs])
            for o_ref, o in zip(out_refs, outs):
                o_ref[...] = o.astype(o_ref.dtype)

    a_spec = pl.BlockSpec((tk, tm), lambda i, j, kk: (kk, i)) if ta else pl.BlockSpec((tm, tk), lambda i, j, kk: (i, kk))
    b_spec = pl.BlockSpec((tn, tk), lambda i, j, kk: (j, kk)) if tb else pl.BlockSpec((tk, tn), lambda i, j, kk: (kk, j))
    mn_spec = pl.BlockSpec((tm, tn), lambda i, j, kk: (i, j))
    row_spec = pl.BlockSpec((1, tn), lambda i, j, kk: (0, j))
    tile_specs = [pl.BlockSpec((tm, tn), lambda i, j, kk, ob=_off(t, tn): (i, j + ob)) for t in tiles]
    outs = pl.pallas_call(
        body, name=name,
        grid=(m // tm, n // tn, nk),
        in_specs=[a_spec, b_spec] + tile_specs + [row_spec] * nr,
        out_specs=[mn_spec] * no,
        out_shape=[jax.ShapeDtypeStruct((m, n), dt) for dt in out_dtypes],
        scratch_shapes=[pltpu.VMEM((tm, tn), F32)],
        compiler_params=_cparams(("parallel", "parallel", "arbitrary")),
    )(a, b, *[_arr(t) for t in tiles], *rowvecs)
    return outs[0] if no == 1 else tuple(outs)


def _branch_merge(ys, ws, u_all, *, name):
    s, d = ys[0].shape[0], ws[0].shape[1]
    tm, tn = _pick(s, (512, 256, 128)), _pick(d, (512, 256, 128))
    nb = len(ys)

    def body(*refs):
        y_refs, w_refs, g_refs = refs[:nb], refs[nb:2 * nb], refs[2 * nb:3 * nb]
        merged_ref, yb_refs = refs[3 * nb], refs[3 * nb + 1:]
        merged = None
        for y_ref, w_ref, g_ref, yb_ref in zip(y_refs, w_refs, g_refs, yb_refs):
            acc = _dot(y_ref[...], w_ref[...])
            yb_ref[...] = acc.astype(yb_ref.dtype)
            term = _sigmoid(g_ref[...]) * acc
            merged = term if merged is None else merged + term
        merged_ref[...] = merged

    mn = pl.BlockSpec((tm, tn), lambda i, j: (i, j))
    in_specs = [pl.BlockSpec((tm, y.shape[1]), lambda i, j: (i, 0)) for y in ys]
    in_specs += [pl.BlockSpec((w.shape[0], tn), lambda i, j: (0, j)) for w in ws]
    in_specs += [pl.BlockSpec((tm, tn), lambda i, j, ob=b * d // tn: (i, j + ob)) for b in range(nb)]
    return pl.pallas_call(
        body, name=name, grid=(s // tm, d // tn), in_specs=in_specs, out_specs=[mn] * (nb + 1),
        out_shape=[jax.ShapeDtypeStruct((s, d), F32)] + [jax.ShapeDtypeStruct((s, d), BF16)] * nb,
        compiler_params=_cparams(("parallel", "parallel")),
    )(*ys, *ws, *[u_all] * nb)


def _rowwise(fn, rows, fulls, outs, *, name, tm=None):
    r = rows[0].shape[0]
    if tm is None:
        widest = max([x.shape[1] for x in rows] + [o[0] for o in outs])
        tm = _pick(r, (max(8, min(512, (512 * 1024) // widest)), 256, 128, 64, 32, 16, 8))
    nrow, nfull, nout = len(rows), len(fulls), len(outs)

    def body(*refs):
        row_refs = refs[:nrow]
        full_refs = refs[nrow:nrow + nfull]
        out_refs = refs[nrow + nfull:]
        res = fn(*[x[...] for x in row_refs], *[x[...] for x in full_refs])
        if not isinstance(res, (tuple, list)):
            res = (res,)
        step = pl.program_id(0)
        for o_ref, o, spec in zip(out_refs, res, outs):
            if spec[2] == "row":
                o_ref[...] = o.astype(o_ref.dtype)
            else:
                @pl.when(step == 0)
                def _(o_ref=o_ref):
                    o_ref[...] = jnp.zeros_like(o_ref)
                o_ref[...] += o

    in_specs = [pl.BlockSpec((tm, x.shape[1]), lambda i, ob=_off(x, x.shape[1]): (i, ob)) for x in rows]
    in_specs += [pl.BlockSpec(x.shape, lambda i, nd=x.ndim: (0,) * nd) for x in fulls]
    out_specs, out_shape = [], []
    for c, dt, kind in outs:
        if kind == "row":
            out_specs.append(pl.BlockSpec((tm, c), lambda i: (i, 0)))
            out_shape.append(jax.ShapeDtypeStruct((r, c), dt))
        else:
            out_specs.append(pl.BlockSpec((1, c), lambda i: (0, 0)))
            out_shape.append(jax.ShapeDtypeStruct((1, c), F32))
    res = pl.pallas_call(
        body, name=name, grid=(r // tm,), in_specs=in_specs, out_specs=out_specs, out_shape=out_shape,
        compiler_params=_cparams(("arbitrary",)),
    )(*[_arr(x) for x in rows], *fulls)
    return res[0] if nout == 1 else tuple(res)


def _colsum(x):
    return jnp.sum(x, axis=0, keepdims=True)


def _rms_parts(x, n_real):
    r = lax.rsqrt(jnp.sum(x * x, axis=-1, keepdims=True) * (1.0 / n_real) + EPS)
    return x * r, r


def _rms_fwd(x, g, *, n_real=None, out_dtype=BF16, name):
    n_real = n_real or x.shape[1]

    def fn(xv, gv):
        xh, _ = _rms_parts(xv, n_real)
        return xh * gv

    return _rowwise(fn, [x], [g], [(x.shape[1], out_dtype, "row")], name=name)


def _rms_bwd_math(xv, gv, dh, n_real):
    xh, r = _rms_parts(xv, n_real)
    dxh = dh * gv
    dx = r * (dxh - xh * (jnp.sum(dxh * xh, axis=-1, keepdims=True) * (1.0 / n_real)))
    return dx, _colsum(dh * xh)


def _rms_bwd(x, g, dh, res=None, *, name):
    n = x.shape[1]
    if res is None:
        def fn(xv, dhv, gv):
            return _rms_bwd_math(xv, gv, dhv.astype(F32), n)
        rows = [x, dh]
    else:
        def fn(xv, dhv, rv, gv):
            dx, dg = _rms_bwd_math(xv, gv, dhv.astype(F32), n)
            return dx + rv, dg
        rows = [x, dh, res]
    return _rowwise(fn, rows, [g], [(n, F32, "row"), (n, F32, "acc")], name=name)


def _seq_call(body, ins, outs, n_blocks, *, name):
    in_specs, args = [], []
    for x, kind in ins:
        in_specs.append(pl.BlockSpec((x.shape[0], LANES), lambda j, ob=_off(x, LANES): (0, j + ob)))
        args.append(_arr(x))
    out_specs, out_shape = [], []
    for shape, dt in outs:
        out_specs.append(pl.BlockSpec((shape[0], LANES), lambda j: (0, j)))
        out_shape.append(jax.ShapeDtypeStruct(shape, dt))
    res = pl.pallas_call(body, name=name, grid=(n_blocks,), in_specs=in_specs, out_specs=out_specs,
                         out_shape=out_shape, compiler_params=_cparams(("parallel",)))(*args)
    return res[0] if len(outs) == 1 else tuple(res)


def _conv_pre(x, w, b, row):
    acc = x * w[CONV_W - 1:CONV_W, :] + b
    for k in range(CONV_W - 1):
        acc = acc + _shift_down(x, CONV_W - 1 - k, row) * w[k:k + 1, :]
    return acc


def _conv_fwd(x, w, b, *, silu, name):
    s, c = x.shape

    def body(x_ref, w_ref, b_ref, y_ref):
        xv = x_ref[...]
        row = lax.broadcasted_iota(jnp.int32, xv.shape, 0)
        pre = _conv_pre(xv, w_ref[...], b_ref[...], row)
        y_ref[...] = _silu(pre) if silu else pre

    return _seq_call(body, [(x, "seq"), (w, "par"), (b, "par")], [((s, c), F32)], c // LANES, name=name)


def _conv_bwd(x, w, b, dy, *, silu, name):
    s, c = x.shape

    def body(x_ref, w_ref, b_ref, dy_ref, dx_ref, dw_ref, db_ref):
        xv, wv, dv = x_ref[...], w_ref[...], dy_ref[...]
        row = lax.broadcasted_iota(jnp.int32, xv.shape, 0)
        if silu:
            dv = dv * _silu_grad(_conv_pre(xv, wv, b_ref[...], row))
        dx = dv * wv[CONV_W - 1:CONV_W, :]
        dws = [None] * CONV_W
        dws[CONV_W - 1] = _colsum(dv * xv)
        for k in range(CONV_W - 1):
            sh = CONV_W - 1 - k
            dx = dx + _shift_up(dv, sh, row) * wv[k:k + 1, :]
            dws[k] = _colsum(dv * _shift_down(xv, sh, row))
        dx_ref[...] = dx
        for k in range(CONV_W):
            dw_ref[k:k + 1, :] = dws[k]
        db_ref[...] = _colsum(dv)

    return _seq_call(body, [(x, "seq"), (w, "par"), (b, "par"), (dy, "seq")],
                     [((s, c), F32), ((CONV_W, c), F32), ((1, c), F32)], c // LANES, name=name)


def _pool_select(levels):
    g = pl.program_id(0)
    return jnp.where(g == 0, levels[0], jnp.where(g == 1, levels[1], jnp.where(g == 2, levels[2], levels[3])))


def _pool_count(row):
    g = pl.program_id(0)
    w = jnp.where(g == 0, POOL_WINDOWS[0], jnp.where(g == 1, POOL_WINDOWS[1],
                                                     jnp.where(g == 2, POOL_WINDOWS[2], POOL_WINDOWS[3])))
    return jnp.minimum(row + 1, w).astype(F32)


def _pool_fwd(u, *, name):
    def body(u_ref, d_ref):
        uv = u_ref[...]
        row = lax.broadcasted_iota(jnp.int32, uv.shape, 0)
        levels, cur, sh = [], uv, 1
        for _ in POOL_WINDOWS:
            cur = cur + _shift_down(cur, sh, row)
            levels.append(cur)
            sh *= 2
        d_ref[...] = _pool_select(levels) / _pool_count(row) - uv

    return _seq_call(body, [(u, "seq")], [(u.shape, F32)], u.shape[1] // LANES, name=name)


def _pool_bwd(dd, *, name):
    def body(dd_ref, du_ref):
        dv = dd_ref[...]
        row = lax.broadcasted_iota(jnp.int32, dv.shape, 0)
        levels, cur, sh = [], dv / _pool_count(row), 1
        for _ in POOL_WINDOWS:
            cur = cur + _shift_up(cur, sh, row)
            levels.append(cur)
            sh *= 2
        du_ref[...] = _pool_select(levels) - dv

    return _seq_call(body, [(dd, "seq")], [(dd.shape, F32)], dd.shape[1] // LANES, name=name)


def _lru_gates(pre_a, pre_i, xc, lam, b_a, b_i):
    r = _sigmoid(pre_a + b_a)
    i = _sigmoid(pre_i + b_i)
    sp = _softplus(-lam)
    log_a = -LRU_C * r * sp
    a = jnp.exp(log_a)
    mult = jnp.sqrt(_neg_expm1(2.0 * log_a))
    return r, i, sp, a, mult


def _lru_fwd(pre, xc, gate_in, lam, b_a, b_i, *, name):
    s, c = xc.shape
    nb = c // LANES

    def body(pa_ref, pi_ref, xc_ref, g_ref, lam_ref, ba_ref, bi_ref, y_ref, h_ref):
        xv = xc_ref[...]
        row = lax.broadcasted_iota(jnp.int32, xv.shape, 0)
        _, i, _, a, mult = _lru_gates(pa_ref[...], pi_ref[...], xv, lam_ref[...], ba_ref[...], bi_ref[...])
        h = xv * i * mult
        d = 1
        while d < s:
            h = h + a * _shift_down(h, d, row)
            a = a * jnp.where(row >= d, pltpu.roll(a, d, 0), 1.0)
            d *= 2
        h_ref[...] = h
        y_ref[...] = h * _gelu(g_ref[...])

    blk = lambda off: pl.BlockSpec((s, LANES), lambda j: (0, j + off))
    par = pl.BlockSpec((1, LANES), lambda j: (0, j))
    return pl.pallas_call(
        body, name=name, grid=(nb,),
        in_specs=[blk(0), blk(nb), blk(0), blk(_off(gate_in, LANES)), par, par, par],
        out_specs=[blk(0), blk(0)],
        out_shape=[jax.ShapeDtypeStruct((s, c), F32)] * 2,
        compiler_params=_cparams(("parallel",)),
    )(pre, pre, xc, _arr(gate_in), lam, b_a, b_i)


def _lru_bwd(pre, xc, gate_in, lam, b_a, b_i, h, dy, *, name):
    s, c = xc.shape
    nb = c // LANES

    def body(pa_ref, pi_ref, xc_ref, g_ref, lam_ref, ba_ref, bi_ref, h_ref, dy_ref,
             dpa_ref, dpi_ref, dxc_ref, dg_ref, dlam_ref, dba_ref, dbi_ref):
        xv, gv, hv, dv = xc_ref[...], g_ref[...], h_ref[...], dy_ref[...]
        row = lax.broadcasted_iota(jnp.int32, xv.shape, 0)
        r, i, sp, a, mult = _lru_gates(pa_ref[...], pi_ref[...], xv, lam_ref[...], ba_ref[...], bi_ref[...])
        dg_ref[...] = dv * hv * _gelu_grad(gv)
        dh = dv * _gelu(gv)
        an = jnp.where(row < s - 1, pltpu.roll(a, s - 1, 0), 0.0)
        d = 1
        while d < s:
            dh = dh + an * _shift_up(dh, d, row)
            an = an * jnp.where(row < s - d, pltpu.roll(an, s - d, 0), 1.0)
            d *= 2
        da = dh * _shift_down(hv, 1, row)
        dxc_ref[...] = dh * i * mult
        di = dh * xv * mult
        dmult = dh * xv * i
        dlog_a = (da - dmult * a / mult) * a
        dr = dlog_a * (-LRU_C) * sp
        dlam_ref[...] = _colsum(dlog_a * LRU_C * r * _sigmoid(-lam_ref[...]))
        dpa = dr * r * (1.0 - r)
        dpi = di * i * (1.0 - i)
        dpa_ref[...] = dpa
        dpi_ref[...] = dpi
        dba_ref[...] = _colsum(dpa)
        dbi_ref[...] = _colsum(dpi)

    blk = lambda off: pl.BlockSpec((s, LANES), lambda j: (0, j + off))
    par = pl.BlockSpec((1, LANES), lambda j: (0, j))
    sc = jax.ShapeDtypeStruct((s, c), F32)
    pc = jax.ShapeDtypeStruct((1, c), F32)
    dpa, dpi, dxc, dg, dlam, dba, dbi = pl.pallas_call(
        body, name=name, grid=(nb,),
        in_specs=[blk(0), blk(nb), blk(0), blk(_off(gate_in, LANES)), par, par, par, blk(0), blk(0)],
        out_specs=[blk(0), blk(0), blk(0), blk(0), par, par, par],
        out_shape=[sc, sc, sc, sc, pc, pc, pc],
        compiler_params=_cparams(("parallel",)),
    )(pre, pre, xc, _arr(gate_in), lam, b_a, b_i, h, dy)
    return dpa, dpi, dxc, dg, dlam, dba, dbi


GROUP_HEADS = 4
SSD_GROUPS = 2


def _ssd_specs(nc, order):
    hw, gw = N_HEADS * LANES, SSD_GROUPS * LANES
    return dict(
        x=pl.BlockSpec((SSD_CHUNK, hw), lambda ci: (order(ci), 0)),
        b=pl.BlockSpec((SSD_CHUNK, gw), lambda ci: (order(ci), hw // gw)),
        c=pl.BlockSpec((SSD_CHUNK, gw), lambda ci: (order(ci), hw // gw + 1)),
        dtcol=pl.BlockSpec((N_HEADS, SSD_CHUNK, 1), lambda ci: (0, order(ci), 0)),
        dtrow=pl.BlockSpec((N_HEADS, 1, SSD_CHUNK), lambda ci: (0, 0, order(ci))),
        scal=pl.BlockSpec((N_HEADS, 1, 1), lambda ci: (0, 0, 0)),
        state=pl.BlockSpec((N_HEADS, 1, LANES, LANES), lambda ci: (0, order(ci), 0, 0)),
        group=pl.BlockSpec((SSD_CHUNK, gw), lambda ci: (order(ci), 0)),
        pacc=pl.BlockSpec((N_HEADS, 1, LANES), lambda ci: (0, 0, 0)),
    )


def _ssd_chunk_terms(dtcol, dtrow, bias, a_log):
    shp = (SSD_CHUNK, SSD_CHUNK)
    row = lax.broadcasted_iota(jnp.int32, shp, 0)
    col = lax.broadcasted_iota(jnp.int32, shp, 1)
    a_head = -jnp.exp(a_log)
    dt_c = jnp.broadcast_to(_softplus(dtcol + bias), shp)
    dt_r = jnp.broadcast_to(_softplus(dtrow + bias), shp)
    cs_c = _cumsum_rows(dt_c * a_head, row)
    cs_r = _cumsum_lanes(dt_r * a_head, col)
    cs_last = jnp.sum(jnp.where(row == SSD_CHUNK - 1, cs_c, 0.0), axis=0, keepdims=True)
    return row, col, a_head, dt_c, cs_c, cs_r, cs_last


def _ssd_fwd(xbc, dtcol, dtrow, bias, a_log, dskip, *, name):
    s = xbc.shape[0]
    nc = s // SSD_CHUNK

    def body(x_ref, b_ref, c_ref, dtc_ref, dtr_ref, bias_ref, alog_ref, d_ref, y_ref, st_ref, state):
        ci = pl.program_id(0)

        @pl.when(ci == 0)
        def _():
            state[...] = jnp.zeros_like(state)

        for gi in range(SSD_GROUPS):
            glanes = slice(gi * LANES, (gi + 1) * LANES)
            bm, cm = b_ref[:, glanes], c_ref[:, glanes]
            cb = _dot(cm, bm, tb=True)
            bm_t = bm.T
            for r in range(gi * GROUP_HEADS, (gi + 1) * GROUP_HEADS):
                lanes = slice(r * LANES, (r + 1) * LANES)
                xv = x_ref[:, lanes]
                row, col, _, dt_c, cs_c, cs_r, cs_last = _ssd_chunk_terms(dtc_ref[r], dtr_ref[r], bias_ref[r], alog_ref[r])
                g = cb * jnp.exp(jnp.where(col <= row, cs_c - cs_r, -jnp.inf))
                xdt = xv * dt_c
                st = state[r]
                st_ref[r, 0] = st
                y_ref[:, lanes] = _dot(g, xdt) + _dot(cm, st) * jnp.exp(cs_c) + xv * d_ref[r]
                state[r] = jnp.exp(cs_last) * st + _dot(bm_t, xdt * jnp.exp(cs_last - cs_c))

    sp = _ssd_specs(nc, lambda ci: ci)
    return pl.pallas_call(
        body, name=name, grid=(nc,),
        in_specs=[sp["x"], sp["b"], sp["c"], sp["dtcol"], sp["dtrow"], sp["scal"], sp["scal"], sp["scal"]],
        out_specs=[sp["x"], sp["state"]],
        out_shape=[jax.ShapeDtypeStruct((s, N_HEADS * LANES), F32),
                   jax.ShapeDtypeStruct((N_HEADS, nc, LANES, LANES), F32)],
        scratch_shapes=[pltpu.VMEM((N_HEADS, LANES, LANES), F32)],
        compiler_params=_cparams(("arbitrary",)),
    )(xbc, xbc, xbc, dtcol, dtrow, bias, a_log, dskip)


def _ssd_bwd(xbc, dtcol, dtrow, bias, a_log, dskip, states, dy, *, name):
    s = xbc.shape[0]
    nc = s // SSD_CHUNK

    def body(x_ref, b_ref, c_ref, dtc_ref, dtr_ref, bias_ref, alog_ref, d_ref, st_ref, dy_ref,
             dx_ref, db_ref, dc_ref, ddt_ref, dbias_ref, dalog_ref, dd_ref, dstate):
        ci = pl.program_id(0)

        @pl.when(ci == 0)
        def _():
            dstate[...] = jnp.zeros_like(dstate)
            dbias_ref[...] = jnp.zeros_like(dbias_ref)
            dalog_ref[...] = jnp.zeros_like(dalog_ref)
            dd_ref[...] = jnp.zeros_like(dd_ref)

        rowsum = lambda v: jnp.sum(v, axis=1, keepdims=True)
        tot = lambda v: jnp.broadcast_to(jnp.sum(v, axis=0, keepdims=True), (1, LANES))
        for gi in range(SSD_GROUPS):
            glanes = slice(gi * LANES, (gi + 1) * LANES)
            bm, cm = b_ref[:, glanes], c_ref[:, glanes]
            cb = _dot(cm, bm, tb=True)
            cb_t = _dot(bm, cm, tb=True)
            cm_t = cm.T
            dbm_sum, dcm_sum = None, None
            for r in range(gi * GROUP_HEADS, (gi + 1) * GROUP_HEADS):
                lanes = slice(r * LANES, (r + 1) * LANES)
                xv, dyv, st = x_ref[:, lanes], dy_ref[:, lanes], st_ref[r, 0]
                dtraw_c, bias = dtc_ref[r], bias_ref[r]
                row, col, a_head, dt_c, cs_c, cs_r, cs_last = _ssd_chunk_terms(dtraw_c, dtr_ref[r], bias, alog_ref[r])
                lmat = jnp.exp(jnp.where(col <= row, cs_c - cs_r, -jnp.inf))
                lmat_t = jnp.exp(jnp.where(row <= col, cs_r - cs_c, -jnp.inf))
                g, g_t = cb * lmat, cb_t * lmat_t
                xdt = xv * dt_c
                e_c = jnp.exp(cs_c)
                f_c = jnp.exp(cs_last - cs_c)
                e_last = jnp.exp(cs_last)
                w = xdt * f_c
                dst = dstate[r]

                dg = _dot(dyv, xdt, tb=True)
                dg_t = _dot(xdt, dyv, tb=True)
                dxdt = _dot(g_t, dyv)
                dcs = rowsum(dg * g) - rowsum(dg_t * g_t)
                dcm = _dot(dg * lmat, bm)
                dbm = _dot(dg_t * lmat_t, cm)
                z = _dot(cm, st)
                dz = dyv * e_c
                dcs = dcs + rowsum(dz * z)
                dcm = dcm + _dot(dz, st, tb=True)
                dstate[r] = _dot(cm_t, dz) + e_last * dst
                dcs_last = jnp.sum(rowsum(dst * st), axis=0, keepdims=True) * jnp.max(e_last, axis=1, keepdims=True)
                dbm = dbm + _dot(w, dst, tb=True)
                dw = _dot(bm, dst)
                dxdt = dxdt + dw * f_c
                q = rowsum(dw * w)
                dcs = dcs - q
                dcs_last = dcs_last + jnp.sum(q, axis=0, keepdims=True)
                dx_ref[:, lanes] = dxdt * dt_c + dyv * d_ref[r]
                ddt = rowsum(dxdt * xv)
                dcs_full = jnp.broadcast_to(dcs, (SSD_CHUNK, SSD_CHUNK)) + jnp.where(row == SSD_CHUNK - 1, dcs_last, 0.0)
                da = jnp.max(_rev_cumsum_rows(dcs_full, row), axis=1, keepdims=True)
                dt_col = jnp.max(dt_c, axis=1, keepdims=True)
                draw = (ddt + da * a_head) * _sigmoid(dtraw_c + bias)
                ddt_ref[r] = draw
                dbias_ref[r] += tot(draw)
                dalog_ref[r] += tot(da * dt_col) * a_head
                dd_ref[r] += tot(rowsum(dyv * xv))
                dbm_sum = dbm if dbm_sum is None else dbm_sum + dbm
                dcm_sum = dcm if dcm_sum is None else dcm_sum + dcm
            db_ref[:, glanes] = dbm_sum
            dc_ref[:, glanes] = dcm_sum

    sp = _ssd_specs(nc, lambda ci: nc - 1 - ci)
    return pl.pallas_call(
        body, name=name, grid=(nc,),
        in_specs=[sp["x"], sp["b"], sp["c"], sp["dtcol"], sp["dtrow"], sp["scal"], sp["scal"], sp["scal"],
                  sp["state"], sp["x"]],
        out_specs=[sp["x"], sp["group"], sp["group"], sp["dtcol"], sp["pacc"], sp["pacc"], sp["pacc"]],
        out_shape=[jax.ShapeDtypeStruct((s, N_HEADS * LANES), F32),
                   jax.ShapeDtypeStruct((s, 2 * LANES), F32),
                   jax.ShapeDtypeStruct((s, 2 * LANES), F32),
                   jax.ShapeDtypeStruct((N_HEADS, s, 1), F32),
                   jax.ShapeDtypeStruct((N_HEADS, 1, LANES), F32),
                   jax.ShapeDtypeStruct((N_HEADS, 1, LANES), F32),
                   jax.ShapeDtypeStruct((N_HEADS, 1, LANES), F32)],
        scratch_shapes=[pltpu.VMEM((N_HEADS, LANES, LANES), F32)],
        compiler_params=_cparams(("arbitrary",)),
    )(xbc, xbc, xbc, dtcol, dtrow, bias, a_log, dskip, states, dy)


def _att_tile(s):
    return _pick(s, (512, 256, 128))


def _tri(t, transposed=False):
    r = lax.broadcasted_iota(jnp.int32, (t, t), 0)
    c = lax.broadcasted_iota(jnp.int32, (t, t), 1)
    return (r <= c) if transposed else (c <= r)


def _rows_at(ref, blk, t):
    return ref[pl.ds(pl.multiple_of(blk * t, t), t), :]


def _flash_fwd(q, k, v, *, name):
    s = q.shape[0]
    t = _att_tile(s)
    nq = s // t

    def body(q_ref, k_ref, v_ref, o_ref, lse_ref):
        i = pl.program_id(1)
        qv = q_ref[...]

        def step(j, carry, diagonal):
            m_old, l_old, acc = carry
            sc = _dot(qv, _rows_at(k_ref, j, t), tb=True)
            if diagonal:
                sc = jnp.where(_tri(t), sc, -jnp.inf)
            m_new = jnp.maximum(m_old, jnp.max(sc, axis=1, keepdims=True))
            alpha = jnp.exp(m_old - m_new)
            p = jnp.exp(sc - m_new)
            return (m_new, alpha * l_old + jnp.sum(p, axis=1, keepdims=True),
                    alpha * acc + _dot(p, _rows_at(v_ref, j, t)))

        init = (jnp.full((t, 1), -jnp.inf, F32), jnp.zeros((t, 1), F32), jnp.zeros((t, LANES), F32))
        carry = lax.fori_loop(0, i, lambda j, c: step(j, c, False), init)
        m_fin, l_fin, acc = step(i, carry, True)
        o_ref[...] = (acc / l_fin).astype(o_ref.dtype)
        lse_ref[0] = m_fin + jnp.log(l_fin)

    q_spec = pl.BlockSpec((t, LANES), lambda h, i: (i, h))
    kv_spec = pl.BlockSpec((s, LANES), lambda h, i: (0, h))
    return pl.pallas_call(
        body, name=name, grid=(N_HEADS, nq),
        in_specs=[q_spec, kv_spec, kv_spec],
        out_specs=[q_spec, pl.BlockSpec((1, t, 1), lambda h, i: (h, i, 0))],
        out_shape=[jax.ShapeDtypeStruct(q.shape, BF16), jax.ShapeDtypeStruct((N_HEADS, s, 1), F32)],
        compiler_params=_cparams(("parallel", "arbitrary")),
    )(q, k, v)


def _att_delta(o, do, *, name):
    s = o.shape[0]

    def body(o_ref, do_ref, dl_ref):
        col = jnp.sum(do_ref[...].astype(F32) * o_ref[...].astype(F32), axis=1, keepdims=True)
        dl_ref[0] = jnp.broadcast_to(col, (s, LANES)).T[:1, :]

    blk = pl.BlockSpec((s, LANES), lambda h: (0, h))
    return pl.pallas_call(
        body, name=name, grid=(N_HEADS,), in_specs=[blk, blk],
        out_specs=pl.BlockSpec((1, 1, s), lambda h: (h, 0, 0)),
        out_shape=jax.ShapeDtypeStruct((N_HEADS, 1, s), F32),
        compiler_params=_cparams(("parallel",)),
    )(o, do)


def _flash_bwd(q, k, v, do, lse_row, delta_row, *, name):
    s = q.shape[0]
    t = _att_tile(s)
    nq = s // t

    def body(q_ref, k_ref, v_ref, do_ref, lse_ref, dl_ref, dq_ref, dk_ref, dv_ref):
        j = pl.program_id(1)
        kv, vv = k_ref[...], v_ref[...]

        @pl.when(j == 0)
        def _():
            dq_ref[...] = jnp.zeros_like(dq_ref)

        def step(i, carry, diagonal):
            dk, dv = carry
            rows = pl.ds(pl.multiple_of(i * t, t), t)
            qi, doi = q_ref[rows, :], do_ref[rows, :]
            p_t = jnp.exp(_dot(kv, qi, tb=True) - lse_ref[0, :, rows])
            if diagonal:
                p_t = jnp.where(_tri(t, transposed=True), p_t, 0.0)
            ds_t = (p_t * (_dot(vv, doi, tb=True) - dl_ref[0, :, rows])).astype(MXU_DTYPE)
            dq_ref[rows, :] += _dot(ds_t, kv, ta=True)
            return dk + _dot(ds_t, qi), dv + _dot(p_t, doi)

        zero = jnp.zeros((t, LANES), F32)
        carry = step(j, (zero, zero), True)
        dk, dv = lax.fori_loop(j + 1, nq, lambda i, c: step(i, c, False), carry)
        dk_ref[...] = dk
        dv_ref[...] = dv

        @pl.when(j == nq - 1)
        def _():
            dq_ref[...] = dq_ref[...] * ATT_SCALE

    q_spec = pl.BlockSpec((s, LANES), lambda h, j: (0, h))
    kv_spec = pl.BlockSpec((t, LANES), lambda h, j: (j, h))
    row_spec = pl.BlockSpec((1, 1, s), lambda h, j: (h, 0, 0))
    return pl.pallas_call(
        body, name=name, grid=(N_HEADS, nq),
        in_specs=[q_spec, kv_spec, kv_spec, q_spec, row_spec, row_spec],
        out_specs=[q_spec, kv_spec, kv_spec],
        out_shape=[jax.ShapeDtypeStruct(q.shape, F32)] * 3,
        compiler_params=_cparams(("parallel", "arbitrary")),
    )(q, k, v, do, lse_row, delta_row)


def _rope(v, cos_t, sin_p, sin_m):
    return v * cos_t + pltpu.roll(v, QK_ROPE // 2, 1) * sin_p + pltpu.roll(v, LANES - QK_ROPE // 2, 1) * sin_m


def _rope_t(d, cos_t, sin_p, sin_m):
    return d * cos_t + pltpu.roll(d * sin_p, LANES - QK_ROPE // 2, 1) + pltpu.roll(d * sin_m, QK_ROPE // 2, 1)


def _att_prep(q_pad, kv2, kr, cos_t, sin_p, sin_m, *, name):
    w = N_HEADS * LANES

    def fn(qv, kvv, krv, c, sp, sm):
        kr_rot = _rope(krv, c, sp, sm)
        qs, ks = [], []
        for h in range(N_HEADS):
            blk = slice(h * LANES, (h + 1) * LANES)
            qs.append(_rope(qv[:, blk], c, sp, sm) * ATT_SCALE)
            ks.append(kvv[:, blk] + kr_rot)
        return jnp.concatenate(qs, axis=1), jnp.concatenate(ks, axis=1), kvv[:, w:]

    return _rowwise(fn, [q_pad, kv2, kr, cos_t, sin_p, sin_m], [],
                    [(w, BF16, "row"), (w, BF16, "row"), (w, BF16, "row")], name=name)


def _att_prep_bwd(dq, dk, cos_t, sin_p, sin_m, *, name):
    w = N_HEADS * LANES

    def fn(dqv, dkv, c, sp, sm):
        outs, dkr = [], None
        for h in range(N_HEADS):
            blk = slice(h * LANES, (h + 1) * LANES)
            outs.append(_rope_t(dqv[:, blk], c, sp, sm))
            dkr = dkv[:, blk] if dkr is None else dkr + dkv[:, blk]
        return jnp.concatenate(outs, axis=1), _rope_t(dkr, c, sp, sm)

    return _rowwise(fn, [dq, dk, cos_t, sin_p, sin_m], [], [(w, BF16, "row"), (LANES, F32, "row")], name=name)


_ANY = pl.BlockSpec(memory_space=pl.ANY)
_MESH = pl.DeviceIdType.MESH


def _mesh_pos():
    return lax.axis_index("x"), lax.axis_index("y"), lax.axis_index("c")


def _remote(src, dst, send_sem, recv_sem, dev):
    return pltpu.make_async_remote_copy(src_ref=src, dst_ref=dst, send_sem=send_sem, recv_sem=recv_sem,
                                        device_id=dev, device_id_type=_MESH)


def _other_chips(x, y):
    chips = [(1 - x, y), (x, 1 - y), (1 - x, 1 - y)]
    return chips, [2 * cx + cy for cx, cy in chips]


def _comm_call(body, ins, out_shapes, n_sems, *, name):
    return pl.pallas_call(
        body, name=name, in_specs=[_ANY] * len(ins), out_specs=[_ANY] * len(out_shapes), out_shape=out_shapes,
        scratch_shapes=[pltpu.SemaphoreType.DMA((k,)) for k in n_sems],
    )(*ins)


def _gather_halves(shards):
    n = len(shards)
    halves = [t.shape[0] // 2 for t in shards]

    def body(*refs):
        xs, outs = refs[:n], refs[n:2 * n]
        send_sems, recv_sems = refs[2 * n:]
        x, y, c = _mesh_pos()
        k = 2 * x + y
        sibling = (x, y, 1 - c)
        chips, ks = _other_chips(x, y)
        half = lambda w, hf: pl.ds(hf * halves[w], halves[w])
        first = [_remote(xs[w].at[half(w, c)], outs[w].at[k, half(w, c)], send_sems.at[6 * w + j], recv_sems.at[6 * w + j],
                         (*chips[j], c)) for w in range(n) for j in range(3)]
        for cp in first:
            cp.start()
        passed = []
        for j in range(3):
            for w in range(n):
                land = outs[w].at[ks[j], half(w, c)]
                _remote(land, land, send_sems.at[6 * w + j], recv_sems.at[6 * w + j], sibling).wait_recv()
                passed.append(_remote(land, land, send_sems.at[6 * w + 3 + j], recv_sems.at[6 * w + 3 + j], sibling))
                passed[-1].start()
        for j in range(3):
            for w in range(n):
                land = outs[w].at[ks[j], half(w, 1 - c)]
                _remote(land, land, send_sems.at[6 * w + 3 + j], recv_sems.at[6 * w + 3 + j], sibling).wait_recv()
        for cp in first + passed:
            cp.wait_send()

    shapes = [jax.ShapeDtypeStruct((4,) + t.shape, t.dtype) for t in shards]
    return _comm_call(body, shards, shapes, (6 * n, 6 * n), name="gather_halves")


_HBM = pl.BlockSpec(memory_space=pltpu.HBM)
_SEM = pl.BlockSpec(memory_space=pltpu.SEMAPHORE)
_EFFECT = pltpu.SideEffectType.DATAFLOW_SIDE_EFFECTING


def _push_start(blocks, *, scatter, name):
    n = len(blocks)

    def body(*refs):
        xs, lands = refs[:n], refs[n:2 * n]
        send_sems, recv_sems = refs[2 * n], refs[2 * n + 1]
        token = refs[-1]
        x, y, c = _mesh_pos()
        k = 2 * x + y
        chips, ks = _other_chips(x, y)
        for w in range(n):
            for j in range(3):
                src = xs[w].at[ks[j]] if scatter else xs[w]
                _remote(src, lands[w].at[k], send_sems.at[3 * w + j], recv_sems.at[3 * w + j], (*chips[j], c)).start()
        token[...] = jnp.zeros_like(token)

    hbm = lambda shape, dtype: pltpu.with_memory_space_constraint(lax.empty(shape, dtype), pltpu.HBM)
    ins = [pltpu.with_memory_space_constraint(t, pltpu.HBM) for t in blocks]
    ins += [hbm(t.shape if scatter else (4,) + t.shape, t.dtype) for t in blocks]
    out_shape = [pltpu.SemaphoreType.DMA((3 * n,)), pltpu.SemaphoreType.DMA((3 * n,))]
    out_shape += [pltpu.HBM(t.shape, t.dtype) for t in ins]
    out_shape += [jax.ShapeDtypeStruct((8, LANES), F32)]
    res = pl.pallas_call(
        body, name=name, out_shape=out_shape, in_specs=[_HBM] * (2 * n),
        out_specs=[_SEM, _SEM] + [_HBM] * (2 * n) + [pl.BlockSpec(memory_space=pltpu.VMEM)],
        input_output_aliases={i: 2 + i for i in range(2 * n)},
        compiler_params=pltpu.CompilerParams(has_side_effects=_EFFECT),
    )(*ins)
    return res[0], res[1], res[2:2 + n], res[2 + n:2 + 2 * n], res[-1]


def _push_wait(send_sems, recv_sems, blocks, lands, after, *, name):
    n = len(blocks)

    def body(*refs):
        lands_in = refs[n:2 * n]
        send_sems, recv_sems = refs[2 * n], refs[2 * n + 1]
        x, y, c = _mesh_pos()
        chips, ks = _other_chips(x, y)
        for w in range(n):
            for j in range(3):
                slot = lands_in[w].at[ks[j]]
                cp = _remote(slot, slot, send_sems.at[3 * w + j], recv_sems.at[3 * w + j], (*chips[j], c))
                cp.wait_send()
                cp.wait_recv()

    out_shape = [pltpu.HBM(t.shape, t.dtype) for t in list(blocks) + list(lands)]
    res = pl.pallas_call(
        body, name=name, out_shape=out_shape,
        in_specs=[_HBM] * (2 * n) + [_SEM, _SEM, pl.BlockSpec(memory_space=pl.ANY)], out_specs=[_HBM] * (2 * n),
        input_output_aliases={i: i for i in range(2 * n)},
        compiler_params=pltpu.CompilerParams(has_side_effects=_EFFECT),
    )(*blocks, *lands, send_sems, recv_sems, after)
    return res[:n], res[n:]


def _send_half(views, *, name):
    n = len(views)

    def body(*refs):
        vs, outs = refs[:n], refs[n:2 * n]
        send_sems, recv_sems = refs[2 * n:]
        x, y, c = _mesh_pos()
        cps = []
        for w in range(n):
            h = views[w].shape[1] // 2
            cps.append(_remote(vs[w].at[:, pl.ds((1 - c) * h, h), :], outs[w], send_sems.at[w], recv_sems.at[w],
                               (x, y, 1 - c)))
            cps[-1].start()
        for cp in cps:
            cp.wait()

    shapes = [jax.ShapeDtypeStruct((t.shape[0], t.shape[1] // 2, t.shape[2]), t.dtype) for t in views]
    return _comm_call(body, views, shapes, (n, n), name=name)


def _swap_with_sibling(mine, *, name):
    n = len(mine)

    def body(*refs):
        hs, outs = refs[:n], refs[n:2 * n]
        send_sems, recv_sems = refs[2 * n:]
        x, y, c = _mesh_pos()
        cps = [_remote(hs[w], outs[w], send_sems.at[w], recv_sems.at[w], (x, y, 1 - c)) for w in range(n)]
        for cp in cps:
            cp.start()
        for cp in cps:
            cp.wait()

    shapes = [jax.ShapeDtypeStruct(t.shape, t.dtype) for t in mine]
    return _comm_call(body, mine, shapes, (n, n), name=name)


def _gather_all(vec, *, name):
    r, w = vec.shape

    def body(v_ref, out_ref, send_sems, recv_sems):
        x, y, c = _mesh_pos()

        def slot(px, py, pc):
            return out_ref.at[4 * px + 2 * py + pc]

        peers = []
        for rel in range(1, 8):
            fx, fy, fc = (rel >> 2) & 1, (rel >> 1) & 1, rel & 1
            peers.append((x ^ fx, y ^ fy, c ^ fc))
        cps = [_remote(v_ref, slot(x, y, c), send_sems.at[j], recv_sems.at[j], peer) for j, peer in enumerate(peers)]
        for cp in cps:
            cp.start()
        for j, peer in enumerate(peers):
            _remote(slot(*peer), slot(*peer), send_sems.at[j], recv_sems.at[j], peer).wait_recv()
        for cp in cps:
            cp.wait_send()

    others = pl.pallas_call(
        body, name=name, in_specs=[_ANY], out_specs=_ANY,
        out_shape=jax.ShapeDtypeStruct((8, r, w), vec.dtype),
        scratch_shapes=[pltpu.SemaphoreType.DMA((7,)), pltpu.SemaphoreType.DMA((7,))],
    )(vec)
    me = 4 * lax.axis_index("x") + 2 * lax.axis_index("y") + lax.axis_index("c")
    return lax.dynamic_update_index_in_dim(others, vec, me, 0)


def _gather_same_core(vec, *, name):
    r, w = vec.shape

    def body(v_ref, out_ref, send_sems, recv_sems):
        x, y, c = _mesh_pos()
        k = 2 * x + y
        chips, ks = _other_chips(x, y)
        cps = [_remote(v_ref, out_ref.at[k], send_sems.at[j], recv_sems.at[j], (*chips[j], c)) for j in range(3)]
        for cp in cps:
            cp.start()
        for j in range(3):
            slot = out_ref.at[ks[j]]
            _remote(slot, slot, send_sems.at[j], recv_sems.at[j], (*chips[j], c)).wait_recv()
        for cp in cps:
            cp.wait_send()

    others = pl.pallas_call(
        body, name=name, in_specs=[_ANY], out_specs=_ANY,
        out_shape=jax.ShapeDtypeStruct((4, r, w), vec.dtype),
        scratch_shapes=[pltpu.SemaphoreType.DMA((3,)), pltpu.SemaphoreType.DMA((3,))],
    )(vec)
    k_chip = 2 * lax.axis_index("x") + lax.axis_index("y")
    return lax.dynamic_update_index_in_dim(others, vec, k_chip, 0)


def _row_tile(rows, row_bytes):
    for tm in (1024, 512, 256, 128, 64, 32, 16):
        if rows % tm == 0 and tm * row_bytes <= ELEMENTWISE_BLOCK_BYTES:
            return tm
    return 16 if rows % 16 == 0 else rows


def _chip_sum_half(g, got, c, *, name):
    nb, r, w = g.shape
    half = r // 2
    tm = _row_tile(half, w * 4)
    per = half // tm

    def body(c_ref, g_ref, o_ref, out_ref):
        out_ref[...] = (g_ref[...] + o_ref[...]).astype(out_ref.dtype)

    return pl.pallas_call(
        body, name=name,
        grid_spec=pltpu.PrefetchScalarGridSpec(
            num_scalar_prefetch=1, grid=(nb, per),
            in_specs=[pl.BlockSpec((1, tm, w), lambda b, i, c_ref: (b, c_ref[0] * per + i, 0)),
                      pl.BlockSpec((1, tm, w), lambda b, i, c_ref: (b, i, 0))],
            out_specs=pl.BlockSpec((1, tm, w), lambda b, i, c_ref: (b, i, 0))),
        out_shape=jax.ShapeDtypeStruct((nb, half, w), BF16),
        compiler_params=_cparams(("parallel", "parallel")),
    )(jnp.reshape(c, (1,)).astype(jnp.int32), g, got)


def _sum_slots(stack, *, name):
    n, r, w = stack.shape
    tm = _row_tile(r, n * w * stack.dtype.itemsize)

    def body(s_ref, out_ref):
        acc = s_ref[0].astype(F32)
        for i in range(1, n):
            acc = acc + s_ref[i].astype(F32)
        out_ref[...] = acc

    return pl.pallas_call(
        body, name=name, grid=(r // tm,),
        in_specs=[pl.BlockSpec((n, tm, w), lambda i: (0, i, 0))],
        out_specs=pl.BlockSpec((tm, w), lambda i: (i, 0)),
        out_shape=jax.ShapeDtypeStruct((r, w), F32),
        compiler_params=_cparams(("parallel",)),
    )(stack)


def _adam_math(wv, gv, mv, vv):
    m_new = ADAM_B1 * mv + (1.0 - ADAM_B1) * gv
    v_new = ADAM_B2 * vv + (1.0 - ADAM_B2) * (gv * gv)
    m_hat = m_new / (1.0 - ADAM_B1 ** ADAM_STEP)
    v_hat = v_new / (1.0 - ADAM_B2 ** ADAM_STEP)
    delta = -ADAM_LR * (m_hat / (jnp.sqrt(v_hat) + ADAM_EPS) + ADAM_WD * wv)
    return delta, m_new, v_new


def _adamw(w, g, m, v, *, name):
    shape = w.shape
    cols = shape[-1]
    flat = lambda t: t.reshape(-1, cols)
    rows = flat(w).shape[0]
    tm = _pick(rows, (256, 128, 64, 32, 16, 8))
    outs = _rowwise(_adam_math, [flat(w), flat(g), flat(m), flat(v)], [], [(cols, F32, "row")] * 3, name=name, tm=tm)
    return tuple(o.reshape(shape) for o in outs)


def _adamw_slots(w, slots, m, v, c, *, name):
    shape = w.shape
    cols = shape[-1]
    half = slots[0][0].shape[1]
    v4 = lambda t: t.reshape(2, 2, half, cols)
    assert all(s.shape == (4, half, cols) for pair in slots for s in pair) and w.size == 4 * half * cols
    tm = _row_tile(half, cols * 4 * 4)

    def body(c_ref, w_ref, m0_ref, o0_ref, m1_ref, o1_ref, m_ref, v_ref, g_ref, d_ref, mo_ref, vo_ref):
        first = pl.program_id(0) == 0
        own = pl.program_id(1) == c_ref[0]
        g = None
        for i in range(4):
            part = jnp.where(first, jnp.where(own, m0_ref[i], o0_ref[i]), jnp.where(own, m1_ref[i], o1_ref[i]))
            g = part.astype(F32) if g is None else g + part.astype(F32)
        delta, m_new, v_new = _adam_math(w_ref[0, 0], g, m_ref[0, 0], v_ref[0, 0])
        g_ref[0, 0], d_ref[0, 0], mo_ref[0, 0], vo_ref[0, 0] = g, delta, m_new, v_new

    blk = pl.BlockSpec((1, 1, tm, cols), lambda l, hf, i, c_ref: (l, hf, i, 0))

    def slot_spec(layer, mine):
        def index(l, hf, i, c_ref):
            same_half = hf * c_ref[0] + (1 - hf) * (1 - c_ref[0])
            use = (l if layer else 1 - l) * (same_half if mine else 1 - same_half)
            return (0, i * use, 0)
        return pl.BlockSpec((4, tm, cols), index)

    outs = pl.pallas_call(
        body, name=name,
        grid_spec=pltpu.PrefetchScalarGridSpec(
            num_scalar_prefetch=1, grid=(2, 2, half // tm),
            in_specs=[blk, slot_spec(0, True), slot_spec(0, False), slot_spec(1, True), slot_spec(1, False), blk, blk],
            out_specs=[blk] * 4),
        out_shape=[jax.ShapeDtypeStruct((2, 2, half, cols), F32)] * 4,
        compiler_params=_cparams(("arbitrary", "arbitrary", "arbitrary")),
    )(jnp.reshape(c, (1,)).astype(jnp.int32), v4(w), slots[0][0], slots[0][1], slots[1][0], slots[1][1], v4(m), v4(v))
    return tuple(o.reshape(shape) for o in outs)


def _pad_blocks(w, axis, n_blocks, real, to=LANES, offset=0):
    axis = axis % w.ndim
    shp = w.shape
    w = w.reshape(shp[:axis] + (n_blocks, real) + shp[axis + 1:])
    pads = [(0, 0)] * w.ndim
    pads[axis + 1] = (offset, to - real - offset)
    w = jnp.pad(w, pads)
    return w.reshape(shp[:axis] + (n_blocks * to,) + shp[axis + 1:])


def _unpad_blocks(w, axis, n_blocks, real, to=LANES, offset=0):
    axis = axis % w.ndim
    shp = w.shape
    w = w.reshape(shp[:axis] + (n_blocks, to) + shp[axis + 1:])
    w = lax.slice_in_dim(w, offset, offset + real, axis=axis + 1)
    return w.reshape(shp[:axis] + (n_blocks * real,) + shp[axis + 1:])


def _block_diag(w):
    n, a, b = w.shape
    eye = jnp.eye(n, dtype=w.dtype)
    return (eye[:, None, :, None] * w[:, :, None, :]).reshape(n * a, n * b)


def _block_diag_t(d, n):
    a, b = d.shape[0] // n, d.shape[1] // n
    d = d.reshape(n, a, n, b)
    return jnp.stack([d[i, :, i, :] for i in range(n)])


_SPLITS = np.cumsum((0,) + SPLIT_SIZES)


def _w_in_groups(w_in):
    sl = lambda i: w_in[:, _SPLITS[i]:_SPLITS[i + 1]]
    xbc = sl(5)
    xbc_pad = jnp.concatenate([_pad_blocks(xbc[:, :MIX], 1, N_HEADS, HEAD),
                               _pad_blocks(xbc[:, MIX:MIX + 2 * HEAD], 1, 2, HEAD),
                               _pad_blocks(xbc[:, MIX + 2 * HEAD:], 1, 2, HEAD)], axis=1)
    return dict(
        cq=sl(0), ckv=sl(1), kr=_pad_blocks(sl(2), 1, 1, QK_ROPE, offset=HEAD), pool=sl(3),
        z=_pad_blocks(sl(4), 1, N_HEADS, HEAD), xbc=xbc_pad, dt=_pad_blocks(sl(6), 1, 1, N_HEADS),
        lru_g=sl(7), lru_x=sl(8), gates=sl(9))


def _w_in_fused(groups):
    parts, at = [], 0
    for name, off, width in IN_LAYOUT:
        assert groups[name].shape[1] == width and off >= at
        if off > at:
            parts.append(jnp.zeros((groups[name].shape[0], off - at), groups[name].dtype))
        parts.append(groups[name])
        at = off + width
    parts.append(jnp.zeros((parts[0].shape[0], IN_ALL_COLS - at), parts[0].dtype))
    return jnp.concatenate(parts, axis=1)


def _in_cols(arr, name):
    off, width = IN_OFFSETS[name]
    return _Cols(arr, off, width)


def _w_in_ungroup(d):
    xbc = d["xbc"]
    w = N_HEADS * LANES
    xbc_real = jnp.concatenate([_unpad_blocks(xbc[:, :w], 1, N_HEADS, HEAD),
                                _unpad_blocks(xbc[:, w:w + 2 * LANES], 1, 2, HEAD),
                                _unpad_blocks(xbc[:, w + 2 * LANES:], 1, 2, HEAD)], axis=1)
    return jnp.concatenate([d["cq"], d["ckv"], _unpad_blocks(d["kr"], 1, 1, QK_ROPE, offset=HEAD), d["pool"],
                            _unpad_blocks(d["z"], 1, N_HEADS, HEAD), xbc_real, _unpad_blocks(d["dt"], 1, 1, N_HEADS),
                            d["lru_g"], d["lru_x"], d["gates"]], axis=1)


def _pad_xbc_vec(v):
    return jnp.concatenate([_pad_blocks(v[..., :MIX], -1, N_HEADS, HEAD),
                            _pad_blocks(v[..., MIX:MIX + 2 * HEAD], -1, 2, HEAD),
                            _pad_blocks(v[..., MIX + 2 * HEAD:], -1, 2, HEAD)], axis=-1)


def _unpad_xbc_vec(v):
    w = N_HEADS * LANES
    return jnp.concatenate([_unpad_blocks(v[..., :w], -1, N_HEADS, HEAD),
                            _unpad_blocks(v[..., w:w + 2 * LANES], -1, 2, HEAD),
                            _unpad_blocks(v[..., w + 2 * LANES:], -1, 2, HEAD)], axis=-1)


def _layer_weights(p):
    q = dict(p)
    q["in_all"] = _w_in_fused(_w_in_groups(p["w_in"]))
    q["uq"] = _pad_blocks(p["w_uq"], 1, N_HEADS, HEAD + QK_ROPE)
    ukv = p["w_ukv"].reshape(KV_LORA, N_HEADS, 2 * HEAD)
    q["ukv"] = jnp.concatenate([_pad_blocks(ukv[:, :, :HEAD].reshape(KV_LORA, -1), 1, N_HEADS, HEAD),
                                _pad_blocks(ukv[:, :, HEAD:].reshape(KV_LORA, -1), 1, N_HEADS, HEAD)], axis=1)
    q["pool_bd"] = _block_diag(p["w_pool"])
    q["lru_bd"] = jnp.concatenate([_block_diag(p["lru_w_a"]), _block_diag(p["lru_w_i"])], axis=1)
    q["br"] = [_pad_blocks(p["w_branch"][0], 0, N_HEADS, HEAD), p["w_branch"][1],
               _pad_blocks(p["w_branch"][2], 0, N_HEADS, HEAD), p["w_branch"][3]]
    q["ssd_conv_w_pad"] = _pad_xbc_vec(p["ssd_conv_w"])
    q["ssd_conv_b_pad"] = _pad_xbc_vec(p["ssd_conv_b"])[None, :]
    q["ssd_norm_pad"] = _pad_blocks(p["ssd_norm"], 0, N_HEADS, HEAD)[None, :]
    return q


def _row(v):
    return v.reshape(1, -1)


def _scal3(v):
    return v.reshape(N_HEADS, 1, 1)


def _layer_fwd(x, p_emb, w, rope, tag):
    n = lambda s: f"{s}_{tag}"
    sv = {"x": x}
    h = _rms_fwd(x, _row(w["g_mix"]), name=n("rms_mix"))
    sv["h"] = h
    u_all = _mm(h, w["in_all"], name=n("in_proj"))
    u = {k: _in_cols(u_all, k) for k in IN_OFFSETS}
    sv["u"] = u

    cqn = _rms_fwd(u["cq"], _row(w["q_norm"]), name=n("rms_q"))
    ckvn = _rms_fwd(u["ckv"], _row(w["kv_norm"]), name=n("rms_kv"))
    q_pad = _mm(cqn, w["uq"], name=n("uq"))
    kv2 = _mm(ckvn, w["ukv"], name=n("ukv"))
    qc, kc, vc = _att_prep(q_pad, kv2, u["kr"], *rope, name=n("att_prep"))
    y_a, lse = _flash_fwd(qc, kc, vc, name=n("flash_fwd"))
    sv.update(cqn=cqn, ckvn=ckvn, qc=qc, kc=kc, vc=vc, y_a=y_a, lse=lse)

    pool_d = _pool_fwd(u["pool"], name=n("pool_fwd"))
    yb_pre, y_b = _mm(pool_d, w["pool_bd"], epilogue=lambda acc, sc: (acc, acc * sc),
                      rowvecs=[_row(w["pool_scale"])], out_dtypes=(F32, BF16), name=n("pool_mm"))
    sv.update(pool_d=pool_d, yb_pre=yb_pre, y_b=y_b)

    xbc_c = _conv_fwd(u["xbc"], w["ssd_conv_w_pad"], w["ssd_conv_b_pad"], silu=True, name=n("ssd_conv"))
    dt8 = lax.slice_in_dim(u_all, IN_OFFSETS["dt"][0], IN_OFFSETS["dt"][0] + N_HEADS, axis=1)
    dtcol = dt8.T[:, :, None]
    dtrow = dt8.T[:, None, :]
    ssd_par = (_scal3(w["ssd_dt_bias"]), _scal3(w["ssd_a_log"]), _scal3(w["ssd_d"]))
    y_ssd, states = _ssd_fwd(xbc_c, dtcol, dtrow, *ssd_par, name=n("ssd_fwd"))

    def ssd_post(yv, zv, gv):
        xh, _ = _rms_parts(yv * _silu(zv), MIX)
        return xh * gv

    y_c = _rowwise(ssd_post, [y_ssd, u["z"]], [w["ssd_norm_pad"]], [(N_HEADS * LANES, BF16, "row")], name=n("ssd_post"))
    sv.update(xbc_c=xbc_c, dtcol=dtcol, dtrow=dtrow, y_ssd=y_ssd, states=states, y_c=y_c)

    xc = _conv_fwd(u["lru_x"], w["lru_conv_w"], _row(w["lru_conv_b"]), silu=False, name=n("lru_conv"))
    pre = _mm(xc, w["lru_bd"], name=n("lru_mm"))
    lru_par = (_row(w["lru_lambda"]), _row(w["lru_b_a"]), _row(w["lru_b_i"]))
    y_d, h_lru = _lru_fwd(pre, xc, u["lru_g"], *lru_par, name=n("lru_fwd"))
    sv.update(xc=xc, pre=pre, h_lru=h_lru, y_d=y_d)

    merged, *ybs = _branch_merge([y_a, y_b, y_c, y_d], w["br"], u_all, name=n("branch_merge"))
    x1 = _mm(merged, w["w_out"], epilogue=lambda acc, xr: (acc + xr,), tiles=[x], name=n("out_proj"))
    sv.update(ybs=ybs, merged=merged, x1=x1)

    h2 = _rms_fwd(x1, _row(w["g_mlp"]), name=n("rms_mlp"))
    a_ff, f_ff = _mm(h2, w["w_ff1"], epilogue=lambda acc: (acc, jnp.square(jnp.maximum(acc, 0.0))),
                     out_dtypes=(BF16, BF16), name=n("ff1"))
    x2 = _mm(f_ff, w["w_ff2"], epilogue=lambda acc, xr: (acc + xr,), tiles=[x1], name=n("ff2"))
    sv.update(h2=h2, a_ff=a_ff, f_ff=f_ff, x2=x2)

    h3 = _rms_fwd(x2, _row(w["g_ple"]), name=n("rms_ple"))
    e_ple = _mm(p_emb, w["w_ple"], name=n("ple_emb"))
    x3, gt_ple = _mm(h3, w["w_ple_gate"], epilogue=lambda acc, ev, xr: (xr + ev * _sigmoid(acc), _sigmoid(acc)),
                     tiles=[e_ple, x2], out_dtypes=(F32, F32), name=n("ple_gate"))
    sv.update(h3=h3, e_ple=e_ple, gt_ple=gt_ple, p_emb=p_emb)
    return x3, sv


def _layer_bwd(dx3, sv, w, rope, tag, early=None, mid=None):
    n = lambda s: f"{s}_{tag}"
    gr = {}
    u = sv["u"]

    de, dpre = _rowwise(lambda d, gt, ev: (d * gt, d * ev * gt * (1.0 - gt)), [dx3, sv["gt_ple"], sv["e_ple"]], [],
                        [(D_MODEL, BF16, "row"), (D_MODEL, BF16, "row")], name=n("ple_bwd"))
    gr["w_ple"] = _mm(sv["p_emb"], de, ta=True, name=n("d_w_ple"))
    gr["w_ple_gate"] = _mm(sv["h3"], dpre, ta=True, name=n("d_w_ple_gate"))
    dh3 = _mm(dpre, w["w_ple_gate"], tb=True, out_dtypes=(BF16,), name=n("d_h3"))
    dx2, dg = _rms_bwd(sv["x2"], _row(w["g_ple"]), dh3, dx3, name=n("rms_ple_bwd"))
    gr["g_ple"] = dg[0]

    gr["w_ff2"] = _mm(sv["f_ff"], dx2, ta=True, name=n("d_w_ff2"))
    da = _mm(dx2, w["w_ff2"], tb=True, epilogue=lambda acc, av: (acc * 2.0 * jnp.maximum(av, 0.0),),
             tiles=[sv["a_ff"]], out_dtypes=(BF16,), name=n("d_a_ff"))
    gr["w_ff1"] = _mm(sv["h2"], da, ta=True, name=n("d_w_ff1"))
    dh2 = _mm(da, w["w_ff1"], tb=True, out_dtypes=(BF16,), name=n("d_h2"))
    dx1, dg = _rms_bwd(sv["x1"], _row(w["g_mlp"]), dh2, dx2, name=n("rms_mlp_bwd"))
    gr["g_mlp"] = dg[0]
    if early is not None:
        dx1 = early(gr, dx1)

    gr["w_out"] = _mm(sv["merged"], dx1, ta=True, name=n("d_w_out"))
    dmerged = _mm(dx1, w["w_out"], tb=True, name=n("d_merged"))

    def merge_bwd(dm, gts, y0, y1, y2, y3):
        dys, dgs = [], []
        for b, yb in enumerate((y0, y1, y2, y3)):
            sg = _sigmoid(gts[:, b * D_MODEL:(b + 1) * D_MODEL])
            dys.append(dm * sg)
            dgs.append(dm * yb * sg * (1.0 - sg))
        return (*dys, jnp.concatenate(dgs, axis=1))

    *dybs, dgates = _rowwise(merge_bwd, [dmerged, u["gates"]] + sv["ybs"], [],
                             [(D_MODEL, BF16, "row")] * 4 + [(4 * D_MODEL, BF16, "row")], name=n("merge_bwd"))
    ys = [sv["y_a"], sv["y_b"], sv["y_c"], sv["y_d"]]
    dwb = [_mm(ys[b], dybs[b], ta=True, name=n(f"d_w_branch{b}")) for b in range(4)]
    gr["w_branch"] = jnp.stack([_unpad_blocks(dwb[0], 0, N_HEADS, HEAD), dwb[1],
                                _unpad_blocks(dwb[2], 0, N_HEADS, HEAD), dwb[3]])
    dy_a = _mm(dybs[0], w["br"][0], tb=True, out_dtypes=(BF16,), name=n("d_y_a"))
    dy_b = _mm(dybs[1], w["br"][1], tb=True, name=n("d_y_b"))
    dy_c = _mm(dybs[2], w["br"][2], tb=True, name=n("d_y_c"))
    dy_d = _mm(dybs[3], w["br"][3], tb=True, name=n("d_y_d"))
    du = {"gates": dgates}

    lru_par = (_row(w["lru_lambda"]), _row(w["lru_b_a"]), _row(w["lru_b_i"]))
    dpa, dpi, dxc_direct, du["lru_g"], dlam, dba, dbi = _lru_bwd(
        sv["pre"], sv["xc"], u["lru_g"], *lru_par, sv["h_lru"], dy_d, name=n("lru_bwd"))
    dpre_lru = jnp.concatenate([dpa, dpi], axis=1)
    d_bd = _mm(sv["xc"], dpre_lru, ta=True, name=n("d_lru_w"))
    gr["lru_w_a"] = _block_diag_t(d_bd[:, :MIX], N_HEADS)
    gr["lru_w_i"] = _block_diag_t(d_bd[:, MIX:], N_HEADS)
    gr["lru_lambda"], gr["lru_b_a"], gr["lru_b_i"] = dlam[0], dba[0], dbi[0]
    dxc = _mm(dpre_lru, w["lru_bd"], tb=True, epilogue=lambda acc, t: (acc + t,), tiles=[dxc_direct], name=n("d_xc"))
    du["lru_x"], gr["lru_conv_w"], dcb = _conv_bwd(u["lru_x"], w["lru_conv_w"], _row(w["lru_conv_b"]), dxc,
                                                  silu=False, name=n("lru_conv_bwd"))
    gr["lru_conv_b"] = dcb[0]

    def ssd_post_bwd(dyc, yv, zv, gv):
        sz = _silu(zv)
        dyz, dgain = _rms_bwd_math(yv * sz, gv, dyc, MIX)
        return dyz * sz, dyz * yv * _silu_grad(zv), dgain

    dy_ssd, du["z"], dgain = _rowwise(ssd_post_bwd, [dy_c, sv["y_ssd"], u["z"]], [w["ssd_norm_pad"]],
                                      [(N_HEADS * LANES, F32, "row"), (N_HEADS * LANES, BF16, "row"),
                                       (N_HEADS * LANES, F32, "acc")], name=n("ssd_post_bwd"))
    gr["ssd_norm"] = _unpad_blocks(dgain[0], 0, N_HEADS, HEAD)
    ssd_par = (_scal3(w["ssd_dt_bias"]), _scal3(w["ssd_a_log"]), _scal3(w["ssd_d"]))
    dxs, dbg, dcg, ddt, dbias, dalog, dd = _ssd_bwd(sv["xbc_c"], sv["dtcol"], sv["dtrow"], *ssd_par, sv["states"],
                                                    dy_ssd, name=n("ssd_bwd"))
    s = dxs.shape[0]
    dxbc_c = jnp.concatenate([dxs, dbg, dcg], axis=1)
    gr["ssd_dt_bias"], gr["ssd_a_log"], gr["ssd_d"] = dbias[:, 0, 0], dalog[:, 0, 0], dd[:, 0, 0]
    du["xbc"], dcw, dcb = _conv_bwd(u["xbc"], w["ssd_conv_w_pad"], w["ssd_conv_b_pad"], dxbc_c, silu=True,
                                    name=n("ssd_conv_bwd"))
    gr["ssd_conv_w"], gr["ssd_conv_b"] = _unpad_xbc_vec(dcw), _unpad_xbc_vec(dcb[0])
    du["dt"] = jnp.pad(ddt[:, :, 0].T, ((0, 0), (0, LANES - N_HEADS)))

    dyb_pre, dscale = _rowwise(lambda d, yp, sc: (d * sc, _colsum(d * yp)), [dy_b, sv["yb_pre"]],
                               [_row(w["pool_scale"])], [(MIX, BF16, "row"), (MIX, F32, "acc")], name=n("pool_scale_bwd"))
    gr["pool_scale"] = dscale[0]
    gr["w_pool"] = _block_diag_t(_mm(sv["pool_d"], dyb_pre, ta=True, name=n("d_w_pool")), 4)
    dd_pool = _mm(dyb_pre, w["pool_bd"], tb=True, name=n("d_pool_d"))
    du["pool"] = _pool_bwd(dd_pool, name=n("pool_bwd"))

    delta = _att_delta(sv["y_a"], dy_a, name=n("att_delta"))
    to_row = lambda t: t.reshape(N_HEADS, 1, s)
    dqc, dkc, dvc = _flash_bwd(sv["qc"], sv["kc"], sv["vc"], dy_a, to_row(sv["lse"]), delta, name=n("flash_bwd"))
    dq_pad, du["kr"] = _att_prep_bwd(dqc, dkc, *rope, name=n("att_prep_bwd"))
    d_uq = _mm(sv["cqn"], dq_pad, ta=True, name=n("d_w_uq"))
    gr["w_uq"] = _unpad_blocks(d_uq, 1, N_HEADS, HEAD + QK_ROPE)
    dcqn = _mm(dq_pad, w["uq"], tb=True, out_dtypes=(BF16,), name=n("d_cqn"))
    du["cq"], dg = _rms_bwd(u["cq"], _row(w["q_norm"]), dcqn, name=n("rms_q_bwd"))
    gr["q_norm"] = dg[0]
    dkv2 = jnp.concatenate([dkc, dvc], axis=1).astype(BF16)
    d_ukv = _mm(sv["ckvn"], dkv2, ta=True, name=n("d_w_ukv"))
    wk = N_HEADS * LANES
    dk_real = _unpad_blocks(d_ukv[:, :wk], 1, N_HEADS, HEAD).reshape(KV_LORA, N_HEADS, HEAD)
    dv_real = _unpad_blocks(d_ukv[:, wk:], 1, N_HEADS, HEAD).reshape(KV_LORA, N_HEADS, HEAD)
    gr["w_ukv"] = jnp.concatenate([dk_real, dv_real], axis=2).reshape(KV_LORA, N_HEADS * 2 * HEAD)
    dckvn = _mm(dkv2, w["ukv"], tb=True, out_dtypes=(BF16,), name=n("d_ckvn"))
    du["ckv"], dg = _rms_bwd(u["ckv"], _row(w["kv_norm"]), dckvn, name=n("rms_kv_bwd"))
    gr["kv_norm"] = dg[0]

    if mid is not None:
        du["dt"] = du["dt"] + mid(gr)
    du_all = _w_in_fused({k: v.astype(BF16) for k, v in du.items()})
    dw_all = _mm(sv["h"], du_all, ta=True, name=n("d_w_in"))
    gr["w_in"] = _w_in_ungroup({k: dw_all[:, off:off + width] for k, off, width in IN_LAYOUT})
    dh = _mm(du_all, w["in_all"], tb=True, name=n("d_h"))
    dx, dg = _rms_bwd(sv["x"], _row(w["g_mix"]), dh, dx1, name=n("rms_mix_bwd"))
    gr["g_mix"] = dg[0]
    return dx, gr


def _pack_rows(n_elems):
    per = PACK_W * PACK_ROWS
    return -(-n_elems // per) * PACK_ROWS


def _pack_flat(parts, dtype):
    flat = jnp.concatenate([p.reshape(-1).astype(dtype) for p in parts])
    rows = _pack_rows(flat.shape[0])
    return jnp.pad(flat, (0, rows * PACK_W - flat.shape[0])).reshape(rows, PACK_W)


def _unpack_flat(buf, shapes):
    lead = buf.shape[:-2]
    flat = buf.reshape(lead + (-1,))
    out, off = [], 0
    for shp in shapes:
        size = int(np.prod(shp))
        out.append(flat[..., off:off + size].reshape(lead + tuple(shp)))
        off += size
    return out


def _merge_shards(t, axis):
    return jnp.concatenate([t[i] for i in range(4)], axis=axis)


def _split_shards(t, axis):
    return jnp.stack(jnp.split(t, 4, axis=axis))


def _rope_tables(positions):
    inv = 1.0 / (ROPE_THETA ** (jnp.arange(0, QK_ROPE, 2, dtype=F32) / QK_ROPE))
    ang = positions.astype(F32)[:, None] * inv
    cos, sin = jnp.cos(ang), jnp.sin(ang)
    s = ang.shape[0]
    half = QK_ROPE // 2
    z = lambda n_: jnp.zeros((s, n_), F32)
    cos_t = jnp.concatenate([jnp.ones((s, HEAD), F32), cos, cos, jnp.ones((s, LANES - HEAD - QK_ROPE), F32)], axis=1)
    sin_p = jnp.concatenate([z(HEAD + half), sin, z(LANES - HEAD - QK_ROPE)], axis=1)
    sin_m = jnp.concatenate([z(HEAD), -sin, z(half + LANES - HEAD - QK_ROPE)], axis=1)
    return cos_t, sin_p, sin_m


def _loss_head(x, g, target, *, name):
    d = x.shape[1]

    def fn(xv, tv, gv):
        xh, r = _rms_parts(xv, d)
        y = xh * gv
        err = y - tv
        dy = err * (1.0 / d)
        dxh = dy * gv
        dx = r * (dxh - xh * (jnp.sum(dxh * xh, axis=-1, keepdims=True) * (1.0 / d)))
        return dx, _colsum(dy * xh), _colsum(err * err) * (0.5 / d)

    return _rowwise(fn, [x, target], [g], [(d, F32, "row"), (d, F32, "acc"), (d, F32, "acc")], name=name)


MATS = tuple((nm, ax) for nm, ax in BIG if nm not in CONV_SHARDED)
EARLY_L0 = ("w_ple", "w_ple_gate", "w_ff2", "w_ff1")


def _grad_view(g, ax_layer):
    if ax_layer == 0:
        return g.reshape(4, g.shape[0] // 4, g.shape[1])
    return g.reshape(1, -1, g.shape[-1])


def _reduce_start(grads_l, mats, c_idx, tag):
    views = [_grad_view(grads_l[nm], ax - 1) for nm, ax in mats]
    got = _send_half(views, name="send_half_" + tag)
    parts = []
    for (nm, ax), v, gt in zip(mats, views, got):
        both = _chip_sum_half(v, gt, c_idx, name=f"chip_sum_{nm}_{tag}")
        parts.append(both if ax == 1 else _split_shards(both[0], 1))
    return _push_start(parts, scatter=True, name="push_grads_" + tag)


def _reduce_finish(state, after, k_chip, tag):
    send_sems, recv_sems, parts, lands, _ = state
    parts, landed = _push_wait(send_sems, recv_sems, parts, lands, after, name="wait_grads_" + tag)
    mine = [lax.dynamic_update_index_in_dim(t, lax.dynamic_index_in_dim(p, k_chip, 0, keepdims=False), k_chip, 0)
            for t, p in zip(landed, parts)]
    return list(zip(mine, _swap_with_sibling(mine, name="swap_halves_" + tag)))


def _step(args):
    x = args["x"][0]
    c_idx = lax.axis_index("c")
    k_chip = 2 * lax.axis_index("x") + lax.axis_index("y")

    mats = MATS
    mine = [[args[nm][l].astype(BF16) for nm, _ in mats] for l in range(2)]
    gathered0 = _gather_halves(mine[0])
    convs = [(nm, ax) for nm, ax in BIG if nm in CONV_SHARDED]
    conv_all = _gather_all(_pack_flat([args[nm] for nm, _ in convs], F32), name="gather_conv_taps")[0::2]
    mine1, gathered0, conv_all = lax.optimization_barrier((mine[1], gathered0, conv_all))
    gathered0 = [lax.dynamic_update_index_in_dim(t, own, k_chip, 0) for t, own in zip(gathered0, mine[0])]
    push1 = _push_start(mine1, scatter=False, name="push_weights_l1")
    full_conv = {nm: _merge_shards(t, ax)
                 for (nm, ax), t in zip(convs, _unpack_flat(conv_all, [args[nm].shape for nm, _ in convs]))}
    rope = _rope_tables(args["positions"][0])

    def layer_weights(l, gathered):
        p = {nm: _merge_shards(t, ax - 1) for (nm, ax), t in zip(mats, gathered)}
        p.update({nm: full_conv[nm][l] for nm in CONV_SHARDED})
        p.update({nm: args[nm][l] for nm in SMALL if nm != "g_final"})
        return _layer_weights(p)

    layers = [layer_weights(0, gathered0), None]
    layers[0]["g_mix"] = layers[0]["g_mix"] + push1[4][0, 0]
    x, sv0 = _layer_fwd(x, args["p"][0, 0], layers[0], rope, "l0")
    own1, landed1 = _push_wait(push1[0], push1[1], push1[2], push1[3], x, name="wait_weights_l1")
    layers[1] = layer_weights(1, [lax.dynamic_update_index_in_dim(t, own, k_chip, 0) for t, own in zip(landed1, own1)])
    x, sv1 = _layer_fwd(x, args["p"][1, 0], layers[1], rope, "l1")
    saved = [sv0, sv1]

    dx, dg_final, loss_part = _loss_head(x, _row(args["g_final"]), args["loss_target"][0], name="loss_head")
    loss = lax.psum(jnp.sum(loss_part), ("x", "y", "c"))

    early_mats = tuple(mt for mt in MATS if mt[0] in EARLY_L0)
    rest_mats = tuple(mt for mt in MATS if mt[0] == "w_in")
    mid_mats = tuple(mt for mt in MATS if mt not in early_mats + rest_mats)
    grads, reduce0_early, reduce0_mid = [None, None], [], []

    def early0(gr, dx1):
        reduce0_early.append(_reduce_start(gr, early_mats, c_idx, "l0e"))
        return dx1 + reduce0_early[0][4][0, 0]

    def mid0(gr):
        reduce0_mid.append(_reduce_start(gr, mid_mats, c_idx, "l0m"))
        return reduce0_mid[0][4][0, 0]

    dx, grads[1] = _layer_bwd(dx, saved[1], layers[1], rope, "l1")
    reduce1 = _reduce_start(grads[1], MATS, c_idx, "l1")
    dx, grads[0] = _layer_bwd(dx + reduce1[4][0, 0], saved[0], layers[0], rope, "l0", early=early0, mid=mid0)
    g_all = {nm: jnp.stack([grads[0][nm], grads[1][nm]]) for nm in SMALL + CONV_SHARDED if nm != "g_final"}
    g_all["g_final"] = dg_final[0]
    all_names = SMALL + CONV_SHARDED
    all_shapes = [g_all[nm].shape for nm in all_names]
    packed = _pack_flat([g_all[nm] for nm in all_names], F32)
    sibling = _swap_with_sibling([packed], name="swap_small_grads")[0]
    pair = jnp.where(c_idx == 0, jnp.stack([packed, sibling]), jnp.stack([sibling, packed]))
    small_all = _gather_same_core(_sum_slots(pair, name="sum_cores"), name="gather_small_grads")
    g0_mats, small_all = lax.optimization_barrier(({nm: grads[0][nm] for nm, _ in rest_mats}, small_all))
    reduce0 = _reduce_start(g0_mats, rest_mats, c_idx, "l0")
    small_sum = _sum_slots(small_all, name="sum_chips")
    g_red = dict(zip(all_names, _unpack_flat(small_sum, all_shapes)))
    for nm, ax in BIG:
        if nm in CONV_SHARDED:
            width = args[nm].shape[ax]
            g_red[nm] = lax.dynamic_slice_in_dim(g_red[nm], k_chip * width, width, axis=ax)
    small_shapes = [args[nm].shape for nm in SMALL]
    pack_small = lambda src: _pack_flat([src(nm) for nm in SMALL], F32)
    upd_small = _adamw(pack_small(lambda nm: args[nm]), pack_small(lambda nm: g_red[nm]),
                       pack_small(lambda nm: args["m_" + nm]), pack_small(lambda nm: args["v_" + nm]), name="adamw_small")
    upd = {nm: trip for nm, trip in zip(SMALL, zip(*[_unpack_flat(t, small_shapes) for t in upd_small]))}
    for nm in CONV_SHARDED:
        upd[nm] = _adamw(args[nm], g_red[nm], args["m_" + nm], args["v_" + nm], name="adamw_" + nm)

    names = lambda mats_: [nm for nm, _ in mats_]
    slots0 = dict(zip(names(early_mats), _reduce_finish(reduce0_early[0], dx, k_chip, "l0e")))
    slots0.update(zip(names(mid_mats), _reduce_finish(reduce0_mid[0], dx, k_chip, "l0m")))
    slots0.update(zip(names(rest_mats), _reduce_finish(reduce0, upd_small[0], k_chip, "l0")))
    slots1 = dict(zip(names(MATS), _reduce_finish(reduce1, dx, k_chip, "l1")))
    for nm, _ in MATS:
        g_red[nm], *upd[nm] = _adamw_slots(args[nm], [slots0[nm], slots1[nm]], args["m_" + nm], args["v_" + nm],
                                           c_idx, name="adamw_" + nm)

    outs = [loss, dx[None]]
    outs += [g_red[nm] for nm in WEIGHTS]
    for i in range(3):
        outs += [upd[nm][i] for nm in WEIGHTS]
    return tuple(outs)


_ARG_NAMES = ("x", "p", "positions") + WEIGHTS + ("loss_target",) + tuple("m_" + nm for nm in WEIGHTS) \
    + tuple("v_" + nm for nm in WEIGHTS)


def kernel(*arrays):
    assert len(arrays) == len(_ARG_NAMES), len(arrays)
    return _step(dict(zip(_ARG_NAMES, arrays)))
```

```python
import math

import jax
import jax.numpy as jnp
import numpy as np
from jax import lax
from jax.experimental import pallas as pl
from jax.experimental.pallas import tpu as pltpu

F32 = jnp.float32
BF16 = jnp.bfloat16
MXU_DTYPE = BF16
LANES = 128
VMEM_LIMIT = 56 * 1024 * 1024
MM_VMEM_BUDGET = 36 * 1024 * 1024
ELEMENTWISE_BLOCK_BYTES = 2 * 1024 * 1024

D_MODEL = 1024
N_HEADS = 8
HEAD = 64
QK_ROPE = 32
Q_LORA = 384
KV_LORA = 256
MIX = 512
SSD_CHUNK = 128
CONV_W = 4
POOL_WINDOWS = (2, 4, 8, 16)
LRU_C = 8.0
EPS = 1e-6
ROPE_THETA = 10000.0
ATT_SCALE = (HEAD + QK_ROPE) ** -0.5
SPLIT_SIZES = (Q_LORA, KV_LORA, QK_ROPE, MIX, MIX, 768, N_HEADS, MIX, MIX, 4 * D_MODEL)
IN_LAYOUT = (("gates", 0, 4096), ("z", 4096, 1024), ("pool", 5120, 512), ("lru_g", 5632, 512), ("lru_x", 6144, 512),
             ("cq", 6912, 384), ("ckv", 7424, 256), ("xbc", 7680, 1536), ("kr", 9216, 128), ("dt", 9344, 128))
IN_OFFSETS = {name: (off, width) for name, off, width in IN_LAYOUT}
IN_ALL_COLS = 9728

ADAM_LR, ADAM_B1, ADAM_B2, ADAM_EPS, ADAM_WD, ADAM_STEP = 0.001, 0.9, 0.999, 1e-08, 0.01, 10

BIG = (("w_in", 2), ("w_uq", 2), ("w_ukv", 2), ("ssd_conv_w", 2), ("lru_conv_w", 2), ("w_branch", 3),
       ("w_out", 1), ("w_ff1", 2), ("w_ff2", 1), ("w_ple_gate", 1), ("w_ple", 2))
SMALL = ("g_mix", "q_norm", "kv_norm", "w_pool", "pool_scale", "ssd_conv_b", "ssd_dt_bias", "ssd_a_log",
         "ssd_d", "ssd_norm", "lru_conv_b", "lru_w_a", "lru_b_a", "lru_w_i", "lru_b_i", "lru_lambda",
         "g_mlp", "g_ple", "g_final")
WEIGHTS = ("g_mix", "w_in", "q_norm", "w_uq", "kv_norm", "w_ukv", "w_pool", "pool_scale", "ssd_conv_w",
           "ssd_conv_b", "ssd_dt_bias", "ssd_a_log", "ssd_d", "ssd_norm", "lru_conv_w", "lru_conv_b", "lru_w_a",
           "lru_b_a", "lru_w_i", "lru_b_i", "lru_lambda", "w_branch", "w_out", "g_mlp", "w_ff1", "w_ff2", "g_ple",
           "w_ple_gate", "w_ple", "g_final")
CONV_SHARDED = ("ssd_conv_w", "lru_conv_w")
PACK_W = 1024
PACK_ROWS = 64


def _cparams(sem, vmem=VMEM_LIMIT):
    return pltpu.CompilerParams(dimension_semantics=sem, vmem_limit_bytes=vmem)


def _pick(n, cands):
    for c in cands:
        if n % c == 0:
            return c
    return n


class _Cols:
    def __init__(self, arr, off, width):
        self.arr, self.off, self.width = arr, off, width

    shape = property(lambda self: (self.arr.shape[0], self.width))
    dtype = property(lambda self: self.arr.dtype)


def _arr(x):
    return x.arr if isinstance(x, _Cols) else x


def _off(x, unit):
    off = x.off if isinstance(x, _Cols) else 0
    assert off % unit == 0, (off, unit)
    return off // unit


def _sigmoid(x):
    return 1.0 / (1.0 + jnp.exp(-x))


def _silu(x):
    return x * _sigmoid(x)


def _silu_grad(x):
    s = _sigmoid(x)
    return s * (1.0 + x * (1.0 - s))


def _softplus(x):
    e = jnp.exp(-jnp.abs(x))
    log1p_e = jnp.where(e < 1e-3, e * (1.0 - e * (0.5 - e * (1.0 / 3.0))), jnp.log(1.0 + e))
    return jnp.maximum(x, 0.0) + log1p_e


_GELU_C = math.sqrt(2.0 / math.pi)


def _gelu(x):
    t = jnp.tanh(_GELU_C * (x + 0.044715 * x * x * x))
    return 0.5 * x * (1.0 + t)


def _gelu_grad(x):
    t = jnp.tanh(_GELU_C * (x + 0.044715 * x * x * x))
    return 0.5 * (1.0 + t) + 0.5 * x * (1.0 - t * t) * _GELU_C * (1.0 + 3.0 * 0.044715 * x * x)


def _neg_expm1(x):
    series = -x * (1.0 + 0.5 * x * (1.0 + (1.0 / 3.0) * x * (1.0 + 0.25 * x)))
    return jnp.where(x > -0.05, series, 1.0 - jnp.exp(x))


def _shift_down(x, k, row):
    return jnp.where(row >= k, pltpu.roll(x, k, 0), 0.0)


def _shift_up(x, k, row):
    n = x.shape[0]
    return jnp.where(row < n - k, pltpu.roll(x, n - k, 0), 0.0)


def _cumsum_rows(x, row):
    d = 1
    while d < x.shape[0]:
        x = x + _shift_down(x, d, row)
        d *= 2
    return x


def _rev_cumsum_rows(x, row):
    d = 1
    while d < x.shape[0]:
        x = x + _shift_up(x, d, row)
        d *= 2
    return x


def _cumsum_lanes(x, col):
    d = 1
    while d < x.shape[1]:
        x = x + jnp.where(col >= d, pltpu.roll(x, d, 1), 0.0)
        d *= 2
    return x


def _dot(a, b, ta=False, tb=False):
    dn = (((0 if ta else 1,), (1 if tb else 0,)), ((), ()))
    return lax.dot_general(a.astype(MXU_DTYPE), b.astype(MXU_DTYPE), dn, preferred_element_type=F32)


def _mm_tiles(m, n, k, a_bytes, b_bytes, mn_bytes):
    best = None
    for tm in (1024, 512, 384, 256, 128):
        for tn in (1024, 512, 384, 256, 128):
            for tk in (2048, 1024, 512, 384, 256, 128):
                if m % tm or n % tn or k % tk:
                    continue
                vmem = 2 * (tm * tk * a_bytes + tk * tn * b_bytes) + 2 * tm * tn * mn_bytes + 4 * tm * tn
                vmem += 2 * (tm * tk + tk * tn)
                if vmem > MM_VMEM_BUDGET:
                    continue
                steps = (m // tm) * (n // tn) * (k // tk)
                key = (steps, vmem)
                if best is None or key < best[0]:
                    best = (key, (tm, tn, tk))
    assert best is not None, (m, n, k)
    return best[1]


def _mm(a, b, *, ta=False, tb=False, epilogue=None, tiles=(), rowvecs=(), out_dtypes=(F32,), name):
    m, k = (a.shape[1], a.shape[0]) if ta else a.shape
    n = b.shape[0] if tb else b.shape[1]
    assert (b.shape[1] if tb else b.shape[0]) == k, (a.shape, b.shape, ta, tb)
    mn_bytes = sum(t.dtype.itemsize for t in tiles) + sum(jnp.dtype(dt).itemsize for dt in out_dtypes)
    tm, tn, tk = _mm_tiles(m, n, k, a.dtype.itemsize, b.dtype.itemsize, mn_bytes)
    nk = k // tk
    nt, nr, no = len(tiles), len(rowvecs), len(out_dtypes)

    def body(*refs):
        a_ref, b_ref = refs[0], refs[1]
        tile_refs = refs[2:2 + nt]
        row_refs = refs[2 + nt:2 + nt + nr]
        out_refs = refs[2 + nt + nr:2 + nt + nr + no]
        acc_ref = refs[-1]
        kk = pl.program_id(2)

        @pl.when(kk == 0)
        def _():
            acc_ref[...] = jnp.zeros_like(acc_ref)

        acc_ref[...] += _dot(a_ref[...], b_ref[...], ta, tb)

        @pl.when(kk == nk - 1)
        def _():
            acc = acc_ref[...]
            if epilogue is None:
                outs = (acc,)
            else:
                outs = epilogue(acc, *[t[...] for t in tile_refs], *[r[...] for r in row_refs])
            for o_ref, o in zip(out_refs, outs):
                o_ref[...] = o.astype(o_ref.dtype)

    a_spec = pl.BlockSpec((tk, tm), lambda i, j, kk: (kk, i)) if ta else pl.BlockSpec((tm, tk), lambda i, j, kk: (i, kk))
    b_spec = pl.BlockSpec((tn, tk), lambda i, j, kk: (j, kk)) if tb else pl.BlockSpec((tk, tn), lambda i, j, kk: (kk, j))
    mn_spec = pl.BlockSpec((tm, tn), lambda i, j, kk: (i, j))
    row_spec = pl.BlockSpec((1, tn), lambda i, j, kk: (0, j))
    tile_specs = [pl.BlockSpec((tm, tn), lambda i, j, kk, ob=_off(t, tn): (i, j + ob)) for t in tiles]
    outs = pl.pallas_call(
        body, name=name,
        grid=(m // tm, n // tn, nk),
        in_specs=[a_spec, b_spec] + tile_specs + [row_spec] * nr,
        out_specs=[mn_spec] * no,
        out_shape=[jax.ShapeDtypeStruct((m, n), dt) for dt in out_dtypes],
        scratch_shapes=[pltpu.VMEM((tm, tn), F32)],
        compiler_params=_cparams(("parallel", "parallel", "arbitrary")),
    )(a, b, *[_arr(t) for t in tiles], *rowvecs)
    return outs[0] if no == 1 else tuple(outs)


def _branch_merge(ys, ws, u_all, *, name):
    s, d = ys[0].shape[0], ws[0].shape[1]
    tm, tn = _pick(s, (512, 256, 128)), _pick(d, (512, 256, 128))
    nb = len(ys)

    def body(*refs):
        y_refs, w_refs, g_refs = refs[:nb], refs[nb:2 * nb], refs[2 * nb:3 * nb]
        merged_ref, yb_refs = refs[3 * nb], refs[3 * nb + 1:]
        merged = None
        for y_ref, w_ref, g_ref, yb_ref in zip(y_refs, w_refs, g_refs, yb_refs):
            acc = _dot(y_ref[...], w_ref[...])
            yb_ref[...] = acc.astype(yb_ref.dtype)
            term = _sigmoid(g_ref[...]) * acc
            merged = term if merged is None else merged + term
        merged_ref[...] = merged

    mn = pl.BlockSpec((tm, tn), lambda i, j: (i, j))
    in_specs = [pl.BlockSpec((tm, y.shape[1]), lambda i, j: (i, 0)) for y in ys]
    in_specs += [pl.BlockSpec((w.shape[0], tn), lambda i, j: (0, j)) for w in ws]
    in_specs += [pl.BlockSpec((tm, tn), lambda i, j, ob=b * d // tn: (i, j + ob)) for b in range(nb)]
    return pl.pallas_call(
        body, name=name, grid=(s // tm, d // tn), in_specs=in_specs, out_specs=[mn] * (nb + 1),
        out_shape=[jax.ShapeDtypeStruct((s, d), F32)] + [jax.ShapeDtypeStruct((s, d), BF16)] * nb,
        compiler_params=_cparams(("parallel", "parallel")),
    )(*ys, *ws, *[u_all] * nb)


def _rowwise(fn, rows, fulls, outs, *, name, tm=None):
    r = rows[0].shape[0]
    if tm is None:
        widest = max([x.shape[1] for x in rows] + [o[0] for o in outs])
        tm = _pick(r, (max(8, min(512, (512 * 1024) // widest)), 256, 128, 64, 32, 16, 8))
    nrow, nfull, nout = len(rows), len(fulls), len(outs)

    def body(*refs):
        row_refs = refs[:nrow]
        full_refs = refs[nrow:nrow + nfull]
        out_refs = refs[nrow + nfull:]
        res = fn(*[x[...] for x in row_refs], *[x[...] for x in full_refs])
        if not isinstance(res, (tuple, list)):
            res = (res,)
        step = pl.program_id(0)
        for o_ref, o, spec in zip(out_refs, res, outs):
            if spec[2] == "row":
                o_ref[...] = o.astype(o_ref.dtype)
            else:
                @pl.when(step == 0)
                def _(o_ref=o_ref):
                    o_ref[...] = jnp.zeros_like(o_ref)
                o_ref[...] += o

    in_specs = [pl.BlockSpec((tm, x.shape[1]), lambda i, ob=_off(x, x.shape[1]): (i, ob)) for x in rows]
    in_specs += [pl.BlockSpec(x.shape, lambda i, nd=x.ndim: (0,) * nd) for x in fulls]
    out_specs, out_shape = [], []
    for c, dt, kind in outs:
        if kind == "row":
            out_specs.append(pl.BlockSpec((tm, c), lambda i: (i, 0)))
            out_shape.append(jax.ShapeDtypeStruct((r, c), dt))
        else:
            out_specs.append(pl.BlockSpec((1, c), lambda i: (0, 0)))
            out_shape.append(jax.ShapeDtypeStruct((1, c), F32))
    res = pl.pallas_call(
        body, name=name, grid=(r // tm,), in_specs=in_specs, out_specs=out_specs, out_shape=out_shape,
        compiler_params=_cparams(("arbitrary",)),
    )(*[_arr(x) for x in rows], *fulls)
    return res[0] if nout == 1 else tuple(res)


def _colsum(x):
    return jnp.sum(x, axis=0, keepdims=True)


def _rms_parts(x, n_real):
    r = lax.rsqrt(jnp.sum(x * x, axis=-1, keepdims=True) * (1.0 / n_real) + EPS)
    return x * r, r


def _rms_fwd(x, g, *, n_real=None, out_dtype=BF16, name):
    n_real = n_real or x.shape[1]

    def fn(xv, gv):
        xh, _ = _rms_parts(xv, n_real)
        return xh * gv

    return _rowwise(fn, [x], [g], [(x.shape[1], out_dtype, "row")], name=name)


def _rms_bwd_math(xv, gv, dh, n_real):
    xh, r = _rms_parts(xv, n_real)
    dxh = dh * gv
    dx = r * (dxh - xh * (jnp.sum(dxh * xh, axis=-1, keepdims=True) * (1.0 / n_real)))
    return dx, _colsum(dh * xh)


def _rms_bwd(x, g, dh, res=None, *, name):
    n = x.shape[1]
    if res is None:
        def fn(xv, dhv, gv):
            return _rms_bwd_math(xv, gv, dhv.astype(F32), n)
        rows = [x, dh]
    else:
        def fn(xv, dhv, rv, gv):
            dx, dg = _rms_bwd_math(xv, gv, dhv.astype(F32), n)
            return dx + rv, dg
        rows = [x, dh, res]
    return _rowwise(fn, rows, [g], [(n, F32, "row"), (n, F32, "acc")], name=name)


def _seq_call(body, ins, outs, n_blocks, *, name):
    in_specs, args = [], []
    for x, kind in ins:
        in_specs.append(pl.BlockSpec((x.shape[0], LANES), lambda j, ob=_off(x, LANES): (0, j + ob)))
        args.append(_arr(x))
    out_specs, out_shape = [], []
    for shape, dt in outs:
        out_specs.append(pl.BlockSpec((shape[0], LANES), lambda j: (0, j)))
        out_shape.append(jax.ShapeDtypeStruct(shape, dt))
    res = pl.pallas_call(body, name=name, grid=(n_blocks,), in_specs=in_specs, out_specs=out_specs,
                         out_shape=out_shape, compiler_params=_cparams(("parallel",)))(*args)
    return res[0] if len(outs) == 1 else tuple(res)


def _conv_pre(x, w, b, row):
    acc = x * w[CONV_W - 1:CONV_W, :] + b
    for k in range(CONV_W - 1):
        acc = acc + _shift_down(x, CONV_W - 1 - k, row) * w[k:k + 1, :]
    return acc


def _conv_fwd(x, w, b, *, silu, name):
    s, c = x.shape

    def body(x_ref, w_ref, b_ref, y_ref):
        xv = x_ref[...]
        row = lax.broadcasted_iota(jnp.int32, xv.shape, 0)
        pre = _conv_pre(xv, w_ref[...], b_ref[...], row)
        y_ref[...] = _silu(pre) if silu else pre

    return _seq_call(body, [(x, "seq"), (w, "par"), (b, "par")], [((s, c), F32)], c // LANES, name=name)


def _conv_bwd(x, w, b, dy, *, silu, name):
    s, c = x.shape

    def body(x_ref, w_ref, b_ref, dy_ref, dx_ref, dw_ref, db_ref):
        xv, wv, dv = x_ref[...], w_ref[...], dy_ref[...]
        row = lax.broadcasted_iota(jnp.int32, xv.shape, 0)
        if silu:
            dv = dv * _silu_grad(_conv_pre(xv, wv, b_ref[...], row))
        dx = dv * wv[CONV_W - 1:CONV_W, :]
        dws = [None] * CONV_W
        dws[CONV_W - 1] = _colsum(dv * xv)
        for k in range(CONV_W - 1):
            sh = CONV_W - 1 - k
            dx = dx + _shift_up(dv, sh, row) * wv[k:k + 1, :]
            dws[k] = _colsum(dv * _shift_down(xv, sh, row))
        dx_ref[...] = dx
        for k in range(CONV_W):
            dw_ref[k:k + 1, :] = dws[k]
        db_ref[...] = _colsum(dv)

    return _seq_call(body, [(x, "seq"), (w, "par"), (b, "par"), (dy, "seq")],
                     [((s, c), F32), ((CONV_W, c), F32), ((1, c), F32)], c // LANES, name=name)


def _pool_select(levels):
    g = pl.program_id(0)
    return jnp.where(g == 0, levels[0], jnp.where(g == 1, levels[1], jnp.where(g == 2, levels[2], levels[3])))


def _pool_count(row):
    g = pl.program_id(0)
    w = jnp.where(g == 0, POOL_WINDOWS[0], jnp.where(g == 1, POOL_WINDOWS[1],
                                                     jnp.where(g == 2, POOL_WINDOWS[2], POOL_WINDOWS[3])))
    return jnp.minimum(row + 1, w).astype(F32)


def _pool_fwd(u, *, name):
    def body(u_ref, d_ref):
        uv = u_ref[...]
        row = lax.broadcasted_iota(jnp.int32, uv.shape, 0)
        levels, cur, sh = [], uv, 1
        for _ in POOL_WINDOWS:
            cur = cur + _shift_down(cur, sh, row)
            levels.append(cur)
            sh *= 2
        d_ref[...] = _pool_select(levels) / _pool_count(row) - uv

    return _seq_call(body, [(u, "seq")], [(u.shape, F32)], u.shape[1] // LANES, name=name)


def _pool_bwd(dd, *, name):
    def body(dd_ref, du_ref):
        dv = dd_ref[...]
        row = lax.broadcasted_iota(jnp.int32, dv.shape, 0)
        levels, cur, sh = [], dv / _pool_count(row), 1
        for _ in POOL_WINDOWS:
            cur = cur + _shift_up(cur, sh, row)
            levels.append(cur)
            sh *= 2
        du_ref[...] = _pool_select(levels) - dv

    return _seq_call(body, [(dd, "seq")], [(dd.shape, F32)], dd.shape[1] // LANES, name=name)


def _lru_gates(pre_a, pre_i, xc, lam, b_a, b_i):
    r = _sigmoid(pre_a + b_a)
    i = _sigmoid(pre_i + b_i)
    sp = _softplus(-lam)
    log_a = -LRU_C * r * sp
    a = jnp.exp(log_a)
    mult = jnp.sqrt(_neg_expm1(2.0 * log_a))
    return r, i, sp, a, mult


def _lru_fwd(pre, xc, gate_in, lam, b_a, b_i, *, name):
    s, c = xc.shape
    nb = c // LANES

    def body(pa_ref, pi_ref, xc_ref, g_ref, lam_ref, ba_ref, bi_ref, y_ref, h_ref):
        xv = xc_ref[...]
        row = lax.broadcasted_iota(jnp.int32, xv.shape, 0)
        _, i, _, a, mult = _lru_gates(pa_ref[...], pi_ref[...], xv, lam_ref[...], ba_ref[...], bi_ref[...])
        h = xv * i * mult
        d = 1
        while d < s:
            h = h + a * _shift_down(h, d, row)
            a = a * jnp.where(row >= d, pltpu.roll(a, d, 0), 1.0)
            d *= 2
        h_ref[...] = h
        y_ref[...] = h * _gelu(g_ref[...])

    blk = lambda off: pl.BlockSpec((s, LANES), lambda j: (0, j + off))
    par = pl.BlockSpec((1, LANES), lambda j: (0, j))
    return pl.pallas_call(
        body, name=name, grid=(nb,),
        in_specs=[blk(0), blk(nb), blk(0), blk(_off(gate_in, LANES)), par, par, par],
        out_specs=[blk(0), blk(0)],
        out_shape=[jax.ShapeDtypeStruct((s, c), F32)] * 2,
        compiler_params=_cparams(("parallel",)),
    )(pre, pre, xc, _arr(gate_in), lam, b_a, b_i)


def _lru_bwd(pre, xc, gate_in, lam, b_a, b_i, h, dy, *, name):
    s, c = xc.shape
    nb = c // LANES

    def body(pa_ref, pi_ref, xc_ref, g_ref, lam_ref, ba_ref, bi_ref, h_ref, dy_ref,
             dpa_ref, dpi_ref, dxc_ref, dg_ref, dlam_ref, dba_ref, dbi_ref):
        xv, gv, hv, dv = xc_ref[...], g_ref[...], h_ref[...], dy_ref[...]
        row = lax.broadcasted_iota(jnp.int32, xv.shape, 0)
        r, i, sp, a, mult = _lru_gates(pa_ref[...], pi_ref[...], xv, lam_ref[...], ba_ref[...], bi_ref[...])
        dg_ref[...] = dv * hv * _gelu_grad(gv)
        dh = dv * _gelu(gv)
        an = jnp.where(row < s - 1, pltpu.roll(a, s - 1, 0), 0.0)
        d = 1
        while d < s:
            dh = dh + an * _shift_up(dh, d, row)
            an = an * jnp.where(row < s - d, pltpu.roll(an, s - d, 0), 1.0)
            d *= 2
        da = dh * _shift_down(hv, 1, row)
        dxc_ref[...] = dh * i * mult
        di = dh * xv * mult
        dmult = dh * xv * i
        dlog_a = (da - dmult * a / mult) * a
        dr = dlog_a * (-LRU_C) * sp
        dlam_ref[...] = _colsum(dlog_a * LRU_C * r * _sigmoid(-lam_ref[...]))
        dpa = dr * r * (1.0 - r)
        dpi = di * i * (1.0 - i)
        dpa_ref[...] = dpa
        dpi_ref[...] = dpi
        dba_ref[...] = _colsum(dpa)
        dbi_ref[...] = _colsum(dpi)

    blk = lambda off: pl.BlockSpec((s, LANES), lambda j: (0, j + off))
    par = pl.BlockSpec((1, LANES), lambda j: (0, j))
    sc = jax.ShapeDtypeStruct((s, c), F32)
    pc = jax.ShapeDtypeStruct((1, c), F32)
    dpa, dpi, dxc, dg, dlam, dba, dbi = pl.pallas_call(
        body, name=name, grid=(nb,),
        in_specs=[blk(0), blk(nb), blk(0), blk(_off(gate_in, LANES)), par, par, par, blk(0), blk(0)],
        out_specs=[blk(0), blk(0), blk(0), blk(0), par, par, par],
        out_shape=[sc, sc, sc, sc, pc, pc, pc],
        compiler_params=_cparams(("parallel",)),
    )(pre, pre, xc, _arr(gate_in), lam, b_a, b_i, h, dy)
    return dpa, dpi, dxc, dg, dlam, dba, dbi


GROUP_HEADS = 4
SSD_GROUPS = 2


def _ssd_specs(nc, order):
    hw, gw = N_HEADS * LANES, SSD_GROUPS * LANES
    return dict(
        x=pl.BlockSpec((SSD_CHUNK, hw), lambda ci: (order(ci), 0)),
        b=pl.BlockSpec((SSD_CHUNK, gw), lambda ci: (order(ci), hw // gw)),
        c=pl.BlockSpec((SSD_CHUNK, gw), lambda ci: (order(ci), hw // gw + 1)),
        dtcol=pl.BlockSpec((N_HEADS, SSD_CHUNK, 1), lambda ci: (0, order(ci), 0)),
        dtrow=pl.BlockSpec((N_HEADS, 1, SSD_CHUNK), lambda ci: (0, 0, order(ci))),
        scal=pl.BlockSpec((N_HEADS, 1, 1), lambda ci: (0, 0, 0)),
        state=pl.BlockSpec((N_HEADS, 1, LANES, LANES), lambda ci: (0, order(ci), 0, 0)),
        group=pl.BlockSpec((SSD_CHUNK, gw), lambda ci: (order(ci), 0)),
        pacc=pl.BlockSpec((N_HEADS, 1, LANES), lambda ci: (0, 0, 0)),
    )


def _ssd_chunk_terms(dtcol, dtrow, bias, a_log):
    shp = (SSD_CHUNK, SSD_CHUNK)
    row = lax.broadcasted_iota(jnp.int32, shp, 0)
    col = lax.broadcasted_iota(jnp.int32, shp, 1)
    a_head = -jnp.exp(a_log)
    dt_c = jnp.broadcast_to(_softplus(dtcol + bias), shp)
    dt_r = jnp.broadcast_to(_softplus(dtrow + bias), shp)
    cs_c = _cumsum_rows(dt_c * a_head, row)
    cs_r = _cumsum_lanes(dt_r * a_head, col)
    cs_last = jnp.sum(jnp.where(row == SSD_CHUNK - 1, cs_c, 0.0), axis=0, keepdims=True)
    return row, col, a_head, dt_c, cs_c, cs_r, cs_last


def _ssd_fwd(xbc, dtcol, dtrow, bias, a_log, dskip, *, name):
    s = xbc.shape[0]
    nc = s // SSD_CHUNK

    def body(x_ref, b_ref, c_ref, dtc_ref, dtr_ref, bias_ref, alog_ref, d_ref, y_ref, st_ref, state):
        ci = pl.program_id(0)

        @pl.when(ci == 0)
        def _():
            state[...] = jnp.zeros_like(state)

        for gi in range(SSD_GROUPS):
            glanes = slice(gi * LANES, (gi + 1) * LANES)
            bm, cm = b_ref[:, glanes], c_ref[:, glanes]
            cb = _dot(cm, bm, tb=True)
            bm_t = bm.T
            for r in range(gi * GROUP_HEADS, (gi + 1) * GROUP_HEADS):
                lanes = slice(r * LANES, (r + 1) * LANES)
                xv = x_ref[:, lanes]
                row, col, _, dt_c, cs_c, cs_r, cs_last = _ssd_chunk_terms(dtc_ref[r], dtr_ref[r], bias_ref[r], alog_ref[r])
                g = cb * jnp.exp(jnp.where(col <= row, cs_c - cs_r, -jnp.inf))
                xdt = xv * dt_c
                st = state[r]
                st_ref[r, 0] = st
                y_ref[:, lanes] = _dot(g, xdt) + _dot(cm, st) * jnp.exp(cs_c) + xv * d_ref[r]
                state[r] = jnp.exp(cs_last) * st + _dot(bm_t, xdt * jnp.exp(cs_last - cs_c))

    sp = _ssd_specs(nc, lambda ci: ci)
    return pl.pallas_call(
        body, name=name, grid=(nc,),
        in_specs=[sp["x"], sp["b"], sp["c"], sp["dtcol"], sp["dtrow"], sp["scal"], sp["scal"], sp["scal"]],
        out_specs=[sp["x"], sp["state"]],
        out_shape=[jax.ShapeDtypeStruct((s, N_HEADS * LANES), F32),
                   jax.ShapeDtypeStruct((N_HEADS, nc, LANES, LANES), F32)],
        scratch_shapes=[pltpu.VMEM((N_HEADS, LANES, LANES), F32)],
        compiler_params=_cparams(("arbitrary",)),
    )(xbc, xbc, xbc, dtcol, dtrow, bias, a_log, dskip)


def _ssd_bwd(xbc, dtcol, dtrow, bias, a_log, dskip, states, dy, *, name):
    s = xbc.shape[0]
    nc = s // SSD_CHUNK

    def body(x_ref, b_ref, c_ref, dtc_ref, dtr_ref, bias_ref, alog_ref, d_ref, st_ref, dy_ref,
             dx_ref, db_ref, dc_ref, ddt_ref, dbias_ref, dalog_ref, dd_ref, dstate):
        ci = pl.program_id(0)

        @pl.when(ci == 0)
        def _():
            dstate[...] = jnp.zeros_like(dstate)
            dbias_ref[...] = jnp.zeros_like(dbias_ref)
            dalog_ref[...] = jnp.zeros_like(dalog_ref)
            dd_ref[...] = jnp.zeros_like(dd_ref)

        rowsum = lambda v: jnp.sum(v, axis=1, keepdims=True)
        tot = lambda v: jnp.broadcast_to(jnp.sum(v, axis=0, keepdims=True), (1, LANES))
        for gi in range(SSD_GROUPS):
            glanes = slice(gi * LANES, (gi + 1) * LANES)
            bm, cm = b_ref[:, glanes], c_ref[:, glanes]
            cb = _dot(cm, bm, tb=True)
            cb_t = _dot(bm, cm, tb=True)
            cm_t = cm.T
            dbm_sum, dcm_sum = None, None
            for r in range(gi * GROUP_HEADS, (gi + 1) * GROUP_HEADS):
                lanes = slice(r * LANES, (r + 1) * LANES)
                xv, dyv, st = x_ref[:, lanes], dy_ref[:, lanes], st_ref[r, 0]
                dtraw_c, bias = dtc_ref[r], bias_ref[r]
                row, col, a_head, dt_c, cs_c, cs_r, cs_last = _ssd_chunk_terms(dtraw_c, dtr_ref[r], bias, alog_ref[r])
                lmat = jnp.exp(jnp.where(col <= row, cs_c - cs_r, -jnp.inf))
                lmat_t = jnp.exp(jnp.where(row <= col, cs_r - cs_c, -jnp.inf))
                g, g_t = cb * lmat, cb_t * lmat_t
                xdt = xv * dt_c
                e_c = jnp.exp(cs_c)
                f_c = jnp.exp(cs_last - cs_c)
                e_last = jnp.exp(cs_last)
                w = xdt * f_c
                dst = dstate[r]

                dg = _dot(dyv, xdt, tb=True)
                dg_t = _dot(xdt, dyv, tb=True)
                dxdt = _dot(g_t, dyv)
                dcs = rowsum(dg * g) - rowsum(dg_t * g_t)
                dcm = _dot(dg * lmat, bm)
                dbm = _dot(dg_t * lmat_t, cm)
                z = _dot(cm, st)
                dz = dyv * e_c
                dcs = dcs + rowsum(dz * z)
                dcm = dcm + _dot(dz, st, tb=True)
                dstate[r] = _dot(cm_t, dz) + e_last * dst
                dcs_last = jnp.sum(rowsum(dst * st), axis=0, keepdims=True) * jnp.max(e_last, axis=1, keepdims=True)
                dbm = dbm + _dot(w, dst, tb=True)
                dw = _dot(bm, dst)
                dxdt = dxdt + dw * f_c
                q = rowsum(dw * w)
                dcs = dcs - q
                dcs_last = dcs_last + jnp.sum(q, axis=0, keepdims=True)
                dx_ref[:, lanes] = dxdt * dt_c + dyv * d_ref[r]
                ddt = rowsum(dxdt * xv)
                dcs_full = jnp.broadcast_to(dcs, (SSD_CHUNK, SSD_CHUNK)) + jnp.where(row == SSD_CHUNK - 1, dcs_last, 0.0)
                da = jnp.max(_rev_cumsum_rows(dcs_full, row), axis=1, keepdims=True)
                dt_col = jnp.max(dt_c, axis=1, keepdims=True)
                draw = (ddt + da * a_head) * _sigmoid(dtraw_c + bias)
                ddt_ref[r] = jnp.broadcast_to(draw, (SSD_CHUNK, LANES)).T[:1, :]
                dbias_ref[r] += tot(draw)
                dalog_ref[r] += tot(da * dt_col) * a_head
                dd_ref[r] += tot(rowsum(dyv * xv))
                dbm_sum = dbm if dbm_sum is None else dbm_sum + dbm
                dcm_sum = dcm if dcm_sum is None else dcm_sum + dcm
            db_ref[:, glanes] = dbm_sum
            dc_ref[:, glanes] = dcm_sum

    sp = _ssd_specs(nc, lambda ci: nc - 1 - ci)
    return pl.pallas_call(
        body, name=name, grid=(nc,),
        in_specs=[sp["x"], sp["b"], sp["c"], sp["dtcol"], sp["dtrow"], sp["scal"], sp["scal"], sp["scal"],
                  sp["state"], sp["x"]],
        out_specs=[sp["x"], sp["group"], sp["group"], sp["dtrow"], sp["pacc"], sp["pacc"], sp["pacc"]],
        out_shape=[jax.ShapeDtypeStruct((s, N_HEADS * LANES), F32),
                   jax.ShapeDtypeStruct((s, 2 * LANES), F32),
                   jax.ShapeDtypeStruct((s, 2 * LANES), F32),
                   jax.ShapeDtypeStruct((N_HEADS, 1, s), F32),
                   jax.ShapeDtypeStruct((N_HEADS, 1, LANES), F32),
                   jax.ShapeDtypeStruct((N_HEADS, 1, LANES), F32),
                   jax.ShapeDtypeStruct((N_HEADS, 1, LANES), F32)],
        scratch_shapes=[pltpu.VMEM((N_HEADS, LANES, LANES), F32)],
        compiler_params=_cparams(("arbitrary",)),
    )(xbc, xbc, xbc, dtcol, dtrow, bias, a_log, dskip, states, dy)


def _att_tile(s):
    return _pick(s, (512, 256, 128))


def _tri(t, transposed=False):
    r = lax.broadcasted_iota(jnp.int32, (t, t), 0)
    c = lax.broadcasted_iota(jnp.int32, (t, t), 1)
    return (r <= c) if transposed else (c <= r)


def _rows_at(ref, blk, t):
    return ref[pl.ds(pl.multiple_of(blk * t, t), t), :]


def _flash_fwd(q, k, v, *, name):
    s = q.shape[0]
    t = _att_tile(s)
    nq = s // t

    def body(q_ref, k_ref, v_ref, o_ref, lse_ref):
        i = pl.program_id(1)
        qv = q_ref[...]

        def step(j, carry, diagonal):
            m_old, l_old, acc = carry
            sc = _dot(qv, _rows_at(k_ref, j, t), tb=True)
            if diagonal:
                sc = jnp.where(_tri(t), sc, -jnp.inf)
            m_new = jnp.maximum(m_old, jnp.max(sc, axis=1, keepdims=True))
            alpha = jnp.exp(m_old - m_new)
            p = jnp.exp(sc - m_new)
            return (m_new, alpha * l_old + jnp.sum(p, axis=1, keepdims=True),
                    alpha * acc + _dot(p, _rows_at(v_ref, j, t)))

        init = (jnp.full((t, 1), -jnp.inf, F32), jnp.zeros((t, 1), F32), jnp.zeros((t, LANES), F32))
        carry = lax.fori_loop(0, i, lambda j, c: step(j, c, False), init)
        m_fin, l_fin, acc = step(i, carry, True)
        o_ref[...] = (acc / l_fin).astype(o_ref.dtype)
        lse_ref[0] = jnp.broadcast_to(m_fin + jnp.log(l_fin), (t, LANES)).T[:1, :]

    q_spec = pl.BlockSpec((t, LANES), lambda h, i: (i, h))
    kv_spec = pl.BlockSpec((s, LANES), lambda h, i: (0, h))
    return pl.pallas_call(
        body, name=name, grid=(N_HEADS, nq),
        in_specs=[q_spec, kv_spec, kv_spec],
        out_specs=[q_spec, pl.BlockSpec((1, 1, t), lambda h, i: (h, 0, i))],
        out_shape=[jax.ShapeDtypeStruct(q.shape, BF16), jax.ShapeDtypeStruct((N_HEADS, 1, s), F32)],
        compiler_params=_cparams(("parallel", "arbitrary")),
    )(q, k, v)


def _att_delta(o, do, *, name):
    s = o.shape[0]

    def body(o_ref, do_ref, dl_ref):
        col = jnp.sum(do_ref[...].astype(F32) * o_ref[...].astype(F32), axis=1, keepdims=True)
        dl_ref[0] = jnp.broadcast_to(col, (s, LANES)).T[:1, :]

    blk = pl.BlockSpec((s, LANES), lambda h: (0, h))
    return pl.pallas_call(
        body, name=name, grid=(N_HEADS,), in_specs=[blk, blk],
        out_specs=pl.BlockSpec((1, 1, s), lambda h: (h, 0, 0)),
        out_shape=jax.ShapeDtypeStruct((N_HEADS, 1, s), F32),
        compiler_params=_cparams(("parallel",)),
    )(o, do)


def _flash_bwd(q, k, v, do, lse_row, delta_row, *, name):
    s = q.shape[0]
    t = _att_tile(s)
    nq = s // t

    def body(q_ref, k_ref, v_ref, do_ref, lse_ref, dl_ref, dq_ref, dk_ref, dv_ref):
        j = pl.program_id(1)
        kv, vv = k_ref[...], v_ref[...]

        @pl.when(j == 0)
        def _():
            dq_ref[...] = jnp.zeros_like(dq_ref)

        def step(i, carry, diagonal):
            dk, dv = carry
            rows = pl.ds(pl.multiple_of(i * t, t), t)
            qi, doi = q_ref[rows, :], do_ref[rows, :]
            p_t = jnp.exp(_dot(kv, qi, tb=True) - lse_ref[0, :, rows])
            if diagonal:
                p_t = jnp.where(_tri(t, transposed=True), p_t, 0.0)
            ds_t = (p_t * (_dot(vv, doi, tb=True) - dl_ref[0, :, rows])).astype(MXU_DTYPE)
            dq_ref[rows, :] += _dot(ds_t, kv, ta=True)
            return dk + _dot(ds_t, qi), dv + _dot(p_t, doi)

        zero = jnp.zeros((t, LANES), F32)
        carry = step(j, (zero, zero), True)
        dk, dv = lax.fori_loop(j + 1, nq, lambda i, c: step(i, c, False), carry)
        dk_ref[...] = dk
        dv_ref[...] = dv

        @pl.when(j == nq - 1)
        def _():
            dq_ref[...] = dq_ref[...] * ATT_SCALE

    q_spec = pl.BlockSpec((s, LANES), lambda h, j: (0, h))
    kv_spec = pl.BlockSpec((t, LANES), lambda h, j: (j, h))
    row_spec = pl.BlockSpec((1, 1, s), lambda h, j: (h, 0, 0))
    return pl.pallas_call(
        body, name=name, grid=(N_HEADS, nq),
        in_specs=[q_spec, kv_spec, kv_spec, q_spec, row_spec, row_spec],
        out_specs=[q_spec, kv_spec, kv_spec],
        out_shape=[jax.ShapeDtypeStruct(q.shape, F32)] * 3,
        compiler_params=_cparams(("parallel", "arbitrary")),
    )(q, k, v, do, lse_row, delta_row)


def _rope(v, cos_t, sin_p, sin_m):
    return v * cos_t + pltpu.roll(v, QK_ROPE // 2, 1) * sin_p + pltpu.roll(v, LANES - QK_ROPE // 2, 1) * sin_m


def _rope_t(d, cos_t, sin_p, sin_m):
    return d * cos_t + pltpu.roll(d * sin_p, LANES - QK_ROPE // 2, 1) + pltpu.roll(d * sin_m, QK_ROPE // 2, 1)


def _att_prep(q_pad, kv2, kr, cos_t, sin_p, sin_m, *, name):
    w = N_HEADS * LANES

    def fn(qv, kvv, krv, c, sp, sm):
        kr_rot = _rope(krv, c, sp, sm)
        qs, ks = [], []
        for h in range(N_HEADS):
            blk = slice(h * LANES, (h + 1) * LANES)
            qs.append(_rope(qv[:, blk], c, sp, sm) * ATT_SCALE)
            ks.append(kvv[:, blk] + kr_rot)
        return jnp.concatenate(qs, axis=1), jnp.concatenate(ks, axis=1), kvv[:, w:]

    return _rowwise(fn, [q_pad, kv2, kr, cos_t, sin_p, sin_m], [],
                    [(w, BF16, "row"), (w, BF16, "row"), (w, BF16, "row")], name=name)


def _att_prep_bwd(dq, dk, cos_t, sin_p, sin_m, *, name):
    w = N_HEADS * LANES

    def fn(dqv, dkv, c, sp, sm):
        outs, dkr = [], None
        for h in range(N_HEADS):
            blk = slice(h * LANES, (h + 1) * LANES)
            outs.append(_rope_t(dqv[:, blk], c, sp, sm))
            dkr = dkv[:, blk] if dkr is None else dkr + dkv[:, blk]
        return jnp.concatenate(outs, axis=1), _rope_t(dkr, c, sp, sm)

    return _rowwise(fn, [dq, dk, cos_t, sin_p, sin_m], [], [(w, BF16, "row"), (LANES, F32, "row")], name=name)


_ANY = pl.BlockSpec(memory_space=pl.ANY)
_MESH = pl.DeviceIdType.MESH


def _mesh_pos():
    return lax.axis_index("x"), lax.axis_index("y"), lax.axis_index("c")


def _remote(src, dst, send_sem, recv_sem, dev):
    return pltpu.make_async_remote_copy(src_ref=src, dst_ref=dst, send_sem=send_sem, recv_sem=recv_sem,
                                        device_id=dev, device_id_type=_MESH)


def _other_chips(x, y):
    chips = [(1 - x, y), (x, 1 - y), (1 - x, 1 - y)]
    return chips, [2 * cx + cy for cx, cy in chips]


def _comm_call(body, ins, out_shapes, n_sems, *, name):
    return pl.pallas_call(
        body, name=name, in_specs=[_ANY] * len(ins), out_specs=[_ANY] * len(out_shapes), out_shape=out_shapes,
        scratch_shapes=[pltpu.SemaphoreType.DMA((k,)) for k in n_sems],
    )(*ins)


def _gather_halves(shards):
    n = len(shards)
    halves = [t.shape[0] // 2 for t in shards]

    def body(*refs):
        xs, outs = refs[:n], refs[n:2 * n]
        send_sems, recv_sems = refs[2 * n:]
        x, y, c = _mesh_pos()
        k = 2 * x + y
        sibling = (x, y, 1 - c)
        chips, ks = _other_chips(x, y)
        half = lambda w, hf: pl.ds(hf * halves[w], halves[w])
        first = [_remote(xs[w].at[half(w, c)], outs[w].at[k, half(w, c)], send_sems.at[6 * w + j], recv_sems.at[6 * w + j],
                         (*chips[j], c)) for w in range(n) for j in range(3)]
        for cp in first:
            cp.start()
        passed = []
        for j in range(3):
            for w in range(n):
                land = outs[w].at[ks[j], half(w, c)]
                _remote(land, land, send_sems.at[6 * w + j], recv_sems.at[6 * w + j], sibling).wait_recv()
                passed.append(_remote(land, land, send_sems.at[6 * w + 3 + j], recv_sems.at[6 * w + 3 + j], sibling))
                passed[-1].start()
        for j in range(3):
            for w in range(n):
                land = outs[w].at[ks[j], half(w, 1 - c)]
                _remote(land, land, send_sems.at[6 * w + 3 + j], recv_sems.at[6 * w + 3 + j], sibling).wait_recv()
        for cp in first + passed:
            cp.wait_send()

    shapes = [jax.ShapeDtypeStruct((4,) + t.shape, t.dtype) for t in shards]
    return _comm_call(body, shards, shapes, (6 * n, 6 * n), name="gather_halves")


_HBM = pl.BlockSpec(memory_space=pltpu.HBM)
_SEM = pl.BlockSpec(memory_space=pltpu.SEMAPHORE)
_EFFECT = pltpu.SideEffectType.DATAFLOW_SIDE_EFFECTING


def _push_start(blocks, *, scatter, name):
    n = len(blocks)

    def body(*refs):
        xs, lands = refs[:n], refs[n:2 * n]
        send_sems, recv_sems = refs[2 * n], refs[2 * n + 1]
        token = refs[-1]
        x, y, c = _mesh_pos()
        k = 2 * x + y
        chips, ks = _other_chips(x, y)
        for w in range(n):
            for j in range(3):
                src = xs[w].at[ks[j]] if scatter else xs[w]
                _remote(src, lands[w].at[k], send_sems.at[3 * w + j], recv_sems.at[3 * w + j], (*chips[j], c)).start()
        token[...] = jnp.zeros_like(token)

    hbm = lambda shape, dtype: pltpu.with_memory_space_constraint(lax.empty(shape, dtype), pltpu.HBM)
    ins = [pltpu.with_memory_space_constraint(t, pltpu.HBM) for t in blocks]
    ins += [hbm(t.shape if scatter else (4,) + t.shape, t.dtype) for t in blocks]
    out_shape = [pltpu.SemaphoreType.DMA((3 * n,)), pltpu.SemaphoreType.DMA((3 * n,))]
    out_shape += [pltpu.HBM(t.shape, t.dtype) for t in ins]
    out_shape += [jax.ShapeDtypeStruct((8, LANES), F32)]
    res = pl.pallas_call(
        body, name=name, out_shape=out_shape, in_specs=[_HBM] * (2 * n),
        out_specs=[_SEM, _SEM] + [_HBM] * (2 * n) + [pl.BlockSpec(memory_space=pltpu.VMEM)],
        input_output_aliases={i: 2 + i for i in range(2 * n)},
        compiler_params=pltpu.CompilerParams(has_side_effects=_EFFECT),
    )(*ins)
    return res[0], res[1], res[2:2 + n], res[2 + n:2 + 2 * n], res[-1]


def _push_wait(send_sems, recv_sems, blocks, lands, after, *, name):
    n = len(blocks)

    def body(*refs):
        lands_in = refs[n:2 * n]
        send_sems, recv_sems = refs[2 * n], refs[2 * n + 1]
        x, y, c = _mesh_pos()
        chips, ks = _other_chips(x, y)
        for w in range(n):
            for j in range(3):
                slot = lands_in[w].at[ks[j]]
                cp = _remote(slot, slot, send_sems.at[3 * w + j], recv_sems.at[3 * w + j], (*chips[j], c))
                cp.wait_send()
                cp.wait_recv()

    out_shape = [pltpu.HBM(t.shape, t.dtype) for t in list(blocks) + list(lands)]
    res = pl.pallas_call(
        body, name=name, out_shape=out_shape,
        in_specs=[_HBM] * (2 * n) + [_SEM, _SEM, pl.BlockSpec(memory_space=pl.ANY)], out_specs=[_HBM] * (2 * n),
        input_output_aliases={i: i for i in range(2 * n)},
        compiler_params=pltpu.CompilerParams(has_side_effects=_EFFECT),
    )(*blocks, *lands, send_sems, recv_sems, after)
    return res[:n], res[n:]


def _send_half(views, *, name):
    n = len(views)

    def body(*refs):
        vs, outs = refs[:n], refs[n:2 * n]
        send_sems, recv_sems = refs[2 * n:]
        x, y, c = _mesh_pos()
        cps = []
        for w in range(n):
            h = views[w].shape[1] // 2
            cps.append(_remote(vs[w].at[:, pl.ds((1 - c) * h, h), :], outs[w], send_sems.at[w], recv_sems.at[w],
                               (x, y, 1 - c)))
            cps[-1].start()
        for cp in cps:
            cp.wait()

    shapes = [jax.ShapeDtypeStruct((t.shape[0], t.shape[1] // 2, t.shape[2]), t.dtype) for t in views]
    return _comm_call(body, views, shapes, (n, n), name=name)


def _swap_with_sibling(mine, *, name):
    n = len(mine)

    def body(*refs):
        hs, outs = refs[:n], refs[n:2 * n]
        send_sems, recv_sems = refs[2 * n:]
        x, y, c = _mesh_pos()
        cps = [_remote(hs[w], outs[w], send_sems.at[w], recv_sems.at[w], (x, y, 1 - c)) for w in range(n)]
        for cp in cps:
            cp.start()
        for cp in cps:
            cp.wait()

    shapes = [jax.ShapeDtypeStruct(t.shape, t.dtype) for t in mine]
    return _comm_call(body, mine, shapes, (n, n), name=name)


def _gather_all(vec, *, name):
    r, w = vec.shape

    def body(v_ref, out_ref, send_sems, recv_sems):
        x, y, c = _mesh_pos()

        def slot(px, py, pc):
            return out_ref.at[4 * px + 2 * py + pc]

        peers = []
        for rel in range(1, 8):
            fx, fy, fc = (rel >> 2) & 1, (rel >> 1) & 1, rel & 1
            peers.append((x ^ fx, y ^ fy, c ^ fc))
        cps = [_remote(v_ref, slot(x, y, c), send_sems.at[j], recv_sems.at[j], peer) for j, peer in enumerate(peers)]
        for cp in cps:
            cp.start()
        for j, peer in enumerate(peers):
            _remote(slot(*peer), slot(*peer), send_sems.at[j], recv_sems.at[j], peer).wait_recv()
        for cp in cps:
            cp.wait_send()

    others = pl.pallas_call(
        body, name=name, in_specs=[_ANY], out_specs=_ANY,
        out_shape=jax.ShapeDtypeStruct((8, r, w), vec.dtype),
        scratch_shapes=[pltpu.SemaphoreType.DMA((7,)), pltpu.SemaphoreType.DMA((7,))],
    )(vec)
    me = 4 * lax.axis_index("x") + 2 * lax.axis_index("y") + lax.axis_index("c")
    return lax.dynamic_update_index_in_dim(others, vec, me, 0)


def _gather_same_core(vec, *, name):
    r, w = vec.shape

    def body(v_ref, out_ref, send_sems, recv_sems):
        x, y, c = _mesh_pos()
        k = 2 * x + y
        chips, ks = _other_chips(x, y)
        cps = [_remote(v_ref, out_ref.at[k], send_sems.at[j], recv_sems.at[j], (*chips[j], c)) for j in range(3)]
        for cp in cps:
            cp.start()
        for j in range(3):
            slot = out_ref.at[ks[j]]
            _remote(slot, slot, send_sems.at[j], recv_sems.at[j], (*chips[j], c)).wait_recv()
        for cp in cps:
            cp.wait_send()

    others = pl.pallas_call(
        body, name=name, in_specs=[_ANY], out_specs=_ANY,
        out_shape=jax.ShapeDtypeStruct((4, r, w), vec.dtype),
        scratch_shapes=[pltpu.SemaphoreType.DMA((3,)), pltpu.SemaphoreType.DMA((3,))],
    )(vec)
    k_chip = 2 * lax.axis_index("x") + lax.axis_index("y")
    return lax.dynamic_update_index_in_dim(others, vec, k_chip, 0)


def _row_tile(rows, row_bytes):
    for tm in (1024, 512, 256, 128, 64, 32, 16):
        if rows % tm == 0 and tm * row_bytes <= ELEMENTWISE_BLOCK_BYTES:
            return tm
    return 16 if rows % 16 == 0 else rows


def _chip_sum_half(g, got, c, *, name):
    nb, r, w = g.shape
    half = r // 2
    tm = _row_tile(half, w * 4)
    per = half // tm

    def body(c_ref, g_ref, o_ref, out_ref):
        out_ref[...] = (g_ref[...] + o_ref[...]).astype(out_ref.dtype)

    return pl.pallas_call(
        body, name=name,
        grid_spec=pltpu.PrefetchScalarGridSpec(
            num_scalar_prefetch=1, grid=(nb, per),
            in_specs=[pl.BlockSpec((1, tm, w), lambda b, i, c_ref: (b, c_ref[0] * per + i, 0)),
                      pl.BlockSpec((1, tm, w), lambda b, i, c_ref: (b, i, 0))],
            out_specs=pl.BlockSpec((1, tm, w), lambda b, i, c_ref: (b, i, 0))),
        out_shape=jax.ShapeDtypeStruct((nb, half, w), BF16),
        compiler_params=_cparams(("parallel", "parallel")),
    )(jnp.reshape(c, (1,)).astype(jnp.int32), g, got)


def _sum_slots(stack, *, name):
    n, r, w = stack.shape
    tm = _row_tile(r, n * w * stack.dtype.itemsize)

    def body(s_ref, out_ref):
        acc = s_ref[0].astype(F32)
        for i in range(1, n):
            acc = acc + s_ref[i].astype(F32)
        out_ref[...] = acc

    return pl.pallas_call(
        body, name=name, grid=(r // tm,),
        in_specs=[pl.BlockSpec((n, tm, w), lambda i: (0, i, 0))],
        out_specs=pl.BlockSpec((tm, w), lambda i: (i, 0)),
        out_shape=jax.ShapeDtypeStruct((r, w), F32),
        compiler_params=_cparams(("parallel",)),
    )(stack)


def _adam_math(wv, gv, mv, vv):
    m_new = ADAM_B1 * mv + (1.0 - ADAM_B1) * gv
    v_new = ADAM_B2 * vv + (1.0 - ADAM_B2) * (gv * gv)
    m_hat = m_new / (1.0 - ADAM_B1 ** ADAM_STEP)
    v_hat = v_new / (1.0 - ADAM_B2 ** ADAM_STEP)
    delta = -ADAM_LR * (m_hat / (jnp.sqrt(v_hat) + ADAM_EPS) + ADAM_WD * wv)
    return delta, m_new, v_new


def _adamw(w, g, m, v, *, name):
    shape = w.shape
    cols = shape[-1]
    flat = lambda t: t.reshape(-1, cols)
    rows = flat(w).shape[0]
    tm = _pick(rows, (256, 128, 64, 32, 16, 8))
    outs = _rowwise(_adam_math, [flat(w), flat(g), flat(m), flat(v)], [], [(cols, F32, "row")] * 3, name=name, tm=tm)
    return tuple(o.reshape(shape) for o in outs)


def _adamw_slots(w, slots, m, v, c, *, name):
    shape = w.shape
    cols = shape[-1]
    half = slots[0][0].shape[1]
    v4 = lambda t: t.reshape(2, 2, half, cols)
    assert all(s.shape == (4, half, cols) for pair in slots for s in pair) and w.size == 4 * half * cols
    tm = _row_tile(half, cols * 4 * 4)

    def body(c_ref, w_ref, m0_ref, o0_ref, m1_ref, o1_ref, m_ref, v_ref, g_ref, d_ref, mo_ref, vo_ref):
        first = pl.program_id(0) == 0
        own = pl.program_id(1) == c_ref[0]
        g = None
        for i in range(4):
            part = jnp.where(first, jnp.where(own, m0_ref[i], o0_ref[i]), jnp.where(own, m1_ref[i], o1_ref[i]))
            g = part.astype(F32) if g is None else g + part.astype(F32)
        delta, m_new, v_new = _adam_math(w_ref[0, 0], g, m_ref[0, 0], v_ref[0, 0])
        g_ref[0, 0], d_ref[0, 0], mo_ref[0, 0], vo_ref[0, 0] = g, delta, m_new, v_new

    blk = pl.BlockSpec((1, 1, tm, cols), lambda l, hf, i, c_ref: (l, hf, i, 0))

    def slot_spec(layer, mine):
        def index(l, hf, i, c_ref):
            same_half = hf * c_ref[0] + (1 - hf) * (1 - c_ref[0])
            use = (l if layer else 1 - l) * (same_half if mine else 1 - same_half)
            return (0, i * use, 0)
        return pl.BlockSpec((4, tm, cols), index)

    outs = pl.pallas_call(
        body, name=name,
        grid_spec=pltpu.PrefetchScalarGridSpec(
            num_scalar_prefetch=1, grid=(2, 2, half // tm),
            in_specs=[blk, slot_spec(0, True), slot_spec(0, False), slot_spec(1, True), slot_spec(1, False), blk, blk],
            out_specs=[blk] * 4),
        out_shape=[jax.ShapeDtypeStruct((2, 2, half, cols), F32)] * 4,
        compiler_params=_cparams(("arbitrary", "arbitrary", "arbitrary")),
    )(jnp.reshape(c, (1,)).astype(jnp.int32), v4(w), slots[0][0], slots[0][1], slots[1][0], slots[1][1], v4(m), v4(v))
    return tuple(o.reshape(shape) for o in outs)


def _pad_blocks(w, axis, n_blocks, real, to=LANES, offset=0):
    axis = axis % w.ndim
    shp = w.shape
    w = w.reshape(shp[:axis] + (n_blocks, real) + shp[axis + 1:])
    pads = [(0, 0)] * w.ndim
    pads[axis + 1] = (offset, to - real - offset)
    w = jnp.pad(w, pads)
    return w.reshape(shp[:axis] + (n_blocks * to,) + shp[axis + 1:])


def _unpad_blocks(w, axis, n_blocks, real, to=LANES, offset=0):
    axis = axis % w.ndim
    shp = w.shape
    w = w.reshape(shp[:axis] + (n_blocks, to) + shp[axis + 1:])
    w = lax.slice_in_dim(w, offset, offset + real, axis=axis + 1)
    return w.reshape(shp[:axis] + (n_blocks * real,) + shp[axis + 1:])


def _block_diag(w):
    n, a, b = w.shape
    eye = jnp.eye(n, dtype=w.dtype)
    return (eye[:, None, :, None] * w[:, :, None, :]).reshape(n * a, n * b)


def _block_diag_t(d, n):
    a, b = d.shape[0] // n, d.shape[1] // n
    d = d.reshape(n, a, n, b)
    return jnp.stack([d[i, :, i, :] for i in range(n)])


_SPLITS = np.cumsum((0,) + SPLIT_SIZES)


def _w_in_groups(w_in):
    sl = lambda i: w_in[:, _SPLITS[i]:_SPLITS[i + 1]]
    xbc = sl(5)
    xbc_pad = jnp.concatenate([_pad_blocks(xbc[:, :MIX], 1, N_HEADS, HEAD),
                               _pad_blocks(xbc[:, MIX:MIX + 2 * HEAD], 1, 2, HEAD),
                               _pad_blocks(xbc[:, MIX + 2 * HEAD:], 1, 2, HEAD)], axis=1)
    return dict(
        cq=sl(0), ckv=sl(1), kr=_pad_blocks(sl(2), 1, 1, QK_ROPE, offset=HEAD), pool=sl(3),
        z=_pad_blocks(sl(4), 1, N_HEADS, HEAD), xbc=xbc_pad, dt=_pad_blocks(sl(6), 1, 1, N_HEADS),
        lru_g=sl(7), lru_x=sl(8), gates=sl(9))


def _w_in_fused(groups):
    parts, at = [], 0
    for name, off, width in IN_LAYOUT:
        assert groups[name].shape[1] == width and off >= at
        if off > at:
            parts.append(jnp.zeros((groups[name].shape[0], off - at), groups[name].dtype))
        parts.append(groups[name])
        at = off + width
    parts.append(jnp.zeros((parts[0].shape[0], IN_ALL_COLS - at), parts[0].dtype))
    return jnp.concatenate(parts, axis=1)


def _in_cols(arr, name):
    off, width = IN_OFFSETS[name]
    return _Cols(arr, off, width)


def _w_in_ungroup(d):
    xbc = d["xbc"]
    w = N_HEADS * LANES
    xbc_real = jnp.concatenate([_unpad_blocks(xbc[:, :w], 1, N_HEADS, HEAD),
                                _unpad_blocks(xbc[:, w:w + 2 * LANES], 1, 2, HEAD),
                                _unpad_blocks(xbc[:, w + 2 * LANES:], 1, 2, HEAD)], axis=1)
    return jnp.concatenate([d["cq"], d["ckv"], _unpad_blocks(d["kr"], 1, 1, QK_ROPE, offset=HEAD), d["pool"],
                            _unpad_blocks(d["z"], 1, N_HEADS, HEAD), xbc_real, _unpad_blocks(d["dt"], 1, 1, N_HEADS),
                            d["lru_g"], d["lru_x"], d["gates"]], axis=1)


def _pad_xbc_vec(v):
    return jnp.concatenate([_pad_blocks(v[..., :MIX], -1, N_HEADS, HEAD),
                            _pad_blocks(v[..., MIX:MIX + 2 * HEAD], -1, 2, HEAD),
                            _pad_blocks(v[..., MIX + 2 * HEAD:], -1, 2, HEAD)], axis=-1)


def _unpad_xbc_vec(v):
    w = N_HEADS * LANES
    return jnp.concatenate([_unpad_blocks(v[..., :w], -1, N_HEADS, HEAD),
                            _unpad_blocks(v[..., w:w + 2 * LANES], -1, 2, HEAD),
                            _unpad_blocks(v[..., w + 2 * LANES:], -1, 2, HEAD)], axis=-1)


def _layer_weights(p):
    q = dict(p)
    q["in_all"] = _w_in_fused(_w_in_groups(p["w_in"]))
    q["uq"] = _pad_blocks(p["w_uq"], 1, N_HEADS, HEAD + QK_ROPE)
    ukv = p["w_ukv"].reshape(KV_LORA, N_HEADS, 2 * HEAD)
    q["ukv"] = jnp.concatenate([_pad_blocks(ukv[:, :, :HEAD].reshape(KV_LORA, -1), 1, N_HEADS, HEAD),
                                _pad_blocks(ukv[:, :, HEAD:].reshape(KV_LORA, -1), 1, N_HEADS, HEAD)], axis=1)
    q["pool_bd"] = _block_diag(p["w_pool"])
    q["lru_bd"] = jnp.concatenate([_block_diag(p["lru_w_a"]), _block_diag(p["lru_w_i"])], axis=1)
    q["br"] = [_pad_blocks(p["w_branch"][0], 0, N_HEADS, HEAD), p["w_branch"][1],
               _pad_blocks(p["w_branch"][2], 0, N_HEADS, HEAD), p["w_branch"][3]]
    q["ssd_conv_w_pad"] = _pad_xbc_vec(p["ssd_conv_w"])
    q["ssd_conv_b_pad"] = _pad_xbc_vec(p["ssd_conv_b"])[None, :]
    q["ssd_norm_pad"] = _pad_blocks(p["ssd_norm"], 0, N_HEADS, HEAD)[None, :]
    return q


def _row(v):
    return v.reshape(1, -1)


def _scal3(v):
    return v.reshape(N_HEADS, 1, 1)


def _layer_fwd(x, p_emb, w, rope, tag):
    n = lambda s: f"{s}_{tag}"
    sv = {"x": x}
    h = _rms_fwd(x, _row(w["g_mix"]), name=n("rms_mix"))
    sv["h"] = h
    u_all = _mm(h, w["in_all"], name=n("in_proj"))
    u = {k: _in_cols(u_all, k) for k in IN_OFFSETS}
    sv["u"] = u

    cqn = _rms_fwd(u["cq"], _row(w["q_norm"]), name=n("rms_q"))
    ckvn = _rms_fwd(u["ckv"], _row(w["kv_norm"]), name=n("rms_kv"))
    q_pad = _mm(cqn, w["uq"], name=n("uq"))
    kv2 = _mm(ckvn, w["ukv"], name=n("ukv"))
    qc, kc, vc = _att_prep(q_pad, kv2, u["kr"], *rope, name=n("att_prep"))
    y_a, lse = _flash_fwd(qc, kc, vc, name=n("flash_fwd"))
    sv.update(cqn=cqn, ckvn=ckvn, qc=qc, kc=kc, vc=vc, y_a=y_a, lse=lse)

    pool_d = _pool_fwd(u["pool"], name=n("pool_fwd"))
    yb_pre, y_b = _mm(pool_d, w["pool_bd"], epilogue=lambda acc, sc: (acc, acc * sc),
                      rowvecs=[_row(w["pool_scale"])], out_dtypes=(F32, BF16), name=n("pool_mm"))
    sv.update(pool_d=pool_d, yb_pre=yb_pre, y_b=y_b)

    xbc_c = _conv_fwd(u["xbc"], w["ssd_conv_w_pad"], w["ssd_conv_b_pad"], silu=True, name=n("ssd_conv"))
    dt8 = lax.slice_in_dim(u_all, IN_OFFSETS["dt"][0], IN_OFFSETS["dt"][0] + N_HEADS, axis=1)
    dtcol = dt8.T[:, :, None]
    dtrow = dt8.T[:, None, :]
    ssd_par = (_scal3(w["ssd_dt_bias"]), _scal3(w["ssd_a_log"]), _scal3(w["ssd_d"]))
    y_ssd, states = _ssd_fwd(xbc_c, dtcol, dtrow, *ssd_par, name=n("ssd_fwd"))

    def ssd_post(yv, zv, gv):
        xh, _ = _rms_parts(yv * _silu(zv), MIX)
        return xh * gv

    y_c = _rowwise(ssd_post, [y_ssd, u["z"]], [w["ssd_norm_pad"]], [(N_HEADS * LANES, BF16, "row")], name=n("ssd_post"))
    sv.update(xbc_c=xbc_c, dtcol=dtcol, dtrow=dtrow, y_ssd=y_ssd, states=states, y_c=y_c)

    xc = _conv_fwd(u["lru_x"], w["lru_conv_w"], _row(w["lru_conv_b"]), silu=False, name=n("lru_conv"))
    pre = _mm(xc, w["lru_bd"], name=n("lru_mm"))
    lru_par = (_row(w["lru_lambda"]), _row(w["lru_b_a"]), _row(w["lru_b_i"]))
    y_d, h_lru = _lru_fwd(pre, xc, u["lru_g"], *lru_par, name=n("lru_fwd"))
    sv.update(xc=xc, pre=pre, h_lru=h_lru, y_d=y_d)

    merged, *ybs = _branch_merge([y_a, y_b, y_c, y_d], w["br"], u_all, name=n("branch_merge"))
    x1 = _mm(merged, w["w_out"], epilogue=lambda acc, xr: (acc + xr,), tiles=[x], name=n("out_proj"))
    sv.update(ybs=ybs, merged=merged, x1=x1)

    h2 = _rms_fwd(x1, _row(w["g_mlp"]), name=n("rms_mlp"))
    a_ff, f_ff = _mm(h2, w["w_ff1"], epilogue=lambda acc: (acc, jnp.square(jnp.maximum(acc, 0.0))),
                     out_dtypes=(BF16, BF16), name=n("ff1"))
    x2 = _mm(f_ff, w["w_ff2"], epilogue=lambda acc, xr: (acc + xr,), tiles=[x1], name=n("ff2"))
    sv.update(h2=h2, a_ff=a_ff, f_ff=f_ff, x2=x2)

    h3 = _rms_fwd(x2, _row(w["g_ple"]), name=n("rms_ple"))
    e_ple = _mm(p_emb, w["w_ple"], name=n("ple_emb"))
    x3, gt_ple = _mm(h3, w["w_ple_gate"], epilogue=lambda acc, ev, xr: (xr + ev * _sigmoid(acc), _sigmoid(acc)),
                     tiles=[e_ple, x2], out_dtypes=(F32, F32), name=n("ple_gate"))
    sv.update(h3=h3, e_ple=e_ple, gt_ple=gt_ple, p_emb=p_emb)
    return x3, sv


def _layer_bwd(dx3, sv, w, rope, tag, early=None, mid=None):
    n = lambda s: f"{s}_{tag}"
    gr = {}
    u = sv["u"]

    de, dpre = _rowwise(lambda d, gt, ev: (d * gt, d * ev * gt * (1.0 - gt)), [dx3, sv["gt_ple"], sv["e_ple"]], [],
                        [(D_MODEL, BF16, "row"), (D_MODEL, BF16, "row")], name=n("ple_bwd"))
    gr["w_ple"] = _mm(sv["p_emb"], de, ta=True, name=n("d_w_ple"))
    gr["w_ple_gate"] = _mm(sv["h3"], dpre, ta=True, name=n("d_w_ple_gate"))
    dh3 = _mm(dpre, w["w_ple_gate"], tb=True, out_dtypes=(BF16,), name=n("d_h3"))
    dx2, dg = _rms_bwd(sv["x2"], _row(w["g_ple"]), dh3, dx3, name=n("rms_ple_bwd"))
    gr["g_ple"] = dg[0]

    gr["w_ff2"] = _mm(sv["f_ff"], dx2, ta=True, name=n("d_w_ff2"))
    da = _mm(dx2, w["w_ff2"], tb=True, epilogue=lambda acc, av: (acc * 2.0 * jnp.maximum(av, 0.0),),
             tiles=[sv["a_ff"]], out_dtypes=(BF16,), name=n("d_a_ff"))
    gr["w_ff1"] = _mm(sv["h2"], da, ta=True, name=n("d_w_ff1"))
    dh2 = _mm(da, w["w_ff1"], tb=True, out_dtypes=(BF16,), name=n("d_h2"))
    dx1, dg = _rms_bwd(sv["x1"], _row(w["g_mlp"]), dh2, dx2, name=n("rms_mlp_bwd"))
    gr["g_mlp"] = dg[0]
    if early is not None:
        dx1 = early(gr, dx1)

    gr["w_out"] = _mm(sv["merged"], dx1, ta=True, name=n("d_w_out"))
    dmerged = _mm(dx1, w["w_out"], tb=True, name=n("d_merged"))

    def merge_bwd(dm, gts, y0, y1, y2, y3):
        dys, dgs = [], []
        for b, yb in enumerate((y0, y1, y2, y3)):
            sg = _sigmoid(gts[:, b * D_MODEL:(b + 1) * D_MODEL])
            dys.append(dm * sg)
            dgs.append(dm * yb * sg * (1.0 - sg))
        return (*dys, jnp.concatenate(dgs, axis=1))

    *dybs, dgates = _rowwise(merge_bwd, [dmerged, u["gates"]] + sv["ybs"], [],
                             [(D_MODEL, BF16, "row")] * 4 + [(4 * D_MODEL, BF16, "row")], name=n("merge_bwd"))
    ys = [sv["y_a"], sv["y_b"], sv["y_c"], sv["y_d"]]
    dwb = [_mm(ys[b], dybs[b], ta=True, name=n(f"d_w_branch{b}")) for b in range(4)]
    gr["w_branch"] = jnp.stack([_unpad_blocks(dwb[0], 0, N_HEADS, HEAD), dwb[1],
                                _unpad_blocks(dwb[2], 0, N_HEADS, HEAD), dwb[3]])
    dy_a = _mm(dybs[0], w["br"][0], tb=True, out_dtypes=(BF16,), name=n("d_y_a"))
    dy_b = _mm(dybs[1], w["br"][1], tb=True, name=n("d_y_b"))
    dy_c = _mm(dybs[2], w["br"][2], tb=True, name=n("d_y_c"))
    dy_d = _mm(dybs[3], w["br"][3], tb=True, name=n("d_y_d"))
    du = {"gates": dgates}

    lru_par = (_row(w["lru_lambda"]), _row(w["lru_b_a"]), _row(w["lru_b_i"]))
    dpa, dpi, dxc_direct, du["lru_g"], dlam, dba, dbi = _lru_bwd(
        sv["pre"], sv["xc"], u["lru_g"], *lru_par, sv["h_lru"], dy_d, name=n("lru_bwd"))
    dpre_lru = jnp.concatenate([dpa, dpi], axis=1)
    d_bd = _mm(sv["xc"], dpre_lru, ta=True, name=n("d_lru_w"))
    gr["lru_w_a"] = _block_diag_t(d_bd[:, :MIX], N_HEADS)
    gr["lru_w_i"] = _block_diag_t(d_bd[:, MIX:], N_HEADS)
    gr["lru_lambda"], gr["lru_b_a"], gr["lru_b_i"] = dlam[0], dba[0], dbi[0]
    dxc = _mm(dpre_lru, w["lru_bd"], tb=True, epilogue=lambda acc, t: (acc + t,), tiles=[dxc_direct], name=n("d_xc"))
    du["lru_x"], gr["lru_conv_w"], dcb = _conv_bwd(u["lru_x"], w["lru_conv_w"], _row(w["lru_conv_b"]), dxc,
                                                  silu=False, name=n("lru_conv_bwd"))
    gr["lru_conv_b"] = dcb[0]

    def ssd_post_bwd(dyc, yv, zv, gv):
        sz = _silu(zv)
        dyz, dgain = _rms_bwd_math(yv * sz, gv, dyc, MIX)
        return dyz * sz, dyz * yv * _silu_grad(zv), dgain

    dy_ssd, du["z"], dgain = _rowwise(ssd_post_bwd, [dy_c, sv["y_ssd"], u["z"]], [w["ssd_norm_pad"]],
                                      [(N_HEADS * LANES, F32, "row"), (N_HEADS * LANES, BF16, "row"),
                                       (N_HEADS * LANES, F32, "acc")], name=n("ssd_post_bwd"))
    gr["ssd_norm"] = _unpad_blocks(dgain[0], 0, N_HEADS, HEAD)
    ssd_par = (_scal3(w["ssd_dt_bias"]), _scal3(w["ssd_a_log"]), _scal3(w["ssd_d"]))
    dxs, dbg, dcg, ddt, dbias, dalog, dd = _ssd_bwd(sv["xbc_c"], sv["dtcol"], sv["dtrow"], *ssd_par, sv["states"],
                                                    dy_ssd, name=n("ssd_bwd"))
    s = dxs.shape[0]
    dxbc_c = jnp.concatenate([dxs, dbg, dcg], axis=1)
    gr["ssd_dt_bias"], gr["ssd_a_log"], gr["ssd_d"] = dbias[:, 0, 0], dalog[:, 0, 0], dd[:, 0, 0]
    du["xbc"], dcw, dcb = _conv_bwd(u["xbc"], w["ssd_conv_w_pad"], w["ssd_conv_b_pad"], dxbc_c, silu=True,
                                    name=n("ssd_conv_bwd"))
    gr["ssd_conv_w"], gr["ssd_conv_b"] = _unpad_xbc_vec(dcw), _unpad_xbc_vec(dcb[0])
    du["dt"] = jnp.pad(ddt[:, 0, :].T, ((0, 0), (0, LANES - N_HEADS)))

    dyb_pre, dscale = _rowwise(lambda d, yp, sc: (d * sc, _colsum(d * yp)), [dy_b, sv["yb_pre"]],
                               [_row(w["pool_scale"])], [(MIX, BF16, "row"), (MIX, F32, "acc")], name=n("pool_scale_bwd"))
    gr["pool_scale"] = dscale[0]
    gr["w_pool"] = _block_diag_t(_mm(sv["pool_d"], dyb_pre, ta=True, name=n("d_w_pool")), 4)
    dd_pool = _mm(dyb_pre, w["pool_bd"], tb=True, name=n("d_pool_d"))
    du["pool"] = _pool_bwd(dd_pool, name=n("pool_bwd"))

    delta = _att_delta(sv["y_a"], dy_a, name=n("att_delta"))
    dqc, dkc, dvc = _flash_bwd(sv["qc"], sv["kc"], sv["vc"], dy_a, sv["lse"], delta, name=n("flash_bwd"))
    dq_pad, du["kr"] = _att_prep_bwd(dqc, dkc, *rope, name=n("att_prep_bwd"))
    d_uq = _mm(sv["cqn"], dq_pad, ta=True, name=n("d_w_uq"))
    gr["w_uq"] = _unpad_blocks(d_uq, 1, N_HEADS, HEAD + QK_ROPE)
    dcqn = _mm(dq_pad, w["uq"], tb=True, out_dtypes=(BF16,), name=n("d_cqn"))
    du["cq"], dg = _rms_bwd(u["cq"], _row(w["q_norm"]), dcqn, name=n("rms_q_bwd"))
    gr["q_norm"] = dg[0]
    dkv2 = jnp.concatenate([dkc, dvc], axis=1).astype(BF16)
    d_ukv = _mm(sv["ckvn"], dkv2, ta=True, name=n("d_w_ukv"))
    wk = N_HEADS * LANES
    dk_real = _unpad_blocks(d_ukv[:, :wk], 1, N_HEADS, HEAD).reshape(KV_LORA, N_HEADS, HEAD)
    dv_real = _unpad_blocks(d_ukv[:, wk:], 1, N_HEADS, HEAD).reshape(KV_LORA, N_HEADS, HEAD)
    gr["w_ukv"] = jnp.concatenate([dk_real, dv_real], axis=2).reshape(KV_LORA, N_HEADS * 2 * HEAD)
    dckvn = _mm(dkv2, w["ukv"], tb=True, out_dtypes=(BF16,), name=n("d_ckvn"))
    du["ckv"], dg = _rms_bwd(u["ckv"], _row(w["kv_norm"]), dckvn, name=n("rms_kv_bwd"))
    gr["kv_norm"] = dg[0]

    if mid is not None:
        du["dt"] = du["dt"] + mid(gr)
    du_all = _w_in_fused({k: v.astype(BF16) for k, v in du.items()})
    dw_all = _mm(sv["h"], du_all, ta=True, name=n("d_w_in"))
    gr["w_in"] = _w_in_ungroup({k: dw_all[:, off:off + width] for k, off, width in IN_LAYOUT})
    dh = _mm(du_all, w["in_all"], tb=True, name=n("d_h"))
    dx, dg = _rms_bwd(sv["x"], _row(w["g_mix"]), dh, dx1, name=n("rms_mix_bwd"))
    gr["g_mix"] = dg[0]
    return dx, gr


def _pack_rows(n_elems):
    per = PACK_W * PACK_ROWS
    return -(-n_elems // per) * PACK_ROWS


def _pack_flat(parts, dtype):
    flat = jnp.concatenate([p.reshape(-1).astype(dtype) for p in parts])
    rows = _pack_rows(flat.shape[0])
    return jnp.pad(flat, (0, rows * PACK_W - flat.shape[0])).reshape(rows, PACK_W)


def _unpack_flat(buf, shapes):
    lead = buf.shape[:-2]
    flat = buf.reshape(lead + (-1,))
    out, off = [], 0
    for shp in shapes:
        size = int(np.prod(shp))
        out.append(flat[..., off:off + size].reshape(lead + tuple(shp)))
        off += size
    return out


def _merge_shards(t, axis):
    return jnp.concatenate([t[i] for i in range(4)], axis=axis)


def _split_shards(t, axis):
    return jnp.stack(jnp.split(t, 4, axis=axis))


def _rope_tables(positions):
    inv = 1.0 / (ROPE_THETA ** (jnp.arange(0, QK_ROPE, 2, dtype=F32) / QK_ROPE))
    ang = positions.astype(F32)[:, None] * inv
    cos, sin = jnp.cos(ang), jnp.sin(ang)
    s = ang.shape[0]
    half = QK_ROPE // 2
    z = lambda n_: jnp.zeros((s, n_), F32)
    cos_t = jnp.concatenate([jnp.ones((s, HEAD), F32), cos, cos, jnp.ones((s, LANES - HEAD - QK_ROPE), F32)], axis=1)
    sin_p = jnp.concatenate([z(HEAD + half), sin, z(LANES - HEAD - QK_ROPE)], axis=1)
    sin_m = jnp.concatenate([z(HEAD), -sin, z(half + LANES - HEAD - QK_ROPE)], axis=1)
    return cos_t, sin_p, sin_m


def _loss_head(x, g, target, *, name):
    d = x.shape[1]

    def fn(xv, tv, gv):
        xh, r = _rms_parts(xv, d)
        y = xh * gv
        err = y - tv
        dy = err * (1.0 / d)
        dxh = dy * gv
        dx = r * (dxh - xh * (jnp.sum(dxh * xh, axis=-1, keepdims=True) * (1.0 / d)))
        return dx, _colsum(dy * xh), _colsum(err * err) * (0.5 / d)

    return _rowwise(fn, [x, target], [g], [(d, F32, "row"), (d, F32, "acc"), (d, F32, "acc")], name=name)


MATS = tuple((nm, ax) for nm, ax in BIG if nm not in CONV_SHARDED)
EARLY_L0 = ("w_ple", "w_ple_gate", "w_ff2", "w_ff1")


def _grad_view(g, ax_layer):
    if ax_layer == 0:
        return g.reshape(4, g.shape[0] // 4, g.shape[1])
    return g.reshape(1, -1, g.shape[-1])


def _reduce_start(grads_l, mats, c_idx, tag):
    views = [_grad_view(grads_l[nm], ax - 1) for nm, ax in mats]
    got = _send_half(views, name="send_half_" + tag)
    parts = []
    for (nm, ax), v, gt in zip(mats, views, got):
        both = _chip_sum_half(v, gt, c_idx, name=f"chip_sum_{nm}_{tag}")
        parts.append(both if ax == 1 else _split_shards(both[0], 1))
    return _push_start(parts, scatter=True, name="push_grads_" + tag)


def _reduce_finish(state, after, k_chip, tag):
    send_sems, recv_sems, parts, lands, _ = state
    parts, landed = _push_wait(send_sems, recv_sems, parts, lands, after, name="wait_grads_" + tag)
    mine = [lax.dynamic_update_index_in_dim(t, lax.dynamic_index_in_dim(p, k_chip, 0, keepdims=False), k_chip, 0)
            for t, p in zip(landed, parts)]
    return list(zip(mine, _swap_with_sibling(mine, name="swap_halves_" + tag)))


def _step(args):
    x = args["x"][0]
    c_idx = lax.axis_index("c")
    k_chip = 2 * lax.axis_index("x") + lax.axis_index("y")

    mats = MATS
    mine = [[args[nm][l].astype(BF16) for nm, _ in mats] for l in range(2)]
    gathered0 = _gather_halves(mine[0])
    convs = [(nm, ax) for nm, ax in BIG if nm in CONV_SHARDED]
    conv_all = _gather_all(_pack_flat([args[nm] for nm, _ in convs], F32), name="gather_conv_taps")[0::2]
    mine1, gathered0, conv_all = lax.optimization_barrier((mine[1], gathered0, conv_all))
    gathered0 = [lax.dynamic_update_index_in_dim(t, own, k_chip, 0) for t, own in zip(gathered0, mine[0])]
    push1 = _push_start(mine1, scatter=False, name="push_weights_l1")
    full_conv = {nm: _merge_shards(t, ax)
                 for (nm, ax), t in zip(convs, _unpack_flat(conv_all, [args[nm].shape for nm, _ in convs]))}
    rope = _rope_tables(args["positions"][0])

    def layer_weights(l, gathered):
        p = {nm: _merge_shards(t, ax - 1) for (nm, ax), t in zip(mats, gathered)}
        p.update({nm: full_conv[nm][l] for nm in CONV_SHARDED})
        p.update({nm: args[nm][l] for nm in SMALL if nm != "g_final"})
        return _layer_weights(p)

    layers = [layer_weights(0, gathered0), None]
    layers[0]["g_mix"] = layers[0]["g_mix"] + push1[4][0, 0]
    x, sv0 = _layer_fwd(x, args["p"][0, 0], layers[0], rope, "l0")
    own1, landed1 = _push_wait(push1[0], push1[1], push1[2], push1[3], x, name="wait_weights_l1")
    layers[1] = layer_weights(1, [lax.dynamic_update_index_in_dim(t, own, k_chip, 0) for t, own in zip(landed1, own1)])
    x, sv1 = _layer_fwd(x, args["p"][1, 0], layers[1], rope, "l1")
    saved = [sv0, sv1]

    dx, dg_final, loss_part = _loss_head(x, _row(args["g_final"]), args["loss_target"][0], name="loss_head")
    loss = lax.psum(jnp.sum(loss_part), ("x", "y", "c"))

    early_mats = tuple(mt for mt in MATS if mt[0] in EARLY_L0)
    rest_mats = tuple(mt for mt in MATS if mt[0] == "w_in")
    mid_mats = tuple(mt for mt in MATS if mt not in early_mats + rest_mats)
    grads, reduce0_early, reduce0_mid = [None, None], [], []

    def early0(gr, dx1):
        reduce0_early.append(_reduce_start(gr, early_mats, c_idx, "l0e"))
        return dx1 + reduce0_early[0][4][0, 0]

    def mid0(gr):
        reduce0_mid.append(_reduce_start(gr, mid_mats, c_idx, "l0m"))
        return reduce0_mid[0][4][0, 0]

    dx, grads[1] = _layer_bwd(dx, saved[1], layers[1], rope, "l1")
    reduce1 = _reduce_start(grads[1], MATS, c_idx, "l1")
    dx, grads[0] = _layer_bwd(dx + reduce1[4][0, 0], saved[0], layers[0], rope, "l0", early=early0, mid=mid0)
    g_all = {nm: jnp.stack([grads[0][nm], grads[1][nm]]) for nm in SMALL + CONV_SHARDED if nm != "g_final"}
    g_all["g_final"] = dg_final[0]
    all_names = SMALL + CONV_SHARDED
    all_shapes = [g_all[nm].shape for nm in all_names]
    packed = _pack_flat([g_all[nm] for nm in all_names], F32)
    sibling = _swap_with_sibling([packed], name="swap_small_grads")[0]
    pair = jnp.where(c_idx == 0, jnp.stack([packed, sibling]), jnp.stack([sibling, packed]))
    small_all = _gather_same_core(_sum_slots(pair, name="sum_cores"), name="gather_small_grads")
    g0_mats, small_all = lax.optimization_barrier(({nm: grads[0][nm] for nm, _ in rest_mats}, small_all))
    reduce0 = _reduce_start(g0_mats, rest_mats, c_idx, "l0")
    small_sum = _sum_slots(small_all, name="sum_chips")
    g_red = dict(zip(all_names, _unpack_flat(small_sum, all_shapes)))
    for nm, ax in BIG:
        if nm in CONV_SHARDED:
            width = args[nm].shape[ax]
            g_red[nm] = lax.dynamic_slice_in_dim(g_red[nm], k_chip * width, width, axis=ax)
    small_shapes = [args[nm].shape for nm in SMALL]
    pack_small = lambda src: _pack_flat([src(nm) for nm in SMALL], F32)
    upd_small = _adamw(pack_small(lambda nm: args[nm]), pack_small(lambda nm: g_red[nm]),
                       pack_small(lambda nm: args["m_" + nm]), pack_small(lambda nm: args["v_" + nm]), name="adamw_small")
    upd = {nm: trip for nm, trip in zip(SMALL, zip(*[_unpack_flat(t, small_shapes) for t in upd_small]))}
    for nm in CONV_SHARDED:
        upd[nm] = _adamw(args[nm], g_red[nm], args["m_" + nm], args["v_" + nm], name="adamw_" + nm)

    names = lambda mats_: [nm for nm, _ in mats_]
    slots0 = dict(zip(names(early_mats), _reduce_finish(reduce0_early[0], dx, k_chip, "l0e")))
    slots0.update(zip(names(mid_mats), _reduce_finish(reduce0_mid[0], dx, k_chip, "l0m")))
    slots0.update(zip(names(rest_mats), _reduce_finish(reduce0, upd_small[0], k_chip, "l0")))
    slots1 = dict(zip(names(MATS), _reduce_finish(reduce1, dx, k_chip, "l1")))
    for nm, _ in MATS:
        g_red[nm], *upd[nm] = _adamw_slots(args[nm], [slots0[nm], slots1[nm]], args["m_" + nm], args["v_" + nm],
                                           c_idx, name="adamw_" + nm)

    outs = [loss, dx[None]]
    outs += [g_red[nm] for nm in WEIGHTS]
    for i in range(3):
        outs += [upd[nm][i] for nm in WEIGHTS]
    return tuple(outs)


_ARG_NAMES = ("x", "p", "positions") + WEIGHTS + ("loss_target",) + tuple("m_" + nm for nm in WEIGHTS) \
    + tuple("v_" + nm for nm in WEIGHTS)


def kernel(*arrays):
    assert len(arrays) == len(_ARG_NAMES), len(arrays)
    return _step(dict(zip(_ARG_NAMES, arrays)))
```

```python
import math

import jax
import jax.numpy as jnp
import numpy as np
from jax import lax
from jax.experimental import pallas as pl
from jax.experimental.pallas import tpu as pltpu

F32 = jnp.float32
BF16 = jnp.bfloat16
MXU_DTYPE = BF16
LANES = 128
VMEM_LIMIT = 56 * 1024 * 1024
MM_VMEM_BUDGET = 36 * 1024 * 1024
ELEMENTWISE_BLOCK_BYTES = 2 * 1024 * 1024

D_MODEL = 1024
N_HEADS = 8
HEAD = 64
QK_ROPE = 32
Q_LORA = 384
KV_LORA = 256
MIX = 512
SSD_CHUNK = 128
CONV_W = 4
POOL_WINDOWS = (2, 4, 8, 16)
LRU_C = 8.0
EPS = 1e-6
ROPE_THETA = 10000.0
ATT_SCALE = (HEAD + QK_ROPE) ** -0.5
SPLIT_SIZES = (Q_LORA, KV_LORA, QK_ROPE, MIX, MIX, 768, N_HEADS, MIX, MIX, 4 * D_MODEL)
IN_LAYOUT = (("gates", 0, 4096), ("z", 4096, 1024), ("pool", 5120, 512), ("lru_g", 5632, 512), ("lru_x", 6144, 512),
             ("cq", 6912, 384), ("ckv", 7424, 256), ("xbc", 7680, 1536), ("kr", 9216, 128), ("dt", 9344, 128))
IN_OFFSETS = {name: (off, width) for name, off, width in IN_LAYOUT}
IN_ALL_COLS = 9728

ADAM_LR, ADAM_B1, ADAM_B2, ADAM_EPS, ADAM_WD, ADAM_STEP = 0.001, 0.9, 0.999, 1e-08, 0.01, 10

BIG = (("w_in", 2), ("w_uq", 2), ("w_ukv", 2), ("ssd_conv_w", 2), ("lru_conv_w", 2), ("w_branch", 3),
       ("w_out", 1), ("w_ff1", 2), ("w_ff2", 1), ("w_ple_gate", 1), ("w_ple", 2))
SMALL = ("g_mix", "q_norm", "kv_norm", "w_pool", "pool_scale", "ssd_conv_b", "ssd_dt_bias", "ssd_a_log",
         "ssd_d", "ssd_norm", "lru_conv_b", "lru_w_a", "lru_b_a", "lru_w_i", "lru_b_i", "lru_lambda",
         "g_mlp", "g_ple", "g_final")
WEIGHTS = ("g_mix", "w_in", "q_norm", "w_uq", "kv_norm", "w_ukv", "w_pool", "pool_scale", "ssd_conv_w",
           "ssd_conv_b", "ssd_dt_bias", "ssd_a_log", "ssd_d", "ssd_norm", "lru_conv_w", "lru_conv_b", "lru_w_a",
           "lru_b_a", "lru_w_i", "lru_b_i", "lru_lambda", "w_branch", "w_out", "g_mlp", "w_ff1", "w_ff2", "g_ple",
           "w_ple_gate", "w_ple", "g_final")
CONV_SHARDED = ("ssd_conv_w", "lru_conv_w")
PACK_W = 1024
PACK_ROWS = 64


def _cparams(sem, vmem=VMEM_LIMIT):
    return pltpu.CompilerParams(dimension_semantics=sem, vmem_limit_bytes=vmem)


def _pick(n, cands):
    for c in cands:
        if n % c == 0:
            return c
    return n


class _Cols:
    def __init__(self, arr, off, width):
        self.arr, self.off, self.width = arr, off, width

    shape = property(lambda self: (self.arr.shape[0], self.width))
    dtype = property(lambda self: self.arr.dtype)


def _arr(x):
    return x.arr if isinstance(x, _Cols) else x


def _off(x, unit):
    off = x.off if isinstance(x, _Cols) else 0
    assert off % unit == 0, (off, unit)
    return off // unit


def _sigmoid(x):
    return 1.0 / (1.0 + jnp.exp(-x))


def _silu(x):
    return x * _sigmoid(x)


def _silu_grad(x):
    s = _sigmoid(x)
    return s * (1.0 + x * (1.0 - s))


def _softplus(x):
    e = jnp.exp(-jnp.abs(x))
    log1p_e = jnp.where(e < 1e-3, e * (1.0 - e * (0.5 - e * (1.0 / 3.0))), jnp.log(1.0 + e))
    return jnp.maximum(x, 0.0) + log1p_e


_GELU_C = math.sqrt(2.0 / math.pi)


def _gelu(x):
    t = jnp.tanh(_GELU_C * (x + 0.044715 * x * x * x))
    return 0.5 * x * (1.0 + t)


def _gelu_grad(x):
    t = jnp.tanh(_GELU_C * (x + 0.044715 * x * x * x))
    return 0.5 * (1.0 + t) + 0.5 * x * (1.0 - t * t) * _GELU_C * (1.0 + 3.0 * 0.044715 * x * x)


def _neg_expm1(x):
    series = -x * (1.0 + 0.5 * x * (1.0 + (1.0 / 3.0) * x * (1.0 + 0.25 * x)))
    return jnp.where(x > -0.05, series, 1.0 - jnp.exp(x))


def _shift_down(x, k, row):
    return jnp.where(row >= k, pltpu.roll(x, k, 0), 0.0)


def _shift_up(x, k, row):
    n = x.shape[0]
    return jnp.where(row < n - k, pltpu.roll(x, n - k, 0), 0.0)


def _cumsum_rows(x, row):
    d = 1
    while d < x.shape[0]:
        x = x + _shift_down(x, d, row)
        d *= 2
    return x


def _rev_cumsum_rows(x, row):
    d = 1
    while d < x.shape[0]:
        x = x + _shift_up(x, d, row)
        d *= 2
    return x


def _cumsum_lanes(x, col):
    d = 1
    while d < x.shape[1]:
        x = x + jnp.where(col >= d, pltpu.roll(x, d, 1), 0.0)
        d *= 2
    return x


def _dot(a, b, ta=False, tb=False):
    dn = (((0 if ta else 1,), (1 if tb else 0,)), ((), ()))
    return lax.dot_general(a.astype(MXU_DTYPE), b.astype(MXU_DTYPE), dn, preferred_element_type=F32)


def _mm_tiles(m, n, k, a_bytes, b_bytes, mn_bytes):
    best = None
    for tm in (1024, 512, 384, 256, 128):
        for tn in (1024, 512, 384, 256, 128):
            for tk in (2048, 1024, 512, 384, 256, 128):
                if m % tm or n % tn or k % tk:
                    continue
                vmem = 2 * (tm * tk * a_bytes + tk * tn * b_bytes) + 2 * tm * tn * mn_bytes + 4 * tm * tn
                vmem += 2 * (tm * tk + tk * tn)
                if vmem > MM_VMEM_BUDGET:
                    continue
                steps = (m // tm) * (n // tn) * (k // tk)
                key = (steps, vmem)
                if best is None or key < best[0]:
                    best = (key, (tm, tn, tk))
    assert best is not None, (m, n, k)
    return best[1]


def _mm(a, b, *, ta=False, tb=False, epilogue=None, tiles=(), rowvecs=(), out_dtypes=(F32,), name):
    m, k = (a.shape[1], a.shape[0]) if ta else a.shape
    n = b.shape[0] if tb else b.shape[1]
    assert (b.shape[1] if tb else b.shape[0]) == k, (a.shape, b.shape, ta, tb)
    mn_bytes = sum(t.dtype.itemsize for t in tiles) + sum(jnp.dtype(dt).itemsize for dt in out_dtypes)
    tm, tn, tk = _mm_tiles(m, n, k, a.dtype.itemsize, b.dtype.itemsize, mn_bytes)
    nk = k // tk
    nt, nr, no = len(tiles), len(rowvecs), len(out_dtypes)

    def body(*refs):
        a_ref, b_ref = refs[0], refs[1]
        tile_refs = refs[2:2 + nt]
        row_refs = refs[2 + nt:2 + nt + nr]
        out_refs = refs[2 + nt + nr:2 + nt + nr + no]
        acc_ref = refs[-1]
        kk = pl.program_id(2)

        @pl.when(kk == 0)
        def _():
            acc_ref[...] = jnp.zeros_like(acc_ref)

        acc_ref[...] += _dot(a_ref[...], b_ref[...], ta, tb)

        @pl.when(kk == nk - 1)
        def _():
            acc = acc_ref[...]
            if epilogue is None:
                outs = (acc,)
            else:
                outs = epilogue(acc, *[t[...] for t in tile_refs], *[r[...] for r in row_refs])
            for o_ref, o in zip(out_refs, outs):
                o_ref[...] = o.astype(o_ref.dtype)

    a_spec = pl.BlockSpec((tk, tm), lambda i, j, kk: (kk, i)) if ta else pl.BlockSpec((tm, tk), lambda i, j, kk: (i, kk))
    b_spec = pl.BlockSpec((tn, tk), lambda i, j, kk: (j, kk)) if tb else pl.BlockSpec((tk, tn), lambda i, j, kk: (kk, j))
    mn_spec = pl.BlockSpec((tm, tn), lambda i, j, kk: (i, j))
    row_spec = pl.BlockSpec((1, tn), lambda i, j, kk: (0, j))
    tile_specs = [pl.BlockSpec((tm, tn), lambda i, j, kk, ob=_off(t, tn): (i, j + ob)) for t in tiles]
    outs = pl.pallas_call(
        body, name=name,
        grid=(m // tm, n // tn, nk),
        in_specs=[a_spec, b_spec] + tile_specs + [row_spec] * nr,
        out_specs=[mn_spec] * no,
        out_shape=[jax.ShapeDtypeStruct((m, n), dt) for dt in out_dtypes],
        scratch_shapes=[pltpu.VMEM((tm, tn), F32)],
        compiler_params=_cparams(("parallel", "parallel", "arbitrary")),
    )(a, b, *[_arr(t) for t in tiles], *rowvecs)
    return outs[0] if no == 1 else tuple(outs)


def _branch_merge(ys, ws, u_all, *, name):
    s, d = ys[0].shape[0], ws[0].shape[1]
    tm, tn = _pick(s, (512, 256, 128)), _pick(d, (512, 256, 128))
    nb = len(ys)

    def body(*refs):
        y_refs, w_refs, g_refs = refs[:nb], refs[nb:2 * nb], refs[2 * nb:3 * nb]
        merged_ref, yb_refs = refs[3 * nb], refs[3 * nb + 1:]
        merged = None
        for y_ref, w_ref, g_ref, yb_ref in zip(y_refs, w_refs, g_refs, yb_refs):
            acc = _dot(y_ref[...], w_ref[...])
            yb_ref[...] = acc.astype(yb_ref.dtype)
            term = _sigmoid(g_ref[...]) * acc
            merged = term if merged is None else merged + term
        merged_ref[...] = merged

    mn = pl.BlockSpec((tm, tn), lambda i, j: (i, j))
    in_specs = [pl.BlockSpec((tm, y.shape[1]), lambda i, j: (i, 0)) for y in ys]
    in_specs += [pl.BlockSpec((w.shape[0], tn), lambda i, j: (0, j)) for w in ws]
    in_specs += [pl.BlockSpec((tm, tn), lambda i, j, ob=b * d // tn: (i, j + ob)) for b in range(nb)]
    return pl.pallas_call(
        body, name=name, grid=(s // tm, d // tn), in_specs=in_specs, out_specs=[mn] * (nb + 1),
        out_shape=[jax.ShapeDtypeStruct((s, d), F32)] + [jax.ShapeDtypeStruct((s, d), BF16)] * nb,
        compiler_params=_cparams(("parallel", "parallel")),
    )(*ys, *ws, *[u_all] * nb)


def _rowwise(fn, rows, fulls, outs, *, name, tm=None):
    r = rows[0].shape[0]
    if tm is None:
        widest = max([x.shape[1] for x in rows] + [o[0] for o in outs])
        tm = _pick(r, (max(8, min(512, (512 * 1024) // widest)), 256, 128, 64, 32, 16, 8))
    nrow, nfull, nout = len(rows), len(fulls), len(outs)

    def body(*refs):
        row_refs = refs[:nrow]
        full_refs = refs[nrow:nrow + nfull]
        out_refs = refs[nrow + nfull:]
        res = fn(*[x[...] for x in row_refs], *[x[...] for x in full_refs])
        if not isinstance(res, (tuple, list)):
            res = (res,)
        step = pl.program_id(0)
        for o_ref, o, spec in zip(out_refs, res, outs):
            if spec[2] == "row":
                o_ref[...] = o.astype(o_ref.dtype)
            else:
                @pl.when(step == 0)
                def _(o_ref=o_ref):
                    o_ref[...] = jnp.zeros_like(o_ref)
                o_ref[...] += o

    in_specs = [pl.BlockSpec((tm, x.shape[1]), lambda i, ob=_off(x, x.shape[1]): (i, ob)) for x in rows]
    in_specs += [pl.BlockSpec(x.shape, lambda i, nd=x.ndim: (0,) * nd) for x in fulls]
    out_specs, out_shape = [], []
    for c, dt, kind in outs:
        if kind == "row":
            out_specs.append(pl.BlockSpec((tm, c), lambda i: (i, 0)))
            out_shape.append(jax.ShapeDtypeStruct((r, c), dt))
        else:
            out_specs.append(pl.BlockSpec((1, c), lambda i: (0, 0)))
            out_shape.append(jax.ShapeDtypeStruct((1, c), F32))
    res = pl.pallas_call(
        body, name=name, grid=(r // tm,), in_specs=in_specs, out_specs=out_specs, out_shape=out_shape,
        compiler_params=_cparams(("arbitrary",)),
    )(*[_arr(x) for x in rows], *fulls)
    return res[0] if nout == 1 else tuple(res)


def _colsum(x):
    return jnp.sum(x, axis=0, keepdims=True)


def _rms_parts(x, n_real):
    r = lax.rsqrt(jnp.sum(x * x, axis=-1, keepdims=True) * (1.0 / n_real) + EPS)
    return x * r, r


def _rms_fwd(x, g, *, n_real=None, out_dtype=BF16, name):
    n_real = n_real or x.shape[1]

    def fn(xv, gv):
        xh, _ = _rms_parts(xv, n_real)
        return xh * gv

    return _rowwise(fn, [x], [g], [(x.shape[1], out_dtype, "row")], name=name)


def _rms_bwd_math(xv, gv, dh, n_real):
    xh, r = _rms_parts(xv, n_real)
    dxh = dh * gv
    dx = r * (dxh - xh * (jnp.sum(dxh * xh, axis=-1, keepdims=True) * (1.0 / n_real)))
    return dx, _colsum(dh * xh)


def _rms_bwd(x, g, dh, res=None, *, name):
    n = x.shape[1]
    if res is None:
        def fn(xv, dhv, gv):
            return _rms_bwd_math(xv, gv, dhv.astype(F32), n)
        rows = [x, dh]
    else:
        def fn(xv, dhv, rv, gv):
            dx, dg = _rms_bwd_math(xv, gv, dhv.astype(F32), n)
            return dx + rv, dg
        rows = [x, dh, res]
    return _rowwise(fn, rows, [g], [(n, F32, "row"), (n, F32, "acc")], name=name)


def _seq_call(body, ins, outs, n_blocks, *, name):
    in_specs, args = [], []
    for x, kind in ins:
        in_specs.append(pl.BlockSpec((x.shape[0], LANES), lambda j, ob=_off(x, LANES): (0, j + ob)))
        args.append(_arr(x))
    out_specs, out_shape = [], []
    for shape, dt in outs:
        out_specs.append(pl.BlockSpec((shape[0], LANES), lambda j: (0, j)))
        out_shape.append(jax.ShapeDtypeStruct(shape, dt))
    res = pl.pallas_call(body, name=name, grid=(n_blocks,), in_specs=in_specs, out_specs=out_specs,
                         out_shape=out_shape, compiler_params=_cparams(("parallel",)))(*args)
    return res[0] if len(outs) == 1 else tuple(res)


def _conv_pre(x, w, b, row):
    acc = x * w[CONV_W - 1:CONV_W, :] + b
    for k in range(CONV_W - 1):
        acc = acc + _shift_down(x, CONV_W - 1 - k, row) * w[k:k + 1, :]
    return acc


def _conv_fwd(x, w, b, *, silu, name):
    s, c = x.shape

    def body(x_ref, w_ref, b_ref, y_ref):
        xv = x_ref[...]
        row = lax.broadcasted_iota(jnp.int32, xv.shape, 0)
        pre = _conv_pre(xv, w_ref[...], b_ref[...], row)
        y_ref[...] = _silu(pre) if silu else pre

    return _seq_call(body, [(x, "seq"), (w, "par"), (b, "par")], [((s, c), F32)], c // LANES, name=name)


def _conv_bwd(x, w, b, dy, *, silu, name):
    s, c = x.shape

    def body(x_ref, w_ref, b_ref, dy_ref, dx_ref, dw_ref, db_ref):
        xv, wv, dv = x_ref[...], w_ref[...], dy_ref[...]
        row = lax.broadcasted_iota(jnp.int32, xv.shape, 0)
        if silu:
            dv = dv * _silu_grad(_conv_pre(xv, wv, b_ref[...], row))
        dx = dv * wv[CONV_W - 1:CONV_W, :]
        dws = [None] * CONV_W
        dws[CONV_W - 1] = _colsum(dv * xv)
        for k in range(CONV_W - 1):
            sh = CONV_W - 1 - k
            dx = dx + _shift_up(dv, sh, row) * wv[k:k + 1, :]
            dws[k] = _colsum(dv * _shift_down(xv, sh, row))
        dx_ref[...] = dx
        for k in range(CONV_W):
            dw_ref[k:k + 1, :] = dws[k]
        db_ref[...] = _colsum(dv)

    return _seq_call(body, [(x, "seq"), (w, "par"), (b, "par"), (dy, "seq")],
                     [((s, c), F32), ((CONV_W, c), F32), ((1, c), F32)], c // LANES, name=name)


def _pool_select(levels):
    g = pl.program_id(0)
    return jnp.where(g == 0, levels[0], jnp.where(g == 1, levels[1], jnp.where(g == 2, levels[2], levels[3])))


def _pool_count(row):
    g = pl.program_id(0)
    w = jnp.where(g == 0, POOL_WINDOWS[0], jnp.where(g == 1, POOL_WINDOWS[1],
                                                     jnp.where(g == 2, POOL_WINDOWS[2], POOL_WINDOWS[3])))
    return jnp.minimum(row + 1, w).astype(F32)


def _pool_fwd(u, *, name):
    def body(u_ref, d_ref):
        uv = u_ref[...]
        row = lax.broadcasted_iota(jnp.int32, uv.shape, 0)
        levels, cur, sh = [], uv, 1
        for _ in POOL_WINDOWS:
            cur = cur + _shift_down(cur, sh, row)
            levels.append(cur)
            sh *= 2
        d_ref[...] = _pool_select(levels) / _pool_count(row) - uv

    return _seq_call(body, [(u, "seq")], [(u.shape, F32)], u.shape[1] // LANES, name=name)


def _pool_bwd(dd, *, name):
    def body(dd_ref, du_ref):
        dv = dd_ref[...]
        row = lax.broadcasted_iota(jnp.int32, dv.shape, 0)
        levels, cur, sh = [], dv / _pool_count(row), 1
        for _ in POOL_WINDOWS:
            cur = cur + _shift_up(cur, sh, row)
            levels.append(cur)
            sh *= 2
        du_ref[...] = _pool_select(levels) - dv

    return _seq_call(body, [(dd, "seq")], [(dd.shape, F32)], dd.shape[1] // LANES, name=name)


def _lru_gates(pre_a, pre_i, xc, lam, b_a, b_i):
    r = _sigmoid(pre_a + b_a)
    i = _sigmoid(pre_i + b_i)
    sp = _softplus(-lam)
    log_a = -LRU_C * r * sp
    a = jnp.exp(log_a)
    mult = jnp.sqrt(_neg_expm1(2.0 * log_a))
    return r, i, sp, a, mult


def _lru_fwd(pre, xc, gate_in, lam, b_a, b_i, *, name):
    s, c = xc.shape
    nb = c // LANES

    def body(pa_ref, pi_ref, xc_ref, g_ref, lam_ref, ba_ref, bi_ref, y_ref, h_ref):
        xv = xc_ref[...]
        row = lax.broadcasted_iota(jnp.int32, xv.shape, 0)
        _, i, _, a, mult = _lru_gates(pa_ref[...], pi_ref[...], xv, lam_ref[...], ba_ref[...], bi_ref[...])
        h = xv * i * mult
        d = 1
        while d < s:
            h = h + a * _shift_down(h, d, row)
            a = a * jnp.where(row >= d, pltpu.roll(a, d, 0), 1.0)
            d *= 2
        h_ref[...] = h
        y_ref[...] = h * _gelu(g_ref[...])

    blk = lambda off: pl.BlockSpec((s, LANES), lambda j: (0, j + off))
    par = pl.BlockSpec((1, LANES), lambda j: (0, j))
    return pl.pallas_call(
        body, name=name, grid=(nb,),
        in_specs=[blk(0), blk(nb), blk(0), blk(_off(gate_in, LANES)), par, par, par],
        out_specs=[blk(0), blk(0)],
        out_shape=[jax.ShapeDtypeStruct((s, c), F32)] * 2,
        compiler_params=_cparams(("parallel",)),
    )(pre, pre, xc, _arr(gate_in), lam, b_a, b_i)


def _lru_bwd(pre, xc, gate_in, lam, b_a, b_i, h, dy, *, name):
    s, c = xc.shape
    nb = c // LANES

    def body(pa_ref, pi_ref, xc_ref, g_ref, lam_ref, ba_ref, bi_ref, h_ref, dy_ref,
             dpa_ref, dpi_ref, dxc_ref, dg_ref, dlam_ref, dba_ref, dbi_ref):
        xv, gv, hv, dv = xc_ref[...], g_ref[...], h_ref[...], dy_ref[...]
        row = lax.broadcasted_iota(jnp.int32, xv.shape, 0)
        r, i, sp, a, mult = _lru_gates(pa_ref[...], pi_ref[...], xv, lam_ref[...], ba_ref[...], bi_ref[...])
        dg_ref[...] = dv * hv * _gelu_grad(gv)
        dh = dv * _gelu(gv)
        an = jnp.where(row < s - 1, pltpu.roll(a, s - 1, 0), 0.0)
        d = 1
        while d < s:
            dh = dh + an * _shift_up(dh, d, row)
            an = an * jnp.where(row < s - d, pltpu.roll(an, s - d, 0), 1.0)
            d *= 2
        da = dh * _shift_down(hv, 1, row)
        dxc_ref[...] = dh * i * mult
        di = dh * xv * mult
        dmult = dh * xv * i
        dlog_a = (da - dmult * a / mult) * a
        dr = dlog_a * (-LRU_C) * sp
        dlam_ref[...] = _colsum(dlog_a * LRU_C * r * _sigmoid(-lam_ref[...]))
        dpa = dr * r * (1.0 - r)
        dpi = di * i * (1.0 - i)
        dpa_ref[...] = dpa
        dpi_ref[...] = dpi
        dba_ref[...] = _colsum(dpa)
        dbi_ref[...] = _colsum(dpi)

    blk = lambda off: pl.BlockSpec((s, LANES), lambda j: (0, j + off))
    par = pl.BlockSpec((1, LANES), lambda j: (0, j))
    sc = jax.ShapeDtypeStruct((s, c), F32)
    pc = jax.ShapeDtypeStruct((1, c), F32)
    dpa, dpi, dxc, dg, dlam, dba, dbi = pl.pallas_call(
        body, name=name, grid=(nb,),
        in_specs=[blk(0), blk(nb), blk(0), blk(_off(gate_in, LANES)), par, par, par, blk(0), blk(0)],
        out_specs=[blk(0), blk(0), blk(0), blk(0), par, par, par],
        out_shape=[sc, sc, sc, sc, pc, pc, pc],
        compiler_params=_cparams(("parallel",)),
    )(pre, pre, xc, _arr(gate_in), lam, b_a, b_i, h, dy)
    return dpa, dpi, dxc, dg, dlam, dba, dbi


GROUP_HEADS = 4
SSD_GROUPS = 2


def _ssd_specs(nc, order):
    hw, gw = N_HEADS * LANES, SSD_GROUPS * LANES
    return dict(
        x=pl.BlockSpec((SSD_CHUNK, hw), lambda ci: (order(ci), 0)),
        b=pl.BlockSpec((SSD_CHUNK, gw), lambda ci: (order(ci), hw // gw)),
        c=pl.BlockSpec((SSD_CHUNK, gw), lambda ci: (order(ci), hw // gw + 1)),
        dtcol=pl.BlockSpec((N_HEADS, SSD_CHUNK, 1), lambda ci: (0, order(ci), 0)),
        dtrow=pl.BlockSpec((N_HEADS, 1, SSD_CHUNK), lambda ci: (0, 0, order(ci))),
        scal=pl.BlockSpec((N_HEADS, 1, 1), lambda ci: (0, 0, 0)),
        state=pl.BlockSpec((N_HEADS, 1, LANES, LANES), lambda ci: (0, order(ci), 0, 0)),
        group=pl.BlockSpec((SSD_CHUNK, gw), lambda ci: (order(ci), 0)),
        pacc=pl.BlockSpec((N_HEADS, 1, LANES), lambda ci: (0, 0, 0)),
        heads_row=pl.BlockSpec((N_HEADS, SSD_CHUNK), lambda ci: (0, order(ci))),
    )


def _ssd_chunk_terms(dtcol, dtrow, bias, a_log):
    shp = (SSD_CHUNK, SSD_CHUNK)
    row = lax.broadcasted_iota(jnp.int32, shp, 0)
    col = lax.broadcasted_iota(jnp.int32, shp, 1)
    a_head = -jnp.exp(a_log)
    dt_c = jnp.broadcast_to(_softplus(dtcol + bias), shp)
    dt_r = jnp.broadcast_to(_softplus(dtrow + bias), shp)
    cs_c = _cumsum_rows(dt_c * a_head, row)
    cs_r = _cumsum_lanes(dt_r * a_head, col)
    cs_last = jnp.sum(jnp.where(row == SSD_CHUNK - 1, cs_c, 0.0), axis=0, keepdims=True)
    return row, col, a_head, dt_c, cs_c, cs_r, cs_last


def _ssd_fwd(xbc, dtcol, dtrow, bias, a_log, dskip, *, name):
    s = xbc.shape[0]
    nc = s // SSD_CHUNK

    def body(x_ref, b_ref, c_ref, dtc_ref, dtr_ref, bias_ref, alog_ref, d_ref, y_ref, st_ref, state):
        ci = pl.program_id(0)

        @pl.when(ci == 0)
        def _():
            state[...] = jnp.zeros_like(state)

        for gi in range(SSD_GROUPS):
            glanes = slice(gi * LANES, (gi + 1) * LANES)
            bm, cm = b_ref[:, glanes], c_ref[:, glanes]
            cb = _dot(cm, bm, tb=True)
            bm_t = bm.T
            for r in range(gi * GROUP_HEADS, (gi + 1) * GROUP_HEADS):
                lanes = slice(r * LANES, (r + 1) * LANES)
                xv = x_ref[:, lanes]
                row, col, _, dt_c, cs_c, cs_r, cs_last = _ssd_chunk_terms(dtc_ref[r], dtr_ref[r], bias_ref[r], alog_ref[r])
                g = cb * jnp.exp(jnp.where(col <= row, cs_c - cs_r, -jnp.inf))
                xdt = xv * dt_c
                st = state[r]
                st_ref[r, 0] = st
                y_ref[:, lanes] = _dot(g, xdt) + _dot(cm, st) * jnp.exp(cs_c) + xv * d_ref[r]
                state[r] = jnp.exp(cs_last) * st + _dot(bm_t, xdt * jnp.exp(cs_last - cs_c))

    sp = _ssd_specs(nc, lambda ci: ci)
    return pl.pallas_call(
        body, name=name, grid=(nc,),
        in_specs=[sp["x"], sp["b"], sp["c"], sp["dtcol"], sp["dtrow"], sp["scal"], sp["scal"], sp["scal"]],
        out_specs=[sp["x"], sp["state"]],
        out_shape=[jax.ShapeDtypeStruct((s, N_HEADS * LANES), F32),
                   jax.ShapeDtypeStruct((N_HEADS, nc, LANES, LANES), F32)],
        scratch_shapes=[pltpu.VMEM((N_HEADS, LANES, LANES), F32)],
        compiler_params=_cparams(("arbitrary",)),
    )(xbc, xbc, xbc, dtcol, dtrow, bias, a_log, dskip)


def _ssd_bwd(xbc, dtcol, dtrow, bias, a_log, dskip, states, dy, *, name):
    s = xbc.shape[0]
    nc = s // SSD_CHUNK

    def body(x_ref, b_ref, c_ref, dtc_ref, dtr_ref, bias_ref, alog_ref, d_ref, st_ref, dy_ref,
             dx_ref, db_ref, dc_ref, ddt_ref, dbias_ref, dalog_ref, dd_ref, dstate):
        ci = pl.program_id(0)

        @pl.when(ci == 0)
        def _():
            dstate[...] = jnp.zeros_like(dstate)
            dbias_ref[...] = jnp.zeros_like(dbias_ref)
            dalog_ref[...] = jnp.zeros_like(dalog_ref)
            dd_ref[...] = jnp.zeros_like(dd_ref)

        rowsum = lambda v: jnp.sum(v, axis=1, keepdims=True)
        tot = lambda v: jnp.broadcast_to(jnp.sum(v, axis=0, keepdims=True), (1, LANES))
        lane = lax.broadcasted_iota(jnp.int32, (SSD_CHUNK, LANES), 1)
        ddt_cols = jnp.zeros((SSD_CHUNK, LANES), F32)
        for gi in range(SSD_GROUPS):
            glanes = slice(gi * LANES, (gi + 1) * LANES)
            bm, cm = b_ref[:, glanes], c_ref[:, glanes]
            cb = _dot(cm, bm, tb=True)
            cb_t = _dot(bm, cm, tb=True)
            cm_t = cm.T
            dbm_sum, dcm_sum = None, None
            for r in range(gi * GROUP_HEADS, (gi + 1) * GROUP_HEADS):
                lanes = slice(r * LANES, (r + 1) * LANES)
                xv, dyv, st = x_ref[:, lanes], dy_ref[:, lanes], st_ref[r, 0]
                dtraw_c, bias = dtc_ref[r], bias_ref[r]
                row, col, a_head, dt_c, cs_c, cs_r, cs_last = _ssd_chunk_terms(dtraw_c, dtr_ref[r], bias, alog_ref[r])
                lmat = jnp.exp(jnp.where(col <= row, cs_c - cs_r, -jnp.inf))
                lmat_t = jnp.exp(jnp.where(row <= col, cs_r - cs_c, -jnp.inf))
                g, g_t = cb * lmat, cb_t * lmat_t
                xdt = xv * dt_c
                e_c = jnp.exp(cs_c)
                f_c = jnp.exp(cs_last - cs_c)
                e_last = jnp.exp(cs_last)
                w = xdt * f_c
                dst = dstate[r]

                dg = _dot(dyv, xdt, tb=True)
                dg_t = _dot(xdt, dyv, tb=True)
                dxdt = _dot(g_t, dyv)
                dcs = rowsum(dg * g) - rowsum(dg_t * g_t)
                dcm = _dot(dg * lmat, bm)
                dbm = _dot(dg_t * lmat_t, cm)
                z = _dot(cm, st)
                dz = dyv * e_c
                dcs = dcs + rowsum(dz * z)
                dcm = dcm + _dot(dz, st, tb=True)
                dstate[r] = _dot(cm_t, dz) + e_last * dst
                dcs_last = jnp.sum(rowsum(dst * st), axis=0, keepdims=True) * jnp.max(e_last, axis=1, keepdims=True)
                dbm = dbm + _dot(w, dst, tb=True)
                dw = _dot(bm, dst)
                dxdt = dxdt + dw * f_c
                q = rowsum(dw * w)
                dcs = dcs - q
                dcs_last = dcs_last + jnp.sum(q, axis=0, keepdims=True)
                dx_ref[:, lanes] = dxdt * dt_c + dyv * d_ref[r]
                ddt = rowsum(dxdt * xv)
                dcs_full = jnp.broadcast_to(dcs, (SSD_CHUNK, SSD_CHUNK)) + jnp.where(row == SSD_CHUNK - 1, dcs_last, 0.0)
                da = jnp.max(_rev_cumsum_rows(dcs_full, row), axis=1, keepdims=True)
                dt_col = jnp.max(dt_c, axis=1, keepdims=True)
                draw = (ddt + da * a_head) * _sigmoid(dtraw_c + bias)
                ddt_cols = jnp.where(lane == r, draw, ddt_cols)
                dbias_ref[r] += tot(draw)
                dalog_ref[r] += tot(da * dt_col) * a_head
                dd_ref[r] += tot(rowsum(dyv * xv))
                dbm_sum = dbm if dbm_sum is None else dbm_sum + dbm
                dcm_sum = dcm if dcm_sum is None else dcm_sum + dcm
            db_ref[:, glanes] = dbm_sum
            dc_ref[:, glanes] = dcm_sum
        ddt_ref[...] = ddt_cols.T[:N_HEADS, :]

    sp = _ssd_specs(nc, lambda ci: nc - 1 - ci)
    return pl.pallas_call(
        body, name=name, grid=(nc,),
        in_specs=[sp["x"], sp["b"], sp["c"], sp["dtcol"], sp["dtrow"], sp["scal"], sp["scal"], sp["scal"],
                  sp["state"], sp["x"]],
        out_specs=[sp["x"], sp["group"], sp["group"], sp["heads_row"], sp["pacc"], sp["pacc"], sp["pacc"]],
        out_shape=[jax.ShapeDtypeStruct((s, N_HEADS * LANES), F32),
                   jax.ShapeDtypeStruct((s, 2 * LANES), F32),
                   jax.ShapeDtypeStruct((s, 2 * LANES), F32),
                   jax.ShapeDtypeStruct((N_HEADS, s), F32),
                   jax.ShapeDtypeStruct((N_HEADS, 1, LANES), F32),
                   jax.ShapeDtypeStruct((N_HEADS, 1, LANES), F32),
                   jax.ShapeDtypeStruct((N_HEADS, 1, LANES), F32)],
        scratch_shapes=[pltpu.VMEM((N_HEADS, LANES, LANES), F32)],
        compiler_params=_cparams(("arbitrary",)),
    )(xbc, xbc, xbc, dtcol, dtrow, bias, a_log, dskip, states, dy)


def _att_tile(s):
    return _pick(s, (512, 256, 128))


def _tri(t, transposed=False):
    r = lax.broadcasted_iota(jnp.int32, (t, t), 0)
    c = lax.broadcasted_iota(jnp.int32, (t, t), 1)
    return (r <= c) if transposed else (c <= r)


def _rows_at(ref, blk, t):
    return ref[pl.ds(pl.multiple_of(blk * t, t), t), :]


def _flash_fwd(q, k, v, *, name):
    s = q.shape[0]
    t = _att_tile(s)
    nq = s // t

    def body(q_ref, k_ref, v_ref, o_ref, lse_ref):
        i = pl.program_id(1)
        qv = q_ref[...]

        def step(j, carry, diagonal):
            m_old, l_old, acc = carry
            sc = _dot(qv, _rows_at(k_ref, j, t), tb=True)
            if diagonal:
                sc = jnp.where(_tri(t), sc, -jnp.inf)
            m_new = jnp.maximum(m_old, jnp.max(sc, axis=1, keepdims=True))
            alpha = jnp.exp(m_old - m_new)
            p = jnp.exp(sc - m_new)
            return (m_new, alpha * l_old + jnp.sum(p, axis=1, keepdims=True),
                    alpha * acc + _dot(p, _rows_at(v_ref, j, t)))

        init = (jnp.full((t, 1), -jnp.inf, F32), jnp.zeros((t, 1), F32), jnp.zeros((t, LANES), F32))
        carry = lax.fori_loop(0, i, lambda j, c: step(j, c, False), init)
        m_fin, l_fin, acc = step(i, carry, True)
        o_ref[...] = (acc / l_fin).astype(o_ref.dtype)
        lse_ref[0] = jnp.broadcast_to(m_fin + jnp.log(l_fin), (t, LANES)).T[:1, :]

    q_spec = pl.BlockSpec((t, LANES), lambda h, i: (i, h))
    kv_spec = pl.BlockSpec((s, LANES), lambda h, i: (0, h))
    return pl.pallas_call(
        body, name=name, grid=(N_HEADS, nq),
        in_specs=[q_spec, kv_spec, kv_spec],
        out_specs=[q_spec, pl.BlockSpec((1, 1, t), lambda h, i: (h, 0, i))],
        out_shape=[jax.ShapeDtypeStruct(q.shape, BF16), jax.ShapeDtypeStruct((N_HEADS, 1, s), F32)],
        compiler_params=_cparams(("parallel", "arbitrary")),
    )(q, k, v)


def _att_delta(o, do, *, name):
    s = o.shape[0]

    def body(o_ref, do_ref, dl_ref):
        col = jnp.sum(do_ref[...].astype(F32) * o_ref[...].astype(F32), axis=1, keepdims=True)
        dl_ref[0] = jnp.broadcast_to(col, (s, LANES)).T[:1, :]

    blk = pl.BlockSpec((s, LANES), lambda h: (0, h))
    return pl.pallas_call(
        body, name=name, grid=(N_HEADS,), in_specs=[blk, blk],
        out_specs=pl.BlockSpec((1, 1, s), lambda h: (h, 0, 0)),
        out_shape=jax.ShapeDtypeStruct((N_HEADS, 1, s), F32),
        compiler_params=_cparams(("parallel",)),
    )(o, do)


def _flash_bwd(q, k, v, do, lse_row, delta_row, *, name):
    s = q.shape[0]
    t = _att_tile(s)
    nq = s // t

    def body(q_ref, k_ref, v_ref, do_ref, lse_ref, dl_ref, dq_ref, dk_ref, dv_ref):
        j = pl.program_id(1)
        kv, vv = k_ref[...], v_ref[...]

        @pl.when(j == 0)
        def _():
            dq_ref[...] = jnp.zeros_like(dq_ref)

        def step(i, carry, diagonal):
            dk, dv = carry
            rows = pl.ds(pl.multiple_of(i * t, t), t)
            qi, doi = q_ref[rows, :], do_ref[rows, :]
            p_t = jnp.exp(_dot(kv, qi, tb=True) - lse_ref[0, :, rows])
            if diagonal:
                p_t = jnp.where(_tri(t, transposed=True), p_t, 0.0)
            ds_t = (p_t * (_dot(vv, doi, tb=True) - dl_ref[0, :, rows])).astype(MXU_DTYPE)
            dq_ref[rows, :] += _dot(ds_t, kv, ta=True)
            return dk + _dot(ds_t, qi), dv + _dot(p_t, doi)

        zero = jnp.zeros((t, LANES), F32)
        carry = step(j, (zero, zero), True)
        dk, dv = lax.fori_loop(j + 1, nq, lambda i, c: step(i, c, False), carry)
        dk_ref[...] = dk
        dv_ref[...] = dv

        @pl.when(j == nq - 1)
        def _():
            dq_ref[...] = dq_ref[...] * ATT_SCALE

    q_spec = pl.BlockSpec((s, LANES), lambda h, j: (0, h))
    kv_spec = pl.BlockSpec((t, LANES), lambda h, j: (j, h))
    row_spec = pl.BlockSpec((1, 1, s), lambda h, j: (h, 0, 0))
    return pl.pallas_call(
        body, name=name, grid=(N_HEADS, nq),
        in_specs=[q_spec, kv_spec, kv_spec, q_spec, row_spec, row_spec],
        out_specs=[q_spec, kv_spec, kv_spec],
        out_shape=[jax.ShapeDtypeStruct(q.shape, F32)] * 3,
        compiler_params=_cparams(("parallel", "arbitrary")),
    )(q, k, v, do, lse_row, delta_row)


def _rope(v, cos_t, sin_p, sin_m):
    return v * cos_t + pltpu.roll(v, QK_ROPE // 2, 1) * sin_p + pltpu.roll(v, LANES - QK_ROPE // 2, 1) * sin_m


def _rope_t(d, cos_t, sin_p, sin_m):
    return d * cos_t + pltpu.roll(d * sin_p, LANES - QK_ROPE // 2, 1) + pltpu.roll(d * sin_m, QK_ROPE // 2, 1)


def _att_prep(q_pad, kv2, kr, cos_t, sin_p, sin_m, *, name):
    w = N_HEADS * LANES

    def fn(qv, kvv, krv, c, sp, sm):
        kr_rot = _rope(krv, c, sp, sm)
        qs, ks = [], []
        for h in range(N_HEADS):
            blk = slice(h * LANES, (h + 1) * LANES)
            qs.append(_rope(qv[:, blk], c, sp, sm) * ATT_SCALE)
            ks.append(kvv[:, blk] + kr_rot)
        return jnp.concatenate(qs, axis=1), jnp.concatenate(ks, axis=1), kvv[:, w:]

    return _rowwise(fn, [q_pad, kv2, kr, cos_t, sin_p, sin_m], [],
                    [(w, BF16, "row"), (w, BF16, "row"), (w, BF16, "row")], name=name)


def _att_prep_bwd(dq, dk, cos_t, sin_p, sin_m, *, name):
    w = N_HEADS * LANES

    def fn(dqv, dkv, c, sp, sm):
        outs, dkr = [], None
        for h in range(N_HEADS):
            blk = slice(h * LANES, (h + 1) * LANES)
            outs.append(_rope_t(dqv[:, blk], c, sp, sm))
            dkr = dkv[:, blk] if dkr is None else dkr + dkv[:, blk]
        return jnp.concatenate(outs, axis=1), _rope_t(dkr, c, sp, sm)

    return _rowwise(fn, [dq, dk, cos_t, sin_p, sin_m], [], [(w, BF16, "row"), (LANES, F32, "row")], name=name)


_ANY = pl.BlockSpec(memory_space=pl.ANY)
_MESH = pl.DeviceIdType.MESH


def _mesh_pos():
    return lax.axis_index("x"), lax.axis_index("y"), lax.axis_index("c")


def _remote(src, dst, send_sem, recv_sem, dev):
    return pltpu.make_async_remote_copy(src_ref=src, dst_ref=dst, send_sem=send_sem, recv_sem=recv_sem,
                                        device_id=dev, device_id_type=_MESH)


def _other_chips(x, y):
    chips = [(1 - x, y), (x, 1 - y), (1 - x, 1 - y)]
    return chips, [2 * cx + cy for cx, cy in chips]


def _comm_call(body, ins, out_shapes, n_sems, *, name):
    return pl.pallas_call(
        body, name=name, in_specs=[_ANY] * len(ins), out_specs=[_ANY] * len(out_shapes), out_shape=out_shapes,
        scratch_shapes=[pltpu.SemaphoreType.DMA((k,)) for k in n_sems],
    )(*ins)


def _gather_halves(shards):
    n = len(shards)
    halves = [t.shape[0] // 2 for t in shards]

    def body(*refs):
        xs, outs = refs[:n], refs[n:2 * n]
        send_sems, recv_sems = refs[2 * n:]
        x, y, c = _mesh_pos()
        k = 2 * x + y
        sibling = (x, y, 1 - c)
        chips, ks = _other_chips(x, y)
        half = lambda w, hf: pl.ds(hf * halves[w], halves[w])
        first = [_remote(xs[w].at[half(w, c)], outs[w].at[k, half(w, c)], send_sems.at[6 * w + j], recv_sems.at[6 * w + j],
                         (*chips[j], c)) for w in range(n) for j in range(3)]
        for cp in first:
            cp.start()
        passed = []
        for j in range(3):
            for w in range(n):
                land = outs[w].at[ks[j], half(w, c)]
                _remote(land, land, send_sems.at[6 * w + j], recv_sems.at[6 * w + j], sibling).wait_recv()
                passed.append(_remote(land, land, send_sems.at[6 * w + 3 + j], recv_sems.at[6 * w + 3 + j], sibling))
                passed[-1].start()
        for j in range(3):
            for w in range(n):
                land = outs[w].at[ks[j], half(w, 1 - c)]
                _remote(land, land, send_sems.at[6 * w + 3 + j], recv_sems.at[6 * w + 3 + j], sibling).wait_recv()
        for cp in first + passed:
            cp.wait_send()

    shapes = [jax.ShapeDtypeStruct((4,) + t.shape, t.dtype) for t in shards]
    return _comm_call(body, shards, shapes, (6 * n, 6 * n), name="gather_halves")


_HBM = pl.BlockSpec(memory_space=pltpu.HBM)
_SEM = pl.BlockSpec(memory_space=pltpu.SEMAPHORE)
_EFFECT = pltpu.SideEffectType.DATAFLOW_SIDE_EFFECTING


def _push_start(blocks, *, scatter, name):
    n = len(blocks)

    def body(*refs):
        xs, lands = refs[:n], refs[n:2 * n]
        send_sems, recv_sems = refs[2 * n], refs[2 * n + 1]
        token = refs[-1]
        x, y, c = _mesh_pos()
        k = 2 * x + y
        chips, ks = _other_chips(x, y)
        for w in range(n):
            for j in range(3):
                src = xs[w].at[ks[j]] if scatter else xs[w]
                _remote(src, lands[w].at[k], send_sems.at[3 * w + j], recv_sems.at[3 * w + j], (*chips[j], c)).start()
        token[...] = jnp.zeros_like(token)

    hbm = lambda shape, dtype: pltpu.with_memory_space_constraint(lax.empty(shape, dtype), pltpu.HBM)
    ins = [pltpu.with_memory_space_constraint(t, pltpu.HBM) for t in blocks]
    ins += [hbm(t.shape if scatter else (4,) + t.shape, t.dtype) for t in blocks]
    out_shape = [pltpu.SemaphoreType.DMA((3 * n,)), pltpu.SemaphoreType.DMA((3 * n,))]
    out_shape += [pltpu.HBM(t.shape, t.dtype) for t in ins]
    out_shape += [jax.ShapeDtypeStruct((8, LANES), F32)]
    res = pl.pallas_call(
        body, name=name, out_shape=out_shape, in_specs=[_HBM] * (2 * n),
        out_specs=[_SEM, _SEM] + [_HBM] * (2 * n) + [pl.BlockSpec(memory_space=pltpu.VMEM)],
        input_output_aliases={i: 2 + i for i in range(2 * n)},
        compiler_params=pltpu.CompilerParams(has_side_effects=_EFFECT),
    )(*ins)
    return res[0], res[1], res[2:2 + n], res[2 + n:2 + 2 * n], res[-1]


def _push_wait(send_sems, recv_sems, blocks, lands, after, *, name):
    n = len(blocks)

    def body(*refs):
        lands_in = refs[n:2 * n]
        send_sems, recv_sems = refs[2 * n], refs[2 * n + 1]
        x, y, c = _mesh_pos()
        chips, ks = _other_chips(x, y)
        for w in range(n):
            for j in range(3):
                slot = lands_in[w].at[ks[j]]
                cp = _remote(slot, slot, send_sems.at[3 * w + j], recv_sems.at[3 * w + j], (*chips[j], c))
                cp.wait_send()
                cp.wait_recv()

    out_shape = [pltpu.HBM(t.shape, t.dtype) for t in list(blocks) + list(lands)]
    res = pl.pallas_call(
        body, name=name, out_shape=out_shape,
        in_specs=[_HBM] * (2 * n) + [_SEM, _SEM, pl.BlockSpec(memory_space=pl.ANY)], out_specs=[_HBM] * (2 * n),
        input_output_aliases={i: i for i in range(2 * n)},
        compiler_params=pltpu.CompilerParams(has_side_effects=_EFFECT),
    )(*blocks, *lands, send_sems, recv_sems, after)
    return res[:n], res[n:]


def _send_half(views, *, name):
    n = len(views)

    def body(*refs):
        vs, outs = refs[:n], refs[n:2 * n]
        send_sems, recv_sems = refs[2 * n:]
        x, y, c = _mesh_pos()
        cps = []
        for w in range(n):
            h = views[w].shape[1] // 2
            cps.append(_remote(vs[w].at[:, pl.ds((1 - c) * h, h), :], outs[w], send_sems.at[w], recv_sems.at[w],
                               (x, y, 1 - c)))
            cps[-1].start()
        for cp in cps:
            cp.wait()

    shapes = [jax.ShapeDtypeStruct((t.shape[0], t.shape[1] // 2, t.shape[2]), t.dtype) for t in views]
    return _comm_call(body, views, shapes, (n, n), name=name)


def _swap_with_sibling(mine, *, name):
    n = len(mine)

    def body(*refs):
        hs, outs = refs[:n], refs[n:2 * n]
        send_sems, recv_sems = refs[2 * n:]
        x, y, c = _mesh_pos()
        cps = [_remote(hs[w], outs[w], send_sems.at[w], recv_sems.at[w], (x, y, 1 - c)) for w in range(n)]
        for cp in cps:
            cp.start()
        for cp in cps:
            cp.wait()

    shapes = [jax.ShapeDtypeStruct(t.shape, t.dtype) for t in mine]
    return _comm_call(body, mine, shapes, (n, n), name=name)


def _gather_all(vec, *, name):
    r, w = vec.shape

    def body(v_ref, out_ref, send_sems, recv_sems):
        x, y, c = _mesh_pos()

        def slot(px, py, pc):
            return out_ref.at[4 * px + 2 * py + pc]

        peers = []
        for rel in range(1, 8):
            fx, fy, fc = (rel >> 2) & 1, (rel >> 1) & 1, rel & 1
            peers.append((x ^ fx, y ^ fy, c ^ fc))
        cps = [_remote(v_ref, slot(x, y, c), send_sems.at[j], recv_sems.at[j], peer) for j, peer in enumerate(peers)]
        for cp in cps:
            cp.start()
        for j, peer in enumerate(peers):
            _remote(slot(*peer), slot(*peer), send_sems.at[j], recv_sems.at[j], peer).wait_recv()
        for cp in cps:
            cp.wait_send()

    others = pl.pallas_call(
        body, name=name, in_specs=[_ANY], out_specs=_ANY,
        out_shape=jax.ShapeDtypeStruct((8, r, w), vec.dtype),
        scratch_shapes=[pltpu.SemaphoreType.DMA((7,)), pltpu.SemaphoreType.DMA((7,))],
    )(vec)
    me = 4 * lax.axis_index("x") + 2 * lax.axis_index("y") + lax.axis_index("c")
    return lax.dynamic_update_index_in_dim(others, vec, me, 0)


def _gather_same_core(vec, *, name):
    r, w = vec.shape

    def body(v_ref, out_ref, send_sems, recv_sems):
        x, y, c = _mesh_pos()
        k = 2 * x + y
        chips, ks = _other_chips(x, y)
        cps = [_remote(v_ref, out_ref.at[k], send_sems.at[j], recv_sems.at[j], (*chips[j], c)) for j in range(3)]
        for cp in cps:
            cp.start()
        for j in range(3):
            slot = out_ref.at[ks[j]]
            _remote(slot, slot, send_sems.at[j], recv_sems.at[j], (*chips[j], c)).wait_recv()
        for cp in cps:
            cp.wait_send()

    others = pl.pallas_call(
        body, name=name, in_specs=[_ANY], out_specs=_ANY,
        out_shape=jax.ShapeDtypeStruct((4, r, w), vec.dtype),
        scratch_shapes=[pltpu.SemaphoreType.DMA((3,)), pltpu.SemaphoreType.DMA((3,))],
    )(vec)
    k_chip = 2 * lax.axis_index("x") + lax.axis_index("y")
    return lax.dynamic_update_index_in_dim(others, vec, k_chip, 0)


def _row_tile(rows, row_bytes):
    for tm in (1024, 512, 256, 128, 64, 32, 16):
        if rows % tm == 0 and tm * row_bytes <= ELEMENTWISE_BLOCK_BYTES:
            return tm
    return 16 if rows % 16 == 0 else rows


def _chip_sum_half(g, got, c, *, name):
    nb, r, w = g.shape
    half = r // 2
    tm = _row_tile(half, w * 4)
    per = half // tm

    def body(c_ref, g_ref, o_ref, out_ref):
        out_ref[...] = (g_ref[...] + o_ref[...]).astype(out_ref.dtype)

    return pl.pallas_call(
        body, name=name,
        grid_spec=pltpu.PrefetchScalarGridSpec(
            num_scalar_prefetch=1, grid=(nb, per),
            in_specs=[pl.BlockSpec((1, tm, w), lambda b, i, c_ref: (b, c_ref[0] * per + i, 0)),
                      pl.BlockSpec((1, tm, w), lambda b, i, c_ref: (b, i, 0))],
            out_specs=pl.BlockSpec((1, tm, w), lambda b, i, c_ref: (b, i, 0))),
        out_shape=jax.ShapeDtypeStruct((nb, half, w), BF16),
        compiler_params=_cparams(("parallel", "parallel")),
    )(jnp.reshape(c, (1,)).astype(jnp.int32), g, got)


def _sum_slots(stack, *, name):
    n, r, w = stack.shape
    tm = _row_tile(r, n * w * stack.dtype.itemsize)

    def body(s_ref, out_ref):
        acc = s_ref[0].astype(F32)
        for i in range(1, n):
            acc = acc + s_ref[i].astype(F32)
        out_ref[...] = acc

    return pl.pallas_call(
        body, name=name, grid=(r // tm,),
        in_specs=[pl.BlockSpec((n, tm, w), lambda i: (0, i, 0))],
        out_specs=pl.BlockSpec((tm, w), lambda i: (i, 0)),
        out_shape=jax.ShapeDtypeStruct((r, w), F32),
        compiler_params=_cparams(("parallel",)),
    )(stack)


def _adam_math(wv, gv, mv, vv):
    m_new = ADAM_B1 * mv + (1.0 - ADAM_B1) * gv
    v_new = ADAM_B2 * vv + (1.0 - ADAM_B2) * (gv * gv)
    m_hat = m_new / (1.0 - ADAM_B1 ** ADAM_STEP)
    v_hat = v_new / (1.0 - ADAM_B2 ** ADAM_STEP)
    delta = -ADAM_LR * (m_hat / (jnp.sqrt(v_hat) + ADAM_EPS) + ADAM_WD * wv)
    return delta, m_new, v_new


def _adamw(w, g, m, v, *, name):
    shape = w.shape
    cols = shape[-1]
    flat = lambda t: t.reshape(-1, cols)
    rows = flat(w).shape[0]
    tm = _pick(rows, (256, 128, 64, 32, 16, 8))
    outs = _rowwise(_adam_math, [flat(w), flat(g), flat(m), flat(v)], [], [(cols, F32, "row")] * 3, name=name, tm=tm)
    return tuple(o.reshape(shape) for o in outs)


def _adamw_slots(w, slots, m, v, c, *, name):
    shape = w.shape
    cols = shape[-1]
    half = slots[0][0].shape[1]
    v4 = lambda t: t.reshape(2, 2, half, cols)
    assert all(s.shape == (4, half, cols) for pair in slots for s in pair) and w.size == 4 * half * cols
    tm = _row_tile(half, cols * 4 * 4)

    def body(c_ref, w_ref, m0_ref, o0_ref, m1_ref, o1_ref, m_ref, v_ref, g_ref, d_ref, mo_ref, vo_ref):
        first = pl.program_id(0) == 0
        own = pl.program_id(1) == c_ref[0]
        g = None
        for i in range(4):
            part = jnp.where(first, jnp.where(own, m0_ref[i], o0_ref[i]), jnp.where(own, m1_ref[i], o1_ref[i]))
            g = part.astype(F32) if g is None else g + part.astype(F32)
        delta, m_new, v_new = _adam_math(w_ref[0, 0], g, m_ref[0, 0], v_ref[0, 0])
        g_ref[0, 0], d_ref[0, 0], mo_ref[0, 0], vo_ref[0, 0] = g, delta, m_new, v_new

    blk = pl.BlockSpec((1, 1, tm, cols), lambda l, hf, i, c_ref: (l, hf, i, 0))

    def slot_spec(layer, mine):
        def index(l, hf, i, c_ref):
            same_half = hf * c_ref[0] + (1 - hf) * (1 - c_ref[0])
            use = (l if layer else 1 - l) * (same_half if mine else 1 - same_half)
            return (0, i * use, 0)
        return pl.BlockSpec((4, tm, cols), index)

    outs = pl.pallas_call(
        body, name=name,
        grid_spec=pltpu.PrefetchScalarGridSpec(
            num_scalar_prefetch=1, grid=(2, 2, half // tm),
            in_specs=[blk, slot_spec(0, True), slot_spec(0, False), slot_spec(1, True), slot_spec(1, False), blk, blk],
            out_specs=[blk] * 4),
        out_shape=[jax.ShapeDtypeStruct((2, 2, half, cols), F32)] * 4,
        compiler_params=_cparams(("arbitrary", "arbitrary", "arbitrary")),
    )(jnp.reshape(c, (1,)).astype(jnp.int32), v4(w), slots[0][0], slots[0][1], slots[1][0], slots[1][1], v4(m), v4(v))
    return tuple(o.reshape(shape) for o in outs)


def _pad_blocks(w, axis, n_blocks, real, to=LANES, offset=0):
    axis = axis % w.ndim
    shp = w.shape
    w = w.reshape(shp[:axis] + (n_blocks, real) + shp[axis + 1:])
    pads = [(0, 0)] * w.ndim
    pads[axis + 1] = (offset, to - real - offset)
    w = jnp.pad(w, pads)
    return w.reshape(shp[:axis] + (n_blocks * to,) + shp[axis + 1:])


def _unpad_blocks(w, axis, n_blocks, real, to=LANES, offset=0):
    axis = axis % w.ndim
    shp = w.shape
    w = w.reshape(shp[:axis] + (n_blocks, to) + shp[axis + 1:])
    w = lax.slice_in_dim(w, offset, offset + real, axis=axis + 1)
    return w.reshape(shp[:axis] + (n_blocks * real,) + shp[axis + 1:])


def _block_diag(w):
    n, a, b = w.shape
    eye = jnp.eye(n, dtype=w.dtype)
    return (eye[:, None, :, None] * w[:, :, None, :]).reshape(n * a, n * b)


def _block_diag_t(d, n):
    a, b = d.shape[0] // n, d.shape[1] // n
    d = d.reshape(n, a, n, b)
    return jnp.stack([d[i, :, i, :] for i in range(n)])


_SPLITS = np.cumsum((0,) + SPLIT_SIZES)


def _w_in_groups(w_in):
    sl = lambda i: w_in[:, _SPLITS[i]:_SPLITS[i + 1]]
    xbc = sl(5)
    xbc_pad = jnp.concatenate([_pad_blocks(xbc[:, :MIX], 1, N_HEADS, HEAD),
                               _pad_blocks(xbc[:, MIX:MIX + 2 * HEAD], 1, 2, HEAD),
                               _pad_blocks(xbc[:, MIX + 2 * HEAD:], 1, 2, HEAD)], axis=1)
    return dict(
        cq=sl(0), ckv=sl(1), kr=_pad_blocks(sl(2), 1, 1, QK_ROPE, offset=HEAD), pool=sl(3),
        z=_pad_blocks(sl(4), 1, N_HEADS, HEAD), xbc=xbc_pad, dt=_pad_blocks(sl(6), 1, 1, N_HEADS),
        lru_g=sl(7), lru_x=sl(8), gates=sl(9))


def _w_in_fused(groups):
    parts, at = [], 0
    for name, off, width in IN_LAYOUT:
        assert groups[name].shape[1] == width and off >= at
        if off > at:
            parts.append(jnp.zeros((groups[name].shape[0], off - at), groups[name].dtype))
        parts.append(groups[name])
        at = off + width
    parts.append(jnp.zeros((parts[0].shape[0], IN_ALL_COLS - at), parts[0].dtype))
    return jnp.concatenate(parts, axis=1)


def _in_cols(arr, name):
    off, width = IN_OFFSETS[name]
    return _Cols(arr, off, width)


def _w_in_ungroup(d):
    xbc = d["xbc"]
    w = N_HEADS * LANES
    xbc_real = jnp.concatenate([_unpad_blocks(xbc[:, :w], 1, N_HEADS, HEAD),
                                _unpad_blocks(xbc[:, w:w + 2 * LANES], 1, 2, HEAD),
                                _unpad_blocks(xbc[:, w + 2 * LANES:], 1, 2, HEAD)], axis=1)
    return jnp.concatenate([d["cq"], d["ckv"], _unpad_blocks(d["kr"], 1, 1, QK_ROPE, offset=HEAD), d["pool"],
                            _unpad_blocks(d["z"], 1, N_HEADS, HEAD), xbc_real, _unpad_blocks(d["dt"], 1, 1, N_HEADS),
                            d["lru_g"], d["lru_x"], d["gates"]], axis=1)


def _pad_xbc_vec(v):
    return jnp.concatenate([_pad_blocks(v[..., :MIX], -1, N_HEADS, HEAD),
                            _pad_blocks(v[..., MIX:MIX + 2 * HEAD], -1, 2, HEAD),
                            _pad_blocks(v[..., MIX + 2 * HEAD:], -1, 2, HEAD)], axis=-1)


def _unpad_xbc_vec(v):
    w = N_HEADS * LANES
    return jnp.concatenate([_unpad_blocks(v[..., :w], -1, N_HEADS, HEAD),
                            _unpad_blocks(v[..., w:w + 2 * LANES], -1, 2, HEAD),
                            _unpad_blocks(v[..., w + 2 * LANES:], -1, 2, HEAD)], axis=-1)


def _layer_weights(p):
    q = dict(p)
    q["in_all"] = _w_in_fused(_w_in_groups(p["w_in"]))
    q["uq"] = _pad_blocks(p["w_uq"], 1, N_HEADS, HEAD + QK_ROPE)
    ukv = p["w_ukv"].reshape(KV_LORA, N_HEADS, 2 * HEAD)
    q["ukv"] = jnp.concatenate([_pad_blocks(ukv[:, :, :HEAD].reshape(KV_LORA, -1), 1, N_HEADS, HEAD),
                                _pad_blocks(ukv[:, :, HEAD:].reshape(KV_LORA, -1), 1, N_HEADS, HEAD)], axis=1)
    q["pool_bd"] = _block_diag(p["w_pool"])
    q["lru_bd"] = jnp.concatenate([_block_diag(p["lru_w_a"]), _block_diag(p["lru_w_i"])], axis=1)
    q["br"] = [_pad_blocks(p["w_branch"][0], 0, N_HEADS, HEAD), p["w_branch"][1],
               _pad_blocks(p["w_branch"][2], 0, N_HEADS, HEAD), p["w_branch"][3]]
    q["ssd_conv_w_pad"] = _pad_xbc_vec(p["ssd_conv_w"])
    q["ssd_conv_b_pad"] = _pad_xbc_vec(p["ssd_conv_b"])[None, :]
    q["ssd_norm_pad"] = _pad_blocks(p["ssd_norm"], 0, N_HEADS, HEAD)[None, :]
    return q


def _row(v):
    return v.reshape(1, -1)


def _scal3(v):
    return v.reshape(N_HEADS, 1, 1)


def _layer_fwd(x, p_emb, w, rope, tag):
    n = lambda s: f"{s}_{tag}"
    sv = {"x": x}
    h = _rms_fwd(x, _row(w["g_mix"]), name=n("rms_mix"))
    sv["h"] = h
    u_all = _mm(h, w["in_all"], name=n("in_proj"))
    u = {k: _in_cols(u_all, k) for k in IN_OFFSETS}
    sv["u"] = u

    cqn = _rms_fwd(u["cq"], _row(w["q_norm"]), name=n("rms_q"))
    ckvn = _rms_fwd(u["ckv"], _row(w["kv_norm"]), name=n("rms_kv"))
    q_pad = _mm(cqn, w["uq"], name=n("uq"))
    kv2 = _mm(ckvn, w["ukv"], name=n("ukv"))
    qc, kc, vc = _att_prep(q_pad, kv2, u["kr"], *rope, name=n("att_prep"))
    y_a, lse = _flash_fwd(qc, kc, vc, name=n("flash_fwd"))
    sv.update(cqn=cqn, ckvn=ckvn, qc=qc, kc=kc, vc=vc, y_a=y_a, lse=lse)

    pool_d = _pool_fwd(u["pool"], name=n("pool_fwd"))
    yb_pre, y_b = _mm(pool_d, w["pool_bd"], epilogue=lambda acc, sc: (acc, acc * sc),
                      rowvecs=[_row(w["pool_scale"])], out_dtypes=(F32, BF16), name=n("pool_mm"))
    sv.update(pool_d=pool_d, yb_pre=yb_pre, y_b=y_b)

    xbc_c = _conv_fwd(u["xbc"], w["ssd_conv_w_pad"], w["ssd_conv_b_pad"], silu=True, name=n("ssd_conv"))
    dt8 = lax.slice_in_dim(u_all, IN_OFFSETS["dt"][0], IN_OFFSETS["dt"][0] + N_HEADS, axis=1)
    dtcol = dt8.T[:, :, None]
    dtrow = dt8.T[:, None, :]
    ssd_par = (_scal3(w["ssd_dt_bias"]), _scal3(w["ssd_a_log"]), _scal3(w["ssd_d"]))
    y_ssd, states = _ssd_fwd(xbc_c, dtcol, dtrow, *ssd_par, name=n("ssd_fwd"))

    def ssd_post(yv, zv, gv):
        xh, _ = _rms_parts(yv * _silu(zv), MIX)
        return xh * gv

    y_c = _rowwise(ssd_post, [y_ssd, u["z"]], [w["ssd_norm_pad"]], [(N_HEADS * LANES, BF16, "row")], name=n("ssd_post"))
    sv.update(xbc_c=xbc_c, dtcol=dtcol, dtrow=dtrow, y_ssd=y_ssd, states=states, y_c=y_c)

    xc = _conv_fwd(u["lru_x"], w["lru_conv_w"], _row(w["lru_conv_b"]), silu=False, name=n("lru_conv"))
    pre = _mm(xc, w["lru_bd"], name=n("lru_mm"))
    lru_par = (_row(w["lru_lambda"]), _row(w["lru_b_a"]), _row(w["lru_b_i"]))
    y_d, h_lru = _lru_fwd(pre, xc, u["lru_g"], *lru_par, name=n("lru_fwd"))
    sv.update(xc=xc, pre=pre, h_lru=h_lru, y_d=y_d)

    merged, *ybs = _branch_merge([y_a, y_b, y_c, y_d], w["br"], u_all, name=n("branch_merge"))
    x1 = _mm(merged, w["w_out"], epilogue=lambda acc, xr: (acc + xr,), tiles=[x], name=n("out_proj"))
    sv.update(ybs=ybs, merged=merged, x1=x1)

    h2 = _rms_fwd(x1, _row(w["g_mlp"]), name=n("rms_mlp"))
    a_ff, f_ff = _mm(h2, w["w_ff1"], epilogue=lambda acc: (acc, jnp.square(jnp.maximum(acc, 0.0))),
                     out_dtypes=(BF16, BF16), name=n("ff1"))
    x2 = _mm(f_ff, w["w_ff2"], epilogue=lambda acc, xr: (acc + xr,), tiles=[x1], name=n("ff2"))
    sv.update(h2=h2, a_ff=a_ff, f_ff=f_ff, x2=x2)

    h3 = _rms_fwd(x2, _row(w["g_ple"]), name=n("rms_ple"))
    e_ple = _mm(p_emb, w["w_ple"], name=n("ple_emb"))
    x3, gt_ple = _mm(h3, w["w_ple_gate"], epilogue=lambda acc, ev, xr: (xr + ev * _sigmoid(acc), _sigmoid(acc)),
                     tiles=[e_ple, x2], out_dtypes=(F32, F32), name=n("ple_gate"))
    sv.update(h3=h3, e_ple=e_ple, gt_ple=gt_ple, p_emb=p_emb)
    return x3, sv


def _layer_bwd(dx3, sv, w, rope, tag, early=None, mid=None):
    n = lambda s: f"{s}_{tag}"
    gr = {}
    u = sv["u"]

    de, dpre = _rowwise(lambda d, gt, ev: (d * gt, d * ev * gt * (1.0 - gt)), [dx3, sv["gt_ple"], sv["e_ple"]], [],
                        [(D_MODEL, BF16, "row"), (D_MODEL, BF16, "row")], name=n("ple_bwd"))
    gr["w_ple"] = _mm(sv["p_emb"], de, ta=True, name=n("d_w_ple"))
    gr["w_ple_gate"] = _mm(sv["h3"], dpre, ta=True, name=n("d_w_ple_gate"))
    dh3 = _mm(dpre, w["w_ple_gate"], tb=True, out_dtypes=(BF16,), name=n("d_h3"))
    dx2, dg = _rms_bwd(sv["x2"], _row(w["g_ple"]), dh3, dx3, name=n("rms_ple_bwd"))
    gr["g_ple"] = dg[0]

    gr["w_ff2"] = _mm(sv["f_ff"], dx2, ta=True, name=n("d_w_ff2"))
    da = _mm(dx2, w["w_ff2"], tb=True, epilogue=lambda acc, av: (acc * 2.0 * jnp.maximum(av, 0.0),),
             tiles=[sv["a_ff"]], out_dtypes=(BF16,), name=n("d_a_ff"))
    gr["w_ff1"] = _mm(sv["h2"], da, ta=True, name=n("d_w_ff1"))
    dh2 = _mm(da, w["w_ff1"], tb=True, out_dtypes=(BF16,), name=n("d_h2"))
    dx1, dg = _rms_bwd(sv["x1"], _row(w["g_mlp"]), dh2, dx2, name=n("rms_mlp_bwd"))
    gr["g_mlp"] = dg[0]
    if early is not None:
        dx1 = early(gr, dx1)

    gr["w_out"] = _mm(sv["merged"], dx1, ta=True, name=n("d_w_out"))
    dmerged = _mm(dx1, w["w_out"], tb=True, name=n("d_merged"))

    def merge_bwd(dm, gts, y0, y1, y2, y3):
        dys, dgs = [], []
        for b, yb in enumerate((y0, y1, y2, y3)):
            sg = _sigmoid(gts[:, b * D_MODEL:(b + 1) * D_MODEL])
            dys.append(dm * sg)
            dgs.append(dm * yb * sg * (1.0 - sg))
        return (*dys, jnp.concatenate(dgs, axis=1))

    *dybs, dgates = _rowwise(merge_bwd, [dmerged, u["gates"]] + sv["ybs"], [],
                             [(D_MODEL, BF16, "row")] * 4 + [(4 * D_MODEL, BF16, "row")], name=n("merge_bwd"))
    ys = [sv["y_a"], sv["y_b"], sv["y_c"], sv["y_d"]]
    dwb = [_mm(ys[b], dybs[b], ta=True, name=n(f"d_w_branch{b}")) for b in range(4)]
    gr["w_branch"] = jnp.stack([_unpad_blocks(dwb[0], 0, N_HEADS, HEAD), dwb[1],
                                _unpad_blocks(dwb[2], 0, N_HEADS, HEAD), dwb[3]])
    dy_a = _mm(dybs[0], w["br"][0], tb=True, out_dtypes=(BF16,), name=n("d_y_a"))
    dy_b = _mm(dybs[1], w["br"][1], tb=True, name=n("d_y_b"))
    dy_c = _mm(dybs[2], w["br"][2], tb=True, name=n("d_y_c"))
    dy_d = _mm(dybs[3], w["br"][3], tb=True, name=n("d_y_d"))
    du = {"gates": dgates}

    lru_par = (_row(w["lru_lambda"]), _row(w["lru_b_a"]), _row(w["lru_b_i"]))
    dpa, dpi, dxc_direct, du["lru_g"], dlam, dba, dbi = _lru_bwd(
        sv["pre"], sv["xc"], u["lru_g"], *lru_par, sv["h_lru"], dy_d, name=n("lru_bwd"))
    dpre_lru = jnp.concatenate([dpa, dpi], axis=1)
    d_bd = _mm(sv["xc"], dpre_lru, ta=True, name=n("d_lru_w"))
    gr["lru_w_a"] = _block_diag_t(d_bd[:, :MIX], N_HEADS)
    gr["lru_w_i"] = _block_diag_t(d_bd[:, MIX:], N_HEADS)
    gr["lru_lambda"], gr["lru_b_a"], gr["lru_b_i"] = dlam[0], dba[0], dbi[0]
    dxc = _mm(dpre_lru, w["lru_bd"], tb=True, epilogue=lambda acc, t: (acc + t,), tiles=[dxc_direct], name=n("d_xc"))
    du["lru_x"], gr["lru_conv_w"], dcb = _conv_bwd(u["lru_x"], w["lru_conv_w"], _row(w["lru_conv_b"]), dxc,
                                                  silu=False, name=n("lru_conv_bwd"))
    gr["lru_conv_b"] = dcb[0]

    def ssd_post_bwd(dyc, yv, zv, gv):
        sz = _silu(zv)
        dyz, dgain = _rms_bwd_math(yv * sz, gv, dyc, MIX)
        return dyz * sz, dyz * yv * _silu_grad(zv), dgain

    dy_ssd, du["z"], dgain = _rowwise(ssd_post_bwd, [dy_c, sv["y_ssd"], u["z"]], [w["ssd_norm_pad"]],
                                      [(N_HEADS * LANES, F32, "row"), (N_HEADS * LANES, BF16, "row"),
                                       (N_HEADS * LANES, F32, "acc")], name=n("ssd_post_bwd"))
    gr["ssd_norm"] = _unpad_blocks(dgain[0], 0, N_HEADS, HEAD)
    ssd_par = (_scal3(w["ssd_dt_bias"]), _scal3(w["ssd_a_log"]), _scal3(w["ssd_d"]))
    dxs, dbg, dcg, ddt, dbias, dalog, dd = _ssd_bwd(sv["xbc_c"], sv["dtcol"], sv["dtrow"], *ssd_par, sv["states"],
                                                    dy_ssd, name=n("ssd_bwd"))
    s = dxs.shape[0]
    dxbc_c = jnp.concatenate([dxs, dbg, dcg], axis=1)
    gr["ssd_dt_bias"], gr["ssd_a_log"], gr["ssd_d"] = dbias[:, 0, 0], dalog[:, 0, 0], dd[:, 0, 0]
    du["xbc"], dcw, dcb = _conv_bwd(u["xbc"], w["ssd_conv_w_pad"], w["ssd_conv_b_pad"], dxbc_c, silu=True,
                                    name=n("ssd_conv_bwd"))
    gr["ssd_conv_w"], gr["ssd_conv_b"] = _unpad_xbc_vec(dcw), _unpad_xbc_vec(dcb[0])
    du["dt"] = jnp.pad(ddt.T, ((0, 0), (0, LANES - N_HEADS)))

    dyb_pre, dscale = _rowwise(lambda d, yp, sc: (d * sc, _colsum(d * yp)), [dy_b, sv["yb_pre"]],
                               [_row(w["pool_scale"])], [(MIX, BF16, "row"), (MIX, F32, "acc")], name=n("pool_scale_bwd"))
    gr["pool_scale"] = dscale[0]
    gr["w_pool"] = _block_diag_t(_mm(sv["pool_d"], dyb_pre, ta=True, name=n("d_w_pool")), 4)
    dd_pool = _mm(dyb_pre, w["pool_bd"], tb=True, name=n("d_pool_d"))
    du["pool"] = _pool_bwd(dd_pool, name=n("pool_bwd"))

    delta = _att_delta(sv["y_a"], dy_a, name=n("att_delta"))
    dqc, dkc, dvc = _flash_bwd(sv["qc"], sv["kc"], sv["vc"], dy_a, sv["lse"], delta, name=n("flash_bwd"))
    dq_pad, du["kr"] = _att_prep_bwd(dqc, dkc, *rope, name=n("att_prep_bwd"))
    d_uq = _mm(sv["cqn"], dq_pad, ta=True, name=n("d_w_uq"))
    gr["w_uq"] = _unpad_blocks(d_uq, 1, N_HEADS, HEAD + QK_ROPE)
    dcqn = _mm(dq_pad, w["uq"], tb=True, out_dtypes=(BF16,), name=n("d_cqn"))
    du["cq"], dg = _rms_bwd(u["cq"], _row(w["q_norm"]), dcqn, name=n("rms_q_bwd"))
    gr["q_norm"] = dg[0]
    dkv2 = jnp.concatenate([dkc, dvc], axis=1).astype(BF16)
    d_ukv = _mm(sv["ckvn"], dkv2, ta=True, name=n("d_w_ukv"))
    wk = N_HEADS * LANES
    dk_real = _unpad_blocks(d_ukv[:, :wk], 1, N_HEADS, HEAD).reshape(KV_LORA, N_HEADS, HEAD)
    dv_real = _unpad_blocks(d_ukv[:, wk:], 1, N_HEADS, HEAD).reshape(KV_LORA, N_HEADS, HEAD)
    gr["w_ukv"] = jnp.concatenate([dk_real, dv_real], axis=2).reshape(KV_LORA, N_HEADS * 2 * HEAD)
    dckvn = _mm(dkv2, w["ukv"], tb=True, out_dtypes=(BF16,), name=n("d_ckvn"))
    du["ckv"], dg = _rms_bwd(u["ckv"], _row(w["kv_norm"]), dckvn, name=n("rms_kv_bwd"))
    gr["kv_norm"] = dg[0]

    if mid is not None:
        du["dt"] = du["dt"] + mid(gr)
    du_all = _w_in_fused({k: v.astype(BF16) for k, v in du.items()})
    dw_all = _mm(sv["h"], du_all, ta=True, name=n("d_w_in"))
    gr["w_in"] = _w_in_ungroup({k: dw_all[:, off:off + width] for k, off, width in IN_LAYOUT})
    dh = _mm(du_all, w["in_all"], tb=True, name=n("d_h"))
    dx, dg = _rms_bwd(sv["x"], _row(w["g_mix"]), dh, dx1, name=n("rms_mix_bwd"))
    gr["g_mix"] = dg[0]
    return dx, gr


def _pack_rows(n_elems):
    per = PACK_W * PACK_ROWS
    return -(-n_elems // per) * PACK_ROWS


def _pack_flat(parts, dtype):
    flat = jnp.concatenate([p.reshape(-1).astype(dtype) for p in parts])
    rows = _pack_rows(flat.shape[0])
    return jnp.pad(flat, (0, rows * PACK_W - flat.shape[0])).reshape(rows, PACK_W)


def _unpack_flat(buf, shapes):
    lead = buf.shape[:-2]
    flat = buf.reshape(lead + (-1,))
    out, off = [], 0
    for shp in shapes:
        size = int(np.prod(shp))
        out.append(flat[..., off:off + size].reshape(lead + tuple(shp)))
        off += size
    return out


def _merge_shards(t, axis):
    return jnp.concatenate([t[i] for i in range(4)], axis=axis)


def _split_shards(t, axis):
    return jnp.stack(jnp.split(t, 4, axis=axis))


def _rope_tables(positions):
    inv = 1.0 / (ROPE_THETA ** (jnp.arange(0, QK_ROPE, 2, dtype=F32) / QK_ROPE))
    ang = positions.astype(F32)[:, None] * inv
    cos, sin = jnp.cos(ang), jnp.sin(ang)
    s = ang.shape[0]
    half = QK_ROPE // 2
    z = lambda n_: jnp.zeros((s, n_), F32)
    cos_t = jnp.concatenate([jnp.ones((s, HEAD), F32), cos, cos, jnp.ones((s, LANES - HEAD - QK_ROPE), F32)], axis=1)
    sin_p = jnp.concatenate([z(HEAD + half), sin, z(LANES - HEAD - QK_ROPE)], axis=1)
    sin_m = jnp.concatenate([z(HEAD), -sin, z(half + LANES - HEAD - QK_ROPE)], axis=1)
    return cos_t, sin_p, sin_m


def _loss_head(x, g, target, *, name):
    d = x.shape[1]

    def fn(xv, tv, gv):
        xh, r = _rms_parts(xv, d)
        y = xh * gv
        err = y - tv
        dy = err * (1.0 / d)
        dxh = dy * gv
        dx = r * (dxh - xh * (jnp.sum(dxh * xh, axis=-1, keepdims=True) * (1.0 / d)))
        return dx, _colsum(dy * xh), _colsum(err * err) * (0.5 / d)

    return _rowwise(fn, [x, target], [g], [(d, F32, "row"), (d, F32, "acc"), (d, F32, "acc")], name=name)


MATS = tuple((nm, ax) for nm, ax in BIG if nm not in CONV_SHARDED)
EARLY_L0 = ("w_ple", "w_ple_gate", "w_ff2", "w_ff1")


def _grad_view(g, ax_layer):
    if ax_layer == 0:
        return g.reshape(4, g.shape[0] // 4, g.shape[1])
    return g.reshape(1, -1, g.shape[-1])


def _reduce_start(grads_l, mats, c_idx, tag):
    views = [_grad_view(grads_l[nm], ax - 1) for nm, ax in mats]
    got = _send_half(views, name="send_half_" + tag)
    parts = []
    for (nm, ax), v, gt in zip(mats, views, got):
        both = _chip_sum_half(v, gt, c_idx, name=f"chip_sum_{nm}_{tag}")
        parts.append(both if ax == 1 else _split_shards(both[0], 1))
    return _push_start(parts, scatter=True, name="push_grads_" + tag)


def _reduce_finish(state, after, k_chip, tag):
    send_sems, recv_sems, parts, lands, _ = state
    parts, landed = _push_wait(send_sems, recv_sems, parts, lands, after, name="wait_grads_" + tag)
    mine = [lax.dynamic_update_index_in_dim(t, lax.dynamic_index_in_dim(p, k_chip, 0, keepdims=False), k_chip, 0)
            for t, p in zip(landed, parts)]
    return list(zip(mine, _swap_with_sibling(mine, name="swap_halves_" + tag)))


def _step(args):
    x = args["x"][0]
    c_idx = lax.axis_index("c")
    k_chip = 2 * lax.axis_index("x") + lax.axis_index("y")

    mats = MATS
    mine = [[args[nm][l].astype(BF16) for nm, _ in mats] for l in range(2)]
    gathered0 = _gather_halves(mine[0])
    convs = [(nm, ax) for nm, ax in BIG if nm in CONV_SHARDED]
    conv_all = _gather_all(_pack_flat([args[nm] for nm, _ in convs], F32), name="gather_conv_taps")[0::2]
    mine1, gathered0, conv_all = lax.optimization_barrier((mine[1], gathered0, conv_all))
    gathered0 = [lax.dynamic_update_index_in_dim(t, own, k_chip, 0) for t, own in zip(gathered0, mine[0])]
    push1 = _push_start(mine1, scatter=False, name="push_weights_l1")
    full_conv = {nm: _merge_shards(t, ax)
                 for (nm, ax), t in zip(convs, _unpack_flat(conv_all, [args[nm].shape for nm, _ in convs]))}
    rope = _rope_tables(args["positions"][0])

    def layer_weights(l, gathered):
        p = {nm: _merge_shards(t, ax - 1) for (nm, ax), t in zip(mats, gathered)}
        p.update({nm: full_conv[nm][l] for nm in CONV_SHARDED})
        p.update({nm: args[nm][l] for nm in SMALL if nm != "g_final"})
        return _layer_weights(p)

    layers = [layer_weights(0, gathered0), None]
    layers[0]["g_mix"] = layers[0]["g_mix"] + push1[4][0, 0]
    x, sv0 = _layer_fwd(x, args["p"][0, 0], layers[0], rope, "l0")
    own1, landed1 = _push_wait(push1[0], push1[1], push1[2], push1[3], x, name="wait_weights_l1")
    layers[1] = layer_weights(1, [lax.dynamic_update_index_in_dim(t, own, k_chip, 0) for t, own in zip(landed1, own1)])
    x, sv1 = _layer_fwd(x, args["p"][1, 0], layers[1], rope, "l1")
    saved = [sv0, sv1]

    dx, dg_final, loss_part = _loss_head(x, _row(args["g_final"]), args["loss_target"][0], name="loss_head")
    loss = lax.psum(jnp.sum(loss_part), ("x", "y", "c"))

    early_mats = tuple(mt for mt in MATS if mt[0] in EARLY_L0)
    rest_mats = tuple(mt for mt in MATS if mt[0] == "w_in")
    mid_mats = tuple(mt for mt in MATS if mt not in early_mats + rest_mats)
    grads, reduce0_early, reduce0_mid = [None, None], [], []

    def early0(gr, dx1):
        reduce0_early.append(_reduce_start(gr, early_mats, c_idx, "l0e"))
        return dx1 + reduce0_early[0][4][0, 0]

    def mid0(gr):
        reduce0_mid.append(_reduce_start(gr, mid_mats, c_idx, "l0m"))
        return reduce0_mid[0][4][0, 0]

    dx, grads[1] = _layer_bwd(dx, saved[1], layers[1], rope, "l1")
    reduce1 = _reduce_start(grads[1], MATS, c_idx, "l1")
    dx, grads[0] = _layer_bwd(dx + reduce1[4][0, 0], saved[0], layers[0], rope, "l0", early=early0, mid=mid0)
    g_all = {nm: jnp.stack([grads[0][nm], grads[1][nm]]) for nm in SMALL + CONV_SHARDED if nm != "g_final"}
    g_all["g_final"] = dg_final[0]
    all_names = SMALL + CONV_SHARDED
    all_shapes = [g_all[nm].shape for nm in all_names]
    packed = _pack_flat([g_all[nm] for nm in all_names], F32)
    sibling = _swap_with_sibling([packed], name="swap_small_grads")[0]
    pair = jnp.where(c_idx == 0, jnp.stack([packed, sibling]), jnp.stack([sibling, packed]))
    small_all = _gather_same_core(_sum_slots(pair, name="sum_cores"), name="gather_small_grads")
    g0_mats, small_all = lax.optimization_barrier(({nm: grads[0][nm] for nm, _ in rest_mats}, small_all))
    reduce0 = _reduce_start(g0_mats, rest_mats, c_idx, "l0")
    small_sum = _sum_slots(small_all, name="sum_chips")
    g_red = dict(zip(all_names, _unpack_flat(small_sum, all_shapes)))
    for nm, ax in BIG:
        if nm in CONV_SHARDED:
            width = args[nm].shape[ax]
            g_red[nm] = lax.dynamic_slice_in_dim(g_red[nm], k_chip * width, width, axis=ax)
    small_shapes = [args[nm].shape for nm in SMALL]
    pack_small = lambda src: _pack_flat([src(nm) for nm in SMALL], F32)
    upd_small = _adamw(pack_small(lambda nm: args[nm]), pack_small(lambda nm: g_red[nm]),
                       pack_small(lambda nm: args["m_" + nm]), pack_small(lambda nm: args["v_" + nm]), name="adamw_small")
    upd = {nm: trip for nm, trip in zip(SMALL, zip(*[_unpack_flat(t, small_shapes) for t in upd_small]))}
    for nm in CONV_SHARDED:
        upd[nm] = _adamw(args[nm], g_red[nm], args["m_" + nm], args["v_" + nm], name="adamw_" + nm)

    names = lambda mats_: [nm for nm, _ in mats_]
    slots0 = dict(zip(names(early_mats), _reduce_finish(reduce0_early[0], dx, k_chip, "l0e")))
    slots0.update(zip(names(mid_mats), _reduce_finish(reduce0_mid[0], dx, k_chip, "l0m")))
    slots0.update(zip(names(rest_mats), _reduce_finish(reduce0, upd_small[0], k_chip, "l0")))
    slots1 = dict(zip(names(MATS), _reduce_finish(reduce1, dx, k_chip, "l1")))
    for nm, _ in MATS:
        g_red[nm], *upd[nm] = _adamw_slots(args[nm], [slots0[nm], slots1[nm]], args["m_" + nm], args["v_" + nm],
                                           c_idx, name="adamw_" + nm)

    outs = [loss, dx[None]]
    outs += [g_red[nm] for nm in WEIGHTS]
    for i in range(3):
        outs += [upd[nm][i] for nm in WEIGHTS]
    return tuple(outs)


_ARG_NAMES = ("x", "p", "positions") + WEIGHTS + ("loss_target",) + tuple("m_" + nm for nm in WEIGHTS) \
    + tuple("v_" + nm for nm in WEIGHTS)


def kernel(*arrays):
    assert len(arrays) == len(_ARG_NAMES), len(arrays)
    return _step(dict(zip(_ARG_NAMES, arrays)))
```
